```python
import jax
import jax.numpy as jnp
from jax import lax
import numpy as np

D_MODEL = 1024
BATCH = 4
SEQ = 4096
DEPTH = 2
DEC_BATCH = 32
DEC_SEQ = 1
PAST_LEN = 8192
PAGE_SIZE = 128

N_A = (DEPTH + 1) // 2
N_B = DEPTH - N_A
A_HEADS = 4
A_INNER = D_MODEL
A_HDIM = A_INNER // A_HEADS
CONV_W = 4
A_CHUNK = 128
B_GROUPS = ((128, 1), (512, 4), (2048, 16))
N_GROUPS = len(B_GROUPS)
B_HEADS = 4
B_HDIM = D_MODEL // 8
B_WIDTH = B_HEADS * B_HDIM
B_BLOCK = 128
N_MEM = 256
M_HEADS = 4
M_HDIM = D_MODEL // 8
M_WIDTH = M_HEADS * M_HDIM
ROPE_THETA = 10000.0
EPS = 1e-6
A_IN = 3 * A_INNER + 2 * M_WIDTH
B_IN = N_GROUPS * B_WIDTH + B_WIDTH + 2 * M_WIDTH
KV_B = N_GROUPS * 2 * B_WIDTH

kernel_name = 'yoco_mlstm_dilated_attn_step'


def _rmsnorm(x, g):
    xf = x.astype(jnp.float32)
    y = xf * lax.rsqrt(jnp.mean(xf * xf, axis=-1, keepdims=True) + EPS)
    return (y * g.astype(jnp.float32)).astype(x.dtype)


def _rope(x, pos):
    dh = x.shape[-1]
    half = dh // 2
    inv = ROPE_THETA ** (-jnp.arange(half, dtype=jnp.float32) / half)
    ang = pos[:, None] * inv[None, :]
    shape = (pos.shape[0],) + (1,) * (x.ndim - 3) + (half,)
    cos = jnp.cos(ang).reshape(shape)
    sin = jnp.sin(ang).reshape(shape)
    xf = x.astype(jnp.float32)
    x1, x2 = xf[..., :half], xf[..., half:]
    return jnp.concatenate([x1 * cos - x2 * sin, x1 * sin + x2 * cos], axis=-1).astype(x.dtype)


def _mem_attention(q, mk, mv):
    s = jnp.einsum('bthd,bnhd->bhtn', q, mk).astype(jnp.float32) * (q.shape[-1] ** -0.5)
    p = jax.nn.softmax(s, axis=-1).astype(mv.dtype)
    return jnp.einsum('bhtn,bnhd->bthd', p, mv)


def _masked_softmax_stats(s, mask):
    s = jnp.where(mask, s, -jnp.inf)
    mx = jnp.max(s, axis=-1, keepdims=True)
    p = jnp.exp(s - mx)
    l = jnp.sum(p, axis=-1, keepdims=True)
    return p / l, (mx + jnp.log(l))[..., 0]


def _dilated_prompt(q, k, v, d, nk):
    B, S, H, dh = q.shape
    ls = S // d
    blk = min(B_BLOCK, ls)
    nblk = -(-ls // blk)
    lp = nblk * blk
    nprev = -(-nk // blk)
    n = B * d

    def sub(a):
        a = a.reshape(B, ls, d, H, dh).transpose(0, 2, 1, 3, 4).reshape(n, ls, H, dh)
        return jnp.pad(a, ((0, 0), (0, lp - ls), (0, 0), (0, 0)))

    def band(a):
        ab = jnp.pad(a.reshape(n, nblk, blk, H, dh), ((0, 0), (nprev, 0), (0, 0), (0, 0), (0, 0)))
        return jnp.concatenate([ab[:, j:j + nblk] for j in range(nprev + 1)], axis=2)

    qb = sub(q).reshape(n, nblk, blk, H, dh)
    kw = band(sub(k))
    vw = band(sub(v))
    wk = (nprev + 1) * blk
    bi = jnp.arange(nblk)[:, None, None]
    qpos = bi * blk + jnp.arange(blk)[None, :, None]
    kpos = bi * blk - nprev * blk + jnp.arange(wk)[None, None, :]
    rel = qpos - kpos
    mask = (rel >= 0) & (rel <= nk) & (kpos >= 0)
    s = jnp.einsum('nbqhd,nbkhd->nbhqk', qb, kw).astype(jnp.float32) * (dh ** -0.5)
    pr, lse = _masked_softmax_stats(s, mask[None, :, None])
    o = jnp.einsum('nbhqk,nbkhd->nbqhd', pr.astype(v.dtype), vw)
    o = o.reshape(n, lp, H, dh)[:, :ls]
    lse = lse.transpose(0, 1, 3, 2).reshape(n, lp, H)[:, :ls]
    o = o.reshape(B, d, ls, H, dh).transpose(0, 2, 1, 3, 4).reshape(B, S, H, dh)
    lse = lse.reshape(B, d, ls, H).transpose(0, 2, 1, 3).reshape(B, S, H)
    return o, lse


def _dilated_decode(q, kbuf, vbuf, kn, vn, d, nk):
    wb = kbuf.shape[1]
    T = q.shape[1]
    kall = jnp.concatenate([kbuf.astype(kn.dtype), kn], axis=1)
    vall = jnp.concatenate([vbuf.astype(vn.dtype), vn], axis=1)
    idx = wb + jnp.arange(T)[:, None] - d * jnp.arange(nk + 1)[None, :]
    valid = idx >= 0
    idx = jnp.maximum(idx, 0)
    kg = kall[:, idx]
    vg = vall[:, idx]
    s = jnp.einsum('bthd,btmhd->bhtm', q, kg).astype(jnp.float32) * (q.shape[-1] ** -0.5)
    pr, lse = _masked_softmax_stats(s, valid[None, None])
    o = jnp.einsum('bhtm,btmhd->bthd', pr.astype(vg.dtype), vg)
    return o, lse.transpose(0, 2, 1)


def _mlstm_chunk_step(carry, inp):
    C, n, m = carry
    q, k, v, li, lf = inp
    L = q.shape[2]
    b = jnp.cumsum(lf, axis=-1)
    causal = jnp.tril(jnp.ones((L, L), dtype=bool))
    dmat = jnp.where(causal, b[..., :, None] - b[..., None, :] + li[..., None, :], -jnp.inf)
    inter = b + m[..., None]
    m_row = jnp.maximum(inter, jnp.max(dmat, axis=-1))
    sc = jnp.einsum('bhtd,bhsd->bhts', q, k) * jnp.exp(dmat - m_row[..., None])
    dec = jnp.exp(inter - m_row)
    num = jnp.einsum('bhts,bhsv->bhtv', sc, v) + dec[..., None] * jnp.einsum('bhvk,bhtk->bhtv', C, q)
    den = jnp.sum(sc, axis=-1) + dec * jnp.einsum('bhk,bhtk->bht', n, q)
    h = num / jnp.maximum(jnp.abs(den), jnp.exp(-m_row))[..., None]
    b_last = b[..., -1]
    g = b_last[..., None] - b + li
    m_new = jnp.maximum(b_last + m, jnp.max(g, axis=-1))
    ws = jnp.exp(g - m_new[..., None])
    dc = jnp.exp(b_last + m - m_new)
    C_new = dc[..., None, None] * C + jnp.einsum('bhs,bhsv,bhsk->bhvk', ws, v, k)
    n_new = dc[..., None] * n + jnp.einsum('bhs,bhsk->bhk', ws, k)
    return (C_new, n_new, m_new), h


def _mlstm_branch(u, o_pre, conv_state, C0, n0, m0, conv_w, conv_b, w_q, w_k, w_v, w_if, b_if, g_hn, skip, chunk):
    B, T, _ = u.shape
    xpad = jnp.concatenate([conv_state.astype(u.dtype), u], axis=1)
    xc = conv_b.astype(u.dtype) + xpad[:, 0:T] * conv_w[0]
    for j in range(1, CONV_W):
        xc = xc + xpad[:, j:j + T] * conv_w[j]
    xc = jax.nn.silu(xc)
    new_conv = xpad[:, T:]
    xh = xc.reshape(B, T, A_HEADS, A_HDIM)
    uh = u.reshape(B, T, A_HEADS, A_HDIM)
    q = jnp.einsum('bthd,hde->bthe', xh, w_q)
    k = jnp.einsum('bthd,hde->bthe', xh, w_k) * (A_HDIM ** -0.5)
    v = jnp.einsum('bthd,hde->bthe', uh, w_v)
    gates = (jnp.concatenate([q, k, v], axis=-1).reshape(B, T, 3 * A_INNER) @ w_if + b_if).astype(jnp.float32)
    li = gates[..., :A_HEADS]
    lf = jax.nn.log_sigmoid(gates[..., A_HEADS:])
    nc = T // chunk

    def to_chunks(a):
        return a.astype(jnp.float32).reshape(B, nc, chunk, A_HEADS, A_HDIM).transpose(1, 0, 3, 2, 4)

    def gate_chunks(a):
        return a.reshape(B, nc, chunk, A_HEADS).transpose(1, 0, 3, 2)

    carry0 = (C0.astype(jnp.float32), n0.astype(jnp.float32), m0.astype(jnp.float32))
    (C1, n1, m1), hs = lax.scan(_mlstm_chunk_step, carry0,
                                (to_chunks(q), to_chunks(k), to_chunks(v), gate_chunks(li), gate_chunks(lf)))
    hs = hs.transpose(1, 0, 3, 2, 4).reshape(B, T, A_HEADS, A_HDIM)
    h = jax.nn.sigmoid(o_pre.astype(jnp.float32)).reshape(B, T, A_HEADS, A_HDIM) * hs
    mu = jnp.mean(h, axis=-1, keepdims=True)
    var = jnp.mean(jnp.square(h - mu), axis=-1, keepdims=True)
    hn = ((h - mu) * lax.rsqrt(var + EPS)).reshape(B, T, A_INNER) * g_hn.astype(jnp.float32)
    y = (hn + skip.astype(jnp.float32) * xc.astype(jnp.float32)).astype(u.dtype)
    return y, new_conv, C1, n1, m1


def _run_stack(x, pos, mem_kv, a_state, wins, chunk, p):
    B, T, _ = x.shape
    conv_l, C_l, n_l, m_l = [], [], [], []
    k_all = None
    v_all = None
    for l in range(DEPTH):
        h = _rmsnorm(x, p['g_pre'][l])
        if l < N_A:
            z = h @ p['w_in_a'][l]
            u, o_pre, zg, qm, zm = jnp.split(
                z, [A_INNER, 2 * A_INNER, 3 * A_INNER, 3 * A_INNER + M_WIDTH], axis=-1)
            ya, cs, C1, n1, m1 = _mlstm_branch(
                u, o_pre, a_state[0][l], a_state[1][l], a_state[2][l], a_state[3][l],
                p['conv_w_a'][l], p['conv_b_a'][l], p['w_q_a'][l], p['w_k_a'][l], p['w_v_a'][l],
                p['w_if_a'][l], p['b_if_a'][l], p['g_hn_a'][l], p['skip_a'][l], chunk)
            conv_l.append(cs)
            C_l.append(C1)
            n_l.append(n1)
            m_l.append(m1)
            y_mix = ya * jax.nn.silu(zg)
            w_out = p['w_out_a'][l]
        else:
            if k_all is None:
                kv = (_rmsnorm(x, p['g_kv']) @ p['w_kv_b']).reshape(B, T, N_GROUPS, 2, B_HEADS, B_HDIM)
                k_all = _rope(kv[:, :, :, 0], pos)
                v_all = kv[:, :, :, 1]
            ib = l - N_A
            z = h @ p['w_in_b'][ib]
            qd, zg, qm, zm = jnp.split(
                z, [N_GROUPS * B_WIDTH, N_GROUPS * B_WIDTH + B_WIDTH,
                    N_GROUPS * B_WIDTH + B_WIDTH + M_WIDTH], axis=-1)
            qd = _rope(qd.reshape(B, T, N_GROUPS, B_HEADS, B_HDIM), pos)
            outs, lses = [], []
            for g, (w, d) in enumerate(B_GROUPS):
                if wins is None:
                    o, lse = _dilated_prompt(qd[:, :, g], k_all[:, :, g], v_all[:, :, g], d, w // d)
                else:
                    o, lse = _dilated_decode(qd[:, :, g], wins[g][0], wins[g][1],
                                             k_all[:, :, g], v_all[:, :, g], d, w // d)
                outs.append(o)
                lses.append(lse)
            wts = jax.nn.softmax(jnp.stack(lses, axis=0), axis=0)
            ydil = jnp.einsum('gbth,gbthd->bthd', wts, jnp.stack(outs, axis=0).astype(jnp.float32))
            y_mix = ydil.reshape(B, T, B_WIDTH).astype(x.dtype) * jax.nn.silu(zg)
            w_out = p['w_out_b'][ib]
        mk = mem_kv[l, :, :, 0].astype(x.dtype)
        mv = mem_kv[l, :, :, 1].astype(x.dtype)
        ym = _mem_attention(qm.reshape(B, T, M_HEADS, M_HDIM), mk, mv).reshape(B, T, M_WIDTH) * jax.nn.silu(zm)
        out = jnp.concatenate([y_mix, ym], axis=-1) @ w_out
        x = x + _rmsnorm(out, p['g_post'][l])
    return x, (jnp.stack(conv_l), jnp.stack(C_l), jnp.stack(n_l), jnp.stack(m_l)), (k_all, v_all)


def _window_rows(k_all, v_all, g, n_rows):
    T = k_all.shape[1]
    return jnp.stack([k_all[:, T - n_rows:, g], v_all[:, T - n_rows:, g]], axis=2)


def setup_inputs(seed: int = 0) -> dict:
    key = jax.random.key(seed)
    ks = iter(list(jax.random.split(key, 40)))

    def nrm(shape, scale):
        return jax.random.normal(next(ks), shape, jnp.float32) * scale

    b_if_a = jnp.concatenate(
        [nrm((N_A, A_HEADS), 0.1),
         jnp.linspace(3.0, 6.0, A_HEADS, dtype=jnp.float32)[None, :] + nrm((N_A, A_HEADS), 0.01)], axis=-1)
    return {
        'x_prompt': nrm((BATCH, SEQ, D_MODEL), 1.0),
        'x_sample': nrm((DEC_BATCH, DEC_SEQ, D_MODEL), 1.0),
        'mem_prompt': nrm((BATCH, N_MEM, D_MODEL), 1.0),
        'state_conv': nrm((N_A, DEC_BATCH, CONV_W - 1, A_INNER), 1.0),
        'state_C': nrm((N_A, DEC_BATCH, A_HEADS, A_HDIM, A_HDIM), 0.1),
        'state_n': nrm((N_A, DEC_BATCH, A_HEADS, A_HDIM), 0.5),
        'state_m': nrm((N_A, DEC_BATCH, A_HEADS), 0.5),
        'cache_win0': nrm((DEC_BATCH, min(B_GROUPS[0][0], PAST_LEN), 2, B_HEADS, B_HDIM), 1.0),
        'cache_win1': nrm((DEC_BATCH, min(B_GROUPS[1][0], PAST_LEN), 2, B_HEADS, B_HDIM), 1.0),
        'cache_win2': nrm((DEC_BATCH, min(B_GROUPS[2][0], PAST_LEN), 2, B_HEADS, B_HDIM), 1.0),
        'cache_mem_kv': nrm((DEPTH, DEC_BATCH, N_MEM, 2, M_HEADS, M_HDIM), 1.0),
        'g_pre': 1.0 + nrm((DEPTH, D_MODEL), 0.01),
        'g_post': 1.0 + nrm((DEPTH, D_MODEL), 0.01),
        'w_in_a': nrm((N_A, D_MODEL, A_IN), D_MODEL ** -0.5),
        'conv_w_a': nrm((N_A, CONV_W, A_INNER), CONV_W ** -0.5),
        'conv_b_a': nrm((N_A, A_INNER), 0.01),
        'w_q_a': nrm((N_A, A_HEADS, A_HDIM, A_HDIM), A_HDIM ** -0.5),
        'w_k_a': nrm((N_A, A_HEADS, A_HDIM, A_HDIM), A_HDIM ** -0.5),
        'w_v_a': nrm((N_A, A_HEADS, A_HDIM, A_HDIM), A_HDIM ** -0.5),
        'w_if_a': nrm((N_A, 3 * A_INNER, 2 * A_HEADS), (3 * A_INNER) ** -0.5),
        'b_if_a': b_if_a,
        'g_hn_a': 1.0 + nrm((N_A, A_INNER), 0.01),
        'skip_a': 1.0 + nrm((N_A, A_INNER), 0.01),
        'w_out_a': nrm((N_A, A_INNER + M_WIDTH, D_MODEL), (A_INNER + M_WIDTH) ** -0.5),
        'g_kv': 1.0 + nrm((D_MODEL,), 0.01),
        'w_kv_b': nrm((D_MODEL, KV_B), D_MODEL ** -0.5),
        'w_in_b': nrm((N_B, D_MODEL, B_IN), D_MODEL ** -0.5),
        'w_out_b': nrm((N_B, B_WIDTH + M_WIDTH, D_MODEL), (B_WIDTH + M_WIDTH) ** -0.5),
        'w_mkv': nrm((DEPTH, D_MODEL, 2 * M_WIDTH), D_MODEL ** -0.5),
    }


def reference(x_prompt, x_sample, mem_prompt, state_conv, state_C, state_n, state_m,
              cache_win0, cache_win1, cache_win2, cache_mem_kv,
              g_pre, g_post, w_in_a, conv_w_a, conv_b_a, w_q_a, w_k_a, w_v_a, w_if_a, b_if_a,
              g_hn_a, skip_a, w_out_a, g_kv, w_kv_b, w_in_b, w_out_b, w_mkv):
    p = {'g_pre': g_pre, 'g_post': g_post, 'w_in_a': w_in_a, 'conv_w_a': conv_w_a, 'conv_b_a': conv_b_a,
         'w_q_a': w_q_a, 'w_k_a': w_k_a, 'w_v_a': w_v_a, 'w_if_a': w_if_a, 'b_if_a': b_if_a,
         'g_hn_a': g_hn_a, 'skip_a': skip_a, 'w_out_a': w_out_a, 'g_kv': g_kv, 'w_kv_b': w_kv_b,
         'w_in_b': w_in_b, 'w_out_b': w_out_b}
    bp, tp = x_prompt.shape[0], x_prompt.shape[1]
    ts = x_sample.shape[1]

    memkv_p = jnp.einsum('bnd,lde->lbne', mem_prompt, w_mkv).reshape(DEPTH, bp, N_MEM, 2, M_HEADS, M_HDIM)
    pos_p = jnp.arange(tp, dtype=jnp.float32)
    a0 = (jnp.zeros((N_A, bp, CONV_W - 1, A_INNER), x_prompt.dtype),
          jnp.zeros((N_A, bp, A_HEADS, A_HDIM, A_HDIM), jnp.float32),
          jnp.zeros((N_A, bp, A_HEADS, A_HDIM), jnp.float32),
          jnp.zeros((N_A, bp, A_HEADS), jnp.float32))
    y_prompt, (conv_p, C_p, n_p, m_p), (k_p, v_p) = _run_stack(
        x_prompt, pos_p, memkv_p, a0, None, min(A_CHUNK, tp), p)
    win0_p = _window_rows(k_p, v_p, 0, min(B_GROUPS[0][0], tp))
    win1_p = _window_rows(k_p, v_p, 1, min(B_GROUPS[1][0], tp))
    win2_p = _window_rows(k_p, v_p, 2, min(B_GROUPS[2][0], tp))

    pos_s = PAST_LEN + jnp.arange(ts, dtype=jnp.float32)
    wins = ((cache_win0[:, :, 0], cache_win0[:, :, 1]),
            (cache_win1[:, :, 0], cache_win1[:, :, 1]),
            (cache_win2[:, :, 0], cache_win2[:, :, 1]))
    y_sample, (conv_s, C_s, n_s, m_s), (k_s, v_s) = _run_stack(
        x_sample, pos_s, cache_mem_kv, (state_conv, state_C, state_n, state_m), wins, ts, p)
    conv_s = conv_s.astype(state_conv.dtype)
    C_s = C_s.astype(state_C.dtype)
    n_s = n_s.astype(state_n.dtype)
    m_s = m_s.astype(state_m.dtype)
    win0_s = _window_rows(k_s, v_s, 0, ts)
    win1_s = _window_rows(k_s, v_s, 1, ts)
    win2_s = _window_rows(k_s, v_s, 2, ts)
    return (y_prompt, y_sample, conv_p, C_p, n_p, m_p, win0_p, win1_p, win2_p, memkv_p,
            conv_s, C_s, n_s, m_s, win0_s, win1_s, win2_s)
```

```python
import functools

import jax
import jax.numpy as jnp
from jax import lax
from jax.experimental import pallas as pl
from jax.experimental.pallas import tpu as pltpu

F32 = jnp.float32
BF16 = jnp.bfloat16

D_MODEL = 1024
A_HEADS = 4
A_HDIM = 256
A_INNER = 1024
CONV_W = 4
A_CHUNK = 128
B_GROUPS = ((128, 1), (512, 4), (2048, 16))
N_GROUPS = 3
B_HEADS = 4
B_HDIM = 128
B_WIDTH = 512
N_MEM = 256
M_HEADS = 4
M_HDIM = 128
M_WIDTH = 512
ROPE_THETA = 10000.0
EPS = 1e-6
PAST_LEN = 8192

LANES = 128
TOK_TILE = 512
ATT_BLK = 128
VMEM_LIMIT = 56 * 1024 * 1024

NT_DIMS = (((1,), (1,)), ((), ()))


def _dot(a, b):
    return jnp.dot(a, b, preferred_element_type=F32)


def _dot_nt(a, b):
    return lax.dot_general(a, b, NT_DIMS, preferred_element_type=F32)


def _sigmoid(x):
    return 1.0 / (1.0 + jnp.exp(-x))


def _silu(x):
    return x * _sigmoid(x)


def _log_sigmoid(x):
    return jnp.minimum(x, 0.0) - jnp.log(1.0 + jnp.exp(-jnp.abs(x)))


def _rms_scale(x):
    return x * lax.rsqrt(jnp.mean(x * x, axis=-1, keepdims=True) + EPS)


def _const_spec(shape):
    nd = len(shape)
    return pl.BlockSpec(shape, lambda *_: (0,) * nd, pipeline_mode=pl.Buffered(1))


def _mem_attention(qm, mk, mv):
    outs = []
    for h in range(M_HEADS):
        sl = slice(h * M_HDIM, (h + 1) * M_HDIM)
        s = _dot_nt(qm[:, sl], mk[:, sl]) * (M_HDIM ** -0.5)
        mx = jnp.max(s, axis=-1, keepdims=True)
        p = jnp.exp(s - mx)
        p = p / jnp.sum(p, axis=-1, keepdims=True)
        outs.append(_dot(p.astype(BF16), mv[:, sl]))
    return jnp.concatenate(outs, axis=-1)


def _memkv_kernel(m_ref, w_ref, o_ref, ob_ref):
    r = _dot(m_ref[...].astype(BF16), w_ref[0])
    o_ref[0] = r
    ob_ref[0] = r.astype(BF16)


def _memkv(mem2d, w_bf):
    nm = mem2d.shape[0]
    nl = w_bf.shape[0]
    tm = min(512, nm)
    return pl.pallas_call(
        _memkv_kernel,
        grid=(nl, nm // tm),
        in_specs=[pl.BlockSpec((tm, D_MODEL), lambda l, i: (i, 0)),
                  pl.BlockSpec((1, D_MODEL, 2 * M_WIDTH), lambda l, i: (l, 0, 0))],
        out_specs=[pl.BlockSpec((1, tm, 2 * M_WIDTH), lambda l, i: (l, i, 0))] * 2,
        out_shape=[jax.ShapeDtypeStruct((nl, nm, 2 * M_WIDTH), F32),
                   jax.ShapeDtypeStruct((nl, nm, 2 * M_WIDTH), BF16)],
        compiler_params=pltpu.CompilerParams(dimension_semantics=("arbitrary", "arbitrary")),
        name="memkv",
    )(mem2d, w_bf)


def _l0_kernel(x_ref, gpre_ref, win_ref, convw_ref, convb_ref, wq_ref, wk_ref, wv_ref,
               wif_ref, bif_ref, ghn_ref, skip_ref, mkv_ref, wout_ref, gpost_ref,
               x1_ref, conv_out, c_out, n_out, m_out,
               h_s, u_s, xc_s, opre_s, zg_s, qm_s, zm_s, qkv_s, gates_s, hs_s, ymix_s,
               c_s, n_s, m_s):
    tt = x_ref.shape[1]
    nsub = tt // A_CHUNK
    i = pl.program_id(1)
    nt = pl.num_programs(1)

    @pl.when(i == 0)
    def _():
        u_s[0:8, :] = jnp.zeros((8, A_INNER), F32)
        c_s[...] = jnp.zeros(c_s.shape, F32)
        n_s[...] = jnp.zeros(n_s.shape, F32)
        m_s[...] = jnp.zeros(m_s.shape, F32)

    gpre = gpre_ref[...]

    def norm_body(c, _):
        r = pl.ds(pl.multiple_of(c * A_CHUNK, A_CHUNK), A_CHUNK)
        h_s[r, :] = (_rms_scale(x_ref[0, r, :]) * gpre).astype(BF16)
        return 0
    lax.fori_loop(0, nsub, norm_body, 0)

    hb = h_s[...]
    u_s[8:8 + tt, :] = _dot(hb, win_ref[:, 0:A_INNER])
    opre_s[...] = _dot(hb, win_ref[:, A_INNER:2 * A_INNER])
    zg_s[...] = _dot(hb, win_ref[:, 2 * A_INNER:3 * A_INNER])
    qm_s[...] = _dot(hb, win_ref[:, 3 * A_INNER:3 * A_INNER + M_WIDTH]).astype(BF16)
    zm_s[...] = _dot(hb, win_ref[:, 3 * A_INNER + M_WIDTH:3 * A_INNER + 2 * M_WIDTH])

    cw = convw_ref[...]
    cb = convb_ref[...]

    for c in range(nsub):
        r0 = c * A_CHUNK
        xc = cb + u_s[r0 + 5:r0 + 5 + A_CHUNK, :] * cw[0:1, :]
        xc = xc + u_s[r0 + 6:r0 + 6 + A_CHUNK, :] * cw[1:2, :]
        xc = xc + u_s[r0 + 7:r0 + 7 + A_CHUNK, :] * cw[2:3, :]
        xc = xc + u_s[r0 + 8:r0 + 8 + A_CHUNK, :] * cw[3:4, :]
        xc_s[r0:r0 + A_CHUNK, :] = _silu(xc)

    for h in range(A_HEADS):
        sl = slice(h * A_HDIM, (h + 1) * A_HDIM)
        xh = xc_s[:, sl].astype(BF16)
        uh = u_s[8:8 + tt, sl].astype(BF16)
        base = h * 3 * A_HDIM
        qkv_s[:, base:base + A_HDIM] = _dot(xh, wq_ref[h]).astype(BF16)
        qkv_s[:, base + A_HDIM:base + 2 * A_HDIM] = (_dot(xh, wk_ref[h]) * (A_HDIM ** -0.5)).astype(BF16)
        qkv_s[:, base + 2 * A_HDIM:base + 3 * A_HDIM] = _dot(uh, wv_ref[h]).astype(BF16)
    gates_s[...] = _dot(qkv_s[...], wif_ref[...]) + bif_ref[...]

    row = lax.broadcasted_iota(jnp.int32, (A_CHUNK, A_CHUNK), 0)
    col = lax.broadcasted_iota(jnp.int32, (A_CHUNK, A_CHUNK), 1)
    causal = col <= row
    tri = jnp.where(causal, 1.0, 0.0).astype(BF16)

    def chunk_body(c, _):
        r0 = pl.multiple_of(c * A_CHUNK, A_CHUNK)
        rs = pl.ds(r0, A_CHUNK)
        g = gates_s[rs, :]
        ls = _log_sigmoid(g)
        t0 = ls.astype(BF16)
        e1 = ls - t0.astype(F32)
        t1 = e1.astype(BF16)
        t2 = (e1 - t1.astype(F32)).astype(BF16)
        bc = _dot(tri, t0) + _dot(tri, t1) + _dot(tri, t2)
        gt = g.T
        bt = bc.T
        for h in range(A_HEADS):
            b_col = bc[:, 4 + h:5 + h]
            b_row = bt[4 + h:5 + h, :]
            li_row = gt[h:h + 1, :]
            li_col = g[:, h:h + 1]
            m_old = m_s[h:h + 1, 0:1]
            dm = jnp.where(causal, b_col - b_row + li_row, -jnp.inf)
            inter = b_col + m_old
            m_row = jnp.maximum(inter, jnp.max(dm, axis=-1, keepdims=True))
            base = h * 3 * A_HDIM
            qh = qkv_s[rs, base:base + A_HDIM]
            kh = qkv_s[rs, base + A_HDIM:base + 2 * A_HDIM]
            vh = qkv_s[rs, base + 2 * A_HDIM:base + 3 * A_HDIM]
            sc = _dot_nt(qh, kh) * jnp.exp(dm - m_row)
            dec = jnp.exp(inter - m_row)
            c_old = c_s[h]
            n_old = n_s[h:h + 1, :]
            num = _dot(sc.astype(BF16), vh) + dec * _dot_nt(qh, c_old.astype(BF16))
            den = (jnp.sum(sc, axis=-1, keepdims=True)
                   + dec * jnp.sum(qh.astype(F32) * n_old, axis=-1, keepdims=True))
            hs_s[rs, h * A_HDIM:(h + 1) * A_HDIM] = num / jnp.maximum(jnp.abs(den), jnp.exp(-m_row))
            b_last = bc[A_CHUNK - 1:A_CHUNK, 4 + h:5 + h]
            g_row = b_last - b_row + li_row
            g_col = b_last - b_col + li_col
            m_new = jnp.maximum(b_last + m_old, jnp.max(g_row, axis=-1, keepdims=True))
            ws_col = jnp.exp(g_col - m_new)
            dc = jnp.exp(b_last + m_old - m_new)
            wv = (ws_col * vh.astype(F32)).T.astype(BF16)
            c_s[h] = dc * c_old + _dot(wv, kh)
            n_s[h:h + 1, :] = dc * n_old + jnp.sum(ws_col * kh.astype(F32), axis=0, keepdims=True)
            m_s[h:h + 1, :] = jnp.broadcast_to(m_new, (1, LANES))
        return 0
    lax.fori_loop(0, nsub, chunk_body, 0)

    ghn = ghn_ref[...]
    skp = skip_ref[...]
    mk = mkv_ref[0, :, 0:M_WIDTH]
    mv = mkv_ref[0, :, M_WIDTH:2 * M_WIDTH]

    def post_body(c, _):
        r0 = pl.multiple_of(c * A_CHUNK, A_CHUNK)
        rs = pl.ds(r0, A_CHUNK)
        hh = _sigmoid(opre_s[rs, :]) * hs_s[rs, :]
        parts = []
        for h in range(A_HEADS):
            v = hh[:, h * A_HDIM:(h + 1) * A_HDIM]
            mu = jnp.mean(v, axis=-1, keepdims=True)
            var = jnp.mean(jnp.square(v - mu), axis=-1, keepdims=True)
            parts.append((v - mu) * lax.rsqrt(var + EPS))
        hn = jnp.concatenate(parts, axis=-1) * ghn
        y = hn + skp * xc_s[rs, :]
        ymix_s[rs, 0:A_INNER] = (y * _silu(zg_s[rs, :])).astype(BF16)
        ym = _mem_attention(qm_s[rs, :], mk, mv) * _silu(zm_s[rs, :])
        ymix_s[rs, A_INNER:A_INNER + M_WIDTH] = ym.astype(BF16)
        return 0
    lax.fori_loop(0, nsub, post_body, 0)

    out = _dot(ymix_s[...], wout_ref[...])
    x1_ref[0] = x_ref[0] + _rms_scale(out) * gpost_ref[...]

    u_s[0:8, :] = u_s[tt:tt + 8, :]

    @pl.when(i == nt - 1)
    def _():
        conv_out[0, 0] = u_s[tt + 5:tt + 8, :]
        c_out[0, 0] = c_s[...]
        n_out[0, 0] = n_s[0:A_HEADS, :]
        m_out[0] = m_s[...]


def _layer0_prompt(x, g_pre, w_in, conv_w, conv_b, wq, wk, wv, wif, bif, ghn, skip, mkv_bf, w_out, g_post):
    b, s, _ = x.shape
    tt = min(TOK_TILE, s)
    nt = s // tt
    a_in = w_in.shape[1]
    tile = lambda bb, i: (bb, i, 0)
    per_b = lambda bb, i: (bb, 0, 0)
    in_specs = [
        pl.BlockSpec((1, tt, D_MODEL), tile),
        _const_spec((1, D_MODEL)),
        _const_spec((D_MODEL, a_in)),
        _const_spec((CONV_W, A_INNER)),
        _const_spec((1, A_INNER)),
        _const_spec((A_HEADS, A_HDIM, A_HDIM)),
        _const_spec((A_HEADS, A_HDIM, A_HDIM)),
        _const_spec((A_HEADS, A_HDIM, A_HDIM)),
        _const_spec((3 * A_INNER, LANES)),
        _const_spec((1, LANES)),
        _const_spec((1, A_INNER)),
        _const_spec((1, A_INNER)),
        pl.BlockSpec((1, N_MEM, 2 * M_WIDTH), per_b),
        _const_spec((A_INNER + M_WIDTH, D_MODEL)),
        _const_spec((1, D_MODEL)),
    ]
    out_specs = [
        pl.BlockSpec((1, tt, D_MODEL), tile),
        pl.BlockSpec((1, 1, CONV_W - 1, A_INNER), lambda bb, i: (0, bb, 0, 0)),
        pl.BlockSpec((1, 1, A_HEADS, A_HDIM, A_HDIM), lambda bb, i: (0, bb, 0, 0, 0)),
        pl.BlockSpec((1, 1, A_HEADS, A_HDIM), lambda bb, i: (0, bb, 0, 0)),
        pl.BlockSpec((1, 8, LANES), per_b),
    ]
    out_shape = [
        jax.ShapeDtypeStruct((b, s, D_MODEL), F32),
        jax.ShapeDtypeStruct((1, b, CONV_W - 1, A_INNER), F32),
        jax.ShapeDtypeStruct((1, b, A_HEADS, A_HDIM, A_HDIM), F32),
        jax.ShapeDtypeStruct((1, b, A_HEADS, A_HDIM), F32),
        jax.ShapeDtypeStruct((b, 8, LANES), F32),
    ]
    scratch = [
        pltpu.VMEM((tt, D_MODEL), BF16),
        pltpu.VMEM((tt + 8, A_INNER), F32),
        pltpu.VMEM((tt, A_INNER), F32),
        pltpu.VMEM((tt, A_INNER), F32),
        pltpu.VMEM((tt, A_INNER), F32),
        pltpu.VMEM((tt, M_WIDTH), BF16),
        pltpu.VMEM((tt, M_WIDTH), F32),
        pltpu.VMEM((tt, 3 * A_INNER), BF16),
        pltpu.VMEM((tt, LANES), F32),
        pltpu.VMEM((tt, A_INNER), F32),
        pltpu.VMEM((tt, A_INNER + M_WIDTH), BF16),
        pltpu.VMEM((A_HEADS, A_HDIM, A_HDIM), F32),
        pltpu.VMEM((8, A_HDIM), F32),
        pltpu.VMEM((8, LANES), F32),
    ]
    return pl.pallas_call(
        _l0_kernel,
        grid=(b, nt),
        in_specs=in_specs,
        out_specs=out_specs,
        out_shape=out_shape,
        scratch_shapes=scratch,
        compiler_params=pltpu.CompilerParams(
            dimension_semantics=("arbitrary", "arbitrary"), vmem_limit_bytes=VMEM_LIMIT),
        name="layer0_prompt",
    )(x, g_pre, w_in, conv_w, conv_b, wq, wk, wv, wif, bif, ghn, skip, mkv_bf, w_out, g_post)


def _perm_matrix(n, d, transposed=False):
    a = lax.broadcasted_iota(jnp.int32, (n, n), 1 if transposed else 0)
    c = lax.broadcasted_iota(jnp.int32, (n, n), 0 if transposed else 1)
    per = n // d
    src = (a & (per - 1)) * d + lax.shift_right_logical(a, per.bit_length() - 1)
    return jnp.where(c == src, 1.0, 0.0).astype(BF16)


def _rope_cols(x, cos, sin_signed):
    outs = []
    for cblk in range(x.shape[1] // B_HDIM):
        xb = x[:, cblk * B_HDIM:(cblk + 1) * B_HDIM]
        outs.append(xb * cos + pltpu.roll(xb, B_HDIM // 2, 1) * sin_signed)
    return jnp.concatenate(outs, axis=-1)


def _l1a_kernel(x_ref, gkv_ref, gpre_ref, wkv_ref, win_ref, cos_ref, sin_ref,
                q0_ref, q1_ref, q2_ref, k0_ref, k1_ref, k2_ref, v0_ref, v1_ref, v2_ref,
                zg_ref, qm_ref, zm_ref, w0_ref, w1_ref, w2_ref):
    tt = x_ref.shape[1]
    x = x_ref[0]
    xn = _rms_scale(x)
    hk = (xn * gkv_ref[...]).astype(BF16)
    hq = (xn * gpre_ref[...]).astype(BF16)
    cos = cos_ref[...]
    sin = sin_ref[...]
    q_refs = (q0_ref, q1_ref, q2_ref)
    k_refs = (k0_ref, k1_ref, k2_ref)
    v_refs = (v0_ref, v1_ref, v2_ref)
    w_refs = (w0_ref, w1_ref, w2_ref)
    qoff = N_GROUPS * B_WIDTH
    zg_ref[0] = _dot(hq, win_ref[:, qoff:qoff + B_WIDTH]).astype(BF16)
    qm_ref[0] = _dot(hq, win_ref[:, qoff + B_WIDTH:qoff + B_WIDTH + M_WIDTH]).astype(BF16)
    zm_ref[0] = _dot(hq, win_ref[:, qoff + B_WIDTH + M_WIDTH:qoff + B_WIDTH + 2 * M_WIDTH]).astype(BF16)
    for g, (_, d) in enumerate(B_GROUPS):
        kf = _rope_cols(_dot(hk, wkv_ref[:, g * 2 * B_WIDTH:g * 2 * B_WIDTH + B_WIDTH]), cos, sin)
        vf = _dot(hk, wkv_ref[:, g * 2 * B_WIDTH + B_WIDTH:(g + 1) * 2 * B_WIDTH])
        qf = _rope_cols(_dot(hq, win_ref[:, g * B_WIDTH:(g + 1) * B_WIDTH]), cos, sin)
        wr = w_refs[g]
        wrows = wr.shape[1]
        wr[0, :, 0:B_WIDTH] = kf[tt - wrows:, :]
        wr[0, :, B_WIDTH:2 * B_WIDTH] = vf[tt - wrows:, :]
        for val, ref in ((qf.astype(BF16), q_refs[g]), (kf.astype(BF16), k_refs[g]), (vf.astype(BF16), v_refs[g])):
            if d == 1:
                ref[0, 0] = val
            else:
                sub = ATT_BLK if d == 4 else 2 * ATT_BLK
                per = sub // d
                pm = _perm_matrix(sub, d)
                for sblk in range(tt // sub):
                    y = _dot(pm, val[sblk * sub:(sblk + 1) * sub, :]).astype(BF16)
                    for r in range(d):
                        ref[0, r, sblk * per:(sblk + 1) * per, :] = y[r * per:(r + 1) * per, :]


def _layer1_proj_prompt(x1, g_kv, g_pre, wkv, win, cos_t, sin_t):
    b, s, _ = x1.shape
    tt = min(TOK_TILE, s)
    nt = s // tt
    tile = lambda bb, i: (bb, i, 0)
    in_specs = [
        pl.BlockSpec((1, tt, D_MODEL), tile),
        _const_spec((1, D_MODEL)),
        _const_spec((1, D_MODEL)),
        _const_spec(wkv.shape),
        _const_spec(win.shape),
        pl.BlockSpec((tt, B_HDIM), lambda bb, i: (i, 0)),
        pl.BlockSpec((tt, B_HDIM), lambda bb, i: (i, 0)),
    ]
    qkv_specs, qkv_shapes = [], []
    for _ in range(3):
        for (_, d) in B_GROUPS:
            qkv_specs.append(pl.BlockSpec((1, d, tt // d, B_WIDTH), lambda bb, i: (bb, 0, i, 0)))
            qkv_shapes.append(jax.ShapeDtypeStruct((b, d, s // d, B_WIDTH), BF16))
    gate_specs = [pl.BlockSpec((1, tt, B_WIDTH), tile)] * 3
    gate_shapes = [jax.ShapeDtypeStruct((b, s, B_WIDTH), BF16)] * 3
    win_specs, win_shapes = [], []
    for (w, _) in B_GROUPS:
        wr = min(w, s)
        rows = min(wr, tt)
        nblk = wr // rows
        win_specs.append(pl.BlockSpec(
            (1, rows, 2 * B_WIDTH),
            functools.partial(lambda bb, i, nb: (bb, jnp.maximum(i - (nt - nb), 0), 0), nb=nblk)))
        win_shapes.append(jax.ShapeDtypeStruct((b, wr, 2 * B_WIDTH), F32))
    return pl.pallas_call(
        _l1a_kernel,
        grid=(b, nt),
        in_specs=in_specs,
        out_specs=qkv_specs + gate_specs + win_specs,
        out_shape=qkv_shapes + gate_shapes + win_shapes,
        compiler_params=pltpu.CompilerParams(
            dimension_semantics=("arbitrary", "arbitrary"), vmem_limit_bytes=VMEM_LIMIT),
        name="layer1_proj_prompt",
    )(x1, g_kv, g_pre, wkv, win, cos_t, sin_t)


def _split3_lanes(cols):
    t = cols[0].shape[0]
    lane = lax.broadcasted_iota(jnp.int32, (t, LANES), 1)
    acc = jnp.zeros((t, LANES), F32)
    for h, cvec in enumerate(cols):
        t0 = cvec.astype(BF16).astype(F32)
        e1 = cvec - t0
        t1 = e1.astype(BF16).astype(F32)
        t2 = e1 - t1
        acc = acc + jnp.where(lane == h, t0, 0.0) + jnp.where(lane == 4 + h, t1, 0.0) \
            + jnp.where(lane == 8 + h, t2, 0.0)
    return acc.astype(BF16)


def _band_attn_kernel(q_ref, kc_ref, kp_ref, vc_ref, vp_ref, o_ref, lse_ref):
    tq = q_ref.shape[2]
    j = pl.program_id(2)
    row = lax.broadcasted_iota(jnp.int32, (ATT_BLK, ATT_BLK), 0)
    col = lax.broadcasted_iota(jnp.int32, (ATT_BLK, ATT_BLK), 1)
    cur_ok = col <= row
    prev_tri = col >= row
    first_pen = jnp.where(j > 0, 0.0, -jnp.inf)
    scale = B_HDIM ** -0.5
    for sb in range(tq // ATT_BLK):
        rs = slice(sb * ATT_BLK, (sb + 1) * ATT_BLK)
        lse_cols = []
        for h in range(B_HEADS):
            hs = slice(h * B_HDIM, (h + 1) * B_HDIM)
            qb = q_ref[0, 0, rs, hs]
            kc = kc_ref[0, 0, rs, hs]
            vc = vc_ref[0, 0, rs, hs]
            if sb == 0:
                kp = kp_ref[0, 0, :, hs]
                vp = vp_ref[0, 0, :, hs]
            else:
                ps = slice((sb - 1) * ATT_BLK, sb * ATT_BLK)
                kp = kc_ref[0, 0, ps, hs]
                vp = vc_ref[0, 0, ps, hs]
            s_c = jnp.where(cur_ok, _dot_nt(qb, kc) * scale, -jnp.inf)
            s_p = _dot_nt(qb, kp) * scale
            if sb == 0:
                s_p = s_p + first_pen
            s_p = jnp.where(prev_tri, s_p, -jnp.inf)
            mx = jnp.maximum(jnp.max(s_c, axis=-1, keepdims=True), jnp.max(s_p, axis=-1, keepdims=True))
            p_c = jnp.exp(s_c - mx)
            p_p = jnp.exp(s_p - mx)
            l = jnp.sum(p_c, axis=-1, keepdims=True) + jnp.sum(p_p, axis=-1, keepdims=True)
            o = _dot((p_c / l).astype(BF16), vc) + _dot((p_p / l).astype(BF16), vp)
            o_ref[0, 0, rs, hs] = o.astype(BF16)
            lse_cols.append(mx + jnp.log(l))
        lse_ref[0, 0, rs, :] = _split3_lanes(lse_cols)


def _band_attention(q, k, v):
    b, d, ls, _ = q.shape
    tq = min(TOK_TILE, ls)
    nj = ls // tq
    ratio = tq // ATT_BLK
    cur = lambda bb, r, j: (bb, r, j, 0)
    prev = lambda bb, r, j: (bb, r, jnp.maximum(j * ratio - 1, 0), 0)
    return pl.pallas_call(
        _band_attn_kernel,
        grid=(b, d, nj),
        in_specs=[pl.BlockSpec((1, 1, tq, B_WIDTH), cur),
                  pl.BlockSpec((1, 1, tq, B_WIDTH), cur),
                  pl.BlockSpec((1, 1, ATT_BLK, B_WIDTH), prev),
                  pl.BlockSpec((1, 1, tq, B_WIDTH), cur),
                  pl.BlockSpec((1, 1, ATT_BLK, B_WIDTH), prev)],
        out_specs=[pl.BlockSpec((1, 1, tq, B_WIDTH), cur),
                   pl.BlockSpec((1, 1, tq, LANES), cur)],
        out_shape=[jax.ShapeDtypeStruct((b, d, ls, B_WIDTH), BF16),
                   jax.ShapeDtypeStruct((b, d, ls, LANES), BF16)],
        compiler_params=pltpu.CompilerParams(
            dimension_semantics=("arbitrary", "arbitrary", "arbitrary"), vmem_limit_bytes=VMEM_LIMIT),
        name="band_attention_d%d" % d,
    )(q, k, k, v, v)


def _unpermute(ref, d, tt, width):
    if d == 1:
        return ref[0, 0].astype(F32)
    sub = ATT_BLK if d == 4 else 2 * ATT_BLK
    per = sub // d
    pm_t = _perm_matrix(sub, d, transposed=True)
    outs = []
    for sblk in range(tt // sub):
        y = jnp.concatenate([ref[0, r, sblk * per:(sblk + 1) * per, :] for r in range(d)], axis=0)
        outs.append(_dot(pm_t, y))
    return jnp.concatenate(outs, axis=0)


def _l1c_kernel(x_ref, o0_ref, o1_ref, o2_ref, l0_ref, l1_ref, l2_ref, zg_ref, qm_ref, zm_ref,
                mkv_ref, wout_ref, gpost_ref, y_ref):
    tt = x_ref.shape[1]
    o_refs = (o0_ref, o1_ref, o2_ref)
    l_refs = (l0_ref, l1_ref, l2_ref)
    outs, lses = [], []
    for g, (_, d) in enumerate(B_GROUPS):
        outs.append(_unpermute(o_refs[g], d, tt, B_WIDTH))
        lt = _unpermute(l_refs[g], d, tt, LANES)
        lses.append(lt[:, 0:4] + lt[:, 4:8] + lt[:, 8:12])
    mx = jnp.maximum(jnp.maximum(lses[0], lses[1]), lses[2])
    es = [jnp.exp(l - mx) for l in lses]
    tot = es[0] + es[1] + es[2]
    ws = [e / tot for e in es]
    parts = []
    for h in range(B_HEADS):
        hs = slice(h * B_HDIM, (h + 1) * B_HDIM)
        acc = ws[0][:, h:h + 1] * outs[0][:, hs]
        acc = acc + ws[1][:, h:h + 1] * outs[1][:, hs]
        acc = acc + ws[2][:, h:h + 1] * outs[2][:, hs]
        parts.append(acc)
    ydil = jnp.concatenate(parts, axis=-1)
    ymix = (ydil * _silu(zg_ref[0].astype(F32))).astype(BF16)
    mk = mkv_ref[0, :, 0:M_WIDTH]
    mv = mkv_ref[0, :, M_WIDTH:2 * M_WIDTH]
    ym = (_mem_attention(qm_ref[0], mk, mv) * _silu(zm_ref[0].astype(F32))).astype(BF16)
    out = _dot(ymix, wout_ref[0:B_WIDTH, :]) + _dot(ym, wout_ref[B_WIDTH:B_WIDTH + M_WIDTH, :])
    y_ref[0] = x_ref[0] + _rms_scale(out) * gpost_ref[...]


def _layer1_out_prompt(x1, os_, ls_, zg, qm, zm, mkv_bf, w_out, g_post):
    b, s, _ = x1.shape
    tt = min(TOK_TILE, s)
    nt = s // tt
    tile = lambda bb, i: (bb, i, 0)
    perm = lambda bb, i: (bb, 0, i, 0)
    in_specs = [pl.BlockSpec((1, tt, D_MODEL), tile)]
    for width in (B_WIDTH, LANES):
        for (_, d) in B_GROUPS:
            in_specs.append(pl.BlockSpec((1, d, tt // d, width), perm))
    in_specs += [pl.BlockSpec((1, tt, B_WIDTH), tile)] * 3
    in_specs += [pl.BlockSpec((1, N_MEM, 2 * M_WIDTH), lambda bb, i: (bb, 0, 0)),
                 _const_spec(w_out.shape), _const_spec((1, D_MODEL))]
    return pl.pallas_call(
        _l1c_kernel,
        grid=(b, nt),
        in_specs=in_specs,
        out_specs=pl.BlockSpec((1, tt, D_MODEL), tile),
        out_shape=jax.ShapeDtypeStruct((b, s, D_MODEL), F32),
        compiler_params=pltpu.CompilerParams(
            dimension_semantics=("arbitrary", "arbitrary"), vmem_limit_bytes=VMEM_LIMIT),
        name="layer1_out_prompt",
    )(x1, *os_, *ls_, zg, qm, zm, mkv_bf, w_out, g_post)


def _rope_tables(pos):
    half = B_HDIM // 2
    inv = ROPE_THETA ** (-jnp.arange(half, dtype=F32) / half)
    ang = pos[:, None] * inv[None, :]
    cos = jnp.cos(ang)
    sin = jnp.sin(ang)
    return jnp.concatenate([cos, cos], axis=-1), jnp.concatenate([-sin, sin], axis=-1)


def _prompt_group(x_prompt, mem_prompt, p):
    b, s, _ = x_prompt.shape
    memkv_f, memkv_b = _memkv(mem_prompt.reshape(b * N_MEM, D_MODEL), p['w_mkv'])
    depth = memkv_f.shape[0]
    memkv_b = memkv_b.reshape(depth, b, N_MEM, 2 * M_WIDTH)
    x1, conv_p, c_p, n_p, m_pad = _layer0_prompt(
        x_prompt, p['g_pre'][0:1], p['w_in_a'][0], p['conv_w_a'][0], p['conv_b_a'], p['w_q_a'][0],
        p['w_k_a'][0], p['w_v_a'][0], p['w_if_a'], p['b_if_a'], p['g_hn_a'], p['skip_a'],
        memkv_b[0], p['w_out_a'][0], p['g_post'][0:1])
    cos_t, sin_t = _rope_tables(jnp.arange(s, dtype=F32))
    outs = _layer1_proj_prompt(x1, p['g_kv'], p['g_pre'][1:2], p['w_kv_b'], p['w_in_b'][0], cos_t, sin_t)
    qs, ks, vs = outs[0:3], outs[3:6], outs[6:9]
    zg, qm, zm = outs[9:12]
    wins = outs[12:15]
    os_, ls_ = [], []
    for g in range(N_GROUPS):
        o, l = _band_attention(qs[g], ks[g], vs[g])
        os_.append(o)
        ls_.append(l)
    y = _layer1_out_prompt(x1, os_, ls_, zg, qm, zm, memkv_b[1], p['w_out_b'][0], p['g_post'][1:2])
    m_p = m_pad[:, 0:A_HEADS, 0][None]
    wins = [w.reshape(b, w.shape[1], 2, B_HEADS, B_HDIM) for w in wins]
    memkv_p = memkv_f.reshape(depth, b, N_MEM, 2, M_HEADS, M_HDIM)
    return y, conv_p, c_p, n_p, m_p, wins, memkv_p


def _prep_params(g_pre, g_post, w_in_a, conv_w_a, conv_b_a, w_q_a, w_k_a, w_v_a, w_if_a, b_if_a,
                 g_hn_a, skip_a, w_out_a, g_kv, w_kv_b, w_in_b, w_out_b, w_mkv):
    wif = jnp.pad(w_if_a[0], ((0, 0), (0, LANES - 2 * A_HEADS))).astype(BF16)
    bif = jnp.pad(b_if_a[0], (0, LANES - 2 * A_HEADS))[None, :]
    return {
        'g_pre': g_pre, 'g_post': g_post,
        'w_in_a': w_in_a.astype(BF16), 'conv_w_a': conv_w_a, 'conv_b_a': conv_b_a,
        'w_q_a': w_q_a.astype(BF16), 'w_k_a': w_k_a.astype(BF16), 'w_v_a': w_v_a.astype(BF16),
        'w_if_a': wif, 'b_if_a': bif, 'g_hn_a': g_hn_a, 'skip_a': skip_a,
        'w_out_a': w_out_a.astype(BF16), 'g_kv': g_kv[None, :], 'w_kv_b': w_kv_b.astype(BF16),
        'w_in_b': w_in_b.astype(BF16), 'w_out_b': w_out_b.astype(BF16), 'w_mkv': w_mkv.astype(BF16),
    }


def _dec_l0_proj_kernel(x_ref, gpre_ref, win_ref, cst_ref, convw_ref, convb_ref, wq_ref, wk_ref, wv_ref,
                        wif_ref, bif_ref,
                        q_ref, k_ref, v_ref, gates_ref, xc_ref, opre_ref, zg_ref, qm_ref, zm_ref, cnew_ref):
    h = (_rms_scale(x_ref[...]) * gpre_ref[...]).astype(BF16)
    u = _dot(h, win_ref[:, 0:A_INNER])
    opre_ref[...] = _dot(h, win_ref[:, A_INNER:2 * A_INNER])
    zg_ref[...] = _dot(h, win_ref[:, 2 * A_INNER:3 * A_INNER])
    qm_ref[...] = _dot(h, win_ref[:, 3 * A_INNER:3 * A_INNER + M_WIDTH])
    zm_ref[...] = _dot(h, win_ref[:, 3 * A_INNER + M_WIDTH:3 * A_INNER + 2 * M_WIDTH])
    cw = convw_ref[...]
    xc = convb_ref[...] + cst_ref[0] * cw[0:1, :]
    xc = xc + cst_ref[1] * cw[1:2, :]
    xc = xc + cst_ref[2] * cw[2:3, :]
    xc = xc + u * cw[3:4, :]
    xc = _silu(xc)
    xc_ref[...] = xc
    cnew_ref[0] = cst_ref[1]
    cnew_ref[1] = cst_ref[2]
    cnew_ref[2] = u
    qs, ks, vs, cat = [], [], [], []
    for hd in range(A_HEADS):
        sl = slice(hd * A_HDIM, (hd + 1) * A_HDIM)
        xh = xc[:, sl].astype(BF16)
        qh = _dot(xh, wq_ref[hd])
        kh = _dot(xh, wk_ref[hd]) * (A_HDIM ** -0.5)
        vh = _dot(u[:, sl].astype(BF16), wv_ref[hd])
        qs.append(qh)
        ks.append(kh)
        vs.append(vh)
        cat += [qh.astype(BF16), kh.astype(BF16), vh.astype(BF16)]
    q_ref[...] = jnp.concatenate(qs, axis=-1)
    k_ref[...] = jnp.concatenate(ks, axis=-1)
    v_ref[...] = jnp.concatenate(vs, axis=-1)
    gates_ref[...] = _dot(jnp.concatenate(cat, axis=-1), wif_ref[...]) + bif_ref[...]


def _whole(shape):
    nd = len(shape)
    return pl.BlockSpec(shape, lambda *_: (0,) * nd)


def _dec_l0_proj(x, g_pre, w_in, cst, conv_w, conv_b, wq, wk, wv, wif, bif):
    nb = x.shape[0]
    args = (x, g_pre, w_in, cst, conv_w, conv_b, wq, wk, wv, wif, bif)
    f = lambda *s: jax.ShapeDtypeStruct(s, F32)
    out_shape = [f(nb, A_INNER), f(nb, A_INNER), f(nb, A_INNER), f(nb, LANES), f(nb, A_INNER), f(nb, A_INNER),
                 f(nb, A_INNER), f(nb, M_WIDTH), f(nb, M_WIDTH), f(CONV_W - 1, nb, A_INNER)]
    return pl.pallas_call(
        _dec_l0_proj_kernel,
        grid=(1,),
        in_specs=[_whole(a.shape) for a in args],
        out_specs=[_whole(o.shape) for o in out_shape],
        out_shape=out_shape,
        compiler_params=pltpu.CompilerParams(dimension_semantics=("arbitrary",), vmem_limit_bytes=VMEM_LIMIT),
        name="dec_l0_proj",
    )(*args)


def _row_to_col(row, eye):
    return jnp.sum(jnp.where(eye, row, 0.0), axis=-1, keepdims=True)


def _col_to_row(colv, eye):
    return jnp.sum(jnp.where(eye, colv, 0.0), axis=0, keepdims=True)


def _dec_mem_attention(q, kv_ref_view):
    kk = kv_ref_view[:, 0]
    vv = kv_ref_view[:, 1]
    s = jnp.sum(kk * q[None], axis=-1, keepdims=True) * (M_HDIM ** -0.5)
    mx = jnp.max(s, axis=0, keepdims=True)
    p = jnp.exp(s - mx)
    p = p / jnp.sum(p, axis=0, keepdims=True)
    return jnp.sum(p * vv, axis=0)


def _dec_mlstm_kernel(q_ref, k_ref, v_ref, gates_ref, m_ref, c_ref, n_ref, qm_ref, kv_ref,
                      hs_ref, c_out, n_out, m_out, ym_ref):
    b = pl.program_id(0)
    rb = pl.ds(b, 1)
    g = gates_ref[rb, :]
    mrow = m_ref[rb, :]
    r = lax.broadcasted_iota(jnp.int32, (A_HDIM, A_HDIM), 0)
    c = lax.broadcasted_iota(jnp.int32, (A_HDIM, A_HDIM), 1)
    eye = r == c
    lane = lax.broadcasted_iota(jnp.int32, (1, LANES), 1)
    m_acc = jnp.zeros((1, LANES), F32)
    for h in range(A_HEADS):
        sl = slice(h * A_HDIM, (h + 1) * A_HDIM)
        qh = q_ref[rb, sl]
        kh = k_ref[rb, sl]
        vh = v_ref[rb, sl]
        c_old = c_ref[0, 0, h]
        n_old = n_ref[0, 0, h:h + 1, :]
        li = g[:, h:h + 1]
        lf = _log_sigmoid(g[:, 4 + h:5 + h])
        m_old = mrow[:, h:h + 1]
        cq = jnp.sum(c_old * qh, axis=-1, keepdims=True)
        nq = jnp.sum(n_old * qh, axis=-1, keepdims=True)
        qk = jnp.sum(qh * kh, axis=-1, keepdims=True)
        inter = lf + m_old
        m_new = jnp.maximum(inter, li)
        ws = jnp.exp(li - m_new)
        dec = jnp.exp(inter - m_new)
        sc = qk * ws
        v_col = _row_to_col(vh, eye)
        den = sc + dec * nq
        h_col = (sc * v_col + dec * cq) / jnp.maximum(jnp.abs(den), jnp.exp(-m_new))
        hs_ref[0, :, sl] = _col_to_row(h_col, eye)
        c_out[0, 0, h] = dec * c_old + (ws * v_col) * kh
        n_out[0, 0, h:h + 1, :] = dec * n_old + ws * kh
        m_acc = m_acc + jnp.where(lane == h, m_new, 0.0)
    m_out[0] = m_acc
    ym_ref[0] = _dec_mem_attention(qm_ref[0], kv_ref.at[0, 0])


def _dec_mlstm(q, k, v, gates, m_in, state_c, state_n, qm3, cache_mem_kv):
    nb = q.shape[0]
    per_b3 = lambda b: (b, 0, 0)
    in_specs = [_whole(q.shape), _whole(k.shape), _whole(v.shape), _whole(gates.shape), _whole(m_in.shape),
                pl.BlockSpec((1, 1, A_HEADS, A_HDIM, A_HDIM), lambda b: (0, b, 0, 0, 0)),
                pl.BlockSpec((1, 1, A_HEADS, A_HDIM), lambda b: (0, b, 0, 0)),
                pl.BlockSpec((1, M_HEADS, M_HDIM), per_b3),
                pl.BlockSpec((1, 1, N_MEM, 2, M_HEADS, M_HDIM), lambda b: (0, b, 0, 0, 0, 0))]
    out_specs = [pl.BlockSpec((1, 1, A_INNER), per_b3),
                 pl.BlockSpec((1, 1, A_HEADS, A_HDIM, A_HDIM), lambda b: (0, b, 0, 0, 0)),
                 pl.BlockSpec((1, 1, A_HEADS, A_HDIM), lambda b: (0, b, 0, 0)),
                 pl.BlockSpec((1, 1, LANES), per_b3),
                 pl.BlockSpec((1, M_HEADS, M_HDIM), per_b3)]
    out_shape = [jax.ShapeDtypeStruct((nb, 1, A_INNER), F32),
                 jax.ShapeDtypeStruct(state_c.shape, F32),
                 jax.ShapeDtypeStruct(state_n.shape, F32),
                 jax.ShapeDtypeStruct((nb, 1, LANES), F32),
                 jax.ShapeDtypeStruct((nb, M_HEADS, M_HDIM), F32)]
    return pl.pallas_call(
        _dec_mlstm_kernel,
        grid=(nb,),
        in_specs=in_specs,
        out_specs=out_specs,
        out_shape=out_shape,
        compiler_params=pltpu.CompilerParams(dimension_semantics=("arbitrary",), vmem_limit_bytes=VMEM_LIMIT),
        name="dec_mlstm",
    )(q, k, v, gates, m_in, state_c, state_n, qm3, cache_mem_kv)


def _dec_mid_kernel(hs_ref, opre_ref, xc_ref, zg_ref, ym_ref, zm_ref, x_ref, ghn_ref, skip_ref, wout_ref,
                    gpost_ref, gkv_ref, gpre_ref, wkv_ref, win_ref, cos_ref, sin_ref,
                    x1_ref, q_ref, k_ref, v_ref, zg1_ref, qm1_ref, zm1_ref):
    hh = _sigmoid(opre_ref[...]) * hs_ref[...]
    parts = []
    for h in range(A_HEADS):
        v = hh[:, h * A_HDIM:(h + 1) * A_HDIM]
        mu = jnp.mean(v, axis=-1, keepdims=True)
        var = jnp.mean(jnp.square(v - mu), axis=-1, keepdims=True)
        parts.append((v - mu) * lax.rsqrt(var + EPS))
    y = jnp.concatenate(parts, axis=-1) * ghn_ref[...] + skip_ref[...] * xc_ref[...]
    ymix = (y * _silu(zg_ref[...])).astype(BF16)
    ym = (ym_ref[...] * _silu(zm_ref[...])).astype(BF16)
    out = _dot(ymix, wout_ref[0:A_INNER, :]) + _dot(ym, wout_ref[A_INNER:A_INNER + M_WIDTH, :])
    x1 = x_ref[...] + _rms_scale(out) * gpost_ref[...]
    x1_ref[...] = x1
    xn = _rms_scale(x1)
    hk = (xn * gkv_ref[...]).astype(BF16)
    hq = (xn * gpre_ref[...]).astype(BF16)
    cos = cos_ref[...]
    sin = sin_ref[...]
    ks, vs = [], []
    for g in range(N_GROUPS):
        ks.append(_rope_cols(_dot(hk, wkv_ref[:, g * 2 * B_WIDTH:g * 2 * B_WIDTH + B_WIDTH]), cos, sin))
        vs.append(_dot(hk, wkv_ref[:, g * 2 * B_WIDTH + B_WIDTH:(g + 1) * 2 * B_WIDTH]))
    k_ref[...] = jnp.concatenate(ks, axis=-1)
    v_ref[...] = jnp.concatenate(vs, axis=-1)
    qoff = N_GROUPS * B_WIDTH
    q_ref[...] = _rope_cols(_dot(hq, win_ref[:, 0:qoff]), cos, sin)
    zg1_ref[...] = _dot(hq, win_ref[:, qoff:qoff + B_WIDTH])
    qm1_ref[...] = _dot(hq, win_ref[:, qoff + B_WIDTH:qoff + B_WIDTH + M_WIDTH])
    zm1_ref[...] = _dot(hq, win_ref[:, qoff + B_WIDTH + M_WIDTH:qoff + B_WIDTH + 2 * M_WIDTH])


def _dec_mid(hs, opre, xc, zg, ym, zm, x, ghn, skip, w_out, g_post, g_kv, g_pre, wkv, win, cos, sin):
    nb = x.shape[0]
    args = (hs, opre, xc, zg, ym, zm, x, ghn, skip, w_out, g_post, g_kv, g_pre, wkv, win, cos, sin)
    f = lambda *s: jax.ShapeDtypeStruct(s, F32)
    out_shape = [f(nb, D_MODEL), f(nb, N_GROUPS * B_WIDTH), f(nb, N_GROUPS * B_WIDTH), f(nb, N_GROUPS * B_WIDTH),
                 f(nb, B_WIDTH), f(nb, M_WIDTH), f(nb, M_WIDTH)]
    return pl.pallas_call(
        _dec_mid_kernel,
        grid=(1,),
        in_specs=[_whole(a.shape) for a in args],
        out_specs=[_whole(o.shape) for o in out_shape],
        out_shape=out_shape,
        compiler_params=pltpu.CompilerParams(dimension_semantics=("arbitrary",), vmem_limit_bytes=VMEM_LIMIT),
        name="dec_mid",
    )(*args)


def _dec_attn_kernel(q_ref, kn_ref, vn_ref, w0_ref, w1_ref, w2_ref, qm_ref, kv_ref, ydil_ref, ym_ref):
    w_refs = (w0_ref, w1_ref, w2_ref)
    scale = B_HDIM ** -0.5
    outs, lses = [], []
    for g in range(N_GROUPS):
        q = q_ref[0, g]
        kn = kn_ref[0, g]
        vn = vn_ref[0, g]
        wv = w_refs[g].at[0]
        kk = wv[:, 0]
        vv = wv[:, 1]
        s_c = jnp.sum(kk * q[None], axis=-1, keepdims=True) * scale
        s_n = jnp.sum(kn * q, axis=-1, keepdims=True) * scale
        mx = jnp.maximum(jnp.max(s_c, axis=0), s_n)
        p_c = jnp.exp(s_c - mx[None])
        p_n = jnp.exp(s_n - mx)
        l = jnp.sum(p_c, axis=0) + p_n
        outs.append(jnp.sum((p_c / l[None]) * vv, axis=0) + (p_n / l) * vn)
        lses.append(mx + jnp.log(l))
    mx = jnp.maximum(jnp.maximum(lses[0], lses[1]), lses[2])
    es = [jnp.exp(l - mx) for l in lses]
    tot = es[0] + es[1] + es[2]
    ydil_ref[0] = (es[0] / tot) * outs[0] + (es[1] / tot) * outs[1] + (es[2] / tot) * outs[2]
    ym_ref[0] = _dec_mem_attention(qm_ref[0], kv_ref.at[0, 0])


def _dec_attn(q4, kn4, vn4, cw0, cw1, cw2, qm3, cache_mem_kv, layer):
    nb = q4.shape[0]
    per_b3 = lambda b: (b, 0, 0)
    per_b4 = lambda b: (b, 0, 0, 0)
    rows = B_GROUPS[0][0]
    win_specs = [pl.BlockSpec((1, rows, 2, B_HEADS, B_HDIM), lambda b: (b, 0, 0, 0, 0))]
    for cw in (cw1, cw2):
        win_specs.append(pl.BlockSpec((1, rows, None, 2, B_HEADS, B_HDIM), lambda b: (b, 0, 0, 0, 0, 0)))
    in_specs = [pl.BlockSpec((1, N_GROUPS, B_HEADS, B_HDIM), per_b4)] * 3 + win_specs + [
        pl.BlockSpec((1, M_HEADS, M_HDIM), per_b3),
        pl.BlockSpec((1, 1, N_MEM, 2, M_HEADS, M_HDIM), lambda b: (layer, b, 0, 0, 0, 0))]
    return pl.pallas_call(
        _dec_attn_kernel,
        grid=(nb,),
        in_specs=in_specs,
        out_specs=[pl.BlockSpec((1, B_HEADS, B_HDIM), per_b3), pl.BlockSpec((1, M_HEADS, M_HDIM), per_b3)],
        out_shape=[jax.ShapeDtypeStruct((nb, B_HEADS, B_HDIM), F32),
                   jax.ShapeDtypeStruct((nb, M_HEADS, M_HDIM), F32)],
        compiler_params=pltpu.CompilerParams(dimension_semantics=("arbitrary",), vmem_limit_bytes=VMEM_LIMIT),
        name="dec_attn",
    )(q4, kn4, vn4, cw0, cw1, cw2, qm3, cache_mem_kv)


def _dec_out_kernel(ydil_ref, zg_ref, ym_ref, zm_ref, x_ref, wout_ref, gpost_ref, y_ref):
    ymix = (ydil_ref[...] * _silu(zg_ref[...])).astype(BF16)
    ym = (ym_ref[...] * _silu(zm_ref[...])).astype(BF16)
    out = _dot(ymix, wout_ref[0:B_WIDTH, :]) + _dot(ym, wout_ref[B_WIDTH:B_WIDTH + M_WIDTH, :])
    y_ref[...] = x_ref[...] + _rms_scale(out) * gpost_ref[...]


def _dec_out(ydil, zg, ym, zm, x1, w_out, g_post):
    args = (ydil, zg, ym, zm, x1, w_out, g_post)
    return pl.pallas_call(
        _dec_out_kernel,
        grid=(1,),
        in_specs=[_whole(a.shape) for a in args],
        out_specs=_whole(x1.shape),
        out_shape=jax.ShapeDtypeStruct(x1.shape, F32),
        compiler_params=pltpu.CompilerParams(dimension_semantics=("arbitrary",), vmem_limit_bytes=VMEM_LIMIT),
        name="dec_out",
    )(*args)


def _sample_group(x_sample, state_conv, state_c, state_n, state_m, cache_wins, cache_mem_kv, p):
    nb = x_sample.shape[0]
    x = x_sample.reshape(nb, D_MODEL)
    cst = state_conv[0].transpose(1, 0, 2)
    q, k, v, gates, xc, opre, zg, qm, zm, cnew = _dec_l0_proj(
        x, p['g_pre'][0:1], p['w_in_a'][0], cst, p['conv_w_a'][0], p['conv_b_a'], p['w_q_a'][0],
        p['w_k_a'][0], p['w_v_a'][0], p['w_if_a'], p['b_if_a'])
    m_in = jnp.pad(state_m[0], ((0, 0), (0, LANES - A_HEADS)))
    hs, c_s, n_s, m_pad, ym0 = _dec_mlstm(q, k, v, gates, m_in, state_c, state_n,
                                          qm.reshape(nb, M_HEADS, M_HDIM), cache_mem_kv)
    pos = PAST_LEN + jnp.arange(1, dtype=F32)
    cos, sin = _rope_tables(pos)
    x1, qd, kn, vn, zg1, qm1, zm1 = _dec_mid(
        hs.reshape(nb, A_INNER), opre, xc, zg, ym0.reshape(nb, M_WIDTH), zm, x, p['g_hn_a'], p['skip_a'],
        p['w_out_a'][0], p['g_post'][0:1], p['g_kv'], p['g_pre'][1:2], p['w_kv_b'], p['w_in_b'][0], cos, sin)
    shp4 = (nb, N_GROUPS, B_HEADS, B_HDIM)
    kn4 = kn.reshape(shp4)
    vn4 = vn.reshape(shp4)
    cws = [cache_wins[0]]
    for g in (1, 2):
        w, d = B_GROUPS[g]
        cws.append(cache_wins[g].reshape(nb, w // d, d, 2, B_HEADS, B_HDIM))
    ydil, ym1 = _dec_attn(qd.reshape(shp4), kn4, vn4, cws[0], cws[1], cws[2],
                          qm1.reshape(nb, M_HEADS, M_HDIM), cache_mem_kv, 1)
    y = _dec_out(ydil.reshape(nb, B_WIDTH), zg1, ym1.reshape(nb, M_WIDTH), zm1, x1, p['w_out_b'][0],
                 p['g_post'][1:2])
    conv_s = cnew.transpose(1, 0, 2)[None]
    m_s = m_pad[:, 0, 0:A_HEADS][None]
    wins_s = [jnp.stack([kn4[:, g], vn4[:, g]], axis=1)[:, None] for g in range(N_GROUPS)]
    return y.reshape(nb, 1, D_MODEL), conv_s, c_s, n_s, m_s, wins_s


def kernel(x_prompt, x_sample, mem_prompt, state_conv, state_C, state_n, state_m, cache_win0, cache_win1,
           cache_win2, cache_mem_kv, g_pre, g_post, w_in_a, conv_w_a, conv_b_a, w_q_a, w_k_a, w_v_a, w_if_a,
           b_if_a, g_hn_a, skip_a, w_out_a, g_kv, w_kv_b, w_in_b, w_out_b, w_mkv):
    p = _prep_params(g_pre, g_post, w_in_a, conv_w_a, conv_b_a, w_q_a, w_k_a, w_v_a, w_if_a, b_if_a,
                     g_hn_a, skip_a, w_out_a, g_kv, w_kv_b, w_in_b, w_out_b, w_mkv)
    y_p, conv_p, c_p, n_p, m_p, wins_p, memkv_p = _prompt_group(x_prompt, mem_prompt, p)
    y_s, conv_s, c_s, n_s, m_s, wins_s = _sample_group(
        x_sample, state_conv, state_C, state_n, state_m, (cache_win0, cache_win1, cache_win2), cache_mem_kv, p)
    return (y_p, y_s, conv_p, c_p, n_p, m_p, wins_p[0], wins_p[1], wins_p[2], memkv_p,
            conv_s, c_s, n_s, m_s, wins_s[0], wins_s[1], wins_s[2])
```

```python
import functools

import jax
import jax.numpy as jnp
from jax import lax
from jax.experimental import pallas as pl
from jax.experimental.pallas import tpu as pltpu

F32 = jnp.float32
BF16 = jnp.bfloat16

D_MODEL = 1024
A_HEADS = 4
A_HDIM = 256
A_INNER = 1024
CONV_W = 4
A_CHUNK = 128
B_GROUPS = ((128, 1), (512, 4), (2048, 16))
N_GROUPS = 3
B_HEADS = 4
B_HDIM = 128
B_WIDTH = 512
N_MEM = 256
M_HEADS = 4
M_HDIM = 128
M_WIDTH = 512
ROPE_THETA = 10000.0
EPS = 1e-6
PAST_LEN = 8192

LANES = 128
TOK_TILE = 512
ATT_BLK = 128
VMEM_LIMIT = 56 * 1024 * 1024

NT_DIMS = (((1,), (1,)), ((), ()))


def _dot(a, b):
    return jnp.dot(a, b, preferred_element_type=F32)


def _dot_nt(a, b):
    return lax.dot_general(a, b, NT_DIMS, preferred_element_type=F32)


def _sigmoid(x):
    return 1.0 / (1.0 + jnp.exp(-x))


def _silu(x):
    return x * _sigmoid(x)


def _log_sigmoid(x):
    return jnp.minimum(x, 0.0) - jnp.log(1.0 + jnp.exp(-jnp.abs(x)))


def _rms_scale(x):
    return x * lax.rsqrt(jnp.mean(x * x, axis=-1, keepdims=True) + EPS)


def _const_spec(shape):
    nd = len(shape)
    return pl.BlockSpec(shape, lambda *_: (0,) * nd, pipeline_mode=pl.Buffered(1))


def _mem_attention(qm, mk, mv):
    heads = range(M_HEADS)
    sl = [slice(h * M_HDIM, (h + 1) * M_HDIM) for h in heads]
    s = [_dot_nt(qm[:, sl[h]], mk[:, sl[h]]) * (M_HDIM ** -0.5) for h in heads]
    mx = [jnp.max(s[h], axis=-1, keepdims=True) for h in heads]
    p = [jnp.exp(s[h] - mx[h]) for h in heads]
    l = [jnp.sum(p[h], axis=-1, keepdims=True) for h in heads]
    outs = [_dot((p[h] / l[h]).astype(BF16), mv[:, sl[h]]) for h in heads]
    return jnp.concatenate(outs, axis=-1)


def _memkv_kernel(m_ref, w_ref, o_ref, ob_ref):
    r = _dot(m_ref[...].astype(BF16), w_ref[0])
    o_ref[0] = r
    ob_ref[0] = r.astype(BF16)


def _memkv(mem2d, w_bf):
    nm = mem2d.shape[0]
    nl = w_bf.shape[0]
    tm = min(512, nm)
    return pl.pallas_call(
        _memkv_kernel,
        grid=(nl, nm // tm),
        in_specs=[pl.BlockSpec((tm, D_MODEL), lambda l, i: (i, 0)),
                  pl.BlockSpec((1, D_MODEL, 2 * M_WIDTH), lambda l, i: (l, 0, 0))],
        out_specs=[pl.BlockSpec((1, tm, 2 * M_WIDTH), lambda l, i: (l, i, 0))] * 2,
        out_shape=[jax.ShapeDtypeStruct((nl, nm, 2 * M_WIDTH), F32),
                   jax.ShapeDtypeStruct((nl, nm, 2 * M_WIDTH), BF16)],
        compiler_params=pltpu.CompilerParams(dimension_semantics=("arbitrary", "arbitrary")),
        name="memkv",
    )(mem2d, w_bf)


def _l0_kernel(x_ref, gpre_ref, win_ref, convw_ref, convb_ref, wq_ref, wk_ref, wkt_ref, wv_ref,
               wif_ref, bif_ref, ghn_ref, skip_ref, mkv_ref, wout_ref, gpost_ref,
               x1_ref, conv_out, c_out, n_out, m_out,
               h_s, u_s, xc_s, opre_s, zg_s, qm_s, zm_s, qkv_s, kt_s, gates_s, ymix_s,
               c_s, n_s, m_s):
    tt = x_ref.shape[1]
    nsub = tt // A_CHUNK
    i = pl.program_id(1)
    nt = pl.num_programs(1)

    @pl.when(i == 0)
    def _():
        u_s[0:8, :] = jnp.zeros((8, A_INNER), F32)
        c_s[...] = jnp.zeros(c_s.shape, F32)
        n_s[...] = jnp.zeros(n_s.shape, F32)
        m_s[...] = jnp.zeros(m_s.shape, F32)

    gpre = gpre_ref[...]

    def norm_body(c, _):
        r = pl.ds(pl.multiple_of(c * A_CHUNK, A_CHUNK), A_CHUNK)
        h_s[r, :] = (_rms_scale(x_ref[0, r, :]) * gpre).astype(BF16)
        return 0
    lax.fori_loop(0, nsub, norm_body, 0)

    hb = h_s[...]
    u_s[8:8 + tt, :] = _dot(hb, win_ref[:, 0:A_INNER])
    opre_s[...] = _dot(hb, win_ref[:, A_INNER:2 * A_INNER])
    zg_s[...] = _dot(hb, win_ref[:, 2 * A_INNER:3 * A_INNER])
    qm_s[...] = _dot(hb, win_ref[:, 3 * A_INNER:3 * A_INNER + M_WIDTH]).astype(BF16)
    zm_s[...] = _dot(hb, win_ref[:, 3 * A_INNER + M_WIDTH:3 * A_INNER + 2 * M_WIDTH])

    cw = convw_ref[...]
    cb = convb_ref[...]

    for c in range(nsub):
        r0 = c * A_CHUNK
        xc = cb + u_s[r0 + 5:r0 + 5 + A_CHUNK, :] * cw[0:1, :]
        xc = xc + u_s[r0 + 6:r0 + 6 + A_CHUNK, :] * cw[1:2, :]
        xc = xc + u_s[r0 + 7:r0 + 7 + A_CHUNK, :] * cw[2:3, :]
        xc = xc + u_s[r0 + 8:r0 + 8 + A_CHUNK, :] * cw[3:4, :]
        xc_s[r0:r0 + A_CHUNK, :] = _silu(xc)

    for h in range(A_HEADS):
        sl = slice(h * A_HDIM, (h + 1) * A_HDIM)
        xh = xc_s[:, sl].astype(BF16)
        uh = u_s[8:8 + tt, sl].astype(BF16)
        base = h * 3 * A_HDIM
        qkv_s[:, base:base + A_HDIM] = _dot(xh, wq_ref[h]).astype(BF16)
        qkv_s[:, base + A_HDIM:base + 2 * A_HDIM] = (_dot(xh, wk_ref[h]) * (A_HDIM ** -0.5)).astype(BF16)
        qkv_s[:, base + 2 * A_HDIM:base + 3 * A_HDIM] = _dot(uh, wv_ref[h]).astype(BF16)
        kt = (_dot_nt(wkt_ref[h], xh) * (A_HDIM ** -0.5)).astype(BF16)
        for c in range(nsub):
            kt_s[h, c] = kt[:, c * A_CHUNK:(c + 1) * A_CHUNK]
    gates_s[...] = _dot(qkv_s[...], wif_ref[...]) + bif_ref[...]

    row = lax.broadcasted_iota(jnp.int32, (A_CHUNK, A_CHUNK), 0)
    col = lax.broadcasted_iota(jnp.int32, (A_CHUNK, A_CHUNK), 1)
    causal = col <= row
    tri = jnp.where(causal, 1.0, 0.0).astype(BF16)

    def chunk_body(c, _):
        r0 = pl.multiple_of(c * A_CHUNK, A_CHUNK)
        rs = pl.ds(r0, A_CHUNK)
        g = gates_s[rs, :]
        ls = _log_sigmoid(g)
        t0 = ls.astype(BF16)
        e1 = ls - t0.astype(F32)
        t1 = e1.astype(BF16)
        t2 = (e1 - t1.astype(F32)).astype(BF16)
        bc = _dot(tri, t0) + _dot(tri, t1) + _dot(tri, t2)
        lane = lax.broadcasted_iota(jnp.int32, (A_CHUNK, LANES), 1)
        xt = jnp.where(lane < A_HEADS, g, bc).T
        heads = range(A_HEADS)
        b_col = [bc[:, 4 + h:5 + h] for h in heads]
        b_row = [xt[4 + h:5 + h, :] for h in heads]
        li_row = [xt[h:h + 1, :] for h in heads]
        li_col = [g[:, h:h + 1] for h in heads]
        m_old = [m_s[h:h + 1, 0:1] for h in heads]
        b_last = [bc[A_CHUNK - 1:A_CHUNK, 4 + h:5 + h] for h in heads]
        qh = [qkv_s[rs, h * 3 * A_HDIM:h * 3 * A_HDIM + A_HDIM] for h in heads]
        kh = [qkv_s[rs, h * 3 * A_HDIM + A_HDIM:h * 3 * A_HDIM + 2 * A_HDIM] for h in heads]
        vh = [qkv_s[rs, h * 3 * A_HDIM + 2 * A_HDIM:(h + 1) * 3 * A_HDIM] for h in heads]
        kt = [kt_s[h, c] for h in heads]
        c_old = [c_s[h] for h in heads]
        n_old = [n_s[h:h + 1, :] for h in heads]
        qk = [_dot_nt(qh[h], kh[h]) for h in heads]
        qc = [_dot(qh[h], c_old[h].astype(BF16)) for h in heads]
        dm = [jnp.where(causal, b_col[h] - b_row[h] + li_row[h], -jnp.inf) for h in heads]
        inter = [b_col[h] + m_old[h] for h in heads]
        m_row = [jnp.maximum(inter[h], jnp.max(dm[h], axis=-1, keepdims=True)) for h in heads]
        g_max = [jnp.max(b_last[h] - b_row[h] + li_row[h], axis=-1, keepdims=True) for h in heads]
        m_new = [jnp.maximum(b_last[h] + m_old[h], g_max[h]) for h in heads]
        sc = [qk[h] * jnp.exp(dm[h] - m_row[h]) for h in heads]
        dec = [jnp.exp(inter[h] - m_row[h]) for h in heads]
        ws_col = [jnp.exp(b_last[h] - b_col[h] + li_col[h] - m_new[h]) for h in heads]
        dc = [jnp.exp(b_last[h] + m_old[h] - m_new[h]) for h in heads]
        sv = [_dot(sc[h].astype(BF16), vh[h]) for h in heads]
        wv = [(ws_col[h] * vh[h].astype(F32)).astype(BF16) for h in heads]
        upd = [_dot(kt[h], wv[h]) for h in heads]
        hs = []
        for h in heads:
            den = (jnp.sum(sc[h], axis=-1, keepdims=True)
                   + dec[h] * jnp.sum(qh[h].astype(F32) * n_old[h], axis=-1, keepdims=True))
            num = sv[h] + dec[h] * qc[h]
            hs.append(num / jnp.maximum(jnp.abs(den), jnp.exp(-m_row[h])))
        for h in heads:
            c_s[h] = dc[h] * c_old[h] + upd[h]
            n_s[h:h + 1, :] = dc[h] * n_old[h] + jnp.sum(ws_col[h] * kh[h].astype(F32), axis=0, keepdims=True)
            m_s[h:h + 1, :] = jnp.broadcast_to(m_new[h], (1, LANES))

        parts = []
        for h in heads:
            v = _sigmoid(opre_s[rs, h * A_HDIM:(h + 1) * A_HDIM]) * hs[h]
            mu = jnp.mean(v, axis=-1, keepdims=True)
            var = jnp.mean(jnp.square(v - mu), axis=-1, keepdims=True)
            parts.append((v - mu) * lax.rsqrt(var + EPS))
        hn = jnp.concatenate(parts, axis=-1) * ghn
        y = hn + skp * xc_s[rs, :]
        ymix_s[rs, 0:A_INNER] = (y * _silu(zg_s[rs, :])).astype(BF16)
        ym = _mem_attention(qm_s[rs, :], mk, mv) * _silu(zm_s[rs, :])
        ymix_s[rs, A_INNER:A_INNER + M_WIDTH] = ym.astype(BF16)
        return 0

    ghn = ghn_ref[...]
    skp = skip_ref[...]
    mk = mkv_ref[0, :, 0:M_WIDTH]
    mv = mkv_ref[0, :, M_WIDTH:2 * M_WIDTH]
    lax.fori_loop(0, nsub, chunk_body, 0)

    out = _dot(ymix_s[...], wout_ref[...])
    x1_ref[0] = x_ref[0] + _rms_scale(out) * gpost_ref[...]

    u_s[0:8, :] = u_s[tt:tt + 8, :]

    @pl.when(i == nt - 1)
    def _():
        conv_out[0, 0] = u_s[tt + 5:tt + 8, :]
        for h in range(A_HEADS):
            c_out[0, 0, h] = c_s[h].T
        n_out[0, 0] = n_s[0:A_HEADS, :]
        m_out[0] = m_s[...]


def _layer0_prompt(x, g_pre, w_in, conv_w, conv_b, wq, wk, wkt, wv, wif, bif, ghn, skip, mkv_bf, w_out,
                   g_post):
    b, s, _ = x.shape
    tt = min(TOK_TILE, s)
    nt = s // tt
    a_in = w_in.shape[1]
    tile = lambda bb, i: (bb, i, 0)
    per_b = lambda bb, i: (bb, 0, 0)
    in_specs = [
        pl.BlockSpec((1, tt, D_MODEL), tile),
        _const_spec((1, D_MODEL)),
        _const_spec((D_MODEL, a_in)),
        _const_spec((CONV_W, A_INNER)),
        _const_spec((1, A_INNER)),
        _const_spec((A_HEADS, A_HDIM, A_HDIM)),
        _const_spec((A_HEADS, A_HDIM, A_HDIM)),
        _const_spec((A_HEADS, A_HDIM, A_HDIM)),
        _const_spec((A_HEADS, A_HDIM, A_HDIM)),
        _const_spec((3 * A_INNER, LANES)),
        _const_spec((1, LANES)),
        _const_spec((1, A_INNER)),
        _const_spec((1, A_INNER)),
        pl.BlockSpec((1, N_MEM, 2 * M_WIDTH), per_b),
        _const_spec((A_INNER + M_WIDTH, D_MODEL)),
        _const_spec((1, D_MODEL)),
    ]
    out_specs = [
        pl.BlockSpec((1, tt, D_MODEL), tile),
        pl.BlockSpec((1, 1, CONV_W - 1, A_INNER), lambda bb, i: (0, bb, 0, 0)),
        pl.BlockSpec((1, 1, A_HEADS, A_HDIM, A_HDIM), lambda bb, i: (0, bb, 0, 0, 0)),
        pl.BlockSpec((1, 1, A_HEADS, A_HDIM), lambda bb, i: (0, bb, 0, 0)),
        pl.BlockSpec((1, 8, LANES), per_b),
    ]
    out_shape = [
        jax.ShapeDtypeStruct((b, s, D_MODEL), F32),
        jax.ShapeDtypeStruct((1, b, CONV_W - 1, A_INNER), F32),
        jax.ShapeDtypeStruct((1, b, A_HEADS, A_HDIM, A_HDIM), F32),
        jax.ShapeDtypeStruct((1, b, A_HEADS, A_HDIM), F32),
        jax.ShapeDtypeStruct((b, 8, LANES), F32),
    ]
    scratch = [
        pltpu.VMEM((tt, D_MODEL), BF16),
        pltpu.VMEM((tt + 8, A_INNER), F32),
        pltpu.VMEM((tt, A_INNER), F32),
        pltpu.VMEM((tt, A_INNER), F32),
        pltpu.VMEM((tt, A_INNER), F32),
        pltpu.VMEM((tt, M_WIDTH), BF16),
        pltpu.VMEM((tt, M_WIDTH), F32),
        pltpu.VMEM((tt, 3 * A_INNER), BF16),
        pltpu.VMEM((A_HEADS, tt // A_CHUNK, A_HDIM, A_CHUNK), BF16),
        pltpu.VMEM((tt, LANES), F32),
        pltpu.VMEM((tt, A_INNER + M_WIDTH), BF16),
        pltpu.VMEM((A_HEADS, A_HDIM, A_HDIM), F32),
        pltpu.VMEM((8, A_HDIM), F32),
        pltpu.VMEM((8, LANES), F32),
    ]
    return pl.pallas_call(
        _l0_kernel,
        grid=(b, nt),
        in_specs=in_specs,
        out_specs=out_specs,
        out_shape=out_shape,
        scratch_shapes=scratch,
        compiler_params=pltpu.CompilerParams(
            dimension_semantics=("arbitrary", "arbitrary"), vmem_limit_bytes=VMEM_LIMIT),
        name="layer0_prompt",
    )(x, g_pre, w_in, conv_w, conv_b, wq, wk, wkt, wv, wif, bif, ghn, skip, mkv_bf, w_out, g_post)


def _perm_matrix(n, d, transposed=False):
    a = lax.broadcasted_iota(jnp.int32, (n, n), 1 if transposed else 0)
    c = lax.broadcasted_iota(jnp.int32, (n, n), 0 if transposed else 1)
    per = n // d
    src = (a & (per - 1)) * d + lax.shift_right_logical(a, per.bit_length() - 1)
    return jnp.where(c == src, 1.0, 0.0).astype(BF16)


def _rope_cols(x, cos, sin_signed):
    outs = []
    for cblk in range(x.shape[1] // B_HDIM):
        xb = x[:, cblk * B_HDIM:(cblk + 1) * B_HDIM]
        outs.append(xb * cos + pltpu.roll(xb, B_HDIM // 2, 1) * sin_signed)
    return jnp.concatenate(outs, axis=-1)


def _l1a_kernel(x_ref, gkv_ref, gpre_ref, wkv_ref, win_ref, cos_ref, sin_ref,
                q0_ref, q1_ref, q2_ref, k0_ref, k1_ref, k2_ref, v0_ref, v1_ref, v2_ref,
                zg_ref, qm_ref, zm_ref, w0_ref, w1_ref, w2_ref):
    tt = x_ref.shape[1]
    x = x_ref[0]
    xn = _rms_scale(x)
    hk = (xn * gkv_ref[...]).astype(BF16)
    hq = (xn * gpre_ref[...]).astype(BF16)
    cos = cos_ref[...]
    sin = sin_ref[...]
    q_refs = (q0_ref, q1_ref, q2_ref)
    k_refs = (k0_ref, k1_ref, k2_ref)
    v_refs = (v0_ref, v1_ref, v2_ref)
    w_refs = (w0_ref, w1_ref, w2_ref)
    qoff = N_GROUPS * B_WIDTH
    zg_ref[0] = _dot(hq, win_ref[:, qoff:qoff + B_WIDTH]).astype(BF16)
    qm_ref[0] = _dot(hq, win_ref[:, qoff + B_WIDTH:qoff + B_WIDTH + M_WIDTH]).astype(BF16)
    zm_ref[0] = _dot(hq, win_ref[:, qoff + B_WIDTH + M_WIDTH:qoff + B_WIDTH + 2 * M_WIDTH]).astype(BF16)
    for g, (_, d) in enumerate(B_GROUPS):
        kf = _rope_cols(_dot(hk, wkv_ref[:, g * 2 * B_WIDTH:g * 2 * B_WIDTH + B_WIDTH]), cos, sin)
        vf = _dot(hk, wkv_ref[:, g * 2 * B_WIDTH + B_WIDTH:(g + 1) * 2 * B_WIDTH])
        qf = _rope_cols(_dot(hq, win_ref[:, g * B_WIDTH:(g + 1) * B_WIDTH]), cos, sin)
        wr = w_refs[g]
        wrows = wr.shape[1]
        wr[0, :, 0:B_WIDTH] = kf[tt - wrows:, :]
        wr[0, :, B_WIDTH:2 * B_WIDTH] = vf[tt - wrows:, :]
        for val, ref in ((qf.astype(BF16), q_refs[g]), (kf.astype(BF16), k_refs[g]), (vf.astype(BF16), v_refs[g])):
            if d == 1:
                ref[0, 0] = val
            else:
                sub = ATT_BLK if d == 4 else 2 * ATT_BLK
                per = sub // d
                pm = _perm_matrix(sub, d)
                for sblk in range(tt // sub):
                    y = _dot(pm, val[sblk * sub:(sblk + 1) * sub, :]).astype(BF16)
                    for r in range(d):
                        ref[0, r, sblk * per:(sblk + 1) * per, :] = y[r * per:(r + 1) * per, :]


def _layer1_proj_prompt(x1, g_kv, g_pre, wkv, win, cos_t, sin_t):
    b, s, _ = x1.shape
    tt = min(TOK_TILE, s)
    nt = s // tt
    tile = lambda bb, i: (bb, i, 0)
    in_specs = [
        pl.BlockSpec((1, tt, D_MODEL), tile),
        _const_spec((1, D_MODEL)),
        _const_spec((1, D_MODEL)),
        _const_spec(wkv.shape),
        _const_spec(win.shape),
        pl.BlockSpec((tt, B_HDIM), lambda bb, i: (i, 0)),
        pl.BlockSpec((tt, B_HDIM), lambda bb, i: (i, 0)),
    ]
    qkv_specs, qkv_shapes = [], []
    for _ in range(3):
        for (_, d) in B_GROUPS:
            qkv_specs.append(pl.BlockSpec((1, d, tt // d, B_WIDTH), lambda bb, i: (bb, 0, i, 0)))
            qkv_shapes.append(jax.ShapeDtypeStruct((b, d, s // d, B_WIDTH), BF16))
    gate_specs = [pl.BlockSpec((1, tt, B_WIDTH), tile)] * 3
    gate_shapes = [jax.ShapeDtypeStruct((b, s, B_WIDTH), BF16)] * 3
    win_specs, win_shapes = [], []
    for (w, _) in B_GROUPS:
        wr = min(w, s)
        rows = min(wr, tt)
        nblk = wr // rows
        win_specs.append(pl.BlockSpec(
            (1, rows, 2 * B_WIDTH),
            functools.partial(lambda bb, i, nb: (bb, jnp.maximum(i - (nt - nb), 0), 0), nb=nblk)))
        win_shapes.append(jax.ShapeDtypeStruct((b, wr, 2 * B_WIDTH), F32))
    return pl.pallas_call(
        _l1a_kernel,
        grid=(b, nt),
        in_specs=in_specs,
        out_specs=qkv_specs + gate_specs + win_specs,
        out_shape=qkv_shapes + gate_shapes + win_shapes,
        compiler_params=pltpu.CompilerParams(
            dimension_semantics=("arbitrary", "arbitrary"), vmem_limit_bytes=VMEM_LIMIT),
        name="layer1_proj_prompt",
    )(x1, g_kv, g_pre, wkv, win, cos_t, sin_t)


def _split3_lanes(cols):
    t = cols[0].shape[0]
    lane = lax.broadcasted_iota(jnp.int32, (t, LANES), 1)
    acc = jnp.zeros((t, LANES), F32)
    for h, cvec in enumerate(cols):
        t0 = cvec.astype(BF16).astype(F32)
        e1 = cvec - t0
        t1 = e1.astype(BF16).astype(F32)
        t2 = e1 - t1
        acc = acc + jnp.where(lane == h, t0, 0.0) + jnp.where(lane == 4 + h, t1, 0.0) \
            + jnp.where(lane == 8 + h, t2, 0.0)
    return acc.astype(BF16)


def _band_attn_kernel(q_ref, kc_ref, kp_ref, vc_ref, vp_ref, o_ref, lse_ref):
    tq = q_ref.shape[2]
    nsb = tq // ATT_BLK
    j = pl.program_id(2)
    row = lax.broadcasted_iota(jnp.int32, (ATT_BLK, 2 * ATT_BLK), 0)
    col = lax.broadcasted_iota(jnp.int32, (ATT_BLK, 2 * ATT_BLK), 1)
    band = jnp.logical_and(col >= row, col <= row + ATT_BLK)
    first_pen = jnp.where(col < ATT_BLK, jnp.where(j > 0, 0.0, -jnp.inf), 0.0)
    scale = B_HDIM ** -0.5
    qs, ks, vs = [], [], []
    for sb in range(nsb):
        rs = slice(sb * ATT_BLK, (sb + 1) * ATT_BLK)
        ps = slice((sb - 1) * ATT_BLK, sb * ATT_BLK)
        for h in range(B_HEADS):
            hs = slice(h * B_HDIM, (h + 1) * B_HDIM)
            qs.append(q_ref[0, 0, rs, hs])
            kp = kp_ref[0, 0, :, hs] if sb == 0 else kc_ref[0, 0, ps, hs]
            vp = vp_ref[0, 0, :, hs] if sb == 0 else vc_ref[0, 0, ps, hs]
            ks.append(jnp.concatenate([kp, kc_ref[0, 0, rs, hs]], axis=0))
            vs.append(jnp.concatenate([vp, vc_ref[0, 0, rs, hs]], axis=0))
    q3 = jnp.stack(qs)
    k3 = jnp.stack(ks)
    v3 = jnp.stack(vs)
    s = jnp.einsum('uqd,ukd->uqk', q3, k3, preferred_element_type=F32) * scale
    s = jnp.concatenate([s[0:B_HEADS] + first_pen[None], s[B_HEADS:]], axis=0)
    s = jnp.where(band[None], s, -jnp.inf)
    mx = jnp.max(s, axis=-1, keepdims=True)
    p = jnp.exp(s - mx)
    l = jnp.sum(p, axis=-1, keepdims=True)
    o = jnp.einsum('uqk,ukd->uqd', (p / l).astype(BF16), v3, preferred_element_type=F32)
    lse = mx + jnp.log(l)
    for sb in range(nsb):
        rs = slice(sb * ATT_BLK, (sb + 1) * ATT_BLK)
        for h in range(B_HEADS):
            o_ref[0, 0, rs, h * B_HDIM:(h + 1) * B_HDIM] = o[sb * B_HEADS + h].astype(BF16)
        lse_ref[0, 0, rs, :] = _split3_lanes([lse[sb * B_HEADS + h] for h in range(B_HEADS)])


def _band_attention(q, k, v):
    b, d, ls, _ = q.shape
    tq = min(TOK_TILE, ls)
    nj = ls // tq
    ratio = tq // ATT_BLK
    cur = lambda bb, r, j: (bb, r, j, 0)
    prev = lambda bb, r, j: (bb, r, jnp.maximum(j * ratio - 1, 0), 0)
    return pl.pallas_call(
        _band_attn_kernel,
        grid=(b, d, nj),
        in_specs=[pl.BlockSpec((1, 1, tq, B_WIDTH), cur),
                  pl.BlockSpec((1, 1, tq, B_WIDTH), cur),
                  pl.BlockSpec((1, 1, ATT_BLK, B_WIDTH), prev),
                  pl.BlockSpec((1, 1, tq, B_WIDTH), cur),
                  pl.BlockSpec((1, 1, ATT_BLK, B_WIDTH), prev)],
        out_specs=[pl.BlockSpec((1, 1, tq, B_WIDTH), cur),
                   pl.BlockSpec((1, 1, tq, LANES), cur)],
        out_shape=[jax.ShapeDtypeStruct((b, d, ls, B_WIDTH), BF16),
                   jax.ShapeDtypeStruct((b, d, ls, LANES), BF16)],
        compiler_params=pltpu.CompilerParams(
            dimension_semantics=("arbitrary", "arbitrary", "arbitrary"), vmem_limit_bytes=VMEM_LIMIT),
        name="band_attention_d%d" % d,
    )(q, k, k, v, v)


def _unpermute(ref, d, tt, width):
    if d == 1:
        return ref[0, 0].astype(F32)
    sub = ATT_BLK if d == 4 else 2 * ATT_BLK
    per = sub // d
    pm_t = _perm_matrix(sub, d, transposed=True)
    outs = []
    for sblk in range(tt // sub):
        y = jnp.concatenate([ref[0, r, sblk * per:(sblk + 1) * per, :] for r in range(d)], axis=0)
        outs.append(_dot(pm_t, y))
    return jnp.concatenate(outs, axis=0)


def _l1c_kernel(x_ref, o0_ref, o1_ref, o2_ref, l0_ref, l1_ref, l2_ref, zg_ref, qm_ref, zm_ref,
                mkv_ref, wout_ref, gpost_ref, y_ref):
    tt = x_ref.shape[1]
    o_refs = (o0_ref, o1_ref, o2_ref)
    l_refs = (l0_ref, l1_ref, l2_ref)
    outs, lses = [], []
    for g, (_, d) in enumerate(B_GROUPS):
        outs.append(_unpermute(o_refs[g], d, tt, B_WIDTH))
        lt = _unpermute(l_refs[g], d, tt, LANES)
        lses.append(lt[:, 0:4] + lt[:, 4:8] + lt[:, 8:12])
    mx = jnp.maximum(jnp.maximum(lses[0], lses[1]), lses[2])
    es = [jnp.exp(l - mx) for l in lses]
    tot = es[0] + es[1] + es[2]
    ws = [e / tot for e in es]
    parts = []
    for h in range(B_HEADS):
        hs = slice(h * B_HDIM, (h + 1) * B_HDIM)
        acc = ws[0][:, h:h + 1] * outs[0][:, hs]
        acc = acc + ws[1][:, h:h + 1] * outs[1][:, hs]
        acc = acc + ws[2][:, h:h + 1] * outs[2][:, hs]
        parts.append(acc)
    ydil = jnp.concatenate(parts, axis=-1)
    ymix = (ydil * _silu(zg_ref[0].astype(F32))).astype(BF16)
    mk = mkv_ref[0, :, 0:M_WIDTH]
    mv = mkv_ref[0, :, M_WIDTH:2 * M_WIDTH]
    ym = (_mem_attention(qm_ref[0], mk, mv) * _silu(zm_ref[0].astype(F32))).astype(BF16)
    out = _dot(ymix, wout_ref[0:B_WIDTH, :]) + _dot(ym, wout_ref[B_WIDTH:B_WIDTH + M_WIDTH, :])
    y_ref[0] = x_ref[0] + _rms_scale(out) * gpost_ref[...]


def _layer1_out_prompt(x1, os_, ls_, zg, qm, zm, mkv_bf, w_out, g_post):
    b, s, _ = x1.shape
    tt = min(TOK_TILE, s)
    nt = s // tt
    tile = lambda bb, i: (bb, i, 0)
    perm = lambda bb, i: (bb, 0, i, 0)
    in_specs = [pl.BlockSpec((1, tt, D_MODEL), tile)]
    for width in (B_WIDTH, LANES):
        for (_, d) in B_GROUPS:
            in_specs.append(pl.BlockSpec((1, d, tt // d, width), perm))
    in_specs += [pl.BlockSpec((1, tt, B_WIDTH), tile)] * 3
    in_specs += [pl.BlockSpec((1, N_MEM, 2 * M_WIDTH), lambda bb, i: (bb, 0, 0)),
                 _const_spec(w_out.shape), _const_spec((1, D_MODEL))]
    return pl.pallas_call(
        _l1c_kernel,
        grid=(b, nt),
        in_specs=in_specs,
        out_specs=pl.BlockSpec((1, tt, D_MODEL), tile),
        out_shape=jax.ShapeDtypeStruct((b, s, D_MODEL), F32),
        compiler_params=pltpu.CompilerParams(
            dimension_semantics=("arbitrary", "arbitrary"), vmem_limit_bytes=VMEM_LIMIT),
        name="layer1_out_prompt",
    )(x1, *os_, *ls_, zg, qm, zm, mkv_bf, w_out, g_post)


def _rope_tables(pos):
    half = B_HDIM // 2
    inv = ROPE_THETA ** (-jnp.arange(half, dtype=F32) / half)
    ang = pos[:, None] * inv[None, :]
    cos = jnp.cos(ang)
    sin = jnp.sin(ang)
    return jnp.concatenate([cos, cos], axis=-1), jnp.concatenate([-sin, sin], axis=-1)


def _prompt_group(x_prompt, mem_prompt, p):
    b, s, _ = x_prompt.shape
    memkv_f, memkv_b = _memkv(mem_prompt.reshape(b * N_MEM, D_MODEL), p['w_mkv'])
    depth = memkv_f.shape[0]
    memkv_b = memkv_b.reshape(depth, b, N_MEM, 2 * M_WIDTH)
    x1, conv_p, c_p, n_p, m_pad = _layer0_prompt(
        x_prompt, p['g_pre'][0:1], p['w_in_a'][0], p['conv_w_a'][0], p['conv_b_a'], p['w_q_a'][0],
        p['w_k_a'][0], jnp.swapaxes(p['w_k_a'][0], 1, 2), p['w_v_a'][0], p['w_if_a'], p['b_if_a'],
        p['g_hn_a'], p['skip_a'], memkv_b[0], p['w_out_a'][0], p['g_post'][0:1])
    cos_t, sin_t = _rope_tables(jnp.arange(s, dtype=F32))
    outs = _layer1_proj_prompt(x1, p['g_kv'], p['g_pre'][1:2], p['w_kv_b'], p['w_in_b'][0], cos_t, sin_t)
    qs, ks, vs = outs[0:3], outs[3:6], outs[6:9]
    zg, qm, zm = outs[9:12]
    wins = outs[12:15]
    os_, ls_ = [], []
    for g in range(N_GROUPS):
        o, l = _band_attention(qs[g], ks[g], vs[g])
        os_.append(o)
        ls_.append(l)
    y = _layer1_out_prompt(x1, os_, ls_, zg, qm, zm, memkv_b[1], p['w_out_b'][0], p['g_post'][1:2])
    m_p = m_pad[:, 0:A_HEADS, 0][None]
    wins = [w.reshape(b, w.shape[1], 2, B_HEADS, B_HDIM) for w in wins]
    memkv_p = memkv_f.reshape(depth, b, N_MEM, 2, M_HEADS, M_HDIM)
    return y, conv_p, c_p, n_p, m_p, wins, memkv_p


def _prep_params(g_pre, g_post, w_in_a, conv_w_a, conv_b_a, w_q_a, w_k_a, w_v_a, w_if_a, b_if_a,
                 g_hn_a, skip_a, w_out_a, g_kv, w_kv_b, w_in_b, w_out_b, w_mkv):
    wif = jnp.pad(w_if_a[0], ((0, 0), (0, LANES - 2 * A_HEADS))).astype(BF16)
    bif = jnp.pad(b_if_a[0], (0, LANES - 2 * A_HEADS))[None, :]
    return {
        'g_pre': g_pre, 'g_post': g_post,
        'w_in_a': w_in_a.astype(BF16), 'conv_w_a': conv_w_a, 'conv_b_a': conv_b_a,
        'w_q_a': w_q_a.astype(BF16), 'w_k_a': w_k_a.astype(BF16), 'w_v_a': w_v_a.astype(BF16),
        'w_if_a': wif, 'b_if_a': bif, 'g_hn_a': g_hn_a, 'skip_a': skip_a,
        'w_out_a': w_out_a.astype(BF16), 'g_kv': g_kv[None, :], 'w_kv_b': w_kv_b.astype(BF16),
        'w_in_b': w_in_b.astype(BF16), 'w_out_b': w_out_b.astype(BF16), 'w_mkv': w_mkv.astype(BF16),
    }


def _dec_l0_proj_kernel(x_ref, gpre_ref, win_ref, cst_ref, convw_ref, convb_ref, wq_ref, wk_ref, wv_ref,
                        wif_ref, bif_ref,
                        q_ref, k_ref, v_ref, gates_ref, xc_ref, opre_ref, zg_ref, qm_ref, zm_ref, cnew_ref):
    h = (_rms_scale(x_ref[...]) * gpre_ref[...]).astype(BF16)
    u = _dot(h, win_ref[:, 0:A_INNER])
    opre_ref[...] = _dot(h, win_ref[:, A_INNER:2 * A_INNER])
    zg_ref[...] = _dot(h, win_ref[:, 2 * A_INNER:3 * A_INNER])
    qm_ref[...] = _dot(h, win_ref[:, 3 * A_INNER:3 * A_INNER + M_WIDTH])
    zm_ref[...] = _dot(h, win_ref[:, 3 * A_INNER + M_WIDTH:3 * A_INNER + 2 * M_WIDTH])
    cw = convw_ref[...]
    xc = convb_ref[...] + cst_ref[0] * cw[0:1, :]
    xc = xc + cst_ref[1] * cw[1:2, :]
    xc = xc + cst_ref[2] * cw[2:3, :]
    xc = xc + u * cw[3:4, :]
    xc = _silu(xc)
    xc_ref[...] = xc
    cnew_ref[0] = cst_ref[1]
    cnew_ref[1] = cst_ref[2]
    cnew_ref[2] = u
    qs, ks, vs, cat = [], [], [], []
    for hd in range(A_HEADS):
        sl = slice(hd * A_HDIM, (hd + 1) * A_HDIM)
        xh = xc[:, sl].astype(BF16)
        qh = _dot(xh, wq_ref[hd])
        kh = _dot(xh, wk_ref[hd]) * (A_HDIM ** -0.5)
        vh = _dot(u[:, sl].astype(BF16), wv_ref[hd])
        qs.append(qh)
        ks.append(kh)
        vs.append(vh)
        cat += [qh.astype(BF16), kh.astype(BF16), vh.astype(BF16)]
    q_ref[...] = jnp.concatenate(qs, axis=-1)
    k_ref[...] = jnp.concatenate(ks, axis=-1)
    v_ref[...] = jnp.concatenate(vs, axis=-1)
    gates_ref[...] = _dot(jnp.concatenate(cat, axis=-1), wif_ref[...]) + bif_ref[...]


def _whole(shape):
    nd = len(shape)
    return pl.BlockSpec(shape, lambda *_: (0,) * nd)


def _dec_l0_proj(x, g_pre, w_in, cst, conv_w, conv_b, wq, wk, wv, wif, bif):
    nb = x.shape[0]
    args = (x, g_pre, w_in, cst, conv_w, conv_b, wq, wk, wv, wif, bif)
    f = lambda *s: jax.ShapeDtypeStruct(s, F32)
    out_shape = [f(nb, A_INNER), f(nb, A_INNER), f(nb, A_INNER), f(nb, LANES), f(nb, A_INNER), f(nb, A_INNER),
                 f(nb, A_INNER), f(nb, M_WIDTH), f(nb, M_WIDTH), f(CONV_W - 1, nb, A_INNER)]
    return pl.pallas_call(
        _dec_l0_proj_kernel,
        grid=(1,),
        in_specs=[_whole(a.shape) for a in args],
        out_specs=[_whole(o.shape) for o in out_shape],
        out_shape=out_shape,
        compiler_params=pltpu.CompilerParams(dimension_semantics=("arbitrary",), vmem_limit_bytes=VMEM_LIMIT),
        name="dec_l0_proj",
    )(*args)


def _row_to_col(row, eye):
    return jnp.sum(jnp.where(eye, row, 0.0), axis=-1, keepdims=True)


def _col_to_row(colv, eye):
    return jnp.sum(jnp.where(eye, colv, 0.0), axis=0, keepdims=True)


def _dec_mem_attention(q, kv_ref_view):
    kk = kv_ref_view[:, 0]
    vv = kv_ref_view[:, 1]
    s = jnp.sum(kk * q[None], axis=-1, keepdims=True) * (M_HDIM ** -0.5)
    mx = jnp.max(s, axis=0, keepdims=True)
    p = jnp.exp(s - mx)
    p = p / jnp.sum(p, axis=0, keepdims=True)
    return jnp.sum(p * vv, axis=0)


def _dec_mlstm_kernel(q_ref, k_ref, v_ref, gates_ref, m_ref, c_ref, n_ref, qm_ref, kv_ref,
                      hs_ref, c_out, n_out, m_out, ym_ref):
    b = pl.program_id(0)
    rb = pl.ds(b, 1)
    g = gates_ref[rb, :]
    mrow = m_ref[rb, :]
    r = lax.broadcasted_iota(jnp.int32, (A_HDIM, A_HDIM), 0)
    c = lax.broadcasted_iota(jnp.int32, (A_HDIM, A_HDIM), 1)
    eye = r == c
    lane = lax.broadcasted_iota(jnp.int32, (1, LANES), 1)
    m_acc = jnp.zeros((1, LANES), F32)
    for h in range(A_HEADS):
        sl = slice(h * A_HDIM, (h + 1) * A_HDIM)
        qh = q_ref[rb, sl]
        kh = k_ref[rb, sl]
        vh = v_ref[rb, sl]
        c_old = c_ref[0, 0, h]
        n_old = n_ref[0, 0, h:h + 1, :]
        li = g[:, h:h + 1]
        lf = _log_sigmoid(g[:, 4 + h:5 + h])
        m_old = mrow[:, h:h + 1]
        cq = jnp.sum(c_old * qh, axis=-1, keepdims=True)
        nq = jnp.sum(n_old * qh, axis=-1, keepdims=True)
        qk = jnp.sum(qh * kh, axis=-1, keepdims=True)
        inter = lf + m_old
        m_new = jnp.maximum(inter, li)
        ws = jnp.exp(li - m_new)
        dec = jnp.exp(inter - m_new)
        sc = qk * ws
        v_col = _row_to_col(vh, eye)
        den = sc + dec * nq
        h_col = (sc * v_col + dec * cq) / jnp.maximum(jnp.abs(den), jnp.exp(-m_new))
        hs_ref[0, :, sl] = _col_to_row(h_col, eye)
        c_out[0, 0, h] = dec * c_old + (ws * v_col) * kh
        n_out[0, 0, h:h + 1, :] = dec * n_old + ws * kh
        m_acc = m_acc + jnp.where(lane == h, m_new, 0.0)
    m_out[0] = m_acc
    ym_ref[0] = _dec_mem_attention(qm_ref[0], kv_ref.at[0, 0])


def _dec_mlstm(q, k, v, gates, m_in, state_c, state_n, qm3, cache_mem_kv):
    nb = q.shape[0]
    per_b3 = lambda b: (b, 0, 0)
    in_specs = [_whole(q.shape), _whole(k.shape), _whole(v.shape), _whole(gates.shape), _whole(m_in.shape),
                pl.BlockSpec((1, 1, A_HEADS, A_HDIM, A_HDIM), lambda b: (0, b, 0, 0, 0)),
                pl.BlockSpec((1, 1, A_HEADS, A_HDIM), lambda b: (0, b, 0, 0)),
                pl.BlockSpec((1, M_HEADS, M_HDIM), per_b3),
                pl.BlockSpec((1, 1, N_MEM, 2, M_HEADS, M_HDIM), lambda b: (0, b, 0, 0, 0, 0))]
    out_specs = [pl.BlockSpec((1, 1, A_INNER), per_b3),
                 pl.BlockSpec((1, 1, A_HEADS, A_HDIM, A_HDIM), lambda b: (0, b, 0, 0, 0)),
                 pl.BlockSpec((1, 1, A_HEADS, A_HDIM), lambda b: (0, b, 0, 0)),
                 pl.BlockSpec((1, 1, LANES), per_b3),
                 pl.BlockSpec((1, M_HEADS, M_HDIM), per_b3)]
    out_shape = [jax.ShapeDtypeStruct((nb, 1, A_INNER), F32),
                 jax.ShapeDtypeStruct(state_c.shape, F32),
                 jax.ShapeDtypeStruct(state_n.shape, F32),
                 jax.ShapeDtypeStruct((nb, 1, LANES), F32),
                 jax.ShapeDtypeStruct((nb, M_HEADS, M_HDIM), F32)]
    return pl.pallas_call(
        _dec_mlstm_kernel,
        grid=(nb,),
        in_specs=in_specs,
        out_specs=out_specs,
        out_shape=out_shape,
        compiler_params=pltpu.CompilerParams(dimension_semantics=("arbitrary",), vmem_limit_bytes=VMEM_LIMIT),
        name="dec_mlstm",
    )(q, k, v, gates, m_in, state_c, state_n, qm3, cache_mem_kv)


def _dec_mid_kernel(hs_ref, opre_ref, xc_ref, zg_ref, ym_ref, zm_ref, x_ref, ghn_ref, skip_ref, wout_ref,
                    gpost_ref, gkv_ref, gpre_ref, wkv_ref, win_ref, cos_ref, sin_ref,
                    x1_ref, q_ref, k_ref, v_ref, zg1_ref, qm1_ref, zm1_ref):
    hh = _sigmoid(opre_ref[...]) * hs_ref[...]
    parts = []
    for h in range(A_HEADS):
        v = hh[:, h * A_HDIM:(h + 1) * A_HDIM]
        mu = jnp.mean(v, axis=-1, keepdims=True)
        var = jnp.mean(jnp.square(v - mu), axis=-1, keepdims=True)
        parts.append((v - mu) * lax.rsqrt(var + EPS))
    y = jnp.concatenate(parts, axis=-1) * ghn_ref[...] + skip_ref[...] * xc_ref[...]
    ymix = (y * _silu(zg_ref[...])).astype(BF16)
    ym = (ym_ref[...] * _silu(zm_ref[...])).astype(BF16)
    out = _dot(ymix, wout_ref[0:A_INNER, :]) + _dot(ym, wout_ref[A_INNER:A_INNER + M_WIDTH, :])
    x1 = x_ref[...] + _rms_scale(out) * gpost_ref[...]
    x1_ref[...] = x1
    xn = _rms_scale(x1)
    hk = (xn * gkv_ref[...]).astype(BF16)
    hq = (xn * gpre_ref[...]).astype(BF16)
    cos = cos_ref[...]
    sin = sin_ref[...]
    ks, vs = [], []
    for g in range(N_GROUPS):
        ks.append(_rope_cols(_dot(hk, wkv_ref[:, g * 2 * B_WIDTH:g * 2 * B_WIDTH + B_WIDTH]), cos, sin))
        vs.append(_dot(hk, wkv_ref[:, g * 2 * B_WIDTH + B_WIDTH:(g + 1) * 2 * B_WIDTH]))
    k_ref[...] = jnp.concatenate(ks, axis=-1)
    v_ref[...] = jnp.concatenate(vs, axis=-1)
    qoff = N_GROUPS * B_WIDTH
    q_ref[...] = _rope_cols(_dot(hq, win_ref[:, 0:qoff]), cos, sin)
    zg1_ref[...] = _dot(hq, win_ref[:, qoff:qoff + B_WIDTH])
    qm1_ref[...] = _dot(hq, win_ref[:, qoff + B_WIDTH:qoff + B_WIDTH + M_WIDTH])
    zm1_ref[...] = _dot(hq, win_ref[:, qoff + B_WIDTH + M_WIDTH:qoff + B_WIDTH + 2 * M_WIDTH])


def _dec_mid(hs, opre, xc, zg, ym, zm, x, ghn, skip, w_out, g_post, g_kv, g_pre, wkv, win, cos, sin):
    nb = x.shape[0]
    args = (hs, opre, xc, zg, ym, zm, x, ghn, skip, w_out, g_post, g_kv, g_pre, wkv, win, cos, sin)
    f = lambda *s: jax.ShapeDtypeStruct(s, F32)
    out_shape = [f(nb, D_MODEL), f(nb, N_GROUPS * B_WIDTH), f(nb, N_GROUPS * B_WIDTH), f(nb, N_GROUPS * B_WIDTH),
                 f(nb, B_WIDTH), f(nb, M_WIDTH), f(nb, M_WIDTH)]
    return pl.pallas_call(
        _dec_mid_kernel,
        grid=(1,),
        in_specs=[_whole(a.shape) for a in args],
        out_specs=[_whole(o.shape) for o in out_shape],
        out_shape=out_shape,
        compiler_params=pltpu.CompilerParams(dimension_semantics=("arbitrary",), vmem_limit_bytes=VMEM_LIMIT),
        name="dec_mid",
    )(*args)


def _dec_attn_kernel(q_ref, kn_ref, vn_ref, w0_ref, w1_ref, w2_ref, qm_ref, kv_ref, ydil_ref, ym_ref):
    w_refs = (w0_ref, w1_ref, w2_ref)
    scale = B_HDIM ** -0.5
    outs, lses = [], []
    for g in range(N_GROUPS):
        q = q_ref[0, g]
        kn = kn_ref[0, g]
        vn = vn_ref[0, g]
        wv = w_refs[g].at[0]
        kk = wv[:, 0]
        vv = wv[:, 1]
        s_c = jnp.sum(kk * q[None], axis=-1, keepdims=True) * scale
        s_n = jnp.sum(kn * q, axis=-1, keepdims=True) * scale
        mx = jnp.maximum(jnp.max(s_c, axis=0), s_n)
        p_c = jnp.exp(s_c - mx[None])
        p_n = jnp.exp(s_n - mx)
        l = jnp.sum(p_c, axis=0) + p_n
        outs.append(jnp.sum((p_c / l[None]) * vv, axis=0) + (p_n / l) * vn)
        lses.append(mx + jnp.log(l))
    mx = jnp.maximum(jnp.maximum(lses[0], lses[1]), lses[2])
    es = [jnp.exp(l - mx) for l in lses]
    tot = es[0] + es[1] + es[2]
    ydil_ref[0] = (es[0] / tot) * outs[0] + (es[1] / tot) * outs[1] + (es[2] / tot) * outs[2]
    ym_ref[0] = _dec_mem_attention(qm_ref[0], kv_ref.at[0, 0])


def _dec_attn(q4, kn4, vn4, cw0, cw1, cw2, qm3, cache_mem_kv, layer):
    nb = q4.shape[0]
    per_b3 = lambda b: (b, 0, 0)
    per_b4 = lambda b: (b, 0, 0, 0)
    rows = B_GROUPS[0][0]
    win_specs = [pl.BlockSpec((1, rows, 2, B_HEADS, B_HDIM), lambda b: (b, 0, 0, 0, 0))]
    for cw in (cw1, cw2):
        win_specs.append(pl.BlockSpec((1, rows, None, 2, B_HEADS, B_HDIM), lambda b: (b, 0, 0, 0, 0, 0)))
    in_specs = [pl.BlockSpec((1, N_GROUPS, B_HEADS, B_HDIM), per_b4)] * 3 + win_specs + [
        pl.BlockSpec((1, M_HEADS, M_HDIM), per_b3),
        pl.BlockSpec((1, 1, N_MEM, 2, M_HEADS, M_HDIM), lambda b: (layer, b, 0, 0, 0, 0))]
    return pl.pallas_call(
        _dec_attn_kernel,
        grid=(nb,),
        in_specs=in_specs,
        out_specs=[pl.BlockSpec((1, B_HEADS, B_HDIM), per_b3), pl.BlockSpec((1, M_HEADS, M_HDIM), per_b3)],
        out_shape=[jax.ShapeDtypeStruct((nb, B_HEADS, B_HDIM), F32),
                   jax.ShapeDtypeStruct((nb, M_HEADS, M_HDIM), F32)],
        compiler_params=pltpu.CompilerParams(dimension_semantics=("arbitrary",), vmem_limit_bytes=VMEM_LIMIT),
        name="dec_attn",
    )(q4, kn4, vn4, cw0, cw1, cw2, qm3, cache_mem_kv)


def _dec_out_kernel(ydil_ref, zg_ref, ym_ref, zm_ref, x_ref, wout_ref, gpost_ref, y_ref):
    ymix = (ydil_ref[...] * _silu(zg_ref[...])).astype(BF16)
    ym = (ym_ref[...] * _silu(zm_ref[...])).astype(BF16)
    out = _dot(ymix, wout_ref[0:B_WIDTH, :]) + _dot(ym, wout_ref[B_WIDTH:B_WIDTH + M_WIDTH, :])
    y_ref[...] = x_ref[...] + _rms_scale(out) * gpost_ref[...]


def _dec_out(ydil, zg, ym, zm, x1, w_out, g_post):
    args = (ydil, zg, ym, zm, x1, w_out, g_post)
    return pl.pallas_call(
        _dec_out_kernel,
        grid=(1,),
        in_specs=[_whole(a.shape) for a in args],
        out_specs=_whole(x1.shape),
        out_shape=jax.ShapeDtypeStruct(x1.shape, F32),
        compiler_params=pltpu.CompilerParams(dimension_semantics=("arbitrary",), vmem_limit_bytes=VMEM_LIMIT),
        name="dec_out",
    )(*args)


def _sample_group(x_sample, state_conv, state_c, state_n, state_m, cache_wins, cache_mem_kv, p):
    nb = x_sample.shape[0]
    x = x_sample.reshape(nb, D_MODEL)
    cst = state_conv[0].transpose(1, 0, 2)
    q, k, v, gates, xc, opre, zg, qm, zm, cnew = _dec_l0_proj(
        x, p['g_pre'][0:1], p['w_in_a'][0], cst, p['conv_w_a'][0], p['conv_b_a'], p['w_q_a'][0],
        p['w_k_a'][0], p['w_v_a'][0], p['w_if_a'], p['b_if_a'])
    m_in = jnp.pad(state_m[0], ((0, 0), (0, LANES - A_HEADS)))
    hs, c_s, n_s, m_pad, ym0 = _dec_mlstm(q, k, v, gates, m_in, state_c, state_n,
                                          qm.reshape(nb, M_HEADS, M_HDIM), cache_mem_kv)
    pos = PAST_LEN + jnp.arange(1, dtype=F32)
    cos, sin = _rope_tables(pos)
    x1, qd, kn, vn, zg1, qm1, zm1 = _dec_mid(
        hs.reshape(nb, A_INNER), opre, xc, zg, ym0.reshape(nb, M_WIDTH), zm, x, p['g_hn_a'], p['skip_a'],
        p['w_out_a'][0], p['g_post'][0:1], p['g_kv'], p['g_pre'][1:2], p['w_kv_b'], p['w_in_b'][0], cos, sin)
    shp4 = (nb, N_GROUPS, B_HEADS, B_HDIM)
    kn4 = kn.reshape(shp4)
    vn4 = vn.reshape(shp4)
    cws = [cache_wins[0]]
    for g in (1, 2):
        w, d = B_GROUPS[g]
        cws.append(cache_wins[g].reshape(nb, w // d, d, 2, B_HEADS, B_HDIM))
    ydil, ym1 = _dec_attn(qd.reshape(shp4), kn4, vn4, cws[0], cws[1], cws[2],
                          qm1.reshape(nb, M_HEADS, M_HDIM), cache_mem_kv, 1)
    y = _dec_out(ydil.reshape(nb, B_WIDTH), zg1, ym1.reshape(nb, M_WIDTH), zm1, x1, p['w_out_b'][0],
                 p['g_post'][1:2])
    conv_s = cnew.transpose(1, 0, 2)[None]
    m_s = m_pad[:, 0, 0:A_HEADS][None]
    wins_s = [jnp.stack([kn4[:, g], vn4[:, g]], axis=1)[:, None] for g in range(N_GROUPS)]
    return y.reshape(nb, 1, D_MODEL), conv_s, c_s, n_s, m_s, wins_s


def kernel(x_prompt, x_sample, mem_prompt, state_conv, state_C, state_n, state_m, cache_win0, cache_win1,
           cache_win2, cache_mem_kv, g_pre, g_post, w_in_a, conv_w_a, conv_b_a, w_q_a, w_k_a, w_v_a, w_if_a,
           b_if_a, g_hn_a, skip_a, w_out_a, g_kv, w_kv_b, w_in_b, w_out_b, w_mkv):
    p = _prep_params(g_pre, g_post, w_in_a, conv_w_a, conv_b_a, w_q_a, w_k_a, w_v_a, w_if_a, b_if_a,
                     g_hn_a, skip_a, w_out_a, g_kv, w_kv_b, w_in_b, w_out_b, w_mkv)
    y_p, conv_p, c_p, n_p, m_p, wins_p, memkv_p = _prompt_group(x_prompt, mem_prompt, p)
    y_s, conv_s, c_s, n_s, m_s, wins_s = _sample_group(
        x_sample, state_conv, state_C, state_n, state_m, (cache_win0, cache_win1, cache_win2), cache_mem_kv, p)
    return (y_p, y_s, conv_p, c_p, n_p, m_p, wins_p[0], wins_p[1], wins_p[2], memkv_p,
            conv_s, c_s, n_s, m_s, wins_s[0], wins_s[1], wins_s[2])
```

```python
import functools

import jax
import jax.numpy as jnp
from jax import lax
from jax.experimental import pallas as pl
from jax.experimental.pallas import tpu as pltpu

F32 = jnp.float32
BF16 = jnp.bfloat16

D_MODEL = 1024
A_HEADS = 4
A_HDIM = 256
A_INNER = 1024
CONV_W = 4
A_CHUNK = 128
B_GROUPS = ((128, 1), (512, 4), (2048, 16))
N_GROUPS = 3
B_HEADS = 4
B_HDIM = 128
B_WIDTH = 512
N_MEM = 256
M_HEADS = 4
M_HDIM = 128
M_WIDTH = 512
ROPE_THETA = 10000.0
EPS = 1e-6
PAST_LEN = 8192

LANES = 128
TOK_TILE = 512
ATT_BLK = 128
VMEM_LIMIT = 56 * 1024 * 1024

NT_DIMS = (((1,), (1,)), ((), ()))


def _dot(a, b):
    return jnp.dot(a, b, preferred_element_type=F32)


def _dot_nt(a, b):
    return lax.dot_general(a, b, NT_DIMS, preferred_element_type=F32)


def _sigmoid(x):
    return 1.0 / (1.0 + jnp.exp(-x))


def _silu(x):
    return x * _sigmoid(x)


def _log_sigmoid(x):
    return jnp.minimum(x, 0.0) - jnp.log(1.0 + jnp.exp(-jnp.abs(x)))


def _rms_scale(x):
    return x * lax.rsqrt(jnp.mean(x * x, axis=-1, keepdims=True) + EPS)


def _const_spec(shape):
    nd = len(shape)
    return pl.BlockSpec(shape, lambda *_: (0,) * nd, pipeline_mode=pl.Buffered(1))


def _mem_attention(qm, mk, mv):
    heads = range(M_HEADS)
    sl = [slice(h * M_HDIM, (h + 1) * M_HDIM) for h in heads]
    s = [_dot_nt(qm[:, sl[h]], mk[:, sl[h]]) * (M_HDIM ** -0.5) for h in heads]
    mx = [jnp.max(s[h], axis=-1, keepdims=True) for h in heads]
    p = [jnp.exp(s[h] - mx[h]) for h in heads]
    l = [jnp.sum(p[h], axis=-1, keepdims=True) for h in heads]
    outs = [_dot((p[h] / l[h]).astype(BF16), mv[:, sl[h]]) for h in heads]
    return jnp.concatenate(outs, axis=-1)


def _store_kv_heads(ref_view, kv, val):
    for h in range(val.shape[1] // LANES):
        ref_view[:, kv, h, :] = val[:, h * LANES:(h + 1) * LANES]


def _memkv_kernel(m_ref, w_ref, o_ref, ob_ref):
    r = _dot(m_ref[...].astype(BF16), w_ref[0])
    _store_kv_heads(o_ref.at[0], 0, r[:, 0:M_WIDTH])
    _store_kv_heads(o_ref.at[0], 1, r[:, M_WIDTH:2 * M_WIDTH])
    ob_ref[0] = r.astype(BF16)


def _memkv(mem2d, w_bf):
    nm = mem2d.shape[0]
    nl = w_bf.shape[0]
    tm = min(512, nm)
    return pl.pallas_call(
        _memkv_kernel,
        grid=(nl, nm // tm),
        in_specs=[pl.BlockSpec((tm, D_MODEL), lambda l, i: (i, 0)),
                  pl.BlockSpec((1, D_MODEL, 2 * M_WIDTH), lambda l, i: (l, 0, 0))],
        out_specs=[pl.BlockSpec((1, tm, 2, M_HEADS, M_HDIM), lambda l, i: (l, i, 0, 0, 0)),
                   pl.BlockSpec((1, tm, 2 * M_WIDTH), lambda l, i: (l, i, 0))],
        out_shape=[jax.ShapeDtypeStruct((nl, nm, 2, M_HEADS, M_HDIM), F32),
                   jax.ShapeDtypeStruct((nl, nm, 2 * M_WIDTH), BF16)],
        compiler_params=pltpu.CompilerParams(dimension_semantics=("arbitrary", "arbitrary")),
        name="memkv",
    )(mem2d, w_bf)


def _l0_kernel(x_ref, gpre_ref, win_ref, convw_ref, convb_ref, wq_ref, wk_ref, wkt_ref, wv_ref,
               wif_ref, bif_ref, ghn_ref, skip_ref, mkv_ref, wout_ref, gpost_ref,
               x1_ref, conv_out, c_out, n_out, m_out,
               h_s, u_s, xc_s, opre_s, zg_s, qm_s, zm_s, qkv_s, kt_s, gates_s, ymix_s,
               c_s, n_s, m_s):
    tt = x_ref.shape[1]
    nsub = tt // A_CHUNK
    i = pl.program_id(1)
    nt = pl.num_programs(1)

    @pl.when(i == 0)
    def _():
        u_s[0:8, :] = jnp.zeros((8, A_INNER), F32)
        c_s[...] = jnp.zeros(c_s.shape, F32)
        n_s[...] = jnp.zeros(n_s.shape, F32)
        m_s[...] = jnp.zeros(m_s.shape, F32)

    gpre = gpre_ref[...]

    def norm_body(c, _):
        r = pl.ds(pl.multiple_of(c * A_CHUNK, A_CHUNK), A_CHUNK)
        h_s[r, :] = (_rms_scale(x_ref[0, r, :]) * gpre).astype(BF16)
        return 0
    lax.fori_loop(0, nsub, norm_body, 0)

    hb = h_s[...]
    u_s[8:8 + tt, :] = _dot(hb, win_ref[:, 0:A_INNER])
    opre_s[...] = _dot(hb, win_ref[:, A_INNER:2 * A_INNER])
    zg_s[...] = _dot(hb, win_ref[:, 2 * A_INNER:3 * A_INNER])
    qm_s[...] = _dot(hb, win_ref[:, 3 * A_INNER:3 * A_INNER + M_WIDTH]).astype(BF16)
    zm_s[...] = _dot(hb, win_ref[:, 3 * A_INNER + M_WIDTH:3 * A_INNER + 2 * M_WIDTH])

    cw = convw_ref[...]
    cb = convb_ref[...]

    for c in range(nsub):
        r0 = c * A_CHUNK
        xc = cb + u_s[r0 + 5:r0 + 5 + A_CHUNK, :] * cw[0:1, :]
        xc = xc + u_s[r0 + 6:r0 + 6 + A_CHUNK, :] * cw[1:2, :]
        xc = xc + u_s[r0 + 7:r0 + 7 + A_CHUNK, :] * cw[2:3, :]
        xc = xc + u_s[r0 + 8:r0 + 8 + A_CHUNK, :] * cw[3:4, :]
        xc_s[r0:r0 + A_CHUNK, :] = _silu(xc)

    for h in range(A_HEADS):
        sl = slice(h * A_HDIM, (h + 1) * A_HDIM)
        xh = xc_s[:, sl].astype(BF16)
        uh = u_s[8:8 + tt, sl].astype(BF16)
        base = h * 3 * A_HDIM
        qkv_s[:, base:base + A_HDIM] = _dot(xh, wq_ref[h]).astype(BF16)
        qkv_s[:, base + A_HDIM:base + 2 * A_HDIM] = (_dot(xh, wk_ref[h]) * (A_HDIM ** -0.5)).astype(BF16)
        qkv_s[:, base + 2 * A_HDIM:base + 3 * A_HDIM] = _dot(uh, wv_ref[h]).astype(BF16)
        kt = (_dot_nt(wkt_ref[h], xh) * (A_HDIM ** -0.5)).astype(BF16)
        for c in range(nsub):
            kt_s[h, c] = kt[:, c * A_CHUNK:(c + 1) * A_CHUNK]
    gates_s[...] = _dot(qkv_s[...], wif_ref[...]) + bif_ref[...]

    row = lax.broadcasted_iota(jnp.int32, (A_CHUNK, A_CHUNK), 0)
    col = lax.broadcasted_iota(jnp.int32, (A_CHUNK, A_CHUNK), 1)
    causal = col <= row
    tri = jnp.where(causal, 1.0, 0.0).astype(BF16)

    def chunk_body(c, _):
        r0 = pl.multiple_of(c * A_CHUNK, A_CHUNK)
        rs = pl.ds(r0, A_CHUNK)
        g = gates_s[rs, :]
        ls = _log_sigmoid(g)
        t0 = ls.astype(BF16)
        e1 = ls - t0.astype(F32)
        t1 = e1.astype(BF16)
        t2 = (e1 - t1.astype(F32)).astype(BF16)
        bc = _dot(tri, t0) + _dot(tri, t1) + _dot(tri, t2)
        lane = lax.broadcasted_iota(jnp.int32, (A_CHUNK, LANES), 1)
        xt = jnp.where(lane < A_HEADS, g, bc).T
        heads = range(A_HEADS)
        b_col = [bc[:, 4 + h:5 + h] for h in heads]
        b_row = [xt[4 + h:5 + h, :] for h in heads]
        li_row = [xt[h:h + 1, :] for h in heads]
        li_col = [g[:, h:h + 1] for h in heads]
        m_old = [m_s[h:h + 1, 0:1] for h in heads]
        b_last = [bc[A_CHUNK - 1:A_CHUNK, 4 + h:5 + h] for h in heads]
        qh = [qkv_s[rs, h * 3 * A_HDIM:h * 3 * A_HDIM + A_HDIM] for h in heads]
        kh = [qkv_s[rs, h * 3 * A_HDIM + A_HDIM:h * 3 * A_HDIM + 2 * A_HDIM] for h in heads]
        vh = [qkv_s[rs, h * 3 * A_HDIM + 2 * A_HDIM:(h + 1) * 3 * A_HDIM] for h in heads]
        kt = [kt_s[h, c] for h in heads]
        c_old = [c_s[h] for h in heads]
        n_old = [n_s[h:h + 1, :] for h in heads]
        qk = [_dot_nt(qh[h], kh[h]) for h in heads]
        qc = [_dot(qh[h], c_old[h].astype(BF16)) for h in heads]
        dm = [jnp.where(causal, b_col[h] - b_row[h] + li_row[h], -jnp.inf) for h in heads]
        inter = [b_col[h] + m_old[h] for h in heads]
        m_row = [jnp.maximum(inter[h], jnp.max(dm[h], axis=-1, keepdims=True)) for h in heads]
        g_max = [jnp.max(b_last[h] - b_row[h] + li_row[h], axis=-1, keepdims=True) for h in heads]
        m_new = [jnp.maximum(b_last[h] + m_old[h], g_max[h]) for h in heads]
        sc = [qk[h] * jnp.exp(dm[h] - m_row[h]) for h in heads]
        dec = [jnp.exp(inter[h] - m_row[h]) for h in heads]
        ws_col = [jnp.exp(b_last[h] - b_col[h] + li_col[h] - m_new[h]) for h in heads]
        dc = [jnp.exp(b_last[h] + m_old[h] - m_new[h]) for h in heads]
        sv = [_dot(sc[h].astype(BF16), vh[h]) for h in heads]
        wv = [(ws_col[h] * vh[h].astype(F32)).astype(BF16) for h in heads]
        upd = [_dot(kt[h], wv[h]) for h in heads]
        hs = []
        for h in heads:
            den = (jnp.sum(sc[h], axis=-1, keepdims=True)
                   + dec[h] * jnp.sum(qh[h].astype(F32) * n_old[h], axis=-1, keepdims=True))
            num = sv[h] + dec[h] * qc[h]
            hs.append(num / jnp.maximum(jnp.abs(den), jnp.exp(-m_row[h])))
        for h in heads:
            c_s[h] = dc[h] * c_old[h] + upd[h]
            n_s[h:h + 1, :] = dc[h] * n_old[h] + jnp.sum(ws_col[h] * kh[h].astype(F32), axis=0, keepdims=True)
            m_s[h:h + 1, :] = jnp.broadcast_to(m_new[h], (1, LANES))

        parts = []
        for h in heads:
            v = _sigmoid(opre_s[rs, h * A_HDIM:(h + 1) * A_HDIM]) * hs[h]
            mu = jnp.mean(v, axis=-1, keepdims=True)
            var = jnp.mean(jnp.square(v - mu), axis=-1, keepdims=True)
            parts.append((v - mu) * lax.rsqrt(var + EPS))
        hn = jnp.concatenate(parts, axis=-1) * ghn
        y = hn + skp * xc_s[rs, :]
        ymix_s[rs, 0:A_INNER] = (y * _silu(zg_s[rs, :])).astype(BF16)
        ym = _mem_attention(qm_s[rs, :], mk, mv) * _silu(zm_s[rs, :])
        ymix_s[rs, A_INNER:A_INNER + M_WIDTH] = ym.astype(BF16)
        return 0

    ghn = ghn_ref[...]
    skp = skip_ref[...]
    mk = mkv_ref[0, :, 0:M_WIDTH]
    mv = mkv_ref[0, :, M_WIDTH:2 * M_WIDTH]
    lax.fori_loop(0, nsub, chunk_body, 0)

    out = _dot(ymix_s[...], wout_ref[...])
    x1_ref[0] = x_ref[0] + _rms_scale(out) * gpost_ref[...]

    u_s[0:8, :] = u_s[tt:tt + 8, :]

    @pl.when(i == nt - 1)
    def _():
        conv_out[0, 0] = u_s[tt + 5:tt + 8, :]
        for h in range(A_HEADS):
            c_out[0, 0, h] = c_s[h].T
        n_out[0, 0] = n_s[0:A_HEADS, :]
        m_out[0] = m_s[...]


def _layer0_prompt(x, g_pre, w_in, conv_w, conv_b, wq, wk, wkt, wv, wif, bif, ghn, skip, mkv_bf, w_out,
                   g_post):
    b, s, _ = x.shape
    tt = min(TOK_TILE, s)
    nt = s // tt
    a_in = w_in.shape[1]
    tile = lambda bb, i: (bb, i, 0)
    per_b = lambda bb, i: (bb, 0, 0)
    in_specs = [
        pl.BlockSpec((1, tt, D_MODEL), tile),
        _const_spec((1, D_MODEL)),
        _const_spec((D_MODEL, a_in)),
        _const_spec((CONV_W, A_INNER)),
        _const_spec((1, A_INNER)),
        _const_spec((A_HEADS, A_HDIM, A_HDIM)),
        _const_spec((A_HEADS, A_HDIM, A_HDIM)),
        _const_spec((A_HEADS, A_HDIM, A_HDIM)),
        _const_spec((A_HEADS, A_HDIM, A_HDIM)),
        _const_spec((3 * A_INNER, LANES)),
        _const_spec((1, LANES)),
        _const_spec((1, A_INNER)),
        _const_spec((1, A_INNER)),
        pl.BlockSpec((1, N_MEM, 2 * M_WIDTH), per_b),
        _const_spec((A_INNER + M_WIDTH, D_MODEL)),
        _const_spec((1, D_MODEL)),
    ]
    out_specs = [
        pl.BlockSpec((1, tt, D_MODEL), tile),
        pl.BlockSpec((1, 1, CONV_W - 1, A_INNER), lambda bb, i: (0, bb, 0, 0)),
        pl.BlockSpec((1, 1, A_HEADS, A_HDIM, A_HDIM), lambda bb, i: (0, bb, 0, 0, 0)),
        pl.BlockSpec((1, 1, A_HEADS, A_HDIM), lambda bb, i: (0, bb, 0, 0)),
        pl.BlockSpec((1, 8, LANES), per_b),
    ]
    out_shape = [
        jax.ShapeDtypeStruct((b, s, D_MODEL), F32),
        jax.ShapeDtypeStruct((1, b, CONV_W - 1, A_INNER), F32),
        jax.ShapeDtypeStruct((1, b, A_HEADS, A_HDIM, A_HDIM), F32),
        jax.ShapeDtypeStruct((1, b, A_HEADS, A_HDIM), F32),
        jax.ShapeDtypeStruct((b, 8, LANES), F32),
    ]
    scratch = [
        pltpu.VMEM((tt, D_MODEL), BF16),
        pltpu.VMEM((tt + 8, A_INNER), F32),
        pltpu.VMEM((tt, A_INNER), F32),
        pltpu.VMEM((tt, A_INNER), F32),
        pltpu.VMEM((tt, A_INNER), F32),
        pltpu.VMEM((tt, M_WIDTH), BF16),
        pltpu.VMEM((tt, M_WIDTH), F32),
        pltpu.VMEM((tt, 3 * A_INNER), BF16),
        pltpu.VMEM((A_HEADS, tt // A_CHUNK, A_HDIM, A_CHUNK), BF16),
        pltpu.VMEM((tt, LANES), F32),
        pltpu.VMEM((tt, A_INNER + M_WIDTH), BF16),
        pltpu.VMEM((A_HEADS, A_HDIM, A_HDIM), F32),
        pltpu.VMEM((8, A_HDIM), F32),
        pltpu.VMEM((8, LANES), F32),
    ]
    return pl.pallas_call(
        _l0_kernel,
        grid=(b, nt),
        in_specs=in_specs,
        out_specs=out_specs,
        out_shape=out_shape,
        scratch_shapes=scratch,
        compiler_params=pltpu.CompilerParams(
            dimension_semantics=("arbitrary", "arbitrary"), vmem_limit_bytes=VMEM_LIMIT),
        name="layer0_prompt",
    )(x, g_pre, w_in, conv_w, conv_b, wq, wk, wkt, wv, wif, bif, ghn, skip, mkv_bf, w_out, g_post)


def _perm_matrix(n, d, transposed=False):
    a = lax.broadcasted_iota(jnp.int32, (n, n), 1 if transposed else 0)
    c = lax.broadcasted_iota(jnp.int32, (n, n), 0 if transposed else 1)
    per = n // d
    src = (a & (per - 1)) * d + lax.shift_right_logical(a, per.bit_length() - 1)
    return jnp.where(c == src, 1.0, 0.0).astype(BF16)


def _rope_cols(x, cos, sin_signed):
    outs = []
    for cblk in range(x.shape[1] // B_HDIM):
        xb = x[:, cblk * B_HDIM:(cblk + 1) * B_HDIM]
        outs.append(xb * cos + pltpu.roll(xb, B_HDIM // 2, 1) * sin_signed)
    return jnp.concatenate(outs, axis=-1)


def _l1a_kernel(nt_static, x_ref, gkv_ref, gpre_ref, wkv_ref, win_ref, cos_ref, sin_ref,
                q0_ref, q1_ref, q2_ref, k0_ref, k1_ref, k2_ref, v0_ref, v1_ref, v2_ref,
                zg_ref, qm_ref, zm_ref, w0_ref, w1_ref, w2_ref):
    tt = x_ref.shape[1]
    x = x_ref[0]
    xn = _rms_scale(x)
    hk = (xn * gkv_ref[...]).astype(BF16)
    hq = (xn * gpre_ref[...]).astype(BF16)
    cos = cos_ref[...]
    sin = sin_ref[...]
    q_refs = (q0_ref, q1_ref, q2_ref)
    k_refs = (k0_ref, k1_ref, k2_ref)
    v_refs = (v0_ref, v1_ref, v2_ref)
    w_refs = (w0_ref, w1_ref, w2_ref)
    qoff = N_GROUPS * B_WIDTH
    zg_ref[0] = _dot(hq, win_ref[:, qoff:qoff + B_WIDTH]).astype(BF16)
    qm_ref[0] = _dot(hq, win_ref[:, qoff + B_WIDTH:qoff + B_WIDTH + M_WIDTH]).astype(BF16)
    zm_ref[0] = _dot(hq, win_ref[:, qoff + B_WIDTH + M_WIDTH:qoff + B_WIDTH + 2 * M_WIDTH]).astype(BF16)
    for g, (_, d) in enumerate(B_GROUPS):
        kf = _rope_cols(_dot(hk, wkv_ref[:, g * 2 * B_WIDTH:g * 2 * B_WIDTH + B_WIDTH]), cos, sin)
        vf = _dot(hk, wkv_ref[:, g * 2 * B_WIDTH + B_WIDTH:(g + 1) * 2 * B_WIDTH])
        qf = _rope_cols(_dot(hq, win_ref[:, g * B_WIDTH:(g + 1) * B_WIDTH]), cos, sin)
        wr = w_refs[g]
        wrows = wr.shape[1]
        nblk = min(B_GROUPS[g][0], tt * nt_static) // wrows

        @pl.when(pl.program_id(1) >= nt_static - nblk)
        def _(wr=wr, wrows=wrows, kf=kf, vf=vf):
            _store_kv_heads(wr.at[0], 0, kf[tt - wrows:, :])
            _store_kv_heads(wr.at[0], 1, vf[tt - wrows:, :])
        for val, ref in ((qf.astype(BF16), q_refs[g]), (kf.astype(BF16), k_refs[g]), (vf.astype(BF16), v_refs[g])):
            if d == 1:
                ref[0, 0] = val
            else:
                sub = ATT_BLK if d == 4 else 2 * ATT_BLK
                per = sub // d
                pm = _perm_matrix(sub, d)
                for sblk in range(tt // sub):
                    y = _dot(pm, val[sblk * sub:(sblk + 1) * sub, :]).astype(BF16)
                    for r in range(d):
                        ref[0, r, sblk * per:(sblk + 1) * per, :] = y[r * per:(r + 1) * per, :]


def _layer1_proj_prompt(x1, g_kv, g_pre, wkv, win, cos_t, sin_t):
    b, s, _ = x1.shape
    tt = min(TOK_TILE, s)
    nt = s // tt
    tile = lambda bb, i: (bb, i, 0)
    in_specs = [
        pl.BlockSpec((1, tt, D_MODEL), tile),
        _const_spec((1, D_MODEL)),
        _const_spec((1, D_MODEL)),
        _const_spec(wkv.shape),
        _const_spec(win.shape),
        pl.BlockSpec((tt, B_HDIM), lambda bb, i: (i, 0)),
        pl.BlockSpec((tt, B_HDIM), lambda bb, i: (i, 0)),
    ]
    qkv_specs, qkv_shapes = [], []
    for _ in range(3):
        for (_, d) in B_GROUPS:
            qkv_specs.append(pl.BlockSpec((1, d, tt // d, B_WIDTH), lambda bb, i: (bb, 0, i, 0)))
            qkv_shapes.append(jax.ShapeDtypeStruct((b, d, s // d, B_WIDTH), BF16))
    gate_specs = [pl.BlockSpec((1, tt, B_WIDTH), tile)] * 3
    gate_shapes = [jax.ShapeDtypeStruct((b, s, B_WIDTH), BF16)] * 3
    win_specs, win_shapes = [], []
    for (w, _) in B_GROUPS:
        wr = min(w, s)
        rows = min(wr, tt)
        nblk = wr // rows
        win_specs.append(pl.BlockSpec(
            (1, rows, 2, B_HEADS, B_HDIM),
            functools.partial(lambda bb, i, nb: (bb, jnp.maximum(i - (nt - nb), 0), 0, 0, 0), nb=nblk)))
        win_shapes.append(jax.ShapeDtypeStruct((b, wr, 2, B_HEADS, B_HDIM), F32))
    return pl.pallas_call(
        functools.partial(_l1a_kernel, nt),
        grid=(b, nt),
        in_specs=in_specs,
        out_specs=qkv_specs + gate_specs + win_specs,
        out_shape=qkv_shapes + gate_shapes + win_shapes,
        compiler_params=pltpu.CompilerParams(
            dimension_semantics=("arbitrary", "arbitrary"), vmem_limit_bytes=VMEM_LIMIT),
        name="layer1_proj_prompt",
    )(x1, g_kv, g_pre, wkv, win, cos_t, sin_t)


def _split3_lanes(cols):
    t = cols[0].shape[0]
    lane = lax.broadcasted_iota(jnp.int32, (t, LANES), 1)
    acc = jnp.zeros((t, LANES), F32)
    for h, cvec in enumerate(cols):
        t0 = cvec.astype(BF16).astype(F32)
        e1 = cvec - t0
        t1 = e1.astype(BF16).astype(F32)
        t2 = e1 - t1
        acc = acc + jnp.where(lane == h, t0, 0.0) + jnp.where(lane == 4 + h, t1, 0.0) \
            + jnp.where(lane == 8 + h, t2, 0.0)
    return acc.astype(BF16)


def _band_attn_kernel(q_ref, kc_ref, kp_ref, vc_ref, vp_ref, o_ref, lse_ref):
    tq = q_ref.shape[2]
    nsb = tq // ATT_BLK
    j = pl.program_id(2)
    row = lax.broadcasted_iota(jnp.int32, (ATT_BLK, 2 * ATT_BLK), 0)
    col = lax.broadcasted_iota(jnp.int32, (ATT_BLK, 2 * ATT_BLK), 1)
    band = jnp.logical_and(col >= row, col <= row + ATT_BLK)
    first_pen = jnp.where(col < ATT_BLK, jnp.where(j > 0, 0.0, -jnp.inf), 0.0)
    scale = B_HDIM ** -0.5
    qs, ks, vs = [], [], []
    for sb in range(nsb):
        rs = slice(sb * ATT_BLK, (sb + 1) * ATT_BLK)
        ps = slice((sb - 1) * ATT_BLK, sb * ATT_BLK)
        for h in range(B_HEADS):
            hs = slice(h * B_HDIM, (h + 1) * B_HDIM)
            qs.append(q_ref[0, 0, rs, hs])
            kp = kp_ref[0, 0, :, hs] if sb == 0 else kc_ref[0, 0, ps, hs]
            vp = vp_ref[0, 0, :, hs] if sb == 0 else vc_ref[0, 0, ps, hs]
            ks.append(jnp.concatenate([kp, kc_ref[0, 0, rs, hs]], axis=0))
            vs.append(jnp.concatenate([vp, vc_ref[0, 0, rs, hs]], axis=0))
    q3 = jnp.stack(qs)
    k3 = jnp.stack(ks)
    v3 = jnp.stack(vs)
    s = jnp.einsum('uqd,ukd->uqk', q3, k3, preferred_element_type=F32) * scale
    s = jnp.concatenate([s[0:B_HEADS] + first_pen[None], s[B_HEADS:]], axis=0)
    s = jnp.where(band[None], s, -jnp.inf)
    mx = jnp.max(s, axis=-1, keepdims=True)
    p = jnp.exp(s - mx)
    l = jnp.sum(p, axis=-1, keepdims=True)
    o = jnp.einsum('uqk,ukd->uqd', (p / l).astype(BF16), v3, preferred_element_type=F32)
    lse = mx + jnp.log(l)
    for sb in range(nsb):
        rs = slice(sb * ATT_BLK, (sb + 1) * ATT_BLK)
        for h in range(B_HEADS):
            o_ref[0, 0, rs, h * B_HDIM:(h + 1) * B_HDIM] = o[sb * B_HEADS + h].astype(BF16)
        lse_ref[0, 0, rs, :] = _split3_lanes([lse[sb * B_HEADS + h] for h in range(B_HEADS)])


def _band_attention(q, k, v):
    b, d, ls, _ = q.shape
    tq = min(TOK_TILE, ls)
    nj = ls // tq
    ratio = tq // ATT_BLK
    cur = lambda bb, r, j: (bb, r, j, 0)
    prev = lambda bb, r, j: (bb, r, jnp.maximum(j * ratio - 1, 0), 0)
    return pl.pallas_call(
        _band_attn_kernel,
        grid=(b, d, nj),
        in_specs=[pl.BlockSpec((1, 1, tq, B_WIDTH), cur),
                  pl.BlockSpec((1, 1, tq, B_WIDTH), cur),
                  pl.BlockSpec((1, 1, ATT_BLK, B_WIDTH), prev),
                  pl.BlockSpec((1, 1, tq, B_WIDTH), cur),
                  pl.BlockSpec((1, 1, ATT_BLK, B_WIDTH), prev)],
        out_specs=[pl.BlockSpec((1, 1, tq, B_WIDTH), cur),
                   pl.BlockSpec((1, 1, tq, LANES), cur)],
        out_shape=[jax.ShapeDtypeStruct((b, d, ls, B_WIDTH), BF16),
                   jax.ShapeDtypeStruct((b, d, ls, LANES), BF16)],
        compiler_params=pltpu.CompilerParams(
            dimension_semantics=("arbitrary", "arbitrary", "arbitrary"), vmem_limit_bytes=VMEM_LIMIT),
        name="band_attention_d%d" % d,
    )(q, k, k, v, v)


def _unpermute(ref, d, tt, width):
    if d == 1:
        return ref[0, 0].astype(F32)
    sub = ATT_BLK if d == 4 else 2 * ATT_BLK
    per = sub // d
    pm_t = _perm_matrix(sub, d, transposed=True)
    outs = []
    for sblk in range(tt // sub):
        y = jnp.concatenate([ref[0, r, sblk * per:(sblk + 1) * per, :] for r in range(d)], axis=0)
        outs.append(_dot(pm_t, y))
    return jnp.concatenate(outs, axis=0)


def _l1c_kernel(x_ref, o0_ref, o1_ref, o2_ref, l0_ref, l1_ref, l2_ref, zg_ref, qm_ref, zm_ref,
                mkv_ref, wout_ref, gpost_ref, y_ref):
    tt = x_ref.shape[1]
    o_refs = (o0_ref, o1_ref, o2_ref)
    l_refs = (l0_ref, l1_ref, l2_ref)
    outs, lses = [], []
    for g, (_, d) in enumerate(B_GROUPS):
        outs.append(_unpermute(o_refs[g], d, tt, B_WIDTH))
        lt = _unpermute(l_refs[g], d, tt, LANES)
        lses.append(lt[:, 0:4] + lt[:, 4:8] + lt[:, 8:12])
    mx = jnp.maximum(jnp.maximum(lses[0], lses[1]), lses[2])
    es = [jnp.exp(l - mx) for l in lses]
    tot = es[0] + es[1] + es[2]
    ws = [e / tot for e in es]
    parts = []
    for h in range(B_HEADS):
        hs = slice(h * B_HDIM, (h + 1) * B_HDIM)
        acc = ws[0][:, h:h + 1] * outs[0][:, hs]
        acc = acc + ws[1][:, h:h + 1] * outs[1][:, hs]
        acc = acc + ws[2][:, h:h + 1] * outs[2][:, hs]
        parts.append(acc)
    ydil = jnp.concatenate(parts, axis=-1)
    ymix = (ydil * _silu(zg_ref[0].astype(F32))).astype(BF16)
    mk = mkv_ref[0, :, 0:M_WIDTH]
    mv = mkv_ref[0, :, M_WIDTH:2 * M_WIDTH]
    ym = (_mem_attention(qm_ref[0], mk, mv) * _silu(zm_ref[0].astype(F32))).astype(BF16)
    out = _dot(ymix, wout_ref[0:B_WIDTH, :]) + _dot(ym, wout_ref[B_WIDTH:B_WIDTH + M_WIDTH, :])
    y_ref[0] = x_ref[0] + _rms_scale(out) * gpost_ref[...]


def _layer1_out_prompt(x1, os_, ls_, zg, qm, zm, mkv_bf, w_out, g_post):
    b, s, _ = x1.shape
    tt = min(TOK_TILE, s)
    nt = s // tt
    tile = lambda bb, i: (bb, i, 0)
    perm = lambda bb, i: (bb, 0, i, 0)
    in_specs = [pl.BlockSpec((1, tt, D_MODEL), tile)]
    for width in (B_WIDTH, LANES):
        for (_, d) in B_GROUPS:
            in_specs.append(pl.BlockSpec((1, d, tt // d, width), perm))
    in_specs += [pl.BlockSpec((1, tt, B_WIDTH), tile)] * 3
    in_specs += [pl.BlockSpec((1, N_MEM, 2 * M_WIDTH), lambda bb, i: (bb, 0, 0)),
                 _const_spec(w_out.shape), _const_spec((1, D_MODEL))]
    return pl.pallas_call(
        _l1c_kernel,
        grid=(b, nt),
        in_specs=in_specs,
        out_specs=pl.BlockSpec((1, tt, D_MODEL), tile),
        out_shape=jax.ShapeDtypeStruct((b, s, D_MODEL), F32),
        compiler_params=pltpu.CompilerParams(
            dimension_semantics=("arbitrary", "arbitrary"), vmem_limit_bytes=VMEM_LIMIT),
        name="layer1_out_prompt",
    )(x1, *os_, *ls_, zg, qm, zm, mkv_bf, w_out, g_post)


def _rope_tables(pos):
    half = B_HDIM // 2
    inv = ROPE_THETA ** (-jnp.arange(half, dtype=F32) / half)
    ang = pos[:, None] * inv[None, :]
    cos = jnp.cos(ang)
    sin = jnp.sin(ang)
    return jnp.concatenate([cos, cos], axis=-1), jnp.concatenate([-sin, sin], axis=-1)


def _prompt_group(x_prompt, mem_prompt, p):
    b, s, _ = x_prompt.shape
    memkv_f, memkv_b = _memkv(mem_prompt.reshape(b * N_MEM, D_MODEL), p['w_mkv'])
    depth = memkv_f.shape[0]
    memkv_b = memkv_b.reshape(depth, b, N_MEM, 2 * M_WIDTH)
    x1, conv_p, c_p, n_p, m_pad = _layer0_prompt(
        x_prompt, p['g_pre'][0:1], p['w_in_a'][0], p['conv_w_a'][0], p['conv_b_a'], p['w_q_a'][0],
        p['w_k_a'][0], jnp.swapaxes(p['w_k_a'][0], 1, 2), p['w_v_a'][0], p['w_if_a'], p['b_if_a'],
        p['g_hn_a'], p['skip_a'], memkv_b[0], p['w_out_a'][0], p['g_post'][0:1])
    cos_t, sin_t = _rope_tables(jnp.arange(s, dtype=F32))
    outs = _layer1_proj_prompt(x1, p['g_kv'], p['g_pre'][1:2], p['w_kv_b'], p['w_in_b'][0], cos_t, sin_t)
    qs, ks, vs = outs[0:3], outs[3:6], outs[6:9]
    zg, qm, zm = outs[9:12]
    wins = outs[12:15]
    os_, ls_ = [], []
    for g in range(N_GROUPS):
        o, l = _band_attention(qs[g], ks[g], vs[g])
        os_.append(o)
        ls_.append(l)
    y = _layer1_out_prompt(x1, os_, ls_, zg, qm, zm, memkv_b[1], p['w_out_b'][0], p['g_post'][1:2])
    m_p = m_pad[:, 0:A_HEADS, 0][None]
    wins = list(wins)
    memkv_p = memkv_f.reshape(depth, b, N_MEM, 2, M_HEADS, M_HDIM)
    return y, conv_p, c_p, n_p, m_p, wins, memkv_p


def _prep_params(g_pre, g_post, w_in_a, conv_w_a, conv_b_a, w_q_a, w_k_a, w_v_a, w_if_a, b_if_a,
                 g_hn_a, skip_a, w_out_a, g_kv, w_kv_b, w_in_b, w_out_b, w_mkv):
    wif = jnp.pad(w_if_a[0], ((0, 0), (0, LANES - 2 * A_HEADS))).astype(BF16)
    bif = jnp.pad(b_if_a[0], (0, LANES - 2 * A_HEADS))[None, :]
    return {
        'g_pre': g_pre, 'g_post': g_post,
        'w_in_a': w_in_a.astype(BF16), 'conv_w_a': conv_w_a, 'conv_b_a': conv_b_a,
        'w_q_a': w_q_a.astype(BF16), 'w_k_a': w_k_a.astype(BF16), 'w_v_a': w_v_a.astype(BF16),
        'w_if_a': wif, 'b_if_a': bif, 'g_hn_a': g_hn_a, 'skip_a': skip_a,
        'w_out_a': w_out_a.astype(BF16), 'g_kv': g_kv[None, :], 'w_kv_b': w_kv_b.astype(BF16),
        'w_in_b': w_in_b.astype(BF16), 'w_out_b': w_out_b.astype(BF16), 'w_mkv': w_mkv.astype(BF16),
    }


def _dec_l0_proj_kernel(x_ref, gpre_ref, win_ref, cst_ref, convw_ref, convb_ref, wq_ref, wk_ref, wv_ref,
                        wif_ref, bif_ref,
                        q_ref, k_ref, v_ref, gates_ref, xc_ref, opre_ref, zg_ref, qm_ref, zm_ref, cnew_ref):
    h = (_rms_scale(x_ref[...]) * gpre_ref[...]).astype(BF16)
    u = _dot(h, win_ref[:, 0:A_INNER])
    opre_ref[...] = _dot(h, win_ref[:, A_INNER:2 * A_INNER])
    zg_ref[...] = _dot(h, win_ref[:, 2 * A_INNER:3 * A_INNER])
    qm_ref[...] = _dot(h, win_ref[:, 3 * A_INNER:3 * A_INNER + M_WIDTH])
    zm_ref[...] = _dot(h, win_ref[:, 3 * A_INNER + M_WIDTH:3 * A_INNER + 2 * M_WIDTH])
    cw = convw_ref[...]
    xc = convb_ref[...] + cst_ref[0] * cw[0:1, :]
    xc = xc + cst_ref[1] * cw[1:2, :]
    xc = xc + cst_ref[2] * cw[2:3, :]
    xc = xc + u * cw[3:4, :]
    xc = _silu(xc)
    xc_ref[...] = xc
    cnew_ref[0] = cst_ref[1]
    cnew_ref[1] = cst_ref[2]
    cnew_ref[2] = u
    qs, ks, vs, cat = [], [], [], []
    for hd in range(A_HEADS):
        sl = slice(hd * A_HDIM, (hd + 1) * A_HDIM)
        xh = xc[:, sl].astype(BF16)
        qh = _dot(xh, wq_ref[hd])
        kh = _dot(xh, wk_ref[hd]) * (A_HDIM ** -0.5)
        vh = _dot(u[:, sl].astype(BF16), wv_ref[hd])
        qs.append(qh)
        ks.append(kh)
        vs.append(vh)
        cat += [qh.astype(BF16), kh.astype(BF16), vh.astype(BF16)]
    q_ref[...] = jnp.concatenate(qs, axis=-1)
    k_ref[...] = jnp.concatenate(ks, axis=-1)
    v_ref[...] = jnp.concatenate(vs, axis=-1)
    gates_ref[...] = _dot(jnp.concatenate(cat, axis=-1), wif_ref[...]) + bif_ref[...]


def _whole(shape):
    nd = len(shape)
    return pl.BlockSpec(shape, lambda *_: (0,) * nd)


def _dec_l0_proj(x, g_pre, w_in, cst, conv_w, conv_b, wq, wk, wv, wif, bif):
    nb = x.shape[0]
    args = (x, g_pre, w_in, cst, conv_w, conv_b, wq, wk, wv, wif, bif)
    f = lambda *s: jax.ShapeDtypeStruct(s, F32)
    out_shape = [f(nb, A_INNER), f(nb, A_INNER), f(nb, A_INNER), f(nb, LANES), f(nb, A_INNER), f(nb, A_INNER),
                 f(nb, A_INNER), f(nb, M_WIDTH), f(nb, M_WIDTH), f(CONV_W - 1, nb, A_INNER)]
    return pl.pallas_call(
        _dec_l0_proj_kernel,
        grid=(1,),
        in_specs=[_whole(a.shape) for a in args],
        out_specs=[_whole(o.shape) for o in out_shape],
        out_shape=out_shape,
        compiler_params=pltpu.CompilerParams(dimension_semantics=("arbitrary",), vmem_limit_bytes=VMEM_LIMIT),
        name="dec_l0_proj",
    )(*args)


def _row_to_col(row, eye):
    return jnp.sum(jnp.where(eye, row, 0.0), axis=-1, keepdims=True)


def _col_to_row(colv, eye):
    return jnp.sum(jnp.where(eye, colv, 0.0), axis=0, keepdims=True)


def _dec_mem_attention(q, kv_ref_view):
    kk = kv_ref_view[:, 0]
    vv = kv_ref_view[:, 1]
    s = jnp.sum(kk * q[None], axis=-1, keepdims=True) * (M_HDIM ** -0.5)
    mx = jnp.max(s, axis=0, keepdims=True)
    p = jnp.exp(s - mx)
    p = p / jnp.sum(p, axis=0, keepdims=True)
    return jnp.sum(p * vv, axis=0)


def _dec_mlstm_kernel(q_ref, k_ref, v_ref, gates_ref, m_ref, c_ref, n_ref, qm_ref, kv_ref,
                      hs_ref, c_out, n_out, m_out, ym_ref):
    b = pl.program_id(0)
    rb = pl.ds(b, 1)
    g = gates_ref[rb, :]
    mrow = m_ref[rb, :]
    r = lax.broadcasted_iota(jnp.int32, (A_HDIM, A_HDIM), 0)
    c = lax.broadcasted_iota(jnp.int32, (A_HDIM, A_HDIM), 1)
    eye = r == c
    lane = lax.broadcasted_iota(jnp.int32, (1, LANES), 1)
    m_acc = jnp.zeros((1, LANES), F32)
    for h in range(A_HEADS):
        sl = slice(h * A_HDIM, (h + 1) * A_HDIM)
        qh = q_ref[rb, sl]
        kh = k_ref[rb, sl]
        vh = v_ref[rb, sl]
        c_old = c_ref[0, 0, h]
        n_old = n_ref[0, 0, h:h + 1, :]
        li = g[:, h:h + 1]
        lf = _log_sigmoid(g[:, 4 + h:5 + h])
        m_old = mrow[:, h:h + 1]
        cq = jnp.sum(c_old * qh, axis=-1, keepdims=True)
        nq = jnp.sum(n_old * qh, axis=-1, keepdims=True)
        qk = jnp.sum(qh * kh, axis=-1, keepdims=True)
        inter = lf + m_old
        m_new = jnp.maximum(inter, li)
        ws = jnp.exp(li - m_new)
        dec = jnp.exp(inter - m_new)
        sc = qk * ws
        v_col = _row_to_col(vh, eye)
        den = sc + dec * nq
        h_col = (sc * v_col + dec * cq) / jnp.maximum(jnp.abs(den), jnp.exp(-m_new))
        hs_ref[0, :, sl] = _col_to_row(h_col, eye)
        c_out[0, 0, h] = dec * c_old + (ws * v_col) * kh
        n_out[0, 0, h:h + 1, :] = dec * n_old + ws * kh
        m_acc = m_acc + jnp.where(lane == h, m_new, 0.0)
    m_out[0] = m_acc
    ym_ref[0] = _dec_mem_attention(qm_ref[0], kv_ref.at[0, 0])


def _dec_mlstm(q, k, v, gates, m_in, state_c, state_n, qm3, cache_mem_kv):
    nb = q.shape[0]
    per_b3 = lambda b: (b, 0, 0)
    in_specs = [_whole(q.shape), _whole(k.shape), _whole(v.shape), _whole(gates.shape), _whole(m_in.shape),
                pl.BlockSpec((1, 1, A_HEADS, A_HDIM, A_HDIM), lambda b: (0, b, 0, 0, 0)),
                pl.BlockSpec((1, 1, A_HEADS, A_HDIM), lambda b: (0, b, 0, 0)),
                pl.BlockSpec((1, M_HEADS, M_HDIM), per_b3),
                pl.BlockSpec((1, 1, N_MEM, 2, M_HEADS, M_HDIM), lambda b: (0, b, 0, 0, 0, 0))]
    out_specs = [pl.BlockSpec((1, 1, A_INNER), per_b3),
                 pl.BlockSpec((1, 1, A_HEADS, A_HDIM, A_HDIM), lambda b: (0, b, 0, 0, 0)),
                 pl.BlockSpec((1, 1, A_HEADS, A_HDIM), lambda b: (0, b, 0, 0)),
                 pl.BlockSpec((1, 1, LANES), per_b3),
                 pl.BlockSpec((1, M_HEADS, M_HDIM), per_b3)]
    out_shape = [jax.ShapeDtypeStruct((nb, 1, A_INNER), F32),
                 jax.ShapeDtypeStruct(state_c.shape, F32),
                 jax.ShapeDtypeStruct(state_n.shape, F32),
                 jax.ShapeDtypeStruct((nb, 1, LANES), F32),
                 jax.ShapeDtypeStruct((nb, M_HEADS, M_HDIM), F32)]
    return pl.pallas_call(
        _dec_mlstm_kernel,
        grid=(nb,),
        in_specs=in_specs,
        out_specs=out_specs,
        out_shape=out_shape,
        compiler_params=pltpu.CompilerParams(dimension_semantics=("arbitrary",), vmem_limit_bytes=VMEM_LIMIT),
        name="dec_mlstm",
    )(q, k, v, gates, m_in, state_c, state_n, qm3, cache_mem_kv)


def _dec_mid_kernel(hs_ref, opre_ref, xc_ref, zg_ref, ym_ref, zm_ref, x_ref, ghn_ref, skip_ref, wout_ref,
                    gpost_ref, gkv_ref, gpre_ref, wkv_ref, win_ref, cos_ref, sin_ref,
                    x1_ref, q_ref, k_ref, v_ref, zg1_ref, qm1_ref, zm1_ref):
    hh = _sigmoid(opre_ref[...]) * hs_ref[...]
    parts = []
    for h in range(A_HEADS):
        v = hh[:, h * A_HDIM:(h + 1) * A_HDIM]
        mu = jnp.mean(v, axis=-1, keepdims=True)
        var = jnp.mean(jnp.square(v - mu), axis=-1, keepdims=True)
        parts.append((v - mu) * lax.rsqrt(var + EPS))
    y = jnp.concatenate(parts, axis=-1) * ghn_ref[...] + skip_ref[...] * xc_ref[...]
    ymix = (y * _silu(zg_ref[...])).astype(BF16)
    ym = (ym_ref[...] * _silu(zm_ref[...])).astype(BF16)
    out = _dot(ymix, wout_ref[0:A_INNER, :]) + _dot(ym, wout_ref[A_INNER:A_INNER + M_WIDTH, :])
    x1 = x_ref[...] + _rms_scale(out) * gpost_ref[...]
    x1_ref[...] = x1
    xn = _rms_scale(x1)
    hk = (xn * gkv_ref[...]).astype(BF16)
    hq = (xn * gpre_ref[...]).astype(BF16)
    cos = cos_ref[...]
    sin = sin_ref[...]
    ks, vs = [], []
    for g in range(N_GROUPS):
        ks.append(_rope_cols(_dot(hk, wkv_ref[:, g * 2 * B_WIDTH:g * 2 * B_WIDTH + B_WIDTH]), cos, sin))
        vs.append(_dot(hk, wkv_ref[:, g * 2 * B_WIDTH + B_WIDTH:(g + 1) * 2 * B_WIDTH]))
    k_ref[...] = jnp.concatenate(ks, axis=-1)
    v_ref[...] = jnp.concatenate(vs, axis=-1)
    qoff = N_GROUPS * B_WIDTH
    q_ref[...] = _rope_cols(_dot(hq, win_ref[:, 0:qoff]), cos, sin)
    zg1_ref[...] = _dot(hq, win_ref[:, qoff:qoff + B_WIDTH])
    qm1_ref[...] = _dot(hq, win_ref[:, qoff + B_WIDTH:qoff + B_WIDTH + M_WIDTH])
    zm1_ref[...] = _dot(hq, win_ref[:, qoff + B_WIDTH + M_WIDTH:qoff + B_WIDTH + 2 * M_WIDTH])


def _dec_mid(hs, opre, xc, zg, ym, zm, x, ghn, skip, w_out, g_post, g_kv, g_pre, wkv, win, cos, sin):
    nb = x.shape[0]
    args = (hs, opre, xc, zg, ym, zm, x, ghn, skip, w_out, g_post, g_kv, g_pre, wkv, win, cos, sin)
    f = lambda *s: jax.ShapeDtypeStruct(s, F32)
    out_shape = [f(nb, D_MODEL), f(nb, N_GROUPS * B_WIDTH), f(nb, N_GROUPS * B_WIDTH), f(nb, N_GROUPS * B_WIDTH),
                 f(nb, B_WIDTH), f(nb, M_WIDTH), f(nb, M_WIDTH)]
    return pl.pallas_call(
        _dec_mid_kernel,
        grid=(1,),
        in_specs=[_whole(a.shape) for a in args],
        out_specs=[_whole(o.shape) for o in out_shape],
        out_shape=out_shape,
        compiler_params=pltpu.CompilerParams(dimension_semantics=("arbitrary",), vmem_limit_bytes=VMEM_LIMIT),
        name="dec_mid",
    )(*args)


def _dec_attn_kernel(q_ref, kn_ref, vn_ref, w0_ref, w1_ref, w2_ref, qm_ref, kv_ref, ydil_ref, ym_ref):
    w_refs = (w0_ref, w1_ref, w2_ref)
    scale = B_HDIM ** -0.5
    outs, lses = [], []
    for g in range(N_GROUPS):
        q = q_ref[0, g]
        kn = kn_ref[0, g]
        vn = vn_ref[0, g]
        wv = w_refs[g].at[0]
        kk = wv[:, 0]
        vv = wv[:, 1]
        s_c = jnp.sum(kk * q[None], axis=-1, keepdims=True) * scale
        s_n = jnp.sum(kn * q, axis=-1, keepdims=True) * scale
        mx = jnp.maximum(jnp.max(s_c, axis=0), s_n)
        p_c = jnp.exp(s_c - mx[None])
        p_n = jnp.exp(s_n - mx)
        l = jnp.sum(p_c, axis=0) + p_n
        outs.append(jnp.sum((p_c / l[None]) * vv, axis=0) + (p_n / l) * vn)
        lses.append(mx + jnp.log(l))
    mx = jnp.maximum(jnp.maximum(lses[0], lses[1]), lses[2])
    es = [jnp.exp(l - mx) for l in lses]
    tot = es[0] + es[1] + es[2]
    ydil_ref[0] = (es[0] / tot) * outs[0] + (es[1] / tot) * outs[1] + (es[2] / tot) * outs[2]
    ym_ref[0] = _dec_mem_attention(qm_ref[0], kv_ref.at[0, 0])


def _dec_attn(q4, kn4, vn4, cw0, cw1, cw2, qm3, cache_mem_kv, layer):
    nb = q4.shape[0]
    per_b3 = lambda b: (b, 0, 0)
    per_b4 = lambda b: (b, 0, 0, 0)
    rows = B_GROUPS[0][0]
    win_specs = [pl.BlockSpec((1, rows, 2, B_HEADS, B_HDIM), lambda b: (b, 0, 0, 0, 0))]
    for cw in (cw1, cw2):
        win_specs.append(pl.BlockSpec((1, rows, None, 2, B_HEADS, B_HDIM), lambda b: (b, 0, 0, 0, 0, 0)))
    in_specs = [pl.BlockSpec((1, N_GROUPS, B_HEADS, B_HDIM), per_b4)] * 3 + win_specs + [
        pl.BlockSpec((1, M_HEADS, M_HDIM), per_b3),
        pl.BlockSpec((1, 1, N_MEM, 2, M_HEADS, M_HDIM), lambda b: (layer, b, 0, 0, 0, 0))]
    return pl.pallas_call(
        _dec_attn_kernel,
        grid=(nb,),
        in_specs=in_specs,
        out_specs=[pl.BlockSpec((1, B_HEADS, B_HDIM), per_b3), pl.BlockSpec((1, M_HEADS, M_HDIM), per_b3)],
        out_shape=[jax.ShapeDtypeStruct((nb, B_HEADS, B_HDIM), F32),
                   jax.ShapeDtypeStruct((nb, M_HEADS, M_HDIM), F32)],
        compiler_params=pltpu.CompilerParams(dimension_semantics=("arbitrary",), vmem_limit_bytes=VMEM_LIMIT),
        name="dec_attn",
    )(q4, kn4, vn4, cw0, cw1, cw2, qm3, cache_mem_kv)


def _dec_out_kernel(ydil_ref, zg_ref, ym_ref, zm_ref, x_ref, wout_ref, gpost_ref, y_ref):
    ymix = (ydil_ref[...] * _silu(zg_ref[...])).astype(BF16)
    ym = (ym_ref[...] * _silu(zm_ref[...])).astype(BF16)
    out = _dot(ymix, wout_ref[0:B_WIDTH, :]) + _dot(ym, wout_ref[B_WIDTH:B_WIDTH + M_WIDTH, :])
    y_ref[...] = x_ref[...] + _rms_scale(out) * gpost_ref[...]


def _dec_out(ydil, zg, ym, zm, x1, w_out, g_post):
    args = (ydil, zg, ym, zm, x1, w_out, g_post)
    return pl.pallas_call(
        _dec_out_kernel,
        grid=(1,),
        in_specs=[_whole(a.shape) for a in args],
        out_specs=_whole(x1.shape),
        out_shape=jax.ShapeDtypeStruct(x1.shape, F32),
        compiler_params=pltpu.CompilerParams(dimension_semantics=("arbitrary",), vmem_limit_bytes=VMEM_LIMIT),
        name="dec_out",
    )(*args)


def _sample_group(x_sample, state_conv, state_c, state_n, state_m, cache_wins, cache_mem_kv, p):
    nb = x_sample.shape[0]
    x = x_sample.reshape(nb, D_MODEL)
    cst = state_conv[0].transpose(1, 0, 2)
    q, k, v, gates, xc, opre, zg, qm, zm, cnew = _dec_l0_proj(
        x, p['g_pre'][0:1], p['w_in_a'][0], cst, p['conv_w_a'][0], p['conv_b_a'], p['w_q_a'][0],
        p['w_k_a'][0], p['w_v_a'][0], p['w_if_a'], p['b_if_a'])
    m_in = jnp.pad(state_m[0], ((0, 0), (0, LANES - A_HEADS)))
    hs, c_s, n_s, m_pad, ym0 = _dec_mlstm(q, k, v, gates, m_in, state_c, state_n,
                                          qm.reshape(nb, M_HEADS, M_HDIM), cache_mem_kv)
    pos = PAST_LEN + jnp.arange(1, dtype=F32)
    cos, sin = _rope_tables(pos)
    x1, qd, kn, vn, zg1, qm1, zm1 = _dec_mid(
        hs.reshape(nb, A_INNER), opre, xc, zg, ym0.reshape(nb, M_WIDTH), zm, x, p['g_hn_a'], p['skip_a'],
        p['w_out_a'][0], p['g_post'][0:1], p['g_kv'], p['g_pre'][1:2], p['w_kv_b'], p['w_in_b'][0], cos, sin)
    shp4 = (nb, N_GROUPS, B_HEADS, B_HDIM)
    kn4 = kn.reshape(shp4)
    vn4 = vn.reshape(shp4)
    cws = [cache_wins[0]]
    for g in (1, 2):
        w, d = B_GROUPS[g]
        cws.append(cache_wins[g].reshape(nb, w // d, d, 2, B_HEADS, B_HDIM))
    ydil, ym1 = _dec_attn(qd.reshape(shp4), kn4, vn4, cws[0], cws[1], cws[2],
                          qm1.reshape(nb, M_HEADS, M_HDIM), cache_mem_kv, 1)
    y = _dec_out(ydil.reshape(nb, B_WIDTH), zg1, ym1.reshape(nb, M_WIDTH), zm1, x1, p['w_out_b'][0],
                 p['g_post'][1:2])
    conv_s = cnew.transpose(1, 0, 2)[None]
    m_s = m_pad[:, 0, 0:A_HEADS][None]
    wins_s = [jnp.stack([kn4[:, g], vn4[:, g]], axis=1)[:, None] for g in range(N_GROUPS)]
    return y.reshape(nb, 1, D_MODEL), conv_s, c_s, n_s, m_s, wins_s


def kernel(x_prompt, x_sample, mem_prompt, state_conv, state_C, state_n, state_m, cache_win0, cache_win1,
           cache_win2, cache_mem_kv, g_pre, g_post, w_in_a, conv_w_a, conv_b_a, w_q_a, w_k_a, w_v_a, w_if_a,
           b_if_a, g_hn_a, skip_a, w_out_a, g_kv, w_kv_b, w_in_b, w_out_b, w_mkv):
    p = _prep_params(g_pre, g_post, w_in_a, conv_w_a, conv_b_a, w_q_a, w_k_a, w_v_a, w_if_a, b_if_a,
                     g_hn_a, skip_a, w_out_a, g_kv, w_kv_b, w_in_b, w_out_b, w_mkv)
    y_p, conv_p, c_p, n_p, m_p, wins_p, memkv_p = _prompt_group(x_prompt, mem_prompt, p)
    y_s, conv_s, c_s, n_s, m_s, wins_s = _sample_group(
        x_sample, state_conv, state_C, state_n, state_m, (cache_win0, cache_win1, cache_win2), cache_mem_kv, p)
    return (y_p, y_s, conv_p, c_p, n_p, m_p, wins_p[0], wins_p[1], wins_p[2], memkv_p,
            conv_s, c_s, n_s, m_s, wins_s[0], wins_s[1], wins_s[2])
```

```python
import functools

import jax
import jax.numpy as jnp
from jax import lax
from jax.experimental import pallas as pl
from jax.experimental.pallas import tpu as pltpu

F32 = jnp.float32
BF16 = jnp.bfloat16

D_MODEL = 1024
A_HEADS = 4
A_HDIM = 256
A_INNER = 1024
CONV_W = 4
A_CHUNK = 128
B_GROUPS = ((128, 1), (512, 4), (2048, 16))
N_GROUPS = 3
B_HEADS = 4
B_HDIM = 128
B_WIDTH = 512
N_MEM = 256
M_HEADS = 4
M_HDIM = 128
M_WIDTH = 512
ROPE_THETA = 10000.0
EPS = 1e-6
PAST_LEN = 8192

LANES = 128
TOK_TILE = 512
ATT_BLK = 128
VMEM_LIMIT = 56 * 1024 * 1024

NT_DIMS = (((1,), (1,)), ((), ()))


def _dot(a, b):
    return jnp.dot(a, b, preferred_element_type=F32)


def _dot_nt(a, b):
    return lax.dot_general(a, b, NT_DIMS, preferred_element_type=F32)


def _sigmoid(x):
    return 1.0 / (1.0 + jnp.exp(-x))


def _silu(x):
    return x * _sigmoid(x)


def _log_sigmoid(x):
    return jnp.minimum(x, 0.0) - jnp.log(1.0 + jnp.exp(-jnp.abs(x)))


def _rms_scale(x):
    return x * lax.rsqrt(jnp.mean(x * x, axis=-1, keepdims=True) + EPS)


def _const_spec(shape):
    nd = len(shape)
    return pl.BlockSpec(shape, lambda *_: (0,) * nd, pipeline_mode=pl.Buffered(1))


def _mem_attention(qm, mk, mv):
    heads = range(M_HEADS)
    sl = [slice(h * M_HDIM, (h + 1) * M_HDIM) for h in heads]
    s = [_dot_nt(qm[:, sl[h]], mk[:, sl[h]]) * (M_HDIM ** -0.5) for h in heads]
    mx = [jnp.max(s[h], axis=-1, keepdims=True) for h in heads]
    p = [jnp.exp(s[h] - mx[h]) for h in heads]
    l = [jnp.sum(p[h], axis=-1, keepdims=True) for h in heads]
    outs = [_dot((p[h] / l[h]).astype(BF16), mv[:, sl[h]]) for h in heads]
    return jnp.concatenate(outs, axis=-1)


def _rows_to_kv_heads(k, v):
    pieces = [a[:, h * LANES:(h + 1) * LANES] for a in (k, v) for h in range(a.shape[1] // LANES)]
    return pltpu.einshape("jrl->rjl", jnp.stack(pieces))


def _memkv_kernel(m_ref, w_ref, o_ref, ob_ref):
    r = _dot(m_ref[...].astype(BF16), w_ref[0])
    o_ref[0] = _rows_to_kv_heads(r[:, 0:M_WIDTH], r[:, M_WIDTH:2 * M_WIDTH])
    ob_ref[0] = r.astype(BF16)


def _memkv(mem2d, w_bf):
    nm = mem2d.shape[0]
    nl = w_bf.shape[0]
    tm = min(512, nm)
    return pl.pallas_call(
        _memkv_kernel,
        grid=(nl, nm // tm),
        in_specs=[pl.BlockSpec((tm, D_MODEL), lambda l, i: (i, 0)),
                  pl.BlockSpec((1, D_MODEL, 2 * M_WIDTH), lambda l, i: (l, 0, 0))],
        out_specs=[pl.BlockSpec((1, tm, 2 * M_HEADS, M_HDIM), lambda l, i: (l, i, 0, 0)),
                   pl.BlockSpec((1, tm, 2 * M_WIDTH), lambda l, i: (l, i, 0))],
        out_shape=[jax.ShapeDtypeStruct((nl, nm, 2 * M_HEADS, M_HDIM), F32),
                   jax.ShapeDtypeStruct((nl, nm, 2 * M_WIDTH), BF16)],
        compiler_params=pltpu.CompilerParams(dimension_semantics=("arbitrary", "arbitrary")),
        name="memkv",
    )(mem2d, w_bf)


def _l0_kernel(x_ref, gpre_ref, win_ref, convw_ref, convb_ref, wq_ref, wk_ref, wkt_ref, wv_ref,
               wif_ref, bif_ref, ghn_ref, skip_ref, mkv_ref, wout_ref, gpost_ref,
               x1_ref, conv_out, c_out, n_out, m_out,
               h_s, u_s, xc_s, opre_s, zg_s, qm_s, zm_s, qkv_s, kt_s, gates_s, ymix_s,
               c_s, n_s, m_s):
    tt = x_ref.shape[1]
    nsub = tt // A_CHUNK
    i = pl.program_id(1)
    nt = pl.num_programs(1)

    @pl.when(i == 0)
    def _():
        u_s[0:8, :] = jnp.zeros((8, A_INNER), F32)
        c_s[...] = jnp.zeros(c_s.shape, F32)
        n_s[...] = jnp.zeros(n_s.shape, F32)
        m_s[...] = jnp.zeros(m_s.shape, F32)

    gpre = gpre_ref[...]

    def norm_body(c, _):
        r = pl.ds(pl.multiple_of(c * A_CHUNK, A_CHUNK), A_CHUNK)
        h_s[r, :] = (_rms_scale(x_ref[0, r, :]) * gpre).astype(BF16)
        return 0
    lax.fori_loop(0, nsub, norm_body, 0)

    hb = h_s[...]
    u_s[8:8 + tt, :] = _dot(hb, win_ref[:, 0:A_INNER])
    opre_s[...] = _dot(hb, win_ref[:, A_INNER:2 * A_INNER])
    zg_s[...] = _dot(hb, win_ref[:, 2 * A_INNER:3 * A_INNER])
    qm_s[...] = _dot(hb, win_ref[:, 3 * A_INNER:3 * A_INNER + M_WIDTH]).astype(BF16)
    zm_s[...] = _dot(hb, win_ref[:, 3 * A_INNER + M_WIDTH:3 * A_INNER + 2 * M_WIDTH])

    cw = convw_ref[...]
    cb = convb_ref[...]

    for c in range(nsub):
        r0 = c * A_CHUNK
        xc = cb + u_s[r0 + 5:r0 + 5 + A_CHUNK, :] * cw[0:1, :]
        xc = xc + u_s[r0 + 6:r0 + 6 + A_CHUNK, :] * cw[1:2, :]
        xc = xc + u_s[r0 + 7:r0 + 7 + A_CHUNK, :] * cw[2:3, :]
        xc = xc + u_s[r0 + 8:r0 + 8 + A_CHUNK, :] * cw[3:4, :]
        xc_s[r0:r0 + A_CHUNK, :] = _silu(xc)

    for h in range(A_HEADS):
        sl = slice(h * A_HDIM, (h + 1) * A_HDIM)
        xh = xc_s[:, sl].astype(BF16)
        uh = u_s[8:8 + tt, sl].astype(BF16)
        base = h * 3 * A_HDIM
        qkv_s[:, base:base + A_HDIM] = _dot(xh, wq_ref[h]).astype(BF16)
        qkv_s[:, base + A_HDIM:base + 2 * A_HDIM] = (_dot(xh, wk_ref[h]) * (A_HDIM ** -0.5)).astype(BF16)
        qkv_s[:, base + 2 * A_HDIM:base + 3 * A_HDIM] = _dot(uh, wv_ref[h]).astype(BF16)
        kt = (_dot_nt(wkt_ref[h], xh) * (A_HDIM ** -0.5)).astype(BF16)
        for c in range(nsub):
            kt_s[h, c] = kt[:, c * A_CHUNK:(c + 1) * A_CHUNK]
    gates_s[...] = _dot(qkv_s[...], wif_ref[...]) + bif_ref[...]

    row = lax.broadcasted_iota(jnp.int32, (A_CHUNK, A_CHUNK), 0)
    col = lax.broadcasted_iota(jnp.int32, (A_CHUNK, A_CHUNK), 1)
    causal = col <= row
    tri = jnp.where(causal, 1.0, 0.0).astype(BF16)

    def chunk_body(c, _):
        r0 = pl.multiple_of(c * A_CHUNK, A_CHUNK)
        rs = pl.ds(r0, A_CHUNK)
        g = gates_s[rs, :]
        ls = _log_sigmoid(g)
        t0 = ls.astype(BF16)
        e1 = ls - t0.astype(F32)
        t1 = e1.astype(BF16)
        t2 = (e1 - t1.astype(F32)).astype(BF16)
        bc = _dot(tri, t0) + _dot(tri, t1) + _dot(tri, t2)
        lane = lax.broadcasted_iota(jnp.int32, (A_CHUNK, LANES), 1)
        xt = jnp.where(lane < A_HEADS, g, bc).T
        heads = range(A_HEADS)
        b_col = [bc[:, 4 + h:5 + h] for h in heads]
        b_row = [xt[4 + h:5 + h, :] for h in heads]
        li_row = [xt[h:h + 1, :] for h in heads]
        li_col = [g[:, h:h + 1] for h in heads]
        m_old = [m_s[h:h + 1, 0:1] for h in heads]
        b_last = [bc[A_CHUNK - 1:A_CHUNK, 4 + h:5 + h] for h in heads]
        qh = [qkv_s[rs, h * 3 * A_HDIM:h * 3 * A_HDIM + A_HDIM] for h in heads]
        kh = [qkv_s[rs, h * 3 * A_HDIM + A_HDIM:h * 3 * A_HDIM + 2 * A_HDIM] for h in heads]
        vh = [qkv_s[rs, h * 3 * A_HDIM + 2 * A_HDIM:(h + 1) * 3 * A_HDIM] for h in heads]
        kt = [kt_s[h, c] for h in heads]
        c_old = [c_s[h] for h in heads]
        n_old = [n_s[h:h + 1, :] for h in heads]
        qk = [_dot_nt(qh[h], kh[h]) for h in heads]
        qc = [_dot(qh[h], c_old[h].astype(BF16)) for h in heads]
        dm = [jnp.where(causal, b_col[h] - b_row[h] + li_row[h], -jnp.inf) for h in heads]
        inter = [b_col[h] + m_old[h] for h in heads]
        m_row = [jnp.maximum(inter[h], jnp.max(dm[h], axis=-1, keepdims=True)) for h in heads]
        g_max = [jnp.max(b_last[h] - b_row[h] + li_row[h], axis=-1, keepdims=True) for h in heads]
        m_new = [jnp.maximum(b_last[h] + m_old[h], g_max[h]) for h in heads]
        sc = [qk[h] * jnp.exp(dm[h] - m_row[h]) for h in heads]
        dec = [jnp.exp(inter[h] - m_row[h]) for h in heads]
        ws_col = [jnp.exp(b_last[h] - b_col[h] + li_col[h] - m_new[h]) for h in heads]
        dc = [jnp.exp(b_last[h] + m_old[h] - m_new[h]) for h in heads]
        sv = [_dot(sc[h].astype(BF16), vh[h]) for h in heads]
        wv = [(ws_col[h] * vh[h].astype(F32)).astype(BF16) for h in heads]
        upd = [_dot(kt[h], wv[h]) for h in heads]
        hs = []
        for h in heads:
            den = (jnp.sum(sc[h], axis=-1, keepdims=True)
                   + dec[h] * jnp.sum(qh[h].astype(F32) * n_old[h], axis=-1, keepdims=True))
            num = sv[h] + dec[h] * qc[h]
            hs.append(num / jnp.maximum(jnp.abs(den), jnp.exp(-m_row[h])))
        for h in heads:
            c_s[h] = dc[h] * c_old[h] + upd[h]
            n_s[h:h + 1, :] = dc[h] * n_old[h] + jnp.sum(ws_col[h] * kh[h].astype(F32), axis=0, keepdims=True)
            m_s[h:h + 1, :] = jnp.broadcast_to(m_new[h], (1, LANES))

        parts = []
        for h in heads:
            v = _sigmoid(opre_s[rs, h * A_HDIM:(h + 1) * A_HDIM]) * hs[h]
            mu = jnp.mean(v, axis=-1, keepdims=True)
            var = jnp.mean(jnp.square(v - mu), axis=-1, keepdims=True)
            parts.append((v - mu) * lax.rsqrt(var + EPS))
        hn = jnp.concatenate(parts, axis=-1) * ghn
        y = hn + skp * xc_s[rs, :]
        ymix_s[rs, 0:A_INNER] = (y * _silu(zg_s[rs, :])).astype(BF16)
        ym = _mem_attention(qm_s[rs, :], mk, mv) * _silu(zm_s[rs, :])
        ymix_s[rs, A_INNER:A_INNER + M_WIDTH] = ym.astype(BF16)
        return 0

    ghn = ghn_ref[...]
    skp = skip_ref[...]
    mk = mkv_ref[0, :, 0:M_WIDTH]
    mv = mkv_ref[0, :, M_WIDTH:2 * M_WIDTH]
    lax.fori_loop(0, nsub, chunk_body, 0)

    out = _dot(ymix_s[...], wout_ref[...])
    x1_ref[0] = x_ref[0] + _rms_scale(out) * gpost_ref[...]

    u_s[0:8, :] = u_s[tt:tt + 8, :]

    @pl.when(i == nt - 1)
    def _():
        conv_out[0, 0] = u_s[tt + 5:tt + 8, :]
        for h in range(A_HEADS):
            c_out[0, 0, h] = c_s[h].T
        n_out[0, 0] = n_s[0:A_HEADS, :]
        m_out[0] = m_s[...]


def _layer0_prompt(x, g_pre, w_in, conv_w, conv_b, wq, wk, wkt, wv, wif, bif, ghn, skip, mkv_bf, w_out,
                   g_post):
    b, s, _ = x.shape
    tt = min(TOK_TILE, s)
    nt = s // tt
    a_in = w_in.shape[1]
    tile = lambda bb, i: (bb, i, 0)
    per_b = lambda bb, i: (bb, 0, 0)
    in_specs = [
        pl.BlockSpec((1, tt, D_MODEL), tile),
        _const_spec((1, D_MODEL)),
        _const_spec((D_MODEL, a_in)),
        _const_spec((CONV_W, A_INNER)),
        _const_spec((1, A_INNER)),
        _const_spec((A_HEADS, A_HDIM, A_HDIM)),
        _const_spec((A_HEADS, A_HDIM, A_HDIM)),
        _const_spec((A_HEADS, A_HDIM, A_HDIM)),
        _const_spec((A_HEADS, A_HDIM, A_HDIM)),
        _const_spec((3 * A_INNER, LANES)),
        _const_spec((1, LANES)),
        _const_spec((1, A_INNER)),
        _const_spec((1, A_INNER)),
        pl.BlockSpec((1, N_MEM, 2 * M_WIDTH), per_b),
        _const_spec((A_INNER + M_WIDTH, D_MODEL)),
        _const_spec((1, D_MODEL)),
    ]
    out_specs = [
        pl.BlockSpec((1, tt, D_MODEL), tile),
        pl.BlockSpec((1, 1, CONV_W - 1, A_INNER), lambda bb, i: (0, bb, 0, 0)),
        pl.BlockSpec((1, 1, A_HEADS, A_HDIM, A_HDIM), lambda bb, i: (0, bb, 0, 0, 0)),
        pl.BlockSpec((1, 1, A_HEADS, A_HDIM), lambda bb, i: (0, bb, 0, 0)),
        pl.BlockSpec((1, 8, LANES), per_b),
    ]
    out_shape = [
        jax.ShapeDtypeStruct((b, s, D_MODEL), F32),
        jax.ShapeDtypeStruct((1, b, CONV_W - 1, A_INNER), F32),
        jax.ShapeDtypeStruct((1, b, A_HEADS, A_HDIM, A_HDIM), F32),
        jax.ShapeDtypeStruct((1, b, A_HEADS, A_HDIM), F32),
        jax.ShapeDtypeStruct((b, 8, LANES), F32),
    ]
    scratch = [
        pltpu.VMEM((tt, D_MODEL), BF16),
        pltpu.VMEM((tt + 8, A_INNER), F32),
        pltpu.VMEM((tt, A_INNER), F32),
        pltpu.VMEM((tt, A_INNER), F32),
        pltpu.VMEM((tt, A_INNER), F32),
        pltpu.VMEM((tt, M_WIDTH), BF16),
        pltpu.VMEM((tt, M_WIDTH), F32),
        pltpu.VMEM((tt, 3 * A_INNER), BF16),
        pltpu.VMEM((A_HEADS, tt // A_CHUNK, A_HDIM, A_CHUNK), BF16),
        pltpu.VMEM((tt, LANES), F32),
        pltpu.VMEM((tt, A_INNER + M_WIDTH), BF16),
        pltpu.VMEM((A_HEADS, A_HDIM, A_HDIM), F32),
        pltpu.VMEM((8, A_HDIM), F32),
        pltpu.VMEM((8, LANES), F32),
    ]
    return pl.pallas_call(
        _l0_kernel,
        grid=(b, nt),
        in_specs=in_specs,
        out_specs=out_specs,
        out_shape=out_shape,
        scratch_shapes=scratch,
        compiler_params=pltpu.CompilerParams(
            dimension_semantics=("arbitrary", "arbitrary"), vmem_limit_bytes=VMEM_LIMIT),
        name="layer0_prompt",
    )(x, g_pre, w_in, conv_w, conv_b, wq, wk, wkt, wv, wif, bif, ghn, skip, mkv_bf, w_out, g_post)


def _rope_cols(x, cos, sin_signed):
    outs = []
    for cblk in range(x.shape[1] // B_HDIM):
        xb = x[:, cblk * B_HDIM:(cblk + 1) * B_HDIM]
        outs.append(xb * cos + pltpu.roll(xb, B_HDIM // 2, 1) * sin_signed)
    return jnp.concatenate(outs, axis=-1)


def _l1a_kernel(x_ref, gkv_ref, gpre_ref, wkv_ref, win_ref, cos_ref, sin_ref,
                q0_ref, q1_ref, q2_ref, k0_ref, k1_ref, k2_ref, v0_ref, v1_ref, v2_ref,
                zg_ref, qm_ref, zm_ref, w0_ref, w1_ref, w2_ref):
    tt = x_ref.shape[1]
    x = x_ref[0]
    xn = _rms_scale(x)
    hk = (xn * gkv_ref[...]).astype(BF16)
    hq = (xn * gpre_ref[...]).astype(BF16)
    cos = cos_ref[...]
    sin = sin_ref[...]
    q_refs = (q0_ref, q1_ref, q2_ref)
    k_refs = (k0_ref, k1_ref, k2_ref)
    v_refs = (v0_ref, v1_ref, v2_ref)
    w_refs = (w0_ref, w1_ref, w2_ref)
    for g in (2, 1, 0):
        d = B_GROUPS[g][1]
        kf = _rope_cols(_dot(hk, wkv_ref[:, g * 2 * B_WIDTH:g * 2 * B_WIDTH + B_WIDTH]), cos, sin)
        vf = _dot(hk, wkv_ref[:, g * 2 * B_WIDTH + B_WIDTH:(g + 1) * 2 * B_WIDTH])
        qf = _rope_cols(_dot(hq, win_ref[:, g * B_WIDTH:(g + 1) * B_WIDTH]), cos, sin)
        wr = w_refs[g]
        wrows = wr.shape[1]
        wr[0] = _rows_to_kv_heads(kf[tt - wrows:, :], vf[tt - wrows:, :])
        for val, ref in ((qf.astype(BF16), q_refs[g]), (kf.astype(BF16), k_refs[g]), (vf.astype(BF16), v_refs[g])):
            ref[0] = pltpu.einshape("(jr)l->rjl", val, r=d)
    qoff = N_GROUPS * B_WIDTH
    zg_ref[0] = _dot(hq, win_ref[:, qoff:qoff + B_WIDTH]).astype(BF16)
    qm_ref[0] = _dot(hq, win_ref[:, qoff + B_WIDTH:qoff + B_WIDTH + M_WIDTH]).astype(BF16)
    zm_ref[0] = _dot(hq, win_ref[:, qoff + B_WIDTH + M_WIDTH:qoff + B_WIDTH + 2 * M_WIDTH]).astype(BF16)


def _layer1_proj_prompt(x1, g_kv, g_pre, wkv, win, cos_t, sin_t):
    b, s, _ = x1.shape
    tt = min(TOK_TILE, s)
    nt = s // tt
    tile = lambda bb, i: (bb, i, 0)
    in_specs = [
        pl.BlockSpec((1, tt, D_MODEL), tile),
        _const_spec((1, D_MODEL)),
        _const_spec((1, D_MODEL)),
        _const_spec(wkv.shape),
        _const_spec(win.shape),
        pl.BlockSpec((tt, B_HDIM), lambda bb, i: (i, 0)),
        pl.BlockSpec((tt, B_HDIM), lambda bb, i: (i, 0)),
    ]
    qkv_specs, qkv_shapes = [], []
    for _ in range(3):
        for (_, d) in B_GROUPS:
            qkv_specs.append(pl.BlockSpec((1, d, tt // d, B_WIDTH), lambda bb, i: (bb, 0, i, 0)))
            qkv_shapes.append(jax.ShapeDtypeStruct((b, d, s // d, B_WIDTH), BF16))
    gate_specs = [pl.BlockSpec((1, tt, B_WIDTH), tile)] * 3
    gate_shapes = [jax.ShapeDtypeStruct((b, s, B_WIDTH), BF16)] * 3
    win_specs, win_shapes = [], []
    for (w, _) in B_GROUPS:
        wr = min(w, s)
        rows = min(wr, tt)
        nblk = wr // rows
        win_specs.append(pl.BlockSpec(
            (1, rows, 2 * B_HEADS, B_HDIM),
            functools.partial(lambda bb, i, nb: (bb, jnp.maximum(i - (nt - nb), 0), 0, 0), nb=nblk)))
        win_shapes.append(jax.ShapeDtypeStruct((b, wr, 2 * B_HEADS, B_HDIM), F32))
    return pl.pallas_call(
        _l1a_kernel,
        grid=(b, nt),
        in_specs=in_specs,
        out_specs=qkv_specs + gate_specs + win_specs,
        out_shape=qkv_shapes + gate_shapes + win_shapes,
        compiler_params=pltpu.CompilerParams(
            dimension_semantics=("arbitrary", "arbitrary"), vmem_limit_bytes=VMEM_LIMIT),
        name="layer1_proj_prompt",
    )(x1, g_kv, g_pre, wkv, win, cos_t, sin_t)


def _cols_to_lanes(cols):
    t = cols[0].shape[0]
    lane = lax.broadcasted_iota(jnp.int32, (t, LANES), 1)
    acc = jnp.zeros((t, LANES), F32)
    for h, cvec in enumerate(cols):
        acc = jnp.where(lane == h, cvec, acc)
    return acc


def _band_attn_kernel(q_ref, kc_ref, kp_ref, vc_ref, vp_ref, o_ref, lse_ref):
    tq = q_ref.shape[2]
    nsb = tq // ATT_BLK
    j = pl.program_id(2)
    row = lax.broadcasted_iota(jnp.int32, (ATT_BLK, 2 * ATT_BLK), 0)
    col = lax.broadcasted_iota(jnp.int32, (ATT_BLK, 2 * ATT_BLK), 1)
    band = jnp.logical_and(col >= row, col <= row + ATT_BLK)
    first_pen = jnp.where(col < ATT_BLK, jnp.where(j > 0, 0.0, -jnp.inf), 0.0)
    scale = B_HDIM ** -0.5
    qs, ks, vs = [], [], []
    for sb in range(nsb):
        rs = slice(sb * ATT_BLK, (sb + 1) * ATT_BLK)
        ps = slice((sb - 1) * ATT_BLK, sb * ATT_BLK)
        for h in range(B_HEADS):
            hs = slice(h * B_HDIM, (h + 1) * B_HDIM)
            qs.append(q_ref[0, 0, rs, hs])
            kp = kp_ref[0, 0, :, hs] if sb == 0 else kc_ref[0, 0, ps, hs]
            vp = vp_ref[0, 0, :, hs] if sb == 0 else vc_ref[0, 0, ps, hs]
            ks.append(jnp.concatenate([kp, kc_ref[0, 0, rs, hs]], axis=0))
            vs.append(jnp.concatenate([vp, vc_ref[0, 0, rs, hs]], axis=0))
    q3 = jnp.stack(qs)
    k3 = jnp.stack(ks)
    v3 = jnp.stack(vs)
    s = jnp.einsum('uqd,ukd->uqk', q3, k3, preferred_element_type=F32) * scale
    s = jnp.concatenate([s[0:B_HEADS] + first_pen[None], s[B_HEADS:]], axis=0)
    s = jnp.where(band[None], s, -jnp.inf)
    mx = jnp.max(s, axis=-1, keepdims=True)
    p = jnp.exp(s - mx)
    l = jnp.sum(p, axis=-1, keepdims=True)
    o = jnp.einsum('uqk,ukd->uqd', (p / l).astype(BF16), v3, preferred_element_type=F32)
    lse = mx + jnp.log(l)
    for sb in range(nsb):
        rs = slice(sb * ATT_BLK, (sb + 1) * ATT_BLK)
        for h in range(B_HEADS):
            o_ref[0, 0, rs, h * B_HDIM:(h + 1) * B_HDIM] = o[sb * B_HEADS + h].astype(BF16)
        lse_ref[0, 0, rs, :] = _cols_to_lanes([lse[sb * B_HEADS + h] for h in range(B_HEADS)])


def _band_attention(q, k, v):
    b, d, ls, _ = q.shape
    tq = min(TOK_TILE, ls)
    nj = ls // tq
    ratio = tq // ATT_BLK
    cur = lambda bb, r, j: (bb, r, j, 0)
    prev = lambda bb, r, j: (bb, r, jnp.maximum(j * ratio - 1, 0), 0)
    return pl.pallas_call(
        _band_attn_kernel,
        grid=(b, d, nj),
        in_specs=[pl.BlockSpec((1, 1, tq, B_WIDTH), cur),
                  pl.BlockSpec((1, 1, tq, B_WIDTH), cur),
                  pl.BlockSpec((1, 1, ATT_BLK, B_WIDTH), prev),
                  pl.BlockSpec((1, 1, tq, B_WIDTH), cur),
                  pl.BlockSpec((1, 1, ATT_BLK, B_WIDTH), prev)],
        out_specs=[pl.BlockSpec((1, 1, tq, B_WIDTH), cur),
                   pl.BlockSpec((1, 1, tq, LANES), cur)],
        out_shape=[jax.ShapeDtypeStruct((b, d, ls, B_WIDTH), BF16),
                   jax.ShapeDtypeStruct((b, d, ls, LANES), F32)],
        compiler_params=pltpu.CompilerParams(
            dimension_semantics=("arbitrary", "arbitrary", "arbitrary"), vmem_limit_bytes=VMEM_LIMIT),
        name="band_attention_d%d" % d,
    )(q, k, k, v, v)


def _unpermute(ref):
    return pltpu.einshape("rjl->(jr)l", ref[0]).astype(F32)


def _l1c_kernel(x_ref, o0_ref, o1_ref, o2_ref, l0_ref, l1_ref, l2_ref, zg_ref, qm_ref, zm_ref,
                mkv_ref, wout_ref, gpost_ref, y_ref):
    tt = x_ref.shape[1]
    o_refs = (o0_ref, o1_ref, o2_ref)
    l_refs = (l0_ref, l1_ref, l2_ref)
    outs, lses = [], []
    for g, (_, d) in enumerate(B_GROUPS):
        outs.append(_unpermute(o_refs[g]))
        lses.append(_unpermute(l_refs[g])[:, 0:B_HEADS])
    mx = jnp.maximum(jnp.maximum(lses[0], lses[1]), lses[2])
    es = [jnp.exp(l - mx) for l in lses]
    tot = es[0] + es[1] + es[2]
    ws = [e / tot for e in es]
    parts = []
    for h in range(B_HEADS):
        hs = slice(h * B_HDIM, (h + 1) * B_HDIM)
        acc = ws[0][:, h:h + 1] * outs[0][:, hs]
        acc = acc + ws[1][:, h:h + 1] * outs[1][:, hs]
        acc = acc + ws[2][:, h:h + 1] * outs[2][:, hs]
        parts.append(acc)
    ydil = jnp.concatenate(parts, axis=-1)
    ymix = (ydil * _silu(zg_ref[0].astype(F32))).astype(BF16)
    mk = mkv_ref[0, :, 0:M_WIDTH]
    mv = mkv_ref[0, :, M_WIDTH:2 * M_WIDTH]
    ym = (_mem_attention(qm_ref[0], mk, mv) * _silu(zm_ref[0].astype(F32))).astype(BF16)
    out = _dot(ymix, wout_ref[0:B_WIDTH, :]) + _dot(ym, wout_ref[B_WIDTH:B_WIDTH + M_WIDTH, :])
    y_ref[0] = x_ref[0] + _rms_scale(out) * gpost_ref[...]


def _layer1_out_prompt(x1, os_, ls_, zg, qm, zm, mkv_bf, w_out, g_post):
    b, s, _ = x1.shape
    tt = min(TOK_TILE, s)
    nt = s // tt
    tile = lambda bb, i: (bb, i, 0)
    perm = lambda bb, i: (bb, 0, i, 0)
    in_specs = [pl.BlockSpec((1, tt, D_MODEL), tile)]
    for width in (B_WIDTH, LANES):
        for (_, d) in B_GROUPS:
            in_specs.append(pl.BlockSpec((1, d, tt // d, width), perm))
    in_specs += [pl.BlockSpec((1, tt, B_WIDTH), tile)] * 3
    in_specs += [pl.BlockSpec((1, N_MEM, 2 * M_WIDTH), lambda bb, i: (bb, 0, 0)),
                 _const_spec(w_out.shape), _const_spec((1, D_MODEL))]
    return pl.pallas_call(
        _l1c_kernel,
        grid=(b, nt),
        in_specs=in_specs,
        out_specs=pl.BlockSpec((1, tt, D_MODEL), tile),
        out_shape=jax.ShapeDtypeStruct((b, s, D_MODEL), F32),
        compiler_params=pltpu.CompilerParams(
            dimension_semantics=("arbitrary", "arbitrary"), vmem_limit_bytes=VMEM_LIMIT),
        name="layer1_out_prompt",
    )(x1, *os_, *ls_, zg, qm, zm, mkv_bf, w_out, g_post)


def _rope_tables(pos):
    half = B_HDIM // 2
    inv = ROPE_THETA ** (-jnp.arange(half, dtype=F32) / half)
    ang = pos[:, None] * inv[None, :]
    cos = jnp.cos(ang)
    sin = jnp.sin(ang)
    return jnp.concatenate([cos, cos], axis=-1), jnp.concatenate([-sin, sin], axis=-1)


def _prompt_group(x_prompt, mem_prompt, p):
    b, s, _ = x_prompt.shape
    memkv_f, memkv_b = _memkv(mem_prompt.reshape(b * N_MEM, D_MODEL), p['w_mkv'])
    depth = memkv_f.shape[0]
    memkv_b = memkv_b.reshape(depth, b, N_MEM, 2 * M_WIDTH)
    x1, conv_p, c_p, n_p, m_pad = _layer0_prompt(
        x_prompt, p['g_pre'][0:1], p['w_in_a'][0], p['conv_w_a'][0], p['conv_b_a'], p['w_q_a'][0],
        p['w_k_a'][0], jnp.swapaxes(p['w_k_a'][0], 1, 2), p['w_v_a'][0], p['w_if_a'], p['b_if_a'],
        p['g_hn_a'], p['skip_a'], memkv_b[0], p['w_out_a'][0], p['g_post'][0:1])
    cos_t, sin_t = _rope_tables(jnp.arange(s, dtype=F32))
    outs = _layer1_proj_prompt(x1, p['g_kv'], p['g_pre'][1:2], p['w_kv_b'], p['w_in_b'][0], cos_t, sin_t)
    qs, ks, vs = outs[0:3], outs[3:6], outs[6:9]
    zg, qm, zm = outs[9:12]
    wins = outs[12:15]
    os_, ls_ = [], []
    for g in range(N_GROUPS):
        o, l = _band_attention(qs[g], ks[g], vs[g])
        os_.append(o)
        ls_.append(l)
    y = _layer1_out_prompt(x1, os_, ls_, zg, qm, zm, memkv_b[1], p['w_out_b'][0], p['g_post'][1:2])
    m_p = m_pad[:, 0:A_HEADS, 0][None]
    wins = [w.reshape(b, w.shape[1], 2, B_HEADS, B_HDIM) for w in wins]
    memkv_p = memkv_f.reshape(depth, b, N_MEM, 2, M_HEADS, M_HDIM)
    return y, conv_p, c_p, n_p, m_p, wins, memkv_p


def _prep_params(g_pre, g_post, w_in_a, conv_w_a, conv_b_a, w_q_a, w_k_a, w_v_a, w_if_a, b_if_a,
                 g_hn_a, skip_a, w_out_a, g_kv, w_kv_b, w_in_b, w_out_b, w_mkv):
    wif = jnp.pad(w_if_a[0], ((0, 0), (0, LANES - 2 * A_HEADS))).astype(BF16)
    bif = jnp.pad(b_if_a[0], (0, LANES - 2 * A_HEADS))[None, :]
    return {
        'g_pre': g_pre, 'g_post': g_post,
        'w_in_a': w_in_a.astype(BF16), 'conv_w_a': conv_w_a, 'conv_b_a': conv_b_a,
        'w_q_a': w_q_a.astype(BF16), 'w_k_a': w_k_a.astype(BF16), 'w_v_a': w_v_a.astype(BF16),
        'w_if_a': wif, 'b_if_a': bif, 'g_hn_a': g_hn_a, 'skip_a': skip_a,
        'w_out_a': w_out_a.astype(BF16), 'g_kv': g_kv[None, :], 'w_kv_b': w_kv_b.astype(BF16),
        'w_in_b': w_in_b.astype(BF16), 'w_out_b': w_out_b.astype(BF16), 'w_mkv': w_mkv.astype(BF16),
    }


def _dec_l0_proj_kernel(x_ref, gpre_ref, win_ref, cst_ref, convw_ref, convb_ref, wq_ref, wk_ref, wv_ref,
                        wif_ref, bif_ref,
                        q_ref, k_ref, v_ref, gates_ref, xc_ref, opre_ref, zg_ref, qm_ref, zm_ref, cnew_ref):
    h = (_rms_scale(x_ref[...]) * gpre_ref[...]).astype(BF16)
    u = _dot(h, win_ref[:, 0:A_INNER])
    opre_ref[...] = _dot(h, win_ref[:, A_INNER:2 * A_INNER])
    zg_ref[...] = _dot(h, win_ref[:, 2 * A_INNER:3 * A_INNER])
    qm_ref[...] = _dot(h, win_ref[:, 3 * A_INNER:3 * A_INNER + M_WIDTH])
    zm_ref[...] = _dot(h, win_ref[:, 3 * A_INNER + M_WIDTH:3 * A_INNER + 2 * M_WIDTH])
    cw = convw_ref[...]
    xc = convb_ref[...] + cst_ref[0] * cw[0:1, :]
    xc = xc + cst_ref[1] * cw[1:2, :]
    xc = xc + cst_ref[2] * cw[2:3, :]
    xc = xc + u * cw[3:4, :]
    xc = _silu(xc)
    xc_ref[...] = xc
    cnew_ref[0] = cst_ref[1]
    cnew_ref[1] = cst_ref[2]
    cnew_ref[2] = u
    qs, ks, vs, cat = [], [], [], []
    for hd in range(A_HEADS):
        sl = slice(hd * A_HDIM, (hd + 1) * A_HDIM)
        xh = xc[:, sl].astype(BF16)
        qh = _dot(xh, wq_ref[hd])
        kh = _dot(xh, wk_ref[hd]) * (A_HDIM ** -0.5)
        vh = _dot(u[:, sl].astype(BF16), wv_ref[hd])
        qs.append(qh)
        ks.append(kh)
        vs.append(vh)
        cat += [qh.astype(BF16), kh.astype(BF16), vh.astype(BF16)]
    q_ref[...] = jnp.concatenate(qs, axis=-1)
    k_ref[...] = jnp.concatenate(ks, axis=-1)
    v_ref[...] = jnp.concatenate(vs, axis=-1)
    gates_ref[...] = _dot(jnp.concatenate(cat, axis=-1), wif_ref[...]) + bif_ref[...]


def _whole(shape):
    nd = len(shape)
    return pl.BlockSpec(shape, lambda *_: (0,) * nd)


def _dec_l0_proj(x, g_pre, w_in, cst, conv_w, conv_b, wq, wk, wv, wif, bif):
    nb = x.shape[0]
    args = (x, g_pre, w_in, cst, conv_w, conv_b, wq, wk, wv, wif, bif)
    f = lambda *s: jax.ShapeDtypeStruct(s, F32)
    out_shape = [f(nb, A_INNER), f(nb, A_INNER), f(nb, A_INNER), f(nb, LANES), f(nb, A_INNER), f(nb, A_INNER),
                 f(nb, A_INNER), f(nb, M_WIDTH), f(nb, M_WIDTH), f(CONV_W - 1, nb, A_INNER)]
    return pl.pallas_call(
        _dec_l0_proj_kernel,
        grid=(1,),
        in_specs=[_whole(a.shape) for a in args],
        out_specs=[_whole(o.shape) for o in out_shape],
        out_shape=out_shape,
        compiler_params=pltpu.CompilerParams(dimension_semantics=("arbitrary",), vmem_limit_bytes=VMEM_LIMIT),
        name="dec_l0_proj",
    )(*args)


def _row_to_col(row, eye):
    return jnp.sum(jnp.where(eye, row, 0.0), axis=-1, keepdims=True)


def _col_to_row(colv, eye):
    return jnp.sum(jnp.where(eye, colv, 0.0), axis=0, keepdims=True)


def _dec_mem_attention(q, kv_ref_view):
    kk = kv_ref_view[:, 0]
    vv = kv_ref_view[:, 1]
    s = jnp.sum(kk * q[None], axis=-1, keepdims=True) * (M_HDIM ** -0.5)
    mx = jnp.max(s, axis=0, keepdims=True)
    p = jnp.exp(s - mx)
    p = p / jnp.sum(p, axis=0, keepdims=True)
    return jnp.sum(p * vv, axis=0)


def _dec_mlstm_kernel(q_ref, k_ref, v_ref, gates_ref, m_ref, c_ref, n_ref, qm_ref, kv_ref,
                      hs_ref, c_out, n_out, m_out, ym_ref):
    b = pl.program_id(0)
    rb = pl.ds(b, 1)
    g = gates_ref[rb, :]
    mrow = m_ref[rb, :]
    r = lax.broadcasted_iota(jnp.int32, (A_HDIM, A_HDIM), 0)
    c = lax.broadcasted_iota(jnp.int32, (A_HDIM, A_HDIM), 1)
    eye = r == c
    lane = lax.broadcasted_iota(jnp.int32, (1, LANES), 1)
    m_acc = jnp.zeros((1, LANES), F32)
    for h in range(A_HEADS):
        sl = slice(h * A_HDIM, (h + 1) * A_HDIM)
        qh = q_ref[rb, sl]
        kh = k_ref[rb, sl]
        vh = v_ref[rb, sl]
        c_old = c_ref[0, 0, h]
        n_old = n_ref[0, 0, h:h + 1, :]
        li = g[:, h:h + 1]
        lf = _log_sigmoid(g[:, 4 + h:5 + h])
        m_old = mrow[:, h:h + 1]
        cq = jnp.sum(c_old * qh, axis=-1, keepdims=True)
        nq = jnp.sum(n_old * qh, axis=-1, keepdims=True)
        qk = jnp.sum(qh * kh, axis=-1, keepdims=True)
        inter = lf + m_old
        m_new = jnp.maximum(inter, li)
        ws = jnp.exp(li - m_new)
        dec = jnp.exp(inter - m_new)
        sc = qk * ws
        v_col = _row_to_col(vh, eye)
        den = sc + dec * nq
        h_col = (sc * v_col + dec * cq) / jnp.maximum(jnp.abs(den), jnp.exp(-m_new))
        hs_ref[0, :, sl] = _col_to_row(h_col, eye)
        c_out[0, 0, h] = dec * c_old + (ws * v_col) * kh
        n_out[0, 0, h:h + 1, :] = dec * n_old + ws * kh
        m_acc = m_acc + jnp.where(lane == h, m_new, 0.0)
    m_out[0] = m_acc
    ym_ref[0] = _dec_mem_attention(qm_ref[0], kv_ref.at[0, 0])


def _dec_mlstm(q, k, v, gates, m_in, state_c, state_n, qm3, cache_mem_kv):
    nb = q.shape[0]
    per_b3 = lambda b: (b, 0, 0)
    in_specs = [_whole(q.shape), _whole(k.shape), _whole(v.shape), _whole(gates.shape), _whole(m_in.shape),
                pl.BlockSpec((1, 1, A_HEADS, A_HDIM, A_HDIM), lambda b: (0, b, 0, 0, 0)),
                pl.BlockSpec((1, 1, A_HEADS, A_HDIM), lambda b: (0, b, 0, 0)),
                pl.BlockSpec((1, M_HEADS, M_HDIM), per_b3),
                pl.BlockSpec((1, 1, N_MEM, 2, M_HEADS, M_HDIM), lambda b: (0, b, 0, 0, 0, 0))]
    out_specs = [pl.BlockSpec((1, 1, A_INNER), per_b3),
                 pl.BlockSpec((1, 1, A_HEADS, A_HDIM, A_HDIM), lambda b: (0, b, 0, 0, 0)),
                 pl.BlockSpec((1, 1, A_HEADS, A_HDIM), lambda b: (0, b, 0, 0)),
                 pl.BlockSpec((1, 1, LANES), per_b3),
                 pl.BlockSpec((1, M_HEADS, M_HDIM), per_b3)]
    out_shape = [jax.ShapeDtypeStruct((nb, 1, A_INNER), F32),
                 jax.ShapeDtypeStruct(state_c.shape, F32),
                 jax.ShapeDtypeStruct(state_n.shape, F32),
                 jax.ShapeDtypeStruct((nb, 1, LANES), F32),
                 jax.ShapeDtypeStruct((nb, M_HEADS, M_HDIM), F32)]
    return pl.pallas_call(
        _dec_mlstm_kernel,
        grid=(nb,),
        in_specs=in_specs,
        out_specs=out_specs,
        out_shape=out_shape,
        compiler_params=pltpu.CompilerParams(dimension_semantics=("arbitrary",), vmem_limit_bytes=VMEM_LIMIT),
        name="dec_mlstm",
    )(q, k, v, gates, m_in, state_c, state_n, qm3, cache_mem_kv)


def _dec_mid_kernel(hs_ref, opre_ref, xc_ref, zg_ref, ym_ref, zm_ref, x_ref, ghn_ref, skip_ref, wout_ref,
                    gpost_ref, gkv_ref, gpre_ref, wkv_ref, win_ref, cos_ref, sin_ref,
                    x1_ref, q_ref, k_ref, v_ref, zg1_ref, qm1_ref, zm1_ref):
    hh = _sigmoid(opre_ref[...]) * hs_ref[...]
    parts = []
    for h in range(A_HEADS):
        v = hh[:, h * A_HDIM:(h + 1) * A_HDIM]
        mu = jnp.mean(v, axis=-1, keepdims=True)
        var = jnp.mean(jnp.square(v - mu), axis=-1, keepdims=True)
        parts.append((v - mu) * lax.rsqrt(var + EPS))
    y = jnp.concatenate(parts, axis=-1) * ghn_ref[...] + skip_ref[...] * xc_ref[...]
    ymix = (y * _silu(zg_ref[...])).astype(BF16)
    ym = (ym_ref[...] * _silu(zm_ref[...])).astype(BF16)
    out = _dot(ymix, wout_ref[0:A_INNER, :]) + _dot(ym, wout_ref[A_INNER:A_INNER + M_WIDTH, :])
    x1 = x_ref[...] + _rms_scale(out) * gpost_ref[...]
    x1_ref[...] = x1
    xn = _rms_scale(x1)
    hk = (xn * gkv_ref[...]).astype(BF16)
    hq = (xn * gpre_ref[...]).astype(BF16)
    cos = cos_ref[...]
    sin = sin_ref[...]
    ks, vs = [], []
    for g in range(N_GROUPS):
        ks.append(_rope_cols(_dot(hk, wkv_ref[:, g * 2 * B_WIDTH:g * 2 * B_WIDTH + B_WIDTH]), cos, sin))
        vs.append(_dot(hk, wkv_ref[:, g * 2 * B_WIDTH + B_WIDTH:(g + 1) * 2 * B_WIDTH]))
    k_ref[...] = jnp.concatenate(ks, axis=-1)
    v_ref[...] = jnp.concatenate(vs, axis=-1)
    qoff = N_GROUPS * B_WIDTH
    q_ref[...] = _rope_cols(_dot(hq, win_ref[:, 0:qoff]), cos, sin)
    zg1_ref[...] = _dot(hq, win_ref[:, qoff:qoff + B_WIDTH])
    qm1_ref[...] = _dot(hq, win_ref[:, qoff + B_WIDTH:qoff + B_WIDTH + M_WIDTH])
    zm1_ref[...] = _dot(hq, win_ref[:, qoff + B_WIDTH + M_WIDTH:qoff + B_WIDTH + 2 * M_WIDTH])


def _dec_mid(hs, opre, xc, zg, ym, zm, x, ghn, skip, w_out, g_post, g_kv, g_pre, wkv, win, cos, sin):
    nb = x.shape[0]
    args = (hs, opre, xc, zg, ym, zm, x, ghn, skip, w_out, g_post, g_kv, g_pre, wkv, win, cos, sin)
    f = lambda *s: jax.ShapeDtypeStruct(s, F32)
    out_shape = [f(nb, D_MODEL), f(nb, N_GROUPS * B_WIDTH), f(nb, N_GROUPS * B_WIDTH), f(nb, N_GROUPS * B_WIDTH),
                 f(nb, B_WIDTH), f(nb, M_WIDTH), f(nb, M_WIDTH)]
    return pl.pallas_call(
        _dec_mid_kernel,
        grid=(1,),
        in_specs=[_whole(a.shape) for a in args],
        out_specs=[_whole(o.shape) for o in out_shape],
        out_shape=out_shape,
        compiler_params=pltpu.CompilerParams(dimension_semantics=("arbitrary",), vmem_limit_bytes=VMEM_LIMIT),
        name="dec_mid",
    )(*args)


def _dec_attn_kernel(q_ref, kn_ref, vn_ref, w0_ref, w1_ref, w2_ref, qm_ref, kv_ref, ydil_ref, ym_ref):
    w_refs = (w0_ref, w1_ref, w2_ref)
    scale = B_HDIM ** -0.5
    outs, lses = [], []
    for g in range(N_GROUPS):
        q = q_ref[0, g]
        kn = kn_ref[0, g]
        vn = vn_ref[0, g]
        wv = w_refs[g].at[0]
        kk = wv[:, 0]
        vv = wv[:, 1]
        s_c = jnp.sum(kk * q[None], axis=-1, keepdims=True) * scale
        s_n = jnp.sum(kn * q, axis=-1, keepdims=True) * scale
        mx = jnp.maximum(jnp.max(s_c, axis=0), s_n)
        p_c = jnp.exp(s_c - mx[None])
        p_n = jnp.exp(s_n - mx)
        l = jnp.sum(p_c, axis=0) + p_n
        outs.append(jnp.sum((p_c / l[None]) * vv, axis=0) + (p_n / l) * vn)
        lses.append(mx + jnp.log(l))
    mx = jnp.maximum(jnp.maximum(lses[0], lses[1]), lses[2])
    es = [jnp.exp(l - mx) for l in lses]
    tot = es[0] + es[1] + es[2]
    ydil_ref[0] = (es[0] / tot) * outs[0] + (es[1] / tot) * outs[1] + (es[2] / tot) * outs[2]
    ym_ref[0] = _dec_mem_attention(qm_ref[0], kv_ref.at[0, 0])


def _dec_attn(q4, kn4, vn4, cw0, cw1, cw2, qm3, cache_mem_kv, layer):
    nb = q4.shape[0]
    per_b3 = lambda b: (b, 0, 0)
    per_b4 = lambda b: (b, 0, 0, 0)
    rows = B_GROUPS[0][0]
    win_specs = [pl.BlockSpec((1, rows, 2, B_HEADS, B_HDIM), lambda b: (b, 0, 0, 0, 0))]
    for cw in (cw1, cw2):
        win_specs.append(pl.BlockSpec((1, rows, None, 2, B_HEADS, B_HDIM), lambda b: (b, 0, 0, 0, 0, 0)))
    in_specs = [pl.BlockSpec((1, N_GROUPS, B_HEADS, B_HDIM), per_b4)] * 3 + win_specs + [
        pl.BlockSpec((1, M_HEADS, M_HDIM), per_b3),
        pl.BlockSpec((1, 1, N_MEM, 2, M_HEADS, M_HDIM), lambda b: (layer, b, 0, 0, 0, 0))]
    return pl.pallas_call(
        _dec_attn_kernel,
        grid=(nb,),
        in_specs=in_specs,
        out_specs=[pl.BlockSpec((1, B_HEADS, B_HDIM), per_b3), pl.BlockSpec((1, M_HEADS, M_HDIM), per_b3)],
        out_shape=[jax.ShapeDtypeStruct((nb, B_HEADS, B_HDIM), F32),
                   jax.ShapeDtypeStruct((nb, M_HEADS, M_HDIM), F32)],
        compiler_params=pltpu.CompilerParams(dimension_semantics=("arbitrary",), vmem_limit_bytes=VMEM_LIMIT),
        name="dec_attn",
    )(q4, kn4, vn4, cw0, cw1, cw2, qm3, cache_mem_kv)


def _dec_out_kernel(ydil_ref, zg_ref, ym_ref, zm_ref, x_ref, wout_ref, gpost_ref, y_ref):
    ymix = (ydil_ref[...] * _silu(zg_ref[...])).astype(BF16)
    ym = (ym_ref[...] * _silu(zm_ref[...])).astype(BF16)
    out = _dot(ymix, wout_ref[0:B_WIDTH, :]) + _dot(ym, wout_ref[B_WIDTH:B_WIDTH + M_WIDTH, :])
    y_ref[...] = x_ref[...] + _rms_scale(out) * gpost_ref[...]


def _dec_out(ydil, zg, ym, zm, x1, w_out, g_post):
    args = (ydil, zg, ym, zm, x1, w_out, g_post)
    return pl.pallas_call(
        _dec_out_kernel,
        grid=(1,),
        in_specs=[_whole(a.shape) for a in args],
        out_specs=_whole(x1.shape),
        out_shape=jax.ShapeDtypeStruct(x1.shape, F32),
        compiler_params=pltpu.CompilerParams(dimension_semantics=("arbitrary",), vmem_limit_bytes=VMEM_LIMIT),
        name="dec_out",
    )(*args)


def _sample_group(x_sample, state_conv, state_c, state_n, state_m, cache_wins, cache_mem_kv, p):
    nb = x_sample.shape[0]
    x = x_sample.reshape(nb, D_MODEL)
    cst = state_conv[0].transpose(1, 0, 2)
    q, k, v, gates, xc, opre, zg, qm, zm, cnew = _dec_l0_proj(
        x, p['g_pre'][0:1], p['w_in_a'][0], cst, p['conv_w_a'][0], p['conv_b_a'], p['w_q_a'][0],
        p['w_k_a'][0], p['w_v_a'][0], p['w_if_a'], p['b_if_a'])
    m_in = jnp.pad(state_m[0], ((0, 0), (0, LANES - A_HEADS)))
    hs, c_s, n_s, m_pad, ym0 = _dec_mlstm(q, k, v, gates, m_in, state_c, state_n,
                                          qm.reshape(nb, M_HEADS, M_HDIM), cache_mem_kv)
    pos = PAST_LEN + jnp.arange(1, dtype=F32)
    cos, sin = _rope_tables(pos)
    x1, qd, kn, vn, zg1, qm1, zm1 = _dec_mid(
        hs.reshape(nb, A_INNER), opre, xc, zg, ym0.reshape(nb, M_WIDTH), zm, x, p['g_hn_a'], p['skip_a'],
        p['w_out_a'][0], p['g_post'][0:1], p['g_kv'], p['g_pre'][1:2], p['w_kv_b'], p['w_in_b'][0], cos, sin)
    shp4 = (nb, N_GROUPS, B_HEADS, B_HDIM)
    kn4 = kn.reshape(shp4)
    vn4 = vn.reshape(shp4)
    cws = [cache_wins[0]]
    for g in (1, 2):
        w, d = B_GROUPS[g]
        cws.append(cache_wins[g].reshape(nb, w // d, d, 2, B_HEADS, B_HDIM))
    ydil, ym1 = _dec_attn(qd.reshape(shp4), kn4, vn4, cws[0], cws[1], cws[2],
                          qm1.reshape(nb, M_HEADS, M_HDIM), cache_mem_kv, 1)
    y = _dec_out(ydil.reshape(nb, B_WIDTH), zg1, ym1.reshape(nb, M_WIDTH), zm1, x1, p['w_out_b'][0],
                 p['g_post'][1:2])
    conv_s = cnew.transpose(1, 0, 2)[None]
    m_s = m_pad[:, 0, 0:A_HEADS][None]
    wins_s = [jnp.stack([kn4[:, g], vn4[:, g]], axis=1)[:, None] for g in range(N_GROUPS)]
    return y.reshape(nb, 1, D_MODEL), conv_s, c_s, n_s, m_s, wins_s


def kernel(x_prompt, x_sample, mem_prompt, state_conv, state_C, state_n, state_m, cache_win0, cache_win1,
           cache_win2, cache_mem_kv, g_pre, g_post, w_in_a, conv_w_a, conv_b_a, w_q_a, w_k_a, w_v_a, w_if_a,
           b_if_a, g_hn_a, skip_a, w_out_a, g_kv, w_kv_b, w_in_b, w_out_b, w_mkv):
    p = _prep_params(g_pre, g_post, w_in_a, conv_w_a, conv_b_a, w_q_a, w_k_a, w_v_a, w_if_a, b_if_a,
                     g_hn_a, skip_a, w_out_a, g_kv, w_kv_b, w_in_b, w_out_b, w_mkv)
    y_p, conv_p, c_p, n_p, m_p, wins_p, memkv_p = _prompt_group(x_prompt, mem_prompt, p)
    y_s, conv_s, c_s, n_s, m_s, wins_s = _sample_group(
        x_sample, state_conv, state_C, state_n, state_m, (cache_win0, cache_win1, cache_win2), cache_mem_kv, p)
    return (y_p, y_s, conv_p, c_p, n_p, m_p, wins_p[0], wins_p[1], wins_p[2], memkv_p,
            conv_s, c_s, n_s, m_s, wins_s[0], wins_s[1], wins_s[2])
```

```python
import functools

import jax
import jax.numpy as jnp
from jax import lax
from jax.experimental import pallas as pl
from jax.experimental.pallas import tpu as pltpu

F32 = jnp.float32
BF16 = jnp.bfloat16

D_MODEL = 1024
A_HEADS = 4
A_HDIM = 256
A_INNER = 1024
CONV_W = 4
A_CHUNK = 128
B_GROUPS = ((128, 1), (512, 4), (2048, 16))
N_GROUPS = 3
B_HEADS = 4
B_HDIM = 128
B_WIDTH = 512
N_MEM = 256
M_HEADS = 4
M_HDIM = 128
M_WIDTH = 512
ROPE_THETA = 10000.0
EPS = 1e-6
PAST_LEN = 8192

LANES = 128
TOK_TILE = 512
L0_TILE = 256
ATT_BLK = 128
VMEM_LIMIT = 56 * 1024 * 1024

NT_DIMS = (((1,), (1,)), ((), ()))


def _dot(a, b):
    return jnp.dot(a, b, preferred_element_type=F32)


def _dot_nt(a, b):
    return lax.dot_general(a, b, NT_DIMS, preferred_element_type=F32)


def _sigmoid(x):
    return 1.0 / (1.0 + jnp.exp(-x))


def _silu(x):
    return x * _sigmoid(x)


def _log_sigmoid(x):
    return jnp.minimum(x, 0.0) - jnp.log(1.0 + jnp.exp(-jnp.abs(x)))


def _rms_scale(x):
    return x * lax.rsqrt(jnp.mean(x * x, axis=-1, keepdims=True) + EPS)


def _const_spec(shape):
    nd = len(shape)
    return pl.BlockSpec(shape, lambda *_: (0,) * nd, pipeline_mode=pl.Buffered(1))


def _mem_attention(qm, mk, mv):
    heads = range(M_HEADS)
    sl = [slice(h * M_HDIM, (h + 1) * M_HDIM) for h in heads]
    s = [_dot_nt(qm[:, sl[h]], mk[:, sl[h]]) * (M_HDIM ** -0.5) for h in heads]
    mx = [jnp.max(s[h], axis=-1, keepdims=True) for h in heads]
    p = [jnp.exp(s[h] - mx[h]) for h in heads]
    l = [jnp.sum(p[h], axis=-1, keepdims=True) for h in heads]
    outs = [_dot((p[h] / l[h]).astype(BF16), mv[:, sl[h]]) for h in heads]
    return jnp.concatenate(outs, axis=-1)


def _rows_to_kv_heads(k, v):
    pieces = [a[:, h * LANES:(h + 1) * LANES] for a in (k, v) for h in range(a.shape[1] // LANES)]
    return jnp.swapaxes(jnp.stack(pieces), 0, 1)


def _memkv_kernel(m_ref, w_ref, o_ref, ob_ref):
    r = _dot(m_ref[...].astype(BF16), w_ref[0])
    o_ref[0] = _rows_to_kv_heads(r[:, 0:M_WIDTH], r[:, M_WIDTH:2 * M_WIDTH])
    ob_ref[0] = r.astype(BF16)


def _memkv(mem2d, w_bf):
    nm = mem2d.shape[0]
    nl = w_bf.shape[0]
    tm = min(512, nm)
    return pl.pallas_call(
        _memkv_kernel,
        grid=(nl, nm // tm),
        in_specs=[pl.BlockSpec((tm, D_MODEL), lambda l, i: (i, 0)),
                  pl.BlockSpec((1, D_MODEL, 2 * M_WIDTH), lambda l, i: (l, 0, 0))],
        out_specs=[pl.BlockSpec((1, tm, 2 * M_HEADS, M_HDIM), lambda l, i: (l, i, 0, 0)),
                   pl.BlockSpec((1, tm, 2 * M_WIDTH), lambda l, i: (l, i, 0))],
        out_shape=[jax.ShapeDtypeStruct((nl, nm, 2 * M_HEADS, M_HDIM), F32),
                   jax.ShapeDtypeStruct((nl, nm, 2 * M_WIDTH), BF16)],
        compiler_params=pltpu.CompilerParams(dimension_semantics=("arbitrary", "arbitrary")),
        name="memkv",
    )(mem2d, w_bf)


def _mlstm_chunk(rs, g, qkv_v, kt_v, c_s, n_s, m_s, causal, tri, hs):
    ls = _log_sigmoid(g)
    t0 = ls.astype(BF16)
    e1 = ls - t0.astype(F32)
    t1 = e1.astype(BF16)
    t2 = (e1 - t1.astype(F32)).astype(BF16)
    bc = _dot(tri, t0) + _dot(tri, t1) + _dot(tri, t2)
    lane = lax.broadcasted_iota(jnp.int32, (A_CHUNK, LANES), 1)
    xt = jnp.where(lane < A_HEADS, g, bc).T
    yield
    heads = range(A_HEADS)
    b_col = [bc[:, 4 + h:5 + h] for h in heads]
    b_row = [xt[4 + h:5 + h, :] for h in heads]
    li_row = [xt[h:h + 1, :] for h in heads]
    li_col = [g[:, h:h + 1] for h in heads]
    m_old = [m_s[h:h + 1, 0:1] for h in heads]
    b_last = [bc[A_CHUNK - 1:A_CHUNK, 4 + h:5 + h] for h in heads]
    qh = [qkv_v[rs, h * 3 * A_HDIM:h * 3 * A_HDIM + A_HDIM] for h in heads]
    kh = [qkv_v[rs, h * 3 * A_HDIM + A_HDIM:h * 3 * A_HDIM + 2 * A_HDIM] for h in heads]
    vh = [qkv_v[rs, h * 3 * A_HDIM + 2 * A_HDIM:(h + 1) * 3 * A_HDIM] for h in heads]
    kt = [kt_v[h] for h in heads]
    c_old = [c_s[h] for h in heads]
    n_old = [n_s[h:h + 1, :] for h in heads]
    qk = [_dot_nt(qh[h], kh[h]) for h in heads]
    qc = [_dot(qh[h], c_old[h].astype(BF16)) for h in heads]
    dm = [jnp.where(causal, b_col[h] - b_row[h] + li_row[h], -jnp.inf) for h in heads]
    inter = [b_col[h] + m_old[h] for h in heads]
    m_row = [jnp.maximum(inter[h], jnp.max(dm[h], axis=-1, keepdims=True)) for h in heads]
    g_max = [jnp.max(b_last[h] - b_row[h] + li_row[h], axis=-1, keepdims=True) for h in heads]
    m_new = [jnp.maximum(b_last[h] + m_old[h], g_max[h]) for h in heads]
    yield
    sc = [qk[h] * jnp.exp(dm[h] - m_row[h]) for h in heads]
    dec = [jnp.exp(inter[h] - m_row[h]) for h in heads]
    ws_col = [jnp.exp(b_last[h] - b_col[h] + li_col[h] - m_new[h]) for h in heads]
    dc = [jnp.exp(b_last[h] + m_old[h] - m_new[h]) for h in heads]
    yield
    sv = [_dot(sc[h].astype(BF16), vh[h]) for h in heads]
    wv = [(ws_col[h] * vh[h].astype(F32)).astype(BF16) for h in heads]
    upd = [_dot(kt[h], wv[h]) for h in heads]
    yield
    for h in heads:
        den = (jnp.sum(sc[h], axis=-1, keepdims=True)
               + dec[h] * jnp.sum(qh[h].astype(F32) * n_old[h], axis=-1, keepdims=True))
        num = sv[h] + dec[h] * qc[h]
        hs.append(num / jnp.maximum(jnp.abs(den), jnp.exp(-m_row[h])))
    yield
    for h in heads:
        c_s[h] = dc[h] * c_old[h] + upd[h]
        n_s[h:h + 1, :] = dc[h] * n_old[h] + jnp.sum(ws_col[h] * kh[h].astype(F32), axis=0, keepdims=True)
        m_s[h:h + 1, :] = jnp.broadcast_to(m_new[h], (1, LANES))
    yield


def _l0p_kernel(nt, x_ref, xp_ref, gpre_ref, win_ref, convw_ref, convb_ref, wq_ref, wk_ref, wkt_ref, wv_ref,
                wif_ref, bif_ref, ghn_ref, skip_ref, mkv_ref, wout_ref, gpost_ref,
                x1_ref, conv_out, c_out, n_out, m_out,
                h_s, u_s, ymix_s, xc_s, opre_s, zg_s, qm_s, zm_s, qkv_s, kt_s, gates_s,
                c_s, n_s, m_s):
    tt = x_ref.shape[1]
    nsub = tt // A_CHUNK
    t = pl.program_id(0)
    parity = lax.rem(t, 2)
    pos1 = lax.rem(t, nt)
    pos2 = lax.rem(t + nt - 1, nt)

    @pl.when(t == 0)
    def _():
        xc_s[1] = jnp.zeros(xc_s.shape[1:], F32)
        opre_s[1] = jnp.zeros(opre_s.shape[1:], F32)
        zg_s[1] = jnp.zeros(zg_s.shape[1:], F32)
        qm_s[1] = jnp.zeros(qm_s.shape[1:], BF16)
        zm_s[1] = jnp.zeros(zm_s.shape[1:], F32)
        qkv_s[1] = jnp.zeros(qkv_s.shape[1:], BF16)
        kt_s[1] = jnp.zeros(kt_s.shape[1:], BF16)
        gates_s[1] = jnp.zeros(gates_s.shape[1:], F32)

    @pl.when(pos1 == 0)
    def _():
        u_s[0:8, :] = jnp.zeros((8, A_INNER), F32)

    @pl.when(pos2 == 0)
    def _():
        c_s[...] = jnp.zeros(c_s.shape, F32)
        n_s[...] = jnp.zeros(n_s.shape, F32)
        m_s[...] = jnp.zeros(m_s.shape, F32)

    row = lax.broadcasted_iota(jnp.int32, (A_CHUNK, A_CHUNK), 0)
    col = lax.broadcasted_iota(jnp.int32, (A_CHUNK, A_CHUNK), 1)
    causal = col <= row
    tri = jnp.where(causal, 1.0, 0.0).astype(BF16)
    ghn = ghn_ref[...]
    skp = skip_ref[...]
    mk = mkv_ref[0, :, 0:M_WIDTH]
    mv = mkv_ref[0, :, M_WIDTH:2 * M_WIDTH]

    def stage2(pslot):
        for c in range(nsub):
            rs = slice(c * A_CHUNK, (c + 1) * A_CHUNK)
            hs = []
            yield from _mlstm_chunk(rs, gates_s[pslot, rs, :], qkv_s.at[pslot], kt_s.at[pslot, :, c],
                                    c_s, n_s, m_s, causal, tri, hs)
            parts = []
            for h in range(A_HEADS):
                v = _sigmoid(opre_s[pslot, rs, h * A_HDIM:(h + 1) * A_HDIM]) * hs[h]
                mu = jnp.mean(v, axis=-1, keepdims=True)
                var = jnp.mean(jnp.square(v - mu), axis=-1, keepdims=True)
                parts.append((v - mu) * lax.rsqrt(var + EPS))
            hn = jnp.concatenate(parts, axis=-1) * ghn
            y = hn + skp * xc_s[pslot, rs, :]
            ymix_s[rs, 0:A_INNER] = (y * _silu(zg_s[pslot, rs, :])).astype(BF16)
            yield
            ym = _mem_attention(qm_s[pslot, rs, :], mk, mv) * _silu(zm_s[pslot, rs, :])
            ymix_s[rs, A_INNER:A_INNER + M_WIDTH] = ym.astype(BF16)
            yield

    def stage1(slot):
        h_s[...] = (_rms_scale(x_ref[0]) * gpre_ref[...]).astype(BF16)
        hb = h_s[...]
        u_s[8:8 + tt, :] = _dot(hb, win_ref[:, 0:A_INNER])
        yield
        cw = convw_ref[...]
        cb = convb_ref[...]
        for c in range(nsub):
            r0 = c * A_CHUNK
            xc = cb + u_s[r0 + 5:r0 + 5 + A_CHUNK, :] * cw[0:1, :]
            xc = xc + u_s[r0 + 6:r0 + 6 + A_CHUNK, :] * cw[1:2, :]
            xc = xc + u_s[r0 + 7:r0 + 7 + A_CHUNK, :] * cw[2:3, :]
            xc = xc + u_s[r0 + 8:r0 + 8 + A_CHUNK, :] * cw[3:4, :]
            xc_s[slot, r0:r0 + A_CHUNK, :] = _silu(xc)
        opre_s[slot] = _dot(hb, win_ref[:, A_INNER:2 * A_INNER])
        yield
        zg_s[slot] = _dot(hb, win_ref[:, 2 * A_INNER:3 * A_INNER])
        yield
        qm_s[slot] = _dot(hb, win_ref[:, 3 * A_INNER:3 * A_INNER + M_WIDTH]).astype(BF16)
        zm_s[slot] = _dot(hb, win_ref[:, 3 * A_INNER + M_WIDTH:3 * A_INNER + 2 * M_WIDTH])
        yield
        for h in range(A_HEADS):
            sl = slice(h * A_HDIM, (h + 1) * A_HDIM)
            xh = xc_s[slot, :, sl].astype(BF16)
            uh = u_s[8:8 + tt, sl].astype(BF16)
            base = h * 3 * A_HDIM
            qkv_s[slot, :, base:base + A_HDIM] = _dot(xh, wq_ref[h]).astype(BF16)
            qkv_s[slot, :, base + A_HDIM:base + 2 * A_HDIM] = (
                _dot(xh, wk_ref[h]) * (A_HDIM ** -0.5)).astype(BF16)
            qkv_s[slot, :, base + 2 * A_HDIM:base + 3 * A_HDIM] = _dot(uh, wv_ref[h]).astype(BF16)
            kt = (_dot_nt(wkt_ref[h], xh) * (A_HDIM ** -0.5)).astype(BF16)
            for c in range(nsub):
                kt_s[slot, h, c] = kt[:, c * A_CHUNK:(c + 1) * A_CHUNK]
            yield
        gates_s[slot] = _dot(qkv_s[slot], wif_ref[...]) + bif_ref[...]
        yield

    def step(slot):
        pending = [stage1(slot), stage2(1 - slot)]
        while pending:
            for gen in list(pending):
                try:
                    next(gen)
                except StopIteration:
                    pending.remove(gen)
        out = _dot(ymix_s[...], wout_ref[...])
        x1_ref[0] = xp_ref[0] + _rms_scale(out) * gpost_ref[...]

    for s in range(2):
        pl.when(parity == s)(functools.partial(step, s))

    @pl.when(pos1 == nt - 1)
    def _():
        conv_out[0, 0] = u_s[tt + 5:tt + 8, :]

    u_s[0:8, :] = u_s[tt:tt + 8, :]

    @pl.when(jnp.logical_and(pos2 == nt - 1, t > 0))
    def _():
        for h in range(A_HEADS):
            c_out[0, 0, h] = c_s[h].T
        n_out[0, 0] = n_s[0:A_HEADS, :]
        m_out[0] = m_s[...]


def _layer0_prompt_pipelined(x, g_pre, w_in, conv_w, conv_b, wq, wk, wkt, wv, wif, bif, ghn, skip, mkv_bf,
                             w_out, g_post):
    b, s, _ = x.shape
    tt = min(L0_TILE, s)
    nt = s // tt
    ntiles = b * nt
    a_in = w_in.shape[1]
    nsub = tt // A_CHUNK

    def cur(t):
        t1 = jnp.minimum(t, ntiles - 1)
        return (t1 // nt, t1 % nt, 0)

    def prev(t):
        t2 = jnp.maximum(t - 1, 0)
        return (t2 // nt, t2 % nt, 0)

    def prev_b(t):
        return jnp.maximum(t - 1, 0) // nt

    in_specs = [
        pl.BlockSpec((1, tt, D_MODEL), cur),
        pl.BlockSpec((1, tt, D_MODEL), prev),
        _const_spec((1, D_MODEL)),
        _const_spec((D_MODEL, a_in)),
        _const_spec((CONV_W, A_INNER)),
        _const_spec((1, A_INNER)),
        _const_spec((A_HEADS, A_HDIM, A_HDIM)),
        _const_spec((A_HEADS, A_HDIM, A_HDIM)),
        _const_spec((A_HEADS, A_HDIM, A_HDIM)),
        _const_spec((A_HEADS, A_HDIM, A_HDIM)),
        _const_spec((3 * A_INNER, LANES)),
        _const_spec((1, LANES)),
        _const_spec((1, A_INNER)),
        _const_spec((1, A_INNER)),
        pl.BlockSpec((1, N_MEM, 2 * M_WIDTH), lambda t: (prev_b(t), 0, 0)),
        _const_spec((A_INNER + M_WIDTH, D_MODEL)),
        _const_spec((1, D_MODEL)),
    ]
    out_specs = [
        pl.BlockSpec((1, tt, D_MODEL), prev),
        pl.BlockSpec((1, 1, CONV_W - 1, A_INNER), lambda t: (0, prev_b(t), 0, 0)),
        pl.BlockSpec((1, 1, A_HEADS, A_HDIM, A_HDIM), lambda t: (0, prev_b(t), 0, 0, 0)),
        pl.BlockSpec((1, 1, A_HEADS, A_HDIM), lambda t: (0, prev_b(t), 0, 0)),
        pl.BlockSpec((1, 8, LANES), lambda t: (prev_b(t), 0, 0)),
    ]
    out_shape = [
        jax.ShapeDtypeStruct((b, s, D_MODEL), F32),
        jax.ShapeDtypeStruct((1, b, CONV_W - 1, A_INNER), F32),
        jax.ShapeDtypeStruct((1, b, A_HEADS, A_HDIM, A_HDIM), F32),
        jax.ShapeDtypeStruct((1, b, A_HEADS, A_HDIM), F32),
        jax.ShapeDtypeStruct((b, 8, LANES), F32),
    ]
    scratch = [
        pltpu.VMEM((tt, D_MODEL), BF16),
        pltpu.VMEM((tt + 8, A_INNER), F32),
        pltpu.VMEM((tt, A_INNER + M_WIDTH), BF16),
        pltpu.VMEM((2, tt, A_INNER), F32),
        pltpu.VMEM((2, tt, A_INNER), F32),
        pltpu.VMEM((2, tt, A_INNER), F32),
        pltpu.VMEM((2, tt, M_WIDTH), BF16),
        pltpu.VMEM((2, tt, M_WIDTH), F32),
        pltpu.VMEM((2, tt, 3 * A_INNER), BF16),
        pltpu.VMEM((2, A_HEADS, nsub, A_HDIM, A_CHUNK), BF16),
        pltpu.VMEM((2, tt, LANES), F32),
        pltpu.VMEM((A_HEADS, A_HDIM, A_HDIM), F32),
        pltpu.VMEM((8, A_HDIM), F32),
        pltpu.VMEM((8, LANES), F32),
    ]
    return pl.pallas_call(
        functools.partial(_l0p_kernel, nt),
        grid=(ntiles + 1,),
        in_specs=in_specs,
        out_specs=out_specs,
        out_shape=out_shape,
        scratch_shapes=scratch,
        compiler_params=pltpu.CompilerParams(
            dimension_semantics=("arbitrary",), vmem_limit_bytes=VMEM_LIMIT),
        name="layer0_prompt",
    )(x, x, g_pre, w_in, conv_w, conv_b, wq, wk, wkt, wv, wif, bif, ghn, skip, mkv_bf, w_out, g_post)


def _l0_kernel(x_ref, gpre_ref, win_ref, convw_ref, convb_ref, wq_ref, wk_ref, wkt_ref, wv_ref,
               wif_ref, bif_ref, ghn_ref, skip_ref, mkv_ref, wout_ref, gpost_ref,
               x1_ref, conv_out, c_out, n_out, m_out,
               h_s, u_s, xc_s, opre_s, zg_s, qm_s, zm_s, qkv_s, kt_s, gates_s, ymix_s,
               c_s, n_s, m_s):
    tt = x_ref.shape[1]
    nsub = tt // A_CHUNK
    i = pl.program_id(1)
    nt = pl.num_programs(1)

    @pl.when(i == 0)
    def _():
        u_s[0:8, :] = jnp.zeros((8, A_INNER), F32)
        c_s[...] = jnp.zeros(c_s.shape, F32)
        n_s[...] = jnp.zeros(n_s.shape, F32)
        m_s[...] = jnp.zeros(m_s.shape, F32)

    gpre = gpre_ref[...]

    def norm_body(c, _):
        r = pl.ds(pl.multiple_of(c * A_CHUNK, A_CHUNK), A_CHUNK)
        h_s[r, :] = (_rms_scale(x_ref[0, r, :]) * gpre).astype(BF16)
        return 0
    lax.fori_loop(0, nsub, norm_body, 0)

    hb = h_s[...]
    u_s[8:8 + tt, :] = _dot(hb, win_ref[:, 0:A_INNER])
    opre_s[...] = _dot(hb, win_ref[:, A_INNER:2 * A_INNER])
    zg_s[...] = _dot(hb, win_ref[:, 2 * A_INNER:3 * A_INNER])
    qm_s[...] = _dot(hb, win_ref[:, 3 * A_INNER:3 * A_INNER + M_WIDTH]).astype(BF16)
    zm_s[...] = _dot(hb, win_ref[:, 3 * A_INNER + M_WIDTH:3 * A_INNER + 2 * M_WIDTH])

    cw = convw_ref[...]
    cb = convb_ref[...]

    for c in range(nsub):
        r0 = c * A_CHUNK
        xc = cb + u_s[r0 + 5:r0 + 5 + A_CHUNK, :] * cw[0:1, :]
        xc = xc + u_s[r0 + 6:r0 + 6 + A_CHUNK, :] * cw[1:2, :]
        xc = xc + u_s[r0 + 7:r0 + 7 + A_CHUNK, :] * cw[2:3, :]
        xc = xc + u_s[r0 + 8:r0 + 8 + A_CHUNK, :] * cw[3:4, :]
        xc_s[r0:r0 + A_CHUNK, :] = _silu(xc)

    for h in range(A_HEADS):
        sl = slice(h * A_HDIM, (h + 1) * A_HDIM)
        xh = xc_s[:, sl].astype(BF16)
        uh = u_s[8:8 + tt, sl].astype(BF16)
        base = h * 3 * A_HDIM
        qkv_s[:, base:base + A_HDIM] = _dot(xh, wq_ref[h]).astype(BF16)
        qkv_s[:, base + A_HDIM:base + 2 * A_HDIM] = (_dot(xh, wk_ref[h]) * (A_HDIM ** -0.5)).astype(BF16)
        qkv_s[:, base + 2 * A_HDIM:base + 3 * A_HDIM] = _dot(uh, wv_ref[h]).astype(BF16)
        kt = (_dot_nt(wkt_ref[h], xh) * (A_HDIM ** -0.5)).astype(BF16)
        for c in range(nsub):
            kt_s[h, c] = kt[:, c * A_CHUNK:(c + 1) * A_CHUNK]
    gates_s[...] = _dot(qkv_s[...], wif_ref[...]) + bif_ref[...]

    row = lax.broadcasted_iota(jnp.int32, (A_CHUNK, A_CHUNK), 0)
    col = lax.broadcasted_iota(jnp.int32, (A_CHUNK, A_CHUNK), 1)
    causal = col <= row
    tri = jnp.where(causal, 1.0, 0.0).astype(BF16)

    def chunk_body(c, _):
        r0 = pl.multiple_of(c * A_CHUNK, A_CHUNK)
        rs = pl.ds(r0, A_CHUNK)
        g = gates_s[rs, :]
        ls = _log_sigmoid(g)
        t0 = ls.astype(BF16)
        e1 = ls - t0.astype(F32)
        t1 = e1.astype(BF16)
        t2 = (e1 - t1.astype(F32)).astype(BF16)
        bc = _dot(tri, t0) + _dot(tri, t1) + _dot(tri, t2)
        lane = lax.broadcasted_iota(jnp.int32, (A_CHUNK, LANES), 1)
        xt = jnp.where(lane < A_HEADS, g, bc).T
        heads = range(A_HEADS)
        b_col = [bc[:, 4 + h:5 + h] for h in heads]
        b_row = [xt[4 + h:5 + h, :] for h in heads]
        li_row = [xt[h:h + 1, :] for h in heads]
        li_col = [g[:, h:h + 1] for h in heads]
        m_old = [m_s[h:h + 1, 0:1] for h in heads]
        b_last = [bc[A_CHUNK - 1:A_CHUNK, 4 + h:5 + h] for h in heads]
        qh = [qkv_s[rs, h * 3 * A_HDIM:h * 3 * A_HDIM + A_HDIM] for h in heads]
        kh = [qkv_s[rs, h * 3 * A_HDIM + A_HDIM:h * 3 * A_HDIM + 2 * A_HDIM] for h in heads]
        vh = [qkv_s[rs, h * 3 * A_HDIM + 2 * A_HDIM:(h + 1) * 3 * A_HDIM] for h in heads]
        kt = [kt_s[h, c] for h in heads]
        c_old = [c_s[h] for h in heads]
        n_old = [n_s[h:h + 1, :] for h in heads]
        qk = [_dot_nt(qh[h], kh[h]) for h in heads]
        qc = [_dot(qh[h], c_old[h].astype(BF16)) for h in heads]
        dm = [jnp.where(causal, b_col[h] - b_row[h] + li_row[h], -jnp.inf) for h in heads]
        inter = [b_col[h] + m_old[h] for h in heads]
        m_row = [jnp.maximum(inter[h], jnp.max(dm[h], axis=-1, keepdims=True)) for h in heads]
        g_max = [jnp.max(b_last[h] - b_row[h] + li_row[h], axis=-1, keepdims=True) for h in heads]
        m_new = [jnp.maximum(b_last[h] + m_old[h], g_max[h]) for h in heads]
        sc = [qk[h] * jnp.exp(dm[h] - m_row[h]) for h in heads]
        dec = [jnp.exp(inter[h] - m_row[h]) for h in heads]
        ws_col = [jnp.exp(b_last[h] - b_col[h] + li_col[h] - m_new[h]) for h in heads]
        dc = [jnp.exp(b_last[h] + m_old[h] - m_new[h]) for h in heads]
        sv = [_dot(sc[h].astype(BF16), vh[h]) for h in heads]
        wv = [(ws_col[h] * vh[h].astype(F32)).astype(BF16) for h in heads]
        upd = [_dot(kt[h], wv[h]) for h in heads]
        hs = []
        for h in heads:
            den = (jnp.sum(sc[h], axis=-1, keepdims=True)
                   + dec[h] * jnp.sum(qh[h].astype(F32) * n_old[h], axis=-1, keepdims=True))
            num = sv[h] + dec[h] * qc[h]
            hs.append(num / jnp.maximum(jnp.abs(den), jnp.exp(-m_row[h])))
        for h in heads:
            c_s[h] = dc[h] * c_old[h] + upd[h]
            n_s[h:h + 1, :] = dc[h] * n_old[h] + jnp.sum(ws_col[h] * kh[h].astype(F32), axis=0, keepdims=True)
            m_s[h:h + 1, :] = jnp.broadcast_to(m_new[h], (1, LANES))

        parts = []
        for h in heads:
            v = _sigmoid(opre_s[rs, h * A_HDIM:(h + 1) * A_HDIM]) * hs[h]
            mu = jnp.mean(v, axis=-1, keepdims=True)
            var = jnp.mean(jnp.square(v - mu), axis=-1, keepdims=True)
            parts.append((v - mu) * lax.rsqrt(var + EPS))
        hn = jnp.concatenate(parts, axis=-1) * ghn
        y = hn + skp * xc_s[rs, :]
        ymix_s[rs, 0:A_INNER] = (y * _silu(zg_s[rs, :])).astype(BF16)
        ym = _mem_attention(qm_s[rs, :], mk, mv) * _silu(zm_s[rs, :])
        ymix_s[rs, A_INNER:A_INNER + M_WIDTH] = ym.astype(BF16)
        return 0

    ghn = ghn_ref[...]
    skp = skip_ref[...]
    mk = mkv_ref[0, :, 0:M_WIDTH]
    mv = mkv_ref[0, :, M_WIDTH:2 * M_WIDTH]
    lax.fori_loop(0, nsub, chunk_body, 0)

    out = _dot(ymix_s[...], wout_ref[...])
    x1_ref[0] = x_ref[0] + _rms_scale(out) * gpost_ref[...]

    u_s[0:8, :] = u_s[tt:tt + 8, :]

    @pl.when(i == nt - 1)
    def _():
        conv_out[0, 0] = u_s[tt + 5:tt + 8, :]
        for h in range(A_HEADS):
            c_out[0, 0, h] = c_s[h].T
        n_out[0, 0] = n_s[0:A_HEADS, :]
        m_out[0] = m_s[...]


def _layer0_prompt(x, g_pre, w_in, conv_w, conv_b, wq, wk, wkt, wv, wif, bif, ghn, skip, mkv_bf, w_out,
                   g_post):
    b, s, _ = x.shape
    tt = min(TOK_TILE, s)
    nt = s // tt
    a_in = w_in.shape[1]
    tile = lambda bb, i: (bb, i, 0)
    per_b = lambda bb, i: (bb, 0, 0)
    in_specs = [
        pl.BlockSpec((1, tt, D_MODEL), tile),
        _const_spec((1, D_MODEL)),
        _const_spec((D_MODEL, a_in)),
        _const_spec((CONV_W, A_INNER)),
        _const_spec((1, A_INNER)),
        _const_spec((A_HEADS, A_HDIM, A_HDIM)),
        _const_spec((A_HEADS, A_HDIM, A_HDIM)),
        _const_spec((A_HEADS, A_HDIM, A_HDIM)),
        _const_spec((A_HEADS, A_HDIM, A_HDIM)),
        _const_spec((3 * A_INNER, LANES)),
        _const_spec((1, LANES)),
        _const_spec((1, A_INNER)),
        _const_spec((1, A_INNER)),
        pl.BlockSpec((1, N_MEM, 2 * M_WIDTH), per_b),
        _const_spec((A_INNER + M_WIDTH, D_MODEL)),
        _const_spec((1, D_MODEL)),
    ]
    out_specs = [
        pl.BlockSpec((1, tt, D_MODEL), tile),
        pl.BlockSpec((1, 1, CONV_W - 1, A_INNER), lambda bb, i: (0, bb, 0, 0)),
        pl.BlockSpec((1, 1, A_HEADS, A_HDIM, A_HDIM), lambda bb, i: (0, bb, 0, 0, 0)),
        pl.BlockSpec((1, 1, A_HEADS, A_HDIM), lambda bb, i: (0, bb, 0, 0)),
        pl.BlockSpec((1, 8, LANES), per_b),
    ]
    out_shape = [
        jax.ShapeDtypeStruct((b, s, D_MODEL), F32),
        jax.ShapeDtypeStruct((1, b, CONV_W - 1, A_INNER), F32),
        jax.ShapeDtypeStruct((1, b, A_HEADS, A_HDIM, A_HDIM), F32),
        jax.ShapeDtypeStruct((1, b, A_HEADS, A_HDIM), F32),
        jax.ShapeDtypeStruct((b, 8, LANES), F32),
    ]
    scratch = [
        pltpu.VMEM((tt, D_MODEL), BF16),
        pltpu.VMEM((tt + 8, A_INNER), F32),
        pltpu.VMEM((tt, A_INNER), F32),
        pltpu.VMEM((tt, A_INNER), F32),
        pltpu.VMEM((tt, A_INNER), F32),
        pltpu.VMEM((tt, M_WIDTH), BF16),
        pltpu.VMEM((tt, M_WIDTH), F32),
        pltpu.VMEM((tt, 3 * A_INNER), BF16),
        pltpu.VMEM((A_HEADS, tt // A_CHUNK, A_HDIM, A_CHUNK), BF16),
        pltpu.VMEM((tt, LANES), F32),
        pltpu.VMEM((tt, A_INNER + M_WIDTH), BF16),
        pltpu.VMEM((A_HEADS, A_HDIM, A_HDIM), F32),
        pltpu.VMEM((8, A_HDIM), F32),
        pltpu.VMEM((8, LANES), F32),
    ]
    return pl.pallas_call(
        _l0_kernel,
        grid=(b, nt),
        in_specs=in_specs,
        out_specs=out_specs,
        out_shape=out_shape,
        scratch_shapes=scratch,
        compiler_params=pltpu.CompilerParams(
            dimension_semantics=("arbitrary", "arbitrary"), vmem_limit_bytes=VMEM_LIMIT),
        name="layer0_prompt",
    )(x, g_pre, w_in, conv_w, conv_b, wq, wk, wkt, wv, wif, bif, ghn, skip, mkv_bf, w_out, g_post)


def _rope_cols(x, cos, sin_signed):
    outs = []
    for cblk in range(x.shape[1] // B_HDIM):
        xb = x[:, cblk * B_HDIM:(cblk + 1) * B_HDIM]
        outs.append(xb * cos + pltpu.roll(xb, B_HDIM // 2, 1) * sin_signed)
    return jnp.concatenate(outs, axis=-1)


def _l1a_kernel(x_ref, gkv_ref, gpre_ref, wkv_ref, win_ref, cos_ref, sin_ref,
                q0_ref, q1_ref, q2_ref, k0_ref, k1_ref, k2_ref, v0_ref, v1_ref, v2_ref,
                zg_ref, qm_ref, zm_ref, w0_ref, w1_ref, w2_ref):
    tt = x_ref.shape[1]
    x = x_ref[0]
    xn = _rms_scale(x)
    hk = (xn * gkv_ref[...]).astype(BF16)
    hq = (xn * gpre_ref[...]).astype(BF16)
    cos = cos_ref[...]
    sin = sin_ref[...]
    q_refs = (q0_ref, q1_ref, q2_ref)
    k_refs = (k0_ref, k1_ref, k2_ref)
    v_refs = (v0_ref, v1_ref, v2_ref)
    w_refs = (w0_ref, w1_ref, w2_ref)
    for g in (2, 1, 0):
        d = B_GROUPS[g][1]
        kf = _rope_cols(_dot(hk, wkv_ref[:, g * 2 * B_WIDTH:g * 2 * B_WIDTH + B_WIDTH]), cos, sin)
        vf = _dot(hk, wkv_ref[:, g * 2 * B_WIDTH + B_WIDTH:(g + 1) * 2 * B_WIDTH])
        qf = _rope_cols(_dot(hq, win_ref[:, g * B_WIDTH:(g + 1) * B_WIDTH]), cos, sin)
        wr = w_refs[g]
        wrows = wr.shape[1]
        wr[0] = _rows_to_kv_heads(kf[tt - wrows:, :], vf[tt - wrows:, :])
        for val, ref in ((qf.astype(BF16), q_refs[g]), (kf.astype(BF16), k_refs[g]), (vf.astype(BF16), v_refs[g])):
            if d == 1:
                ref[0, 0] = val
            else:
                ref[0] = jnp.swapaxes(val.reshape(tt // d, d, val.shape[1]), 0, 1)
    qoff = N_GROUPS * B_WIDTH
    zg_ref[0] = _dot(hq, win_ref[:, qoff:qoff + B_WIDTH]).astype(BF16)
    qm_ref[0] = _dot(hq, win_ref[:, qoff + B_WIDTH:qoff + B_WIDTH + M_WIDTH]).astype(BF16)
    zm_ref[0] = _dot(hq, win_ref[:, qoff + B_WIDTH + M_WIDTH:qoff + B_WIDTH + 2 * M_WIDTH]).astype(BF16)


def _layer1_proj_prompt(x1, g_kv, g_pre, wkv, win, cos_t, sin_t):
    b, s, _ = x1.shape
    tt = min(TOK_TILE, s)
    nt = s // tt
    tile = lambda bb, i: (bb, i, 0)
    in_specs = [
        pl.BlockSpec((1, tt, D_MODEL), tile),
        _const_spec((1, D_MODEL)),
        _const_spec((1, D_MODEL)),
        _const_spec(wkv.shape),
        _const_spec(win.shape),
        pl.BlockSpec((tt, B_HDIM), lambda bb, i: (i, 0)),
        pl.BlockSpec((tt, B_HDIM), lambda bb, i: (i, 0)),
    ]
    qkv_specs, qkv_shapes = [], []
    for _ in range(3):
        for (_, d) in B_GROUPS:
            qkv_specs.append(pl.BlockSpec((1, d, tt // d, B_WIDTH), lambda bb, i: (bb, 0, i, 0)))
            qkv_shapes.append(jax.ShapeDtypeStruct((b, d, s // d, B_WIDTH), BF16))
    gate_specs = [pl.BlockSpec((1, tt, B_WIDTH), tile)] * 3
    gate_shapes = [jax.ShapeDtypeStruct((b, s, B_WIDTH), BF16)] * 3
    win_specs, win_shapes = [], []
    for (w, _) in B_GROUPS:
        wr = min(w, s)
        rows = min(wr, tt)
        nblk = wr // rows
        win_specs.append(pl.BlockSpec(
            (1, rows, 2 * B_HEADS, B_HDIM),
            functools.partial(lambda bb, i, nb: (bb, jnp.maximum(i - (nt - nb), 0), 0, 0), nb=nblk)))
        win_shapes.append(jax.ShapeDtypeStruct((b, wr, 2 * B_HEADS, B_HDIM), F32))
    return pl.pallas_call(
        _l1a_kernel,
        grid=(b, nt),
        in_specs=in_specs,
        out_specs=qkv_specs + gate_specs + win_specs,
        out_shape=qkv_shapes + gate_shapes + win_shapes,
        compiler_params=pltpu.CompilerParams(
            dimension_semantics=("arbitrary", "arbitrary"), vmem_limit_bytes=VMEM_LIMIT),
        name="layer1_proj_prompt",
    )(x1, g_kv, g_pre, wkv, win, cos_t, sin_t)


def _cols_to_lanes(cols):
    t = cols[0].shape[0]
    lane = lax.broadcasted_iota(jnp.int32, (t, LANES), 1)
    acc = jnp.zeros((t, LANES), F32)
    for h, cvec in enumerate(cols):
        acc = jnp.where(lane == h, cvec, acc)
    return acc


def _band_attn_kernel(q_ref, kc_ref, kp_ref, vc_ref, vp_ref, o_ref, lse_ref):
    tq = q_ref.shape[2]
    nsb = tq // ATT_BLK
    j = pl.program_id(2)
    row = lax.broadcasted_iota(jnp.int32, (ATT_BLK, 2 * ATT_BLK), 0)
    col = lax.broadcasted_iota(jnp.int32, (ATT_BLK, 2 * ATT_BLK), 1)
    band = jnp.logical_and(col >= row, col <= row + ATT_BLK)
    first_pen = jnp.where(col < ATT_BLK, jnp.where(j > 0, 0.0, -jnp.inf), 0.0)
    scale = B_HDIM ** -0.5
    qs, ks, vs = [], [], []
    for sb in range(nsb):
        rs = slice(sb * ATT_BLK, (sb + 1) * ATT_BLK)
        ps = slice((sb - 1) * ATT_BLK, sb * ATT_BLK)
        for h in range(B_HEADS):
            hs = slice(h * B_HDIM, (h + 1) * B_HDIM)
            qs.append(q_ref[0, 0, rs, hs])
            kp = kp_ref[0, 0, :, hs] if sb == 0 else kc_ref[0, 0, ps, hs]
            vp = vp_ref[0, 0, :, hs] if sb == 0 else vc_ref[0, 0, ps, hs]
            ks.append(jnp.concatenate([kp, kc_ref[0, 0, rs, hs]], axis=0))
            vs.append(jnp.concatenate([vp, vc_ref[0, 0, rs, hs]], axis=0))
    q3 = jnp.stack(qs)
    k3 = jnp.stack(ks)
    v3 = jnp.stack(vs)
    s = jnp.einsum('uqd,ukd->uqk', q3, k3, preferred_element_type=F32) * scale
    s = jnp.concatenate([s[0:B_HEADS] + first_pen[None], s[B_HEADS:]], axis=0)
    s = jnp.where(band[None], s, -jnp.inf)
    mx = jnp.max(s, axis=-1, keepdims=True)
    p = jnp.exp(s - mx)
    l = jnp.sum(p, axis=-1, keepdims=True)
    o = jnp.einsum('uqk,ukd->uqd', (p / l).astype(BF16), v3, preferred_element_type=F32)
    lse = mx + jnp.log(l)
    for sb in range(nsb):
        rs = slice(sb * ATT_BLK, (sb + 1) * ATT_BLK)
        for h in range(B_HEADS):
            o_ref[0, 0, rs, h * B_HDIM:(h + 1) * B_HDIM] = o[sb * B_HEADS + h].astype(BF16)
        lse_ref[0, 0, rs, :] = _cols_to_lanes([lse[sb * B_HEADS + h] for h in range(B_HEADS)])


def _band_attention(q, k, v):
    b, d, ls, _ = q.shape
    tq = min(TOK_TILE, ls)
    nj = ls // tq
    ratio = tq // ATT_BLK
    cur = lambda bb, r, j: (bb, r, j, 0)
    prev = lambda bb, r, j: (bb, r, jnp.maximum(j * ratio - 1, 0), 0)
    return pl.pallas_call(
        _band_attn_kernel,
        grid=(b, d, nj),
        in_specs=[pl.BlockSpec((1, 1, tq, B_WIDTH), cur),
                  pl.BlockSpec((1, 1, tq, B_WIDTH), cur),
                  pl.BlockSpec((1, 1, ATT_BLK, B_WIDTH), prev),
                  pl.BlockSpec((1, 1, tq, B_WIDTH), cur),
                  pl.BlockSpec((1, 1, ATT_BLK, B_WIDTH), prev)],
        out_specs=[pl.BlockSpec((1, 1, tq, B_WIDTH), cur),
                   pl.BlockSpec((1, 1, tq, LANES), cur)],
        out_shape=[jax.ShapeDtypeStruct((b, d, ls, B_WIDTH), BF16),
                   jax.ShapeDtypeStruct((b, d, ls, LANES), F32)],
        compiler_params=pltpu.CompilerParams(
            dimension_semantics=("arbitrary", "arbitrary", "arbitrary"), vmem_limit_bytes=VMEM_LIMIT),
        name="band_attention_d%d" % d,
    )(q, k, k, v, v)


def _unpermute(ref):
    d, rows, width = ref.shape[1:]
    if d == 1:
        return ref[0, 0].astype(F32)
    return jnp.swapaxes(ref[0], 0, 1).reshape(d * rows, width).astype(F32)


def _l1c_kernel(x_ref, o0_ref, o1_ref, o2_ref, l0_ref, l1_ref, l2_ref, zg_ref, qm_ref, zm_ref,
                mkv_ref, wout_ref, gpost_ref, y_ref):
    tt = x_ref.shape[1]
    o_refs = (o0_ref, o1_ref, o2_ref)
    l_refs = (l0_ref, l1_ref, l2_ref)
    outs, lses = [], []
    for g, (_, d) in enumerate(B_GROUPS):
        outs.append(_unpermute(o_refs[g]))
        lses.append(_unpermute(l_refs[g])[:, 0:B_HEADS])
    mx = jnp.maximum(jnp.maximum(lses[0], lses[1]), lses[2])
    es = [jnp.exp(l - mx) for l in lses]
    tot = es[0] + es[1] + es[2]
    ws = [e / tot for e in es]
    parts = []
    for h in range(B_HEADS):
        hs = slice(h * B_HDIM, (h + 1) * B_HDIM)
        acc = ws[0][:, h:h + 1] * outs[0][:, hs]
        acc = acc + ws[1][:, h:h + 1] * outs[1][:, hs]
        acc = acc + ws[2][:, h:h + 1] * outs[2][:, hs]
        parts.append(acc)
    ydil = jnp.concatenate(parts, axis=-1)
    ymix = (ydil * _silu(zg_ref[0].astype(F32))).astype(BF16)
    mk = mkv_ref[0, :, 0:M_WIDTH]
    mv = mkv_ref[0, :, M_WIDTH:2 * M_WIDTH]
    ym = (_mem_attention(qm_ref[0], mk, mv) * _silu(zm_ref[0].astype(F32))).astype(BF16)
    out = _dot(ymix, wout_ref[0:B_WIDTH, :]) + _dot(ym, wout_ref[B_WIDTH:B_WIDTH + M_WIDTH, :])
    y_ref[0] = x_ref[0] + _rms_scale(out) * gpost_ref[...]


def _layer1_out_prompt(x1, os_, ls_, zg, qm, zm, mkv_bf, w_out, g_post):
    b, s, _ = x1.shape
    tt = min(TOK_TILE, s)
    nt = s // tt
    tile = lambda bb, i: (bb, i, 0)
    perm = lambda bb, i: (bb, 0, i, 0)
    in_specs = [pl.BlockSpec((1, tt, D_MODEL), tile)]
    for width in (B_WIDTH, LANES):
        for (_, d) in B_GROUPS:
            in_specs.append(pl.BlockSpec((1, d, tt // d, width), perm))
    in_specs += [pl.BlockSpec((1, tt, B_WIDTH), tile)] * 3
    in_specs += [pl.BlockSpec((1, N_MEM, 2 * M_WIDTH), lambda bb, i: (bb, 0, 0)),
                 _const_spec(w_out.shape), _const_spec((1, D_MODEL))]
    return pl.pallas_call(
        _l1c_kernel,
        grid=(b, nt),
        in_specs=in_specs,
        out_specs=pl.BlockSpec((1, tt, D_MODEL), tile),
        out_shape=jax.ShapeDtypeStruct((b, s, D_MODEL), F32),
        compiler_params=pltpu.CompilerParams(
            dimension_semantics=("arbitrary", "arbitrary"), vmem_limit_bytes=VMEM_LIMIT),
        name="layer1_out_prompt",
    )(x1, *os_, *ls_, zg, qm, zm, mkv_bf, w_out, g_post)


def _rope_tables(pos):
    half = B_HDIM // 2
    inv = ROPE_THETA ** (-jnp.arange(half, dtype=F32) / half)
    ang = pos[:, None] * inv[None, :]
    cos = jnp.cos(ang)
    sin = jnp.sin(ang)
    return jnp.concatenate([cos, cos], axis=-1), jnp.concatenate([-sin, sin], axis=-1)


def _prompt_group(x_prompt, mem_prompt, p):
    b, s, _ = x_prompt.shape
    memkv_f, memkv_b = _memkv(mem_prompt.reshape(b * N_MEM, D_MODEL), p['w_mkv'])
    depth = memkv_f.shape[0]
    memkv_b = memkv_b.reshape(depth, b, N_MEM, 2 * M_WIDTH)
    x1, conv_p, c_p, n_p, m_pad = _layer0_prompt_pipelined(
        x_prompt, p['g_pre'][0:1], p['w_in_a'][0], p['conv_w_a'][0], p['conv_b_a'], p['w_q_a'][0],
        p['w_k_a'][0], jnp.swapaxes(p['w_k_a'][0], 1, 2), p['w_v_a'][0], p['w_if_a'], p['b_if_a'],
        p['g_hn_a'], p['skip_a'], memkv_b[0], p['w_out_a'][0], p['g_post'][0:1])
    cos_t, sin_t = _rope_tables(jnp.arange(s, dtype=F32))
    outs = _layer1_proj_prompt(x1, p['g_kv'], p['g_pre'][1:2], p['w_kv_b'], p['w_in_b'][0], cos_t, sin_t)
    qs, ks, vs = outs[0:3], outs[3:6], outs[6:9]
    zg, qm, zm = outs[9:12]
    wins = outs[12:15]
    os_, ls_ = [], []
    for g in range(N_GROUPS):
        o, l = _band_attention(qs[g], ks[g], vs[g])
        os_.append(o)
        ls_.append(l)
    y = _layer1_out_prompt(x1, os_, ls_, zg, qm, zm, memkv_b[1], p['w_out_b'][0], p['g_post'][1:2])
    m_p = m_pad[:, 0:A_HEADS, 0][None]
    wins = [w.reshape(b, w.shape[1], 2, B_HEADS, B_HDIM) for w in wins]
    memkv_p = memkv_f.reshape(depth, b, N_MEM, 2, M_HEADS, M_HDIM)
    return y, conv_p, c_p, n_p, m_p, wins, memkv_p


def _prep_params(g_pre, g_post, w_in_a, conv_w_a, conv_b_a, w_q_a, w_k_a, w_v_a, w_if_a, b_if_a,
                 g_hn_a, skip_a, w_out_a, g_kv, w_kv_b, w_in_b, w_out_b, w_mkv):
    wif = jnp.pad(w_if_a[0], ((0, 0), (0, LANES - 2 * A_HEADS))).astype(BF16)
    bif = jnp.pad(b_if_a[0], (0, LANES - 2 * A_HEADS))[None, :]
    return {
        'g_pre': g_pre, 'g_post': g_post,
        'w_in_a': w_in_a.astype(BF16), 'conv_w_a': conv_w_a, 'conv_b_a': conv_b_a,
        'w_q_a': w_q_a.astype(BF16), 'w_k_a': w_k_a.astype(BF16), 'w_v_a': w_v_a.astype(BF16),
        'w_if_a': wif, 'b_if_a': bif, 'g_hn_a': g_hn_a, 'skip_a': skip_a,
        'w_out_a': w_out_a.astype(BF16), 'g_kv': g_kv[None, :], 'w_kv_b': w_kv_b.astype(BF16),
        'w_in_b': w_in_b.astype(BF16), 'w_out_b': w_out_b.astype(BF16), 'w_mkv': w_mkv.astype(BF16),
    }


def _dec_l0_proj_kernel(x_ref, gpre_ref, win_ref, cst_ref, convw_ref, convb_ref, wq_ref, wk_ref, wv_ref,
                        wif_ref, bif_ref,
                        q_ref, k_ref, v_ref, gates_ref, xc_ref, opre_ref, zg_ref, qm_ref, zm_ref, cnew_ref):
    h = (_rms_scale(x_ref[...]) * gpre_ref[...]).astype(BF16)
    u = _dot(h, win_ref[:, 0:A_INNER])
    opre_ref[...] = _dot(h, win_ref[:, A_INNER:2 * A_INNER])
    zg_ref[...] = _dot(h, win_ref[:, 2 * A_INNER:3 * A_INNER])
    qm_ref[...] = _dot(h, win_ref[:, 3 * A_INNER:3 * A_INNER + M_WIDTH])
    zm_ref[...] = _dot(h, win_ref[:, 3 * A_INNER + M_WIDTH:3 * A_INNER + 2 * M_WIDTH])
    cw = convw_ref[...]
    xc = convb_ref[...] + cst_ref[0] * cw[0:1, :]
    xc = xc + cst_ref[1] * cw[1:2, :]
    xc = xc + cst_ref[2] * cw[2:3, :]
    xc = xc + u * cw[3:4, :]
    xc = _silu(xc)
    xc_ref[...] = xc
    cnew_ref[0] = cst_ref[1]
    cnew_ref[1] = cst_ref[2]
    cnew_ref[2] = u
    qs, ks, vs, cat = [], [], [], []
    for hd in range(A_HEADS):
        sl = slice(hd * A_HDIM, (hd + 1) * A_HDIM)
        xh = xc[:, sl].astype(BF16)
        qh = _dot(xh, wq_ref[hd])
        kh = _dot(xh, wk_ref[hd]) * (A_HDIM ** -0.5)
        vh = _dot(u[:, sl].astype(BF16), wv_ref[hd])
        qs.append(qh)
        ks.append(kh)
        vs.append(vh)
        cat += [qh.astype(BF16), kh.astype(BF16), vh.astype(BF16)]
    q_ref[...] = jnp.concatenate(qs, axis=-1)
    k_ref[...] = jnp.concatenate(ks, axis=-1)
    v_ref[...] = jnp.concatenate(vs, axis=-1)
    gates_ref[...] = _dot(jnp.concatenate(cat, axis=-1), wif_ref[...]) + bif_ref[...]


def _whole(shape):
    nd = len(shape)
    return pl.BlockSpec(shape, lambda *_: (0,) * nd)


def _dec_l0_proj(x, g_pre, w_in, cst, conv_w, conv_b, wq, wk, wv, wif, bif):
    nb = x.shape[0]
    args = (x, g_pre, w_in, cst, conv_w, conv_b, wq, wk, wv, wif, bif)
    f = lambda *s: jax.ShapeDtypeStruct(s, F32)
    out_shape = [f(nb, A_INNER), f(nb, A_INNER), f(nb, A_INNER), f(nb, LANES), f(nb, A_INNER), f(nb, A_INNER),
                 f(nb, A_INNER), f(nb, M_WIDTH), f(nb, M_WIDTH), f(CONV_W - 1, nb, A_INNER)]
    return pl.pallas_call(
        _dec_l0_proj_kernel,
        grid=(1,),
        in_specs=[_whole(a.shape) for a in args],
        out_specs=[_whole(o.shape) for o in out_shape],
        out_shape=out_shape,
        compiler_params=pltpu.CompilerParams(dimension_semantics=("arbitrary",), vmem_limit_bytes=VMEM_LIMIT),
        name="dec_l0_proj",
    )(*args)


def _row_to_col(row, eye):
    return jnp.sum(jnp.where(eye, row, 0.0), axis=-1, keepdims=True)


def _col_to_row(colv, eye):
    return jnp.sum(jnp.where(eye, colv, 0.0), axis=0, keepdims=True)


def _dec_mem_attention(q, kv_ref_view):
    kk = kv_ref_view[:, 0]
    vv = kv_ref_view[:, 1]
    s = jnp.sum(kk * q[None], axis=-1, keepdims=True) * (M_HDIM ** -0.5)
    mx = jnp.max(s, axis=0, keepdims=True)
    p = jnp.exp(s - mx)
    p = p / jnp.sum(p, axis=0, keepdims=True)
    return jnp.sum(p * vv, axis=0)


def _dec_mlstm_kernel(q_ref, k_ref, v_ref, gates_ref, m_ref, c_ref, n_ref, qm_ref, kv_ref,
                      hs_ref, c_out, n_out, m_out, ym_ref):
    b = pl.program_id(0)
    rb = pl.ds(b, 1)
    g = gates_ref[rb, :]
    mrow = m_ref[rb, :]
    r = lax.broadcasted_iota(jnp.int32, (A_HDIM, A_HDIM), 0)
    c = lax.broadcasted_iota(jnp.int32, (A_HDIM, A_HDIM), 1)
    eye = r == c
    lane = lax.broadcasted_iota(jnp.int32, (1, LANES), 1)
    m_acc = jnp.zeros((1, LANES), F32)
    for h in range(A_HEADS):
        sl = slice(h * A_HDIM, (h + 1) * A_HDIM)
        qh = q_ref[rb, sl]
        kh = k_ref[rb, sl]
        vh = v_ref[rb, sl]
        c_old = c_ref[0, 0, h]
        n_old = n_ref[0, 0, h:h + 1, :]
        li = g[:, h:h + 1]
        lf = _log_sigmoid(g[:, 4 + h:5 + h])
        m_old = mrow[:, h:h + 1]
        cq = jnp.sum(c_old * qh, axis=-1, keepdims=True)
        nq = jnp.sum(n_old * qh, axis=-1, keepdims=True)
        qk = jnp.sum(qh * kh, axis=-1, keepdims=True)
        inter = lf + m_old
        m_new = jnp.maximum(inter, li)
        ws = jnp.exp(li - m_new)
        dec = jnp.exp(inter - m_new)
        sc = qk * ws
        v_col = _row_to_col(vh, eye)
        den = sc + dec * nq
        h_col = (sc * v_col + dec * cq) / jnp.maximum(jnp.abs(den), jnp.exp(-m_new))
        hs_ref[0, :, sl] = _col_to_row(h_col, eye)
        c_out[0, 0, h] = dec * c_old + (ws * v_col) * kh
        n_out[0, 0, h:h + 1, :] = dec * n_old + ws * kh
        m_acc = m_acc + jnp.where(lane == h, m_new, 0.0)
    m_out[0] = m_acc
    ym_ref[0] = _dec_mem_attention(qm_ref[0], kv_ref.at[0, 0])


def _dec_mlstm(q, k, v, gates, m_in, state_c, state_n, qm3, cache_mem_kv):
    nb = q.shape[0]
    per_b3 = lambda b: (b, 0, 0)
    in_specs = [_whole(q.shape), _whole(k.shape), _whole(v.shape), _whole(gates.shape), _whole(m_in.shape),
                pl.BlockSpec((1, 1, A_HEADS, A_HDIM, A_HDIM), lambda b: (0, b, 0, 0, 0)),
                pl.BlockSpec((1, 1, A_HEADS, A_HDIM), lambda b: (0, b, 0, 0)),
                pl.BlockSpec((1, M_HEADS, M_HDIM), per_b3),
                pl.BlockSpec((1, 1, N_MEM, 2, M_HEADS, M_HDIM), lambda b: (0, b, 0, 0, 0, 0))]
    out_specs = [pl.BlockSpec((1, 1, A_INNER), per_b3),
                 pl.BlockSpec((1, 1, A_HEADS, A_HDIM, A_HDIM), lambda b: (0, b, 0, 0, 0)),
                 pl.BlockSpec((1, 1, A_HEADS, A_HDIM), lambda b: (0, b, 0, 0)),
                 pl.BlockSpec((1, 1, LANES), per_b3),
                 pl.BlockSpec((1, M_HEADS, M_HDIM), per_b3)]
    out_shape = [jax.ShapeDtypeStruct((nb, 1, A_INNER), F32),
                 jax.ShapeDtypeStruct(state_c.shape, F32),
                 jax.ShapeDtypeStruct(state_n.shape, F32),
                 jax.ShapeDtypeStruct((nb, 1, LANES), F32),
                 jax.ShapeDtypeStruct((nb, M_HEADS, M_HDIM), F32)]
    return pl.pallas_call(
        _dec_mlstm_kernel,
        grid=(nb,),
        in_specs=in_specs,
        out_specs=out_specs,
        out_shape=out_shape,
        compiler_params=pltpu.CompilerParams(dimension_semantics=("arbitrary",), vmem_limit_bytes=VMEM_LIMIT),
        name="dec_mlstm",
    )(q, k, v, gates, m_in, state_c, state_n, qm3, cache_mem_kv)


def _dec_mid_kernel(hs_ref, opre_ref, xc_ref, zg_ref, ym_ref, zm_ref, x_ref, ghn_ref, skip_ref, wout_ref,
                    gpost_ref, gkv_ref, gpre_ref, wkv_ref, win_ref, cos_ref, sin_ref,
                    x1_ref, q_ref, k_ref, v_ref, zg1_ref, qm1_ref, zm1_ref):
    hh = _sigmoid(opre_ref[...]) * hs_ref[...]
    parts = []
    for h in range(A_HEADS):
        v = hh[:, h * A_HDIM:(h + 1) * A_HDIM]
        mu = jnp.mean(v, axis=-1, keepdims=True)
        var = jnp.mean(jnp.square(v - mu), axis=-1, keepdims=True)
        parts.append((v - mu) * lax.rsqrt(var + EPS))
    y = jnp.concatenate(parts, axis=-1) * ghn_ref[...] + skip_ref[...] * xc_ref[...]
    ymix = (y * _silu(zg_ref[...])).astype(BF16)
    ym = (ym_ref[...] * _silu(zm_ref[...])).astype(BF16)
    out = _dot(ymix, wout_ref[0:A_INNER, :]) + _dot(ym, wout_ref[A_INNER:A_INNER + M_WIDTH, :])
    x1 = x_ref[...] + _rms_scale(out) * gpost_ref[...]
    x1_ref[...] = x1
    xn = _rms_scale(x1)
    hk = (xn * gkv_ref[...]).astype(BF16)
    hq = (xn * gpre_ref[...]).astype(BF16)
    cos = cos_ref[...]
    sin = sin_ref[...]
    ks, vs = [], []
    for g in range(N_GROUPS):
        ks.append(_rope_cols(_dot(hk, wkv_ref[:, g * 2 * B_WIDTH:g * 2 * B_WIDTH + B_WIDTH]), cos, sin))
        vs.append(_dot(hk, wkv_ref[:, g * 2 * B_WIDTH + B_WIDTH:(g + 1) * 2 * B_WIDTH]))
    k_ref[...] = jnp.concatenate(ks, axis=-1)
    v_ref[...] = jnp.concatenate(vs, axis=-1)
    qoff = N_GROUPS * B_WIDTH
    q_ref[...] = _rope_cols(_dot(hq, win_ref[:, 0:qoff]), cos, sin)
    zg1_ref[...] = _dot(hq, win_ref[:, qoff:qoff + B_WIDTH])
    qm1_ref[...] = _dot(hq, win_ref[:, qoff + B_WIDTH:qoff + B_WIDTH + M_WIDTH])
    zm1_ref[...] = _dot(hq, win_ref[:, qoff + B_WIDTH + M_WIDTH:qoff + B_WIDTH + 2 * M_WIDTH])


def _dec_mid(hs, opre, xc, zg, ym, zm, x, ghn, skip, w_out, g_post, g_kv, g_pre, wkv, win, cos, sin):
    nb = x.shape[0]
    args = (hs, opre, xc, zg, ym, zm, x, ghn, skip, w_out, g_post, g_kv, g_pre, wkv, win, cos, sin)
    f = lambda *s: jax.ShapeDtypeStruct(s, F32)
    out_shape = [f(nb, D_MODEL), f(nb, N_GROUPS * B_WIDTH), f(nb, N_GROUPS * B_WIDTH), f(nb, N_GROUPS * B_WIDTH),
                 f(nb, B_WIDTH), f(nb, M_WIDTH), f(nb, M_WIDTH)]
    return pl.pallas_call(
        _dec_mid_kernel,
        grid=(1,),
        in_specs=[_whole(a.shape) for a in args],
        out_specs=[_whole(o.shape) for o in out_shape],
        out_shape=out_shape,
        compiler_params=pltpu.CompilerParams(dimension_semantics=("arbitrary",), vmem_limit_bytes=VMEM_LIMIT),
        name="dec_mid",
    )(*args)


def _dec_attn_kernel(q_ref, kn_ref, vn_ref, w0_ref, w1_ref, w2_ref, qm_ref, kv_ref, ydil_ref, ym_ref):
    w_refs = (w0_ref, w1_ref, w2_ref)
    scale = B_HDIM ** -0.5
    outs, lses = [], []
    for g in range(N_GROUPS):
        q = q_ref[0, g]
        kn = kn_ref[0, g]
        vn = vn_ref[0, g]
        wv = w_refs[g].at[0]
        kk = wv[:, 0]
        vv = wv[:, 1]
        s_c = jnp.sum(kk * q[None], axis=-1, keepdims=True) * scale
        s_n = jnp.sum(kn * q, axis=-1, keepdims=True) * scale
        mx = jnp.maximum(jnp.max(s_c, axis=0), s_n)
        p_c = jnp.exp(s_c - mx[None])
        p_n = jnp.exp(s_n - mx)
        l = jnp.sum(p_c, axis=0) + p_n
        outs.append(jnp.sum((p_c / l[None]) * vv, axis=0) + (p_n / l) * vn)
        lses.append(mx + jnp.log(l))
    mx = jnp.maximum(jnp.maximum(lses[0], lses[1]), lses[2])
    es = [jnp.exp(l - mx) for l in lses]
    tot = es[0] + es[1] + es[2]
    ydil_ref[0] = (es[0] / tot) * outs[0] + (es[1] / tot) * outs[1] + (es[2] / tot) * outs[2]
    ym_ref[0] = _dec_mem_attention(qm_ref[0], kv_ref.at[0, 0])


def _dec_attn(q4, kn4, vn4, cw0, cw1, cw2, qm3, cache_mem_kv, layer):
    nb = q4.shape[0]
    per_b3 = lambda b: (b, 0, 0)
    per_b4 = lambda b: (b, 0, 0, 0)
    rows = B_GROUPS[0][0]
    win_specs = [pl.BlockSpec((1, rows, 2, B_HEADS, B_HDIM), lambda b: (b, 0, 0, 0, 0))]
    for cw in (cw1, cw2):
        win_specs.append(pl.BlockSpec((1, rows, None, 2, B_HEADS, B_HDIM), lambda b: (b, 0, 0, 0, 0, 0)))
    in_specs = [pl.BlockSpec((1, N_GROUPS, B_HEADS, B_HDIM), per_b4)] * 3 + win_specs + [
        pl.BlockSpec((1, M_HEADS, M_HDIM), per_b3),
        pl.BlockSpec((1, 1, N_MEM, 2, M_HEADS, M_HDIM), lambda b: (layer, b, 0, 0, 0, 0))]
    return pl.pallas_call(
        _dec_attn_kernel,
        grid=(nb,),
        in_specs=in_specs,
        out_specs=[pl.BlockSpec((1, B_HEADS, B_HDIM), per_b3), pl.BlockSpec((1, M_HEADS, M_HDIM), per_b3)],
        out_shape=[jax.ShapeDtypeStruct((nb, B_HEADS, B_HDIM), F32),
                   jax.ShapeDtypeStruct((nb, M_HEADS, M_HDIM), F32)],
        compiler_params=pltpu.CompilerParams(dimension_semantics=("arbitrary",), vmem_limit_bytes=VMEM_LIMIT),
        name="dec_attn",
    )(q4, kn4, vn4, cw0, cw1, cw2, qm3, cache_mem_kv)


def _dec_out_kernel(ydil_ref, zg_ref, ym_ref, zm_ref, x_ref, wout_ref, gpost_ref, y_ref):
    ymix = (ydil_ref[...] * _silu(zg_ref[...])).astype(BF16)
    ym = (ym_ref[...] * _silu(zm_ref[...])).astype(BF16)
    out = _dot(ymix, wout_ref[0:B_WIDTH, :]) + _dot(ym, wout_ref[B_WIDTH:B_WIDTH + M_WIDTH, :])
    y_ref[...] = x_ref[...] + _rms_scale(out) * gpost_ref[...]


def _dec_out(ydil, zg, ym, zm, x1, w_out, g_post):
    args = (ydil, zg, ym, zm, x1, w_out, g_post)
    return pl.pallas_call(
        _dec_out_kernel,
        grid=(1,),
        in_specs=[_whole(a.shape) for a in args],
        out_specs=_whole(x1.shape),
        out_shape=jax.ShapeDtypeStruct(x1.shape, F32),
        compiler_params=pltpu.CompilerParams(dimension_semantics=("arbitrary",), vmem_limit_bytes=VMEM_LIMIT),
        name="dec_out",
    )(*args)


def _sample_group(x_sample, state_conv, state_c, state_n, state_m, cache_wins, cache_mem_kv, p):
    nb = x_sample.shape[0]
    x = x_sample.reshape(nb, D_MODEL)
    cst = state_conv[0].transpose(1, 0, 2)
    q, k, v, gates, xc, opre, zg, qm, zm, cnew = _dec_l0_proj(
        x, p['g_pre'][0:1], p['w_in_a'][0], cst, p['conv_w_a'][0], p['conv_b_a'], p['w_q_a'][0],
        p['w_k_a'][0], p['w_v_a'][0], p['w_if_a'], p['b_if_a'])
    m_in = jnp.pad(state_m[0], ((0, 0), (0, LANES - A_HEADS)))
    hs, c_s, n_s, m_pad, ym0 = _dec_mlstm(q, k, v, gates, m_in, state_c, state_n,
                                          qm.reshape(nb, M_HEADS, M_HDIM), cache_mem_kv)
    pos = PAST_LEN + jnp.arange(1, dtype=F32)
    cos, sin = _rope_tables(pos)
    x1, qd, kn, vn, zg1, qm1, zm1 = _dec_mid(
        hs.reshape(nb, A_INNER), opre, xc, zg, ym0.reshape(nb, M_WIDTH), zm, x, p['g_hn_a'], p['skip_a'],
        p['w_out_a'][0], p['g_post'][0:1], p['g_kv'], p['g_pre'][1:2], p['w_kv_b'], p['w_in_b'][0], cos, sin)
    shp4 = (nb, N_GROUPS, B_HEADS, B_HDIM)
    kn4 = kn.reshape(shp4)
    vn4 = vn.reshape(shp4)
    cws = [cache_wins[0]]
    for g in (1, 2):
        w, d = B_GROUPS[g]
        cws.append(cache_wins[g].reshape(nb, w // d, d, 2, B_HEADS, B_HDIM))
    ydil, ym1 = _dec_attn(qd.reshape(shp4), kn4, vn4, cws[0], cws[1], cws[2],
                          qm1.reshape(nb, M_HEADS, M_HDIM), cache_mem_kv, 1)
    y = _dec_out(ydil.reshape(nb, B_WIDTH), zg1, ym1.reshape(nb, M_WIDTH), zm1, x1, p['w_out_b'][0],
                 p['g_post'][1:2])
    conv_s = cnew.transpose(1, 0, 2)[None]
    m_s = m_pad[:, 0, 0:A_HEADS][None]
    wins_s = [jnp.stack([kn4[:, g], vn4[:, g]], axis=1)[:, None] for g in range(N_GROUPS)]
    return y.reshape(nb, 1, D_MODEL), conv_s, c_s, n_s, m_s, wins_s


def kernel(x_prompt, x_sample, mem_prompt, state_conv, state_C, state_n, state_m, cache_win0, cache_win1,
           cache_win2, cache_mem_kv, g_pre, g_post, w_in_a, conv_w_a, conv_b_a, w_q_a, w_k_a, w_v_a, w_if_a,
           b_if_a, g_hn_a, skip_a, w_out_a, g_kv, w_kv_b, w_in_b, w_out_b, w_mkv):
    p = _prep_params(g_pre, g_post, w_in_a, conv_w_a, conv_b_a, w_q_a, w_k_a, w_v_a, w_if_a, b_if_a,
                     g_hn_a, skip_a, w_out_a, g_kv, w_kv_b, w_in_b, w_out_b, w_mkv)
    y_p, conv_p, c_p, n_p, m_p, wins_p, memkv_p = _prompt_group(x_prompt, mem_prompt, p)
    y_s, conv_s, c_s, n_s, m_s, wins_s = _sample_group(
        x_sample, state_conv, state_C, state_n, state_m, (cache_win0, cache_win1, cache_win2), cache_mem_kv, p)
    return (y_p, y_s, conv_p, c_p, n_p, m_p, wins_p[0], wins_p[1], wins_p[2], memkv_p,
            conv_s, c_s, n_s, m_s, wins_s[0], wins_s[1], wins_s[2])
```

```python
import functools

import jax
import jax.numpy as jnp
from jax import lax
from jax.experimental import pallas as pl
from jax.experimental.pallas import tpu as pltpu

F32 = jnp.float32
BF16 = jnp.bfloat16

D_MODEL = 1024
A_HEADS = 4
A_HDIM = 256
A_INNER = 1024
CONV_W = 4
A_CHUNK = 128
B_GROUPS = ((128, 1), (512, 4), (2048, 16))
N_GROUPS = 3
B_HEADS = 4
B_HDIM = 128
B_WIDTH = 512
N_MEM = 256
M_HEADS = 4
M_HDIM = 128
M_WIDTH = 512
ROPE_THETA = 10000.0
EPS = 1e-6
PAST_LEN = 8192

LANES = 128
TOK_TILE = 512
L0_TILE = 256
ATT_BLK = 128
VMEM_LIMIT = 56 * 1024 * 1024

NT_DIMS = (((1,), (1,)), ((), ()))
LOG2E = 1.4426950408889634


def _dot(a, b):
    return jnp.dot(a, b, preferred_element_type=F32)


def _dot_nt(a, b):
    return lax.dot_general(a, b, NT_DIMS, preferred_element_type=F32)


def _sigmoid(x):
    return 1.0 / (1.0 + jnp.exp(-x))


def _silu(x):
    return x * _sigmoid(x)


def _log_sigmoid(x):
    return jnp.minimum(x, 0.0) - jnp.log(1.0 + jnp.exp(-jnp.abs(x)))


def _rms_scale(x):
    return x * lax.rsqrt(jnp.mean(x * x, axis=-1, keepdims=True) + EPS)


def _const_spec(shape):
    nd = len(shape)
    return pl.BlockSpec(shape, lambda *_: (0,) * nd, pipeline_mode=pl.Buffered(1))


def _mem_attention(qm, mk, mv):
    heads = range(M_HEADS)
    sl = [slice(h * M_HDIM, (h + 1) * M_HDIM) for h in heads]
    s = [_dot_nt(qm[:, sl[h]], mk[:, sl[h]]) * (M_HDIM ** -0.5) for h in heads]
    mx = [jnp.max(s[h], axis=-1, keepdims=True) for h in heads]
    p = [jnp.exp(s[h] - mx[h]) for h in heads]
    l = [jnp.sum(p[h], axis=-1, keepdims=True) for h in heads]
    outs = [_dot((p[h] / l[h]).astype(BF16), mv[:, sl[h]]) for h in heads]
    return jnp.concatenate(outs, axis=-1)


def _rows_to_kv_heads(k, v):
    pieces = [a[:, h * LANES:(h + 1) * LANES] for a in (k, v) for h in range(a.shape[1] // LANES)]
    return jnp.swapaxes(jnp.stack(pieces), 0, 1)


def _memkv_kernel(m_ref, w_ref, o_ref, ob_ref):
    r = _dot(m_ref[...].astype(BF16), w_ref[0])
    o_ref[0] = _rows_to_kv_heads(r[:, 0:M_WIDTH], r[:, M_WIDTH:2 * M_WIDTH])
    ob_ref[0] = r.astype(BF16)


def _memkv(mem2d, w_bf):
    nm = mem2d.shape[0]
    nl = w_bf.shape[0]
    tm = min(512, nm)
    return pl.pallas_call(
        _memkv_kernel,
        grid=(nl, nm // tm),
        in_specs=[pl.BlockSpec((tm, D_MODEL), lambda l, i: (i, 0)),
                  pl.BlockSpec((1, D_MODEL, 2 * M_WIDTH), lambda l, i: (l, 0, 0))],
        out_specs=[pl.BlockSpec((1, tm, 2 * M_HEADS, M_HDIM), lambda l, i: (l, i, 0, 0)),
                   pl.BlockSpec((1, tm, 2 * M_WIDTH), lambda l, i: (l, i, 0))],
        out_shape=[jax.ShapeDtypeStruct((nl, nm, 2 * M_HEADS, M_HDIM), F32),
                   jax.ShapeDtypeStruct((nl, nm, 2 * M_WIDTH), BF16)],
        compiler_params=pltpu.CompilerParams(dimension_semantics=("arbitrary", "arbitrary")),
        name="memkv",
    )(mem2d, w_bf)


def _mlstm_chunk(rs, g, qkv_v, kt_v, c_s, n_s, m_s, causal, tri, hs):
    ls = _log_sigmoid(g)
    t0 = ls.astype(BF16)
    e1 = ls - t0.astype(F32)
    t1 = e1.astype(BF16)
    t2 = (e1 - t1.astype(F32)).astype(BF16)
    bc = _dot(tri, t0) + _dot(tri, t1) + _dot(tri, t2)
    lane = lax.broadcasted_iota(jnp.int32, (A_CHUNK, LANES), 1)
    xt = jnp.where(lane < A_HEADS, g, bc).T
    yield
    heads = range(A_HEADS)
    b_col = [bc[:, 4 + h:5 + h] for h in heads]
    b_row = [xt[4 + h:5 + h, :] for h in heads]
    li_row = [xt[h:h + 1, :] for h in heads]
    li_col = [g[:, h:h + 1] for h in heads]
    m_old = [m_s[h:h + 1, 0:1] for h in heads]
    b_last = [bc[A_CHUNK - 1:A_CHUNK, 4 + h:5 + h] for h in heads]
    qh = [qkv_v[rs, h * 3 * A_HDIM:h * 3 * A_HDIM + A_HDIM] for h in heads]
    kh = [qkv_v[rs, h * 3 * A_HDIM + A_HDIM:h * 3 * A_HDIM + 2 * A_HDIM] for h in heads]
    vh = [qkv_v[rs, h * 3 * A_HDIM + 2 * A_HDIM:(h + 1) * 3 * A_HDIM] for h in heads]
    kt = [kt_v[h] for h in heads]
    c_old = [c_s[h] for h in heads]
    n_old = [n_s[h:h + 1, :] for h in heads]
    qk = [_dot_nt(qh[h], kh[h]) for h in heads]
    qc = [_dot(qh[h], c_old[h].astype(BF16)) for h in heads]
    dm = [jnp.where(causal, b_col[h] - b_row[h] + li_row[h], -jnp.inf) for h in heads]
    inter = [b_col[h] + m_old[h] for h in heads]
    m_row = [jnp.maximum(inter[h], jnp.max(dm[h], axis=-1, keepdims=True)) for h in heads]
    g_max = [jnp.max(b_last[h] - b_row[h] + li_row[h], axis=-1, keepdims=True) for h in heads]
    m_new = [jnp.maximum(b_last[h] + m_old[h], g_max[h]) for h in heads]
    yield
    sc = [qk[h] * jnp.exp(dm[h] - m_row[h]) for h in heads]
    dec = [jnp.exp(inter[h] - m_row[h]) for h in heads]
    ws_col = [jnp.exp(b_last[h] - b_col[h] + li_col[h] - m_new[h]) for h in heads]
    dc = [jnp.exp(b_last[h] + m_old[h] - m_new[h]) for h in heads]
    yield
    sv = [_dot(sc[h].astype(BF16), vh[h]) for h in heads]
    wv = [(ws_col[h] * vh[h].astype(F32)).astype(BF16) for h in heads]
    upd = [_dot(kt[h], wv[h]) for h in heads]
    yield
    for h in heads:
        den = (jnp.sum(sc[h], axis=-1, keepdims=True)
               + dec[h] * jnp.sum(qh[h].astype(F32) * n_old[h], axis=-1, keepdims=True))
        num = sv[h] + dec[h] * qc[h]
        hs.append(num / jnp.maximum(jnp.abs(den), jnp.exp(-m_row[h])))
    yield
    for h in heads:
        c_s[h] = dc[h] * c_old[h] + upd[h]
        n_s[h:h + 1, :] = dc[h] * n_old[h] + jnp.sum(ws_col[h] * kh[h].astype(F32), axis=0, keepdims=True)
        m_s[h:h + 1, :] = jnp.broadcast_to(m_new[h], (1, LANES))
    yield


def _l0p_kernel(nt, x_ref, xp_ref, gpre_ref, win_ref, convw_ref, convb_ref, wq_ref, wk_ref, wkt_ref, wv_ref,
                wif_ref, bif_ref, ghn_ref, skip_ref, mkv_ref, wout_ref, gpost_ref,
                x1_ref, conv_out, c_out, n_out, m_out,
                h_s, u_s, ymix_s, xc_s, opre_s, zg_s, qm_s, zm_s, qkv_s, kt_s, gates_s,
                c_s, n_s, m_s):
    tt = x_ref.shape[1]
    nsub = tt // A_CHUNK
    t = pl.program_id(0)
    parity = lax.rem(t + 1, 2)
    pos1 = lax.rem(t + nt - 1, nt)
    pos2 = lax.rem(t + 2 * nt - 2, nt)

    @pl.when(t == 0)
    def _():
        h_s[...] = jnp.zeros(h_s.shape, BF16)
        u_s[...] = jnp.zeros(u_s.shape, F32)
        xc_s[0] = jnp.zeros(xc_s.shape[1:], F32)
        opre_s[0] = jnp.zeros(opre_s.shape[1:], F32)
        zg_s[0] = jnp.zeros(zg_s.shape[1:], F32)
        qm_s[0] = jnp.zeros(qm_s.shape[1:], BF16)
        zm_s[0] = jnp.zeros(zm_s.shape[1:], F32)
        qkv_s[0] = jnp.zeros(qkv_s.shape[1:], BF16)
        kt_s[0] = jnp.zeros(kt_s.shape[1:], BF16)
        gates_s[0] = jnp.zeros(gates_s.shape[1:], F32)

    @pl.when(pos1 == 0)
    def _():
        u_s[0:8, :] = jnp.zeros((8, A_INNER), F32)

    @pl.when(pos2 == 0)
    def _():
        c_s[...] = jnp.zeros(c_s.shape, F32)
        n_s[...] = jnp.zeros(n_s.shape, F32)
        m_s[...] = jnp.zeros(m_s.shape, F32)

    row = lax.broadcasted_iota(jnp.int32, (A_CHUNK, A_CHUNK), 0)
    col = lax.broadcasted_iota(jnp.int32, (A_CHUNK, A_CHUNK), 1)
    causal = col <= row
    tri = jnp.where(causal, 1.0, 0.0).astype(BF16)
    ghn = ghn_ref[...]
    skp = skip_ref[...]
    mk = mkv_ref[0, :, 0:M_WIDTH]
    mv = mkv_ref[0, :, M_WIDTH:2 * M_WIDTH]

    def stage2(pslot):
        for c in range(nsub):
            rs = slice(c * A_CHUNK, (c + 1) * A_CHUNK)
            hs = []
            yield from _mlstm_chunk(rs, gates_s[pslot, rs, :], qkv_s.at[pslot], kt_s.at[pslot, :, c],
                                    c_s, n_s, m_s, causal, tri, hs)
            parts = []
            for h in range(A_HEADS):
                v = _sigmoid(opre_s[pslot, rs, h * A_HDIM:(h + 1) * A_HDIM]) * hs[h]
                mu = jnp.mean(v, axis=-1, keepdims=True)
                var = jnp.mean(jnp.square(v - mu), axis=-1, keepdims=True)
                parts.append((v - mu) * lax.rsqrt(var + EPS))
            hn = jnp.concatenate(parts, axis=-1) * ghn
            y = hn + skp * xc_s[pslot, rs, :]
            ymix_s[rs, 0:A_INNER] = (y * _silu(zg_s[pslot, rs, :])).astype(BF16)
            yield
            ym = _mem_attention(qm_s[pslot, rs, :], mk, mv) * _silu(zm_s[pslot, rs, :])
            ymix_s[rs, A_INNER:A_INNER + M_WIDTH] = ym.astype(BF16)
            yield

    def stage1(slot):
        hb = h_s[...]
        u_s[8:8 + tt, :] = _dot(hb, win_ref[:, 0:A_INNER])
        yield
        cw = convw_ref[...]
        cb = convb_ref[...]
        for c in range(nsub):
            r0 = c * A_CHUNK
            blk = u_s[r0:r0 + A_CHUNK + 8, :]
            xc = cb + pltpu.roll(blk, 3, 0)[8:, :] * cw[0:1, :]
            xc = xc + pltpu.roll(blk, 2, 0)[8:, :] * cw[1:2, :]
            xc = xc + pltpu.roll(blk, 1, 0)[8:, :] * cw[2:3, :]
            xc = xc + blk[8:, :] * cw[3:4, :]
            xc_s[slot, r0:r0 + A_CHUNK, :] = _silu(xc)
        opre_s[slot] = _dot(hb, win_ref[:, A_INNER:2 * A_INNER])
        yield
        zg_s[slot] = _dot(hb, win_ref[:, 2 * A_INNER:3 * A_INNER])
        yield
        qm_s[slot] = _dot(hb, win_ref[:, 3 * A_INNER:3 * A_INNER + M_WIDTH]).astype(BF16)
        zm_s[slot] = _dot(hb, win_ref[:, 3 * A_INNER + M_WIDTH:3 * A_INNER + 2 * M_WIDTH])
        yield
        for h in range(A_HEADS):
            sl = slice(h * A_HDIM, (h + 1) * A_HDIM)
            xh = xc_s[slot, :, sl].astype(BF16)
            uh = u_s[8:8 + tt, sl].astype(BF16)
            base = h * 3 * A_HDIM
            qkv_s[slot, :, base:base + A_HDIM] = _dot(xh, wq_ref[h]).astype(BF16)
            qkv_s[slot, :, base + A_HDIM:base + 2 * A_HDIM] = (
                _dot(xh, wk_ref[h]) * (A_HDIM ** -0.5)).astype(BF16)
            qkv_s[slot, :, base + 2 * A_HDIM:base + 3 * A_HDIM] = _dot(uh, wv_ref[h]).astype(BF16)
            kt = (_dot_nt(wkt_ref[h], xh) * (A_HDIM ** -0.5)).astype(BF16)
            for c in range(nsub):
                kt_s[slot, h, c] = kt[:, c * A_CHUNK:(c + 1) * A_CHUNK]
            yield
        gates_s[slot] = _dot(qkv_s[slot], wif_ref[...]) + bif_ref[...]
        yield

    def step(slot):
        pending = [stage1(slot), stage2(1 - slot)]
        while pending:
            for gen in list(pending):
                try:
                    next(gen)
                except StopIteration:
                    pending.remove(gen)
        h_next = (_rms_scale(x_ref[0]) * gpre_ref[...]).astype(BF16)
        out = _dot(ymix_s[...], wout_ref[...])
        h_s[...] = h_next
        x1_ref[0] = xp_ref[0] + _rms_scale(out) * gpost_ref[...]

    for s in range(2):
        pl.when(parity == s)(functools.partial(step, s))

    @pl.when(jnp.logical_and(pos1 == nt - 1, t > 0))
    def _():
        conv_out[0, 0] = u_s[tt + 5:tt + 8, :]

    u_s[0:8, :] = u_s[tt:tt + 8, :]

    @pl.when(jnp.logical_and(pos2 == nt - 1, t > 1))
    def _():
        for h in range(A_HEADS):
            c_out[0, 0, h] = c_s[h].T
        n_out[0, 0] = n_s[0:A_HEADS, :]
        m_out[0] = m_s[...]


def _layer0_prompt_pipelined(x, g_pre, w_in, conv_w, conv_b, wq, wk, wkt, wv, wif, bif, ghn, skip, mkv_bf,
                             w_out, g_post):
    b, s, _ = x.shape
    tt = min(L0_TILE, s)
    nt = s // tt
    ntiles = b * nt
    a_in = w_in.shape[1]
    nsub = tt // A_CHUNK

    def cur(t):
        t1 = jnp.minimum(t, ntiles - 1)
        return (t1 // nt, t1 % nt, 0)

    def prev(t):
        t2 = jnp.maximum(t - 2, 0)
        return (t2 // nt, t2 % nt, 0)

    def prev_b(t):
        return jnp.maximum(t - 2, 0) // nt

    in_specs = [
        pl.BlockSpec((1, tt, D_MODEL), cur),
        pl.BlockSpec((1, tt, D_MODEL), prev),
        _const_spec((1, D_MODEL)),
        _const_spec((D_MODEL, a_in)),
        _const_spec((CONV_W, A_INNER)),
        _const_spec((1, A_INNER)),
        _const_spec((A_HEADS, A_HDIM, A_HDIM)),
        _const_spec((A_HEADS, A_HDIM, A_HDIM)),
        _const_spec((A_HEADS, A_HDIM, A_HDIM)),
        _const_spec((A_HEADS, A_HDIM, A_HDIM)),
        _const_spec((3 * A_INNER, LANES)),
        _const_spec((1, LANES)),
        _const_spec((1, A_INNER)),
        _const_spec((1, A_INNER)),
        pl.BlockSpec((1, N_MEM, 2 * M_WIDTH), lambda t: (prev_b(t), 0, 0)),
        _const_spec((A_INNER + M_WIDTH, D_MODEL)),
        _const_spec((1, D_MODEL)),
    ]
    out_specs = [
        pl.BlockSpec((1, tt, D_MODEL), prev),
        pl.BlockSpec((1, 1, CONV_W - 1, A_INNER), lambda t: (0, prev_b(t), 0, 0)),
        pl.BlockSpec((1, 1, A_HEADS, A_HDIM, A_HDIM), lambda t: (0, prev_b(t), 0, 0, 0)),
        pl.BlockSpec((1, 1, A_HEADS, A_HDIM), lambda t: (0, prev_b(t), 0, 0)),
        pl.BlockSpec((1, 8, LANES), lambda t: (prev_b(t), 0, 0)),
    ]
    out_shape = [
        jax.ShapeDtypeStruct((b, s, D_MODEL), F32),
        jax.ShapeDtypeStruct((1, b, CONV_W - 1, A_INNER), F32),
        jax.ShapeDtypeStruct((1, b, A_HEADS, A_HDIM, A_HDIM), F32),
        jax.ShapeDtypeStruct((1, b, A_HEADS, A_HDIM), F32),
        jax.ShapeDtypeStruct((b, 8, LANES), F32),
    ]
    scratch = [
        pltpu.VMEM((tt, D_MODEL), BF16),
        pltpu.VMEM((tt + 8, A_INNER), F32),
        pltpu.VMEM((tt, A_INNER + M_WIDTH), BF16),
        pltpu.VMEM((2, tt, A_INNER), F32),
        pltpu.VMEM((2, tt, A_INNER), F32),
        pltpu.VMEM((2, tt, A_INNER), F32),
        pltpu.VMEM((2, tt, M_WIDTH), BF16),
        pltpu.VMEM((2, tt, M_WIDTH), F32),
        pltpu.VMEM((2, tt, 3 * A_INNER), BF16),
        pltpu.VMEM((2, A_HEADS, nsub, A_HDIM, A_CHUNK), BF16),
        pltpu.VMEM((2, tt, LANES), F32),
        pltpu.VMEM((A_HEADS, A_HDIM, A_HDIM), F32),
        pltpu.VMEM((8, A_HDIM), F32),
        pltpu.VMEM((8, LANES), F32),
    ]
    return pl.pallas_call(
        functools.partial(_l0p_kernel, nt),
        grid=(ntiles + 2,),
        in_specs=in_specs,
        out_specs=out_specs,
        out_shape=out_shape,
        scratch_shapes=scratch,
        compiler_params=pltpu.CompilerParams(
            dimension_semantics=("arbitrary",), vmem_limit_bytes=VMEM_LIMIT),
        name="layer0_prompt",
    )(x, x, g_pre, w_in, conv_w, conv_b, wq, wk, wkt, wv, wif, bif, ghn, skip, mkv_bf, w_out, g_post)


def _l0_kernel(x_ref, gpre_ref, win_ref, convw_ref, convb_ref, wq_ref, wk_ref, wkt_ref, wv_ref,
               wif_ref, bif_ref, ghn_ref, skip_ref, mkv_ref, wout_ref, gpost_ref,
               x1_ref, conv_out, c_out, n_out, m_out,
               h_s, u_s, xc_s, opre_s, zg_s, qm_s, zm_s, qkv_s, kt_s, gates_s, ymix_s,
               c_s, n_s, m_s):
    tt = x_ref.shape[1]
    nsub = tt // A_CHUNK
    i = pl.program_id(1)
    nt = pl.num_programs(1)

    @pl.when(i == 0)
    def _():
        u_s[0:8, :] = jnp.zeros((8, A_INNER), F32)
        c_s[...] = jnp.zeros(c_s.shape, F32)
        n_s[...] = jnp.zeros(n_s.shape, F32)
        m_s[...] = jnp.zeros(m_s.shape, F32)

    gpre = gpre_ref[...]

    def norm_body(c, _):
        r = pl.ds(pl.multiple_of(c * A_CHUNK, A_CHUNK), A_CHUNK)
        h_s[r, :] = (_rms_scale(x_ref[0, r, :]) * gpre).astype(BF16)
        return 0
    lax.fori_loop(0, nsub, norm_body, 0)

    hb = h_s[...]
    u_s[8:8 + tt, :] = _dot(hb, win_ref[:, 0:A_INNER])
    opre_s[...] = _dot(hb, win_ref[:, A_INNER:2 * A_INNER])
    zg_s[...] = _dot(hb, win_ref[:, 2 * A_INNER:3 * A_INNER])
    qm_s[...] = _dot(hb, win_ref[:, 3 * A_INNER:3 * A_INNER + M_WIDTH]).astype(BF16)
    zm_s[...] = _dot(hb, win_ref[:, 3 * A_INNER + M_WIDTH:3 * A_INNER + 2 * M_WIDTH])

    cw = convw_ref[...]
    cb = convb_ref[...]

    for c in range(nsub):
        r0 = c * A_CHUNK
        xc = cb + u_s[r0 + 5:r0 + 5 + A_CHUNK, :] * cw[0:1, :]
        xc = xc + u_s[r0 + 6:r0 + 6 + A_CHUNK, :] * cw[1:2, :]
        xc = xc + u_s[r0 + 7:r0 + 7 + A_CHUNK, :] * cw[2:3, :]
        xc = xc + u_s[r0 + 8:r0 + 8 + A_CHUNK, :] * cw[3:4, :]
        xc_s[r0:r0 + A_CHUNK, :] = _silu(xc)

    for h in range(A_HEADS):
        sl = slice(h * A_HDIM, (h + 1) * A_HDIM)
        xh = xc_s[:, sl].astype(BF16)
        uh = u_s[8:8 + tt, sl].astype(BF16)
        base = h * 3 * A_HDIM
        qkv_s[:, base:base + A_HDIM] = _dot(xh, wq_ref[h]).astype(BF16)
        qkv_s[:, base + A_HDIM:base + 2 * A_HDIM] = (_dot(xh, wk_ref[h]) * (A_HDIM ** -0.5)).astype(BF16)
        qkv_s[:, base + 2 * A_HDIM:base + 3 * A_HDIM] = _dot(uh, wv_ref[h]).astype(BF16)
        kt = (_dot_nt(wkt_ref[h], xh) * (A_HDIM ** -0.5)).astype(BF16)
        for c in range(nsub):
            kt_s[h, c] = kt[:, c * A_CHUNK:(c + 1) * A_CHUNK]
    gates_s[...] = _dot(qkv_s[...], wif_ref[...]) + bif_ref[...]

    row = lax.broadcasted_iota(jnp.int32, (A_CHUNK, A_CHUNK), 0)
    col = lax.broadcasted_iota(jnp.int32, (A_CHUNK, A_CHUNK), 1)
    causal = col <= row
    tri = jnp.where(causal, 1.0, 0.0).astype(BF16)

    def chunk_body(c, _):
        r0 = pl.multiple_of(c * A_CHUNK, A_CHUNK)
        rs = pl.ds(r0, A_CHUNK)
        g = gates_s[rs, :]
        ls = _log_sigmoid(g)
        t0 = ls.astype(BF16)
        e1 = ls - t0.astype(F32)
        t1 = e1.astype(BF16)
        t2 = (e1 - t1.astype(F32)).astype(BF16)
        bc = _dot(tri, t0) + _dot(tri, t1) + _dot(tri, t2)
        lane = lax.broadcasted_iota(jnp.int32, (A_CHUNK, LANES), 1)
        xt = jnp.where(lane < A_HEADS, g, bc).T
        heads = range(A_HEADS)
        b_col = [bc[:, 4 + h:5 + h] for h in heads]
        b_row = [xt[4 + h:5 + h, :] for h in heads]
        li_row = [xt[h:h + 1, :] for h in heads]
        li_col = [g[:, h:h + 1] for h in heads]
        m_old = [m_s[h:h + 1, 0:1] for h in heads]
        b_last = [bc[A_CHUNK - 1:A_CHUNK, 4 + h:5 + h] for h in heads]
        qh = [qkv_s[rs, h * 3 * A_HDIM:h * 3 * A_HDIM + A_HDIM] for h in heads]
        kh = [qkv_s[rs, h * 3 * A_HDIM + A_HDIM:h * 3 * A_HDIM + 2 * A_HDIM] for h in heads]
        vh = [qkv_s[rs, h * 3 * A_HDIM + 2 * A_HDIM:(h + 1) * 3 * A_HDIM] for h in heads]
        kt = [kt_s[h, c] for h in heads]
        c_old = [c_s[h] for h in heads]
        n_old = [n_s[h:h + 1, :] for h in heads]
        qk = [_dot_nt(qh[h], kh[h]) for h in heads]
        qc = [_dot(qh[h], c_old[h].astype(BF16)) for h in heads]
        dm = [jnp.where(causal, b_col[h] - b_row[h] + li_row[h], -jnp.inf) for h in heads]
        inter = [b_col[h] + m_old[h] for h in heads]
        m_row = [jnp.maximum(inter[h], jnp.max(dm[h], axis=-1, keepdims=True)) for h in heads]
        g_max = [jnp.max(b_last[h] - b_row[h] + li_row[h], axis=-1, keepdims=True) for h in heads]
        m_new = [jnp.maximum(b_last[h] + m_old[h], g_max[h]) for h in heads]
        sc = [qk[h] * jnp.exp(dm[h] - m_row[h]) for h in heads]
        dec = [jnp.exp(inter[h] - m_row[h]) for h in heads]
        ws_col = [jnp.exp(b_last[h] - b_col[h] + li_col[h] - m_new[h]) for h in heads]
        dc = [jnp.exp(b_last[h] + m_old[h] - m_new[h]) for h in heads]
        sv = [_dot(sc[h].astype(BF16), vh[h]) for h in heads]
        wv = [(ws_col[h] * vh[h].astype(F32)).astype(BF16) for h in heads]
        upd = [_dot(kt[h], wv[h]) for h in heads]
        hs = []
        for h in heads:
            den = (jnp.sum(sc[h], axis=-1, keepdims=True)
                   + dec[h] * jnp.sum(qh[h].astype(F32) * n_old[h], axis=-1, keepdims=True))
            num = sv[h] + dec[h] * qc[h]
            hs.append(num / jnp.maximum(jnp.abs(den), jnp.exp(-m_row[h])))
        for h in heads:
            c_s[h] = dc[h] * c_old[h] + upd[h]
            n_s[h:h + 1, :] = dc[h] * n_old[h] + jnp.sum(ws_col[h] * kh[h].astype(F32), axis=0, keepdims=True)
            m_s[h:h + 1, :] = jnp.broadcast_to(m_new[h], (1, LANES))

        parts = []
        for h in heads:
            v = _sigmoid(opre_s[rs, h * A_HDIM:(h + 1) * A_HDIM]) * hs[h]
            mu = jnp.mean(v, axis=-1, keepdims=True)
            var = jnp.mean(jnp.square(v - mu), axis=-1, keepdims=True)
            parts.append((v - mu) * lax.rsqrt(var + EPS))
        hn = jnp.concatenate(parts, axis=-1) * ghn
        y = hn + skp * xc_s[rs, :]
        ymix_s[rs, 0:A_INNER] = (y * _silu(zg_s[rs, :])).astype(BF16)
        ym = _mem_attention(qm_s[rs, :], mk, mv) * _silu(zm_s[rs, :])
        ymix_s[rs, A_INNER:A_INNER + M_WIDTH] = ym.astype(BF16)
        return 0

    ghn = ghn_ref[...]
    skp = skip_ref[...]
    mk = mkv_ref[0, :, 0:M_WIDTH]
    mv = mkv_ref[0, :, M_WIDTH:2 * M_WIDTH]
    lax.fori_loop(0, nsub, chunk_body, 0)

    out = _dot(ymix_s[...], wout_ref[...])
    x1_ref[0] = x_ref[0] + _rms_scale(out) * gpost_ref[...]

    u_s[0:8, :] = u_s[tt:tt + 8, :]

    @pl.when(i == nt - 1)
    def _():
        conv_out[0, 0] = u_s[tt + 5:tt + 8, :]
        for h in range(A_HEADS):
            c_out[0, 0, h] = c_s[h].T
        n_out[0, 0] = n_s[0:A_HEADS, :]
        m_out[0] = m_s[...]


def _layer0_prompt(x, g_pre, w_in, conv_w, conv_b, wq, wk, wkt, wv, wif, bif, ghn, skip, mkv_bf, w_out,
                   g_post):
    b, s, _ = x.shape
    tt = min(TOK_TILE, s)
    nt = s // tt
    a_in = w_in.shape[1]
    tile = lambda bb, i: (bb, i, 0)
    per_b = lambda bb, i: (bb, 0, 0)
    in_specs = [
        pl.BlockSpec((1, tt, D_MODEL), tile),
        _const_spec((1, D_MODEL)),
        _const_spec((D_MODEL, a_in)),
        _const_spec((CONV_W, A_INNER)),
        _const_spec((1, A_INNER)),
        _const_spec((A_HEADS, A_HDIM, A_HDIM)),
        _const_spec((A_HEADS, A_HDIM, A_HDIM)),
        _const_spec((A_HEADS, A_HDIM, A_HDIM)),
        _const_spec((A_HEADS, A_HDIM, A_HDIM)),
        _const_spec((3 * A_INNER, LANES)),
        _const_spec((1, LANES)),
        _const_spec((1, A_INNER)),
        _const_spec((1, A_INNER)),
        pl.BlockSpec((1, N_MEM, 2 * M_WIDTH), per_b),
        _const_spec((A_INNER + M_WIDTH, D_MODEL)),
        _const_spec((1, D_MODEL)),
    ]
    out_specs = [
        pl.BlockSpec((1, tt, D_MODEL), tile),
        pl.BlockSpec((1, 1, CONV_W - 1, A_INNER), lambda bb, i: (0, bb, 0, 0)),
        pl.BlockSpec((1, 1, A_HEADS, A_HDIM, A_HDIM), lambda bb, i: (0, bb, 0, 0, 0)),
        pl.BlockSpec((1, 1, A_HEADS, A_HDIM), lambda bb, i: (0, bb, 0, 0)),
        pl.BlockSpec((1, 8, LANES), per_b),
    ]
    out_shape = [
        jax.ShapeDtypeStruct((b, s, D_MODEL), F32),
        jax.ShapeDtypeStruct((1, b, CONV_W - 1, A_INNER), F32),
        jax.ShapeDtypeStruct((1, b, A_HEADS, A_HDIM, A_HDIM), F32),
        jax.ShapeDtypeStruct((1, b, A_HEADS, A_HDIM), F32),
        jax.ShapeDtypeStruct((b, 8, LANES), F32),
    ]
    scratch = [
        pltpu.VMEM((tt, D_MODEL), BF16),
        pltpu.VMEM((tt + 8, A_INNER), F32),
        pltpu.VMEM((tt, A_INNER), F32),
        pltpu.VMEM((tt, A_INNER), F32),
        pltpu.VMEM((tt, A_INNER), F32),
        pltpu.VMEM((tt, M_WIDTH), BF16),
        pltpu.VMEM((tt, M_WIDTH), F32),
        pltpu.VMEM((tt, 3 * A_INNER), BF16),
        pltpu.VMEM((A_HEADS, tt // A_CHUNK, A_HDIM, A_CHUNK), BF16),
        pltpu.VMEM((tt, LANES), F32),
        pltpu.VMEM((tt, A_INNER + M_WIDTH), BF16),
        pltpu.VMEM((A_HEADS, A_HDIM, A_HDIM), F32),
        pltpu.VMEM((8, A_HDIM), F32),
        pltpu.VMEM((8, LANES), F32),
    ]
    return pl.pallas_call(
        _l0_kernel,
        grid=(b, nt),
        in_specs=in_specs,
        out_specs=out_specs,
        out_shape=out_shape,
        scratch_shapes=scratch,
        compiler_params=pltpu.CompilerParams(
            dimension_semantics=("arbitrary", "arbitrary"), vmem_limit_bytes=VMEM_LIMIT),
        name="layer0_prompt",
    )(x, g_pre, w_in, conv_w, conv_b, wq, wk, wkt, wv, wif, bif, ghn, skip, mkv_bf, w_out, g_post)


def _rope_cols(x, cos, sin_signed):
    outs = []
    for cblk in range(x.shape[1] // B_HDIM):
        xb = x[:, cblk * B_HDIM:(cblk + 1) * B_HDIM]
        outs.append(xb * cos + pltpu.roll(xb, B_HDIM // 2, 1) * sin_signed)
    return jnp.concatenate(outs, axis=-1)


def _l1a_kernel(x_ref, gkv_ref, gpre_ref, wkv_ref, win_ref, cos_ref, sin_ref,
                q0_ref, q1_ref, q2_ref, k0_ref, k1_ref, k2_ref, v0_ref, v1_ref, v2_ref,
                zg_ref, qm_ref, zm_ref, w0_ref, w1_ref, w2_ref):
    tt = x_ref.shape[1]
    xn = _rms_scale(x_ref[0])
    hk = (xn * gkv_ref[...]).astype(BF16)
    hq = (xn * gpre_ref[...]).astype(BF16)
    cos = cos_ref[...]
    sin = sin_ref[...]
    q_refs = (q0_ref, q1_ref, q2_ref)
    k_refs = (k0_ref, k1_ref, k2_ref)
    v_refs = (v0_ref, v1_ref, v2_ref)
    w_refs = (w0_ref, w1_ref, w2_ref)
    for g in (2, 1, 0):
        d = B_GROUPS[g][1]
        kf = _rope_cols(_dot(hk, wkv_ref[:, g * 2 * B_WIDTH:g * 2 * B_WIDTH + B_WIDTH]), cos, sin)
        vf = _dot(hk, wkv_ref[:, g * 2 * B_WIDTH + B_WIDTH:(g + 1) * 2 * B_WIDTH])
        qf = _rope_cols(_dot(hq, win_ref[:, g * B_WIDTH:(g + 1) * B_WIDTH]), cos, sin)
        wr = w_refs[g]
        wrows = wr.shape[1]
        wr[0] = _rows_to_kv_heads(kf[tt - wrows:, :], vf[tt - wrows:, :])
        for val, ref in ((qf.astype(BF16), q_refs[g]), (kf.astype(BF16), k_refs[g]), (vf.astype(BF16), v_refs[g])):
            if d == 1:
                ref[0, 0] = val
            else:
                ref[0] = jnp.swapaxes(val.reshape(tt // d, d, val.shape[1]), 0, 1)
    qoff = N_GROUPS * B_WIDTH
    zg_ref[0] = _dot(hq, win_ref[:, qoff:qoff + B_WIDTH]).astype(BF16)
    qm_ref[0] = _dot(hq, win_ref[:, qoff + B_WIDTH:qoff + B_WIDTH + M_WIDTH]).astype(BF16)
    zm_ref[0] = _dot(hq, win_ref[:, qoff + B_WIDTH + M_WIDTH:qoff + B_WIDTH + 2 * M_WIDTH]).astype(BF16)


def _layer1_proj_prompt(x1, g_kv, g_pre, wkv, win, cos_t, sin_t):
    b, s, _ = x1.shape
    tt = min(TOK_TILE, s)
    nt = s // tt
    tile = lambda bb, i: (bb, i, 0)
    in_specs = [
        pl.BlockSpec((1, tt, D_MODEL), tile),
        _const_spec((1, D_MODEL)),
        _const_spec((1, D_MODEL)),
        _const_spec(wkv.shape),
        _const_spec(win.shape),
        pl.BlockSpec((tt, B_HDIM), lambda bb, i: (i, 0)),
        pl.BlockSpec((tt, B_HDIM), lambda bb, i: (i, 0)),
    ]
    qkv_specs, qkv_shapes = [], []
    for _ in range(3):
        for (_, d) in B_GROUPS:
            qkv_specs.append(pl.BlockSpec((1, d, tt // d, B_WIDTH), lambda bb, i: (bb, 0, i, 0)))
            qkv_shapes.append(jax.ShapeDtypeStruct((b, d, s // d, B_WIDTH), BF16))
    gate_specs = [pl.BlockSpec((1, tt, B_WIDTH), tile)] * 3
    gate_shapes = [jax.ShapeDtypeStruct((b, s, B_WIDTH), BF16)] * 3
    win_specs, win_shapes = [], []
    for (w, _) in B_GROUPS:
        wr = min(w, s)
        rows = min(wr, tt)
        nblk = wr // rows
        win_specs.append(pl.BlockSpec(
            (1, rows, 2 * B_HEADS, B_HDIM),
            functools.partial(lambda bb, i, nb: (bb, jnp.maximum(i - (nt - nb), 0), 0, 0), nb=nblk)))
        win_shapes.append(jax.ShapeDtypeStruct((b, wr, 2 * B_HEADS, B_HDIM), F32))
    return pl.pallas_call(
        _l1a_kernel,
        grid=(b, nt),
        in_specs=in_specs,
        out_specs=qkv_specs + gate_specs + win_specs,
        out_shape=qkv_shapes + gate_shapes + win_shapes,
        compiler_params=pltpu.CompilerParams(
            dimension_semantics=("arbitrary", "arbitrary"), vmem_limit_bytes=VMEM_LIMIT),
        name="layer1_proj_prompt",
    )(x1, g_kv, g_pre, wkv, win, cos_t, sin_t)


def _cols_to_lanes(cols):
    t = cols[0].shape[0]
    lane = lax.broadcasted_iota(jnp.int32, (t, LANES), 1)
    acc = jnp.zeros((t, LANES), F32)
    for h, cvec in enumerate(cols):
        acc = jnp.where(lane == h, cvec, acc)
    return acc


def _band_attn_kernel(q_ref, kc_ref, kp_ref, vc_ref, vp_ref, o_ref, lse_ref):
    nres, tq = q_ref.shape[1:3]
    nsb = tq // ATT_BLK
    j = pl.program_id(2)
    row = lax.broadcasted_iota(jnp.int32, (ATT_BLK, 2 * ATT_BLK), 0)
    col = lax.broadcasted_iota(jnp.int32, (ATT_BLK, 2 * ATT_BLK), 1)
    band = jnp.logical_and(col >= row, col <= row + ATT_BLK)
    first_pen = jnp.where(col < ATT_BLK, jnp.where(j > 0, 0.0, -jnp.inf), 0.0)
    scale = B_HDIM ** -0.5
    qs, ks, vs, pens = [], [], [], []
    for r in range(nres):
        for sb in range(nsb):
            rs = slice(sb * ATT_BLK, (sb + 1) * ATT_BLK)
            ps = slice((sb - 1) * ATT_BLK, sb * ATT_BLK)
            for h in range(B_HEADS):
                hs = slice(h * B_HDIM, (h + 1) * B_HDIM)
                qs.append(q_ref[0, r, rs, hs])
                kp = kp_ref[0, r, :, hs] if sb == 0 else kc_ref[0, r, ps, hs]
                vp = vp_ref[0, r, :, hs] if sb == 0 else vc_ref[0, r, ps, hs]
                ks.append(jnp.concatenate([kp, kc_ref[0, r, rs, hs]], axis=0))
                vs.append(jnp.concatenate([vp, vc_ref[0, r, rs, hs]], axis=0))
                pens.append(sb == 0)
    q3 = jnp.stack(qs)
    k3 = jnp.stack(ks)
    v3 = jnp.stack(vs)
    s = jnp.einsum('uqd,ukd->uqk', q3, k3, preferred_element_type=F32)
    s = jnp.stack([s[u] + first_pen if pens[u] else s[u] for u in range(len(pens))])
    s = jnp.where(band[None], s, -jnp.inf)
    mx = jnp.max(s, axis=-1, keepdims=True)
    p = jnp.exp2((s - mx) * (scale * LOG2E))
    l = jnp.sum(p, axis=-1, keepdims=True)
    o = jnp.einsum('uqk,ukd->uqd', p.astype(BF16), v3, preferred_element_type=F32) / l
    lse = mx * scale + jnp.log(l)
    for r in range(nres):
        for sb in range(nsb):
            rs = slice(sb * ATT_BLK, (sb + 1) * ATT_BLK)
            u0 = (r * nsb + sb) * B_HEADS
            for h in range(B_HEADS):
                o_ref[0, r, rs, h * B_HDIM:(h + 1) * B_HDIM] = o[u0 + h].astype(BF16)
            lse_ref[0, r, rs, :] = _cols_to_lanes([lse[u0 + h] for h in range(B_HEADS)])


def _band_attention(q, k, v):
    b, d, ls, _ = q.shape
    tq = min(TOK_TILE, ls)
    nj = ls // tq
    ratio = tq // ATT_BLK
    nres = min(d, TOK_TILE // tq)
    cur = lambda bb, r, j: (bb, r, j, 0)
    prev = lambda bb, r, j: (bb, r, jnp.maximum(j * ratio - 1, 0), 0)
    return pl.pallas_call(
        _band_attn_kernel,
        grid=(b, d // nres, nj),
        in_specs=[pl.BlockSpec((1, nres, tq, B_WIDTH), cur),
                  pl.BlockSpec((1, nres, tq, B_WIDTH), cur),
                  pl.BlockSpec((1, nres, ATT_BLK, B_WIDTH), prev),
                  pl.BlockSpec((1, nres, tq, B_WIDTH), cur),
                  pl.BlockSpec((1, nres, ATT_BLK, B_WIDTH), prev)],
        out_specs=[pl.BlockSpec((1, nres, tq, B_WIDTH), cur),
                   pl.BlockSpec((1, nres, tq, LANES), cur)],
        out_shape=[jax.ShapeDtypeStruct((b, d, ls, B_WIDTH), BF16),
                   jax.ShapeDtypeStruct((b, d, ls, LANES), F32)],
        compiler_params=pltpu.CompilerParams(
            dimension_semantics=("arbitrary", "arbitrary", "arbitrary"), vmem_limit_bytes=VMEM_LIMIT),
        name="band_attention_d%d" % d,
    )(q, k, k, v, v)


def _unpermute(ref):
    d, rows, width = ref.shape[1:]
    if d == 1:
        return ref[0, 0].astype(F32)
    return jnp.swapaxes(ref[0], 0, 1).reshape(d * rows, width).astype(F32)


def _l1c_kernel(x_ref, o0_ref, o1_ref, o2_ref, l0_ref, l1_ref, l2_ref, zg_ref, qm_ref, zm_ref,
                mkv_ref, wout_ref, gpost_ref, y_ref):
    tt = x_ref.shape[1]
    o_refs = (o0_ref, o1_ref, o2_ref)
    l_refs = (l0_ref, l1_ref, l2_ref)
    outs, lses = [], []
    for g, (_, d) in enumerate(B_GROUPS):
        outs.append(_unpermute(o_refs[g]))
        lses.append(_unpermute(l_refs[g])[:, 0:B_HEADS])
    mx = jnp.maximum(jnp.maximum(lses[0], lses[1]), lses[2])
    es = [jnp.exp(l - mx) for l in lses]
    tot = es[0] + es[1] + es[2]
    ws = [e / tot for e in es]
    parts = []
    for h in range(B_HEADS):
        hs = slice(h * B_HDIM, (h + 1) * B_HDIM)
        acc = ws[0][:, h:h + 1] * outs[0][:, hs]
        acc = acc + ws[1][:, h:h + 1] * outs[1][:, hs]
        acc = acc + ws[2][:, h:h + 1] * outs[2][:, hs]
        parts.append(acc)
    ydil = jnp.concatenate(parts, axis=-1)
    ymix = (ydil * _silu(zg_ref[0].astype(F32))).astype(BF16)
    mk = mkv_ref[0, :, 0:M_WIDTH]
    mv = mkv_ref[0, :, M_WIDTH:2 * M_WIDTH]
    ym = (_mem_attention(qm_ref[0], mk, mv) * _silu(zm_ref[0].astype(F32))).astype(BF16)
    out = _dot(ymix, wout_ref[0:B_WIDTH, :]) + _dot(ym, wout_ref[B_WIDTH:B_WIDTH + M_WIDTH, :])
    y_ref[0] = x_ref[0] + _rms_scale(out) * gpost_ref[...]


def _layer1_out_prompt(x1, os_, ls_, zg, qm, zm, mkv_bf, w_out, g_post):
    b, s, _ = x1.shape
    tt = min(TOK_TILE, s)
    nt = s // tt
    tile = lambda bb, i: (bb, i, 0)
    perm = lambda bb, i: (bb, 0, i, 0)
    in_specs = [pl.BlockSpec((1, tt, D_MODEL), tile)]
    for width in (B_WIDTH, LANES):
        for (_, d) in B_GROUPS:
            in_specs.append(pl.BlockSpec((1, d, tt // d, width), perm))
    in_specs += [pl.BlockSpec((1, tt, B_WIDTH), tile)] * 3
    in_specs += [pl.BlockSpec((1, N_MEM, 2 * M_WIDTH), lambda bb, i: (bb, 0, 0)),
                 _const_spec(w_out.shape), _const_spec((1, D_MODEL))]
    return pl.pallas_call(
        _l1c_kernel,
        grid=(b, nt),
        in_specs=in_specs,
        out_specs=pl.BlockSpec((1, tt, D_MODEL), tile),
        out_shape=jax.ShapeDtypeStruct((b, s, D_MODEL), F32),
        compiler_params=pltpu.CompilerParams(
            dimension_semantics=("arbitrary", "arbitrary"), vmem_limit_bytes=VMEM_LIMIT),
        name="layer1_out_prompt",
    )(x1, *os_, *ls_, zg, qm, zm, mkv_bf, w_out, g_post)


def _rope_tables(pos):
    half = B_HDIM // 2
    inv = ROPE_THETA ** (-jnp.arange(half, dtype=F32) / half)
    ang = pos[:, None] * inv[None, :]
    cos = jnp.cos(ang)
    sin = jnp.sin(ang)
    return jnp.concatenate([cos, cos], axis=-1), jnp.concatenate([-sin, sin], axis=-1)


def _prompt_group(x_prompt, mem_prompt, p):
    b, s, _ = x_prompt.shape
    memkv_f, memkv_b = _memkv(mem_prompt.reshape(b * N_MEM, D_MODEL), p['w_mkv'])
    depth = memkv_f.shape[0]
    memkv_b = memkv_b.reshape(depth, b, N_MEM, 2 * M_WIDTH)
    x1, conv_p, c_p, n_p, m_pad = _layer0_prompt_pipelined(
        x_prompt, p['g_pre'][0:1], p['w_in_a'][0], p['conv_w_a'][0], p['conv_b_a'], p['w_q_a'][0],
        p['w_k_a'][0], jnp.swapaxes(p['w_k_a'][0], 1, 2), p['w_v_a'][0], p['w_if_a'], p['b_if_a'],
        p['g_hn_a'], p['skip_a'], memkv_b[0], p['w_out_a'][0], p['g_post'][0:1])
    cos_t, sin_t = _rope_tables(jnp.arange(s, dtype=F32))
    outs = _layer1_proj_prompt(x1, p['g_kv'], p['g_pre'][1:2], p['w_kv_b'], p['w_in_b'][0], cos_t, sin_t)
    qs, ks, vs = outs[0:3], outs[3:6], outs[6:9]
    zg, qm, zm = outs[9:12]
    wins = outs[12:15]
    os_, ls_ = [], []
    for g in range(N_GROUPS):
        o, l = _band_attention(qs[g], ks[g], vs[g])
        os_.append(o)
        ls_.append(l)
    y = _layer1_out_prompt(x1, os_, ls_, zg, qm, zm, memkv_b[1], p['w_out_b'][0], p['g_post'][1:2])
    m_p = m_pad[:, 0:A_HEADS, 0][None]
    wins = [w.reshape(b, w.shape[1], 2, B_HEADS, B_HDIM) for w in wins]
    memkv_p = memkv_f.reshape(depth, b, N_MEM, 2, M_HEADS, M_HDIM)
    return y, conv_p, c_p, n_p, m_p, wins, memkv_p


def _prep_params(g_pre, g_post, w_in_a, conv_w_a, conv_b_a, w_q_a, w_k_a, w_v_a, w_if_a, b_if_a,
                 g_hn_a, skip_a, w_out_a, g_kv, w_kv_b, w_in_b, w_out_b, w_mkv):
    wif = jnp.pad(w_if_a[0], ((0, 0), (0, LANES - 2 * A_HEADS))).astype(BF16)
    bif = jnp.pad(b_if_a[0], (0, LANES - 2 * A_HEADS))[None, :]
    return {
        'g_pre': g_pre, 'g_post': g_post,
        'w_in_a': w_in_a.astype(BF16), 'conv_w_a': conv_w_a, 'conv_b_a': conv_b_a,
        'w_q_a': w_q_a.astype(BF16), 'w_k_a': w_k_a.astype(BF16), 'w_v_a': w_v_a.astype(BF16),
        'w_if_a': wif, 'b_if_a': bif, 'g_hn_a': g_hn_a, 'skip_a': skip_a,
        'w_out_a': w_out_a.astype(BF16), 'g_kv': g_kv[None, :], 'w_kv_b': w_kv_b.astype(BF16),
        'w_in_b': w_in_b.astype(BF16), 'w_out_b': w_out_b.astype(BF16), 'w_mkv': w_mkv.astype(BF16),
    }


def _dec_l0_proj_kernel(x_ref, gpre_ref, win_ref, cst_ref, convw_ref, convb_ref, wq_ref, wk_ref, wv_ref,
                        wif_ref, bif_ref,
                        q_ref, k_ref, v_ref, gates_ref, xc_ref, opre_ref, zg_ref, qm_ref, zm_ref, cnew_ref):
    h = (_rms_scale(x_ref[...]) * gpre_ref[...]).astype(BF16)
    u = _dot(h, win_ref[:, 0:A_INNER])
    opre_ref[...] = _dot(h, win_ref[:, A_INNER:2 * A_INNER])
    zg_ref[...] = _dot(h, win_ref[:, 2 * A_INNER:3 * A_INNER])
    qm_ref[...] = _dot(h, win_ref[:, 3 * A_INNER:3 * A_INNER + M_WIDTH])
    zm_ref[...] = _dot(h, win_ref[:, 3 * A_INNER + M_WIDTH:3 * A_INNER + 2 * M_WIDTH])
    cw = convw_ref[...]
    xc = convb_ref[...] + cst_ref[0] * cw[0:1, :]
    xc = xc + cst_ref[1] * cw[1:2, :]
    xc = xc + cst_ref[2] * cw[2:3, :]
    xc = xc + u * cw[3:4, :]
    xc = _silu(xc)
    xc_ref[...] = xc
    cnew_ref[0] = cst_ref[1]
    cnew_ref[1] = cst_ref[2]
    cnew_ref[2] = u
    qs, ks, vs, cat = [], [], [], []
    for hd in range(A_HEADS):
        sl = slice(hd * A_HDIM, (hd + 1) * A_HDIM)
        xh = xc[:, sl].astype(BF16)
        qh = _dot(xh, wq_ref[hd])
        kh = _dot(xh, wk_ref[hd]) * (A_HDIM ** -0.5)
        vh = _dot(u[:, sl].astype(BF16), wv_ref[hd])
        qs.append(qh)
        ks.append(kh)
        vs.append(vh)
        cat += [qh.astype(BF16), kh.astype(BF16), vh.astype(BF16)]
    q_ref[...] = jnp.concatenate(qs, axis=-1)
    k_ref[...] = jnp.concatenate(ks, axis=-1)
    v_ref[...] = jnp.concatenate(vs, axis=-1)
    gates_ref[...] = _dot(jnp.concatenate(cat, axis=-1), wif_ref[...]) + bif_ref[...]


def _whole(shape):
    nd = len(shape)
    return pl.BlockSpec(shape, lambda *_: (0,) * nd)


def _dec_l0_proj(x, g_pre, w_in, cst, conv_w, conv_b, wq, wk, wv, wif, bif):
    nb = x.shape[0]
    args = (x, g_pre, w_in, cst, conv_w, conv_b, wq, wk, wv, wif, bif)
    f = lambda *s: jax.ShapeDtypeStruct(s, F32)
    out_shape = [f(nb, A_INNER), f(nb, A_INNER), f(nb, A_INNER), f(nb, LANES), f(nb, A_INNER), f(nb, A_INNER),
                 f(nb, A_INNER), f(nb, M_WIDTH), f(nb, M_WIDTH), f(CONV_W - 1, nb, A_INNER)]
    return pl.pallas_call(
        _dec_l0_proj_kernel,
        grid=(1,),
        in_specs=[_whole(a.shape) for a in args],
        out_specs=[_whole(o.shape) for o in out_shape],
        out_shape=out_shape,
        compiler_params=pltpu.CompilerParams(dimension_semantics=("arbitrary",), vmem_limit_bytes=VMEM_LIMIT),
        name="dec_l0_proj",
    )(*args)


def _row_to_col(row, eye):
    return jnp.sum(jnp.where(eye, row, 0.0), axis=-1, keepdims=True)


def _col_to_row(colv, eye):
    return jnp.sum(jnp.where(eye, colv, 0.0), axis=0, keepdims=True)


def _dec_mem_attention(q, kv_ref_view):
    kk = kv_ref_view[:, 0]
    vv = kv_ref_view[:, 1]
    s = jnp.sum(kk * (q * (M_HDIM ** -0.5))[None], axis=-1, keepdims=True)
    mx = jnp.max(s, axis=0, keepdims=True)
    p = jnp.exp(s - mx)
    return jnp.sum(p * vv, axis=0) / jnp.sum(p, axis=0)


def _dec_mlstm_kernel(q_ref, k_ref, v_ref, gates_ref, m_ref, c_ref, n_ref, qm_ref, kv_ref,
                      hs_ref, c_out, n_out, m_out, ym_ref):
    b = pl.program_id(0)
    rb = pl.ds(b, 1)
    g = gates_ref[rb, :]
    mrow = m_ref[rb, :]
    r = lax.broadcasted_iota(jnp.int32, (A_HDIM, A_HDIM), 0)
    c = lax.broadcasted_iota(jnp.int32, (A_HDIM, A_HDIM), 1)
    eye = r == c
    lane = lax.broadcasted_iota(jnp.int32, (1, LANES), 1)
    heads = range(A_HEADS)
    sl = [slice(h * A_HDIM, (h + 1) * A_HDIM) for h in heads]
    qh = [q_ref[rb, sl[h]] for h in heads]
    kh = [k_ref[rb, sl[h]] for h in heads]
    vh = [v_ref[rb, sl[h]] for h in heads]
    c_old = [c_ref[0, 0, h] for h in heads]
    n_old = [n_ref[0, 0, h:h + 1, :] for h in heads]
    li = [g[:, h:h + 1] for h in heads]
    lf = [_log_sigmoid(g[:, 4 + h:5 + h]) for h in heads]
    m_old = [mrow[:, h:h + 1] for h in heads]
    cq = [jnp.sum(c_old[h] * qh[h], axis=-1, keepdims=True) for h in heads]
    v_col = [_row_to_col(vh[h], eye) for h in heads]
    nq = [jnp.sum(n_old[h] * qh[h], axis=-1, keepdims=True) for h in heads]
    qk = [jnp.sum(qh[h] * kh[h], axis=-1, keepdims=True) for h in heads]
    inter = [lf[h] + m_old[h] for h in heads]
    m_new = [jnp.maximum(inter[h], li[h]) for h in heads]
    ws = [jnp.exp(li[h] - m_new[h]) for h in heads]
    dec = [jnp.exp(inter[h] - m_new[h]) for h in heads]
    sc = [qk[h] * ws[h] for h in heads]
    den = [sc[h] + dec[h] * nq[h] for h in heads]
    h_col = [(sc[h] * v_col[h] + dec[h] * cq[h]) / jnp.maximum(jnp.abs(den[h]), jnp.exp(-m_new[h]))
             for h in heads]
    for h in heads:
        c_out[0, 0, h] = dec[h] * c_old[h] + (ws[h] * v_col[h]) * kh[h]
        n_out[0, 0, h:h + 1, :] = dec[h] * n_old[h] + ws[h] * kh[h]
    for h in heads:
        hs_ref[0, :, sl[h]] = _col_to_row(h_col[h], eye)
    m_acc = jnp.zeros((1, LANES), F32)
    for h in heads:
        m_acc = m_acc + jnp.where(lane == h, m_new[h], 0.0)
    m_out[0] = m_acc
    ym_ref[0] = _dec_mem_attention(qm_ref[0], kv_ref.at[0, 0])


def _dec_mlstm(q, k, v, gates, m_in, state_c, state_n, qm3, cache_mem_kv):
    nb = q.shape[0]
    per_b3 = lambda b: (b, 0, 0)
    in_specs = [_whole(q.shape), _whole(k.shape), _whole(v.shape), _whole(gates.shape), _whole(m_in.shape),
                pl.BlockSpec((1, 1, A_HEADS, A_HDIM, A_HDIM), lambda b: (0, b, 0, 0, 0)),
                pl.BlockSpec((1, 1, A_HEADS, A_HDIM), lambda b: (0, b, 0, 0)),
                pl.BlockSpec((1, M_HEADS, M_HDIM), per_b3),
                pl.BlockSpec((1, 1, N_MEM, 2, M_HEADS, M_HDIM), lambda b: (0, b, 0, 0, 0, 0))]
    out_specs = [pl.BlockSpec((1, 1, A_INNER), per_b3),
                 pl.BlockSpec((1, 1, A_HEADS, A_HDIM, A_HDIM), lambda b: (0, b, 0, 0, 0)),
                 pl.BlockSpec((1, 1, A_HEADS, A_HDIM), lambda b: (0, b, 0, 0)),
                 pl.BlockSpec((1, 1, LANES), per_b3),
                 pl.BlockSpec((1, M_HEADS, M_HDIM), per_b3)]
    out_shape = [jax.ShapeDtypeStruct((nb, 1, A_INNER), F32),
                 jax.ShapeDtypeStruct(state_c.shape, F32),
                 jax.ShapeDtypeStruct(state_n.shape, F32),
                 jax.ShapeDtypeStruct((nb, 1, LANES), F32),
                 jax.ShapeDtypeStruct((nb, M_HEADS, M_HDIM), F32)]
    return pl.pallas_call(
        _dec_mlstm_kernel,
        grid=(nb,),
        in_specs=in_specs,
        out_specs=out_specs,
        out_shape=out_shape,
        compiler_params=pltpu.CompilerParams(dimension_semantics=("arbitrary",), vmem_limit_bytes=VMEM_LIMIT),
        name="dec_mlstm",
    )(q, k, v, gates, m_in, state_c, state_n, qm3, cache_mem_kv)


def _dec_mid_kernel(hs_ref, opre_ref, xc_ref, zg_ref, ym_ref, zm_ref, x_ref, ghn_ref, skip_ref, wout_ref,
                    gpost_ref, gkv_ref, gpre_ref, wkv_ref, win_ref, cos_ref, sin_ref,
                    x1_ref, q_ref, k_ref, v_ref, zg1_ref, qm1_ref, zm1_ref):
    hh = _sigmoid(opre_ref[...]) * hs_ref[...]
    parts = []
    for h in range(A_HEADS):
        v = hh[:, h * A_HDIM:(h + 1) * A_HDIM]
        mu = jnp.mean(v, axis=-1, keepdims=True)
        var = jnp.mean(jnp.square(v - mu), axis=-1, keepdims=True)
        parts.append((v - mu) * lax.rsqrt(var + EPS))
    y = jnp.concatenate(parts, axis=-1) * ghn_ref[...] + skip_ref[...] * xc_ref[...]
    ymix = (y * _silu(zg_ref[...])).astype(BF16)
    ym = (ym_ref[...] * _silu(zm_ref[...])).astype(BF16)
    out = _dot(ymix, wout_ref[0:A_INNER, :]) + _dot(ym, wout_ref[A_INNER:A_INNER + M_WIDTH, :])
    x1 = x_ref[...] + _rms_scale(out) * gpost_ref[...]
    x1_ref[...] = x1
    xn = _rms_scale(x1)
    hk = (xn * gkv_ref[...]).astype(BF16)
    hq = (xn * gpre_ref[...]).astype(BF16)
    cos = cos_ref[...]
    sin = sin_ref[...]
    ks, vs = [], []
    for g in range(N_GROUPS):
        ks.append(_rope_cols(_dot(hk, wkv_ref[:, g * 2 * B_WIDTH:g * 2 * B_WIDTH + B_WIDTH]), cos, sin))
        vs.append(_dot(hk, wkv_ref[:, g * 2 * B_WIDTH + B_WIDTH:(g + 1) * 2 * B_WIDTH]))
    k_ref[...] = jnp.concatenate(ks, axis=-1)
    v_ref[...] = jnp.concatenate(vs, axis=-1)
    qoff = N_GROUPS * B_WIDTH
    q_ref[...] = _rope_cols(_dot(hq, win_ref[:, 0:qoff]), cos, sin)
    zg1_ref[...] = _dot(hq, win_ref[:, qoff:qoff + B_WIDTH])
    qm1_ref[...] = _dot(hq, win_ref[:, qoff + B_WIDTH:qoff + B_WIDTH + M_WIDTH])
    zm1_ref[...] = _dot(hq, win_ref[:, qoff + B_WIDTH + M_WIDTH:qoff + B_WIDTH + 2 * M_WIDTH])


def _dec_mid(hs, opre, xc, zg, ym, zm, x, ghn, skip, w_out, g_post, g_kv, g_pre, wkv, win, cos, sin):
    nb = x.shape[0]
    args = (hs, opre, xc, zg, ym, zm, x, ghn, skip, w_out, g_post, g_kv, g_pre, wkv, win, cos, sin)
    f = lambda *s: jax.ShapeDtypeStruct(s, F32)
    out_shape = [f(nb, D_MODEL), f(nb, N_GROUPS * B_WIDTH), f(nb, N_GROUPS * B_WIDTH), f(nb, N_GROUPS * B_WIDTH),
                 f(nb, B_WIDTH), f(nb, M_WIDTH), f(nb, M_WIDTH)]
    return pl.pallas_call(
        _dec_mid_kernel,
        grid=(1,),
        in_specs=[_whole(a.shape) for a in args],
        out_specs=[_whole(o.shape) for o in out_shape],
        out_shape=out_shape,
        compiler_params=pltpu.CompilerParams(dimension_semantics=("arbitrary",), vmem_limit_bytes=VMEM_LIMIT),
        name="dec_mid",
    )(*args)


def _dec_attn_kernel(q_ref, kn_ref, vn_ref, w0_ref, w1_ref, w2_ref, qm_ref, kv_ref, ydil_ref, ym_ref):
    w_refs = (w0_ref, w1_ref, w2_ref)
    scale = B_HDIM ** -0.5
    groups = range(N_GROUPS)
    q = [q_ref[0, g] * scale for g in groups]
    s_c = [jnp.sum(w_refs[g][0, :, 0] * q[g][None], axis=-1, keepdims=True) for g in groups]
    s_n = [jnp.sum(kn_ref[0, g] * q[g], axis=-1, keepdims=True) for g in groups]
    mxs = [jnp.maximum(jnp.max(s_c[g], axis=0), s_n[g]) for g in groups]
    p_c = [jnp.exp(s_c[g] - mxs[g][None]) for g in groups]
    p_n = [jnp.exp(s_n[g] - mxs[g]) for g in groups]
    ls = [jnp.sum(p_c[g], axis=0) + p_n[g] for g in groups]
    outs = [(jnp.sum(p_c[g] * w_refs[g][0, :, 1], axis=0) + p_n[g] * vn_ref[0, g]) / ls[g] for g in groups]
    lses = [mxs[g] + jnp.log(ls[g]) for g in groups]
    mx = jnp.maximum(jnp.maximum(lses[0], lses[1]), lses[2])
    es = [jnp.exp(l - mx) for l in lses]
    tot = es[0] + es[1] + es[2]
    ydil_ref[0] = (es[0] / tot) * outs[0] + (es[1] / tot) * outs[1] + (es[2] / tot) * outs[2]
    ym_ref[0] = _dec_mem_attention(qm_ref[0], kv_ref.at[0, 0])


def _dec_attn(q4, kn4, vn4, cw0, cw1, cw2, qm3, cache_mem_kv, layer):
    nb = q4.shape[0]
    per_b3 = lambda b: (b, 0, 0)
    per_b4 = lambda b: (b, 0, 0, 0)
    rows = B_GROUPS[0][0]
    win_specs = [pl.BlockSpec((1, rows, 2, B_HEADS, B_HDIM), lambda b: (b, 0, 0, 0, 0))]
    for cw in (cw1, cw2):
        win_specs.append(pl.BlockSpec((1, rows, None, 2, B_HEADS, B_HDIM), lambda b: (b, 0, 0, 0, 0, 0)))
    in_specs = [pl.BlockSpec((1, N_GROUPS, B_HEADS, B_HDIM), per_b4)] * 3 + win_specs + [
        pl.BlockSpec((1, M_HEADS, M_HDIM), per_b3),
        pl.BlockSpec((1, 1, N_MEM, 2, M_HEADS, M_HDIM), lambda b: (layer, b, 0, 0, 0, 0))]
    return pl.pallas_call(
        _dec_attn_kernel,
        grid=(nb,),
        in_specs=in_specs,
        out_specs=[pl.BlockSpec((1, B_HEADS, B_HDIM), per_b3), pl.BlockSpec((1, M_HEADS, M_HDIM), per_b3)],
        out_shape=[jax.ShapeDtypeStruct((nb, B_HEADS, B_HDIM), F32),
                   jax.ShapeDtypeStruct((nb, M_HEADS, M_HDIM), F32)],
        compiler_params=pltpu.CompilerParams(dimension_semantics=("arbitrary",), vmem_limit_bytes=VMEM_LIMIT),
        name="dec_attn",
    )(q4, kn4, vn4, cw0, cw1, cw2, qm3, cache_mem_kv)


def _dec_out_kernel(ydil_ref, zg_ref, ym_ref, zm_ref, x_ref, wout_ref, gpost_ref, y_ref):
    ymix = (ydil_ref[...] * _silu(zg_ref[...])).astype(BF16)
    ym = (ym_ref[...] * _silu(zm_ref[...])).astype(BF16)
    out = _dot(ymix, wout_ref[0:B_WIDTH, :]) + _dot(ym, wout_ref[B_WIDTH:B_WIDTH + M_WIDTH, :])
    y_ref[...] = x_ref[...] + _rms_scale(out) * gpost_ref[...]


def _dec_out(ydil, zg, ym, zm, x1, w_out, g_post):
    args = (ydil, zg, ym, zm, x1, w_out, g_post)
    return pl.pallas_call(
        _dec_out_kernel,
        grid=(1,),
        in_specs=[_whole(a.shape) for a in args],
        out_specs=_whole(x1.shape),
        out_shape=jax.ShapeDtypeStruct(x1.shape, F32),
        compiler_params=pltpu.CompilerParams(dimension_semantics=("arbitrary",), vmem_limit_bytes=VMEM_LIMIT),
        name="dec_out",
    )(*args)


def _sample_group(x_sample, state_conv, state_c, state_n, state_m, cache_wins, cache_mem_kv, p):
    nb = x_sample.shape[0]
    x = x_sample.reshape(nb, D_MODEL)
    cst = state_conv[0].transpose(1, 0, 2)
    q, k, v, gates, xc, opre, zg, qm, zm, cnew = _dec_l0_proj(
        x, p['g_pre'][0:1], p['w_in_a'][0], cst, p['conv_w_a'][0], p['conv_b_a'], p['w_q_a'][0],
        p['w_k_a'][0], p['w_v_a'][0], p['w_if_a'], p['b_if_a'])
    m_in = jnp.pad(state_m[0], ((0, 0), (0, LANES - A_HEADS)))
    hs, c_s, n_s, m_pad, ym0 = _dec_mlstm(q, k, v, gates, m_in, state_c, state_n,
                                          qm.reshape(nb, M_HEADS, M_HDIM), cache_mem_kv)
    pos = PAST_LEN + jnp.arange(1, dtype=F32)
    cos, sin = _rope_tables(pos)
    x1, qd, kn, vn, zg1, qm1, zm1 = _dec_mid(
        hs.reshape(nb, A_INNER), opre, xc, zg, ym0.reshape(nb, M_WIDTH), zm, x, p['g_hn_a'], p['skip_a'],
        p['w_out_a'][0], p['g_post'][0:1], p['g_kv'], p['g_pre'][1:2], p['w_kv_b'], p['w_in_b'][0], cos, sin)
    shp4 = (nb, N_GROUPS, B_HEADS, B_HDIM)
    kn4 = kn.reshape(shp4)
    vn4 = vn.reshape(shp4)
    cws = [cache_wins[0]]
    for g in (1, 2):
        w, d = B_GROUPS[g]
        cws.append(cache_wins[g].reshape(nb, w // d, d, 2, B_HEADS, B_HDIM))
    ydil, ym1 = _dec_attn(qd.reshape(shp4), kn4, vn4, cws[0], cws[1], cws[2],
                          qm1.reshape(nb, M_HEADS, M_HDIM), cache_mem_kv, 1)
    y = _dec_out(ydil.reshape(nb, B_WIDTH), zg1, ym1.reshape(nb, M_WIDTH), zm1, x1, p['w_out_b'][0],
                 p['g_post'][1:2])
    conv_s = cnew.transpose(1, 0, 2)[None]
    m_s = m_pad[:, 0, 0:A_HEADS][None]
    wins_s = [jnp.stack([kn4[:, g], vn4[:, g]], axis=1)[:, None] for g in range(N_GROUPS)]
    return y.reshape(nb, 1, D_MODEL), conv_s, c_s, n_s, m_s, wins_s


def kernel(x_prompt, x_sample, mem_prompt, state_conv, state_C, state_n, state_m, cache_win0, cache_win1,
           cache_win2, cache_mem_kv, g_pre, g_post, w_in_a, conv_w_a, conv_b_a, w_q_a, w_k_a, w_v_a, w_if_a,
           b_if_a, g_hn_a, skip_a, w_out_a, g_kv, w_kv_b, w_in_b, w_out_b, w_mkv):
    p = _prep_params(g_pre, g_post, w_in_a, conv_w_a, conv_b_a, w_q_a, w_k_a, w_v_a, w_if_a, b_if_a,
                     g_hn_a, skip_a, w_out_a, g_kv, w_kv_b, w_in_b, w_out_b, w_mkv)
    y_p, conv_p, c_p, n_p, m_p, wins_p, memkv_p = _prompt_group(x_prompt, mem_prompt, p)
    y_s, conv_s, c_s, n_s, m_s, wins_s = _sample_group(
        x_sample, state_conv, state_C, state_n, state_m, (cache_win0, cache_win1, cache_win2), cache_mem_kv, p)
    return (y_p, y_s, conv_p, c_p, n_p, m_p, wins_p[0], wins_p[1], wins_p[2], memkv_p,
            conv_s, c_s, n_s, m_s, wins_s[0], wins_s[1], wins_s[2])
```

```python
import functools

import jax
import jax.numpy as jnp
from jax import lax
from jax.experimental import pallas as pl
from jax.experimental.pallas import tpu as pltpu

F32 = jnp.float32
BF16 = jnp.bfloat16

D_MODEL = 1024
A_HEADS = 4
A_HDIM = 256
A_INNER = 1024
CONV_W = 4
A_CHUNK = 128
B_GROUPS = ((128, 1), (512, 4), (2048, 16))
N_GROUPS = 3
B_HEADS = 4
B_HDIM = 128
B_WIDTH = 512
N_MEM = 256
M_HEADS = 4
M_HDIM = 128
M_WIDTH = 512
ROPE_THETA = 10000.0
EPS = 1e-6
PAST_LEN = 8192

LANES = 128
TOK_TILE = 512
L0_TILE = 256
ATT_BLK = 128
VMEM_LIMIT = 56 * 1024 * 1024

NT_DIMS = (((1,), (1,)), ((), ()))
LOG2E = 1.4426950408889634


def _dot(a, b):
    return jnp.dot(a, b, preferred_element_type=F32)


def _dot_nt(a, b):
    return lax.dot_general(a, b, NT_DIMS, preferred_element_type=F32)


def _sigmoid(x):
    return 1.0 / (1.0 + jnp.exp(-x))


def _silu(x):
    return x * _sigmoid(x)


def _log_sigmoid(x):
    return jnp.minimum(x, 0.0) - jnp.log(1.0 + jnp.exp(-jnp.abs(x)))


def _rms_scale(x):
    return x * lax.rsqrt(jnp.mean(x * x, axis=-1, keepdims=True) + EPS)


def _const_spec(shape):
    nd = len(shape)
    return pl.BlockSpec(shape, lambda *_: (0,) * nd, pipeline_mode=pl.Buffered(1))


def _mem_attention(qm, mk, mv):
    heads = range(M_HEADS)
    sl = [slice(h * M_HDIM, (h + 1) * M_HDIM) for h in heads]
    s = [_dot_nt(qm[:, sl[h]], mk[:, sl[h]]) * (M_HDIM ** -0.5) for h in heads]
    mx = [jnp.max(s[h], axis=-1, keepdims=True) for h in heads]
    p = [jnp.exp(s[h] - mx[h]) for h in heads]
    l = [jnp.sum(p[h], axis=-1, keepdims=True) for h in heads]
    outs = [_dot((p[h] / l[h]).astype(BF16), mv[:, sl[h]]) for h in heads]
    return jnp.concatenate(outs, axis=-1)


def _rows_to_kv_heads(k, v):
    pieces = [a[:, h * LANES:(h + 1) * LANES] for a in (k, v) for h in range(a.shape[1] // LANES)]
    return jnp.swapaxes(jnp.stack(pieces), 0, 1)


def _memkv_kernel(m_ref, w_ref, o_ref, ob_ref):
    r = _dot(m_ref[...].astype(BF16), w_ref[0])
    o_ref[0] = _rows_to_kv_heads(r[:, 0:M_WIDTH], r[:, M_WIDTH:2 * M_WIDTH])
    ob_ref[0] = r.astype(BF16)


def _memkv(mem2d, w_bf):
    nm = mem2d.shape[0]
    nl = w_bf.shape[0]
    tm = min(512, nm)
    return pl.pallas_call(
        _memkv_kernel,
        grid=(nl, nm // tm),
        in_specs=[pl.BlockSpec((tm, D_MODEL), lambda l, i: (i, 0)),
                  pl.BlockSpec((1, D_MODEL, 2 * M_WIDTH), lambda l, i: (l, 0, 0))],
        out_specs=[pl.BlockSpec((1, tm, 2 * M_HEADS, M_HDIM), lambda l, i: (l, i, 0, 0)),
                   pl.BlockSpec((1, tm, 2 * M_WIDTH), lambda l, i: (l, i, 0))],
        out_shape=[jax.ShapeDtypeStruct((nl, nm, 2 * M_HEADS, M_HDIM), F32),
                   jax.ShapeDtypeStruct((nl, nm, 2 * M_WIDTH), BF16)],
        compiler_params=pltpu.CompilerParams(dimension_semantics=("arbitrary", "arbitrary")),
        name="memkv",
    )(mem2d, w_bf)


def _mlstm_chunk(rs, g, qkv_v, kt_v, c_s, n_s, m_s, causal, tri, hs):
    ls = _log_sigmoid(g)
    t0 = ls.astype(BF16)
    e1 = ls - t0.astype(F32)
    t1 = e1.astype(BF16)
    t2 = (e1 - t1.astype(F32)).astype(BF16)
    bc = _dot(tri, t0) + _dot(tri, t1) + _dot(tri, t2)
    lane = lax.broadcasted_iota(jnp.int32, (A_CHUNK, LANES), 1)
    xt = jnp.where(lane < A_HEADS, g, bc).T
    yield
    heads = range(A_HEADS)
    b_col = [bc[:, 4 + h:5 + h] for h in heads]
    b_row = [xt[4 + h:5 + h, :] for h in heads]
    li_row = [xt[h:h + 1, :] for h in heads]
    li_col = [g[:, h:h + 1] for h in heads]
    m_old = [m_s[h:h + 1, 0:1] for h in heads]
    b_last = [bc[A_CHUNK - 1:A_CHUNK, 4 + h:5 + h] for h in heads]
    qh = [qkv_v[rs, h * 3 * A_HDIM:h * 3 * A_HDIM + A_HDIM] for h in heads]
    kh = [qkv_v[rs, h * 3 * A_HDIM + A_HDIM:h * 3 * A_HDIM + 2 * A_HDIM] for h in heads]
    vh = [qkv_v[rs, h * 3 * A_HDIM + 2 * A_HDIM:(h + 1) * 3 * A_HDIM] for h in heads]
    kt = [kt_v[h] for h in heads]
    c_old = [c_s[h] for h in heads]
    n_old = [n_s[h:h + 1, :] for h in heads]
    qk = [_dot_nt(qh[h], kh[h]) for h in heads]
    qc = [_dot(qh[h], c_old[h].astype(BF16)) for h in heads]
    dm = [jnp.where(causal, b_col[h] - b_row[h] + li_row[h], -jnp.inf) for h in heads]
    inter = [b_col[h] + m_old[h] for h in heads]
    m_row = [jnp.maximum(inter[h], jnp.max(dm[h], axis=-1, keepdims=True)) for h in heads]
    g_max = [jnp.max(b_last[h] - b_row[h] + li_row[h], axis=-1, keepdims=True) for h in heads]
    m_new = [jnp.maximum(b_last[h] + m_old[h], g_max[h]) for h in heads]
    yield
    sc = [qk[h] * jnp.exp(dm[h] - m_row[h]) for h in heads]
    dec = [jnp.exp(inter[h] - m_row[h]) for h in heads]
    ws_col = [jnp.exp(b_last[h] - b_col[h] + li_col[h] - m_new[h]) for h in heads]
    dc = [jnp.exp(b_last[h] + m_old[h] - m_new[h]) for h in heads]
    yield
    sv = [_dot(sc[h].astype(BF16), vh[h]) for h in heads]
    wv = [(ws_col[h] * vh[h].astype(F32)).astype(BF16) for h in heads]
    upd = [_dot(kt[h], wv[h]) for h in heads]
    yield
    for h in heads:
        den = (jnp.sum(sc[h], axis=-1, keepdims=True)
               + dec[h] * jnp.sum(qh[h].astype(F32) * n_old[h], axis=-1, keepdims=True))
        num = sv[h] + dec[h] * qc[h]
        hs.append(num / jnp.maximum(jnp.abs(den), jnp.exp(-m_row[h])))
    yield
    for h in heads:
        c_s[h] = dc[h] * c_old[h] + upd[h]
        n_s[h:h + 1, :] = dc[h] * n_old[h] + jnp.sum(ws_col[h] * kh[h].astype(F32), axis=0, keepdims=True)
        m_s[h:h + 1, :] = jnp.broadcast_to(m_new[h], (1, LANES))
    yield


def _l0p_kernel(nt, x_ref, xp_ref, gpre_ref, win_ref, convw_ref, convb_ref, wq_ref, wk_ref, wkt_ref, wv_ref,
                wif_ref, bif_ref, ghn_ref, skip_ref, mkv_ref, wout_ref, gpost_ref,
                x1_ref, conv_out, c_out, n_out, m_out,
                h_s, u_s, ymix_s, xc_s, opre_s, zg_s, qm_s, zm_s, qkv_s, kt_s, gates_s,
                c_s, n_s, m_s):
    tt = x_ref.shape[1]
    nsub = tt // A_CHUNK
    t = pl.program_id(0)
    parity = lax.rem(t + 1, 2)
    pos1 = lax.rem(t + nt - 1, nt)
    pos2 = lax.rem(t + 2 * nt - 2, nt)

    @pl.when(t == 0)
    def _():
        h_s[...] = jnp.zeros(h_s.shape, BF16)
        u_s[...] = jnp.zeros(u_s.shape, F32)
        xc_s[0] = jnp.zeros(xc_s.shape[1:], F32)
        opre_s[0] = jnp.zeros(opre_s.shape[1:], F32)
        zg_s[0] = jnp.zeros(zg_s.shape[1:], F32)
        qm_s[0] = jnp.zeros(qm_s.shape[1:], BF16)
        zm_s[0] = jnp.zeros(zm_s.shape[1:], F32)
        qkv_s[0] = jnp.zeros(qkv_s.shape[1:], BF16)
        kt_s[0] = jnp.zeros(kt_s.shape[1:], BF16)
        gates_s[0] = jnp.zeros(gates_s.shape[1:], F32)

    @pl.when(pos1 == 0)
    def _():
        u_s[0:8, :] = jnp.zeros((8, A_INNER), F32)

    @pl.when(pos2 == 0)
    def _():
        c_s[...] = jnp.zeros(c_s.shape, F32)
        n_s[...] = jnp.zeros(n_s.shape, F32)
        m_s[...] = jnp.zeros(m_s.shape, F32)

    row = lax.broadcasted_iota(jnp.int32, (A_CHUNK, A_CHUNK), 0)
    col = lax.broadcasted_iota(jnp.int32, (A_CHUNK, A_CHUNK), 1)
    causal = col <= row
    tri = jnp.where(causal, 1.0, 0.0).astype(BF16)
    ghn = ghn_ref[...]
    skp = skip_ref[...]
    mk = mkv_ref[0, :, 0:M_WIDTH]
    mv = mkv_ref[0, :, M_WIDTH:2 * M_WIDTH]

    def stage2(pslot):
        for c in range(nsub):
            rs = slice(c * A_CHUNK, (c + 1) * A_CHUNK)
            hs = []
            yield from _mlstm_chunk(rs, gates_s[pslot, rs, :], qkv_s.at[pslot], kt_s.at[pslot, :, c],
                                    c_s, n_s, m_s, causal, tri, hs)
            parts = []
            for h in range(A_HEADS):
                v = _sigmoid(opre_s[pslot, rs, h * A_HDIM:(h + 1) * A_HDIM]) * hs[h]
                mu = jnp.mean(v, axis=-1, keepdims=True)
                var = jnp.mean(jnp.square(v - mu), axis=-1, keepdims=True)
                parts.append((v - mu) * lax.rsqrt(var + EPS))
            hn = jnp.concatenate(parts, axis=-1) * ghn
            y = hn + skp * xc_s[pslot, rs, :]
            ymix_s[rs, 0:A_INNER] = (y * _silu(zg_s[pslot, rs, :])).astype(BF16)
            yield
            ym = _mem_attention(qm_s[pslot, rs, :], mk, mv) * _silu(zm_s[pslot, rs, :])
            ymix_s[rs, A_INNER:A_INNER + M_WIDTH] = ym.astype(BF16)
            yield

    def stage1(slot):
        hb = h_s[...]
        u_s[8:8 + tt, :] = _dot(hb, win_ref[:, 0:A_INNER])
        yield
        cw = convw_ref[...]
        cb = convb_ref[...]
        for c in range(nsub):
            r0 = c * A_CHUNK
            blk = u_s[r0:r0 + A_CHUNK + 8, :]
            xc = cb + pltpu.roll(blk, 3, 0)[8:, :] * cw[0:1, :]
            xc = xc + pltpu.roll(blk, 2, 0)[8:, :] * cw[1:2, :]
            xc = xc + pltpu.roll(blk, 1, 0)[8:, :] * cw[2:3, :]
            xc = xc + blk[8:, :] * cw[3:4, :]
            xc_s[slot, r0:r0 + A_CHUNK, :] = _silu(xc)
        opre_s[slot] = _dot(hb, win_ref[:, A_INNER:2 * A_INNER])
        yield
        zg_s[slot] = _dot(hb, win_ref[:, 2 * A_INNER:3 * A_INNER])
        yield
        qm_s[slot] = _dot(hb, win_ref[:, 3 * A_INNER:3 * A_INNER + M_WIDTH]).astype(BF16)
        zm_s[slot] = _dot(hb, win_ref[:, 3 * A_INNER + M_WIDTH:3 * A_INNER + 2 * M_WIDTH])
        yield
        for h in range(A_HEADS):
            sl = slice(h * A_HDIM, (h + 1) * A_HDIM)
            xh = xc_s[slot, :, sl].astype(BF16)
            uh = u_s[8:8 + tt, sl].astype(BF16)
            base = h * 3 * A_HDIM
            qkv_s[slot, :, base:base + A_HDIM] = _dot(xh, wq_ref[h]).astype(BF16)
            qkv_s[slot, :, base + A_HDIM:base + 2 * A_HDIM] = (
                _dot(xh, wk_ref[h]) * (A_HDIM ** -0.5)).astype(BF16)
            qkv_s[slot, :, base + 2 * A_HDIM:base + 3 * A_HDIM] = _dot(uh, wv_ref[h]).astype(BF16)
            kt = (_dot_nt(wkt_ref[h], xh) * (A_HDIM ** -0.5)).astype(BF16)
            for c in range(nsub):
                kt_s[slot, h, c] = kt[:, c * A_CHUNK:(c + 1) * A_CHUNK]
            yield
        gates_s[slot] = _dot(qkv_s[slot], wif_ref[...]) + bif_ref[...]
        yield

    def step(slot):
        pending = [stage1(slot), stage2(1 - slot)]
        while pending:
            for gen in list(pending):
                try:
                    next(gen)
                except StopIteration:
                    pending.remove(gen)
        h_next = (_rms_scale(x_ref[0]) * gpre_ref[...]).astype(BF16)
        out = _dot(ymix_s[...], wout_ref[...])
        h_s[...] = h_next
        x1_ref[0] = xp_ref[0] + _rms_scale(out) * gpost_ref[...]

    for s in range(2):
        pl.when(parity == s)(functools.partial(step, s))

    @pl.when(jnp.logical_and(pos1 == nt - 1, t > 0))
    def _():
        conv_out[0, 0] = u_s[tt + 5:tt + 8, :]

    u_s[0:8, :] = u_s[tt:tt + 8, :]

    @pl.when(jnp.logical_and(pos2 == nt - 1, t > 1))
    def _():
        for h in range(A_HEADS):
            c_out[0, 0, h] = c_s[h].T
        n_out[0, 0] = n_s[0:A_HEADS, :]
        m_out[0] = m_s[...]


def _layer0_prompt_pipelined(x, g_pre, w_in, conv_w, conv_b, wq, wk, wkt, wv, wif, bif, ghn, skip, mkv_bf,
                             w_out, g_post):
    b, s, _ = x.shape
    tt = min(L0_TILE, s)
    nt = s // tt
    ntiles = b * nt
    a_in = w_in.shape[1]
    nsub = tt // A_CHUNK

    def cur(t):
        t1 = jnp.minimum(t, ntiles - 1)
        return (t1 // nt, t1 % nt, 0)

    def prev(t):
        t2 = jnp.maximum(t - 2, 0)
        return (t2 // nt, t2 % nt, 0)

    def prev_b(t):
        return jnp.maximum(t - 2, 0) // nt

    in_specs = [
        pl.BlockSpec((1, tt, D_MODEL), cur),
        pl.BlockSpec((1, tt, D_MODEL), prev),
        _const_spec((1, D_MODEL)),
        _const_spec((D_MODEL, a_in)),
        _const_spec((CONV_W, A_INNER)),
        _const_spec((1, A_INNER)),
        _const_spec((A_HEADS, A_HDIM, A_HDIM)),
        _const_spec((A_HEADS, A_HDIM, A_HDIM)),
        _const_spec((A_HEADS, A_HDIM, A_HDIM)),
        _const_spec((A_HEADS, A_HDIM, A_HDIM)),
        _const_spec((3 * A_INNER, LANES)),
        _const_spec((1, LANES)),
        _const_spec((1, A_INNER)),
        _const_spec((1, A_INNER)),
        pl.BlockSpec((1, N_MEM, 2 * M_WIDTH), lambda t: (prev_b(t), 0, 0)),
        _const_spec((A_INNER + M_WIDTH, D_MODEL)),
        _const_spec((1, D_MODEL)),
    ]
    out_specs = [
        pl.BlockSpec((1, tt, D_MODEL), prev),
        pl.BlockSpec((1, 1, CONV_W - 1, A_INNER), lambda t: (0, prev_b(t), 0, 0)),
        pl.BlockSpec((1, 1, A_HEADS, A_HDIM, A_HDIM), lambda t: (0, prev_b(t), 0, 0, 0)),
        pl.BlockSpec((1, 1, A_HEADS, A_HDIM), lambda t: (0, prev_b(t), 0, 0)),
        pl.BlockSpec((1, 8, LANES), lambda t: (prev_b(t), 0, 0)),
    ]
    out_shape = [
        jax.ShapeDtypeStruct((b, s, D_MODEL), F32),
        jax.ShapeDtypeStruct((1, b, CONV_W - 1, A_INNER), F32),
        jax.ShapeDtypeStruct((1, b, A_HEADS, A_HDIM, A_HDIM), F32),
        jax.ShapeDtypeStruct((1, b, A_HEADS, A_HDIM), F32),
        jax.ShapeDtypeStruct((b, 8, LANES), F32),
    ]
    scratch = [
        pltpu.VMEM((tt, D_MODEL), BF16),
        pltpu.VMEM((tt + 8, A_INNER), F32),
        pltpu.VMEM((tt, A_INNER + M_WIDTH), BF16),
        pltpu.VMEM((2, tt, A_INNER), F32),
        pltpu.VMEM((2, tt, A_INNER), F32),
        pltpu.VMEM((2, tt, A_INNER), F32),
        pltpu.VMEM((2, tt, M_WIDTH), BF16),
        pltpu.VMEM((2, tt, M_WIDTH), F32),
        pltpu.VMEM((2, tt, 3 * A_INNER), BF16),
        pltpu.VMEM((2, A_HEADS, nsub, A_HDIM, A_CHUNK), BF16),
        pltpu.VMEM((2, tt, LANES), F32),
        pltpu.VMEM((A_HEADS, A_HDIM, A_HDIM), F32),
        pltpu.VMEM((8, A_HDIM), F32),
        pltpu.VMEM((8, LANES), F32),
    ]
    return pl.pallas_call(
        functools.partial(_l0p_kernel, nt),
        grid=(ntiles + 2,),
        in_specs=in_specs,
        out_specs=out_specs,
        out_shape=out_shape,
        scratch_shapes=scratch,
        compiler_params=pltpu.CompilerParams(
            dimension_semantics=("arbitrary",), vmem_limit_bytes=VMEM_LIMIT),
        name="layer0_prompt",
    )(x, x, g_pre, w_in, conv_w, conv_b, wq, wk, wkt, wv, wif, bif, ghn, skip, mkv_bf, w_out, g_post)


def _l0_kernel(x_ref, gpre_ref, win_ref, convw_ref, convb_ref, wq_ref, wk_ref, wkt_ref, wv_ref,
               wif_ref, bif_ref, ghn_ref, skip_ref, mkv_ref, wout_ref, gpost_ref,
               x1_ref, conv_out, c_out, n_out, m_out,
               h_s, u_s, xc_s, opre_s, zg_s, qm_s, zm_s, qkv_s, kt_s, gates_s, ymix_s,
               c_s, n_s, m_s):
    tt = x_ref.shape[1]
    nsub = tt // A_CHUNK
    i = pl.program_id(1)
    nt = pl.num_programs(1)

    @pl.when(i == 0)
    def _():
        u_s[0:8, :] = jnp.zeros((8, A_INNER), F32)
        c_s[...] = jnp.zeros(c_s.shape, F32)
        n_s[...] = jnp.zeros(n_s.shape, F32)
        m_s[...] = jnp.zeros(m_s.shape, F32)

    gpre = gpre_ref[...]

    def norm_body(c, _):
        r = pl.ds(pl.multiple_of(c * A_CHUNK, A_CHUNK), A_CHUNK)
        h_s[r, :] = (_rms_scale(x_ref[0, r, :]) * gpre).astype(BF16)
        return 0
    lax.fori_loop(0, nsub, norm_body, 0)

    hb = h_s[...]
    u_s[8:8 + tt, :] = _dot(hb, win_ref[:, 0:A_INNER])
    opre_s[...] = _dot(hb, win_ref[:, A_INNER:2 * A_INNER])
    zg_s[...] = _dot(hb, win_ref[:, 2 * A_INNER:3 * A_INNER])
    qm_s[...] = _dot(hb, win_ref[:, 3 * A_INNER:3 * A_INNER + M_WIDTH]).astype(BF16)
    zm_s[...] = _dot(hb, win_ref[:, 3 * A_INNER + M_WIDTH:3 * A_INNER + 2 * M_WIDTH])

    cw = convw_ref[...]
    cb = convb_ref[...]

    for c in range(nsub):
        r0 = c * A_CHUNK
        xc = cb + u_s[r0 + 5:r0 + 5 + A_CHUNK, :] * cw[0:1, :]
        xc = xc + u_s[r0 + 6:r0 + 6 + A_CHUNK, :] * cw[1:2, :]
        xc = xc + u_s[r0 + 7:r0 + 7 + A_CHUNK, :] * cw[2:3, :]
        xc = xc + u_s[r0 + 8:r0 + 8 + A_CHUNK, :] * cw[3:4, :]
        xc_s[r0:r0 + A_CHUNK, :] = _silu(xc)

    for h in range(A_HEADS):
        sl = slice(h * A_HDIM, (h + 1) * A_HDIM)
        xh = xc_s[:, sl].astype(BF16)
        uh = u_s[8:8 + tt, sl].astype(BF16)
        base = h * 3 * A_HDIM
        qkv_s[:, base:base + A_HDIM] = _dot(xh, wq_ref[h]).astype(BF16)
        qkv_s[:, base + A_HDIM:base + 2 * A_HDIM] = (_dot(xh, wk_ref[h]) * (A_HDIM ** -0.5)).astype(BF16)
        qkv_s[:, base + 2 * A_HDIM:base + 3 * A_HDIM] = _dot(uh, wv_ref[h]).astype(BF16)
        kt = (_dot_nt(wkt_ref[h], xh) * (A_HDIM ** -0.5)).astype(BF16)
        for c in range(nsub):
            kt_s[h, c] = kt[:, c * A_CHUNK:(c + 1) * A_CHUNK]
    gates_s[...] = _dot(qkv_s[...], wif_ref[...]) + bif_ref[...]

    row = lax.broadcasted_iota(jnp.int32, (A_CHUNK, A_CHUNK), 0)
    col = lax.broadcasted_iota(jnp.int32, (A_CHUNK, A_CHUNK), 1)
    causal = col <= row
    tri = jnp.where(causal, 1.0, 0.0).astype(BF16)

    def chunk_body(c, _):
        r0 = pl.multiple_of(c * A_CHUNK, A_CHUNK)
        rs = pl.ds(r0, A_CHUNK)
        g = gates_s[rs, :]
        ls = _log_sigmoid(g)
        t0 = ls.astype(BF16)
        e1 = ls - t0.astype(F32)
        t1 = e1.astype(BF16)
        t2 = (e1 - t1.astype(F32)).astype(BF16)
        bc = _dot(tri, t0) + _dot(tri, t1) + _dot(tri, t2)
        lane = lax.broadcasted_iota(jnp.int32, (A_CHUNK, LANES), 1)
        xt = jnp.where(lane < A_HEADS, g, bc).T
        heads = range(A_HEADS)
        b_col = [bc[:, 4 + h:5 + h] for h in heads]
        b_row = [xt[4 + h:5 + h, :] for h in heads]
        li_row = [xt[h:h + 1, :] for h in heads]
        li_col = [g[:, h:h + 1] for h in heads]
        m_old = [m_s[h:h + 1, 0:1] for h in heads]
        b_last = [bc[A_CHUNK - 1:A_CHUNK, 4 + h:5 + h] for h in heads]
        qh = [qkv_s[rs, h * 3 * A_HDIM:h * 3 * A_HDIM + A_HDIM] for h in heads]
        kh = [qkv_s[rs, h * 3 * A_HDIM + A_HDIM:h * 3 * A_HDIM + 2 * A_HDIM] for h in heads]
        vh = [qkv_s[rs, h * 3 * A_HDIM + 2 * A_HDIM:(h + 1) * 3 * A_HDIM] for h in heads]
        kt = [kt_s[h, c] for h in heads]
        c_old = [c_s[h] for h in heads]
        n_old = [n_s[h:h + 1, :] for h in heads]
        qk = [_dot_nt(qh[h], kh[h]) for h in heads]
        qc = [_dot(qh[h], c_old[h].astype(BF16)) for h in heads]
        dm = [jnp.where(causal, b_col[h] - b_row[h] + li_row[h], -jnp.inf) for h in heads]
        inter = [b_col[h] + m_old[h] for h in heads]
        m_row = [jnp.maximum(inter[h], jnp.max(dm[h], axis=-1, keepdims=True)) for h in heads]
        g_max = [jnp.max(b_last[h] - b_row[h] + li_row[h], axis=-1, keepdims=True) for h in heads]
        m_new = [jnp.maximum(b_last[h] + m_old[h], g_max[h]) for h in heads]
        sc = [qk[h] * jnp.exp(dm[h] - m_row[h]) for h in heads]
        dec = [jnp.exp(inter[h] - m_row[h]) for h in heads]
        ws_col = [jnp.exp(b_last[h] - b_col[h] + li_col[h] - m_new[h]) for h in heads]
        dc = [jnp.exp(b_last[h] + m_old[h] - m_new[h]) for h in heads]
        sv = [_dot(sc[h].astype(BF16), vh[h]) for h in heads]
        wv = [(ws_col[h] * vh[h].astype(F32)).astype(BF16) for h in heads]
        upd = [_dot(kt[h], wv[h]) for h in heads]
        hs = []
        for h in heads:
            den = (jnp.sum(sc[h], axis=-1, keepdims=True)
                   + dec[h] * jnp.sum(qh[h].astype(F32) * n_old[h], axis=-1, keepdims=True))
            num = sv[h] + dec[h] * qc[h]
            hs.append(num / jnp.maximum(jnp.abs(den), jnp.exp(-m_row[h])))
        for h in heads:
            c_s[h] = dc[h] * c_old[h] + upd[h]
            n_s[h:h + 1, :] = dc[h] * n_old[h] + jnp.sum(ws_col[h] * kh[h].astype(F32), axis=0, keepdims=True)
            m_s[h:h + 1, :] = jnp.broadcast_to(m_new[h], (1, LANES))

        parts = []
        for h in heads:
            v = _sigmoid(opre_s[rs, h * A_HDIM:(h + 1) * A_HDIM]) * hs[h]
            mu = jnp.mean(v, axis=-1, keepdims=True)
            var = jnp.mean(jnp.square(v - mu), axis=-1, keepdims=True)
            parts.append((v - mu) * lax.rsqrt(var + EPS))
        hn = jnp.concatenate(parts, axis=-1) * ghn
        y = hn + skp * xc_s[rs, :]
        ymix_s[rs, 0:A_INNER] = (y * _silu(zg_s[rs, :])).astype(BF16)
        ym = _mem_attention(qm_s[rs, :], mk, mv) * _silu(zm_s[rs, :])
        ymix_s[rs, A_INNER:A_INNER + M_WIDTH] = ym.astype(BF16)
        return 0

    ghn = ghn_ref[...]
    skp = skip_ref[...]
    mk = mkv_ref[0, :, 0:M_WIDTH]
    mv = mkv_ref[0, :, M_WIDTH:2 * M_WIDTH]
    lax.fori_loop(0, nsub, chunk_body, 0)

    out = _dot(ymix_s[...], wout_ref[...])
    x1_ref[0] = x_ref[0] + _rms_scale(out) * gpost_ref[...]

    u_s[0:8, :] = u_s[tt:tt + 8, :]

    @pl.when(i == nt - 1)
    def _():
        conv_out[0, 0] = u_s[tt + 5:tt + 8, :]
        for h in range(A_HEADS):
            c_out[0, 0, h] = c_s[h].T
        n_out[0, 0] = n_s[0:A_HEADS, :]
        m_out[0] = m_s[...]


def _layer0_prompt(x, g_pre, w_in, conv_w, conv_b, wq, wk, wkt, wv, wif, bif, ghn, skip, mkv_bf, w_out,
                   g_post):
    b, s, _ = x.shape
    tt = min(TOK_TILE, s)
    nt = s // tt
    a_in = w_in.shape[1]
    tile = lambda bb, i: (bb, i, 0)
    per_b = lambda bb, i: (bb, 0, 0)
    in_specs = [
        pl.BlockSpec((1, tt, D_MODEL), tile),
        _const_spec((1, D_MODEL)),
        _const_spec((D_MODEL, a_in)),
        _const_spec((CONV_W, A_INNER)),
        _const_spec((1, A_INNER)),
        _const_spec((A_HEADS, A_HDIM, A_HDIM)),
        _const_spec((A_HEADS, A_HDIM, A_HDIM)),
        _const_spec((A_HEADS, A_HDIM, A_HDIM)),
        _const_spec((A_HEADS, A_HDIM, A_HDIM)),
        _const_spec((3 * A_INNER, LANES)),
        _const_spec((1, LANES)),
        _const_spec((1, A_INNER)),
        _const_spec((1, A_INNER)),
        pl.BlockSpec((1, N_MEM, 2 * M_WIDTH), per_b),
        _const_spec((A_INNER + M_WIDTH, D_MODEL)),
        _const_spec((1, D_MODEL)),
    ]
    out_specs = [
        pl.BlockSpec((1, tt, D_MODEL), tile),
        pl.BlockSpec((1, 1, CONV_W - 1, A_INNER), lambda bb, i: (0, bb, 0, 0)),
        pl.BlockSpec((1, 1, A_HEADS, A_HDIM, A_HDIM), lambda bb, i: (0, bb, 0, 0, 0)),
        pl.BlockSpec((1, 1, A_HEADS, A_HDIM), lambda bb, i: (0, bb, 0, 0)),
        pl.BlockSpec((1, 8, LANES), per_b),
    ]
    out_shape = [
        jax.ShapeDtypeStruct((b, s, D_MODEL), F32),
        jax.ShapeDtypeStruct((1, b, CONV_W - 1, A_INNER), F32),
        jax.ShapeDtypeStruct((1, b, A_HEADS, A_HDIM, A_HDIM), F32),
        jax.ShapeDtypeStruct((1, b, A_HEADS, A_HDIM), F32),
        jax.ShapeDtypeStruct((b, 8, LANES), F32),
    ]
    scratch = [
        pltpu.VMEM((tt, D_MODEL), BF16),
        pltpu.VMEM((tt + 8, A_INNER), F32),
        pltpu.VMEM((tt, A_INNER), F32),
        pltpu.VMEM((tt, A_INNER), F32),
        pltpu.VMEM((tt, A_INNER), F32),
        pltpu.VMEM((tt, M_WIDTH), BF16),
        pltpu.VMEM((tt, M_WIDTH), F32),
        pltpu.VMEM((tt, 3 * A_INNER), BF16),
        pltpu.VMEM((A_HEADS, tt // A_CHUNK, A_HDIM, A_CHUNK), BF16),
        pltpu.VMEM((tt, LANES), F32),
        pltpu.VMEM((tt, A_INNER + M_WIDTH), BF16),
        pltpu.VMEM((A_HEADS, A_HDIM, A_HDIM), F32),
        pltpu.VMEM((8, A_HDIM), F32),
        pltpu.VMEM((8, LANES), F32),
    ]
    return pl.pallas_call(
        _l0_kernel,
        grid=(b, nt),
        in_specs=in_specs,
        out_specs=out_specs,
        out_shape=out_shape,
        scratch_shapes=scratch,
        compiler_params=pltpu.CompilerParams(
            dimension_semantics=("arbitrary", "arbitrary"), vmem_limit_bytes=VMEM_LIMIT),
        name="layer0_prompt",
    )(x, g_pre, w_in, conv_w, conv_b, wq, wk, wkt, wv, wif, bif, ghn, skip, mkv_bf, w_out, g_post)


def _rope_cols(x, cos, sin_signed):
    outs = []
    for cblk in range(x.shape[1] // B_HDIM):
        xb = x[:, cblk * B_HDIM:(cblk + 1) * B_HDIM]
        outs.append(xb * cos + pltpu.roll(xb, B_HDIM // 2, 1) * sin_signed)
    return jnp.concatenate(outs, axis=-1)


def _l1a_kernel(n_dec_in, *refs):
    x_ref, gkv_ref, gpre_ref, wkv_ref, win_ref, cos_ref, sin_ref = refs[0:7]
    dec_in = refs[7:7 + n_dec_in]
    (q0_ref, q1_ref, q2_ref, k0_ref, k1_ref, k2_ref, v0_ref, v1_ref, v2_ref,
     zg_ref, qm_ref, zm_ref, w0_ref, w1_ref, w2_ref) = refs[7 + n_dec_in:22 + n_dec_in]
    dec_out = refs[22 + n_dec_in:]
    tt = x_ref.shape[1]
    xn = _rms_scale(x_ref[0])
    hk = (xn * gkv_ref[...]).astype(BF16)
    hq = (xn * gpre_ref[...]).astype(BF16)
    cos = cos_ref[...]
    sin = sin_ref[...]
    q_refs = (q0_ref, q1_ref, q2_ref)
    k_refs = (k0_ref, k1_ref, k2_ref)
    v_refs = (v0_ref, v1_ref, v2_ref)
    w_refs = (w0_ref, w1_ref, w2_ref)
    for g in (2, 1, 0):
        d = B_GROUPS[g][1]
        kf = _rope_cols(_dot(hk, wkv_ref[:, g * 2 * B_WIDTH:g * 2 * B_WIDTH + B_WIDTH]), cos, sin)
        vf = _dot(hk, wkv_ref[:, g * 2 * B_WIDTH + B_WIDTH:(g + 1) * 2 * B_WIDTH])
        qf = _rope_cols(_dot(hq, win_ref[:, g * B_WIDTH:(g + 1) * B_WIDTH]), cos, sin)
        wr = w_refs[g]
        wrows = wr.shape[1]
        wr[0] = _rows_to_kv_heads(kf[tt - wrows:, :], vf[tt - wrows:, :])
        for val, ref in ((qf.astype(BF16), q_refs[g]), (kf.astype(BF16), k_refs[g]), (vf.astype(BF16), v_refs[g])):
            if d == 1:
                ref[0, 0] = val
            else:
                ref[0] = jnp.swapaxes(val.reshape(tt // d, d, val.shape[1]), 0, 1)
        if g == 2 and n_dec_in:
            _dec_attn_kernel(*dec_in, *dec_out)
    qoff = N_GROUPS * B_WIDTH
    zg_ref[0] = _dot(hq, win_ref[:, qoff:qoff + B_WIDTH]).astype(BF16)
    qm_ref[0] = _dot(hq, win_ref[:, qoff + B_WIDTH:qoff + B_WIDTH + M_WIDTH]).astype(BF16)
    zm_ref[0] = _dot(hq, win_ref[:, qoff + B_WIDTH + M_WIDTH:qoff + B_WIDTH + 2 * M_WIDTH]).astype(BF16)


def _layer1_proj_prompt(x1, g_kv, g_pre, wkv, win, cos_t, sin_t, dec_job=None):
    b, s, _ = x1.shape
    tt = min(TOK_TILE, s)
    nt = s // tt
    tile = lambda bb, i: (bb, i, 0)
    in_specs = [
        pl.BlockSpec((1, tt, D_MODEL), tile),
        _const_spec((1, D_MODEL)),
        _const_spec((1, D_MODEL)),
        _const_spec(wkv.shape),
        _const_spec(win.shape),
        pl.BlockSpec((tt, B_HDIM), lambda bb, i: (i, 0)),
        pl.BlockSpec((tt, B_HDIM), lambda bb, i: (i, 0)),
    ]
    qkv_specs, qkv_shapes = [], []
    for _ in range(3):
        for (_, d) in B_GROUPS:
            qkv_specs.append(pl.BlockSpec((1, d, tt // d, B_WIDTH), lambda bb, i: (bb, 0, i, 0)))
            qkv_shapes.append(jax.ShapeDtypeStruct((b, d, s // d, B_WIDTH), BF16))
    gate_specs = [pl.BlockSpec((1, tt, B_WIDTH), tile)] * 3
    gate_shapes = [jax.ShapeDtypeStruct((b, s, B_WIDTH), BF16)] * 3
    win_specs, win_shapes = [], []
    for (w, _) in B_GROUPS:
        wr = min(w, s)
        rows = min(wr, tt)
        nblk = wr // rows
        win_specs.append(pl.BlockSpec(
            (1, rows, 2 * B_HEADS, B_HDIM),
            functools.partial(lambda bb, i, nb: (bb, jnp.maximum(i - (nt - nb), 0), 0, 0), nb=nblk)))
        win_shapes.append(jax.ShapeDtypeStruct((b, wr, 2 * B_HEADS, B_HDIM), F32))
    dec_args, dec_specs, dec_out_specs, dec_out_shapes = [], [], [], []
    if dec_job is not None:
        dec_args, dec_specs, dec_out_specs, dec_out_shapes = _dec_attn_specs(
            *dec_job, seq_index=lambda bb, i: bb * nt + i)
    return pl.pallas_call(
        functools.partial(_l1a_kernel, len(dec_args)),
        grid=(b, nt),
        in_specs=in_specs + dec_specs,
        out_specs=qkv_specs + gate_specs + win_specs + dec_out_specs,
        out_shape=qkv_shapes + gate_shapes + win_shapes + dec_out_shapes,
        compiler_params=pltpu.CompilerParams(
            dimension_semantics=("arbitrary", "arbitrary"), vmem_limit_bytes=VMEM_LIMIT),
        name="layer1_proj_prompt",
    )(x1, g_kv, g_pre, wkv, win, cos_t, sin_t, *dec_args)


def _cols_to_lanes(cols):
    t = cols[0].shape[0]
    lane = lax.broadcasted_iota(jnp.int32, (t, LANES), 1)
    acc = jnp.zeros((t, LANES), F32)
    for h, cvec in enumerate(cols):
        acc = jnp.where(lane == h, cvec, acc)
    return acc


def _band_attn_kernel(q_ref, kc_ref, kp_ref, vc_ref, vp_ref, o_ref, lse_ref):
    nres, tq = q_ref.shape[1:3]
    nsb = tq // ATT_BLK
    j = pl.program_id(2)
    row = lax.broadcasted_iota(jnp.int32, (ATT_BLK, 2 * ATT_BLK), 0)
    col = lax.broadcasted_iota(jnp.int32, (ATT_BLK, 2 * ATT_BLK), 1)
    band = jnp.logical_and(col >= row, col <= row + ATT_BLK)
    first_pen = jnp.where(col < ATT_BLK, jnp.where(j > 0, 0.0, -jnp.inf), 0.0)
    scale = B_HDIM ** -0.5
    qs, ks, vs, pens = [], [], [], []
    for r in range(nres):
        for sb in range(nsb):
            rs = slice(sb * ATT_BLK, (sb + 1) * ATT_BLK)
            ps = slice((sb - 1) * ATT_BLK, sb * ATT_BLK)
            for h in range(B_HEADS):
                hs = slice(h * B_HDIM, (h + 1) * B_HDIM)
                qs.append(q_ref[0, r, rs, hs])
                kp = kp_ref[0, r, :, hs] if sb == 0 else kc_ref[0, r, ps, hs]
                vp = vp_ref[0, r, :, hs] if sb == 0 else vc_ref[0, r, ps, hs]
                ks.append(jnp.concatenate([kp, kc_ref[0, r, rs, hs]], axis=0))
                vs.append(jnp.concatenate([vp, vc_ref[0, r, rs, hs]], axis=0))
                pens.append(sb == 0)
    q3 = jnp.stack(qs)
    k3 = jnp.stack(ks)
    v3 = jnp.stack(vs)
    s = jnp.einsum('uqd,ukd->uqk', q3, k3, preferred_element_type=F32)
    s = jnp.stack([s[u] + first_pen if pens[u] else s[u] for u in range(len(pens))])
    s = jnp.where(band[None], s, -jnp.inf)
    mx = jnp.max(s, axis=-1, keepdims=True)
    p = jnp.exp2((s - mx) * (scale * LOG2E))
    l = jnp.sum(p, axis=-1, keepdims=True)
    o = jnp.einsum('uqk,ukd->uqd', p.astype(BF16), v3, preferred_element_type=F32) / l
    lse = mx * scale + jnp.log(l)
    for r in range(nres):
        for sb in range(nsb):
            rs = slice(sb * ATT_BLK, (sb + 1) * ATT_BLK)
            u0 = (r * nsb + sb) * B_HEADS
            for h in range(B_HEADS):
                o_ref[0, r, rs, h * B_HDIM:(h + 1) * B_HDIM] = o[u0 + h].astype(BF16)
            lse_ref[0, r, rs, :] = _cols_to_lanes([lse[u0 + h] for h in range(B_HEADS)])


def _band_attention(q, k, v):
    b, d, ls, _ = q.shape
    tq = min(TOK_TILE, ls)
    nj = ls // tq
    ratio = tq // ATT_BLK
    nres = min(d, TOK_TILE // tq)
    cur = lambda bb, r, j: (bb, r, j, 0)
    prev = lambda bb, r, j: (bb, r, jnp.maximum(j * ratio - 1, 0), 0)
    return pl.pallas_call(
        _band_attn_kernel,
        grid=(b, d // nres, nj),
        in_specs=[pl.BlockSpec((1, nres, tq, B_WIDTH), cur),
                  pl.BlockSpec((1, nres, tq, B_WIDTH), cur),
                  pl.BlockSpec((1, nres, ATT_BLK, B_WIDTH), prev),
                  pl.BlockSpec((1, nres, tq, B_WIDTH), cur),
                  pl.BlockSpec((1, nres, ATT_BLK, B_WIDTH), prev)],
        out_specs=[pl.BlockSpec((1, nres, tq, B_WIDTH), cur),
                   pl.BlockSpec((1, nres, tq, LANES), cur)],
        out_shape=[jax.ShapeDtypeStruct((b, d, ls, B_WIDTH), BF16),
                   jax.ShapeDtypeStruct((b, d, ls, LANES), F32)],
        compiler_params=pltpu.CompilerParams(
            dimension_semantics=("arbitrary", "arbitrary", "arbitrary"), vmem_limit_bytes=VMEM_LIMIT),
        name="band_attention_d%d" % d,
    )(q, k, k, v, v)


def _unpermute(ref):
    d, rows, width = ref.shape[1:]
    if d == 1:
        return ref[0, 0].astype(F32)
    return jnp.swapaxes(ref[0], 0, 1).reshape(d * rows, width).astype(F32)


def _l1c_kernel(x_ref, o0_ref, o1_ref, o2_ref, l0_ref, l1_ref, l2_ref, zg_ref, qm_ref, zm_ref,
                mkv_ref, wout_ref, gpost_ref, y_ref):
    tt = x_ref.shape[1]
    o_refs = (o0_ref, o1_ref, o2_ref)
    l_refs = (l0_ref, l1_ref, l2_ref)
    outs, lses = [], []
    for g, (_, d) in enumerate(B_GROUPS):
        outs.append(_unpermute(o_refs[g]))
        lses.append(_unpermute(l_refs[g])[:, 0:B_HEADS])
    mx = jnp.maximum(jnp.maximum(lses[0], lses[1]), lses[2])
    es = [jnp.exp(l - mx) for l in lses]
    tot = es[0] + es[1] + es[2]
    ws = [e / tot for e in es]
    parts = []
    for h in range(B_HEADS):
        hs = slice(h * B_HDIM, (h + 1) * B_HDIM)
        acc = ws[0][:, h:h + 1] * outs[0][:, hs]
        acc = acc + ws[1][:, h:h + 1] * outs[1][:, hs]
        acc = acc + ws[2][:, h:h + 1] * outs[2][:, hs]
        parts.append(acc)
    ydil = jnp.concatenate(parts, axis=-1)
    ymix = (ydil * _silu(zg_ref[0].astype(F32))).astype(BF16)
    mk = mkv_ref[0, :, 0:M_WIDTH]
    mv = mkv_ref[0, :, M_WIDTH:2 * M_WIDTH]
    ym = (_mem_attention(qm_ref[0], mk, mv) * _silu(zm_ref[0].astype(F32))).astype(BF16)
    out = _dot(ymix, wout_ref[0:B_WIDTH, :]) + _dot(ym, wout_ref[B_WIDTH:B_WIDTH + M_WIDTH, :])
    y_ref[0] = x_ref[0] + _rms_scale(out) * gpost_ref[...]


def _layer1_out_prompt(x1, os_, ls_, zg, qm, zm, mkv_bf, w_out, g_post):
    b, s, _ = x1.shape
    tt = min(TOK_TILE, s)
    nt = s // tt
    tile = lambda bb, i: (bb, i, 0)
    perm = lambda bb, i: (bb, 0, i, 0)
    in_specs = [pl.BlockSpec((1, tt, D_MODEL), tile)]
    for width in (B_WIDTH, LANES):
        for (_, d) in B_GROUPS:
            in_specs.append(pl.BlockSpec((1, d, tt // d, width), perm))
    in_specs += [pl.BlockSpec((1, tt, B_WIDTH), tile)] * 3
    in_specs += [pl.BlockSpec((1, N_MEM, 2 * M_WIDTH), lambda bb, i: (bb, 0, 0)),
                 _const_spec(w_out.shape), _const_spec((1, D_MODEL))]
    return pl.pallas_call(
        _l1c_kernel,
        grid=(b, nt),
        in_specs=in_specs,
        out_specs=pl.BlockSpec((1, tt, D_MODEL), tile),
        out_shape=jax.ShapeDtypeStruct((b, s, D_MODEL), F32),
        compiler_params=pltpu.CompilerParams(
            dimension_semantics=("arbitrary", "arbitrary"), vmem_limit_bytes=VMEM_LIMIT),
        name="layer1_out_prompt",
    )(x1, *os_, *ls_, zg, qm, zm, mkv_bf, w_out, g_post)


def _rope_tables(pos):
    half = B_HDIM // 2
    inv = ROPE_THETA ** (-jnp.arange(half, dtype=F32) / half)
    ang = pos[:, None] * inv[None, :]
    cos = jnp.cos(ang)
    sin = jnp.sin(ang)
    return jnp.concatenate([cos, cos], axis=-1), jnp.concatenate([-sin, sin], axis=-1)


def _prompt_group(x_prompt, mem_prompt, p, dec_job=None):
    b, s, _ = x_prompt.shape
    memkv_f, memkv_b = _memkv(mem_prompt.reshape(b * N_MEM, D_MODEL), p['w_mkv'])
    depth = memkv_f.shape[0]
    memkv_b = memkv_b.reshape(depth, b, N_MEM, 2 * M_WIDTH)
    x1, conv_p, c_p, n_p, m_pad = _layer0_prompt_pipelined(
        x_prompt, p['g_pre'][0:1], p['w_in_a'][0], p['conv_w_a'][0], p['conv_b_a'], p['w_q_a'][0],
        p['w_k_a'][0], jnp.swapaxes(p['w_k_a'][0], 1, 2), p['w_v_a'][0], p['w_if_a'], p['b_if_a'],
        p['g_hn_a'], p['skip_a'], memkv_b[0], p['w_out_a'][0], p['g_post'][0:1])
    cos_t, sin_t = _rope_tables(jnp.arange(s, dtype=F32))
    if dec_job is not None and b * (s // min(TOK_TILE, s)) != dec_job[0].shape[0]:
        dec_job = None
    outs = _layer1_proj_prompt(x1, p['g_kv'], p['g_pre'][1:2], p['w_kv_b'], p['w_in_b'][0], cos_t, sin_t,
                               dec_job)
    dec_res = tuple(outs[15:17]) if dec_job is not None else None
    qs, ks, vs = outs[0:3], outs[3:6], outs[6:9]
    zg, qm, zm = outs[9:12]
    wins = outs[12:15]
    os_, ls_ = [], []
    for g in range(N_GROUPS):
        o, l = _band_attention(qs[g], ks[g], vs[g])
        os_.append(o)
        ls_.append(l)
    y = _layer1_out_prompt(x1, os_, ls_, zg, qm, zm, memkv_b[1], p['w_out_b'][0], p['g_post'][1:2])
    m_p = m_pad[:, 0:A_HEADS, 0][None]
    wins = [w.reshape(b, w.shape[1], 2, B_HEADS, B_HDIM) for w in wins]
    memkv_p = memkv_f.reshape(depth, b, N_MEM, 2, M_HEADS, M_HDIM)
    return y, conv_p, c_p, n_p, m_p, wins, memkv_p, dec_res


def _prep_params(g_pre, g_post, w_in_a, conv_w_a, conv_b_a, w_q_a, w_k_a, w_v_a, w_if_a, b_if_a,
                 g_hn_a, skip_a, w_out_a, g_kv, w_kv_b, w_in_b, w_out_b, w_mkv):
    wif = jnp.pad(w_if_a[0], ((0, 0), (0, LANES - 2 * A_HEADS))).astype(BF16)
    bif = jnp.pad(b_if_a[0], (0, LANES - 2 * A_HEADS))[None, :]
    return {
        'g_pre': g_pre, 'g_post': g_post,
        'w_in_a': w_in_a.astype(BF16), 'conv_w_a': conv_w_a, 'conv_b_a': conv_b_a,
        'w_q_a': w_q_a.astype(BF16), 'w_k_a': w_k_a.astype(BF16), 'w_v_a': w_v_a.astype(BF16),
        'w_if_a': wif, 'b_if_a': bif, 'g_hn_a': g_hn_a, 'skip_a': skip_a,
        'w_out_a': w_out_a.astype(BF16), 'g_kv': g_kv[None, :], 'w_kv_b': w_kv_b.astype(BF16),
        'w_in_b': w_in_b.astype(BF16), 'w_out_b': w_out_b.astype(BF16), 'w_mkv': w_mkv.astype(BF16),
    }


def _dec_l0_proj_kernel(x_ref, gpre_ref, win_ref, cst_ref, convw_ref, convb_ref, wq_ref, wk_ref, wv_ref,
                        wif_ref, bif_ref,
                        q_ref, k_ref, v_ref, gates_ref, xc_ref, opre_ref, zg_ref, qm_ref, zm_ref, cnew_ref):
    h = (_rms_scale(x_ref[...]) * gpre_ref[...]).astype(BF16)
    u = _dot(h, win_ref[:, 0:A_INNER])
    opre_ref[...] = _dot(h, win_ref[:, A_INNER:2 * A_INNER])
    zg_ref[...] = _dot(h, win_ref[:, 2 * A_INNER:3 * A_INNER])
    qm_ref[...] = _dot(h, win_ref[:, 3 * A_INNER:3 * A_INNER + M_WIDTH])
    zm_ref[...] = _dot(h, win_ref[:, 3 * A_INNER + M_WIDTH:3 * A_INNER + 2 * M_WIDTH])
    cw = convw_ref[...]
    xc = convb_ref[...] + cst_ref[0] * cw[0:1, :]
    xc = xc + cst_ref[1] * cw[1:2, :]
    xc = xc + cst_ref[2] * cw[2:3, :]
    xc = xc + u * cw[3:4, :]
    xc = _silu(xc)
    xc_ref[...] = xc
    cnew_ref[0] = cst_ref[1]
    cnew_ref[1] = cst_ref[2]
    cnew_ref[2] = u
    qs, ks, vs, cat = [], [], [], []
    for hd in range(A_HEADS):
        sl = slice(hd * A_HDIM, (hd + 1) * A_HDIM)
        xh = xc[:, sl].astype(BF16)
        qh = _dot(xh, wq_ref[hd])
        kh = _dot(xh, wk_ref[hd]) * (A_HDIM ** -0.5)
        vh = _dot(u[:, sl].astype(BF16), wv_ref[hd])
        qs.append(qh)
        ks.append(kh)
        vs.append(vh)
        cat += [qh.astype(BF16), kh.astype(BF16), vh.astype(BF16)]
    q_ref[...] = jnp.concatenate(qs, axis=-1)
    k_ref[...] = jnp.concatenate(ks, axis=-1)
    v_ref[...] = jnp.concatenate(vs, axis=-1)
    gates_ref[...] = _dot(jnp.concatenate(cat, axis=-1), wif_ref[...]) + bif_ref[...]


def _whole(shape):
    nd = len(shape)
    return pl.BlockSpec(shape, lambda *_: (0,) * nd)


def _dec_l0_proj(x, g_pre, w_in, cst, conv_w, conv_b, wq, wk, wv, wif, bif):
    nb = x.shape[0]
    args = (x, g_pre, w_in, cst, conv_w, conv_b, wq, wk, wv, wif, bif)
    f = lambda *s: jax.ShapeDtypeStruct(s, F32)
    out_shape = [f(nb, A_INNER), f(nb, A_INNER), f(nb, A_INNER), f(nb, LANES), f(nb, A_INNER), f(nb, A_INNER),
                 f(nb, A_INNER), f(nb, M_WIDTH), f(nb, M_WIDTH), f(CONV_W - 1, nb, A_INNER)]
    return pl.pallas_call(
        _dec_l0_proj_kernel,
        grid=(1,),
        in_specs=[_whole(a.shape) for a in args],
        out_specs=[_whole(o.shape) for o in out_shape],
        out_shape=out_shape,
        compiler_params=pltpu.CompilerParams(dimension_semantics=("arbitrary",), vmem_limit_bytes=VMEM_LIMIT),
        name="dec_l0_proj",
    )(*args)


def _row_to_col(row, eye):
    return jnp.sum(jnp.where(eye, row, 0.0), axis=-1, keepdims=True)


def _col_to_row(colv, eye):
    return jnp.sum(jnp.where(eye, colv, 0.0), axis=0, keepdims=True)


def _dec_mem_attention(q, kv_ref_view):
    kk = kv_ref_view[:, 0]
    vv = kv_ref_view[:, 1]
    s = jnp.sum(kk * (q * (M_HDIM ** -0.5))[None], axis=-1, keepdims=True)
    mx = jnp.max(s, axis=0, keepdims=True)
    p = jnp.exp(s - mx)
    return jnp.sum(p * vv, axis=0) / jnp.sum(p, axis=0)


def _dec_mlstm_kernel(q_ref, k_ref, v_ref, gates_ref, m_ref, c_ref, n_ref, qm_ref, kv_ref,
                      hs_ref, c_out, n_out, m_out, ym_ref):
    b = pl.program_id(0)
    rb = pl.ds(b, 1)
    g = gates_ref[rb, :]
    mrow = m_ref[rb, :]
    r = lax.broadcasted_iota(jnp.int32, (A_HDIM, A_HDIM), 0)
    c = lax.broadcasted_iota(jnp.int32, (A_HDIM, A_HDIM), 1)
    eye = r == c
    lane = lax.broadcasted_iota(jnp.int32, (1, LANES), 1)
    heads = range(A_HEADS)
    sl = [slice(h * A_HDIM, (h + 1) * A_HDIM) for h in heads]
    qh = [q_ref[rb, sl[h]] for h in heads]
    kh = [k_ref[rb, sl[h]] for h in heads]
    vh = [v_ref[rb, sl[h]] for h in heads]
    c_old = [c_ref[0, 0, h] for h in heads]
    n_old = [n_ref[0, 0, h:h + 1, :] for h in heads]
    li = [g[:, h:h + 1] for h in heads]
    lf = [_log_sigmoid(g[:, 4 + h:5 + h]) for h in heads]
    m_old = [mrow[:, h:h + 1] for h in heads]
    cq = [jnp.sum(c_old[h] * qh[h], axis=-1, keepdims=True) for h in heads]
    v_col = [_row_to_col(vh[h], eye) for h in heads]
    nq = [jnp.sum(n_old[h] * qh[h], axis=-1, keepdims=True) for h in heads]
    qk = [jnp.sum(qh[h] * kh[h], axis=-1, keepdims=True) for h in heads]
    inter = [lf[h] + m_old[h] for h in heads]
    m_new = [jnp.maximum(inter[h], li[h]) for h in heads]
    ws = [jnp.exp(li[h] - m_new[h]) for h in heads]
    dec = [jnp.exp(inter[h] - m_new[h]) for h in heads]
    sc = [qk[h] * ws[h] for h in heads]
    den = [sc[h] + dec[h] * nq[h] for h in heads]
    h_col = [(sc[h] * v_col[h] + dec[h] * cq[h]) / jnp.maximum(jnp.abs(den[h]), jnp.exp(-m_new[h]))
             for h in heads]
    for h in heads:
        c_out[0, 0, h] = dec[h] * c_old[h] + (ws[h] * v_col[h]) * kh[h]
        n_out[0, 0, h:h + 1, :] = dec[h] * n_old[h] + ws[h] * kh[h]
    for h in heads:
        hs_ref[0, :, sl[h]] = _col_to_row(h_col[h], eye)
    m_acc = jnp.zeros((1, LANES), F32)
    for h in heads:
        m_acc = m_acc + jnp.where(lane == h, m_new[h], 0.0)
    m_out[0] = m_acc
    ym_ref[0] = _dec_mem_attention(qm_ref[0], kv_ref.at[0, 0])


def _dec_mlstm(q, k, v, gates, m_in, state_c, state_n, qm3, cache_mem_kv):
    nb = q.shape[0]
    per_b3 = lambda b: (b, 0, 0)
    in_specs = [_whole(q.shape), _whole(k.shape), _whole(v.shape), _whole(gates.shape), _whole(m_in.shape),
                pl.BlockSpec((1, 1, A_HEADS, A_HDIM, A_HDIM), lambda b: (0, b, 0, 0, 0)),
                pl.BlockSpec((1, 1, A_HEADS, A_HDIM), lambda b: (0, b, 0, 0)),
                pl.BlockSpec((1, M_HEADS, M_HDIM), per_b3),
                pl.BlockSpec((1, 1, N_MEM, 2, M_HEADS, M_HDIM), lambda b: (0, b, 0, 0, 0, 0))]
    out_specs = [pl.BlockSpec((1, 1, A_INNER), per_b3),
                 pl.BlockSpec((1, 1, A_HEADS, A_HDIM, A_HDIM), lambda b: (0, b, 0, 0, 0)),
                 pl.BlockSpec((1, 1, A_HEADS, A_HDIM), lambda b: (0, b, 0, 0)),
                 pl.BlockSpec((1, 1, LANES), per_b3),
                 pl.BlockSpec((1, M_HEADS, M_HDIM), per_b3)]
    out_shape = [jax.ShapeDtypeStruct((nb, 1, A_INNER), F32),
                 jax.ShapeDtypeStruct(state_c.shape, F32),
                 jax.ShapeDtypeStruct(state_n.shape, F32),
                 jax.ShapeDtypeStruct((nb, 1, LANES), F32),
                 jax.ShapeDtypeStruct((nb, M_HEADS, M_HDIM), F32)]
    return pl.pallas_call(
        _dec_mlstm_kernel,
        grid=(nb,),
        in_specs=in_specs,
        out_specs=out_specs,
        out_shape=out_shape,
        compiler_params=pltpu.CompilerParams(dimension_semantics=("arbitrary",), vmem_limit_bytes=VMEM_LIMIT),
        name="dec_mlstm",
    )(q, k, v, gates, m_in, state_c, state_n, qm3, cache_mem_kv)


def _dec_mid_kernel(hs_ref, opre_ref, xc_ref, zg_ref, ym_ref, zm_ref, x_ref, ghn_ref, skip_ref, wout_ref,
                    gpost_ref, gkv_ref, gpre_ref, wkv_ref, win_ref, cos_ref, sin_ref,
                    x1_ref, q_ref, k_ref, v_ref, zg1_ref, qm1_ref, zm1_ref):
    hh = _sigmoid(opre_ref[...]) * hs_ref[...]
    parts = []
    for h in range(A_HEADS):
        v = hh[:, h * A_HDIM:(h + 1) * A_HDIM]
        mu = jnp.mean(v, axis=-1, keepdims=True)
        var = jnp.mean(jnp.square(v - mu), axis=-1, keepdims=True)
        parts.append((v - mu) * lax.rsqrt(var + EPS))
    y = jnp.concatenate(parts, axis=-1) * ghn_ref[...] + skip_ref[...] * xc_ref[...]
    ymix = (y * _silu(zg_ref[...])).astype(BF16)
    ym = (ym_ref[...] * _silu(zm_ref[...])).astype(BF16)
    out = _dot(ymix, wout_ref[0:A_INNER, :]) + _dot(ym, wout_ref[A_INNER:A_INNER + M_WIDTH, :])
    x1 = x_ref[...] + _rms_scale(out) * gpost_ref[...]
    x1_ref[...] = x1
    xn = _rms_scale(x1)
    hk = (xn * gkv_ref[...]).astype(BF16)
    hq = (xn * gpre_ref[...]).astype(BF16)
    cos = cos_ref[...]
    sin = sin_ref[...]
    ks, vs = [], []
    for g in range(N_GROUPS):
        ks.append(_rope_cols(_dot(hk, wkv_ref[:, g * 2 * B_WIDTH:g * 2 * B_WIDTH + B_WIDTH]), cos, sin))
        vs.append(_dot(hk, wkv_ref[:, g * 2 * B_WIDTH + B_WIDTH:(g + 1) * 2 * B_WIDTH]))
    k_ref[...] = jnp.concatenate(ks, axis=-1)
    v_ref[...] = jnp.concatenate(vs, axis=-1)
    qoff = N_GROUPS * B_WIDTH
    q_ref[...] = _rope_cols(_dot(hq, win_ref[:, 0:qoff]), cos, sin)
    zg1_ref[...] = _dot(hq, win_ref[:, qoff:qoff + B_WIDTH])
    qm1_ref[...] = _dot(hq, win_ref[:, qoff + B_WIDTH:qoff + B_WIDTH + M_WIDTH])
    zm1_ref[...] = _dot(hq, win_ref[:, qoff + B_WIDTH + M_WIDTH:qoff + B_WIDTH + 2 * M_WIDTH])


def _dec_mid(hs, opre, xc, zg, ym, zm, x, ghn, skip, w_out, g_post, g_kv, g_pre, wkv, win, cos, sin):
    nb = x.shape[0]
    args = (hs, opre, xc, zg, ym, zm, x, ghn, skip, w_out, g_post, g_kv, g_pre, wkv, win, cos, sin)
    f = lambda *s: jax.ShapeDtypeStruct(s, F32)
    out_shape = [f(nb, D_MODEL), f(nb, N_GROUPS * B_WIDTH), f(nb, N_GROUPS * B_WIDTH), f(nb, N_GROUPS * B_WIDTH),
                 f(nb, B_WIDTH), f(nb, M_WIDTH), f(nb, M_WIDTH)]
    return pl.pallas_call(
        _dec_mid_kernel,
        grid=(1,),
        in_specs=[_whole(a.shape) for a in args],
        out_specs=[_whole(o.shape) for o in out_shape],
        out_shape=out_shape,
        compiler_params=pltpu.CompilerParams(dimension_semantics=("arbitrary",), vmem_limit_bytes=VMEM_LIMIT),
        name="dec_mid",
    )(*args)


def _dec_attn_kernel(q_ref, kn_ref, vn_ref, w0_ref, w1_ref, w2_ref, qm_ref, kv_ref, ydil_ref, ym_ref):
    w_refs = (w0_ref, w1_ref, w2_ref)
    scale = B_HDIM ** -0.5
    groups = range(N_GROUPS)
    q = [q_ref[0, g] * scale for g in groups]
    s_c = [jnp.sum(w_refs[g][0, :, 0] * q[g][None], axis=-1, keepdims=True) for g in groups]
    s_n = [jnp.sum(kn_ref[0, g] * q[g], axis=-1, keepdims=True) for g in groups]
    mxs = [jnp.maximum(jnp.max(s_c[g], axis=0), s_n[g]) for g in groups]
    p_c = [jnp.exp(s_c[g] - mxs[g][None]) for g in groups]
    p_n = [jnp.exp(s_n[g] - mxs[g]) for g in groups]
    ls = [jnp.sum(p_c[g], axis=0) + p_n[g] for g in groups]
    outs = [(jnp.sum(p_c[g] * w_refs[g][0, :, 1], axis=0) + p_n[g] * vn_ref[0, g]) / ls[g] for g in groups]
    lses = [mxs[g] + jnp.log(ls[g]) for g in groups]
    mx = jnp.maximum(jnp.maximum(lses[0], lses[1]), lses[2])
    es = [jnp.exp(l - mx) for l in lses]
    tot = es[0] + es[1] + es[2]
    ydil_ref[0] = (es[0] / tot) * outs[0] + (es[1] / tot) * outs[1] + (es[2] / tot) * outs[2]
    ym_ref[0] = _dec_mem_attention(qm_ref[0], kv_ref.at[0, 0])


def _dec_attn_specs(q4, kn4, vn4, cw0, cw1, cw2, qm3, cache_mem_kv, layer, seq_index):
    nb = q4.shape[0]
    rows = B_GROUPS[0][0]

    def at(*tail):
        return lambda *idx: (seq_index(*idx),) + tail

    win_specs = [pl.BlockSpec((1, rows, 2, B_HEADS, B_HDIM), at(0, 0, 0, 0)),
                 pl.BlockSpec((1, rows, None, 2, B_HEADS, B_HDIM), at(0, 0, 0, 0, 0)),
                 pl.BlockSpec((1, rows, None, 2, B_HEADS, B_HDIM), at(0, 0, 0, 0, 0))]
    in_specs = [pl.BlockSpec((1, N_GROUPS, B_HEADS, B_HDIM), at(0, 0, 0))] * 3 + win_specs + [
        pl.BlockSpec((1, M_HEADS, M_HDIM), at(0, 0)),
        pl.BlockSpec((1, 1, N_MEM, 2, M_HEADS, M_HDIM), lambda *idx: (layer, seq_index(*idx), 0, 0, 0, 0))]
    out_specs = [pl.BlockSpec((1, B_HEADS, B_HDIM), at(0, 0)), pl.BlockSpec((1, M_HEADS, M_HDIM), at(0, 0))]
    out_shapes = [jax.ShapeDtypeStruct((nb, B_HEADS, B_HDIM), F32),
                  jax.ShapeDtypeStruct((nb, M_HEADS, M_HDIM), F32)]
    return [q4, kn4, vn4, cw0, cw1, cw2, qm3, cache_mem_kv], in_specs, out_specs, out_shapes


def _dec_attn(q4, kn4, vn4, cw0, cw1, cw2, qm3, cache_mem_kv, layer):
    args, in_specs, out_specs, out_shapes = _dec_attn_specs(
        q4, kn4, vn4, cw0, cw1, cw2, qm3, cache_mem_kv, layer, seq_index=lambda b: b)
    return pl.pallas_call(
        _dec_attn_kernel,
        grid=(q4.shape[0],),
        in_specs=in_specs,
        out_specs=out_specs,
        out_shape=out_shapes,
        compiler_params=pltpu.CompilerParams(dimension_semantics=("arbitrary",), vmem_limit_bytes=VMEM_LIMIT),
        name="dec_attn",
    )(*args)


def _dec_out_kernel(ydil_ref, zg_ref, ym_ref, zm_ref, x_ref, wout_ref, gpost_ref, y_ref):
    ymix = (ydil_ref[...] * _silu(zg_ref[...])).astype(BF16)
    ym = (ym_ref[...] * _silu(zm_ref[...])).astype(BF16)
    out = _dot(ymix, wout_ref[0:B_WIDTH, :]) + _dot(ym, wout_ref[B_WIDTH:B_WIDTH + M_WIDTH, :])
    y_ref[...] = x_ref[...] + _rms_scale(out) * gpost_ref[...]


def _dec_out(ydil, zg, ym, zm, x1, w_out, g_post):
    args = (ydil, zg, ym, zm, x1, w_out, g_post)
    return pl.pallas_call(
        _dec_out_kernel,
        grid=(1,),
        in_specs=[_whole(a.shape) for a in args],
        out_specs=_whole(x1.shape),
        out_shape=jax.ShapeDtypeStruct(x1.shape, F32),
        compiler_params=pltpu.CompilerParams(dimension_semantics=("arbitrary",), vmem_limit_bytes=VMEM_LIMIT),
        name="dec_out",
    )(*args)


def _sample_front(x_sample, state_conv, state_c, state_n, state_m, cache_wins, cache_mem_kv, p):
    nb = x_sample.shape[0]
    x = x_sample.reshape(nb, D_MODEL)
    cst = state_conv[0].transpose(1, 0, 2)
    q, k, v, gates, xc, opre, zg, qm, zm, cnew = _dec_l0_proj(
        x, p['g_pre'][0:1], p['w_in_a'][0], cst, p['conv_w_a'][0], p['conv_b_a'], p['w_q_a'][0],
        p['w_k_a'][0], p['w_v_a'][0], p['w_if_a'], p['b_if_a'])
    m_in = jnp.pad(state_m[0], ((0, 0), (0, LANES - A_HEADS)))
    hs, c_s, n_s, m_pad, ym0 = _dec_mlstm(q, k, v, gates, m_in, state_c, state_n,
                                          qm.reshape(nb, M_HEADS, M_HDIM), cache_mem_kv)
    pos = PAST_LEN + jnp.arange(1, dtype=F32)
    cos, sin = _rope_tables(pos)
    x1, qd, kn, vn, zg1, qm1, zm1 = _dec_mid(
        hs.reshape(nb, A_INNER), opre, xc, zg, ym0.reshape(nb, M_WIDTH), zm, x, p['g_hn_a'], p['skip_a'],
        p['w_out_a'][0], p['g_post'][0:1], p['g_kv'], p['g_pre'][1:2], p['w_kv_b'], p['w_in_b'][0], cos, sin)
    shp4 = (nb, N_GROUPS, B_HEADS, B_HDIM)
    kn4 = kn.reshape(shp4)
    vn4 = vn.reshape(shp4)
    cws = [cache_wins[0]]
    for g in (1, 2):
        w, d = B_GROUPS[g]
        cws.append(cache_wins[g].reshape(nb, w // d, d, 2, B_HEADS, B_HDIM))
    attn_job = (qd.reshape(shp4), kn4, vn4, cws[0], cws[1], cws[2], qm1.reshape(nb, M_HEADS, M_HDIM),
                cache_mem_kv, 1)
    conv_s = cnew.transpose(1, 0, 2)[None]
    m_s = m_pad[:, 0, 0:A_HEADS][None]
    wins_s = [jnp.stack([kn4[:, g], vn4[:, g]], axis=1)[:, None] for g in range(N_GROUPS)]
    return attn_job, (zg1, zm1, x1), (conv_s, c_s, n_s, m_s, wins_s)


def _sample_back(ydil, ym1, rest, p):
    zg1, zm1, x1 = rest
    nb = x1.shape[0]
    y = _dec_out(ydil.reshape(nb, B_WIDTH), zg1, ym1.reshape(nb, M_WIDTH), zm1, x1, p['w_out_b'][0],
                 p['g_post'][1:2])
    return y.reshape(nb, 1, D_MODEL)


def _sample_group(x_sample, state_conv, state_c, state_n, state_m, cache_wins, cache_mem_kv, p):
    attn_job, rest, (conv_s, c_s, n_s, m_s, wins_s) = _sample_front(
        x_sample, state_conv, state_c, state_n, state_m, cache_wins, cache_mem_kv, p)
    ydil, ym1 = _dec_attn(*attn_job)
    return _sample_back(ydil, ym1, rest, p), conv_s, c_s, n_s, m_s, wins_s


def kernel(x_prompt, x_sample, mem_prompt, state_conv, state_C, state_n, state_m, cache_win0, cache_win1,
           cache_win2, cache_mem_kv, g_pre, g_post, w_in_a, conv_w_a, conv_b_a, w_q_a, w_k_a, w_v_a, w_if_a,
           b_if_a, g_hn_a, skip_a, w_out_a, g_kv, w_kv_b, w_in_b, w_out_b, w_mkv):
    p = _prep_params(g_pre, g_post, w_in_a, conv_w_a, conv_b_a, w_q_a, w_k_a, w_v_a, w_if_a, b_if_a,
                     g_hn_a, skip_a, w_out_a, g_kv, w_kv_b, w_in_b, w_out_b, w_mkv)
    attn_job, rest, (conv_s, c_s, n_s, m_s, wins_s) = _sample_front(
        x_sample, state_conv, state_C, state_n, state_m, (cache_win0, cache_win1, cache_win2), cache_mem_kv, p)
    y_p, conv_p, c_p, n_p, m_p, wins_p, memkv_p, dec_res = _prompt_group(x_prompt, mem_prompt, p, attn_job)
    if dec_res is None:
        dec_res = _dec_attn(*attn_job)
    y_s = _sample_back(dec_res[0], dec_res[1], rest, p)
    return (y_p, y_s, conv_p, c_p, n_p, m_p, wins_p[0], wins_p[1], wins_p[2], memkv_p,
            conv_s, c_s, n_s, m_s, wins_s[0], wins_s[1], wins_s[2])
```

```python
import functools

import jax
import jax.numpy as jnp
from jax import lax
from jax.experimental import pallas as pl
from jax.experimental.pallas import tpu as pltpu

F32 = jnp.float32
BF16 = jnp.bfloat16

D_MODEL = 1024
A_HEADS = 4
A_HDIM = 256
A_INNER = 1024
CONV_W = 4
A_CHUNK = 128
B_GROUPS = ((128, 1), (512, 4), (2048, 16))
N_GROUPS = 3
B_HEADS = 4
B_HDIM = 128
B_WIDTH = 512
N_MEM = 256
M_HEADS = 4
M_HDIM = 128
M_WIDTH = 512
ROPE_THETA = 10000.0
EPS = 1e-6
PAST_LEN = 8192

LANES = 128
TOK_TILE = 512
L0_TILE = 256
ATT_BLK = 128
VMEM_LIMIT = 56 * 1024 * 1024

NT_DIMS = (((1,), (1,)), ((), ()))
LOG2E = 1.4426950408889634


def _dot(a, b):
    return jnp.dot(a, b, preferred_element_type=F32)


def _dot_nt(a, b):
    return lax.dot_general(a, b, NT_DIMS, preferred_element_type=F32)


def _sigmoid(x):
    return 1.0 / (1.0 + jnp.exp(-x))


def _silu(x):
    return x * _sigmoid(x)


def _log_sigmoid(x):
    return jnp.minimum(x, 0.0) - jnp.log(1.0 + jnp.exp(-jnp.abs(x)))


def _rms_scale(x):
    return x * lax.rsqrt(jnp.mean(x * x, axis=-1, keepdims=True) + EPS)


def _const_spec(shape):
    nd = len(shape)
    return pl.BlockSpec(shape, lambda *_: (0,) * nd, pipeline_mode=pl.Buffered(1))


def _mem_attention(qm, mk, mv):
    heads = range(M_HEADS)
    sl = [slice(h * M_HDIM, (h + 1) * M_HDIM) for h in heads]
    s = [_dot_nt(qm[:, sl[h]], mk[:, sl[h]]) * (M_HDIM ** -0.5) for h in heads]
    mx = [jnp.max(s[h], axis=-1, keepdims=True) for h in heads]
    p = [jnp.exp(s[h] - mx[h]) for h in heads]
    l = [jnp.sum(p[h], axis=-1, keepdims=True) for h in heads]
    outs = [_dot((p[h] / l[h]).astype(BF16), mv[:, sl[h]]) for h in heads]
    return jnp.concatenate(outs, axis=-1)


def _rows_to_kv_heads(k, v):
    pieces = [a[:, h * LANES:(h + 1) * LANES] for a in (k, v) for h in range(a.shape[1] // LANES)]
    return jnp.swapaxes(jnp.stack(pieces), 0, 1)


def _memkv_kernel(m_ref, w_ref, o_ref, ob_ref):
    r = _dot(m_ref[...].astype(BF16), w_ref[0])
    o_ref[0] = _rows_to_kv_heads(r[:, 0:M_WIDTH], r[:, M_WIDTH:2 * M_WIDTH])
    ob_ref[0] = r.astype(BF16)


def _memkv(mem2d, w_bf):
    nm = mem2d.shape[0]
    nl = w_bf.shape[0]
    tm = min(512, nm)
    return pl.pallas_call(
        _memkv_kernel,
        grid=(nl, nm // tm),
        in_specs=[pl.BlockSpec((tm, D_MODEL), lambda l, i: (i, 0)),
                  pl.BlockSpec((1, D_MODEL, 2 * M_WIDTH), lambda l, i: (l, 0, 0))],
        out_specs=[pl.BlockSpec((1, tm, 2 * M_HEADS, M_HDIM), lambda l, i: (l, i, 0, 0)),
                   pl.BlockSpec((1, tm, 2 * M_WIDTH), lambda l, i: (l, i, 0))],
        out_shape=[jax.ShapeDtypeStruct((nl, nm, 2 * M_HEADS, M_HDIM), F32),
                   jax.ShapeDtypeStruct((nl, nm, 2 * M_WIDTH), BF16)],
        compiler_params=pltpu.CompilerParams(dimension_semantics=("arbitrary", "arbitrary")),
        name="memkv",
    )(mem2d, w_bf)


def _mlstm_chunk(rs, g, qkv_v, kt_v, c_s, n_s, m_s, causal, tri, hs):
    ls = _log_sigmoid(g)
    t0 = ls.astype(BF16)
    e1 = ls - t0.astype(F32)
    t1 = e1.astype(BF16)
    t2 = (e1 - t1.astype(F32)).astype(BF16)
    bc = _dot(tri, t0) + _dot(tri, t1) + _dot(tri, t2)
    lane = lax.broadcasted_iota(jnp.int32, (A_CHUNK, LANES), 1)
    xt = jnp.where(lane < A_HEADS, g, bc).T
    yield
    heads = range(A_HEADS)
    b_col = [bc[:, 4 + h:5 + h] for h in heads]
    b_row = [xt[4 + h:5 + h, :] for h in heads]
    li_row = [xt[h:h + 1, :] for h in heads]
    li_col = [g[:, h:h + 1] for h in heads]
    m_old = [m_s[h:h + 1, 0:1] for h in heads]
    b_last = [bc[A_CHUNK - 1:A_CHUNK, 4 + h:5 + h] for h in heads]
    qh = [qkv_v[rs, h * 3 * A_HDIM:h * 3 * A_HDIM + A_HDIM] for h in heads]
    kh = [qkv_v[rs, h * 3 * A_HDIM + A_HDIM:h * 3 * A_HDIM + 2 * A_HDIM] for h in heads]
    vh = [qkv_v[rs, h * 3 * A_HDIM + 2 * A_HDIM:(h + 1) * 3 * A_HDIM] for h in heads]
    kt = [kt_v[h] for h in heads]
    c_old = [c_s[h] for h in heads]
    n_old = [n_s[h:h + 1, :] for h in heads]
    qk = [_dot_nt(qh[h], kh[h]) for h in heads]
    qc = [_dot(qh[h], c_old[h].astype(BF16)) for h in heads]
    dm = [jnp.where(causal, b_col[h] - b_row[h] + li_row[h], -jnp.inf) for h in heads]
    inter = [b_col[h] + m_old[h] for h in heads]
    m_row = [jnp.maximum(inter[h], jnp.max(dm[h], axis=-1, keepdims=True)) for h in heads]
    g_max = [jnp.max(b_last[h] - b_row[h] + li_row[h], axis=-1, keepdims=True) for h in heads]
    m_new = [jnp.maximum(b_last[h] + m_old[h], g_max[h]) for h in heads]
    yield
    sc = [qk[h] * jnp.exp(dm[h] - m_row[h]) for h in heads]
    dec = [jnp.exp(inter[h] - m_row[h]) for h in heads]
    ws_col = [jnp.exp(b_last[h] - b_col[h] + li_col[h] - m_new[h]) for h in heads]
    dc = [jnp.exp(b_last[h] + m_old[h] - m_new[h]) for h in heads]
    yield
    sv = [_dot(sc[h].astype(BF16), vh[h]) for h in heads]
    wv = [(ws_col[h] * vh[h].astype(F32)).astype(BF16) for h in heads]
    upd = [_dot(kt[h], wv[h]) for h in heads]
    yield
    for h in heads:
        den = (jnp.sum(sc[h], axis=-1, keepdims=True)
               + dec[h] * jnp.sum(qh[h].astype(F32) * n_old[h], axis=-1, keepdims=True))
        num = sv[h] + dec[h] * qc[h]
        hs.append(num / jnp.maximum(jnp.abs(den), jnp.exp(-m_row[h])))
    yield
    for h in heads:
        c_s[h] = dc[h] * c_old[h] + upd[h]
        n_s[h:h + 1, :] = dc[h] * n_old[h] + jnp.sum(ws_col[h] * kh[h].astype(F32), axis=0, keepdims=True)
        m_s[h:h + 1, :] = jnp.broadcast_to(m_new[h], (1, LANES))
    yield


def _l0p_kernel(nt, n_dec_seq, *refs):
    n_dec_in, n_dec_out = (9, 5) if n_dec_seq else (0, 0)
    (x_ref, xp_ref, gpre_ref, win_ref, convw_ref, convb_ref, wq_ref, wk_ref, wkt_ref, wv_ref,
     wif_ref, bif_ref, ghn_ref, skip_ref, mkv_ref, wout_ref, gpost_ref) = refs[0:17]
    dec_in = refs[17:17 + n_dec_in]
    x1_ref, conv_out, c_out, n_out, m_out = refs[17 + n_dec_in:22 + n_dec_in]
    dec_out = refs[22 + n_dec_in:22 + n_dec_in + n_dec_out]
    (h_s, u_s, ymix_s, xc_s, opre_s, zg_s, qm_s, zm_s, qkv_s, kt_s, gates_s,
     c_s, n_s, m_s) = refs[22 + n_dec_in + n_dec_out:]
    tt = x_ref.shape[1]
    nsub = tt // A_CHUNK
    t = pl.program_id(0)
    parity = lax.rem(t + 1, 2)
    pos1 = lax.rem(t + nt - 1, nt)
    pos2 = lax.rem(t + 2 * nt - 2, nt)

    @pl.when(t == 0)
    def _():
        h_s[...] = jnp.zeros(h_s.shape, BF16)
        u_s[...] = jnp.zeros(u_s.shape, F32)
        xc_s[0] = jnp.zeros(xc_s.shape[1:], F32)
        opre_s[0] = jnp.zeros(opre_s.shape[1:], F32)
        zg_s[0] = jnp.zeros(zg_s.shape[1:], F32)
        qm_s[0] = jnp.zeros(qm_s.shape[1:], BF16)
        zm_s[0] = jnp.zeros(zm_s.shape[1:], F32)
        qkv_s[0] = jnp.zeros(qkv_s.shape[1:], BF16)
        kt_s[0] = jnp.zeros(kt_s.shape[1:], BF16)
        gates_s[0] = jnp.zeros(gates_s.shape[1:], F32)

    @pl.when(pos1 == 0)
    def _():
        u_s[0:8, :] = jnp.zeros((8, A_INNER), F32)

    @pl.when(pos2 == 0)
    def _():
        c_s[...] = jnp.zeros(c_s.shape, F32)
        n_s[...] = jnp.zeros(n_s.shape, F32)
        m_s[...] = jnp.zeros(m_s.shape, F32)

    row = lax.broadcasted_iota(jnp.int32, (A_CHUNK, A_CHUNK), 0)
    col = lax.broadcasted_iota(jnp.int32, (A_CHUNK, A_CHUNK), 1)
    causal = col <= row
    tri = jnp.where(causal, 1.0, 0.0).astype(BF16)
    ghn = ghn_ref[...]
    skp = skip_ref[...]
    mk = mkv_ref[0, :, 0:M_WIDTH]
    mv = mkv_ref[0, :, M_WIDTH:2 * M_WIDTH]

    def stage2(pslot):
        ym = _mem_attention(qm_s[pslot], mk, mv) * _silu(zm_s[pslot])
        ymix_s[:, A_INNER:A_INNER + M_WIDTH] = ym.astype(BF16)
        yield
        for c in range(nsub):
            rs = slice(c * A_CHUNK, (c + 1) * A_CHUNK)
            hs = []
            yield from _mlstm_chunk(rs, gates_s[pslot, rs, :], qkv_s.at[pslot], kt_s.at[pslot, :, c],
                                    c_s, n_s, m_s, causal, tri, hs)
            parts = []
            for h in range(A_HEADS):
                v = _sigmoid(opre_s[pslot, rs, h * A_HDIM:(h + 1) * A_HDIM]) * hs[h]
                mu = jnp.mean(v, axis=-1, keepdims=True)
                var = jnp.mean(jnp.square(v - mu), axis=-1, keepdims=True)
                parts.append((v - mu) * lax.rsqrt(var + EPS))
            hn = jnp.concatenate(parts, axis=-1) * ghn
            y = hn + skp * xc_s[pslot, rs, :]
            ymix_s[rs, 0:A_INNER] = (y * _silu(zg_s[pslot, rs, :])).astype(BF16)
            yield

    def stage1(slot):
        hb = h_s[...]
        u_s[8:8 + tt, :] = _dot(hb, win_ref[:, 0:A_INNER])
        yield
        cw = convw_ref[...]
        cb = convb_ref[...]
        for c in range(nsub):
            r0 = c * A_CHUNK
            blk = u_s[r0:r0 + A_CHUNK + 8, :]
            xc = cb + pltpu.roll(blk, 3, 0)[8:, :] * cw[0:1, :]
            xc = xc + pltpu.roll(blk, 2, 0)[8:, :] * cw[1:2, :]
            xc = xc + pltpu.roll(blk, 1, 0)[8:, :] * cw[2:3, :]
            xc = xc + blk[8:, :] * cw[3:4, :]
            xc_s[slot, r0:r0 + A_CHUNK, :] = _silu(xc)
        opre_s[slot] = _dot(hb, win_ref[:, A_INNER:2 * A_INNER])
        yield
        zg_s[slot] = _dot(hb, win_ref[:, 2 * A_INNER:3 * A_INNER])
        yield
        qm_s[slot] = _dot(hb, win_ref[:, 3 * A_INNER:3 * A_INNER + M_WIDTH]).astype(BF16)
        zm_s[slot] = _dot(hb, win_ref[:, 3 * A_INNER + M_WIDTH:3 * A_INNER + 2 * M_WIDTH])
        yield
        for h in range(A_HEADS):
            sl = slice(h * A_HDIM, (h + 1) * A_HDIM)
            xh = xc_s[slot, :, sl].astype(BF16)
            uh = u_s[8:8 + tt, sl].astype(BF16)
            base = h * 3 * A_HDIM
            qkv_s[slot, :, base:base + A_HDIM] = _dot(xh, wq_ref[h]).astype(BF16)
            qkv_s[slot, :, base + A_HDIM:base + 2 * A_HDIM] = (
                _dot(xh, wk_ref[h]) * (A_HDIM ** -0.5)).astype(BF16)
            qkv_s[slot, :, base + 2 * A_HDIM:base + 3 * A_HDIM] = _dot(uh, wv_ref[h]).astype(BF16)
            kt = (_dot_nt(wkt_ref[h], xh) * (A_HDIM ** -0.5)).astype(BF16)
            for c in range(nsub):
                kt_s[slot, h, c] = kt[:, c * A_CHUNK:(c + 1) * A_CHUNK]
            yield
        gates_s[slot] = _dot(qkv_s[slot], wif_ref[...]) + bif_ref[...]
        yield

    def step(slot):
        if n_dec_seq:
            first = slot == 1
            heads = (0, 1) if first else (2, 3)
            _dec_mlstm_body(jnp.minimum(t // 2, n_dec_seq - 1), heads, first, *dec_in, *dec_out)
        pending = [stage1(slot), stage2(1 - slot)]
        while pending:
            for gen in list(pending):
                try:
                    next(gen)
                except StopIteration:
                    pending.remove(gen)
        h_next = (_rms_scale(x_ref[0]) * gpre_ref[...]).astype(BF16)
        out = _dot(ymix_s[...], wout_ref[...])
        h_s[...] = h_next
        x1_ref[0] = xp_ref[0] + _rms_scale(out) * gpost_ref[...]

    for s in range(2):
        pl.when(parity == s)(functools.partial(step, s))

    @pl.when(jnp.logical_and(pos1 == nt - 1, t > 0))
    def _():
        conv_out[0, 0] = u_s[tt + 5:tt + 8, :]

    u_s[0:8, :] = u_s[tt:tt + 8, :]

    @pl.when(jnp.logical_and(pos2 == nt - 1, t > 1))
    def _():
        for h in range(A_HEADS):
            c_out[0, 0, h] = c_s[h].T
        n_out[0, 0] = n_s[0:A_HEADS, :]
        m_out[0] = m_s[...]


def _layer0_prompt_pipelined(x, g_pre, w_in, conv_w, conv_b, wq, wk, wkt, wv, wif, bif, ghn, skip, mkv_bf,
                             w_out, g_post, dec_job=None):
    b, s, _ = x.shape
    tt = min(L0_TILE, s)
    nt = s // tt
    ntiles = b * nt
    a_in = w_in.shape[1]
    nsub = tt // A_CHUNK

    def cur(t):
        t1 = jnp.minimum(t, ntiles - 1)
        return (t1 // nt, t1 % nt, 0)

    def prev(t):
        t2 = jnp.maximum(t - 2, 0)
        return (t2 // nt, t2 % nt, 0)

    def prev_b(t):
        return jnp.maximum(t - 2, 0) // nt

    in_specs = [
        pl.BlockSpec((1, tt, D_MODEL), cur),
        pl.BlockSpec((1, tt, D_MODEL), prev),
        _const_spec((1, D_MODEL)),
        _const_spec((D_MODEL, a_in)),
        _const_spec((CONV_W, A_INNER)),
        _const_spec((1, A_INNER)),
        _const_spec((A_HEADS, A_HDIM, A_HDIM)),
        _const_spec((A_HEADS, A_HDIM, A_HDIM)),
        _const_spec((A_HEADS, A_HDIM, A_HDIM)),
        _const_spec((A_HEADS, A_HDIM, A_HDIM)),
        _const_spec((3 * A_INNER, LANES)),
        _const_spec((1, LANES)),
        _const_spec((1, A_INNER)),
        _const_spec((1, A_INNER)),
        pl.BlockSpec((1, N_MEM, 2 * M_WIDTH), lambda t: (prev_b(t), 0, 0)),
        _const_spec((A_INNER + M_WIDTH, D_MODEL)),
        _const_spec((1, D_MODEL)),
    ]
    out_specs = [
        pl.BlockSpec((1, tt, D_MODEL), prev),
        pl.BlockSpec((1, 1, CONV_W - 1, A_INNER), lambda t: (0, prev_b(t), 0, 0)),
        pl.BlockSpec((1, 1, A_HEADS, A_HDIM, A_HDIM), lambda t: (0, prev_b(t), 0, 0, 0)),
        pl.BlockSpec((1, 1, A_HEADS, A_HDIM), lambda t: (0, prev_b(t), 0, 0)),
        pl.BlockSpec((1, 8, LANES), lambda t: (prev_b(t), 0, 0)),
    ]
    out_shape = [
        jax.ShapeDtypeStruct((b, s, D_MODEL), F32),
        jax.ShapeDtypeStruct((1, b, CONV_W - 1, A_INNER), F32),
        jax.ShapeDtypeStruct((1, b, A_HEADS, A_HDIM, A_HDIM), F32),
        jax.ShapeDtypeStruct((1, b, A_HEADS, A_HDIM), F32),
        jax.ShapeDtypeStruct((b, 8, LANES), F32),
    ]
    scratch = [
        pltpu.VMEM((tt, D_MODEL), BF16),
        pltpu.VMEM((tt + 8, A_INNER), F32),
        pltpu.VMEM((tt, A_INNER + M_WIDTH), BF16),
        pltpu.VMEM((2, tt, A_INNER), F32),
        pltpu.VMEM((2, tt, A_INNER), F32),
        pltpu.VMEM((2, tt, A_INNER), F32),
        pltpu.VMEM((2, tt, M_WIDTH), BF16),
        pltpu.VMEM((2, tt, M_WIDTH), F32),
        pltpu.VMEM((2, tt, 3 * A_INNER), BF16),
        pltpu.VMEM((2, A_HEADS, nsub, A_HDIM, A_CHUNK), BF16),
        pltpu.VMEM((2, tt, LANES), F32),
        pltpu.VMEM((A_HEADS, A_HDIM, A_HDIM), F32),
        pltpu.VMEM((8, A_HDIM), F32),
        pltpu.VMEM((8, LANES), F32),
    ]
    dec_args, dec_specs, dec_out_specs, dec_out_shapes, n_dec_seq = [], [], [], [], 0
    if dec_job is not None:
        n_dec_seq = dec_job[0].shape[0]
        dec_args, dec_specs, dec_out_specs, dec_out_shapes = _dec_mlstm_specs(
            *dec_job, seq_index=lambda t: jnp.minimum(t // 2, n_dec_seq - 1))
    return pl.pallas_call(
        functools.partial(_l0p_kernel, nt, n_dec_seq),
        grid=(ntiles + 2,),
        in_specs=in_specs + dec_specs,
        out_specs=out_specs + dec_out_specs,
        out_shape=out_shape + dec_out_shapes,
        scratch_shapes=scratch,
        compiler_params=pltpu.CompilerParams(
            dimension_semantics=("arbitrary",), vmem_limit_bytes=VMEM_LIMIT),
        name="layer0_prompt",
    )(x, x, g_pre, w_in, conv_w, conv_b, wq, wk, wkt, wv, wif, bif, ghn, skip, mkv_bf, w_out, g_post,
      *dec_args)


def _l0_kernel(x_ref, gpre_ref, win_ref, convw_ref, convb_ref, wq_ref, wk_ref, wkt_ref, wv_ref,
               wif_ref, bif_ref, ghn_ref, skip_ref, mkv_ref, wout_ref, gpost_ref,
               x1_ref, conv_out, c_out, n_out, m_out,
               h_s, u_s, xc_s, opre_s, zg_s, qm_s, zm_s, qkv_s, kt_s, gates_s, ymix_s,
               c_s, n_s, m_s):
    tt = x_ref.shape[1]
    nsub = tt // A_CHUNK
    i = pl.program_id(1)
    nt = pl.num_programs(1)

    @pl.when(i == 0)
    def _():
        u_s[0:8, :] = jnp.zeros((8, A_INNER), F32)
        c_s[...] = jnp.zeros(c_s.shape, F32)
        n_s[...] = jnp.zeros(n_s.shape, F32)
        m_s[...] = jnp.zeros(m_s.shape, F32)

    gpre = gpre_ref[...]

    def norm_body(c, _):
        r = pl.ds(pl.multiple_of(c * A_CHUNK, A_CHUNK), A_CHUNK)
        h_s[r, :] = (_rms_scale(x_ref[0, r, :]) * gpre).astype(BF16)
        return 0
    lax.fori_loop(0, nsub, norm_body, 0)

    hb = h_s[...]
    u_s[8:8 + tt, :] = _dot(hb, win_ref[:, 0:A_INNER])
    opre_s[...] = _dot(hb, win_ref[:, A_INNER:2 * A_INNER])
    zg_s[...] = _dot(hb, win_ref[:, 2 * A_INNER:3 * A_INNER])
    qm_s[...] = _dot(hb, win_ref[:, 3 * A_INNER:3 * A_INNER + M_WIDTH]).astype(BF16)
    zm_s[...] = _dot(hb, win_ref[:, 3 * A_INNER + M_WIDTH:3 * A_INNER + 2 * M_WIDTH])

    cw = convw_ref[...]
    cb = convb_ref[...]

    for c in range(nsub):
        r0 = c * A_CHUNK
        xc = cb + u_s[r0 + 5:r0 + 5 + A_CHUNK, :] * cw[0:1, :]
        xc = xc + u_s[r0 + 6:r0 + 6 + A_CHUNK, :] * cw[1:2, :]
        xc = xc + u_s[r0 + 7:r0 + 7 + A_CHUNK, :] * cw[2:3, :]
        xc = xc + u_s[r0 + 8:r0 + 8 + A_CHUNK, :] * cw[3:4, :]
        xc_s[r0:r0 + A_CHUNK, :] = _silu(xc)

    for h in range(A_HEADS):
        sl = slice(h * A_HDIM, (h + 1) * A_HDIM)
        xh = xc_s[:, sl].astype(BF16)
        uh = u_s[8:8 + tt, sl].astype(BF16)
        base = h * 3 * A_HDIM
        qkv_s[:, base:base + A_HDIM] = _dot(xh, wq_ref[h]).astype(BF16)
        qkv_s[:, base + A_HDIM:base + 2 * A_HDIM] = (_dot(xh, wk_ref[h]) * (A_HDIM ** -0.5)).astype(BF16)
        qkv_s[:, base + 2 * A_HDIM:base + 3 * A_HDIM] = _dot(uh, wv_ref[h]).astype(BF16)
        kt = (_dot_nt(wkt_ref[h], xh) * (A_HDIM ** -0.5)).astype(BF16)
        for c in range(nsub):
            kt_s[h, c] = kt[:, c * A_CHUNK:(c + 1) * A_CHUNK]
    gates_s[...] = _dot(qkv_s[...], wif_ref[...]) + bif_ref[...]

    row = lax.broadcasted_iota(jnp.int32, (A_CHUNK, A_CHUNK), 0)
    col = lax.broadcasted_iota(jnp.int32, (A_CHUNK, A_CHUNK), 1)
    causal = col <= row
    tri = jnp.where(causal, 1.0, 0.0).astype(BF16)

    def chunk_body(c, _):
        r0 = pl.multiple_of(c * A_CHUNK, A_CHUNK)
        rs = pl.ds(r0, A_CHUNK)
        g = gates_s[rs, :]
        ls = _log_sigmoid(g)
        t0 = ls.astype(BF16)
        e1 = ls - t0.astype(F32)
        t1 = e1.astype(BF16)
        t2 = (e1 - t1.astype(F32)).astype(BF16)
        bc = _dot(tri, t0) + _dot(tri, t1) + _dot(tri, t2)
        lane = lax.broadcasted_iota(jnp.int32, (A_CHUNK, LANES), 1)
        xt = jnp.where(lane < A_HEADS, g, bc).T
        heads = range(A_HEADS)
        b_col = [bc[:, 4 + h:5 + h] for h in heads]
        b_row = [xt[4 + h:5 + h, :] for h in heads]
        li_row = [xt[h:h + 1, :] for h in heads]
        li_col = [g[:, h:h + 1] for h in heads]
        m_old = [m_s[h:h + 1, 0:1] for h in heads]
        b_last = [bc[A_CHUNK - 1:A_CHUNK, 4 + h:5 + h] for h in heads]
        qh = [qkv_s[rs, h * 3 * A_HDIM:h * 3 * A_HDIM + A_HDIM] for h in heads]
        kh = [qkv_s[rs, h * 3 * A_HDIM + A_HDIM:h * 3 * A_HDIM + 2 * A_HDIM] for h in heads]
        vh = [qkv_s[rs, h * 3 * A_HDIM + 2 * A_HDIM:(h + 1) * 3 * A_HDIM] for h in heads]
        kt = [kt_s[h, c] for h in heads]
        c_old = [c_s[h] for h in heads]
        n_old = [n_s[h:h + 1, :] for h in heads]
        qk = [_dot_nt(qh[h], kh[h]) for h in heads]
        qc = [_dot(qh[h], c_old[h].astype(BF16)) for h in heads]
        dm = [jnp.where(causal, b_col[h] - b_row[h] + li_row[h], -jnp.inf) for h in heads]
        inter = [b_col[h] + m_old[h] for h in heads]
        m_row = [jnp.maximum(inter[h], jnp.max(dm[h], axis=-1, keepdims=True)) for h in heads]
        g_max = [jnp.max(b_last[h] - b_row[h] + li_row[h], axis=-1, keepdims=True) for h in heads]
        m_new = [jnp.maximum(b_last[h] + m_old[h], g_max[h]) for h in heads]
        sc = [qk[h] * jnp.exp(dm[h] - m_row[h]) for h in heads]
        dec = [jnp.exp(inter[h] - m_row[h]) for h in heads]
        ws_col = [jnp.exp(b_last[h] - b_col[h] + li_col[h] - m_new[h]) for h in heads]
        dc = [jnp.exp(b_last[h] + m_old[h] - m_new[h]) for h in heads]
        sv = [_dot(sc[h].astype(BF16), vh[h]) for h in heads]
        wv = [(ws_col[h] * vh[h].astype(F32)).astype(BF16) for h in heads]
        upd = [_dot(kt[h], wv[h]) for h in heads]
        hs = []
        for h in heads:
            den = (jnp.sum(sc[h], axis=-1, keepdims=True)
                   + dec[h] * jnp.sum(qh[h].astype(F32) * n_old[h], axis=-1, keepdims=True))
            num = sv[h] + dec[h] * qc[h]
            hs.append(num / jnp.maximum(jnp.abs(den), jnp.exp(-m_row[h])))
        for h in heads:
            c_s[h] = dc[h] * c_old[h] + upd[h]
            n_s[h:h + 1, :] = dc[h] * n_old[h] + jnp.sum(ws_col[h] * kh[h].astype(F32), axis=0, keepdims=True)
            m_s[h:h + 1, :] = jnp.broadcast_to(m_new[h], (1, LANES))

        parts = []
        for h in heads:
            v = _sigmoid(opre_s[rs, h * A_HDIM:(h + 1) * A_HDIM]) * hs[h]
            mu = jnp.mean(v, axis=-1, keepdims=True)
            var = jnp.mean(jnp.square(v - mu), axis=-1, keepdims=True)
            parts.append((v - mu) * lax.rsqrt(var + EPS))
        hn = jnp.concatenate(parts, axis=-1) * ghn
        y = hn + skp * xc_s[rs, :]
        ymix_s[rs, 0:A_INNER] = (y * _silu(zg_s[rs, :])).astype(BF16)
        ym = _mem_attention(qm_s[rs, :], mk, mv) * _silu(zm_s[rs, :])
        ymix_s[rs, A_INNER:A_INNER + M_WIDTH] = ym.astype(BF16)
        return 0

    ghn = ghn_ref[...]
    skp = skip_ref[...]
    mk = mkv_ref[0, :, 0:M_WIDTH]
    mv = mkv_ref[0, :, M_WIDTH:2 * M_WIDTH]
    lax.fori_loop(0, nsub, chunk_body, 0)

    out = _dot(ymix_s[...], wout_ref[...])
    x1_ref[0] = x_ref[0] + _rms_scale(out) * gpost_ref[...]

    u_s[0:8, :] = u_s[tt:tt + 8, :]

    @pl.when(i == nt - 1)
    def _():
        conv_out[0, 0] = u_s[tt + 5:tt + 8, :]
        for h in range(A_HEADS):
            c_out[0, 0, h] = c_s[h].T
        n_out[0, 0] = n_s[0:A_HEADS, :]
        m_out[0] = m_s[...]


def _layer0_prompt(x, g_pre, w_in, conv_w, conv_b, wq, wk, wkt, wv, wif, bif, ghn, skip, mkv_bf, w_out,
                   g_post):
    b, s, _ = x.shape
    tt = min(TOK_TILE, s)
    nt = s // tt
    a_in = w_in.shape[1]
    tile = lambda bb, i: (bb, i, 0)
    per_b = lambda bb, i: (bb, 0, 0)
    in_specs = [
        pl.BlockSpec((1, tt, D_MODEL), tile),
        _const_spec((1, D_MODEL)),
        _const_spec((D_MODEL, a_in)),
        _const_spec((CONV_W, A_INNER)),
        _const_spec((1, A_INNER)),
        _const_spec((A_HEADS, A_HDIM, A_HDIM)),
        _const_spec((A_HEADS, A_HDIM, A_HDIM)),
        _const_spec((A_HEADS, A_HDIM, A_HDIM)),
        _const_spec((A_HEADS, A_HDIM, A_HDIM)),
        _const_spec((3 * A_INNER, LANES)),
        _const_spec((1, LANES)),
        _const_spec((1, A_INNER)),
        _const_spec((1, A_INNER)),
        pl.BlockSpec((1, N_MEM, 2 * M_WIDTH), per_b),
        _const_spec((A_INNER + M_WIDTH, D_MODEL)),
        _const_spec((1, D_MODEL)),
    ]
    out_specs = [
        pl.BlockSpec((1, tt, D_MODEL), tile),
        pl.BlockSpec((1, 1, CONV_W - 1, A_INNER), lambda bb, i: (0, bb, 0, 0)),
        pl.BlockSpec((1, 1, A_HEADS, A_HDIM, A_HDIM), lambda bb, i: (0, bb, 0, 0, 0)),
        pl.BlockSpec((1, 1, A_HEADS, A_HDIM), lambda bb, i: (0, bb, 0, 0)),
        pl.BlockSpec((1, 8, LANES), per_b),
    ]
    out_shape = [
        jax.ShapeDtypeStruct((b, s, D_MODEL), F32),
        jax.ShapeDtypeStruct((1, b, CONV_W - 1, A_INNER), F32),
        jax.ShapeDtypeStruct((1, b, A_HEADS, A_HDIM, A_HDIM), F32),
        jax.ShapeDtypeStruct((1, b, A_HEADS, A_HDIM), F32),
        jax.ShapeDtypeStruct((b, 8, LANES), F32),
    ]
    scratch = [
        pltpu.VMEM((tt, D_MODEL), BF16),
        pltpu.VMEM((tt + 8, A_INNER), F32),
        pltpu.VMEM((tt, A_INNER), F32),
        pltpu.VMEM((tt, A_INNER), F32),
        pltpu.VMEM((tt, A_INNER), F32),
        pltpu.VMEM((tt, M_WIDTH), BF16),
        pltpu.VMEM((tt, M_WIDTH), F32),
        pltpu.VMEM((tt, 3 * A_INNER), BF16),
        pltpu.VMEM((A_HEADS, tt // A_CHUNK, A_HDIM, A_CHUNK), BF16),
        pltpu.VMEM((tt, LANES), F32),
        pltpu.VMEM((tt, A_INNER + M_WIDTH), BF16),
        pltpu.VMEM((A_HEADS, A_HDIM, A_HDIM), F32),
        pltpu.VMEM((8, A_HDIM), F32),
        pltpu.VMEM((8, LANES), F32),
    ]
    return pl.pallas_call(
        _l0_kernel,
        grid=(b, nt),
        in_specs=in_specs,
        out_specs=out_specs,
        out_shape=out_shape,
        scratch_shapes=scratch,
        compiler_params=pltpu.CompilerParams(
            dimension_semantics=("arbitrary", "arbitrary"), vmem_limit_bytes=VMEM_LIMIT),
        name="layer0_prompt",
    )(x, g_pre, w_in, conv_w, conv_b, wq, wk, wkt, wv, wif, bif, ghn, skip, mkv_bf, w_out, g_post)


def _rope_cols(x, cos, sin_signed):
    outs = []
    for cblk in range(x.shape[1] // B_HDIM):
        xb = x[:, cblk * B_HDIM:(cblk + 1) * B_HDIM]
        outs.append(xb * cos + pltpu.roll(xb, B_HDIM // 2, 1) * sin_signed)
    return jnp.concatenate(outs, axis=-1)


def _l1a_kernel(n_dec_in, *refs):
    x_ref, gkv_ref, gpre_ref, wkv_ref, win_ref, cos_ref, sin_ref = refs[0:7]
    dec_in = refs[7:7 + n_dec_in]
    (q0_ref, q1_ref, q2_ref, k0_ref, k1_ref, k2_ref, v0_ref, v1_ref, v2_ref,
     zg_ref, qm_ref, zm_ref, w0_ref, w1_ref, w2_ref) = refs[7 + n_dec_in:22 + n_dec_in]
    dec_out = refs[22 + n_dec_in:]
    tt = x_ref.shape[1]
    xn = _rms_scale(x_ref[0])
    hk = (xn * gkv_ref[...]).astype(BF16)
    hq = (xn * gpre_ref[...]).astype(BF16)
    cos = cos_ref[...]
    sin = sin_ref[...]
    q_refs = (q0_ref, q1_ref, q2_ref)
    k_refs = (k0_ref, k1_ref, k2_ref)
    v_refs = (v0_ref, v1_ref, v2_ref)
    w_refs = (w0_ref, w1_ref, w2_ref)
    for g in (2, 1, 0):
        d = B_GROUPS[g][1]
        kf = _rope_cols(_dot(hk, wkv_ref[:, g * 2 * B_WIDTH:g * 2 * B_WIDTH + B_WIDTH]), cos, sin)
        vf = _dot(hk, wkv_ref[:, g * 2 * B_WIDTH + B_WIDTH:(g + 1) * 2 * B_WIDTH])
        qf = _rope_cols(_dot(hq, win_ref[:, g * B_WIDTH:(g + 1) * B_WIDTH]), cos, sin)
        wr = w_refs[g]
        wrows = wr.shape[1]
        wr[0] = _rows_to_kv_heads(kf[tt - wrows:, :], vf[tt - wrows:, :])
        for val, ref in ((qf.astype(BF16), q_refs[g]), (kf.astype(BF16), k_refs[g]), (vf.astype(BF16), v_refs[g])):
            if d == 1:
                ref[0, 0] = val
            else:
                ref[0] = jnp.swapaxes(val.reshape(tt // d, d, val.shape[1]), 0, 1)
        if g == 2 and n_dec_in:
            _dec_attn_kernel(*dec_in, *dec_out)
    qoff = N_GROUPS * B_WIDTH
    zg_ref[0] = _dot(hq, win_ref[:, qoff:qoff + B_WIDTH]).astype(BF16)
    qm_ref[0] = _dot(hq, win_ref[:, qoff + B_WIDTH:qoff + B_WIDTH + M_WIDTH]).astype(BF16)
    zm_ref[0] = _dot(hq, win_ref[:, qoff + B_WIDTH + M_WIDTH:qoff + B_WIDTH + 2 * M_WIDTH]).astype(BF16)


def _layer1_proj_prompt(x1, g_kv, g_pre, wkv, win, cos_t, sin_t, dec_job=None):
    b, s, _ = x1.shape
    tt = min(TOK_TILE, s)
    nt = s // tt
    tile = lambda bb, i: (bb, i, 0)
    in_specs = [
        pl.BlockSpec((1, tt, D_MODEL), tile),
        _const_spec((1, D_MODEL)),
        _const_spec((1, D_MODEL)),
        _const_spec(wkv.shape),
        _const_spec(win.shape),
        pl.BlockSpec((tt, B_HDIM), lambda bb, i: (i, 0)),
        pl.BlockSpec((tt, B_HDIM), lambda bb, i: (i, 0)),
    ]
    qkv_specs, qkv_shapes = [], []
    for _ in range(3):
        for (_, d) in B_GROUPS:
            qkv_specs.append(pl.BlockSpec((1, d, tt // d, B_WIDTH), lambda bb, i: (bb, 0, i, 0)))
            qkv_shapes.append(jax.ShapeDtypeStruct((b, d, s // d, B_WIDTH), BF16))
    gate_specs = [pl.BlockSpec((1, tt, B_WIDTH), tile)] * 3
    gate_shapes = [jax.ShapeDtypeStruct((b, s, B_WIDTH), BF16)] * 3
    win_specs, win_shapes = [], []
    for (w, _) in B_GROUPS:
        wr = min(w, s)
        rows = min(wr, tt)
        nblk = wr // rows
        win_specs.append(pl.BlockSpec(
            (1, rows, 2 * B_HEADS, B_HDIM),
            functools.partial(lambda bb, i, nb: (bb, jnp.maximum(i - (nt - nb), 0), 0, 0), nb=nblk)))
        win_shapes.append(jax.ShapeDtypeStruct((b, wr, 2 * B_HEADS, B_HDIM), F32))
    dec_args, dec_specs, dec_out_specs, dec_out_shapes = [], [], [], []
    if dec_job is not None:
        dec_args, dec_specs, dec_out_specs, dec_out_shapes = _dec_attn_specs(
            *dec_job, seq_index=lambda bb, i: bb * nt + i)
    return pl.pallas_call(
        functools.partial(_l1a_kernel, len(dec_args)),
        grid=(b, nt),
        in_specs=in_specs + dec_specs,
        out_specs=qkv_specs + gate_specs + win_specs + dec_out_specs,
        out_shape=qkv_shapes + gate_shapes + win_shapes + dec_out_shapes,
        compiler_params=pltpu.CompilerParams(
            dimension_semantics=("arbitrary", "arbitrary"), vmem_limit_bytes=VMEM_LIMIT),
        name="layer1_proj_prompt",
    )(x1, g_kv, g_pre, wkv, win, cos_t, sin_t, *dec_args)


def _cols_to_lanes(cols):
    t = cols[0].shape[0]
    lane = lax.broadcasted_iota(jnp.int32, (t, LANES), 1)
    acc = jnp.zeros((t, LANES), F32)
    for h, cvec in enumerate(cols):
        acc = jnp.where(lane == h, cvec, acc)
    return acc


def _band_attn_kernel(q_ref, kc_ref, kp_ref, vc_ref, vp_ref, o_ref, lse_ref):
    nres, tq = q_ref.shape[1:3]
    nsb = tq // ATT_BLK
    j = pl.program_id(2)
    row = lax.broadcasted_iota(jnp.int32, (ATT_BLK, 2 * ATT_BLK), 0)
    col = lax.broadcasted_iota(jnp.int32, (ATT_BLK, 2 * ATT_BLK), 1)
    band = jnp.logical_and(col >= row, col <= row + ATT_BLK)
    first_pen = jnp.where(col < ATT_BLK, jnp.where(j > 0, 0.0, -jnp.inf), 0.0)
    scale = B_HDIM ** -0.5
    qs, ks, vs, pens = [], [], [], []
    for r in range(nres):
        for sb in range(nsb):
            rs = slice(sb * ATT_BLK, (sb + 1) * ATT_BLK)
            ps = slice((sb - 1) * ATT_BLK, sb * ATT_BLK)
            for h in range(B_HEADS):
                hs = slice(h * B_HDIM, (h + 1) * B_HDIM)
                qs.append(q_ref[0, r, rs, hs])
                kp = kp_ref[0, r, :, hs] if sb == 0 else kc_ref[0, r, ps, hs]
                vp = vp_ref[0, r, :, hs] if sb == 0 else vc_ref[0, r, ps, hs]
                ks.append(jnp.concatenate([kp, kc_ref[0, r, rs, hs]], axis=0))
                vs.append(jnp.concatenate([vp, vc_ref[0, r, rs, hs]], axis=0))
                pens.append(sb == 0)
    q3 = jnp.stack(qs)
    k3 = jnp.stack(ks)
    v3 = jnp.stack(vs)
    s = jnp.einsum('uqd,ukd->uqk', q3, k3, preferred_element_type=F32)
    s = jnp.stack([s[u] + first_pen if pens[u] else s[u] for u in range(len(pens))])
    s = jnp.where(band[None], s, -jnp.inf)
    mx = jnp.max(s, axis=-1, keepdims=True)
    p = jnp.exp2((s - mx) * (scale * LOG2E))
    l = jnp.sum(p, axis=-1, keepdims=True)
    o = jnp.einsum('uqk,ukd->uqd', p.astype(BF16), v3, preferred_element_type=F32) / l
    lse = mx * scale + jnp.log(l)
    for r in range(nres):
        for sb in range(nsb):
            rs = slice(sb * ATT_BLK, (sb + 1) * ATT_BLK)
            u0 = (r * nsb + sb) * B_HEADS
            for h in range(B_HEADS):
                o_ref[0, r, rs, h * B_HDIM:(h + 1) * B_HDIM] = o[u0 + h].astype(BF16)
            lse_ref[0, r, rs, :] = _cols_to_lanes([lse[u0 + h] for h in range(B_HEADS)])


def _band_attention(q, k, v):
    b, d, ls, _ = q.shape
    tq = min(TOK_TILE, ls)
    nj = ls // tq
    ratio = tq // ATT_BLK
    nres = min(d, TOK_TILE // tq)
    cur = lambda bb, r, j: (bb, r, j, 0)
    prev = lambda bb, r, j: (bb, r, jnp.maximum(j * ratio - 1, 0), 0)
    return pl.pallas_call(
        _band_attn_kernel,
        grid=(b, d // nres, nj),
        in_specs=[pl.BlockSpec((1, nres, tq, B_WIDTH), cur),
                  pl.BlockSpec((1, nres, tq, B_WIDTH), cur),
                  pl.BlockSpec((1, nres, ATT_BLK, B_WIDTH), prev),
                  pl.BlockSpec((1, nres, tq, B_WIDTH), cur),
                  pl.BlockSpec((1, nres, ATT_BLK, B_WIDTH), prev)],
        out_specs=[pl.BlockSpec((1, nres, tq, B_WIDTH), cur),
                   pl.BlockSpec((1, nres, tq, LANES), cur)],
        out_shape=[jax.ShapeDtypeStruct((b, d, ls, B_WIDTH), BF16),
                   jax.ShapeDtypeStruct((b, d, ls, LANES), F32)],
        compiler_params=pltpu.CompilerParams(
            dimension_semantics=("arbitrary", "arbitrary", "arbitrary"), vmem_limit_bytes=VMEM_LIMIT),
        name="band_attention_d%d" % d,
    )(q, k, k, v, v)


def _unpermute(ref):
    d, rows, width = ref.shape[1:]
    if d == 1:
        return ref[0, 0].astype(F32)
    return jnp.swapaxes(ref[0], 0, 1).reshape(d * rows, width).astype(F32)


def _l1c_kernel(x_ref, o0_ref, o1_ref, o2_ref, l0_ref, l1_ref, l2_ref, zg_ref, qm_ref, zm_ref,
                mkv_ref, wout_ref, gpost_ref, y_ref):
    tt = x_ref.shape[1]
    o_refs = (o0_ref, o1_ref, o2_ref)
    l_refs = (l0_ref, l1_ref, l2_ref)
    outs, lses = [], []
    for g, (_, d) in enumerate(B_GROUPS):
        outs.append(_unpermute(o_refs[g]))
        lses.append(_unpermute(l_refs[g])[:, 0:B_HEADS])
    mx = jnp.maximum(jnp.maximum(lses[0], lses[1]), lses[2])
    es = [jnp.exp(l - mx) for l in lses]
    tot = es[0] + es[1] + es[2]
    ws = [e / tot for e in es]
    parts = []
    for h in range(B_HEADS):
        hs = slice(h * B_HDIM, (h + 1) * B_HDIM)
        acc = ws[0][:, h:h + 1] * outs[0][:, hs]
        acc = acc + ws[1][:, h:h + 1] * outs[1][:, hs]
        acc = acc + ws[2][:, h:h + 1] * outs[2][:, hs]
        parts.append(acc)
    ydil = jnp.concatenate(parts, axis=-1)
    ymix = (ydil * _silu(zg_ref[0].astype(F32))).astype(BF16)
    mk = mkv_ref[0, :, 0:M_WIDTH]
    mv = mkv_ref[0, :, M_WIDTH:2 * M_WIDTH]
    ym = (_mem_attention(qm_ref[0], mk, mv) * _silu(zm_ref[0].astype(F32))).astype(BF16)
    out = _dot(ymix, wout_ref[0:B_WIDTH, :]) + _dot(ym, wout_ref[B_WIDTH:B_WIDTH + M_WIDTH, :])
    y_ref[0] = x_ref[0] + _rms_scale(out) * gpost_ref[...]


def _layer1_out_prompt(x1, os_, ls_, zg, qm, zm, mkv_bf, w_out, g_post):
    b, s, _ = x1.shape
    tt = min(TOK_TILE, s)
    nt = s // tt
    tile = lambda bb, i: (bb, i, 0)
    perm = lambda bb, i: (bb, 0, i, 0)
    in_specs = [pl.BlockSpec((1, tt, D_MODEL), tile)]
    for width in (B_WIDTH, LANES):
        for (_, d) in B_GROUPS:
            in_specs.append(pl.BlockSpec((1, d, tt // d, width), perm))
    in_specs += [pl.BlockSpec((1, tt, B_WIDTH), tile)] * 3
    in_specs += [pl.BlockSpec((1, N_MEM, 2 * M_WIDTH), lambda bb, i: (bb, 0, 0)),
                 _const_spec(w_out.shape), _const_spec((1, D_MODEL))]
    return pl.pallas_call(
        _l1c_kernel,
        grid=(b, nt),
        in_specs=in_specs,
        out_specs=pl.BlockSpec((1, tt, D_MODEL), tile),
        out_shape=jax.ShapeDtypeStruct((b, s, D_MODEL), F32),
        compiler_params=pltpu.CompilerParams(
            dimension_semantics=("arbitrary", "arbitrary"), vmem_limit_bytes=VMEM_LIMIT),
        name="layer1_out_prompt",
    )(x1, *os_, *ls_, zg, qm, zm, mkv_bf, w_out, g_post)


def _rope_tables(pos):
    half = B_HDIM // 2
    inv = ROPE_THETA ** (-jnp.arange(half, dtype=F32) / half)
    ang = pos[:, None] * inv[None, :]
    cos = jnp.cos(ang)
    sin = jnp.sin(ang)
    return jnp.concatenate([cos, cos], axis=-1), jnp.concatenate([-sin, sin], axis=-1)


def _prompt_group(x_prompt, mem_prompt, p, sample=None):
    b, s, _ = x_prompt.shape
    memkv_f, memkv_b = _memkv(mem_prompt.reshape(b * N_MEM, D_MODEL), p['w_mkv'])
    depth = memkv_f.shape[0]
    memkv_b = memkv_b.reshape(depth, b, N_MEM, 2 * M_WIDTH)
    job0 = sample.mlstm_job() if sample is not None else None
    if job0 is not None and b * (s // min(L0_TILE, s)) + 2 < 2 * job0[0].shape[0]:
        job0 = None
    outs0 = _layer0_prompt_pipelined(
        x_prompt, p['g_pre'][0:1], p['w_in_a'][0], p['conv_w_a'][0], p['conv_b_a'], p['w_q_a'][0],
        p['w_k_a'][0], jnp.swapaxes(p['w_k_a'][0], 1, 2), p['w_v_a'][0], p['w_if_a'], p['b_if_a'],
        p['g_hn_a'], p['skip_a'], memkv_b[0], p['w_out_a'][0], p['g_post'][0:1], job0)
    x1, conv_p, c_p, n_p, m_pad = outs0[0:5]
    if sample is not None:
        sample.after_mlstm(outs0[5:10] if job0 is not None else _dec_mlstm(*sample.mlstm_job()))
    cos_t, sin_t = _rope_tables(jnp.arange(s, dtype=F32))
    job1 = sample.attn_job if sample is not None else None
    if job1 is not None and b * (s // min(TOK_TILE, s)) != job1[0].shape[0]:
        job1 = None
    outs = _layer1_proj_prompt(x1, p['g_kv'], p['g_pre'][1:2], p['w_kv_b'], p['w_in_b'][0], cos_t, sin_t,
                               job1)
    if sample is not None:
        sample.after_attn(tuple(outs[15:17]) if job1 is not None else _dec_attn(*sample.attn_job))
    qs, ks, vs = outs[0:3], outs[3:6], outs[6:9]
    zg, qm, zm = outs[9:12]
    wins = outs[12:15]
    os_, ls_ = [], []
    for g in range(N_GROUPS):
        o, l = _band_attention(qs[g], ks[g], vs[g])
        os_.append(o)
        ls_.append(l)
    y = _layer1_out_prompt(x1, os_, ls_, zg, qm, zm, memkv_b[1], p['w_out_b'][0], p['g_post'][1:2])
    m_p = m_pad[:, 0:A_HEADS, 0][None]
    wins = [w.reshape(b, w.shape[1], 2, B_HEADS, B_HDIM) for w in wins]
    memkv_p = memkv_f.reshape(depth, b, N_MEM, 2, M_HEADS, M_HDIM)
    return y, conv_p, c_p, n_p, m_p, wins, memkv_p


def _prep_params(g_pre, g_post, w_in_a, conv_w_a, conv_b_a, w_q_a, w_k_a, w_v_a, w_if_a, b_if_a,
                 g_hn_a, skip_a, w_out_a, g_kv, w_kv_b, w_in_b, w_out_b, w_mkv):
    wif = jnp.pad(w_if_a[0], ((0, 0), (0, LANES - 2 * A_HEADS))).astype(BF16)
    bif = jnp.pad(b_if_a[0], (0, LANES - 2 * A_HEADS))[None, :]
    return {
        'g_pre': g_pre, 'g_post': g_post,
        'w_in_a': w_in_a.astype(BF16), 'conv_w_a': conv_w_a, 'conv_b_a': conv_b_a,
        'w_q_a': w_q_a.astype(BF16), 'w_k_a': w_k_a.astype(BF16), 'w_v_a': w_v_a.astype(BF16),
        'w_if_a': wif, 'b_if_a': bif, 'g_hn_a': g_hn_a, 'skip_a': skip_a,
        'w_out_a': w_out_a.astype(BF16), 'g_kv': g_kv[None, :], 'w_kv_b': w_kv_b.astype(BF16),
        'w_in_b': w_in_b.astype(BF16), 'w_out_b': w_out_b.astype(BF16), 'w_mkv': w_mkv.astype(BF16),
    }


def _dec_l0_proj_kernel(x_ref, gpre_ref, win_ref, cst_ref, convw_ref, convb_ref, wq_ref, wk_ref, wv_ref,
                        wif_ref, bif_ref,
                        q_ref, k_ref, v_ref, gates_ref, xc_ref, opre_ref, zg_ref, qm_ref, zm_ref, cnew_ref):
    h = (_rms_scale(x_ref[...]) * gpre_ref[...]).astype(BF16)
    u = _dot(h, win_ref[:, 0:A_INNER])
    opre_ref[...] = _dot(h, win_ref[:, A_INNER:2 * A_INNER])
    zg_ref[...] = _dot(h, win_ref[:, 2 * A_INNER:3 * A_INNER])
    qm_ref[...] = _dot(h, win_ref[:, 3 * A_INNER:3 * A_INNER + M_WIDTH])
    zm_ref[...] = _dot(h, win_ref[:, 3 * A_INNER + M_WIDTH:3 * A_INNER + 2 * M_WIDTH])
    cw = convw_ref[...]
    xc = convb_ref[...] + cst_ref[0] * cw[0:1, :]
    xc = xc + cst_ref[1] * cw[1:2, :]
    xc = xc + cst_ref[2] * cw[2:3, :]
    xc = xc + u * cw[3:4, :]
    xc = _silu(xc)
    xc_ref[...] = xc
    cnew_ref[0] = cst_ref[1]
    cnew_ref[1] = cst_ref[2]
    cnew_ref[2] = u
    qs, ks, vs, cat = [], [], [], []
    for hd in range(A_HEADS):
        sl = slice(hd * A_HDIM, (hd + 1) * A_HDIM)
        xh = xc[:, sl].astype(BF16)
        qh = _dot(xh, wq_ref[hd])
        kh = _dot(xh, wk_ref[hd]) * (A_HDIM ** -0.5)
        vh = _dot(u[:, sl].astype(BF16), wv_ref[hd])
        qs.append(qh)
        ks.append(kh)
        vs.append(vh)
        cat += [qh.astype(BF16), kh.astype(BF16), vh.astype(BF16)]
    q_ref[...] = jnp.concatenate(qs, axis=-1)
    k_ref[...] = jnp.concatenate(ks, axis=-1)
    v_ref[...] = jnp.concatenate(vs, axis=-1)
    gates_ref[...] = _dot(jnp.concatenate(cat, axis=-1), wif_ref[...]) + bif_ref[...]


def _whole(shape):
    nd = len(shape)
    return pl.BlockSpec(shape, lambda *_: (0,) * nd)


def _dec_l0_proj(x, g_pre, w_in, cst, conv_w, conv_b, wq, wk, wv, wif, bif):
    nb = x.shape[0]
    args = (x, g_pre, w_in, cst, conv_w, conv_b, wq, wk, wv, wif, bif)
    f = lambda *s: jax.ShapeDtypeStruct(s, F32)
    out_shape = [f(nb, A_INNER), f(nb, A_INNER), f(nb, A_INNER), f(nb, LANES), f(nb, A_INNER), f(nb, A_INNER),
                 f(nb, A_INNER), f(nb, M_WIDTH), f(nb, M_WIDTH), f(CONV_W - 1, nb, A_INNER)]
    return pl.pallas_call(
        _dec_l0_proj_kernel,
        grid=(1,),
        in_specs=[_whole(a.shape) for a in args],
        out_specs=[_whole(o.shape) for o in out_shape],
        out_shape=out_shape,
        compiler_params=pltpu.CompilerParams(dimension_semantics=("arbitrary",), vmem_limit_bytes=VMEM_LIMIT),
        name="dec_l0_proj",
    )(*args)


def _row_to_col(row, eye):
    return jnp.sum(jnp.where(eye, row, 0.0), axis=-1, keepdims=True)


def _col_to_row(colv, eye):
    return jnp.sum(jnp.where(eye, colv, 0.0), axis=0, keepdims=True)


def _dec_mem_attention(q, kv_ref_view):
    kk = kv_ref_view[:, 0]
    vv = kv_ref_view[:, 1]
    s = jnp.sum(kk * (q * (M_HDIM ** -0.5))[None], axis=-1, keepdims=True)
    mx = jnp.max(s, axis=0, keepdims=True)
    p = jnp.exp(s - mx)
    return jnp.sum(p * vv, axis=0) / jnp.sum(p, axis=0)


def _dec_mlstm_body(b, heads, with_mem, q_ref, k_ref, v_ref, gates_ref, m_ref, c_ref, n_ref, qm_ref, kv_ref,
                    hs_ref, c_out, n_out, m_out, ym_ref):
    rb = pl.ds(b, 1)
    g = gates_ref[rb, :]
    mrow = m_ref[rb, :]
    r = lax.broadcasted_iota(jnp.int32, (A_HDIM, A_HDIM), 0)
    c = lax.broadcasted_iota(jnp.int32, (A_HDIM, A_HDIM), 1)
    eye = r == c
    sl = {h: slice(h * A_HDIM, (h + 1) * A_HDIM) for h in heads}
    qh = {h: q_ref[rb, sl[h]] for h in heads}
    kh = {h: k_ref[rb, sl[h]] for h in heads}
    vh = {h: v_ref[rb, sl[h]] for h in heads}
    c_old = {h: c_ref[0, 0, h] for h in heads}
    n_old = {h: n_ref[0, 0, h:h + 1, :] for h in heads}
    li = {h: g[:, h:h + 1] for h in heads}
    lf = {h: _log_sigmoid(g[:, 4 + h:5 + h]) for h in heads}
    m_old = {h: mrow[:, h:h + 1] for h in heads}
    cq = {h: jnp.sum(c_old[h] * qh[h], axis=-1, keepdims=True) for h in heads}
    v_col = {h: _row_to_col(vh[h], eye) for h in heads}
    nq = {h: jnp.sum(n_old[h] * qh[h], axis=-1, keepdims=True) for h in heads}
    qk = {h: jnp.sum(qh[h] * kh[h], axis=-1, keepdims=True) for h in heads}
    inter = {h: lf[h] + m_old[h] for h in heads}
    m_new = {h: jnp.maximum(inter[h], li[h]) for h in heads}
    ws = {h: jnp.exp(li[h] - m_new[h]) for h in heads}
    dec = {h: jnp.exp(inter[h] - m_new[h]) for h in heads}
    sc = {h: qk[h] * ws[h] for h in heads}
    den = {h: sc[h] + dec[h] * nq[h] for h in heads}
    h_col = {h: (sc[h] * v_col[h] + dec[h] * cq[h]) / jnp.maximum(jnp.abs(den[h]), jnp.exp(-m_new[h]))
             for h in heads}
    for h in heads:
        c_out[0, 0, h] = dec[h] * c_old[h] + (ws[h] * v_col[h]) * kh[h]
        n_out[0, 0, h:h + 1, :] = dec[h] * n_old[h] + ws[h] * kh[h]
        m_out[0, h:h + 1, :] = jnp.broadcast_to(m_new[h], (1, LANES))
    for h in heads:
        hs_ref[0, :, sl[h]] = _col_to_row(h_col[h], eye)
    if with_mem:
        ym_ref[0] = _dec_mem_attention(qm_ref[0], kv_ref.at[0, 0])


def _dec_mlstm_kernel(*refs):
    _dec_mlstm_body(pl.program_id(0), range(A_HEADS), True, *refs)


def _dec_mlstm_specs(q, k, v, gates, m_in, state_c, state_n, qm3, cache_mem_kv, seq_index):
    nb = q.shape[0]

    def at(*tail, lead=()):
        return lambda *idx: lead + (seq_index(*idx),) + tail

    in_specs = [_whole(q.shape), _whole(k.shape), _whole(v.shape), _whole(gates.shape), _whole(m_in.shape),
                pl.BlockSpec((1, 1, A_HEADS, A_HDIM, A_HDIM), at(0, 0, 0, lead=(0,))),
                pl.BlockSpec((1, 1, A_HEADS, A_HDIM), at(0, 0, lead=(0,))),
                pl.BlockSpec((1, M_HEADS, M_HDIM), at(0, 0)),
                pl.BlockSpec((1, 1, N_MEM, 2, M_HEADS, M_HDIM), at(0, 0, 0, 0, lead=(0,)))]
    out_specs = [pl.BlockSpec((1, 1, A_INNER), at(0, 0)),
                 pl.BlockSpec((1, 1, A_HEADS, A_HDIM, A_HDIM), at(0, 0, 0, lead=(0,))),
                 pl.BlockSpec((1, 1, A_HEADS, A_HDIM), at(0, 0, lead=(0,))),
                 pl.BlockSpec((1, A_HEADS, LANES), at(0, 0)),
                 pl.BlockSpec((1, M_HEADS, M_HDIM), at(0, 0))]
    out_shapes = [jax.ShapeDtypeStruct((nb, 1, A_INNER), F32),
                  jax.ShapeDtypeStruct(state_c.shape, F32),
                  jax.ShapeDtypeStruct(state_n.shape, F32),
                  jax.ShapeDtypeStruct((nb, A_HEADS, LANES), F32),
                  jax.ShapeDtypeStruct((nb, M_HEADS, M_HDIM), F32)]
    return [q, k, v, gates, m_in, state_c, state_n, qm3, cache_mem_kv], in_specs, out_specs, out_shapes


def _dec_mlstm(*job):
    args, in_specs, out_specs, out_shapes = _dec_mlstm_specs(*job, seq_index=lambda b: b)
    return pl.pallas_call(
        _dec_mlstm_kernel,
        grid=(args[0].shape[0],),
        in_specs=in_specs,
        out_specs=out_specs,
        out_shape=out_shapes,
        compiler_params=pltpu.CompilerParams(dimension_semantics=("arbitrary",), vmem_limit_bytes=VMEM_LIMIT),
        name="dec_mlstm",
    )(*args)


def _dec_mid_kernel(hs_ref, opre_ref, xc_ref, zg_ref, ym_ref, zm_ref, x_ref, ghn_ref, skip_ref, wout_ref,
                    gpost_ref, gkv_ref, gpre_ref, wkv_ref, win_ref, cos_ref, sin_ref,
                    x1_ref, q_ref, k_ref, v_ref, zg1_ref, qm1_ref, zm1_ref):
    hh = _sigmoid(opre_ref[...]) * hs_ref[...]
    parts = []
    for h in range(A_HEADS):
        v = hh[:, h * A_HDIM:(h + 1) * A_HDIM]
        mu = jnp.mean(v, axis=-1, keepdims=True)
        var = jnp.mean(jnp.square(v - mu), axis=-1, keepdims=True)
        parts.append((v - mu) * lax.rsqrt(var + EPS))
    y = jnp.concatenate(parts, axis=-1) * ghn_ref[...] + skip_ref[...] * xc_ref[...]
    ymix = (y * _silu(zg_ref[...])).astype(BF16)
    ym = (ym_ref[...] * _silu(zm_ref[...])).astype(BF16)
    out = _dot(ymix, wout_ref[0:A_INNER, :]) + _dot(ym, wout_ref[A_INNER:A_INNER + M_WIDTH, :])
    x1 = x_ref[...] + _rms_scale(out) * gpost_ref[...]
    x1_ref[...] = x1
    xn = _rms_scale(x1)
    hk = (xn * gkv_ref[...]).astype(BF16)
    hq = (xn * gpre_ref[...]).astype(BF16)
    cos = cos_ref[...]
    sin = sin_ref[...]
    ks, vs = [], []
    for g in range(N_GROUPS):
        ks.append(_rope_cols(_dot(hk, wkv_ref[:, g * 2 * B_WIDTH:g * 2 * B_WIDTH + B_WIDTH]), cos, sin))
        vs.append(_dot(hk, wkv_ref[:, g * 2 * B_WIDTH + B_WIDTH:(g + 1) * 2 * B_WIDTH]))
    k_ref[...] = jnp.concatenate(ks, axis=-1)
    v_ref[...] = jnp.concatenate(vs, axis=-1)
    qoff = N_GROUPS * B_WIDTH
    q_ref[...] = _rope_cols(_dot(hq, win_ref[:, 0:qoff]), cos, sin)
    zg1_ref[...] = _dot(hq, win_ref[:, qoff:qoff + B_WIDTH])
    qm1_ref[...] = _dot(hq, win_ref[:, qoff + B_WIDTH:qoff + B_WIDTH + M_WIDTH])
    zm1_ref[...] = _dot(hq, win_ref[:, qoff + B_WIDTH + M_WIDTH:qoff + B_WIDTH + 2 * M_WIDTH])


def _dec_mid(hs, opre, xc, zg, ym, zm, x, ghn, skip, w_out, g_post, g_kv, g_pre, wkv, win, cos, sin):
    nb = x.shape[0]
    args = (hs, opre, xc, zg, ym, zm, x, ghn, skip, w_out, g_post, g_kv, g_pre, wkv, win, cos, sin)
    f = lambda *s: jax.ShapeDtypeStruct(s, F32)
    out_shape = [f(nb, D_MODEL), f(nb, N_GROUPS * B_WIDTH), f(nb, N_GROUPS * B_WIDTH), f(nb, N_GROUPS * B_WIDTH),
                 f(nb, B_WIDTH), f(nb, M_WIDTH), f(nb, M_WIDTH)]
    return pl.pallas_call(
        _dec_mid_kernel,
        grid=(1,),
        in_specs=[_whole(a.shape) for a in args],
        out_specs=[_whole(o.shape) for o in out_shape],
        out_shape=out_shape,
        compiler_params=pltpu.CompilerParams(dimension_semantics=("arbitrary",), vmem_limit_bytes=VMEM_LIMIT),
        name="dec_mid",
    )(*args)


def _dec_attn_kernel(q_ref, kn_ref, vn_ref, w0_ref, w1_ref, w2_ref, qm_ref, kv_ref, ydil_ref, ym_ref):
    w_refs = (w0_ref, w1_ref, w2_ref)
    scale = B_HDIM ** -0.5
    groups = range(N_GROUPS)
    q = [q_ref[0, g] * scale for g in groups]
    s_c = [jnp.sum(w_refs[g][0, :, 0] * q[g][None], axis=-1, keepdims=True) for g in groups]
    s_n = [jnp.sum(kn_ref[0, g] * q[g], axis=-1, keepdims=True) for g in groups]
    mxs = [jnp.maximum(jnp.max(s_c[g], axis=0), s_n[g]) for g in groups]
    p_c = [jnp.exp(s_c[g] - mxs[g][None]) for g in groups]
    p_n = [jnp.exp(s_n[g] - mxs[g]) for g in groups]
    ls = [jnp.sum(p_c[g], axis=0) + p_n[g] for g in groups]
    outs = [(jnp.sum(p_c[g] * w_refs[g][0, :, 1], axis=0) + p_n[g] * vn_ref[0, g]) / ls[g] for g in groups]
    lses = [mxs[g] + jnp.log(ls[g]) for g in groups]
    mx = jnp.maximum(jnp.maximum(lses[0], lses[1]), lses[2])
    es = [jnp.exp(l - mx) for l in lses]
    tot = es[0] + es[1] + es[2]
    ydil_ref[0] = (es[0] / tot) * outs[0] + (es[1] / tot) * outs[1] + (es[2] / tot) * outs[2]
    ym_ref[0] = _dec_mem_attention(qm_ref[0], kv_ref.at[0, 0])


def _dec_attn_specs(q4, kn4, vn4, cw0, cw1, cw2, qm3, cache_mem_kv, layer, seq_index):
    nb = q4.shape[0]
    rows = B_GROUPS[0][0]

    def at(*tail):
        return lambda *idx: (seq_index(*idx),) + tail

    win_specs = [pl.BlockSpec((1, rows, 2, B_HEADS, B_HDIM), at(0, 0, 0, 0)),
                 pl.BlockSpec((1, rows, None, 2, B_HEADS, B_HDIM), at(0, 0, 0, 0, 0)),
                 pl.BlockSpec((1, rows, None, 2, B_HEADS, B_HDIM), at(0, 0, 0, 0, 0))]
    in_specs = [pl.BlockSpec((1, N_GROUPS, B_HEADS, B_HDIM), at(0, 0, 0))] * 3 + win_specs + [
        pl.BlockSpec((1, M_HEADS, M_HDIM), at(0, 0)),
        pl.BlockSpec((1, 1, N_MEM, 2, M_HEADS, M_HDIM), lambda *idx: (layer, seq_index(*idx), 0, 0, 0, 0))]
    out_specs = [pl.BlockSpec((1, B_HEADS, B_HDIM), at(0, 0)), pl.BlockSpec((1, M_HEADS, M_HDIM), at(0, 0))]
    out_shapes = [jax.ShapeDtypeStruct((nb, B_HEADS, B_HDIM), F32),
                  jax.ShapeDtypeStruct((nb, M_HEADS, M_HDIM), F32)]
    return [q4, kn4, vn4, cw0, cw1, cw2, qm3, cache_mem_kv], in_specs, out_specs, out_shapes


def _dec_attn(q4, kn4, vn4, cw0, cw1, cw2, qm3, cache_mem_kv, layer):
    args, in_specs, out_specs, out_shapes = _dec_attn_specs(
        q4, kn4, vn4, cw0, cw1, cw2, qm3, cache_mem_kv, layer, seq_index=lambda b: b)
    return pl.pallas_call(
        _dec_attn_kernel,
        grid=(q4.shape[0],),
        in_specs=in_specs,
        out_specs=out_specs,
        out_shape=out_shapes,
        compiler_params=pltpu.CompilerParams(dimension_semantics=("arbitrary",), vmem_limit_bytes=VMEM_LIMIT),
        name="dec_attn",
    )(*args)


def _dec_out_kernel(ydil_ref, zg_ref, ym_ref, zm_ref, x_ref, wout_ref, gpost_ref, y_ref):
    ymix = (ydil_ref[...] * _silu(zg_ref[...])).astype(BF16)
    ym = (ym_ref[...] * _silu(zm_ref[...])).astype(BF16)
    out = _dot(ymix, wout_ref[0:B_WIDTH, :]) + _dot(ym, wout_ref[B_WIDTH:B_WIDTH + M_WIDTH, :])
    y_ref[...] = x_ref[...] + _rms_scale(out) * gpost_ref[...]


def _dec_out(ydil, zg, ym, zm, x1, w_out, g_post):
    args = (ydil, zg, ym, zm, x1, w_out, g_post)
    return pl.pallas_call(
        _dec_out_kernel,
        grid=(1,),
        in_specs=[_whole(a.shape) for a in args],
        out_specs=_whole(x1.shape),
        out_shape=jax.ShapeDtypeStruct(x1.shape, F32),
        compiler_params=pltpu.CompilerParams(dimension_semantics=("arbitrary",), vmem_limit_bytes=VMEM_LIMIT),
        name="dec_out",
    )(*args)


class _SampleGroup:
    def __init__(self, x_sample, state_conv, state_c, state_n, state_m, cache_wins, cache_mem_kv, p):
        self.p = p
        self.nb = nb = x_sample.shape[0]
        self.cache_wins = cache_wins
        self.cache_mem_kv = cache_mem_kv
        self.x = x_sample.reshape(nb, D_MODEL)
        cst = state_conv[0].transpose(1, 0, 2)
        q, k, v, gates, self.xc, self.opre, self.zg, qm, self.zm, cnew = _dec_l0_proj(
            self.x, p['g_pre'][0:1], p['w_in_a'][0], cst, p['conv_w_a'][0], p['conv_b_a'], p['w_q_a'][0],
            p['w_k_a'][0], p['w_v_a'][0], p['w_if_a'], p['b_if_a'])
        m_in = jnp.pad(state_m[0], ((0, 0), (0, LANES - A_HEADS)))
        self._mlstm_job = (q, k, v, gates, m_in, state_c, state_n, qm.reshape(nb, M_HEADS, M_HDIM), cache_mem_kv)
        self.conv_s = cnew.transpose(1, 0, 2)[None]
        self.attn_job = None

    def mlstm_job(self):
        return self._mlstm_job

    def after_mlstm(self, res):
        p, nb = self.p, self.nb
        hs, self.c_s, self.n_s, m_rows, ym0 = res
        self.m_s = m_rows[:, :, 0][None]
        cos, sin = _rope_tables(PAST_LEN + jnp.arange(1, dtype=F32))
        self.x1, qd, kn, vn, self.zg1, qm1, self.zm1 = _dec_mid(
            hs.reshape(nb, A_INNER), self.opre, self.xc, self.zg, ym0.reshape(nb, M_WIDTH), self.zm, self.x,
            p['g_hn_a'], p['skip_a'], p['w_out_a'][0], p['g_post'][0:1], p['g_kv'], p['g_pre'][1:2],
            p['w_kv_b'], p['w_in_b'][0], cos, sin)
        shp4 = (nb, N_GROUPS, B_HEADS, B_HDIM)
        kn4 = kn.reshape(shp4)
        vn4 = vn.reshape(shp4)
        cws = [self.cache_wins[0]]
        for g in (1, 2):
            w, d = B_GROUPS[g]
            cws.append(self.cache_wins[g].reshape(nb, w // d, d, 2, B_HEADS, B_HDIM))
        self.attn_job = (qd.reshape(shp4), kn4, vn4, cws[0], cws[1], cws[2], qm1.reshape(nb, M_HEADS, M_HDIM),
                         self.cache_mem_kv, 1)
        self.wins_s = [jnp.stack([kn4[:, g], vn4[:, g]], axis=1)[:, None] for g in range(N_GROUPS)]

    def after_attn(self, res):
        p, nb = self.p, self.nb
        ydil, ym1 = res
        y = _dec_out(ydil.reshape(nb, B_WIDTH), self.zg1, ym1.reshape(nb, M_WIDTH), self.zm1, self.x1,
                     p['w_out_b'][0], p['g_post'][1:2])
        self.y = y.reshape(nb, 1, D_MODEL)

    def outputs(self):
        return self.y, self.conv_s, self.c_s, self.n_s, self.m_s, self.wins_s


def _sample_group(x_sample, state_conv, state_c, state_n, state_m, cache_wins, cache_mem_kv, p):
    sg = _SampleGroup(x_sample, state_conv, state_c, state_n, state_m, cache_wins, cache_mem_kv, p)
    sg.after_mlstm(_dec_mlstm(*sg.mlstm_job()))
    sg.after_attn(_dec_attn(*sg.attn_job))
    return sg.outputs()


def kernel(x_prompt, x_sample, mem_prompt, state_conv, state_C, state_n, state_m, cache_win0, cache_win1,
           cache_win2, cache_mem_kv, g_pre, g_post, w_in_a, conv_w_a, conv_b_a, w_q_a, w_k_a, w_v_a, w_if_a,
           b_if_a, g_hn_a, skip_a, w_out_a, g_kv, w_kv_b, w_in_b, w_out_b, w_mkv):
    p = _prep_params(g_pre, g_post, w_in_a, conv_w_a, conv_b_a, w_q_a, w_k_a, w_v_a, w_if_a, b_if_a,
                     g_hn_a, skip_a, w_out_a, g_kv, w_kv_b, w_in_b, w_out_b, w_mkv)
    sample = _SampleGroup(x_sample, state_conv, state_C, state_n, state_m,
                          (cache_win0, cache_win1, cache_win2), cache_mem_kv, p)
    y_p, conv_p, c_p, n_p, m_p, wins_p, memkv_p = _prompt_group(x_prompt, mem_prompt, p, sample)
    y_s, conv_s, c_s, n_s, m_s, wins_s = sample.outputs()
    return (y_p, y_s, conv_p, c_p, n_p, m_p, wins_p[0], wins_p[1], wins_p[2], memkv_p,
            conv_s, c_s, n_s, m_s, wins_s[0], wins_s[1], wins_s[2])
```

```python
import functools

import jax
import jax.numpy as jnp
from jax import lax
from jax.experimental import pallas as pl
from jax.experimental.pallas import tpu as pltpu

F32 = jnp.float32
BF16 = jnp.bfloat16

D_MODEL = 1024
A_HEADS = 4
A_HDIM = 256
A_INNER = 1024
CONV_W = 4
A_CHUNK = 128
B_GROUPS = ((128, 1), (512, 4), (2048, 16))
N_GROUPS = 3
B_HEADS = 4
B_HDIM = 128
B_WIDTH = 512
N_MEM = 256
M_HEADS = 4
M_HDIM = 128
M_WIDTH = 512
ROPE_THETA = 10000.0
EPS = 1e-6
PAST_LEN = 8192

LANES = 128
TOK_TILE = 512
L0_TILE = 256
ATT_BLK = 128
ATT_GROUP = 2
VMEM_LIMIT = 56 * 1024 * 1024

NT_DIMS = (((1,), (1,)), ((), ()))
LOG2E = 1.4426950408889634


def _dot(a, b):
    return jnp.dot(a, b, preferred_element_type=F32)


def _dot_nt(a, b):
    return lax.dot_general(a, b, NT_DIMS, preferred_element_type=F32)


def _sigmoid(x):
    return 1.0 / (1.0 + jnp.exp(-x))


def _silu(x):
    return x * _sigmoid(x)


def _log_sigmoid(x):
    return jnp.minimum(x, 0.0) - jnp.log(1.0 + jnp.exp(-jnp.abs(x)))


def _rms_scale(x):
    return x * lax.rsqrt(jnp.mean(x * x, axis=-1, keepdims=True) + EPS)


def _const_spec(shape):
    nd = len(shape)
    return pl.BlockSpec(shape, lambda *_: (0,) * nd, pipeline_mode=pl.Buffered(1))


def _mem_attention(qm, mk, mv):
    heads = range(M_HEADS)
    sl = [slice(h * M_HDIM, (h + 1) * M_HDIM) for h in heads]
    s = [_dot_nt(qm[:, sl[h]], mk[:, sl[h]]) * (M_HDIM ** -0.5) for h in heads]
    mx = [jnp.max(s[h], axis=-1, keepdims=True) for h in heads]
    p = [jnp.exp(s[h] - mx[h]) for h in heads]
    l = [jnp.sum(p[h], axis=-1, keepdims=True) for h in heads]
    outs = [_dot((p[h] / l[h]).astype(BF16), mv[:, sl[h]]) for h in heads]
    return jnp.concatenate(outs, axis=-1)


def _rows_to_heads(x):
    return jnp.swapaxes(jnp.stack([x[:, h * LANES:(h + 1) * LANES] for h in range(x.shape[1] // LANES)]), 0, 1)


def _heads_to_rows(ref):
    return jnp.concatenate([ref[:, h, :] for h in range(ref.shape[1])], axis=-1)


def _rows_to_kv_heads(k, v):
    pieces = [a[:, h * LANES:(h + 1) * LANES] for a in (k, v) for h in range(a.shape[1] // LANES)]
    return jnp.swapaxes(jnp.stack(pieces), 0, 1)


def _memkv_kernel(m_ref, w_ref, o_ref, ob_ref):
    r = _dot(m_ref[...].astype(BF16), w_ref[0])
    o_ref[0] = _rows_to_kv_heads(r[:, 0:M_WIDTH], r[:, M_WIDTH:2 * M_WIDTH])
    ob_ref[0] = r.astype(BF16)


def _memkv(mem2d, w_bf):
    nm = mem2d.shape[0]
    nl = w_bf.shape[0]
    tm = min(512, nm)
    return pl.pallas_call(
        _memkv_kernel,
        grid=(nl, nm // tm),
        in_specs=[pl.BlockSpec((tm, D_MODEL), lambda l, i: (i, 0)),
                  pl.BlockSpec((1, D_MODEL, 2 * M_WIDTH), lambda l, i: (l, 0, 0))],
        out_specs=[pl.BlockSpec((1, tm, 2 * M_HEADS, M_HDIM), lambda l, i: (l, i, 0, 0)),
                   pl.BlockSpec((1, tm, 2 * M_WIDTH), lambda l, i: (l, i, 0))],
        out_shape=[jax.ShapeDtypeStruct((nl, nm, 2 * M_HEADS, M_HDIM), F32),
                   jax.ShapeDtypeStruct((nl, nm, 2 * M_WIDTH), BF16)],
        compiler_params=pltpu.CompilerParams(dimension_semantics=("arbitrary", "arbitrary")),
        name="memkv",
    )(mem2d, w_bf)


def _mlstm_chunk(rs, g, qkv_v, kt_v, c_s, n_s, m_s, causal, tri, hs):
    ls = _log_sigmoid(g)
    t0 = ls.astype(BF16)
    e1 = ls - t0.astype(F32)
    t1 = e1.astype(BF16)
    t2 = (e1 - t1.astype(F32)).astype(BF16)
    bc = _dot(tri, t0) + _dot(tri, t1) + _dot(tri, t2)
    lane = lax.broadcasted_iota(jnp.int32, (A_CHUNK, LANES), 1)
    xt = jnp.where(lane < A_HEADS, g, bc).T
    yield
    heads = range(A_HEADS)
    b_col = [bc[:, 4 + h:5 + h] for h in heads]
    b_row = [xt[4 + h:5 + h, :] for h in heads]
    li_row = [xt[h:h + 1, :] for h in heads]
    li_col = [g[:, h:h + 1] for h in heads]
    m_old = [m_s[h:h + 1, 0:1] for h in heads]
    b_last = [bc[A_CHUNK - 1:A_CHUNK, 4 + h:5 + h] for h in heads]
    qh = [qkv_v[rs, h * 3 * A_HDIM:h * 3 * A_HDIM + A_HDIM] for h in heads]
    kh = [qkv_v[rs, h * 3 * A_HDIM + A_HDIM:h * 3 * A_HDIM + 2 * A_HDIM] for h in heads]
    vh = [qkv_v[rs, h * 3 * A_HDIM + 2 * A_HDIM:(h + 1) * 3 * A_HDIM] for h in heads]
    kt = [kt_v[h] for h in heads]
    c_old = [c_s[h] for h in heads]
    n_old = [n_s[h:h + 1, :] for h in heads]
    qk = [_dot_nt(qh[h], kh[h]) for h in heads]
    qc = [_dot(qh[h], c_old[h].astype(BF16)) for h in heads]
    dm = [jnp.where(causal, b_col[h] - b_row[h] + li_row[h], -jnp.inf) for h in heads]
    inter = [b_col[h] + m_old[h] for h in heads]
    m_row = [jnp.maximum(inter[h], jnp.max(dm[h], axis=-1, keepdims=True)) for h in heads]
    g_max = [jnp.max(b_last[h] - b_row[h] + li_row[h], axis=-1, keepdims=True) for h in heads]
    m_new = [jnp.maximum(b_last[h] + m_old[h], g_max[h]) for h in heads]
    yield
    sc = [qk[h] * jnp.exp(dm[h] - m_row[h]) for h in heads]
    dec = [jnp.exp(inter[h] - m_row[h]) for h in heads]
    ws_col = [jnp.exp(b_last[h] - b_col[h] + li_col[h] - m_new[h]) for h in heads]
    dc = [jnp.exp(b_last[h] + m_old[h] - m_new[h]) for h in heads]
    yield
    sv = [_dot(sc[h].astype(BF16), vh[h]) for h in heads]
    wv = [(ws_col[h] * vh[h].astype(F32)).astype(BF16) for h in heads]
    upd = [_dot(kt[h], wv[h]) for h in heads]
    yield
    for h in heads:
        den = (jnp.sum(sc[h], axis=-1, keepdims=True)
               + dec[h] * jnp.sum(qh[h].astype(F32) * n_old[h], axis=-1, keepdims=True))
        num = sv[h] + dec[h] * qc[h]
        hs.append(num / jnp.maximum(jnp.abs(den), jnp.exp(-m_row[h])))
    yield
    for h in heads:
        c_s[h] = dc[h] * c_old[h] + upd[h]
        n_s[h:h + 1, :] = dc[h] * n_old[h] + jnp.sum(ws_col[h] * kh[h].astype(F32), axis=0, keepdims=True)
        m_s[h:h + 1, :] = jnp.broadcast_to(m_new[h], (1, LANES))
    yield


def _l0p_kernel(nt, n_dec_seq, *refs):
    n_dec_in, n_dec_out = (9, 5) if n_dec_seq else (0, 0)
    (x_ref, xp_ref, gpre_ref, win_ref, convw_ref, convb_ref, wq_ref, wk_ref, wkt_ref, wv_ref,
     wif_ref, bif_ref, ghn_ref, skip_ref, mkv_ref, wout_ref, gpost_ref) = refs[0:17]
    dec_in = refs[17:17 + n_dec_in]
    x1_ref, conv_out, c_out, n_out, m_out = refs[17 + n_dec_in:22 + n_dec_in]
    dec_out = refs[22 + n_dec_in:22 + n_dec_in + n_dec_out]
    (h_s, u_s, ymix_s, xc_s, opre_s, zg_s, qm_s, zm_s, qkv_s, kt_s, gates_s,
     c_s, n_s, m_s) = refs[22 + n_dec_in + n_dec_out:]
    tt = x_ref.shape[1]
    nsub = tt // A_CHUNK
    t = pl.program_id(0)
    parity = lax.rem(t + 1, 2)
    pos1 = lax.rem(t + nt - 1, nt)
    pos2 = lax.rem(t + 2 * nt - 2, nt)

    @pl.when(t == 0)
    def _():
        h_s[...] = jnp.zeros(h_s.shape, BF16)
        u_s[...] = jnp.zeros(u_s.shape, F32)
        xc_s[0] = jnp.zeros(xc_s.shape[1:], F32)
        opre_s[0] = jnp.zeros(opre_s.shape[1:], F32)
        zg_s[0] = jnp.zeros(zg_s.shape[1:], F32)
        qm_s[0] = jnp.zeros(qm_s.shape[1:], BF16)
        zm_s[0] = jnp.zeros(zm_s.shape[1:], F32)
        qkv_s[0] = jnp.zeros(qkv_s.shape[1:], BF16)
        kt_s[0] = jnp.zeros(kt_s.shape[1:], BF16)
        gates_s[0] = jnp.zeros(gates_s.shape[1:], F32)

    @pl.when(pos1 == 0)
    def _():
        u_s[0:8, :] = jnp.zeros((8, A_INNER), F32)

    @pl.when(pos2 == 0)
    def _():
        c_s[...] = jnp.zeros(c_s.shape, F32)
        n_s[...] = jnp.zeros(n_s.shape, F32)
        m_s[...] = jnp.zeros(m_s.shape, F32)

    row = lax.broadcasted_iota(jnp.int32, (A_CHUNK, A_CHUNK), 0)
    col = lax.broadcasted_iota(jnp.int32, (A_CHUNK, A_CHUNK), 1)
    causal = col <= row
    tri = jnp.where(causal, 1.0, 0.0).astype(BF16)
    ghn = ghn_ref[...]
    skp = skip_ref[...]
    mk = mkv_ref[0, :, 0:M_WIDTH]
    mv = mkv_ref[0, :, M_WIDTH:2 * M_WIDTH]

    def stage2(pslot):
        ym = _mem_attention(qm_s[pslot], mk, mv) * _silu(zm_s[pslot])
        ymix_s[:, A_INNER:A_INNER + M_WIDTH] = ym.astype(BF16)
        yield
        for c in range(nsub):
            rs = slice(c * A_CHUNK, (c + 1) * A_CHUNK)
            hs = []
            yield from _mlstm_chunk(rs, gates_s[pslot, rs, :], qkv_s.at[pslot], kt_s.at[pslot, :, c],
                                    c_s, n_s, m_s, causal, tri, hs)
            parts = []
            for h in range(A_HEADS):
                v = _sigmoid(opre_s[pslot, rs, h * A_HDIM:(h + 1) * A_HDIM]) * hs[h]
                mu = jnp.mean(v, axis=-1, keepdims=True)
                var = jnp.mean(jnp.square(v - mu), axis=-1, keepdims=True)
                parts.append((v - mu) * lax.rsqrt(var + EPS))
            hn = jnp.concatenate(parts, axis=-1) * ghn
            y = hn + skp * xc_s[pslot, rs, :]
            ymix_s[rs, 0:A_INNER] = (y * _silu(zg_s[pslot, rs, :])).astype(BF16)
            yield

    def stage1(slot):
        hb = h_s[...]
        u_s[8:8 + tt, :] = _dot(hb, win_ref[:, 0:A_INNER])
        yield
        cw = convw_ref[...]
        cb = convb_ref[...]
        for c in range(nsub):
            r0 = c * A_CHUNK
            blk = u_s[r0:r0 + A_CHUNK + 8, :]
            xc = cb + pltpu.roll(blk, 3, 0)[8:, :] * cw[0:1, :]
            xc = xc + pltpu.roll(blk, 2, 0)[8:, :] * cw[1:2, :]
            xc = xc + pltpu.roll(blk, 1, 0)[8:, :] * cw[2:3, :]
            xc = xc + blk[8:, :] * cw[3:4, :]
            xc_s[slot, r0:r0 + A_CHUNK, :] = _silu(xc)
        opre_s[slot] = _dot(hb, win_ref[:, A_INNER:2 * A_INNER])
        yield
        zg_s[slot] = _dot(hb, win_ref[:, 2 * A_INNER:3 * A_INNER])
        yield
        qm_s[slot] = _dot(hb, win_ref[:, 3 * A_INNER:3 * A_INNER + M_WIDTH]).astype(BF16)
        zm_s[slot] = _dot(hb, win_ref[:, 3 * A_INNER + M_WIDTH:3 * A_INNER + 2 * M_WIDTH])
        yield
        for h in range(A_HEADS):
            sl = slice(h * A_HDIM, (h + 1) * A_HDIM)
            xh = xc_s[slot, :, sl].astype(BF16)
            uh = u_s[8:8 + tt, sl].astype(BF16)
            base = h * 3 * A_HDIM
            qkv_s[slot, :, base:base + A_HDIM] = _dot(xh, wq_ref[h]).astype(BF16)
            qkv_s[slot, :, base + A_HDIM:base + 2 * A_HDIM] = (
                _dot(xh, wk_ref[h]) * (A_HDIM ** -0.5)).astype(BF16)
            qkv_s[slot, :, base + 2 * A_HDIM:base + 3 * A_HDIM] = _dot(uh, wv_ref[h]).astype(BF16)
            kt = (_dot_nt(wkt_ref[h], xh) * (A_HDIM ** -0.5)).astype(BF16)
            for c in range(nsub):
                kt_s[slot, h, c] = kt[:, c * A_CHUNK:(c + 1) * A_CHUNK]
            yield
        gates_s[slot] = _dot(qkv_s[slot], wif_ref[...]) + bif_ref[...]
        yield

    def step(slot):
        if n_dec_seq:
            first = slot == 1
            heads = (0, 1) if first else (2, 3)
            _dec_mlstm_body(jnp.minimum(t // 2, n_dec_seq - 1), heads, first, *dec_in, *dec_out)
        pending = [stage1(slot), stage2(1 - slot)]
        while pending:
            for gen in list(pending):
                try:
                    next(gen)
                except StopIteration:
                    pending.remove(gen)
        h_next = (_rms_scale(x_ref[0]) * gpre_ref[...]).astype(BF16)
        out = _dot(ymix_s[...], wout_ref[...])
        h_s[...] = h_next
        x1_ref[0] = xp_ref[0] + _rms_scale(out) * gpost_ref[...]

    for s in range(2):
        pl.when(parity == s)(functools.partial(step, s))

    @pl.when(jnp.logical_and(pos1 == nt - 1, t > 0))
    def _():
        conv_out[0, 0] = u_s[tt + 5:tt + 8, :]

    u_s[0:8, :] = u_s[tt:tt + 8, :]

    @pl.when(jnp.logical_and(pos2 == nt - 1, t > 1))
    def _():
        for h in range(A_HEADS):
            c_out[0, 0, h] = c_s[h].T
        n_out[0, 0] = n_s[0:A_HEADS, :]
        m_out[0] = m_s[...]


def _layer0_prompt_pipelined(x, g_pre, w_in, conv_w, conv_b, wq, wk, wkt, wv, wif, bif, ghn, skip, mkv_bf,
                             w_out, g_post, dec_job=None):
    b, s, _ = x.shape
    tt = min(L0_TILE, s)
    nt = s // tt
    ntiles = b * nt
    a_in = w_in.shape[1]
    nsub = tt // A_CHUNK

    def cur(t):
        t1 = jnp.minimum(t, ntiles - 1)
        return (t1 // nt, t1 % nt, 0)

    def prev(t):
        t2 = jnp.maximum(t - 2, 0)
        return (t2 // nt, t2 % nt, 0)

    def prev_b(t):
        return jnp.maximum(t - 2, 0) // nt

    in_specs = [
        pl.BlockSpec((1, tt, D_MODEL), cur),
        pl.BlockSpec((1, tt, D_MODEL), prev),
        _const_spec((1, D_MODEL)),
        _const_spec((D_MODEL, a_in)),
        _const_spec((CONV_W, A_INNER)),
        _const_spec((1, A_INNER)),
        _const_spec((A_HEADS, A_HDIM, A_HDIM)),
        _const_spec((A_HEADS, A_HDIM, A_HDIM)),
        _const_spec((A_HEADS, A_HDIM, A_HDIM)),
        _const_spec((A_HEADS, A_HDIM, A_HDIM)),
        _const_spec((3 * A_INNER, LANES)),
        _const_spec((1, LANES)),
        _const_spec((1, A_INNER)),
        _const_spec((1, A_INNER)),
        pl.BlockSpec((1, N_MEM, 2 * M_WIDTH), lambda t: (prev_b(t), 0, 0)),
        _const_spec((A_INNER + M_WIDTH, D_MODEL)),
        _const_spec((1, D_MODEL)),
    ]
    out_specs = [
        pl.BlockSpec((1, tt, D_MODEL), prev),
        pl.BlockSpec((1, 1, CONV_W - 1, A_INNER), lambda t: (0, prev_b(t), 0, 0)),
        pl.BlockSpec((1, 1, A_HEADS, A_HDIM, A_HDIM), lambda t: (0, prev_b(t), 0, 0, 0)),
        pl.BlockSpec((1, 1, A_HEADS, A_HDIM), lambda t: (0, prev_b(t), 0, 0)),
        pl.BlockSpec((1, 8, LANES), lambda t: (prev_b(t), 0, 0)),
    ]
    out_shape = [
        jax.ShapeDtypeStruct((b, s, D_MODEL), F32),
        jax.ShapeDtypeStruct((1, b, CONV_W - 1, A_INNER), F32),
        jax.ShapeDtypeStruct((1, b, A_HEADS, A_HDIM, A_HDIM), F32),
        jax.ShapeDtypeStruct((1, b, A_HEADS, A_HDIM), F32),
        jax.ShapeDtypeStruct((b, 8, LANES), F32),
    ]
    scratch = [
        pltpu.VMEM((tt, D_MODEL), BF16),
        pltpu.VMEM((tt + 8, A_INNER), F32),
        pltpu.VMEM((tt, A_INNER + M_WIDTH), BF16),
        pltpu.VMEM((2, tt, A_INNER), F32),
        pltpu.VMEM((2, tt, A_INNER), F32),
        pltpu.VMEM((2, tt, A_INNER), F32),
        pltpu.VMEM((2, tt, M_WIDTH), BF16),
        pltpu.VMEM((2, tt, M_WIDTH), F32),
        pltpu.VMEM((2, tt, 3 * A_INNER), BF16),
        pltpu.VMEM((2, A_HEADS, nsub, A_HDIM, A_CHUNK), BF16),
        pltpu.VMEM((2, tt, LANES), F32),
        pltpu.VMEM((A_HEADS, A_HDIM, A_HDIM), F32),
        pltpu.VMEM((8, A_HDIM), F32),
        pltpu.VMEM((8, LANES), F32),
    ]
    dec_args, dec_specs, dec_out_specs, dec_out_shapes, n_dec_seq = [], [], [], [], 0
    if dec_job is not None:
        n_dec_seq = dec_job[0].shape[0]
        dec_args, dec_specs, dec_out_specs, dec_out_shapes = _dec_mlstm_specs(
            *dec_job, seq_index=lambda t: jnp.minimum(t // 2, n_dec_seq - 1))
    return pl.pallas_call(
        functools.partial(_l0p_kernel, nt, n_dec_seq),
        grid=(ntiles + 2,),
        in_specs=in_specs + dec_specs,
        out_specs=out_specs + dec_out_specs,
        out_shape=out_shape + dec_out_shapes,
        scratch_shapes=scratch,
        compiler_params=pltpu.CompilerParams(
            dimension_semantics=("arbitrary",), vmem_limit_bytes=VMEM_LIMIT),
        name="layer0_prompt",
    )(x, x, g_pre, w_in, conv_w, conv_b, wq, wk, wkt, wv, wif, bif, ghn, skip, mkv_bf, w_out, g_post,
      *dec_args)


def _rope_cols(x, cos, sin_signed):
    outs = []
    for cblk in range(x.shape[1] // B_HDIM):
        xb = x[:, cblk * B_HDIM:(cblk + 1) * B_HDIM]
        outs.append(xb * cos + pltpu.roll(xb, B_HDIM // 2, 1) * sin_signed)
    return jnp.concatenate(outs, axis=-1)


def _l1a_kernel(n_dec_in, *refs):
    x_ref, gkv_ref, gpre_ref, wkv_ref, win_ref, cos_ref, sin_ref = refs[0:7]
    dec_in = refs[7:7 + n_dec_in]
    (q0_ref, q1_ref, q2_ref, k0_ref, k1_ref, k2_ref, v0_ref, v1_ref, v2_ref,
     zg_ref, qm_ref, zm_ref, w0_ref, w1_ref, w2_ref) = refs[7 + n_dec_in:22 + n_dec_in]
    dec_out = refs[22 + n_dec_in:]
    tt = x_ref.shape[1]
    xn = _rms_scale(x_ref[0])
    hk = (xn * gkv_ref[...]).astype(BF16)
    hq = (xn * gpre_ref[...]).astype(BF16)
    cos = cos_ref[...]
    sin = sin_ref[...]
    q_refs = (q0_ref, q1_ref, q2_ref)
    k_refs = (k0_ref, k1_ref, k2_ref)
    v_refs = (v0_ref, v1_ref, v2_ref)
    w_refs = (w0_ref, w1_ref, w2_ref)
    for g in (2, 1, 0):
        d = B_GROUPS[g][1]
        kf = _rope_cols(_dot(hk, wkv_ref[:, g * 2 * B_WIDTH:g * 2 * B_WIDTH + B_WIDTH]), cos, sin)
        vf = _dot(hk, wkv_ref[:, g * 2 * B_WIDTH + B_WIDTH:(g + 1) * 2 * B_WIDTH])
        qf = _rope_cols(_dot(hq, win_ref[:, g * B_WIDTH:(g + 1) * B_WIDTH]), cos, sin)
        wr = w_refs[g]
        wrows = wr.shape[1]
        wr[0] = _rows_to_kv_heads(kf[tt - wrows:, :], vf[tt - wrows:, :])
        for val, ref in ((qf.astype(BF16), q_refs[g]), (kf.astype(BF16), k_refs[g]), (vf.astype(BF16), v_refs[g])):
            if d == 1:
                ref[0, 0] = val
            else:
                ref[0] = jnp.swapaxes(val.reshape(tt // d, d, val.shape[1]), 0, 1)
        if g == 2 and n_dec_in:
            _dec_attn_kernel(*dec_in, *dec_out)
    qoff = N_GROUPS * B_WIDTH
    zg_ref[0] = _dot(hq, win_ref[:, qoff:qoff + B_WIDTH]).astype(BF16)
    qm_ref[0] = _dot(hq, win_ref[:, qoff + B_WIDTH:qoff + B_WIDTH + M_WIDTH]).astype(BF16)
    zm_ref[0] = _dot(hq, win_ref[:, qoff + B_WIDTH + M_WIDTH:qoff + B_WIDTH + 2 * M_WIDTH]).astype(BF16)


def _layer1_proj_prompt(x1, g_kv, g_pre, wkv, win, cos_t, sin_t, dec_job=None):
    b, s, _ = x1.shape
    tt = min(TOK_TILE, s)
    nt = s // tt
    tile = lambda bb, i: (bb, i, 0)
    in_specs = [
        pl.BlockSpec((1, tt, D_MODEL), tile),
        _const_spec((1, D_MODEL)),
        _const_spec((1, D_MODEL)),
        _const_spec(wkv.shape),
        _const_spec(win.shape),
        pl.BlockSpec((tt, B_HDIM), lambda bb, i: (i, 0)),
        pl.BlockSpec((tt, B_HDIM), lambda bb, i: (i, 0)),
    ]
    qkv_specs, qkv_shapes = [], []
    for _ in range(3):
        for (_, d) in B_GROUPS:
            qkv_specs.append(pl.BlockSpec((1, d, tt // d, B_WIDTH), lambda bb, i: (bb, 0, i, 0)))
            qkv_shapes.append(jax.ShapeDtypeStruct((b, d, s // d, B_WIDTH), BF16))
    gate_specs = [pl.BlockSpec((1, tt, B_WIDTH), tile)] * 3
    gate_shapes = [jax.ShapeDtypeStruct((b, s, B_WIDTH), BF16)] * 3
    win_specs, win_shapes = [], []
    for (w, _) in B_GROUPS:
        wr = min(w, s)
        rows = min(wr, tt)
        nblk = wr // rows
        win_specs.append(pl.BlockSpec(
            (1, rows, 2 * B_HEADS, B_HDIM),
            functools.partial(lambda bb, i, nb: (bb, jnp.maximum(i - (nt - nb), 0), 0, 0), nb=nblk)))
        win_shapes.append(jax.ShapeDtypeStruct((b, wr, 2 * B_HEADS, B_HDIM), F32))
    dec_args, dec_specs, dec_out_specs, dec_out_shapes = [], [], [], []
    if dec_job is not None:
        dec_args, dec_specs, dec_out_specs, dec_out_shapes = _dec_attn_specs(
            *dec_job, seq_index=lambda bb, i: bb * nt + i)
    return pl.pallas_call(
        functools.partial(_l1a_kernel, len(dec_args)),
        grid=(b, nt),
        in_specs=in_specs + dec_specs,
        out_specs=qkv_specs + gate_specs + win_specs + dec_out_specs,
        out_shape=qkv_shapes + gate_shapes + win_shapes + dec_out_shapes,
        compiler_params=pltpu.CompilerParams(
            dimension_semantics=("arbitrary", "arbitrary"), vmem_limit_bytes=VMEM_LIMIT),
        name="layer1_proj_prompt",
    )(x1, g_kv, g_pre, wkv, win, cos_t, sin_t, *dec_args)


def _cols_to_lanes(cols):
    t = cols[0].shape[0]
    lane = lax.broadcasted_iota(jnp.int32, (t, LANES), 1)
    acc = jnp.zeros((t, LANES), F32)
    for h, cvec in enumerate(cols):
        acc = jnp.where(lane == h, cvec, acc)
    return acc


def _band_attn_kernel(q_ref, kc_ref, kp_ref, vc_ref, vp_ref, o_ref, lse_ref):
    nres, tq = q_ref.shape[1:3]
    nsb = tq // ATT_BLK
    j = pl.program_id(2)
    row = lax.broadcasted_iota(jnp.int32, (ATT_BLK, 2 * ATT_BLK), 0)
    col = lax.broadcasted_iota(jnp.int32, (ATT_BLK, 2 * ATT_BLK), 1)
    band = jnp.logical_and(col >= row, col <= row + ATT_BLK)
    first_pen = jnp.where(col < ATT_BLK, jnp.where(j > 0, 0.0, -jnp.inf), 0.0)
    scale = B_HDIM ** -0.5
    blocks = [(r, sb) for r in range(nres) for sb in range(nsb)]
    for g0 in range(0, len(blocks), ATT_GROUP):
        grp = blocks[g0:g0 + ATT_GROUP]
        qs, ks, vs, pens = [], [], [], []
        for r, sb in grp:
            rs = slice(sb * ATT_BLK, (sb + 1) * ATT_BLK)
            ps = slice((sb - 1) * ATT_BLK, sb * ATT_BLK)
            for h in range(B_HEADS):
                hs = slice(h * B_HDIM, (h + 1) * B_HDIM)
                qs.append(q_ref[0, r, rs, hs])
                kp = kp_ref[0, r, :, hs] if sb == 0 else kc_ref[0, r, ps, hs]
                vp = vp_ref[0, r, :, hs] if sb == 0 else vc_ref[0, r, ps, hs]
                ks.append(jnp.concatenate([kp, kc_ref[0, r, rs, hs]], axis=0))
                vs.append(jnp.concatenate([vp, vc_ref[0, r, rs, hs]], axis=0))
                pens.append(sb == 0)
        q3 = jnp.stack(qs)
        k3 = jnp.stack(ks)
        v3 = jnp.stack(vs)
        s = jnp.einsum('uqd,ukd->uqk', q3, k3, preferred_element_type=F32)
        s = jnp.stack([s[u] + first_pen if pens[u] else s[u] for u in range(len(pens))])
        s = jnp.where(band[None], s, -jnp.inf)
        mx = jnp.max(s, axis=-1, keepdims=True)
        p = jnp.exp2((s - mx) * (scale * LOG2E))
        l = jnp.sum(p, axis=-1, keepdims=True)
        o = jnp.einsum('uqk,ukd->uqd', p.astype(BF16), v3, preferred_element_type=F32) / l
        lse = mx * scale + jnp.log(l)
        for i, (r, sb) in enumerate(grp):
            rs = slice(sb * ATT_BLK, (sb + 1) * ATT_BLK)
            for h in range(B_HEADS):
                o_ref[0, r, rs, h * B_HDIM:(h + 1) * B_HDIM] = o[i * B_HEADS + h].astype(BF16)
            lse_ref[0, r, rs, :] = _cols_to_lanes([lse[i * B_HEADS + h] for h in range(B_HEADS)])


def _band_attention(q, k, v):
    b, d, ls, _ = q.shape
    tq = min(TOK_TILE, ls)
    nj = ls // tq
    ratio = tq // ATT_BLK
    nres = min(d, TOK_TILE // tq)
    cur = lambda bb, r, j: (bb, r, j, 0)
    prev = lambda bb, r, j: (bb, r, jnp.maximum(j * ratio - 1, 0), 0)
    return pl.pallas_call(
        _band_attn_kernel,
        grid=(b, d // nres, nj),
        in_specs=[pl.BlockSpec((1, nres, tq, B_WIDTH), cur),
                  pl.BlockSpec((1, nres, tq, B_WIDTH), cur),
                  pl.BlockSpec((1, nres, ATT_BLK, B_WIDTH), prev),
                  pl.BlockSpec((1, nres, tq, B_WIDTH), cur),
                  pl.BlockSpec((1, nres, ATT_BLK, B_WIDTH), prev)],
        out_specs=[pl.BlockSpec((1, nres, tq, B_WIDTH), cur),
                   pl.BlockSpec((1, nres, tq, LANES), cur)],
        out_shape=[jax.ShapeDtypeStruct((b, d, ls, B_WIDTH), BF16),
                   jax.ShapeDtypeStruct((b, d, ls, LANES), F32)],
        compiler_params=pltpu.CompilerParams(
            dimension_semantics=("arbitrary", "arbitrary", "arbitrary"), vmem_limit_bytes=VMEM_LIMIT),
        name="band_attention_d%d" % d,
    )(q, k, k, v, v)


def _unpermute(ref):
    d, rows, width = ref.shape[1:]
    if d == 1:
        return ref[0, 0].astype(F32)
    return jnp.swapaxes(ref[0], 0, 1).reshape(d * rows, width).astype(F32)


def _l1c_kernel(x_ref, o0_ref, o1_ref, o2_ref, l0_ref, l1_ref, l2_ref, zg_ref, qm_ref, zm_ref,
                mkv_ref, wout_ref, gpost_ref, y_ref):
    tt = x_ref.shape[1]
    o_refs = (o0_ref, o1_ref, o2_ref)
    l_refs = (l0_ref, l1_ref, l2_ref)
    outs, lses = [], []
    for g, (_, d) in enumerate(B_GROUPS):
        outs.append(_unpermute(o_refs[g]))
        lses.append(_unpermute(l_refs[g])[:, 0:B_HEADS])
    mx = jnp.maximum(jnp.maximum(lses[0], lses[1]), lses[2])
    es = [jnp.exp(l - mx) for l in lses]
    tot = es[0] + es[1] + es[2]
    ws = [e / tot for e in es]
    parts = []
    for h in range(B_HEADS):
        hs = slice(h * B_HDIM, (h + 1) * B_HDIM)
        acc = ws[0][:, h:h + 1] * outs[0][:, hs]
        acc = acc + ws[1][:, h:h + 1] * outs[1][:, hs]
        acc = acc + ws[2][:, h:h + 1] * outs[2][:, hs]
        parts.append(acc)
    ydil = jnp.concatenate(parts, axis=-1)
    ymix = (ydil * _silu(zg_ref[0].astype(F32))).astype(BF16)
    mk = mkv_ref[0, :, 0:M_WIDTH]
    mv = mkv_ref[0, :, M_WIDTH:2 * M_WIDTH]
    ym = (_mem_attention(qm_ref[0], mk, mv) * _silu(zm_ref[0].astype(F32))).astype(BF16)
    out = _dot(ymix, wout_ref[0:B_WIDTH, :]) + _dot(ym, wout_ref[B_WIDTH:B_WIDTH + M_WIDTH, :])
    y_ref[0] = x_ref[0] + _rms_scale(out) * gpost_ref[...]


def _layer1_out_prompt(x1, os_, ls_, zg, qm, zm, mkv_bf, w_out, g_post):
    b, s, _ = x1.shape
    tt = min(TOK_TILE, s)
    nt = s // tt
    tile = lambda bb, i: (bb, i, 0)
    perm = lambda bb, i: (bb, 0, i, 0)
    in_specs = [pl.BlockSpec((1, tt, D_MODEL), tile)]
    for width in (B_WIDTH, LANES):
        for (_, d) in B_GROUPS:
            in_specs.append(pl.BlockSpec((1, d, tt // d, width), perm))
    in_specs += [pl.BlockSpec((1, tt, B_WIDTH), tile)] * 3
    in_specs += [pl.BlockSpec((1, N_MEM, 2 * M_WIDTH), lambda bb, i: (bb, 0, 0)),
                 _const_spec(w_out.shape), _const_spec((1, D_MODEL))]
    return pl.pallas_call(
        _l1c_kernel,
        grid=(b, nt),
        in_specs=in_specs,
        out_specs=pl.BlockSpec((1, tt, D_MODEL), tile),
        out_shape=jax.ShapeDtypeStruct((b, s, D_MODEL), F32),
        compiler_params=pltpu.CompilerParams(
            dimension_semantics=("arbitrary", "arbitrary"), vmem_limit_bytes=VMEM_LIMIT),
        name="layer1_out_prompt",
    )(x1, *os_, *ls_, zg, qm, zm, mkv_bf, w_out, g_post)


def _rope_tables(pos):
    half = B_HDIM // 2
    inv = ROPE_THETA ** (-jnp.arange(half, dtype=F32) / half)
    ang = pos[:, None] * inv[None, :]
    cos = jnp.cos(ang)
    sin = jnp.sin(ang)
    return jnp.concatenate([cos, cos], axis=-1), jnp.concatenate([-sin, sin], axis=-1)


def _prompt_group(x_prompt, mem_prompt, p, sample=None):
    b, s, _ = x_prompt.shape
    memkv_f, memkv_b = _memkv(mem_prompt.reshape(b * N_MEM, D_MODEL), p['w_mkv'])
    depth = memkv_f.shape[0]
    memkv_b = memkv_b.reshape(depth, b, N_MEM, 2 * M_WIDTH)
    job0 = sample.mlstm_job() if sample is not None else None
    if job0 is not None and b * (s // min(L0_TILE, s)) + 2 < 2 * job0[0].shape[0]:
        job0 = None
    outs0 = _layer0_prompt_pipelined(
        x_prompt, p['g_pre'][0:1], p['w_in_a'][0], p['conv_w_a'][0], p['conv_b_a'], p['w_q_a'][0],
        p['w_k_a'][0], jnp.swapaxes(p['w_k_a'][0], 1, 2), p['w_v_a'][0], p['w_if_a'], p['b_if_a'],
        p['g_hn_a'], p['skip_a'], memkv_b[0], p['w_out_a'][0], p['g_post'][0:1], job0)
    x1, conv_p, c_p, n_p, m_pad = outs0[0:5]
    if sample is not None:
        sample.after_mlstm(outs0[5:10] if job0 is not None else _dec_mlstm(*sample.mlstm_job()))
    cos_t, sin_t = _rope_tables(jnp.arange(s, dtype=F32))
    job1 = sample.attn_job if sample is not None else None
    if job1 is not None and b * (s // min(TOK_TILE, s)) != job1[0].shape[0]:
        job1 = None
    outs = _layer1_proj_prompt(x1, p['g_kv'], p['g_pre'][1:2], p['w_kv_b'], p['w_in_b'][0], cos_t, sin_t,
                               job1)
    if sample is not None:
        sample.after_attn(tuple(outs[15:17]) if job1 is not None else _dec_attn(*sample.attn_job))
    qs, ks, vs = outs[0:3], outs[3:6], outs[6:9]
    zg, qm, zm = outs[9:12]
    wins = outs[12:15]
    os_, ls_ = [], []
    for g in range(N_GROUPS):
        o, l = _band_attention(qs[g], ks[g], vs[g])
        os_.append(o)
        ls_.append(l)
    y = _layer1_out_prompt(x1, os_, ls_, zg, qm, zm, memkv_b[1], p['w_out_b'][0], p['g_post'][1:2])
    m_p = m_pad[:, 0:A_HEADS, 0][None]
    wins = [w.reshape(b, w.shape[1], 2, B_HEADS, B_HDIM) for w in wins]
    memkv_p = memkv_f.reshape(depth, b, N_MEM, 2, M_HEADS, M_HDIM)
    return y, conv_p, c_p, n_p, m_p, wins, memkv_p


def _prep_params(g_pre, g_post, w_in_a, conv_w_a, conv_b_a, w_q_a, w_k_a, w_v_a, w_if_a, b_if_a,
                 g_hn_a, skip_a, w_out_a, g_kv, w_kv_b, w_in_b, w_out_b, w_mkv):
    wif = jnp.pad(w_if_a[0], ((0, 0), (0, LANES - 2 * A_HEADS))).astype(BF16)
    bif = jnp.pad(b_if_a[0], (0, LANES - 2 * A_HEADS))[None, :]
    return {
        'g_pre': g_pre, 'g_post': g_post,
        'w_in_a': w_in_a.astype(BF16), 'conv_w_a': conv_w_a, 'conv_b_a': conv_b_a,
        'w_q_a': w_q_a.astype(BF16), 'w_k_a': w_k_a.astype(BF16), 'w_v_a': w_v_a.astype(BF16),
        'w_if_a': wif, 'b_if_a': bif, 'g_hn_a': g_hn_a, 'skip_a': skip_a,
        'w_out_a': w_out_a.astype(BF16), 'g_kv': g_kv[None, :], 'w_kv_b': w_kv_b.astype(BF16),
        'w_in_b': w_in_b.astype(BF16), 'w_out_b': w_out_b.astype(BF16), 'w_mkv': w_mkv.astype(BF16),
    }


def _dec_l0_proj_kernel(x_ref, gpre_ref, win_ref, cst_ref, convw_ref, convb_ref, wq_ref, wk_ref, wv_ref,
                        wif_ref, bif_ref,
                        q_ref, k_ref, v_ref, gates_ref, xc_ref, opre_ref, zg_ref, qm_ref, zm_ref, cnew_ref):
    h = (_rms_scale(x_ref[:, 0, :]) * gpre_ref[...]).astype(BF16)
    u = _dot(h, win_ref[:, 0:A_INNER])
    opre_ref[...] = _dot(h, win_ref[:, A_INNER:2 * A_INNER])
    zg_ref[...] = _dot(h, win_ref[:, 2 * A_INNER:3 * A_INNER])
    qm_ref[...] = _rows_to_heads(_dot(h, win_ref[:, 3 * A_INNER:3 * A_INNER + M_WIDTH]))
    zm_ref[...] = _dot(h, win_ref[:, 3 * A_INNER + M_WIDTH:3 * A_INNER + 2 * M_WIDTH])
    cw = convw_ref[...]
    xc = convb_ref[...] + cst_ref[0, :, 0, :] * cw[0:1, :]
    xc = xc + cst_ref[0, :, 1, :] * cw[1:2, :]
    xc = xc + cst_ref[0, :, 2, :] * cw[2:3, :]
    xc = xc + u * cw[3:4, :]
    xc = _silu(xc)
    xc_ref[...] = xc
    cnew_ref[0, :, 0, :] = cst_ref[0, :, 1, :]
    cnew_ref[0, :, 1, :] = cst_ref[0, :, 2, :]
    cnew_ref[0, :, 2, :] = u
    qs, ks, vs, cat = [], [], [], []
    for hd in range(A_HEADS):
        sl = slice(hd * A_HDIM, (hd + 1) * A_HDIM)
        xh = xc[:, sl].astype(BF16)
        qh = _dot(xh, wq_ref[hd])
        kh = _dot(xh, wk_ref[hd]) * (A_HDIM ** -0.5)
        vh = _dot(u[:, sl].astype(BF16), wv_ref[hd])
        qs.append(qh)
        ks.append(kh)
        vs.append(vh)
        cat += [qh.astype(BF16), kh.astype(BF16), vh.astype(BF16)]
    q_ref[...] = jnp.concatenate(qs, axis=-1)
    k_ref[...] = jnp.concatenate(ks, axis=-1)
    v_ref[...] = jnp.concatenate(vs, axis=-1)
    gates_ref[...] = _dot(jnp.concatenate(cat, axis=-1), wif_ref[...]) + bif_ref[...]


def _whole(shape):
    nd = len(shape)
    return pl.BlockSpec(shape, lambda *_: (0,) * nd)


def _dec_l0_proj(x, g_pre, w_in, cst, conv_w, conv_b, wq, wk, wv, wif, bif):
    nb = x.shape[0]
    args = (x, g_pre, w_in, cst, conv_w, conv_b, wq, wk, wv, wif, bif)
    f = lambda *s: jax.ShapeDtypeStruct(s, F32)
    out_shape = [f(nb, A_INNER), f(nb, A_INNER), f(nb, A_INNER), f(nb, LANES), f(nb, A_INNER), f(nb, A_INNER),
                 f(nb, A_INNER), f(nb, M_HEADS, M_HDIM), f(nb, M_WIDTH), f(1, nb, CONV_W - 1, A_INNER)]
    return pl.pallas_call(
        _dec_l0_proj_kernel,
        grid=(1,),
        in_specs=[_whole(a.shape) for a in args],
        out_specs=[_whole(o.shape) for o in out_shape],
        out_shape=out_shape,
        compiler_params=pltpu.CompilerParams(dimension_semantics=("arbitrary",), vmem_limit_bytes=VMEM_LIMIT),
        name="dec_l0_proj",
    )(*args)


def _row_to_col(row, eye):
    return jnp.sum(jnp.where(eye, row, 0.0), axis=-1, keepdims=True)


def _col_to_row(colv, eye):
    return jnp.sum(jnp.where(eye, colv, 0.0), axis=0, keepdims=True)


def _dec_mem_attention(q, kv_ref_view):
    kk = kv_ref_view[:, 0]
    vv = kv_ref_view[:, 1]
    s = jnp.sum(kk * (q * (M_HDIM ** -0.5))[None], axis=-1, keepdims=True)
    mx = jnp.max(s, axis=0, keepdims=True)
    p = jnp.exp(s - mx)
    return jnp.sum(p * vv, axis=0) / jnp.sum(p, axis=0)


def _dec_mlstm_body(b, heads, with_mem, q_ref, k_ref, v_ref, gates_ref, m_ref, c_ref, n_ref, qm_ref, kv_ref,
                    hs_ref, c_out, n_out, m_out, ym_ref):
    rb = pl.ds(b, 1)
    g = gates_ref[rb, :]
    mrow = m_ref[0, rb, :]
    r = lax.broadcasted_iota(jnp.int32, (A_HDIM, A_HDIM), 0)
    c = lax.broadcasted_iota(jnp.int32, (A_HDIM, A_HDIM), 1)
    eye = r == c
    sl = {h: slice(h * A_HDIM, (h + 1) * A_HDIM) for h in heads}
    qh = {h: q_ref[rb, sl[h]] for h in heads}
    kh = {h: k_ref[rb, sl[h]] for h in heads}
    vh = {h: v_ref[rb, sl[h]] for h in heads}
    c_old = {h: c_ref[0, 0, h] for h in heads}
    n_old = {h: n_ref[0, 0, h:h + 1, :] for h in heads}
    li = {h: g[:, h:h + 1] for h in heads}
    lf = {h: _log_sigmoid(g[:, 4 + h:5 + h]) for h in heads}
    m_old = {h: mrow[:, h:h + 1] for h in heads}
    cq = {h: jnp.sum(c_old[h] * qh[h], axis=-1, keepdims=True) for h in heads}
    v_col = {h: _row_to_col(vh[h], eye) for h in heads}
    nq = {h: jnp.sum(n_old[h] * qh[h], axis=-1, keepdims=True) for h in heads}
    qk = {h: jnp.sum(qh[h] * kh[h], axis=-1, keepdims=True) for h in heads}
    inter = {h: lf[h] + m_old[h] for h in heads}
    m_new = {h: jnp.maximum(inter[h], li[h]) for h in heads}
    ws = {h: jnp.exp(li[h] - m_new[h]) for h in heads}
    dec = {h: jnp.exp(inter[h] - m_new[h]) for h in heads}
    sc = {h: qk[h] * ws[h] for h in heads}
    den = {h: sc[h] + dec[h] * nq[h] for h in heads}
    h_col = {h: (sc[h] * v_col[h] + dec[h] * cq[h]) / jnp.maximum(jnp.abs(den[h]), jnp.exp(-m_new[h]))
             for h in heads}
    for h in heads:
        c_out[0, 0, h] = dec[h] * c_old[h] + (ws[h] * v_col[h]) * kh[h]
        n_out[0, 0, h:h + 1, :] = dec[h] * n_old[h] + ws[h] * kh[h]
        m_out[0, h:h + 1, :] = jnp.broadcast_to(m_new[h], (1, LANES))
    for h in heads:
        hs_ref[0, :, sl[h]] = _col_to_row(h_col[h], eye)
    if with_mem:
        ym_ref[0] = _dec_mem_attention(qm_ref[0], kv_ref.at[0, 0])


def _dec_mlstm_kernel(*refs):
    _dec_mlstm_body(pl.program_id(0), range(A_HEADS), True, *refs)


def _dec_mlstm_specs(q, k, v, gates, m_in, state_c, state_n, qm3, cache_mem_kv, seq_index):
    nb = q.shape[0]

    def at(*tail, lead=()):
        return lambda *idx: lead + (seq_index(*idx),) + tail

    in_specs = [_whole(q.shape), _whole(k.shape), _whole(v.shape), _whole(gates.shape), _whole(m_in.shape),
                pl.BlockSpec((1, 1, A_HEADS, A_HDIM, A_HDIM), at(0, 0, 0, lead=(0,))),
                pl.BlockSpec((1, 1, A_HEADS, A_HDIM), at(0, 0, lead=(0,))),
                pl.BlockSpec((1, M_HEADS, M_HDIM), at(0, 0)),
                pl.BlockSpec((1, 1, N_MEM, 2, M_HEADS, M_HDIM), at(0, 0, 0, 0, lead=(0,)))]
    out_specs = [pl.BlockSpec((1, 1, A_INNER), at(0, 0)),
                 pl.BlockSpec((1, 1, A_HEADS, A_HDIM, A_HDIM), at(0, 0, 0, lead=(0,))),
                 pl.BlockSpec((1, 1, A_HEADS, A_HDIM), at(0, 0, lead=(0,))),
                 pl.BlockSpec((1, A_HEADS, LANES), at(0, 0)),
                 pl.BlockSpec((1, M_HEADS, M_HDIM), at(0, 0))]
    out_shapes = [jax.ShapeDtypeStruct((nb, 1, A_INNER), F32),
                  jax.ShapeDtypeStruct(state_c.shape, F32),
                  jax.ShapeDtypeStruct(state_n.shape, F32),
                  jax.ShapeDtypeStruct((nb, A_HEADS, LANES), F32),
                  jax.ShapeDtypeStruct((nb, M_HEADS, M_HDIM), F32)]
    return [q, k, v, gates, m_in, state_c, state_n, qm3, cache_mem_kv], in_specs, out_specs, out_shapes


def _dec_mlstm(*job):
    args, in_specs, out_specs, out_shapes = _dec_mlstm_specs(*job, seq_index=lambda b: b)
    return pl.pallas_call(
        _dec_mlstm_kernel,
        grid=(args[0].shape[0],),
        in_specs=in_specs,
        out_specs=out_specs,
        out_shape=out_shapes,
        compiler_params=pltpu.CompilerParams(dimension_semantics=("arbitrary",), vmem_limit_bytes=VMEM_LIMIT),
        name="dec_mlstm",
    )(*args)


def _dec_mid_kernel(hs_ref, opre_ref, xc_ref, zg_ref, ym_ref, zm_ref, x_ref, ghn_ref, skip_ref, wout_ref,
                    gpost_ref, gkv_ref, gpre_ref, wkv_ref, win_ref, cos_ref, sin_ref,
                    x1_ref, q_ref, k_ref, v_ref, zg1_ref, qm1_ref, zm1_ref, win0_ref, win1_ref, win2_ref):
    hh = _sigmoid(opre_ref[...]) * hs_ref[:, 0, :]
    parts = []
    for h in range(A_HEADS):
        v = hh[:, h * A_HDIM:(h + 1) * A_HDIM]
        mu = jnp.mean(v, axis=-1, keepdims=True)
        var = jnp.mean(jnp.square(v - mu), axis=-1, keepdims=True)
        parts.append((v - mu) * lax.rsqrt(var + EPS))
    y = jnp.concatenate(parts, axis=-1) * ghn_ref[...] + skip_ref[...] * xc_ref[...]
    ymix = (y * _silu(zg_ref[...])).astype(BF16)
    ym = (_heads_to_rows(ym_ref) * _silu(zm_ref[...])).astype(BF16)
    out = _dot(ymix, wout_ref[0:A_INNER, :]) + _dot(ym, wout_ref[A_INNER:A_INNER + M_WIDTH, :])
    x1 = x_ref[:, 0, :] + _rms_scale(out) * gpost_ref[...]
    x1_ref[...] = x1
    xn = _rms_scale(x1)
    hk = (xn * gkv_ref[...]).astype(BF16)
    hq = (xn * gpre_ref[...]).astype(BF16)
    cos = cos_ref[...]
    sin = sin_ref[...]
    ks, vs = [], []
    for g in range(N_GROUPS):
        ks.append(_rope_cols(_dot(hk, wkv_ref[:, g * 2 * B_WIDTH:g * 2 * B_WIDTH + B_WIDTH]), cos, sin))
        vs.append(_dot(hk, wkv_ref[:, g * 2 * B_WIDTH + B_WIDTH:(g + 1) * 2 * B_WIDTH]))
    k_ref[...] = _rows_to_heads(jnp.concatenate(ks, axis=-1))
    v_ref[...] = _rows_to_heads(jnp.concatenate(vs, axis=-1))
    for g, wref in enumerate((win0_ref, win1_ref, win2_ref)):
        wref[...] = _rows_to_kv_heads(ks[g], vs[g])
    qoff = N_GROUPS * B_WIDTH
    q_ref[...] = _rows_to_heads(_rope_cols(_dot(hq, win_ref[:, 0:qoff]), cos, sin))
    zg1_ref[...] = _dot(hq, win_ref[:, qoff:qoff + B_WIDTH])
    qm1_ref[...] = _rows_to_heads(_dot(hq, win_ref[:, qoff + B_WIDTH:qoff + B_WIDTH + M_WIDTH]))
    zm1_ref[...] = _dot(hq, win_ref[:, qoff + B_WIDTH + M_WIDTH:qoff + B_WIDTH + 2 * M_WIDTH])


def _dec_mid(hs, opre, xc, zg, ym, zm, x, ghn, skip, w_out, g_post, g_kv, g_pre, wkv, win, cos, sin):
    nb = x.shape[0]
    args = (hs, opre, xc, zg, ym, zm, x, ghn, skip, w_out, g_post, g_kv, g_pre, wkv, win, cos, sin)
    f = lambda *s: jax.ShapeDtypeStruct(s, F32)
    gh = N_GROUPS * B_HEADS
    out_shape = [f(nb, D_MODEL), f(nb, gh, B_HDIM), f(nb, gh, B_HDIM), f(nb, gh, B_HDIM),
                 f(nb, B_WIDTH), f(nb, M_HEADS, M_HDIM), f(nb, M_WIDTH)] + [f(nb, 2 * B_HEADS, B_HDIM)] * N_GROUPS
    return pl.pallas_call(
        _dec_mid_kernel,
        grid=(1,),
        in_specs=[_whole(a.shape) for a in args],
        out_specs=[_whole(o.shape) for o in out_shape],
        out_shape=out_shape,
        compiler_params=pltpu.CompilerParams(dimension_semantics=("arbitrary",), vmem_limit_bytes=VMEM_LIMIT),
        name="dec_mid",
    )(*args)


def _dec_attn_kernel(q_ref, kn_ref, vn_ref, w0_ref, w1_ref, w2_ref, qm_ref, kv_ref, ydil_ref, ym_ref):
    w_refs = (w0_ref, w1_ref, w2_ref)
    scale = B_HDIM ** -0.5
    groups = range(N_GROUPS)
    hsl = [slice(g * B_HEADS, (g + 1) * B_HEADS) for g in groups]
    q = [q_ref[0, hsl[g], :] * scale for g in groups]
    s_c = [jnp.sum(w_refs[g][0, :, 0] * q[g][None], axis=-1, keepdims=True) for g in groups]
    s_n = [jnp.sum(kn_ref[0, hsl[g], :] * q[g], axis=-1, keepdims=True) for g in groups]
    mxs = [jnp.maximum(jnp.max(s_c[g], axis=0), s_n[g]) for g in groups]
    p_c = [jnp.exp(s_c[g] - mxs[g][None]) for g in groups]
    p_n = [jnp.exp(s_n[g] - mxs[g]) for g in groups]
    ls = [jnp.sum(p_c[g], axis=0) + p_n[g] for g in groups]
    outs = [(jnp.sum(p_c[g] * w_refs[g][0, :, 1], axis=0) + p_n[g] * vn_ref[0, hsl[g], :]) / ls[g]
            for g in groups]
    lses = [mxs[g] + jnp.log(ls[g]) for g in groups]
    mx = jnp.maximum(jnp.maximum(lses[0], lses[1]), lses[2])
    es = [jnp.exp(l - mx) for l in lses]
    tot = es[0] + es[1] + es[2]
    ydil_ref[0] = (es[0] / tot) * outs[0] + (es[1] / tot) * outs[1] + (es[2] / tot) * outs[2]
    ym_ref[0] = _dec_mem_attention(qm_ref[0], kv_ref.at[0, 0])


def _dec_attn_specs(q4, kn4, vn4, cw0, cw1, cw2, qm3, cache_mem_kv, layer, seq_index):
    nb = q4.shape[0]
    rows = B_GROUPS[0][0]

    def at(*tail):
        return lambda *idx: (seq_index(*idx),) + tail

    win_specs = [pl.BlockSpec((1, rows, 2, B_HEADS, B_HDIM), at(0, 0, 0, 0)),
                 pl.BlockSpec((1, rows, None, 2, B_HEADS, B_HDIM), at(0, 0, 0, 0, 0)),
                 pl.BlockSpec((1, rows, None, 2, B_HEADS, B_HDIM), at(0, 0, 0, 0, 0))]
    in_specs = [pl.BlockSpec((1, N_GROUPS * B_HEADS, B_HDIM), at(0, 0))] * 3 + win_specs + [
        pl.BlockSpec((1, M_HEADS, M_HDIM), at(0, 0)),
        pl.BlockSpec((1, 1, N_MEM, 2, M_HEADS, M_HDIM), lambda *idx: (layer, seq_index(*idx), 0, 0, 0, 0))]
    out_specs = [pl.BlockSpec((1, B_HEADS, B_HDIM), at(0, 0)), pl.BlockSpec((1, M_HEADS, M_HDIM), at(0, 0))]
    out_shapes = [jax.ShapeDtypeStruct((nb, B_HEADS, B_HDIM), F32),
                  jax.ShapeDtypeStruct((nb, M_HEADS, M_HDIM), F32)]
    return [q4, kn4, vn4, cw0, cw1, cw2, qm3, cache_mem_kv], in_specs, out_specs, out_shapes


def _dec_attn(q4, kn4, vn4, cw0, cw1, cw2, qm3, cache_mem_kv, layer):
    args, in_specs, out_specs, out_shapes = _dec_attn_specs(
        q4, kn4, vn4, cw0, cw1, cw2, qm3, cache_mem_kv, layer, seq_index=lambda b: b)
    return pl.pallas_call(
        _dec_attn_kernel,
        grid=(q4.shape[0],),
        in_specs=in_specs,
        out_specs=out_specs,
        out_shape=out_shapes,
        compiler_params=pltpu.CompilerParams(dimension_semantics=("arbitrary",), vmem_limit_bytes=VMEM_LIMIT),
        name="dec_attn",
    )(*args)


def _dec_out_kernel(ydil_ref, zg_ref, ym_ref, zm_ref, x_ref, wout_ref, gpost_ref, y_ref):
    ymix = (_heads_to_rows(ydil_ref) * _silu(zg_ref[...])).astype(BF16)
    ym = (_heads_to_rows(ym_ref) * _silu(zm_ref[...])).astype(BF16)
    out = _dot(ymix, wout_ref[0:B_WIDTH, :]) + _dot(ym, wout_ref[B_WIDTH:B_WIDTH + M_WIDTH, :])
    y_ref[:, 0, :] = x_ref[...] + _rms_scale(out) * gpost_ref[...]


def _dec_out(ydil, zg, ym, zm, x1, w_out, g_post):
    args = (ydil, zg, ym, zm, x1, w_out, g_post)
    out_shape = (x1.shape[0], 1, x1.shape[1])
    return pl.pallas_call(
        _dec_out_kernel,
        grid=(1,),
        in_specs=[_whole(a.shape) for a in args],
        out_specs=_whole(out_shape),
        out_shape=jax.ShapeDtypeStruct(out_shape, F32),
        compiler_params=pltpu.CompilerParams(dimension_semantics=("arbitrary",), vmem_limit_bytes=VMEM_LIMIT),
        name="dec_out",
    )(*args)


class _SampleGroup:
    def __init__(self, x_sample, state_conv, state_c, state_n, state_m, cache_wins, cache_mem_kv, p):
        self.p = p
        self.nb = nb = x_sample.shape[0]
        self.cache_wins = cache_wins
        self.cache_mem_kv = cache_mem_kv
        self.x = x_sample
        q, k, v, gates, self.xc, self.opre, self.zg, qm3, self.zm, self.conv_s = _dec_l0_proj(
            x_sample, p['g_pre'][0:1], p['w_in_a'][0], state_conv, p['conv_w_a'][0], p['conv_b_a'],
            p['w_q_a'][0], p['w_k_a'][0], p['w_v_a'][0], p['w_if_a'], p['b_if_a'])
        self._mlstm_job = (q, k, v, gates, state_m, state_c, state_n, qm3, cache_mem_kv)
        self.attn_job = None

    def mlstm_job(self):
        return self._mlstm_job

    def after_mlstm(self, res):
        p, nb = self.p, self.nb
        hs, self.c_s, self.n_s, m_rows, ym0 = res
        self.m_s = m_rows[:, :, 0][None]
        cos, sin = _rope_tables(PAST_LEN + jnp.arange(1, dtype=F32))
        self.x1, qd, kn, vn, self.zg1, qm1, self.zm1, w0, w1, w2 = _dec_mid(
            hs, self.opre, self.xc, self.zg, ym0, self.zm, self.x,
            p['g_hn_a'], p['skip_a'], p['w_out_a'][0], p['g_post'][0:1], p['g_kv'], p['g_pre'][1:2],
            p['w_kv_b'], p['w_in_b'][0], cos, sin)
        cws = [self.cache_wins[0]]
        for g in (1, 2):
            w, d = B_GROUPS[g]
            cws.append(self.cache_wins[g].reshape(nb, w // d, d, 2, B_HEADS, B_HDIM))
        self.attn_job = (qd, kn, vn, cws[0], cws[1], cws[2], qm1, self.cache_mem_kv, 1)
        self.wins_s = [w.reshape(nb, 1, 2, B_HEADS, B_HDIM) for w in (w0, w1, w2)]

    def after_attn(self, res):
        p = self.p
        ydil, ym1 = res
        self.y = _dec_out(ydil, self.zg1, ym1, self.zm1, self.x1, p['w_out_b'][0], p['g_post'][1:2])

    def outputs(self):
        return self.y, self.conv_s, self.c_s, self.n_s, self.m_s, self.wins_s


def _sample_group(x_sample, state_conv, state_c, state_n, state_m, cache_wins, cache_mem_kv, p):
    sg = _SampleGroup(x_sample, state_conv, state_c, state_n, state_m, cache_wins, cache_mem_kv, p)
    sg.after_mlstm(_dec_mlstm(*sg.mlstm_job()))
    sg.after_attn(_dec_attn(*sg.attn_job))
    return sg.outputs()


def kernel(x_prompt, x_sample, mem_prompt, state_conv, state_C, state_n, state_m, cache_win0, cache_win1,
           cache_win2, cache_mem_kv, g_pre, g_post, w_in_a, conv_w_a, conv_b_a, w_q_a, w_k_a, w_v_a, w_if_a,
           b_if_a, g_hn_a, skip_a, w_out_a, g_kv, w_kv_b, w_in_b, w_out_b, w_mkv):
    p = _prep_params(g_pre, g_post, w_in_a, conv_w_a, conv_b_a, w_q_a, w_k_a, w_v_a, w_if_a, b_if_a,
                     g_hn_a, skip_a, w_out_a, g_kv, w_kv_b, w_in_b, w_out_b, w_mkv)
    sample = _SampleGroup(x_sample, state_conv, state_C, state_n, state_m,
                          (cache_win0, cache_win1, cache_win2), cache_mem_kv, p)
    y_p, conv_p, c_p, n_p, m_p, wins_p, memkv_p = _prompt_group(x_prompt, mem_prompt, p, sample)
    y_s, conv_s, c_s, n_s, m_s, wins_s = sample.outputs()
    return (y_p, y_s, conv_p, c_p, n_p, m_p, wins_p[0], wins_p[1], wins_p[2], memkv_p,
            conv_s, c_s, n_s, m_s, wins_s[0], wins_s[1], wins_s[2])
```

```python
import functools

import jax
import jax.numpy as jnp
from jax import lax
from jax.experimental import pallas as pl
from jax.experimental.pallas import tpu as pltpu

F32 = jnp.float32
BF16 = jnp.bfloat16

D_MODEL = 1024
A_HEADS = 4
A_HDIM = 256
A_INNER = 1024
CONV_W = 4
A_CHUNK = 128
B_GROUPS = ((128, 1), (512, 4), (2048, 16))
N_GROUPS = 3
B_HEADS = 4
B_HDIM = 128
B_WIDTH = 512
N_MEM = 256
M_HEADS = 4
M_HDIM = 128
M_WIDTH = 512
ROPE_THETA = 10000.0
EPS = 1e-6
PAST_LEN = 8192

LANES = 128
TOK_TILE = 512
L0_TILE = 256
ATT_BLK = 128
ATT_TILE = 1024
ATT_GROUP = 2
VMEM_LIMIT = 56 * 1024 * 1024

NT_DIMS = (((1,), (1,)), ((), ()))
LOG2E = 1.4426950408889634


def _dot(a, b):
    return jnp.dot(a, b, preferred_element_type=F32)


def _dot_nt(a, b):
    return lax.dot_general(a, b, NT_DIMS, preferred_element_type=F32)


def _sigmoid(x):
    return 1.0 / (1.0 + jnp.exp(-x))


def _silu(x):
    return x * _sigmoid(x)


def _log_sigmoid(x):
    return jnp.minimum(x, 0.0) - jnp.log(1.0 + jnp.exp(-jnp.abs(x)))


def _rms_scale(x):
    return x * lax.rsqrt(jnp.mean(x * x, axis=-1, keepdims=True) + EPS)


def _const_spec(shape):
    nd = len(shape)
    return pl.BlockSpec(shape, lambda *_: (0,) * nd, pipeline_mode=pl.Buffered(1))


def _mem_attention(qm, mk, mv):
    heads = range(M_HEADS)
    sl = [slice(h * M_HDIM, (h + 1) * M_HDIM) for h in heads]
    s = [_dot_nt(qm[:, sl[h]], mk[:, sl[h]]) * (M_HDIM ** -0.5) for h in heads]
    mx = [jnp.max(s[h], axis=-1, keepdims=True) for h in heads]
    p = [jnp.exp(s[h] - mx[h]) for h in heads]
    l = [jnp.sum(p[h], axis=-1, keepdims=True) for h in heads]
    outs = [_dot((p[h] / l[h]).astype(BF16), mv[:, sl[h]]) for h in heads]
    return jnp.concatenate(outs, axis=-1)


def _rows_to_heads(x):
    return jnp.swapaxes(jnp.stack([x[:, h * LANES:(h + 1) * LANES] for h in range(x.shape[1] // LANES)]), 0, 1)


def _heads_to_rows(ref):
    return jnp.concatenate([ref[:, h, :] for h in range(ref.shape[1])], axis=-1)


def _rows_to_kv_heads(k, v):
    pieces = [a[:, h * LANES:(h + 1) * LANES] for a in (k, v) for h in range(a.shape[1] // LANES)]
    return jnp.swapaxes(jnp.stack(pieces), 0, 1)


def _memkv_kernel(m_ref, w_ref, o_ref, ob_ref):
    r = _dot(m_ref[...].astype(BF16), w_ref[0])
    o_ref[0] = _rows_to_kv_heads(r[:, 0:M_WIDTH], r[:, M_WIDTH:2 * M_WIDTH])
    ob_ref[0] = r.astype(BF16)


def _memkv(mem2d, w_bf):
    nm = mem2d.shape[0]
    nl = w_bf.shape[0]
    tm = min(512, nm)
    return pl.pallas_call(
        _memkv_kernel,
        grid=(nl, nm // tm),
        in_specs=[pl.BlockSpec((tm, D_MODEL), lambda l, i: (i, 0)),
                  pl.BlockSpec((1, D_MODEL, 2 * M_WIDTH), lambda l, i: (l, 0, 0))],
        out_specs=[pl.BlockSpec((1, tm, 2 * M_HEADS, M_HDIM), lambda l, i: (l, i, 0, 0)),
                   pl.BlockSpec((1, tm, 2 * M_WIDTH), lambda l, i: (l, i, 0))],
        out_shape=[jax.ShapeDtypeStruct((nl, nm, 2 * M_HEADS, M_HDIM), F32),
                   jax.ShapeDtypeStruct((nl, nm, 2 * M_WIDTH), BF16)],
        compiler_params=pltpu.CompilerParams(dimension_semantics=("arbitrary", "arbitrary")),
        name="memkv",
    )(mem2d, w_bf)


def _mlstm_chunk(rs, g, qkv_v, kt_v, c_s, n_s, m_s, causal, hs):
    ls = _log_sigmoid(g)
    tok = lax.broadcasted_iota(jnp.int32, (A_CHUNK, LANES), 0)
    bc = ls
    shift = 1
    while shift < A_CHUNK:
        bc = bc + jnp.where(tok >= shift, pltpu.roll(bc, shift, 0), 0.0)
        shift *= 2
    lane = lax.broadcasted_iota(jnp.int32, (A_CHUNK, LANES), 1)
    xt = jnp.where(lane < A_HEADS, g, bc).T
    yield
    heads = range(A_HEADS)
    b_col = [bc[:, 4 + h:5 + h] for h in heads]
    b_row = [xt[4 + h:5 + h, :] for h in heads]
    li_row = [xt[h:h + 1, :] for h in heads]
    li_col = [g[:, h:h + 1] for h in heads]
    m_old = [m_s[h:h + 1, 0:1] for h in heads]
    b_last = [bc[A_CHUNK - 1:A_CHUNK, 4 + h:5 + h] for h in heads]
    qh = [qkv_v[rs, h * 3 * A_HDIM:h * 3 * A_HDIM + A_HDIM] for h in heads]
    kh = [qkv_v[rs, h * 3 * A_HDIM + A_HDIM:h * 3 * A_HDIM + 2 * A_HDIM] for h in heads]
    vh = [qkv_v[rs, h * 3 * A_HDIM + 2 * A_HDIM:(h + 1) * 3 * A_HDIM] for h in heads]
    kt = [kt_v[h] for h in heads]
    c_old = [c_s[h] for h in heads]
    n_old = [n_s[h:h + 1, :] for h in heads]
    qk = [_dot_nt(qh[h], kh[h]) for h in heads]
    qc = [_dot(qh[h], c_old[h].astype(BF16)) for h in heads]
    dm = [jnp.where(causal, b_col[h] - b_row[h] + li_row[h], -jnp.inf) for h in heads]
    inter = [b_col[h] + m_old[h] for h in heads]
    m_row = [jnp.maximum(inter[h], jnp.max(dm[h], axis=-1, keepdims=True)) for h in heads]
    g_max = [jnp.max(b_last[h] - b_row[h] + li_row[h], axis=-1, keepdims=True) for h in heads]
    m_new = [jnp.maximum(b_last[h] + m_old[h], g_max[h]) for h in heads]
    yield
    sc = [qk[h] * jnp.exp(dm[h] - m_row[h]) for h in heads]
    dec = [jnp.exp(inter[h] - m_row[h]) for h in heads]
    ws_col = [jnp.exp(b_last[h] - b_col[h] + li_col[h] - m_new[h]) for h in heads]
    dc = [jnp.exp(b_last[h] + m_old[h] - m_new[h]) for h in heads]
    yield
    sv = [_dot(sc[h].astype(BF16), vh[h]) for h in heads]
    wv = [(ws_col[h] * vh[h].astype(F32)).astype(BF16) for h in heads]
    upd = [_dot(kt[h], wv[h]) for h in heads]
    yield
    for h in heads:
        den = (jnp.sum(sc[h], axis=-1, keepdims=True)
               + dec[h] * jnp.sum(qh[h].astype(F32) * n_old[h], axis=-1, keepdims=True))
        num = sv[h] + dec[h] * qc[h]
        hs.append(num / jnp.maximum(jnp.abs(den), jnp.exp(-m_row[h])))
    yield
    for h in heads:
        c_s[h] = dc[h] * c_old[h] + upd[h]
        n_s[h:h + 1, :] = dc[h] * n_old[h] + jnp.sum(ws_col[h] * kh[h].astype(F32), axis=0, keepdims=True)
        m_s[h:h + 1, :] = jnp.broadcast_to(m_new[h], (1, LANES))
    yield


def _l0p_kernel(nt, n_dec_seq, *refs):
    n_dec_in, n_dec_out = (9, 5) if n_dec_seq else (0, 0)
    (x_ref, xp_ref, gpre_ref, win_ref, convw_ref, convb_ref, wq_ref, wk_ref, wkt_ref, wv_ref,
     wif_ref, bif_ref, ghn_ref, skip_ref, mkv_ref, wout_ref, gpost_ref) = refs[0:17]
    dec_in = refs[17:17 + n_dec_in]
    x1_ref, conv_out, c_out, n_out, m_out = refs[17 + n_dec_in:22 + n_dec_in]
    dec_out = refs[22 + n_dec_in:22 + n_dec_in + n_dec_out]
    (h_s, u_s, ymix_s, xc_s, opre_s, zg_s, qm_s, zm_s, qkv_s, kt_s, gates_s,
     c_s, n_s, m_s) = refs[22 + n_dec_in + n_dec_out:]
    tt = x_ref.shape[1]
    nsub = tt // A_CHUNK
    t = pl.program_id(0)
    parity = lax.rem(t + 1, 2)
    pos1 = lax.rem(t + nt - 1, nt)
    pos2 = lax.rem(t + 2 * nt - 2, nt)

    @pl.when(t == 0)
    def _():
        h_s[...] = jnp.zeros(h_s.shape, BF16)
        u_s[...] = jnp.zeros(u_s.shape, F32)
        xc_s[0] = jnp.zeros(xc_s.shape[1:], F32)
        opre_s[0] = jnp.zeros(opre_s.shape[1:], F32)
        zg_s[0] = jnp.zeros(zg_s.shape[1:], F32)
        qm_s[0] = jnp.zeros(qm_s.shape[1:], BF16)
        zm_s[0] = jnp.zeros(zm_s.shape[1:], F32)
        qkv_s[0] = jnp.zeros(qkv_s.shape[1:], BF16)
        kt_s[0] = jnp.zeros(kt_s.shape[1:], BF16)
        gates_s[0] = jnp.zeros(gates_s.shape[1:], F32)

    @pl.when(pos1 == 0)
    def _():
        u_s[0:8, :] = jnp.zeros((8, A_INNER), F32)

    @pl.when(pos2 == 0)
    def _():
        c_s[...] = jnp.zeros(c_s.shape, F32)
        n_s[...] = jnp.zeros(n_s.shape, F32)
        m_s[...] = jnp.zeros(m_s.shape, F32)

    row = lax.broadcasted_iota(jnp.int32, (A_CHUNK, A_CHUNK), 0)
    col = lax.broadcasted_iota(jnp.int32, (A_CHUNK, A_CHUNK), 1)
    causal = col <= row
    ghn = ghn_ref[...]
    skp = skip_ref[...]
    mk = mkv_ref[0, :, 0:M_WIDTH]
    mv = mkv_ref[0, :, M_WIDTH:2 * M_WIDTH]

    def stage2(pslot):
        ym = _mem_attention(qm_s[pslot], mk, mv) * _silu(zm_s[pslot])
        ymix_s[:, A_INNER:A_INNER + M_WIDTH] = ym.astype(BF16)
        yield
        for c in range(nsub):
            rs = slice(c * A_CHUNK, (c + 1) * A_CHUNK)
            hs = []
            yield from _mlstm_chunk(rs, gates_s[pslot, rs, :], qkv_s.at[pslot], kt_s.at[pslot, :, c],
                                    c_s, n_s, m_s, causal, hs)
            parts = []
            for h in range(A_HEADS):
                v = _sigmoid(opre_s[pslot, rs, h * A_HDIM:(h + 1) * A_HDIM]) * hs[h]
                mu = jnp.mean(v, axis=-1, keepdims=True)
                var = jnp.mean(jnp.square(v - mu), axis=-1, keepdims=True)
                parts.append((v - mu) * lax.rsqrt(var + EPS))
            hn = jnp.concatenate(parts, axis=-1) * ghn
            y = hn + skp * xc_s[pslot, rs, :]
            ymix_s[rs, 0:A_INNER] = (y * _silu(zg_s[pslot, rs, :])).astype(BF16)
            yield

    def stage1(slot):
        hb = h_s[...]
        u_s[8:8 + tt, :] = _dot(hb, win_ref[:, 0:A_INNER])
        yield
        cw = convw_ref[...]
        cb = convb_ref[...]
        for c in range(nsub):
            r0 = c * A_CHUNK
            blk = u_s[r0:r0 + A_CHUNK + 8, :]
            xc = cb + pltpu.roll(blk, 3, 0)[8:, :] * cw[0:1, :]
            xc = xc + pltpu.roll(blk, 2, 0)[8:, :] * cw[1:2, :]
            xc = xc + pltpu.roll(blk, 1, 0)[8:, :] * cw[2:3, :]
            xc = xc + blk[8:, :] * cw[3:4, :]
            xc_s[slot, r0:r0 + A_CHUNK, :] = _silu(xc)
        opre_s[slot] = _dot(hb, win_ref[:, A_INNER:2 * A_INNER])
        yield
        zg_s[slot] = _dot(hb, win_ref[:, 2 * A_INNER:3 * A_INNER])
        yield
        qm_s[slot] = _dot(hb, win_ref[:, 3 * A_INNER:3 * A_INNER + M_WIDTH]).astype(BF16)
        zm_s[slot] = _dot(hb, win_ref[:, 3 * A_INNER + M_WIDTH:3 * A_INNER + 2 * M_WIDTH])
        yield
        for h in range(A_HEADS):
            sl = slice(h * A_HDIM, (h + 1) * A_HDIM)
            xh = xc_s[slot, :, sl].astype(BF16)
            uh = u_s[8:8 + tt, sl].astype(BF16)
            base = h * 3 * A_HDIM
            qkv_s[slot, :, base:base + A_HDIM] = _dot(xh, wq_ref[h]).astype(BF16)
            qkv_s[slot, :, base + A_HDIM:base + 2 * A_HDIM] = (
                _dot(xh, wk_ref[h]) * (A_HDIM ** -0.5)).astype(BF16)
            qkv_s[slot, :, base + 2 * A_HDIM:base + 3 * A_HDIM] = _dot(uh, wv_ref[h]).astype(BF16)
            kt = (_dot_nt(wkt_ref[h], xh) * (A_HDIM ** -0.5)).astype(BF16)
            for c in range(nsub):
                kt_s[slot, h, c] = kt[:, c * A_CHUNK:(c + 1) * A_CHUNK]
            yield
        gates_s[slot] = _dot(qkv_s[slot], wif_ref[...]) + bif_ref[...]
        yield

    def step(slot):
        if n_dec_seq:
            first = slot == 1
            heads = (0, 1) if first else (2, 3)
            _dec_mlstm_body(jnp.minimum(t // 2, n_dec_seq - 1), heads, first, *dec_in, *dec_out)
        pending = [stage1(slot), stage2(1 - slot)]
        while pending:
            for gen in list(pending):
                try:
                    next(gen)
                except StopIteration:
                    pending.remove(gen)
        h_next = (_rms_scale(x_ref[0]) * gpre_ref[...]).astype(BF16)
        out = _dot(ymix_s[...], wout_ref[...])
        h_s[...] = h_next
        x1_ref[0] = xp_ref[0] + _rms_scale(out) * gpost_ref[...]

    for s in range(2):
        pl.when(parity == s)(functools.partial(step, s))

    @pl.when(jnp.logical_and(pos1 == nt - 1, t > 0))
    def _():
        conv_out[0, 0] = u_s[tt + 5:tt + 8, :]

    u_s[0:8, :] = u_s[tt:tt + 8, :]

    @pl.when(jnp.logical_and(pos2 == nt - 1, t > 1))
    def _():
        for h in range(A_HEADS):
            c_out[0, 0, h] = c_s[h].T
        n_out[0, 0] = n_s[0:A_HEADS, :]
        m_out[0] = m_s[...]


def _layer0_prompt_pipelined(x, g_pre, w_in, conv_w, conv_b, wq, wk, wkt, wv, wif, bif, ghn, skip, mkv_bf,
                             w_out, g_post, dec_job=None):
    b, s, _ = x.shape
    tt = min(L0_TILE, s)
    nt = s // tt
    ntiles = b * nt
    a_in = w_in.shape[1]
    nsub = tt // A_CHUNK

    def cur(t):
        t1 = jnp.minimum(t, ntiles - 1)
        return (t1 // nt, t1 % nt, 0)

    def prev(t):
        t2 = jnp.maximum(t - 2, 0)
        return (t2 // nt, t2 % nt, 0)

    def prev_b(t):
        return jnp.maximum(t - 2, 0) // nt

    in_specs = [
        pl.BlockSpec((1, tt, D_MODEL), cur),
        pl.BlockSpec((1, tt, D_MODEL), prev),
        _const_spec((1, D_MODEL)),
        _const_spec((D_MODEL, a_in)),
        _const_spec((CONV_W, A_INNER)),
        _const_spec((1, A_INNER)),
        _const_spec((A_HEADS, A_HDIM, A_HDIM)),
        _const_spec((A_HEADS, A_HDIM, A_HDIM)),
        _const_spec((A_HEADS, A_HDIM, A_HDIM)),
        _const_spec((A_HEADS, A_HDIM, A_HDIM)),
        _const_spec((3 * A_INNER, LANES)),
        _const_spec((1, LANES)),
        _const_spec((1, A_INNER)),
        _const_spec((1, A_INNER)),
        pl.BlockSpec((1, N_MEM, 2 * M_WIDTH), lambda t: (prev_b(t), 0, 0)),
        _const_spec((A_INNER + M_WIDTH, D_MODEL)),
        _const_spec((1, D_MODEL)),
    ]
    out_specs = [
        pl.BlockSpec((1, tt, D_MODEL), prev),
        pl.BlockSpec((1, 1, CONV_W - 1, A_INNER), lambda t: (0, prev_b(t), 0, 0)),
        pl.BlockSpec((1, 1, A_HEADS, A_HDIM, A_HDIM), lambda t: (0, prev_b(t), 0, 0, 0)),
        pl.BlockSpec((1, 1, A_HEADS, A_HDIM), lambda t: (0, prev_b(t), 0, 0)),
        pl.BlockSpec((1, 8, LANES), lambda t: (prev_b(t), 0, 0)),
    ]
    out_shape = [
        jax.ShapeDtypeStruct((b, s, D_MODEL), F32),
        jax.ShapeDtypeStruct((1, b, CONV_W - 1, A_INNER), F32),
        jax.ShapeDtypeStruct((1, b, A_HEADS, A_HDIM, A_HDIM), F32),
        jax.ShapeDtypeStruct((1, b, A_HEADS, A_HDIM), F32),
        jax.ShapeDtypeStruct((b, 8, LANES), F32),
    ]
    scratch = [
        pltpu.VMEM((tt, D_MODEL), BF16),
        pltpu.VMEM((tt + 8, A_INNER), F32),
        pltpu.VMEM((tt, A_INNER + M_WIDTH), BF16),
        pltpu.VMEM((2, tt, A_INNER), F32),
        pltpu.VMEM((2, tt, A_INNER), F32),
        pltpu.VMEM((2, tt, A_INNER), F32),
        pltpu.VMEM((2, tt, M_WIDTH), BF16),
        pltpu.VMEM((2, tt, M_WIDTH), F32),
        pltpu.VMEM((2, tt, 3 * A_INNER), BF16),
        pltpu.VMEM((2, A_HEADS, nsub, A_HDIM, A_CHUNK), BF16),
        pltpu.VMEM((2, tt, LANES), F32),
        pltpu.VMEM((A_HEADS, A_HDIM, A_HDIM), F32),
        pltpu.VMEM((8, A_HDIM), F32),
        pltpu.VMEM((8, LANES), F32),
    ]
    dec_args, dec_specs, dec_out_specs, dec_out_shapes, n_dec_seq = [], [], [], [], 0
    if dec_job is not None:
        n_dec_seq = dec_job[0].shape[0]
        dec_args, dec_specs, dec_out_specs, dec_out_shapes = _dec_mlstm_specs(
            *dec_job, seq_index=lambda t: jnp.minimum(t // 2, n_dec_seq - 1))
    return pl.pallas_call(
        functools.partial(_l0p_kernel, nt, n_dec_seq),
        grid=(ntiles + 2,),
        in_specs=in_specs + dec_specs,
        out_specs=out_specs + dec_out_specs,
        out_shape=out_shape + dec_out_shapes,
        scratch_shapes=scratch,
        compiler_params=pltpu.CompilerParams(
            dimension_semantics=("arbitrary",), vmem_limit_bytes=VMEM_LIMIT),
        name="layer0_prompt",
    )(x, x, g_pre, w_in, conv_w, conv_b, wq, wk, wkt, wv, wif, bif, ghn, skip, mkv_bf, w_out, g_post,
      *dec_args)


def _rope_cols(x, cos, sin_signed):
    outs = []
    for cblk in range(x.shape[1] // B_HDIM):
        xb = x[:, cblk * B_HDIM:(cblk + 1) * B_HDIM]
        outs.append(xb * cos + pltpu.roll(xb, B_HDIM // 2, 1) * sin_signed)
    return jnp.concatenate(outs, axis=-1)


def _l1a_kernel(n_dec_in, *refs):
    x_ref, gkv_ref, gpre_ref, wkv_ref, win_ref, cos_ref, sin_ref = refs[0:7]
    dec_in = refs[7:7 + n_dec_in]
    (q0_ref, q1_ref, q2_ref, k0_ref, k1_ref, k2_ref, v0_ref, v1_ref, v2_ref,
     zg_ref, qm_ref, zm_ref, w0_ref, w1_ref, w2_ref) = refs[7 + n_dec_in:22 + n_dec_in]
    dec_out = refs[22 + n_dec_in:]
    tt = x_ref.shape[1]
    xn = _rms_scale(x_ref[0])
    hk = (xn * gkv_ref[...]).astype(BF16)
    hq = (xn * gpre_ref[...]).astype(BF16)
    cos = cos_ref[...]
    sin = sin_ref[...]
    q_refs = (q0_ref, q1_ref, q2_ref)
    k_refs = (k0_ref, k1_ref, k2_ref)
    v_refs = (v0_ref, v1_ref, v2_ref)
    w_refs = (w0_ref, w1_ref, w2_ref)
    for g in (2, 1, 0):
        d = B_GROUPS[g][1]
        kf = _rope_cols(_dot(hk, wkv_ref[:, g * 2 * B_WIDTH:g * 2 * B_WIDTH + B_WIDTH]), cos, sin)
        vf = _dot(hk, wkv_ref[:, g * 2 * B_WIDTH + B_WIDTH:(g + 1) * 2 * B_WIDTH])
        qf = _rope_cols(_dot(hq, win_ref[:, g * B_WIDTH:(g + 1) * B_WIDTH]), cos, sin)
        wr = w_refs[g]
        wrows = wr.shape[1]
        wr[0] = _rows_to_kv_heads(kf[tt - wrows:, :], vf[tt - wrows:, :])
        for val, ref in ((qf.astype(BF16), q_refs[g]), (kf.astype(BF16), k_refs[g]), (vf.astype(BF16), v_refs[g])):
            if d == 1:
                ref[0, 0] = val
            else:
                ref[0] = jnp.swapaxes(val.reshape(tt // d, d, val.shape[1]), 0, 1)
        if g == 2 and n_dec_in:
            _dec_attn_kernel(*dec_in, *dec_out)
    qoff = N_GROUPS * B_WIDTH
    zg_ref[0] = _dot(hq, win_ref[:, qoff:qoff + B_WIDTH]).astype(BF16)
    qm_ref[0] = _dot(hq, win_ref[:, qoff + B_WIDTH:qoff + B_WIDTH + M_WIDTH]).astype(BF16)
    zm_ref[0] = _dot(hq, win_ref[:, qoff + B_WIDTH + M_WIDTH:qoff + B_WIDTH + 2 * M_WIDTH]).astype(BF16)


def _layer1_proj_prompt(x1, g_kv, g_pre, wkv, win, cos_t, sin_t, dec_job=None):
    b, s, _ = x1.shape
    tt = min(TOK_TILE, s)
    nt = s // tt
    tile = lambda bb, i: (bb, i, 0)
    in_specs = [
        pl.BlockSpec((1, tt, D_MODEL), tile),
        _const_spec((1, D_MODEL)),
        _const_spec((1, D_MODEL)),
        _const_spec(wkv.shape),
        _const_spec(win.shape),
        pl.BlockSpec((tt, B_HDIM), lambda bb, i: (i, 0)),
        pl.BlockSpec((tt, B_HDIM), lambda bb, i: (i, 0)),
    ]
    qkv_specs, qkv_shapes = [], []
    for _ in range(3):
        for (_, d) in B_GROUPS:
            qkv_specs.append(pl.BlockSpec((1, d, tt // d, B_WIDTH), lambda bb, i: (bb, 0, i, 0)))
            qkv_shapes.append(jax.ShapeDtypeStruct((b, d, s // d, B_WIDTH), BF16))
    gate_specs = [pl.BlockSpec((1, tt, B_WIDTH), tile)] * 3
    gate_shapes = [jax.ShapeDtypeStruct((b, s, B_WIDTH), BF16)] * 3
    win_specs, win_shapes = [], []
    for (w, _) in B_GROUPS:
        wr = min(w, s)
        rows = min(wr, tt)
        nblk = wr // rows
        win_specs.append(pl.BlockSpec(
            (1, rows, 2 * B_HEADS, B_HDIM),
            functools.partial(lambda bb, i, nb: (bb, jnp.maximum(i - (nt - nb), 0), 0, 0), nb=nblk)))
        win_shapes.append(jax.ShapeDtypeStruct((b, wr, 2 * B_HEADS, B_HDIM), F32))
    dec_args, dec_specs, dec_out_specs, dec_out_shapes = [], [], [], []
    if dec_job is not None:
        dec_args, dec_specs, dec_out_specs, dec_out_shapes = _dec_attn_specs(
            *dec_job, seq_index=lambda bb, i: bb * nt + i)
    return pl.pallas_call(
        functools.partial(_l1a_kernel, len(dec_args)),
        grid=(b, nt),
        in_specs=in_specs + dec_specs,
        out_specs=qkv_specs + gate_specs + win_specs + dec_out_specs,
        out_shape=qkv_shapes + gate_shapes + win_shapes + dec_out_shapes,
        compiler_params=pltpu.CompilerParams(
            dimension_semantics=("arbitrary", "arbitrary"), vmem_limit_bytes=VMEM_LIMIT),
        name="layer1_proj_prompt",
    )(x1, g_kv, g_pre, wkv, win, cos_t, sin_t, *dec_args)


def _cols_to_lanes(cols):
    t = cols[0].shape[0]
    lane = lax.broadcasted_iota(jnp.int32, (t, LANES), 1)
    acc = jnp.zeros((t, LANES), F32)
    for h, cvec in enumerate(cols):
        acc = jnp.where(lane == h, cvec, acc)
    return acc


def _band_attn_kernel(q_ref, kc_ref, kp_ref, vc_ref, vp_ref, o_ref, lse_ref):
    nres, tq = q_ref.shape[1:3]
    nsb = tq // ATT_BLK
    j = pl.program_id(2)
    row = lax.broadcasted_iota(jnp.int32, (ATT_BLK, 2 * ATT_BLK), 0)
    col = lax.broadcasted_iota(jnp.int32, (ATT_BLK, 2 * ATT_BLK), 1)
    band = jnp.logical_and(col >= row, col <= row + ATT_BLK)
    first_pen = jnp.where(col < ATT_BLK, jnp.where(j > 0, 0.0, -jnp.inf), 0.0)
    scale = B_HDIM ** -0.5
    blocks = [(r, sb) for r in range(nres) for sb in range(nsb)]
    for g0 in range(0, len(blocks), ATT_GROUP):
        grp = blocks[g0:g0 + ATT_GROUP]
        qs, ks, vs, pens = [], [], [], []
        for r, sb in grp:
            rs = slice(sb * ATT_BLK, (sb + 1) * ATT_BLK)
            ps = slice((sb - 1) * ATT_BLK, sb * ATT_BLK)
            for h in range(B_HEADS):
                hs = slice(h * B_HDIM, (h + 1) * B_HDIM)
                qs.append(q_ref[0, r, rs, hs])
                kp = kp_ref[0, r, :, hs] if sb == 0 else kc_ref[0, r, ps, hs]
                vp = vp_ref[0, r, :, hs] if sb == 0 else vc_ref[0, r, ps, hs]
                ks.append(jnp.concatenate([kp, kc_ref[0, r, rs, hs]], axis=0))
                vs.append(jnp.concatenate([vp, vc_ref[0, r, rs, hs]], axis=0))
                pens.append(sb == 0)
        q3 = jnp.stack(qs)
        k3 = jnp.stack(ks)
        v3 = jnp.stack(vs)
        s = jnp.einsum('uqd,ukd->uqk', q3, k3, preferred_element_type=F32)
        s = jnp.stack([s[u] + first_pen if pens[u] else s[u] for u in range(len(pens))])
        s = jnp.where(band[None], s, -jnp.inf)
        mx = jnp.max(s, axis=-1, keepdims=True)
        p = jnp.exp2((s - mx) * (scale * LOG2E))
        l = jnp.sum(p, axis=-1, keepdims=True)
        o = jnp.einsum('uqk,ukd->uqd', p.astype(BF16), v3, preferred_element_type=F32) / l
        lse = mx * scale + jnp.log(l)
        for i, (r, sb) in enumerate(grp):
            rs = slice(sb * ATT_BLK, (sb + 1) * ATT_BLK)
            for h in range(B_HEADS):
                o_ref[0, r, rs, h * B_HDIM:(h + 1) * B_HDIM] = o[i * B_HEADS + h].astype(BF16)
            lse_ref[0, r, rs, :] = _cols_to_lanes([lse[i * B_HEADS + h] for h in range(B_HEADS)])


def _band_attention(q, k, v):
    b, d, ls, _ = q.shape
    tq = min(ATT_TILE, ls)
    nj = ls // tq
    ratio = tq // ATT_BLK
    nres = min(d, ATT_TILE // tq)
    cur = lambda bb, r, j: (bb, r, j, 0)
    prev = lambda bb, r, j: (bb, r, jnp.maximum(j * ratio - 1, 0), 0)
    return pl.pallas_call(
        _band_attn_kernel,
        grid=(b, d // nres, nj),
        in_specs=[pl.BlockSpec((1, nres, tq, B_WIDTH), cur),
                  pl.BlockSpec((1, nres, tq, B_WIDTH), cur),
                  pl.BlockSpec((1, nres, ATT_BLK, B_WIDTH), prev),
                  pl.BlockSpec((1, nres, tq, B_WIDTH), cur),
                  pl.BlockSpec((1, nres, ATT_BLK, B_WIDTH), prev)],
        out_specs=[pl.BlockSpec((1, nres, tq, B_WIDTH), cur),
                   pl.BlockSpec((1, nres, tq, LANES), cur)],
        out_shape=[jax.ShapeDtypeStruct((b, d, ls, B_WIDTH), BF16),
                   jax.ShapeDtypeStruct((b, d, ls, LANES), F32)],
        compiler_params=pltpu.CompilerParams(
            dimension_semantics=("arbitrary", "arbitrary", "arbitrary"), vmem_limit_bytes=VMEM_LIMIT),
        name="band_attention_d%d" % d,
    )(q, k, k, v, v)


def _unpermute(ref):
    d, rows, width = ref.shape[1:]
    if d == 1:
        return ref[0, 0]
    return jnp.swapaxes(ref[0], 0, 1).reshape(d * rows, width)


def _l1c_kernel(x_ref, o0_ref, o1_ref, o2_ref, l0_ref, l1_ref, l2_ref, zg_ref, qm_ref, zm_ref,
                mkv_ref, wout_ref, gpost_ref, y_ref):
    tt = x_ref.shape[1]
    o_refs = (o0_ref, o1_ref, o2_ref)
    l_refs = (l0_ref, l1_ref, l2_ref)
    outs, lses = [], []
    for g, (_, d) in enumerate(B_GROUPS):
        outs.append(_unpermute(o_refs[g]))
        lses.append(_unpermute(l_refs[g])[:, 0:B_HEADS])
    mx = jnp.maximum(jnp.maximum(lses[0], lses[1]), lses[2])
    es = [jnp.exp(l - mx) for l in lses]
    tot = es[0] + es[1] + es[2]
    ws = [(e / tot).astype(BF16) for e in es]
    parts = []
    for h in range(B_HEADS):
        hs = slice(h * B_HDIM, (h + 1) * B_HDIM)
        acc = ws[0][:, h:h + 1] * outs[0][:, hs]
        acc = acc + ws[1][:, h:h + 1] * outs[1][:, hs]
        acc = acc + ws[2][:, h:h + 1] * outs[2][:, hs]
        parts.append(acc)
    ydil = jnp.concatenate(parts, axis=-1)
    ymix = (ydil.astype(F32) * _silu(zg_ref[0].astype(F32))).astype(BF16)
    mk = mkv_ref[0, :, 0:M_WIDTH]
    mv = mkv_ref[0, :, M_WIDTH:2 * M_WIDTH]
    ym = (_mem_attention(qm_ref[0], mk, mv) * _silu(zm_ref[0].astype(F32))).astype(BF16)
    out = _dot(ymix, wout_ref[0:B_WIDTH, :]) + _dot(ym, wout_ref[B_WIDTH:B_WIDTH + M_WIDTH, :])
    y_ref[0] = x_ref[0] + _rms_scale(out) * gpost_ref[...]


def _layer1_out_prompt(x1, os_, ls_, zg, qm, zm, mkv_bf, w_out, g_post):
    b, s, _ = x1.shape
    tt = min(TOK_TILE, s)
    nt = s // tt
    tile = lambda bb, i: (bb, i, 0)
    perm = lambda bb, i: (bb, 0, i, 0)
    in_specs = [pl.BlockSpec((1, tt, D_MODEL), tile)]
    for width in (B_WIDTH, LANES):
        for (_, d) in B_GROUPS:
            in_specs.append(pl.BlockSpec((1, d, tt // d, width), perm))
    in_specs += [pl.BlockSpec((1, tt, B_WIDTH), tile)] * 3
    in_specs += [pl.BlockSpec((1, N_MEM, 2 * M_WIDTH), lambda bb, i: (bb, 0, 0)),
                 _const_spec(w_out.shape), _const_spec((1, D_MODEL))]
    return pl.pallas_call(
        _l1c_kernel,
        grid=(b, nt),
        in_specs=in_specs,
        out_specs=pl.BlockSpec((1, tt, D_MODEL), tile),
        out_shape=jax.ShapeDtypeStruct((b, s, D_MODEL), F32),
        compiler_params=pltpu.CompilerParams(
            dimension_semantics=("arbitrary", "arbitrary"), vmem_limit_bytes=VMEM_LIMIT),
        name="layer1_out_prompt",
    )(x1, *os_, *ls_, zg, qm, zm, mkv_bf, w_out, g_post)


def _rope_tables(pos):
    half = B_HDIM // 2
    inv = ROPE_THETA ** (-jnp.arange(half, dtype=F32) / half)
    ang = pos[:, None] * inv[None, :]
    cos = jnp.cos(ang)
    sin = jnp.sin(ang)
    return jnp.concatenate([cos, cos], axis=-1), jnp.concatenate([-sin, sin], axis=-1)


def _prompt_group(x_prompt, mem_prompt, p, sample=None):
    b, s, _ = x_prompt.shape
    memkv_f, memkv_b = _memkv(mem_prompt.reshape(b * N_MEM, D_MODEL), p['w_mkv'])
    depth = memkv_f.shape[0]
    memkv_b = memkv_b.reshape(depth, b, N_MEM, 2 * M_WIDTH)
    job0 = sample.mlstm_job() if sample is not None else None
    if job0 is not None and b * (s // min(L0_TILE, s)) + 2 < 2 * job0[0].shape[0]:
        job0 = None
    outs0 = _layer0_prompt_pipelined(
        x_prompt, p['g_pre'][0:1], p['w_in_a'][0], p['conv_w_a'][0], p['conv_b_a'], p['w_q_a'][0],
        p['w_k_a'][0], jnp.swapaxes(p['w_k_a'][0], 1, 2), p['w_v_a'][0], p['w_if_a'], p['b_if_a'],
        p['g_hn_a'], p['skip_a'], memkv_b[0], p['w_out_a'][0], p['g_post'][0:1], job0)
    x1, conv_p, c_p, n_p, m_pad = outs0[0:5]
    if sample is not None:
        sample.after_mlstm(outs0[5:10] if job0 is not None else _dec_mlstm(*sample.mlstm_job()))
    cos_t, sin_t = _rope_tables(jnp.arange(s, dtype=F32))
    job1 = sample.attn_job if sample is not None else None
    if job1 is not None and b * (s // min(TOK_TILE, s)) != job1[0].shape[0]:
        job1 = None
    outs = _layer1_proj_prompt(x1, p['g_kv'], p['g_pre'][1:2], p['w_kv_b'], p['w_in_b'][0], cos_t, sin_t,
                               job1)
    if sample is not None:
        sample.after_attn(tuple(outs[15:17]) if job1 is not None else _dec_attn(*sample.attn_job))
    qs, ks, vs = outs[0:3], outs[3:6], outs[6:9]
    zg, qm, zm = outs[9:12]
    wins = outs[12:15]
    os_, ls_ = [], []
    for g in range(N_GROUPS):
        o, l = _band_attention(qs[g], ks[g], vs[g])
        os_.append(o)
        ls_.append(l)
    y = _layer1_out_prompt(x1, os_, ls_, zg, qm, zm, memkv_b[1], p['w_out_b'][0], p['g_post'][1:2])
    m_p = m_pad[:, 0:A_HEADS, 0][None]
    wins = [w.reshape(b, w.shape[1], 2, B_HEADS, B_HDIM) for w in wins]
    memkv_p = memkv_f.reshape(depth, b, N_MEM, 2, M_HEADS, M_HDIM)
    return y, conv_p, c_p, n_p, m_p, wins, memkv_p


def _prep_params(g_pre, g_post, w_in_a, conv_w_a, conv_b_a, w_q_a, w_k_a, w_v_a, w_if_a, b_if_a,
                 g_hn_a, skip_a, w_out_a, g_kv, w_kv_b, w_in_b, w_out_b, w_mkv):
    wif = jnp.pad(w_if_a[0], ((0, 0), (0, LANES - 2 * A_HEADS))).astype(BF16)
    bif = jnp.pad(b_if_a[0], (0, LANES - 2 * A_HEADS))[None, :]
    return {
        'g_pre': g_pre, 'g_post': g_post,
        'w_in_a': w_in_a.astype(BF16), 'conv_w_a': conv_w_a, 'conv_b_a': conv_b_a,
        'w_q_a': w_q_a.astype(BF16), 'w_k_a': w_k_a.astype(BF16), 'w_v_a': w_v_a.astype(BF16),
        'w_if_a': wif, 'b_if_a': bif, 'g_hn_a': g_hn_a, 'skip_a': skip_a,
        'w_out_a': w_out_a.astype(BF16), 'g_kv': g_kv[None, :], 'w_kv_b': w_kv_b.astype(BF16),
        'w_in_b': w_in_b.astype(BF16), 'w_out_b': w_out_b.astype(BF16), 'w_mkv': w_mkv.astype(BF16),
    }


def _dec_l0_proj_kernel(x_ref, gpre_ref, win_ref, cst_ref, convw_ref, convb_ref, wq_ref, wk_ref, wv_ref,
                        wif_ref, bif_ref,
                        q_ref, k_ref, v_ref, gates_ref, xc_ref, opre_ref, zg_ref, qm_ref, zm_ref, cnew_ref):
    h = (_rms_scale(x_ref[:, 0, :]) * gpre_ref[...]).astype(BF16)
    u = _dot(h, win_ref[:, 0:A_INNER])
    opre_ref[...] = _dot(h, win_ref[:, A_INNER:2 * A_INNER])
    zg_ref[...] = _dot(h, win_ref[:, 2 * A_INNER:3 * A_INNER])
    qm_ref[...] = _rows_to_heads(_dot(h, win_ref[:, 3 * A_INNER:3 * A_INNER + M_WIDTH]))
    zm_ref[...] = _dot(h, win_ref[:, 3 * A_INNER + M_WIDTH:3 * A_INNER + 2 * M_WIDTH])
    cw = convw_ref[...]
    xc = convb_ref[...] + cst_ref[0, :, 0, :] * cw[0:1, :]
    xc = xc + cst_ref[0, :, 1, :] * cw[1:2, :]
    xc = xc + cst_ref[0, :, 2, :] * cw[2:3, :]
    xc = xc + u * cw[3:4, :]
    xc = _silu(xc)
    xc_ref[...] = xc
    cnew_ref[0, :, 0, :] = cst_ref[0, :, 1, :]
    cnew_ref[0, :, 1, :] = cst_ref[0, :, 2, :]
    cnew_ref[0, :, 2, :] = u
    qs, ks, vs, cat = [], [], [], []
    for hd in range(A_HEADS):
        sl = slice(hd * A_HDIM, (hd + 1) * A_HDIM)
        xh = xc[:, sl].astype(BF16)
        qh = _dot(xh, wq_ref[hd])
        kh = _dot(xh, wk_ref[hd]) * (A_HDIM ** -0.5)
        vh = _dot(u[:, sl].astype(BF16), wv_ref[hd])
        qs.append(qh)
        ks.append(kh)
        vs.append(vh)
        cat += [qh.astype(BF16), kh.astype(BF16), vh.astype(BF16)]
    q_ref[...] = jnp.concatenate(qs, axis=-1)
    k_ref[...] = jnp.concatenate(ks, axis=-1)
    v_ref[...] = jnp.concatenate(vs, axis=-1)
    gates_ref[...] = _dot(jnp.concatenate(cat, axis=-1), wif_ref[...]) + bif_ref[...]


def _whole(shape):
    nd = len(shape)
    return pl.BlockSpec(shape, lambda *_: (0,) * nd)


def _dec_l0_proj(x, g_pre, w_in, cst, conv_w, conv_b, wq, wk, wv, wif, bif):
    nb = x.shape[0]
    args = (x, g_pre, w_in, cst, conv_w, conv_b, wq, wk, wv, wif, bif)
    f = lambda *s: jax.ShapeDtypeStruct(s, F32)
    out_shape = [f(nb, A_INNER), f(nb, A_INNER), f(nb, A_INNER), f(nb, LANES), f(nb, A_INNER), f(nb, A_INNER),
                 f(nb, A_INNER), f(nb, M_HEADS, M_HDIM), f(nb, M_WIDTH), f(1, nb, CONV_W - 1, A_INNER)]
    return pl.pallas_call(
        _dec_l0_proj_kernel,
        grid=(1,),
        in_specs=[_whole(a.shape) for a in args],
        out_specs=[_whole(o.shape) for o in out_shape],
        out_shape=out_shape,
        compiler_params=pltpu.CompilerParams(dimension_semantics=("arbitrary",), vmem_limit_bytes=VMEM_LIMIT),
        name="dec_l0_proj",
    )(*args)


def _row_to_col(row, eye):
    return jnp.sum(jnp.where(eye, row, 0.0), axis=-1, keepdims=True)


def _col_to_row(colv, eye):
    return jnp.sum(jnp.where(eye, colv, 0.0), axis=0, keepdims=True)


def _dec_mem_attention(q, kv_ref_view):
    kk = kv_ref_view[:, 0]
    vv = kv_ref_view[:, 1]
    s = jnp.sum(kk * (q * (M_HDIM ** -0.5))[None], axis=-1, keepdims=True)
    mx = jnp.max(s, axis=0, keepdims=True)
    p = jnp.exp(s - mx)
    return jnp.sum(p * vv, axis=0) / jnp.sum(p, axis=0)


def _dec_mlstm_body(b, heads, with_mem, q_ref, k_ref, v_ref, gates_ref, m_ref, c_ref, n_ref, qm_ref, kv_ref,
                    hs_ref, c_out, n_out, m_out, ym_ref):
    rb = pl.ds(b, 1)
    g = gates_ref[rb, :]
    mrow = m_ref[0, rb, :]
    r = lax.broadcasted_iota(jnp.int32, (A_HDIM, A_HDIM), 0)
    c = lax.broadcasted_iota(jnp.int32, (A_HDIM, A_HDIM), 1)
    eye = r == c
    sl = {h: slice(h * A_HDIM, (h + 1) * A_HDIM) for h in heads}
    qh = {h: q_ref[rb, sl[h]] for h in heads}
    kh = {h: k_ref[rb, sl[h]] for h in heads}
    vh = {h: v_ref[rb, sl[h]] for h in heads}
    c_old = {h: c_ref[0, 0, h] for h in heads}
    n_old = {h: n_ref[0, 0, h:h + 1, :] for h in heads}
    li = {h: g[:, h:h + 1] for h in heads}
    lf = {h: _log_sigmoid(g[:, 4 + h:5 + h]) for h in heads}
    m_old = {h: mrow[:, h:h + 1] for h in heads}
    cq = {h: jnp.sum(c_old[h] * qh[h], axis=-1, keepdims=True) for h in heads}
    v_col = {h: _row_to_col(vh[h], eye) for h in heads}
    nq = {h: jnp.sum(n_old[h] * qh[h], axis=-1, keepdims=True) for h in heads}
    qk = {h: jnp.sum(qh[h] * kh[h], axis=-1, keepdims=True) for h in heads}
    inter = {h: lf[h] + m_old[h] for h in heads}
    m_new = {h: jnp.maximum(inter[h], li[h]) for h in heads}
    ws = {h: jnp.exp(li[h] - m_new[h]) for h in heads}
    dec = {h: jnp.exp(inter[h] - m_new[h]) for h in heads}
    sc = {h: qk[h] * ws[h] for h in heads}
    den = {h: sc[h] + dec[h] * nq[h] for h in heads}
    h_col = {h: (sc[h] * v_col[h] + dec[h] * cq[h]) / jnp.maximum(jnp.abs(den[h]), jnp.exp(-m_new[h]))
             for h in heads}
    for h in heads:
        c_out[0, 0, h] = dec[h] * c_old[h] + (ws[h] * v_col[h]) * kh[h]
        n_out[0, 0, h:h + 1, :] = dec[h] * n_old[h] + ws[h] * kh[h]
        m_out[0, h:h + 1, :] = jnp.broadcast_to(m_new[h], (1, LANES))
    for h in heads:
        hs_ref[0, :, sl[h]] = _col_to_row(h_col[h], eye)
    if with_mem:
        ym_ref[0] = _dec_mem_attention(qm_ref[0], kv_ref.at[0, 0])


def _dec_mlstm_kernel(*refs):
    _dec_mlstm_body(pl.program_id(0), range(A_HEADS), True, *refs)


def _dec_mlstm_specs(q, k, v, gates, m_in, state_c, state_n, qm3, cache_mem_kv, seq_index):
    nb = q.shape[0]

    def at(*tail, lead=()):
        return lambda *idx: lead + (seq_index(*idx),) + tail

    in_specs = [_whole(q.shape), _whole(k.shape), _whole(v.shape), _whole(gates.shape), _whole(m_in.shape),
                pl.BlockSpec((1, 1, A_HEADS, A_HDIM, A_HDIM), at(0, 0, 0, lead=(0,))),
                pl.BlockSpec((1, 1, A_HEADS, A_HDIM), at(0, 0, lead=(0,))),
                pl.BlockSpec((1, M_HEADS, M_HDIM), at(0, 0)),
                pl.BlockSpec((1, 1, N_MEM, 2, M_HEADS, M_HDIM), at(0, 0, 0, 0, lead=(0,)))]
    out_specs = [pl.BlockSpec((1, 1, A_INNER), at(0, 0)),
                 pl.BlockSpec((1, 1, A_HEADS, A_HDIM, A_HDIM), at(0, 0, 0, lead=(0,))),
                 pl.BlockSpec((1, 1, A_HEADS, A_HDIM), at(0, 0, lead=(0,))),
                 pl.BlockSpec((1, A_HEADS, LANES), at(0, 0)),
                 pl.BlockSpec((1, M_HEADS, M_HDIM), at(0, 0))]
    out_shapes = [jax.ShapeDtypeStruct((nb, 1, A_INNER), F32),
                  jax.ShapeDtypeStruct(state_c.shape, F32),
                  jax.ShapeDtypeStruct(state_n.shape, F32),
                  jax.ShapeDtypeStruct((nb, A_HEADS, LANES), F32),
                  jax.ShapeDtypeStruct((nb, M_HEADS, M_HDIM), F32)]
    return [q, k, v, gates, m_in, state_c, state_n, qm3, cache_mem_kv], in_specs, out_specs, out_shapes


def _dec_mlstm(*job):
    args, in_specs, out_specs, out_shapes = _dec_mlstm_specs(*job, seq_index=lambda b: b)
    return pl.pallas_call(
        _dec_mlstm_kernel,
        grid=(args[0].shape[0],),
        in_specs=in_specs,
        out_specs=out_specs,
        out_shape=out_shapes,
        compiler_params=pltpu.CompilerParams(dimension_semantics=("arbitrary",), vmem_limit_bytes=VMEM_LIMIT),
        name="dec_mlstm",
    )(*args)


def _dec_mid_kernel(hs_ref, opre_ref, xc_ref, zg_ref, ym_ref, zm_ref, x_ref, ghn_ref, skip_ref, wout_ref,
                    gpost_ref, gkv_ref, gpre_ref, wkv_ref, win_ref, cos_ref, sin_ref,
                    x1_ref, q_ref, k_ref, v_ref, zg1_ref, qm1_ref, zm1_ref, win0_ref, win1_ref, win2_ref):
    hh = _sigmoid(opre_ref[...]) * hs_ref[:, 0, :]
    parts = []
    for h in range(A_HEADS):
        v = hh[:, h * A_HDIM:(h + 1) * A_HDIM]
        mu = jnp.mean(v, axis=-1, keepdims=True)
        var = jnp.mean(jnp.square(v - mu), axis=-1, keepdims=True)
        parts.append((v - mu) * lax.rsqrt(var + EPS))
    y = jnp.concatenate(parts, axis=-1) * ghn_ref[...] + skip_ref[...] * xc_ref[...]
    ymix = (y * _silu(zg_ref[...])).astype(BF16)
    ym = (_heads_to_rows(ym_ref) * _silu(zm_ref[...])).astype(BF16)
    out = _dot(ymix, wout_ref[0:A_INNER, :]) + _dot(ym, wout_ref[A_INNER:A_INNER + M_WIDTH, :])
    x1 = x_ref[:, 0, :] + _rms_scale(out) * gpost_ref[...]
    x1_ref[...] = x1
    xn = _rms_scale(x1)
    hk = (xn * gkv_ref[...]).astype(BF16)
    hq = (xn * gpre_ref[...]).astype(BF16)
    cos = cos_ref[...]
    sin = sin_ref[...]
    ks, vs = [], []
    for g in range(N_GROUPS):
        ks.append(_rope_cols(_dot(hk, wkv_ref[:, g * 2 * B_WIDTH:g * 2 * B_WIDTH + B_WIDTH]), cos, sin))
        vs.append(_dot(hk, wkv_ref[:, g * 2 * B_WIDTH + B_WIDTH:(g + 1) * 2 * B_WIDTH]))
    k_ref[...] = _rows_to_heads(jnp.concatenate(ks, axis=-1))
    v_ref[...] = _rows_to_heads(jnp.concatenate(vs, axis=-1))
    for g, wref in enumerate((win0_ref, win1_ref, win2_ref)):
        wref[...] = _rows_to_kv_heads(ks[g], vs[g])
    qoff = N_GROUPS * B_WIDTH
    q_ref[...] = _rows_to_heads(_rope_cols(_dot(hq, win_ref[:, 0:qoff]), cos, sin))
    zg1_ref[...] = _dot(hq, win_ref[:, qoff:qoff + B_WIDTH])
    qm1_ref[...] = _rows_to_heads(_dot(hq, win_ref[:, qoff + B_WIDTH:qoff + B_WIDTH + M_WIDTH]))
    zm1_ref[...] = _dot(hq, win_ref[:, qoff + B_WIDTH + M_WIDTH:qoff + B_WIDTH + 2 * M_WIDTH])


def _dec_mid(hs, opre, xc, zg, ym, zm, x, ghn, skip, w_out, g_post, g_kv, g_pre, wkv, win, cos, sin):
    nb = x.shape[0]
    args = (hs, opre, xc, zg, ym, zm, x, ghn, skip, w_out, g_post, g_kv, g_pre, wkv, win, cos, sin)
    f = lambda *s: jax.ShapeDtypeStruct(s, F32)
    gh = N_GROUPS * B_HEADS
    out_shape = [f(nb, D_MODEL), f(nb, gh, B_HDIM), f(nb, gh, B_HDIM), f(nb, gh, B_HDIM),
                 f(nb, B_WIDTH), f(nb, M_HEADS, M_HDIM), f(nb, M_WIDTH)] + [f(nb, 2 * B_HEADS, B_HDIM)] * N_GROUPS
    return pl.pallas_call(
        _dec_mid_kernel,
        grid=(1,),
        in_specs=[_whole(a.shape) for a in args],
        out_specs=[_whole(o.shape) for o in out_shape],
        out_shape=out_shape,
        compiler_params=pltpu.CompilerParams(dimension_semantics=("arbitrary",), vmem_limit_bytes=VMEM_LIMIT),
        name="dec_mid",
    )(*args)


def _dec_attn_kernel(q_ref, kn_ref, vn_ref, w0_ref, w1_ref, w2_ref, qm_ref, kv_ref, ydil_ref, ym_ref):
    w_refs = (w0_ref, w1_ref, w2_ref)
    scale = B_HDIM ** -0.5
    groups = range(N_GROUPS)
    hsl = [slice(g * B_HEADS, (g + 1) * B_HEADS) for g in groups]
    q = [q_ref[0, hsl[g], :] * scale for g in groups]
    s_c = [jnp.sum(w_refs[g][0, :, 0] * q[g][None], axis=-1, keepdims=True) for g in groups]
    s_n = [jnp.sum(kn_ref[0, hsl[g], :] * q[g], axis=-1, keepdims=True) for g in groups]
    mxs = [jnp.maximum(jnp.max(s_c[g], axis=0), s_n[g]) for g in groups]
    p_c = [jnp.exp(s_c[g] - mxs[g][None]) for g in groups]
    p_n = [jnp.exp(s_n[g] - mxs[g]) for g in groups]
    ls = [jnp.sum(p_c[g], axis=0) + p_n[g] for g in groups]
    outs = [(jnp.sum(p_c[g] * w_refs[g][0, :, 1], axis=0) + p_n[g] * vn_ref[0, hsl[g], :]) / ls[g]
            for g in groups]
    lses = [mxs[g] + jnp.log(ls[g]) for g in groups]
    mx = jnp.maximum(jnp.maximum(lses[0], lses[1]), lses[2])
    es = [jnp.exp(l - mx) for l in lses]
    tot = es[0] + es[1] + es[2]
    ydil_ref[0] = (es[0] / tot) * outs[0] + (es[1] / tot) * outs[1] + (es[2] / tot) * outs[2]
    ym_ref[0] = _dec_mem_attention(qm_ref[0], kv_ref.at[0, 0])


def _dec_attn_specs(q4, kn4, vn4, cw0, cw1, cw2, qm3, cache_mem_kv, layer, seq_index):
    nb = q4.shape[0]
    rows = B_GROUPS[0][0]

    def at(*tail):
        return lambda *idx: (seq_index(*idx),) + tail

    win_specs = [pl.BlockSpec((1, rows, 2, B_HEADS, B_HDIM), at(0, 0, 0, 0)),
                 pl.BlockSpec((1, rows, None, 2, B_HEADS, B_HDIM), at(0, 0, 0, 0, 0)),
                 pl.BlockSpec((1, rows, None, 2, B_HEADS, B_HDIM), at(0, 0, 0, 0, 0))]
    in_specs = [pl.BlockSpec((1, N_GROUPS * B_HEADS, B_HDIM), at(0, 0))] * 3 + win_specs + [
        pl.BlockSpec((1, M_HEADS, M_HDIM), at(0, 0)),
        pl.BlockSpec((1, 1, N_MEM, 2, M_HEADS, M_HDIM), lambda *idx: (layer, seq_index(*idx), 0, 0, 0, 0))]
    out_specs = [pl.BlockSpec((1, B_HEADS, B_HDIM), at(0, 0)), pl.BlockSpec((1, M_HEADS, M_HDIM), at(0, 0))]
    out_shapes = [jax.ShapeDtypeStruct((nb, B_HEADS, B_HDIM), F32),
                  jax.ShapeDtypeStruct((nb, M_HEADS, M_HDIM), F32)]
    return [q4, kn4, vn4, cw0, cw1, cw2, qm3, cache_mem_kv], in_specs, out_specs, out_shapes


def _dec_attn(q4, kn4, vn4, cw0, cw1, cw2, qm3, cache_mem_kv, layer):
    args, in_specs, out_specs, out_shapes = _dec_attn_specs(
        q4, kn4, vn4, cw0, cw1, cw2, qm3, cache_mem_kv, layer, seq_index=lambda b: b)
    return pl.pallas_call(
        _dec_attn_kernel,
        grid=(q4.shape[0],),
        in_specs=in_specs,
        out_specs=out_specs,
        out_shape=out_shapes,
        compiler_params=pltpu.CompilerParams(dimension_semantics=("arbitrary",), vmem_limit_bytes=VMEM_LIMIT),
        name="dec_attn",
    )(*args)


def _dec_out_kernel(ydil_ref, zg_ref, ym_ref, zm_ref, x_ref, wout_ref, gpost_ref, y_ref):
    ymix = (_heads_to_rows(ydil_ref) * _silu(zg_ref[...])).astype(BF16)
    ym = (_heads_to_rows(ym_ref) * _silu(zm_ref[...])).astype(BF16)
    out = _dot(ymix, wout_ref[0:B_WIDTH, :]) + _dot(ym, wout_ref[B_WIDTH:B_WIDTH + M_WIDTH, :])
    y_ref[:, 0, :] = x_ref[...] + _rms_scale(out) * gpost_ref[...]


def _dec_out(ydil, zg, ym, zm, x1, w_out, g_post):
    args = (ydil, zg, ym, zm, x1, w_out, g_post)
    out_shape = (x1.shape[0], 1, x1.shape[1])
    return pl.pallas_call(
        _dec_out_kernel,
        grid=(1,),
        in_specs=[_whole(a.shape) for a in args],
        out_specs=_whole(out_shape),
        out_shape=jax.ShapeDtypeStruct(out_shape, F32),
        compiler_params=pltpu.CompilerParams(dimension_semantics=("arbitrary",), vmem_limit_bytes=VMEM_LIMIT),
        name="dec_out",
    )(*args)


class _SampleGroup:
    def __init__(self, x_sample, state_conv, state_c, state_n, state_m, cache_wins, cache_mem_kv, p):
        self.p = p
        self.nb = nb = x_sample.shape[0]
        self.cache_wins = cache_wins
        self.cache_mem_kv = cache_mem_kv
        self.x = x_sample
        q, k, v, gates, self.xc, self.opre, self.zg, qm3, self.zm, self.conv_s = _dec_l0_proj(
            x_sample, p['g_pre'][0:1], p['w_in_a'][0], state_conv, p['conv_w_a'][0], p['conv_b_a'],
            p['w_q_a'][0], p['w_k_a'][0], p['w_v_a'][0], p['w_if_a'], p['b_if_a'])
        self._mlstm_job = (q, k, v, gates, state_m, state_c, state_n, qm3, cache_mem_kv)
        self.attn_job = None

    def mlstm_job(self):
        return self._mlstm_job

    def after_mlstm(self, res):
        p, nb = self.p, self.nb
        hs, self.c_s, self.n_s, m_rows, ym0 = res
        self.m_s = m_rows[:, :, 0][None]
        cos, sin = _rope_tables(PAST_LEN + jnp.arange(1, dtype=F32))
        self.x1, qd, kn, vn, self.zg1, qm1, self.zm1, w0, w1, w2 = _dec_mid(
            hs, self.opre, self.xc, self.zg, ym0, self.zm, self.x,
            p['g_hn_a'], p['skip_a'], p['w_out_a'][0], p['g_post'][0:1], p['g_kv'], p['g_pre'][1:2],
            p['w_kv_b'], p['w_in_b'][0], cos, sin)
        cws = [self.cache_wins[0]]
        for g in (1, 2):
            w, d = B_GROUPS[g]
            cws.append(self.cache_wins[g].reshape(nb, w // d, d, 2, B_HEADS, B_HDIM))
        self.attn_job = (qd, kn, vn, cws[0], cws[1], cws[2], qm1, self.cache_mem_kv, 1)
        self.wins_s = [w.reshape(nb, 1, 2, B_HEADS, B_HDIM) for w in (w0, w1, w2)]

    def after_attn(self, res):
        p = self.p
        ydil, ym1 = res
        self.y = _dec_out(ydil, self.zg1, ym1, self.zm1, self.x1, p['w_out_b'][0], p['g_post'][1:2])

    def outputs(self):
        return self.y, self.conv_s, self.c_s, self.n_s, self.m_s, self.wins_s


def _sample_group(x_sample, state_conv, state_c, state_n, state_m, cache_wins, cache_mem_kv, p):
    sg = _SampleGroup(x_sample, state_conv, state_c, state_n, state_m, cache_wins, cache_mem_kv, p)
    sg.after_mlstm(_dec_mlstm(*sg.mlstm_job()))
    sg.after_attn(_dec_attn(*sg.attn_job))
    return sg.outputs()


def kernel(x_prompt, x_sample, mem_prompt, state_conv, state_C, state_n, state_m, cache_win0, cache_win1,
           cache_win2, cache_mem_kv, g_pre, g_post, w_in_a, conv_w_a, conv_b_a, w_q_a, w_k_a, w_v_a, w_if_a,
           b_if_a, g_hn_a, skip_a, w_out_a, g_kv, w_kv_b, w_in_b, w_out_b, w_mkv):
    p = _prep_params(g_pre, g_post, w_in_a, conv_w_a, conv_b_a, w_q_a, w_k_a, w_v_a, w_if_a, b_if_a,
                     g_hn_a, skip_a, w_out_a, g_kv, w_kv_b, w_in_b, w_out_b, w_mkv)
    sample = _SampleGroup(x_sample, state_conv, state_C, state_n, state_m,
                          (cache_win0, cache_win1, cache_win2), cache_mem_kv, p)
    y_p, conv_p, c_p, n_p, m_p, wins_p, memkv_p = _prompt_group(x_prompt, mem_prompt, p, sample)
    y_s, conv_s, c_s, n_s, m_s, wins_s = sample.outputs()
    return (y_p, y_s, conv_p, c_p, n_p, m_p, wins_p[0], wins_p[1], wins_p[2], memkv_p,
            conv_s, c_s, n_s, m_s, wins_s[0], wins_s[1], wins_s[2])
```

```python
import functools

import jax
import jax.numpy as jnp
from jax import lax
from jax.experimental import pallas as pl
from jax.experimental.pallas import tpu as pltpu

F32 = jnp.float32
BF16 = jnp.bfloat16

D_MODEL = 1024
A_HEADS = 4
A_HDIM = 256
A_INNER = 1024
CONV_W = 4
A_CHUNK = 128
B_GROUPS = ((128, 1), (512, 4), (2048, 16))
N_GROUPS = 3
B_HEADS = 4
B_HDIM = 128
B_WIDTH = 512
N_MEM = 256
M_HEADS = 4
M_HDIM = 128
M_WIDTH = 512
ROPE_THETA = 10000.0
EPS = 1e-6
PAST_LEN = 8192

LANES = 128
TOK_TILE = 512
L0_TILE = 256
ATT_BLK = 128
ATT_TILE = 1024
ATT_GROUP = 2
VMEM_LIMIT = 56 * 1024 * 1024

NT_DIMS = (((1,), (1,)), ((), ()))
LOG2E = 1.4426950408889634


def _dot(a, b):
    return jnp.dot(a, b, preferred_element_type=F32)


def _dot_nt(a, b):
    return lax.dot_general(a, b, NT_DIMS, preferred_element_type=F32)


def _sigmoid(x):
    return 1.0 / (1.0 + jnp.exp(-x))


def _silu(x):
    return x * _sigmoid(x)


def _log_sigmoid(x):
    return jnp.minimum(x, 0.0) - jnp.log(1.0 + jnp.exp(-jnp.abs(x)))


def _rms_scale(x):
    return x * lax.rsqrt(jnp.mean(x * x, axis=-1, keepdims=True) + EPS)


def _const_spec(shape):
    nd = len(shape)
    return pl.BlockSpec(shape, lambda *_: (0,) * nd, pipeline_mode=pl.Buffered(1))


def _mem_attention(qm, mk, mv):
    heads = range(M_HEADS)
    sl = [slice(h * M_HDIM, (h + 1) * M_HDIM) for h in heads]
    s = [_dot_nt(qm[:, sl[h]], mk[:, sl[h]]) * (M_HDIM ** -0.5) for h in heads]
    mx = [jnp.max(s[h], axis=-1, keepdims=True) for h in heads]
    p = [jnp.exp(s[h] - mx[h]) for h in heads]
    l = [jnp.sum(p[h], axis=-1, keepdims=True) for h in heads]
    outs = [_dot((p[h] / l[h]).astype(BF16), mv[:, sl[h]]) for h in heads]
    return jnp.concatenate(outs, axis=-1)


def _rows_to_heads(x):
    return jnp.swapaxes(jnp.stack([x[:, h * LANES:(h + 1) * LANES] for h in range(x.shape[1] // LANES)]), 0, 1)


def _heads_to_rows(ref):
    return jnp.concatenate([ref[:, h, :] for h in range(ref.shape[1])], axis=-1)


def _rows_to_kv_heads(k, v):
    pieces = [a[:, h * LANES:(h + 1) * LANES] for a in (k, v) for h in range(a.shape[1] // LANES)]
    return jnp.swapaxes(jnp.stack(pieces), 0, 1)


def _memkv_kernel(m_ref, w_ref, o_ref, ob_ref):
    r = _dot(m_ref[...].astype(BF16), w_ref[0])
    o_ref[0] = _rows_to_kv_heads(r[:, 0:M_WIDTH], r[:, M_WIDTH:2 * M_WIDTH])
    ob_ref[0] = r.astype(BF16)


def _memkv(mem2d, w_bf):
    nm = mem2d.shape[0]
    nl = w_bf.shape[0]
    tm = min(512, nm)
    return pl.pallas_call(
        _memkv_kernel,
        grid=(nl, nm // tm),
        in_specs=[pl.BlockSpec((tm, D_MODEL), lambda l, i: (i, 0)),
                  pl.BlockSpec((1, D_MODEL, 2 * M_WIDTH), lambda l, i: (l, 0, 0))],
        out_specs=[pl.BlockSpec((1, tm, 2 * M_HEADS, M_HDIM), lambda l, i: (l, i, 0, 0)),
                   pl.BlockSpec((1, tm, 2 * M_WIDTH), lambda l, i: (l, i, 0))],
        out_shape=[jax.ShapeDtypeStruct((nl, nm, 2 * M_HEADS, M_HDIM), F32),
                   jax.ShapeDtypeStruct((nl, nm, 2 * M_WIDTH), BF16)],
        compiler_params=pltpu.CompilerParams(dimension_semantics=("arbitrary", "arbitrary")),
        name="memkv",
    )(mem2d, w_bf)


def _mlstm_chunk(rs, g, qkv_v, kt_v, c_s, n_s, m_s, causal, hs):
    ls = _log_sigmoid(g)
    tok = lax.broadcasted_iota(jnp.int32, (A_CHUNK, LANES), 0)
    bc = ls
    shift = 1
    while shift < A_CHUNK:
        bc = bc + jnp.where(tok >= shift, pltpu.roll(bc, shift, 0), 0.0)
        shift *= 2
    lane = lax.broadcasted_iota(jnp.int32, (A_CHUNK, LANES), 1)
    xt = jnp.where(lane < A_HEADS, g, bc).T
    yield
    heads = range(A_HEADS)
    b_col = [bc[:, 4 + h:5 + h] for h in heads]
    b_row = [xt[4 + h:5 + h, :] for h in heads]
    li_row = [xt[h:h + 1, :] for h in heads]
    li_col = [g[:, h:h + 1] for h in heads]
    m_old = [m_s[h:h + 1, 0:1] for h in heads]
    b_last = [bc[A_CHUNK - 1:A_CHUNK, 4 + h:5 + h] for h in heads]
    qh = [qkv_v[rs, h * 3 * A_HDIM:h * 3 * A_HDIM + A_HDIM] for h in heads]
    kh = [qkv_v[rs, h * 3 * A_HDIM + A_HDIM:h * 3 * A_HDIM + 2 * A_HDIM] for h in heads]
    vh = [qkv_v[rs, h * 3 * A_HDIM + 2 * A_HDIM:(h + 1) * 3 * A_HDIM] for h in heads]
    kt = [kt_v[h] for h in heads]
    c_old = [c_s[h] for h in heads]
    n_old = [n_s[h:h + 1, :] for h in heads]
    qk = [_dot_nt(qh[h], kh[h]) for h in heads]
    qc = [_dot(qh[h], c_old[h].astype(BF16)) for h in heads]
    dm = [jnp.where(causal, b_col[h] - b_row[h] + li_row[h], -jnp.inf) for h in heads]
    inter = [b_col[h] + m_old[h] for h in heads]
    m_row = [jnp.maximum(inter[h], jnp.max(dm[h], axis=-1, keepdims=True)) for h in heads]
    g_max = [jnp.max(b_last[h] - b_row[h] + li_row[h], axis=-1, keepdims=True) for h in heads]
    m_new = [jnp.maximum(b_last[h] + m_old[h], g_max[h]) for h in heads]
    yield
    sc = [qk[h] * jnp.exp(dm[h] - m_row[h]) for h in heads]
    dec = [jnp.exp(inter[h] - m_row[h]) for h in heads]
    ws_col = [jnp.exp(b_last[h] - b_col[h] + li_col[h] - m_new[h]) for h in heads]
    dc = [jnp.exp(b_last[h] + m_old[h] - m_new[h]) for h in heads]
    yield
    sv = [_dot(sc[h].astype(BF16), vh[h]) for h in heads]
    wv = [(ws_col[h] * vh[h].astype(F32)).astype(BF16) for h in heads]
    upd = [_dot(kt[h], wv[h]) for h in heads]
    yield
    for h in heads:
        den = (jnp.sum(sc[h], axis=-1, keepdims=True)
               + dec[h] * jnp.sum(qh[h].astype(F32) * n_old[h], axis=-1, keepdims=True))
        num = sv[h] + dec[h] * qc[h]
        hs.append(num / jnp.maximum(jnp.abs(den), jnp.exp(-m_row[h])))
    yield
    for h in heads:
        c_s[h] = dc[h] * c_old[h] + upd[h]
        n_s[h:h + 1, :] = dc[h] * n_old[h] + jnp.sum(ws_col[h] * kh[h].astype(F32), axis=0, keepdims=True)
        m_s[h:h + 1, :] = jnp.broadcast_to(m_new[h], (1, LANES))
    yield


def _l0p_kernel(nt, n_dec_seq, *refs):
    n_dec_in, n_dec_out = (9, 5) if n_dec_seq else (0, 0)
    (x_ref, xp_ref, gpre_ref, win_ref, convw_ref, convb_ref, wq_ref, wk_ref, wkt_ref, wv_ref,
     wif_ref, bif_ref, ghn_ref, skip_ref, mkv_ref, wout_ref, gpost_ref) = refs[0:17]
    dec_in = refs[17:17 + n_dec_in]
    x1_ref, conv_out, c_out, n_out, m_out = refs[17 + n_dec_in:22 + n_dec_in]
    dec_out = refs[22 + n_dec_in:22 + n_dec_in + n_dec_out]
    (h_s, u_s, ymix_s, xc_s, opre_s, zg_s, qm_s, zm_s, qkv_s, kt_s, gates_s,
     c_s, n_s, m_s) = refs[22 + n_dec_in + n_dec_out:]
    tt = x_ref.shape[1]
    nsub = tt // A_CHUNK
    t = pl.program_id(0)
    parity = lax.rem(t + 1, 2)
    pos1 = lax.rem(t + nt - 1, nt)
    pos2 = lax.rem(t + 2 * nt - 2, nt)

    @pl.when(t == 0)
    def _():
        h_s[...] = jnp.zeros(h_s.shape, BF16)
        u_s[...] = jnp.zeros(u_s.shape, F32)
        xc_s[0] = jnp.zeros(xc_s.shape[1:], F32)
        opre_s[0] = jnp.zeros(opre_s.shape[1:], F32)
        zg_s[0] = jnp.zeros(zg_s.shape[1:], F32)
        qm_s[0] = jnp.zeros(qm_s.shape[1:], BF16)
        zm_s[0] = jnp.zeros(zm_s.shape[1:], F32)
        qkv_s[0] = jnp.zeros(qkv_s.shape[1:], BF16)
        kt_s[0] = jnp.zeros(kt_s.shape[1:], BF16)
        gates_s[0] = jnp.zeros(gates_s.shape[1:], F32)

    @pl.when(pos1 == 0)
    def _():
        u_s[0:8, :] = jnp.zeros((8, A_INNER), F32)

    @pl.when(pos2 == 0)
    def _():
        c_s[...] = jnp.zeros(c_s.shape, F32)
        n_s[...] = jnp.zeros(n_s.shape, F32)
        m_s[...] = jnp.zeros(m_s.shape, F32)

    row = lax.broadcasted_iota(jnp.int32, (A_CHUNK, A_CHUNK), 0)
    col = lax.broadcasted_iota(jnp.int32, (A_CHUNK, A_CHUNK), 1)
    causal = col <= row
    ghn = ghn_ref[...]
    skp = skip_ref[...]
    mk = mkv_ref[0, :, 0:M_WIDTH]
    mv = mkv_ref[0, :, M_WIDTH:2 * M_WIDTH]

    def stage2(pslot):
        ym = _mem_attention(qm_s[pslot], mk, mv) * _silu(zm_s[pslot])
        ymix_s[:, A_INNER:A_INNER + M_WIDTH] = ym.astype(BF16)
        yield
        for c in range(nsub):
            rs = slice(c * A_CHUNK, (c + 1) * A_CHUNK)
            hs = []
            yield from _mlstm_chunk(rs, gates_s[pslot, rs, :], qkv_s.at[pslot], kt_s.at[pslot, :, c],
                                    c_s, n_s, m_s, causal, hs)
            parts = []
            for h in range(A_HEADS):
                v = _sigmoid(opre_s[pslot, rs, h * A_HDIM:(h + 1) * A_HDIM]) * hs[h]
                mu = jnp.mean(v, axis=-1, keepdims=True)
                var = jnp.mean(jnp.square(v - mu), axis=-1, keepdims=True)
                parts.append((v - mu) * lax.rsqrt(var + EPS))
            hn = jnp.concatenate(parts, axis=-1) * ghn
            y = hn + skp * xc_s[pslot, rs, :]
            ymix_s[rs, 0:A_INNER] = (y * _silu(zg_s[pslot, rs, :])).astype(BF16)
            yield

    def stage1(slot):
        hb = h_s[...]
        u_s[8:8 + tt, :] = _dot(hb, win_ref[:, 0:A_INNER])
        yield
        cw = convw_ref[...]
        cb = convb_ref[...]
        for c in range(nsub):
            r0 = c * A_CHUNK
            blk = u_s[r0:r0 + A_CHUNK + 8, :]
            xc = cb + pltpu.roll(blk, 3, 0)[8:, :] * cw[0:1, :]
            xc = xc + pltpu.roll(blk, 2, 0)[8:, :] * cw[1:2, :]
            xc = xc + pltpu.roll(blk, 1, 0)[8:, :] * cw[2:3, :]
            xc = xc + blk[8:, :] * cw[3:4, :]
            xc_s[slot, r0:r0 + A_CHUNK, :] = _silu(xc)
        opre_s[slot] = _dot(hb, win_ref[:, A_INNER:2 * A_INNER])
        yield
        zg_s[slot] = _dot(hb, win_ref[:, 2 * A_INNER:3 * A_INNER])
        yield
        qm_s[slot] = _dot(hb, win_ref[:, 3 * A_INNER:3 * A_INNER + M_WIDTH]).astype(BF16)
        zm_s[slot] = _dot(hb, win_ref[:, 3 * A_INNER + M_WIDTH:3 * A_INNER + 2 * M_WIDTH])
        yield
        for h in range(A_HEADS):
            sl = slice(h * A_HDIM, (h + 1) * A_HDIM)
            xh = xc_s[slot, :, sl].astype(BF16)
            uh = u_s[8:8 + tt, sl].astype(BF16)
            base = h * 3 * A_HDIM
            qkv_s[slot, :, base:base + A_HDIM] = _dot(xh, wq_ref[h]).astype(BF16)
            qkv_s[slot, :, base + A_HDIM:base + 2 * A_HDIM] = (
                _dot(xh, wk_ref[h]) * (A_HDIM ** -0.5)).astype(BF16)
            qkv_s[slot, :, base + 2 * A_HDIM:base + 3 * A_HDIM] = _dot(uh, wv_ref[h]).astype(BF16)
            kt = (_dot_nt(wkt_ref[h], xh) * (A_HDIM ** -0.5)).astype(BF16)
            for c in range(nsub):
                kt_s[slot, h, c] = kt[:, c * A_CHUNK:(c + 1) * A_CHUNK]
            yield
        gates_s[slot] = _dot(qkv_s[slot], wif_ref[...]) + bif_ref[...]
        yield

    def step(slot):
        if n_dec_seq:
            first = slot == 1
            heads = (0, 1) if first else (2, 3)
            _dec_mlstm_body(jnp.minimum(t // 2, n_dec_seq - 1), heads, first, *dec_in, *dec_out)
        pending = [stage2(1 - slot), stage1(slot)]
        while pending:
            for gen in list(pending):
                try:
                    next(gen)
                except StopIteration:
                    pending.remove(gen)
        h_next = (_rms_scale(x_ref[0]) * gpre_ref[...]).astype(BF16)
        out = _dot(ymix_s[...], wout_ref[...])
        h_s[...] = h_next
        x1_ref[0] = xp_ref[0] + _rms_scale(out) * gpost_ref[...]

    for s in range(2):
        pl.when(parity == s)(functools.partial(step, s))

    @pl.when(jnp.logical_and(pos1 == nt - 1, t > 0))
    def _():
        conv_out[0, 0] = u_s[tt + 5:tt + 8, :]

    u_s[0:8, :] = u_s[tt:tt + 8, :]

    @pl.when(jnp.logical_and(pos2 == nt - 1, t > 1))
    def _():
        for h in range(A_HEADS):
            c_out[0, 0, h] = c_s[h].T
        n_out[0, 0] = n_s[0:A_HEADS, :]
        m_out[0] = m_s[...]


def _layer0_prompt_pipelined(x, g_pre, w_in, conv_w, conv_b, wq, wk, wkt, wv, wif, bif, ghn, skip, mkv_bf,
                             w_out, g_post, dec_job=None):
    b, s, _ = x.shape
    tt = min(L0_TILE, s)
    nt = s // tt
    ntiles = b * nt
    a_in = w_in.shape[1]
    nsub = tt // A_CHUNK

    def cur(t):
        t1 = jnp.minimum(t, ntiles - 1)
        return (t1 // nt, t1 % nt, 0)

    def prev(t):
        t2 = jnp.maximum(t - 2, 0)
        return (t2 // nt, t2 % nt, 0)

    def prev_b(t):
        return jnp.maximum(t - 2, 0) // nt

    in_specs = [
        pl.BlockSpec((1, tt, D_MODEL), cur),
        pl.BlockSpec((1, tt, D_MODEL), prev),
        _const_spec((1, D_MODEL)),
        _const_spec((D_MODEL, a_in)),
        _const_spec((CONV_W, A_INNER)),
        _const_spec((1, A_INNER)),
        _const_spec((A_HEADS, A_HDIM, A_HDIM)),
        _const_spec((A_HEADS, A_HDIM, A_HDIM)),
        _const_spec((A_HEADS, A_HDIM, A_HDIM)),
        _const_spec((A_HEADS, A_HDIM, A_HDIM)),
        _const_spec((3 * A_INNER, LANES)),
        _const_spec((1, LANES)),
        _const_spec((1, A_INNER)),
        _const_spec((1, A_INNER)),
        pl.BlockSpec((1, N_MEM, 2 * M_WIDTH), lambda t: (prev_b(t), 0, 0)),
        _const_spec((A_INNER + M_WIDTH, D_MODEL)),
        _const_spec((1, D_MODEL)),
    ]
    out_specs = [
        pl.BlockSpec((1, tt, D_MODEL), prev),
        pl.BlockSpec((1, 1, CONV_W - 1, A_INNER), lambda t: (0, prev_b(t), 0, 0)),
        pl.BlockSpec((1, 1, A_HEADS, A_HDIM, A_HDIM), lambda t: (0, prev_b(t), 0, 0, 0)),
        pl.BlockSpec((1, 1, A_HEADS, A_HDIM), lambda t: (0, prev_b(t), 0, 0)),
        pl.BlockSpec((1, 8, LANES), lambda t: (prev_b(t), 0, 0)),
    ]
    out_shape = [
        jax.ShapeDtypeStruct((b, s, D_MODEL), F32),
        jax.ShapeDtypeStruct((1, b, CONV_W - 1, A_INNER), F32),
        jax.ShapeDtypeStruct((1, b, A_HEADS, A_HDIM, A_HDIM), F32),
        jax.ShapeDtypeStruct((1, b, A_HEADS, A_HDIM), F32),
        jax.ShapeDtypeStruct((b, 8, LANES), F32),
    ]
    scratch = [
        pltpu.VMEM((tt, D_MODEL), BF16),
        pltpu.VMEM((tt + 8, A_INNER), F32),
        pltpu.VMEM((tt, A_INNER + M_WIDTH), BF16),
        pltpu.VMEM((2, tt, A_INNER), F32),
        pltpu.VMEM((2, tt, A_INNER), F32),
        pltpu.VMEM((2, tt, A_INNER), F32),
        pltpu.VMEM((2, tt, M_WIDTH), BF16),
        pltpu.VMEM((2, tt, M_WIDTH), F32),
        pltpu.VMEM((2, tt, 3 * A_INNER), BF16),
        pltpu.VMEM((2, A_HEADS, nsub, A_HDIM, A_CHUNK), BF16),
        pltpu.VMEM((2, tt, LANES), F32),
        pltpu.VMEM((A_HEADS, A_HDIM, A_HDIM), F32),
        pltpu.VMEM((8, A_HDIM), F32),
        pltpu.VMEM((8, LANES), F32),
    ]
    dec_args, dec_specs, dec_out_specs, dec_out_shapes, n_dec_seq = [], [], [], [], 0
    if dec_job is not None:
        n_dec_seq = dec_job[0].shape[0]
        dec_args, dec_specs, dec_out_specs, dec_out_shapes = _dec_mlstm_specs(
            *dec_job, seq_index=lambda t: jnp.minimum(t // 2, n_dec_seq - 1))
    return pl.pallas_call(
        functools.partial(_l0p_kernel, nt, n_dec_seq),
        grid=(ntiles + 2,),
        in_specs=in_specs + dec_specs,
        out_specs=out_specs + dec_out_specs,
        out_shape=out_shape + dec_out_shapes,
        scratch_shapes=scratch,
        compiler_params=pltpu.CompilerParams(
            dimension_semantics=("arbitrary",), vmem_limit_bytes=VMEM_LIMIT),
        name="layer0_prompt",
    )(x, x, g_pre, w_in, conv_w, conv_b, wq, wk, wkt, wv, wif, bif, ghn, skip, mkv_bf, w_out, g_post,
      *dec_args)


def _rope_cols(x, cos, sin_signed):
    outs = []
    for cblk in range(x.shape[1] // B_HDIM):
        xb = x[:, cblk * B_HDIM:(cblk + 1) * B_HDIM]
        outs.append(xb * cos + pltpu.roll(xb, B_HDIM // 2, 1) * sin_signed)
    return jnp.concatenate(outs, axis=-1)


def _l1a_kernel(n_dec_in, *refs):
    x_ref, gkv_ref, gpre_ref, wkv_ref, win_ref, cos_ref, sin_ref = refs[0:7]
    dec_in = refs[7:7 + n_dec_in]
    (q0_ref, q1_ref, q2_ref, k0_ref, k1_ref, k2_ref, v0_ref, v1_ref, v2_ref,
     zg_ref, qm_ref, zm_ref, w0_ref, w1_ref, w2_ref) = refs[7 + n_dec_in:22 + n_dec_in]
    dec_out = refs[22 + n_dec_in:]
    tt = x_ref.shape[1]
    xn = _rms_scale(x_ref[0])
    hk = (xn * gkv_ref[...]).astype(BF16)
    hq = (xn * gpre_ref[...]).astype(BF16)
    cos = cos_ref[...]
    sin = sin_ref[...]
    q_refs = (q0_ref, q1_ref, q2_ref)
    k_refs = (k0_ref, k1_ref, k2_ref)
    v_refs = (v0_ref, v1_ref, v2_ref)
    w_refs = (w0_ref, w1_ref, w2_ref)
    for g in (2, 1, 0):
        d = B_GROUPS[g][1]
        kf = _rope_cols(_dot(hk, wkv_ref[:, g * 2 * B_WIDTH:g * 2 * B_WIDTH + B_WIDTH]), cos, sin)
        vf = _dot(hk, wkv_ref[:, g * 2 * B_WIDTH + B_WIDTH:(g + 1) * 2 * B_WIDTH])
        qf = _rope_cols(_dot(hq, win_ref[:, g * B_WIDTH:(g + 1) * B_WIDTH]), cos, sin)
        wr = w_refs[g]
        wrows = wr.shape[1]
        wr[0] = _rows_to_kv_heads(kf[tt - wrows:, :], vf[tt - wrows:, :])
        for val, ref in ((qf.astype(BF16), q_refs[g]), (kf.astype(BF16), k_refs[g]), (vf.astype(BF16), v_refs[g])):
            if d == 1:
                ref[0, 0] = val
            else:
                ref[0] = jnp.swapaxes(val.reshape(tt // d, d, val.shape[1]), 0, 1)
        if g == 2 and n_dec_in:
            _dec_attn_kernel(*dec_in, *dec_out)
    qoff = N_GROUPS * B_WIDTH
    zg_ref[0] = _dot(hq, win_ref[:, qoff:qoff + B_WIDTH]).astype(BF16)
    qm_ref[0] = _dot(hq, win_ref[:, qoff + B_WIDTH:qoff + B_WIDTH + M_WIDTH]).astype(BF16)
    zm_ref[0] = _dot(hq, win_ref[:, qoff + B_WIDTH + M_WIDTH:qoff + B_WIDTH + 2 * M_WIDTH]).astype(BF16)


def _layer1_proj_prompt(x1, g_kv, g_pre, wkv, win, cos_t, sin_t, dec_job=None):
    b, s, _ = x1.shape
    tt = min(TOK_TILE, s)
    nt = s // tt
    tile = lambda bb, i: (bb, i, 0)
    in_specs = [
        pl.BlockSpec((1, tt, D_MODEL), tile),
        _const_spec((1, D_MODEL)),
        _const_spec((1, D_MODEL)),
        _const_spec(wkv.shape),
        _const_spec(win.shape),
        pl.BlockSpec((tt, B_HDIM), lambda bb, i: (i, 0)),
        pl.BlockSpec((tt, B_HDIM), lambda bb, i: (i, 0)),
    ]
    qkv_specs, qkv_shapes = [], []
    for _ in range(3):
        for (_, d) in B_GROUPS:
            qkv_specs.append(pl.BlockSpec((1, d, tt // d, B_WIDTH), lambda bb, i: (bb, 0, i, 0)))
            qkv_shapes.append(jax.ShapeDtypeStruct((b, d, s // d, B_WIDTH), BF16))
    gate_specs = [pl.BlockSpec((1, tt, B_WIDTH), tile)] * 3
    gate_shapes = [jax.ShapeDtypeStruct((b, s, B_WIDTH), BF16)] * 3
    win_specs, win_shapes = [], []
    for (w, _) in B_GROUPS:
        wr = min(w, s)
        rows = min(wr, tt)
        nblk = wr // rows
        win_specs.append(pl.BlockSpec(
            (1, rows, 2 * B_HEADS, B_HDIM),
            functools.partial(lambda bb, i, nb: (bb, jnp.maximum(i - (nt - nb), 0), 0, 0), nb=nblk)))
        win_shapes.append(jax.ShapeDtypeStruct((b, wr, 2 * B_HEADS, B_HDIM), F32))
    dec_args, dec_specs, dec_out_specs, dec_out_shapes = [], [], [], []
    if dec_job is not None:
        dec_args, dec_specs, dec_out_specs, dec_out_shapes = _dec_attn_specs(
            *dec_job, seq_index=lambda bb, i: bb * nt + i)
    return pl.pallas_call(
        functools.partial(_l1a_kernel, len(dec_args)),
        grid=(b, nt),
        in_specs=in_specs + dec_specs,
        out_specs=qkv_specs + gate_specs + win_specs + dec_out_specs,
        out_shape=qkv_shapes + gate_shapes + win_shapes + dec_out_shapes,
        compiler_params=pltpu.CompilerParams(
            dimension_semantics=("arbitrary", "arbitrary"), vmem_limit_bytes=VMEM_LIMIT),
        name="layer1_proj_prompt",
    )(x1, g_kv, g_pre, wkv, win, cos_t, sin_t, *dec_args)


def _cols_to_lanes(cols):
    t = cols[0].shape[0]
    lane = lax.broadcasted_iota(jnp.int32, (t, LANES), 1)
    acc = jnp.zeros((t, LANES), F32)
    for h, cvec in enumerate(cols):
        acc = jnp.where(lane == h, cvec, acc)
    return acc


def _band_attn_kernel(q_ref, kc_ref, kp_ref, vc_ref, vp_ref, o_ref, lse_ref):
    nres, tq = q_ref.shape[1:3]
    nsb = tq // ATT_BLK
    j = pl.program_id(2)
    row = lax.broadcasted_iota(jnp.int32, (ATT_BLK, 2 * ATT_BLK), 0)
    col = lax.broadcasted_iota(jnp.int32, (ATT_BLK, 2 * ATT_BLK), 1)
    band = jnp.logical_and(col >= row, col <= row + ATT_BLK)
    first_pen = jnp.where(col < ATT_BLK, jnp.where(j > 0, 0.0, -jnp.inf), 0.0)
    scale = B_HDIM ** -0.5
    blocks = [(r, sb) for r in range(nres) for sb in range(nsb)]
    for g0 in range(0, len(blocks), ATT_GROUP):
        grp = blocks[g0:g0 + ATT_GROUP]
        qs, ks, vs, pens = [], [], [], []
        for r, sb in grp:
            rs = slice(sb * ATT_BLK, (sb + 1) * ATT_BLK)
            ps = slice((sb - 1) * ATT_BLK, sb * ATT_BLK)
            for h in range(B_HEADS):
                hs = slice(h * B_HDIM, (h + 1) * B_HDIM)
                qs.append(q_ref[0, r, rs, hs])
                kp = kp_ref[0, r, :, hs] if sb == 0 else kc_ref[0, r, ps, hs]
                vp = vp_ref[0, r, :, hs] if sb == 0 else vc_ref[0, r, ps, hs]
                ks.append(jnp.concatenate([kp, kc_ref[0, r, rs, hs]], axis=0))
                vs.append(jnp.concatenate([vp, vc_ref[0, r, rs, hs]], axis=0))
                pens.append(sb == 0)
        q3 = jnp.stack(qs)
        k3 = jnp.stack(ks)
        v3 = jnp.stack(vs)
        s = jnp.einsum('uqd,ukd->uqk', q3, k3, preferred_element_type=F32)
        s = jnp.stack([s[u] + first_pen if pens[u] else s[u] for u in range(len(pens))])
        s = jnp.where(band[None], s, -jnp.inf)
        mx = jnp.max(s, axis=-1, keepdims=True)
        p = jnp.exp2((s - mx) * (scale * LOG2E))
        l = jnp.sum(p, axis=-1, keepdims=True)
        o = jnp.einsum('uqk,ukd->uqd', p.astype(BF16), v3, preferred_element_type=F32) / l
        lse = mx * scale + jnp.log(l)
        for i, (r, sb) in enumerate(grp):
            rs = slice(sb * ATT_BLK, (sb + 1) * ATT_BLK)
            for h in range(B_HEADS):
                o_ref[0, r, rs, h * B_HDIM:(h + 1) * B_HDIM] = o[i * B_HEADS + h].astype(BF16)
            lse_ref[0, r, rs, :] = _cols_to_lanes([lse[i * B_HEADS + h] for h in range(B_HEADS)])


def _band_attention(q, k, v):
    b, d, ls, _ = q.shape
    tq = min(ATT_TILE, ls)
    nj = ls // tq
    ratio = tq // ATT_BLK
    nres = min(d, ATT_TILE // tq)
    cur = lambda bb, r, j: (bb, r, j, 0)
    prev = lambda bb, r, j: (bb, r, jnp.maximum(j * ratio - 1, 0), 0)
    return pl.pallas_call(
        _band_attn_kernel,
        grid=(b, d // nres, nj),
        in_specs=[pl.BlockSpec((1, nres, tq, B_WIDTH), cur),
                  pl.BlockSpec((1, nres, tq, B_WIDTH), cur),
                  pl.BlockSpec((1, nres, ATT_BLK, B_WIDTH), prev),
                  pl.BlockSpec((1, nres, tq, B_WIDTH), cur),
                  pl.BlockSpec((1, nres, ATT_BLK, B_WIDTH), prev)],
        out_specs=[pl.BlockSpec((1, nres, tq, B_WIDTH), cur),
                   pl.BlockSpec((1, nres, tq, LANES), cur)],
        out_shape=[jax.ShapeDtypeStruct((b, d, ls, B_WIDTH), BF16),
                   jax.ShapeDtypeStruct((b, d, ls, LANES), F32)],
        compiler_params=pltpu.CompilerParams(
            dimension_semantics=("arbitrary", "arbitrary", "arbitrary"), vmem_limit_bytes=VMEM_LIMIT),
        name="band_attention_d%d" % d,
    )(q, k, k, v, v)


def _unpermute(ref):
    d, rows, width = ref.shape[1:]
    if d == 1:
        return ref[0, 0]
    return jnp.swapaxes(ref[0], 0, 1).reshape(d * rows, width)


def _l1c_kernel(x_ref, o0_ref, o1_ref, o2_ref, l0_ref, l1_ref, l2_ref, zg_ref, qm_ref, zm_ref,
                mkv_ref, wout_ref, gpost_ref, y_ref):
    tt = x_ref.shape[1]
    o_refs = (o0_ref, o1_ref, o2_ref)
    l_refs = (l0_ref, l1_ref, l2_ref)
    outs, lses = [], []
    for g, (_, d) in enumerate(B_GROUPS):
        outs.append(_unpermute(o_refs[g]))
        lses.append(_unpermute(l_refs[g])[:, 0:B_HEADS])
    mx = jnp.maximum(jnp.maximum(lses[0], lses[1]), lses[2])
    es = [jnp.exp(l - mx) for l in lses]
    tot = es[0] + es[1] + es[2]
    ws = [(e / tot).astype(BF16) for e in es]
    parts = []
    for h in range(B_HEADS):
        hs = slice(h * B_HDIM, (h + 1) * B_HDIM)
        acc = ws[0][:, h:h + 1] * outs[0][:, hs]
        acc = acc + ws[1][:, h:h + 1] * outs[1][:, hs]
        acc = acc + ws[2][:, h:h + 1] * outs[2][:, hs]
        parts.append(acc)
    ydil = jnp.concatenate(parts, axis=-1)
    ymix = (ydil.astype(F32) * _silu(zg_ref[0].astype(F32))).astype(BF16)
    mk = mkv_ref[0, :, 0:M_WIDTH]
    mv = mkv_ref[0, :, M_WIDTH:2 * M_WIDTH]
    ym = (_mem_attention(qm_ref[0], mk, mv) * _silu(zm_ref[0].astype(F32))).astype(BF16)
    out = _dot(ymix, wout_ref[0:B_WIDTH, :]) + _dot(ym, wout_ref[B_WIDTH:B_WIDTH + M_WIDTH, :])
    y_ref[0] = x_ref[0] + _rms_scale(out) * gpost_ref[...]


def _layer1_out_prompt(x1, os_, ls_, zg, qm, zm, mkv_bf, w_out, g_post):
    b, s, _ = x1.shape
    tt = min(TOK_TILE, s)
    nt = s // tt
    tile = lambda bb, i: (bb, i, 0)
    perm = lambda bb, i: (bb, 0, i, 0)
    in_specs = [pl.BlockSpec((1, tt, D_MODEL), tile)]
    for width in (B_WIDTH, LANES):
        for (_, d) in B_GROUPS:
            in_specs.append(pl.BlockSpec((1, d, tt // d, width), perm))
    in_specs += [pl.BlockSpec((1, tt, B_WIDTH), tile)] * 3
    in_specs += [pl.BlockSpec((1, N_MEM, 2 * M_WIDTH), lambda bb, i: (bb, 0, 0)),
                 _const_spec(w_out.shape), _const_spec((1, D_MODEL))]
    return pl.pallas_call(
        _l1c_kernel,
        grid=(b, nt),
        in_specs=in_specs,
        out_specs=pl.BlockSpec((1, tt, D_MODEL), tile),
        out_shape=jax.ShapeDtypeStruct((b, s, D_MODEL), F32),
        compiler_params=pltpu.CompilerParams(
            dimension_semantics=("arbitrary", "arbitrary"), vmem_limit_bytes=VMEM_LIMIT),
        name="layer1_out_prompt",
    )(x1, *os_, *ls_, zg, qm, zm, mkv_bf, w_out, g_post)


def _rope_tables(pos):
    half = B_HDIM // 2
    inv = ROPE_THETA ** (-jnp.arange(half, dtype=F32) / half)
    ang = pos[:, None] * inv[None, :]
    cos = jnp.cos(ang)
    sin = jnp.sin(ang)
    return jnp.concatenate([cos, cos], axis=-1), jnp.concatenate([-sin, sin], axis=-1)


def _prompt_group(x_prompt, mem_prompt, p, sample=None):
    b, s, _ = x_prompt.shape
    memkv_f, memkv_b = _memkv(mem_prompt.reshape(b * N_MEM, D_MODEL), p['w_mkv'])
    depth = memkv_f.shape[0]
    memkv_b = memkv_b.reshape(depth, b, N_MEM, 2 * M_WIDTH)
    job0 = sample.mlstm_job() if sample is not None else None
    if job0 is not None and b * (s // min(L0_TILE, s)) + 2 < 2 * job0[0].shape[0]:
        job0 = None
    outs0 = _layer0_prompt_pipelined(
        x_prompt, p['g_pre'][0:1], p['w_in_a'][0], p['conv_w_a'][0], p['conv_b_a'], p['w_q_a'][0],
        p['w_k_a'][0], jnp.swapaxes(p['w_k_a'][0], 1, 2), p['w_v_a'][0], p['w_if_a'], p['b_if_a'],
        p['g_hn_a'], p['skip_a'], memkv_b[0], p['w_out_a'][0], p['g_post'][0:1], job0)
    x1, conv_p, c_p, n_p, m_pad = outs0[0:5]
    if sample is not None:
        sample.after_mlstm(outs0[5:10] if job0 is not None else _dec_mlstm(*sample.mlstm_job()))
    cos_t, sin_t = _rope_tables(jnp.arange(s, dtype=F32))
    job1 = sample.attn_job if sample is not None else None
    if job1 is not None and b * (s // min(TOK_TILE, s)) != job1[0].shape[0]:
        job1 = None
    outs = _layer1_proj_prompt(x1, p['g_kv'], p['g_pre'][1:2], p['w_kv_b'], p['w_in_b'][0], cos_t, sin_t,
                               job1)
    if sample is not None:
        sample.after_attn(tuple(outs[15:17]) if job1 is not None else _dec_attn(*sample.attn_job))
    qs, ks, vs = outs[0:3], outs[3:6], outs[6:9]
    zg, qm, zm = outs[9:12]
    wins = outs[12:15]
    os_, ls_ = [], []
    for g in range(N_GROUPS):
        o, l = _band_attention(qs[g], ks[g], vs[g])
        os_.append(o)
        ls_.append(l)
    y = _layer1_out_prompt(x1, os_, ls_, zg, qm, zm, memkv_b[1], p['w_out_b'][0], p['g_post'][1:2])
    m_p = m_pad[:, 0:A_HEADS, 0][None]
    wins = [w.reshape(b, w.shape[1], 2, B_HEADS, B_HDIM) for w in wins]
    memkv_p = memkv_f.reshape(depth, b, N_MEM, 2, M_HEADS, M_HDIM)
    return y, conv_p, c_p, n_p, m_p, wins, memkv_p


def _prep_params(g_pre, g_post, w_in_a, conv_w_a, conv_b_a, w_q_a, w_k_a, w_v_a, w_if_a, b_if_a,
                 g_hn_a, skip_a, w_out_a, g_kv, w_kv_b, w_in_b, w_out_b, w_mkv):
    wif = jnp.pad(w_if_a[0], ((0, 0), (0, LANES - 2 * A_HEADS))).astype(BF16)
    bif = jnp.pad(b_if_a[0], (0, LANES - 2 * A_HEADS))[None, :]
    return {
        'g_pre': g_pre, 'g_post': g_post,
        'w_in_a': w_in_a.astype(BF16), 'conv_w_a': conv_w_a, 'conv_b_a': conv_b_a,
        'w_q_a': w_q_a.astype(BF16), 'w_k_a': w_k_a.astype(BF16), 'w_v_a': w_v_a.astype(BF16),
        'w_if_a': wif, 'b_if_a': bif, 'g_hn_a': g_hn_a, 'skip_a': skip_a,
        'w_out_a': w_out_a.astype(BF16), 'g_kv': g_kv[None, :], 'w_kv_b': w_kv_b.astype(BF16),
        'w_in_b': w_in_b.astype(BF16), 'w_out_b': w_out_b.astype(BF16), 'w_mkv': w_mkv.astype(BF16),
    }


def _dec_l0_proj_kernel(x_ref, gpre_ref, win_ref, cst_ref, convw_ref, convb_ref, wq_ref, wk_ref, wv_ref,
                        wif_ref, bif_ref,
                        q_ref, k_ref, v_ref, gates_ref, xc_ref, opre_ref, zg_ref, qm_ref, zm_ref, cnew_ref):
    h = (_rms_scale(x_ref[:, 0, :]) * gpre_ref[...]).astype(BF16)
    u = _dot(h, win_ref[:, 0:A_INNER])
    opre_ref[...] = _dot(h, win_ref[:, A_INNER:2 * A_INNER])
    zg_ref[...] = _dot(h, win_ref[:, 2 * A_INNER:3 * A_INNER])
    qm_ref[...] = _rows_to_heads(_dot(h, win_ref[:, 3 * A_INNER:3 * A_INNER + M_WIDTH]))
    zm_ref[...] = _dot(h, win_ref[:, 3 * A_INNER + M_WIDTH:3 * A_INNER + 2 * M_WIDTH])
    cw = convw_ref[...]
    xc = convb_ref[...] + cst_ref[0, :, 0, :] * cw[0:1, :]
    xc = xc + cst_ref[0, :, 1, :] * cw[1:2, :]
    xc = xc + cst_ref[0, :, 2, :] * cw[2:3, :]
    xc = xc + u * cw[3:4, :]
    xc = _silu(xc)
    xc_ref[...] = xc
    cnew_ref[0, :, 0, :] = cst_ref[0, :, 1, :]
    cnew_ref[0, :, 1, :] = cst_ref[0, :, 2, :]
    cnew_ref[0, :, 2, :] = u
    qs, ks, vs, cat = [], [], [], []
    for hd in range(A_HEADS):
        sl = slice(hd * A_HDIM, (hd + 1) * A_HDIM)
        xh = xc[:, sl].astype(BF16)
        qh = _dot(xh, wq_ref[hd])
        kh = _dot(xh, wk_ref[hd]) * (A_HDIM ** -0.5)
        vh = _dot(u[:, sl].astype(BF16), wv_ref[hd])
        qs.append(qh)
        ks.append(kh)
        vs.append(vh)
        cat += [qh.astype(BF16), kh.astype(BF16), vh.astype(BF16)]
    q_ref[...] = jnp.concatenate(qs, axis=-1)
    k_ref[...] = jnp.concatenate(ks, axis=-1)
    v_ref[...] = jnp.concatenate(vs, axis=-1)
    gates_ref[...] = _dot(jnp.concatenate(cat, axis=-1), wif_ref[...]) + bif_ref[...]


def _whole(shape):
    nd = len(shape)
    return pl.BlockSpec(shape, lambda *_: (0,) * nd)


def _dec_l0_proj(x, g_pre, w_in, cst, conv_w, conv_b, wq, wk, wv, wif, bif):
    nb = x.shape[0]
    args = (x, g_pre, w_in, cst, conv_w, conv_b, wq, wk, wv, wif, bif)
    f = lambda *s: jax.ShapeDtypeStruct(s, F32)
    out_shape = [f(nb, A_INNER), f(nb, A_INNER), f(nb, A_INNER), f(nb, LANES), f(nb, A_INNER), f(nb, A_INNER),
                 f(nb, A_INNER), f(nb, M_HEADS, M_HDIM), f(nb, M_WIDTH), f(1, nb, CONV_W - 1, A_INNER)]
    return pl.pallas_call(
        _dec_l0_proj_kernel,
        grid=(1,),
        in_specs=[_whole(a.shape) for a in args],
        out_specs=[_whole(o.shape) for o in out_shape],
        out_shape=out_shape,
        compiler_params=pltpu.CompilerParams(dimension_semantics=("arbitrary",), vmem_limit_bytes=VMEM_LIMIT),
        name="dec_l0_proj",
    )(*args)


def _row_to_col(row, eye):
    return jnp.sum(jnp.where(eye, row, 0.0), axis=-1, keepdims=True)


def _col_to_row(colv, eye):
    return jnp.sum(jnp.where(eye, colv, 0.0), axis=0, keepdims=True)


def _dec_mem_attention(q, kv_ref_view):
    kk = kv_ref_view[:, 0]
    vv = kv_ref_view[:, 1]
    s = jnp.sum(kk * (q * (M_HDIM ** -0.5))[None], axis=-1, keepdims=True)
    mx = jnp.max(s, axis=0, keepdims=True)
    p = jnp.exp(s - mx)
    return jnp.sum(p * vv, axis=0) / jnp.sum(p, axis=0)


def _dec_mlstm_body(b, heads, with_mem, q_ref, k_ref, v_ref, gates_ref, m_ref, c_ref, n_ref, qm_ref, kv_ref,
                    hs_ref, c_out, n_out, m_out, ym_ref):
    rb = pl.ds(b, 1)
    g = gates_ref[rb, :]
    mrow = m_ref[0, rb, :]
    r = lax.broadcasted_iota(jnp.int32, (A_HDIM, A_HDIM), 0)
    c = lax.broadcasted_iota(jnp.int32, (A_HDIM, A_HDIM), 1)
    eye = r == c
    sl = {h: slice(h * A_HDIM, (h + 1) * A_HDIM) for h in heads}
    qh = {h: q_ref[rb, sl[h]] for h in heads}
    kh = {h: k_ref[rb, sl[h]] for h in heads}
    vh = {h: v_ref[rb, sl[h]] for h in heads}
    c_old = {h: c_ref[0, 0, h] for h in heads}
    n_old = {h: n_ref[0, 0, h:h + 1, :] for h in heads}
    li = {h: g[:, h:h + 1] for h in heads}
    lf = {h: _log_sigmoid(g[:, 4 + h:5 + h]) for h in heads}
    m_old = {h: mrow[:, h:h + 1] for h in heads}
    cq = {h: jnp.sum(c_old[h] * qh[h], axis=-1, keepdims=True) for h in heads}
    v_col = {h: _row_to_col(vh[h], eye) for h in heads}
    nq = {h: jnp.sum(n_old[h] * qh[h], axis=-1, keepdims=True) for h in heads}
    qk = {h: jnp.sum(qh[h] * kh[h], axis=-1, keepdims=True) for h in heads}
    inter = {h: lf[h] + m_old[h] for h in heads}
    m_new = {h: jnp.maximum(inter[h], li[h]) for h in heads}
    ws = {h: jnp.exp(li[h] - m_new[h]) for h in heads}
    dec = {h: jnp.exp(inter[h] - m_new[h]) for h in heads}
    sc = {h: qk[h] * ws[h] for h in heads}
    den = {h: sc[h] + dec[h] * nq[h] for h in heads}
    h_col = {h: (sc[h] * v_col[h] + dec[h] * cq[h]) / jnp.maximum(jnp.abs(den[h]), jnp.exp(-m_new[h]))
             for h in heads}
    for h in heads:
        c_out[0, 0, h] = dec[h] * c_old[h] + (ws[h] * v_col[h]) * kh[h]
        n_out[0, 0, h:h + 1, :] = dec[h] * n_old[h] + ws[h] * kh[h]
        m_out[0, h:h + 1, :] = jnp.broadcast_to(m_new[h], (1, LANES))
    for h in heads:
        hs_ref[0, :, sl[h]] = _col_to_row(h_col[h], eye)
    if with_mem:
        ym_ref[0] = _dec_mem_attention(qm_ref[0], kv_ref.at[0, 0])


def _dec_mlstm_kernel(*refs):
    _dec_mlstm_body(pl.program_id(0), range(A_HEADS), True, *refs)


def _dec_mlstm_specs(q, k, v, gates, m_in, state_c, state_n, qm3, cache_mem_kv, seq_index):
    nb = q.shape[0]

    def at(*tail, lead=()):
        return lambda *idx: lead + (seq_index(*idx),) + tail

    in_specs = [_whole(q.shape), _whole(k.shape), _whole(v.shape), _whole(gates.shape), _whole(m_in.shape),
                pl.BlockSpec((1, 1, A_HEADS, A_HDIM, A_HDIM), at(0, 0, 0, lead=(0,))),
                pl.BlockSpec((1, 1, A_HEADS, A_HDIM), at(0, 0, lead=(0,))),
                pl.BlockSpec((1, M_HEADS, M_HDIM), at(0, 0)),
                pl.BlockSpec((1, 1, N_MEM, 2, M_HEADS, M_HDIM), at(0, 0, 0, 0, lead=(0,)))]
    out_specs = [pl.BlockSpec((1, 1, A_INNER), at(0, 0)),
                 pl.BlockSpec((1, 1, A_HEADS, A_HDIM, A_HDIM), at(0, 0, 0, lead=(0,))),
                 pl.BlockSpec((1, 1, A_HEADS, A_HDIM), at(0, 0, lead=(0,))),
                 pl.BlockSpec((1, A_HEADS, LANES), at(0, 0)),
                 pl.BlockSpec((1, M_HEADS, M_HDIM), at(0, 0))]
    out_shapes = [jax.ShapeDtypeStruct((nb, 1, A_INNER), F32),
                  jax.ShapeDtypeStruct(state_c.shape, F32),
                  jax.ShapeDtypeStruct(state_n.shape, F32),
                  jax.ShapeDtypeStruct((nb, A_HEADS, LANES), F32),
                  jax.ShapeDtypeStruct((nb, M_HEADS, M_HDIM), F32)]
    return [q, k, v, gates, m_in, state_c, state_n, qm3, cache_mem_kv], in_specs, out_specs, out_shapes


def _dec_mlstm(*job):
    args, in_specs, out_specs, out_shapes = _dec_mlstm_specs(*job, seq_index=lambda b: b)
    return pl.pallas_call(
        _dec_mlstm_kernel,
        grid=(args[0].shape[0],),
        in_specs=in_specs,
        out_specs=out_specs,
        out_shape=out_shapes,
        compiler_params=pltpu.CompilerParams(dimension_semantics=("arbitrary",), vmem_limit_bytes=VMEM_LIMIT),
        name="dec_mlstm",
    )(*args)


def _dec_mid_kernel(hs_ref, opre_ref, xc_ref, zg_ref, ym_ref, zm_ref, x_ref, ghn_ref, skip_ref, wout_ref,
                    gpost_ref, gkv_ref, gpre_ref, wkv_ref, win_ref, cos_ref, sin_ref,
                    x1_ref, q_ref, k_ref, v_ref, zg1_ref, qm1_ref, zm1_ref, win0_ref, win1_ref, win2_ref):
    hh = _sigmoid(opre_ref[...]) * hs_ref[:, 0, :]
    parts = []
    for h in range(A_HEADS):
        v = hh[:, h * A_HDIM:(h + 1) * A_HDIM]
        mu = jnp.mean(v, axis=-1, keepdims=True)
        var = jnp.mean(jnp.square(v - mu), axis=-1, keepdims=True)
        parts.append((v - mu) * lax.rsqrt(var + EPS))
    y = jnp.concatenate(parts, axis=-1) * ghn_ref[...] + skip_ref[...] * xc_ref[...]
    ymix = (y * _silu(zg_ref[...])).astype(BF16)
    ym = (_heads_to_rows(ym_ref) * _silu(zm_ref[...])).astype(BF16)
    out = _dot(ymix, wout_ref[0:A_INNER, :]) + _dot(ym, wout_ref[A_INNER:A_INNER + M_WIDTH, :])
    x1 = x_ref[:, 0, :] + _rms_scale(out) * gpost_ref[...]
    x1_ref[...] = x1
    xn = _rms_scale(x1)
    hk = (xn * gkv_ref[...]).astype(BF16)
    hq = (xn * gpre_ref[...]).astype(BF16)
    cos = cos_ref[...]
    sin = sin_ref[...]
    ks, vs = [], []
    for g in range(N_GROUPS):
        ks.append(_rope_cols(_dot(hk, wkv_ref[:, g * 2 * B_WIDTH:g * 2 * B_WIDTH + B_WIDTH]), cos, sin))
        vs.append(_dot(hk, wkv_ref[:, g * 2 * B_WIDTH + B_WIDTH:(g + 1) * 2 * B_WIDTH]))
    k_ref[...] = _rows_to_heads(jnp.concatenate(ks, axis=-1))
    v_ref[...] = _rows_to_heads(jnp.concatenate(vs, axis=-1))
    for g, wref in enumerate((win0_ref, win1_ref, win2_ref)):
        wref[...] = _rows_to_kv_heads(ks[g], vs[g])
    qoff = N_GROUPS * B_WIDTH
    q_ref[...] = _rows_to_heads(_rope_cols(_dot(hq, win_ref[:, 0:qoff]), cos, sin))
    zg1_ref[...] = _dot(hq, win_ref[:, qoff:qoff + B_WIDTH])
    qm1_ref[...] = _rows_to_heads(_dot(hq, win_ref[:, qoff + B_WIDTH:qoff + B_WIDTH + M_WIDTH]))
    zm1_ref[...] = _dot(hq, win_ref[:, qoff + B_WIDTH + M_WIDTH:qoff + B_WIDTH + 2 * M_WIDTH])


def _dec_mid(hs, opre, xc, zg, ym, zm, x, ghn, skip, w_out, g_post, g_kv, g_pre, wkv, win, cos, sin):
    nb = x.shape[0]
    args = (hs, opre, xc, zg, ym, zm, x, ghn, skip, w_out, g_post, g_kv, g_pre, wkv, win, cos, sin)
    f = lambda *s: jax.ShapeDtypeStruct(s, F32)
    gh = N_GROUPS * B_HEADS
    out_shape = [f(nb, D_MODEL), f(nb, gh, B_HDIM), f(nb, gh, B_HDIM), f(nb, gh, B_HDIM),
                 f(nb, B_WIDTH), f(nb, M_HEADS, M_HDIM), f(nb, M_WIDTH)] + [f(nb, 2 * B_HEADS, B_HDIM)] * N_GROUPS
    return pl.pallas_call(
        _dec_mid_kernel,
        grid=(1,),
        in_specs=[_whole(a.shape) for a in args],
        out_specs=[_whole(o.shape) for o in out_shape],
        out_shape=out_shape,
        compiler_params=pltpu.CompilerParams(dimension_semantics=("arbitrary",), vmem_limit_bytes=VMEM_LIMIT),
        name="dec_mid",
    )(*args)


def _dec_attn_kernel(q_ref, kn_ref, vn_ref, w0_ref, w1_ref, w2_ref, qm_ref, kv_ref, ydil_ref, ym_ref):
    w_refs = (w0_ref, w1_ref, w2_ref)
    scale = B_HDIM ** -0.5
    groups = range(N_GROUPS)
    hsl = [slice(g * B_HEADS, (g + 1) * B_HEADS) for g in groups]
    q = [q_ref[0, hsl[g], :] * scale for g in groups]
    s_c = [jnp.sum(w_refs[g][0, :, 0] * q[g][None], axis=-1, keepdims=True) for g in groups]
    s_n = [jnp.sum(kn_ref[0, hsl[g], :] * q[g], axis=-1, keepdims=True) for g in groups]
    mxs = [jnp.maximum(jnp.max(s_c[g], axis=0), s_n[g]) for g in groups]
    p_c = [jnp.exp(s_c[g] - mxs[g][None]) for g in groups]
    p_n = [jnp.exp(s_n[g] - mxs[g]) for g in groups]
    ls = [jnp.sum(p_c[g], axis=0) + p_n[g] for g in groups]
    outs = [(jnp.sum(p_c[g] * w_refs[g][0, :, 1], axis=0) + p_n[g] * vn_ref[0, hsl[g], :]) / ls[g]
            for g in groups]
    lses = [mxs[g] + jnp.log(ls[g]) for g in groups]
    mx = jnp.maximum(jnp.maximum(lses[0], lses[1]), lses[2])
    es = [jnp.exp(l - mx) for l in lses]
    tot = es[0] + es[1] + es[2]
    ydil_ref[0] = (es[0] / tot) * outs[0] + (es[1] / tot) * outs[1] + (es[2] / tot) * outs[2]
    ym_ref[0] = _dec_mem_attention(qm_ref[0], kv_ref.at[0, 0])


def _dec_attn_specs(q4, kn4, vn4, cw0, cw1, cw2, qm3, cache_mem_kv, layer, seq_index):
    nb = q4.shape[0]
    rows = B_GROUPS[0][0]

    def at(*tail):
        return lambda *idx: (seq_index(*idx),) + tail

    win_specs = [pl.BlockSpec((1, rows, 2, B_HEADS, B_HDIM), at(0, 0, 0, 0)),
                 pl.BlockSpec((1, rows, None, 2, B_HEADS, B_HDIM), at(0, 0, 0, 0, 0)),
                 pl.BlockSpec((1, rows, None, 2, B_HEADS, B_HDIM), at(0, 0, 0, 0, 0))]
    in_specs = [pl.BlockSpec((1, N_GROUPS * B_HEADS, B_HDIM), at(0, 0))] * 3 + win_specs + [
        pl.BlockSpec((1, M_HEADS, M_HDIM), at(0, 0)),
        pl.BlockSpec((1, 1, N_MEM, 2, M_HEADS, M_HDIM), lambda *idx: (layer, seq_index(*idx), 0, 0, 0, 0))]
    out_specs = [pl.BlockSpec((1, B_HEADS, B_HDIM), at(0, 0)), pl.BlockSpec((1, M_HEADS, M_HDIM), at(0, 0))]
    out_shapes = [jax.ShapeDtypeStruct((nb, B_HEADS, B_HDIM), F32),
                  jax.ShapeDtypeStruct((nb, M_HEADS, M_HDIM), F32)]
    return [q4, kn4, vn4, cw0, cw1, cw2, qm3, cache_mem_kv], in_specs, out_specs, out_shapes


def _dec_attn(q4, kn4, vn4, cw0, cw1, cw2, qm3, cache_mem_kv, layer):
    args, in_specs, out_specs, out_shapes = _dec_attn_specs(
        q4, kn4, vn4, cw0, cw1, cw2, qm3, cache_mem_kv, layer, seq_index=lambda b: b)
    return pl.pallas_call(
        _dec_attn_kernel,
        grid=(q4.shape[0],),
        in_specs=in_specs,
        out_specs=out_specs,
        out_shape=out_shapes,
        compiler_params=pltpu.CompilerParams(dimension_semantics=("arbitrary",), vmem_limit_bytes=VMEM_LIMIT),
        name="dec_attn",
    )(*args)


def _dec_out_kernel(ydil_ref, zg_ref, ym_ref, zm_ref, x_ref, wout_ref, gpost_ref, y_ref):
    ymix = (_heads_to_rows(ydil_ref) * _silu(zg_ref[...])).astype(BF16)
    ym = (_heads_to_rows(ym_ref) * _silu(zm_ref[...])).astype(BF16)
    out = _dot(ymix, wout_ref[0:B_WIDTH, :]) + _dot(ym, wout_ref[B_WIDTH:B_WIDTH + M_WIDTH, :])
    y_ref[:, 0, :] = x_ref[...] + _rms_scale(out) * gpost_ref[...]


def _dec_out(ydil, zg, ym, zm, x1, w_out, g_post):
    args = (ydil, zg, ym, zm, x1, w_out, g_post)
    out_shape = (x1.shape[0], 1, x1.shape[1])
    return pl.pallas_call(
        _dec_out_kernel,
        grid=(1,),
        in_specs=[_whole(a.shape) for a in args],
        out_specs=_whole(out_shape),
        out_shape=jax.ShapeDtypeStruct(out_shape, F32),
        compiler_params=pltpu.CompilerParams(dimension_semantics=("arbitrary",), vmem_limit_bytes=VMEM_LIMIT),
        name="dec_out",
    )(*args)


class _SampleGroup:
    def __init__(self, x_sample, state_conv, state_c, state_n, state_m, cache_wins, cache_mem_kv, p):
        self.p = p
        self.nb = nb = x_sample.shape[0]
        self.cache_wins = cache_wins
        self.cache_mem_kv = cache_mem_kv
        self.x = x_sample
        q, k, v, gates, self.xc, self.opre, self.zg, qm3, self.zm, self.conv_s = _dec_l0_proj(
            x_sample, p['g_pre'][0:1], p['w_in_a'][0], state_conv, p['conv_w_a'][0], p['conv_b_a'],
            p['w_q_a'][0], p['w_k_a'][0], p['w_v_a'][0], p['w_if_a'], p['b_if_a'])
        self._mlstm_job = (q, k, v, gates, state_m, state_c, state_n, qm3, cache_mem_kv)
        self.attn_job = None

    def mlstm_job(self):
        return self._mlstm_job

    def after_mlstm(self, res):
        p, nb = self.p, self.nb
        hs, self.c_s, self.n_s, m_rows, ym0 = res
        self.m_s = m_rows[:, :, 0][None]
        cos, sin = _rope_tables(PAST_LEN + jnp.arange(1, dtype=F32))
        self.x1, qd, kn, vn, self.zg1, qm1, self.zm1, w0, w1, w2 = _dec_mid(
            hs, self.opre, self.xc, self.zg, ym0, self.zm, self.x,
            p['g_hn_a'], p['skip_a'], p['w_out_a'][0], p['g_post'][0:1], p['g_kv'], p['g_pre'][1:2],
            p['w_kv_b'], p['w_in_b'][0], cos, sin)
        cws = [self.cache_wins[0]]
        for g in (1, 2):
            w, d = B_GROUPS[g]
            cws.append(self.cache_wins[g].reshape(nb, w // d, d, 2, B_HEADS, B_HDIM))
        self.attn_job = (qd, kn, vn, cws[0], cws[1], cws[2], qm1, self.cache_mem_kv, 1)
        self.wins_s = [w.reshape(nb, 1, 2, B_HEADS, B_HDIM) for w in (w0, w1, w2)]

    def after_attn(self, res):
        p = self.p
        ydil, ym1 = res
        self.y = _dec_out(ydil, self.zg1, ym1, self.zm1, self.x1, p['w_out_b'][0], p['g_post'][1:2])

    def outputs(self):
        return self.y, self.conv_s, self.c_s, self.n_s, self.m_s, self.wins_s


def kernel(x_prompt, x_sample, mem_prompt, state_conv, state_C, state_n, state_m, cache_win0, cache_win1,
           cache_win2, cache_mem_kv, g_pre, g_post, w_in_a, conv_w_a, conv_b_a, w_q_a, w_k_a, w_v_a, w_if_a,
           b_if_a, g_hn_a, skip_a, w_out_a, g_kv, w_kv_b, w_in_b, w_out_b, w_mkv):
    p = _prep_params(g_pre, g_post, w_in_a, conv_w_a, conv_b_a, w_q_a, w_k_a, w_v_a, w_if_a, b_if_a,
                     g_hn_a, skip_a, w_out_a, g_kv, w_kv_b, w_in_b, w_out_b, w_mkv)
    sample = _SampleGroup(x_sample, state_conv, state_C, state_n, state_m,
                          (cache_win0, cache_win1, cache_win2), cache_mem_kv, p)
    y_p, conv_p, c_p, n_p, m_p, wins_p, memkv_p = _prompt_group(x_prompt, mem_prompt, p, sample)
    y_s, conv_s, c_s, n_s, m_s, wins_s = sample.outputs()
    return (y_p, y_s, conv_p, c_p, n_p, m_p, wins_p[0], wins_p[1], wins_p[2], memkv_p,
            conv_s, c_s, n_s, m_s, wins_s[0], wins_s[1], wins_s[2])
```

```python
import functools

import jax
import jax.numpy as jnp
from jax import lax
from jax.experimental import pallas as pl
from jax.experimental.pallas import tpu as pltpu

F32 = jnp.float32
BF16 = jnp.bfloat16

D_MODEL = 1024
A_HEADS = 4
A_HDIM = 256
A_INNER = 1024
CONV_W = 4
A_CHUNK = 128
B_GROUPS = ((128, 1), (512, 4), (2048, 16))
N_GROUPS = 3
B_HEADS = 4
B_HDIM = 128
B_WIDTH = 512
N_MEM = 256
M_HEADS = 4
M_HDIM = 128
M_WIDTH = 512
ROPE_THETA = 10000.0
EPS = 1e-6
PAST_LEN = 8192

LANES = 128
TOK_TILE = 512
L0_TILE = 256
L1OUT_TILE = 1024
ATT_BLK = 128
ATT_TILE = 2048
ATT_GROUP = 2
VMEM_LIMIT = 56 * 1024 * 1024

NT_DIMS = (((1,), (1,)), ((), ()))
LOG2E = 1.4426950408889634


def _dot(a, b):
    return jnp.dot(a, b, preferred_element_type=F32)


def _dot_nt(a, b):
    return lax.dot_general(a, b, NT_DIMS, preferred_element_type=F32)


def _sigmoid(x):
    return 1.0 / (1.0 + jnp.exp(-x))


def _silu(x):
    return x * _sigmoid(x)


def _log_sigmoid(x):
    return jnp.minimum(x, 0.0) - jnp.log(1.0 + jnp.exp(-jnp.abs(x)))


def _rms_scale(x):
    return x * lax.rsqrt(jnp.mean(x * x, axis=-1, keepdims=True) + EPS)


def _const_spec(shape):
    nd = len(shape)
    return pl.BlockSpec(shape, lambda *_: (0,) * nd, pipeline_mode=pl.Buffered(1))


def _mem_attention(qm, mk, mv):
    heads = range(M_HEADS)
    sl = [slice(h * M_HDIM, (h + 1) * M_HDIM) for h in heads]
    s = [_dot_nt(qm[:, sl[h]], mk[:, sl[h]]) for h in heads]
    mx = [jnp.max(s[h], axis=-1, keepdims=True) for h in heads]
    p = [jnp.exp2((s[h] - mx[h]) * (M_HDIM ** -0.5 * LOG2E)) for h in heads]
    l = [jnp.sum(p[h], axis=-1, keepdims=True) for h in heads]
    outs = [_dot(p[h].astype(BF16), mv[:, sl[h]]) / l[h] for h in heads]
    return jnp.concatenate(outs, axis=-1)


def _rows_to_heads(x):
    return jnp.swapaxes(jnp.stack([x[:, h * LANES:(h + 1) * LANES] for h in range(x.shape[1] // LANES)]), 0, 1)


def _heads_to_rows(ref):
    return jnp.concatenate([ref[:, h, :] for h in range(ref.shape[1])], axis=-1)


def _rows_to_kv_heads(k, v):
    pieces = [a[:, h * LANES:(h + 1) * LANES] for a in (k, v) for h in range(a.shape[1] // LANES)]
    return jnp.swapaxes(jnp.stack(pieces), 0, 1)


def _memkv_kernel(m_ref, w_ref, o_ref, ob_ref):
    r = _dot(m_ref[...].astype(BF16), w_ref[0])
    o_ref[0] = _rows_to_kv_heads(r[:, 0:M_WIDTH], r[:, M_WIDTH:2 * M_WIDTH])
    ob_ref[0] = r.astype(BF16)


def _memkv(mem2d, w_bf):
    nm = mem2d.shape[0]
    nl = w_bf.shape[0]
    tm = min(512, nm)
    return pl.pallas_call(
        _memkv_kernel,
        grid=(nl, nm // tm),
        in_specs=[pl.BlockSpec((tm, D_MODEL), lambda l, i: (i, 0)),
                  pl.BlockSpec((1, D_MODEL, 2 * M_WIDTH), lambda l, i: (l, 0, 0))],
        out_specs=[pl.BlockSpec((1, tm, 2 * M_HEADS, M_HDIM), lambda l, i: (l, i, 0, 0)),
                   pl.BlockSpec((1, tm, 2 * M_WIDTH), lambda l, i: (l, i, 0))],
        out_shape=[jax.ShapeDtypeStruct((nl, nm, 2 * M_HEADS, M_HDIM), F32),
                   jax.ShapeDtypeStruct((nl, nm, 2 * M_WIDTH), BF16)],
        compiler_params=pltpu.CompilerParams(dimension_semantics=("arbitrary", "arbitrary")),
        name="memkv",
    )(mem2d, w_bf)


def _mlstm_chunk(rs, g, qkv_v, kt_v, c_s, n_s, m_s, causal, hs):
    ls = _log_sigmoid(g)
    tok = lax.broadcasted_iota(jnp.int32, (A_CHUNK, LANES), 0)
    bc = ls
    shift = 1
    while shift < A_CHUNK:
        bc = bc + jnp.where(tok >= shift, pltpu.roll(bc, shift, 0), 0.0)
        shift *= 2
    lane = lax.broadcasted_iota(jnp.int32, (A_CHUNK, LANES), 1)
    xt = jnp.where(lane < A_HEADS, g, bc).T
    yield
    heads = range(A_HEADS)
    b_col = [bc[:, 4 + h:5 + h] for h in heads]
    b_row = [xt[4 + h:5 + h, :] for h in heads]
    li_row = [xt[h:h + 1, :] for h in heads]
    li_col = [g[:, h:h + 1] for h in heads]
    m_old = [m_s[h:h + 1, 0:1] for h in heads]
    b_last = [bc[A_CHUNK - 1:A_CHUNK, 4 + h:5 + h] for h in heads]
    qh = [qkv_v[rs, h * 3 * A_HDIM:h * 3 * A_HDIM + A_HDIM] for h in heads]
    kh = [qkv_v[rs, h * 3 * A_HDIM + A_HDIM:h * 3 * A_HDIM + 2 * A_HDIM] for h in heads]
    vh = [qkv_v[rs, h * 3 * A_HDIM + 2 * A_HDIM:(h + 1) * 3 * A_HDIM] for h in heads]
    kt = [kt_v[h] for h in heads]
    c_old = [c_s[h] for h in heads]
    n_old = [n_s[h:h + 1, :] for h in heads]
    qk = [_dot_nt(qh[h], kh[h]) for h in heads]
    qc = [_dot(qh[h], c_old[h].astype(BF16)) for h in heads]
    dm = [jnp.where(causal, b_col[h] - b_row[h] + li_row[h], -jnp.inf) for h in heads]
    inter = [b_col[h] + m_old[h] for h in heads]
    m_row = [jnp.maximum(inter[h], jnp.max(dm[h], axis=-1, keepdims=True)) for h in heads]
    g_max = [jnp.max(b_last[h] - b_row[h] + li_row[h], axis=-1, keepdims=True) for h in heads]
    m_new = [jnp.maximum(b_last[h] + m_old[h], g_max[h]) for h in heads]
    yield
    sc = [qk[h] * jnp.exp(dm[h] - m_row[h]) for h in heads]
    dec = [jnp.exp(inter[h] - m_row[h]) for h in heads]
    ws_col = [jnp.exp(b_last[h] - b_col[h] + li_col[h] - m_new[h]) for h in heads]
    dc = [jnp.exp(b_last[h] + m_old[h] - m_new[h]) for h in heads]
    yield
    sv = [_dot(sc[h].astype(BF16), vh[h]) for h in heads]
    wv = [(ws_col[h] * vh[h].astype(F32)).astype(BF16) for h in heads]
    upd = [_dot(kt[h], wv[h]) for h in heads]
    yield
    for h in heads:
        den = (jnp.sum(sc[h], axis=-1, keepdims=True)
               + dec[h] * jnp.sum(qh[h].astype(F32) * n_old[h], axis=-1, keepdims=True))
        num = sv[h] + dec[h] * qc[h]
        hs.append(num / jnp.maximum(jnp.abs(den), jnp.exp(-m_row[h])))
    yield
    for h in heads:
        c_s[h] = dc[h] * c_old[h] + upd[h]
        n_s[h:h + 1, :] = dc[h] * n_old[h] + jnp.sum(ws_col[h] * kh[h].astype(F32), axis=0, keepdims=True)
        m_s[h:h + 1, :] = jnp.broadcast_to(m_new[h], (1, LANES))
    yield


def _l0p_kernel(nt, n_dec_seq, *refs):
    n_dec_in, n_dec_out = (9, 5) if n_dec_seq else (0, 0)
    (x_ref, xp_ref, gpre_ref, win_ref, convw_ref, convb_ref, wq_ref, wk_ref, wkt_ref, wv_ref,
     wif_ref, bif_ref, ghn_ref, skip_ref, mkv_ref, wout_ref, gpost_ref) = refs[0:17]
    dec_in = refs[17:17 + n_dec_in]
    x1_ref, conv_out, c_out, n_out, m_out = refs[17 + n_dec_in:22 + n_dec_in]
    dec_out = refs[22 + n_dec_in:22 + n_dec_in + n_dec_out]
    (h_s, u_s, ymix_s, xc_s, opre_s, zg_s, qm_s, zm_s, qkv_s, kt_s, gates_s,
     c_s, n_s, m_s) = refs[22 + n_dec_in + n_dec_out:]
    tt = x_ref.shape[1]
    nsub = tt // A_CHUNK
    t = pl.program_id(0)
    parity = lax.rem(t + 1, 2)
    pos1 = lax.rem(t + nt - 1, nt)
    pos2 = lax.rem(t + 2 * nt - 2, nt)

    @pl.when(t == 0)
    def _():
        h_s[...] = jnp.zeros(h_s.shape, BF16)
        u_s[...] = jnp.zeros(u_s.shape, F32)
        xc_s[0] = jnp.zeros(xc_s.shape[1:], F32)
        opre_s[0] = jnp.zeros(opre_s.shape[1:], F32)
        zg_s[0] = jnp.zeros(zg_s.shape[1:], F32)
        qm_s[0] = jnp.zeros(qm_s.shape[1:], BF16)
        zm_s[0] = jnp.zeros(zm_s.shape[1:], F32)
        qkv_s[0] = jnp.zeros(qkv_s.shape[1:], BF16)
        kt_s[0] = jnp.zeros(kt_s.shape[1:], BF16)
        gates_s[0] = jnp.zeros(gates_s.shape[1:], F32)

    @pl.when(pos1 == 0)
    def _():
        u_s[0:8, :] = jnp.zeros((8, A_INNER), F32)

    @pl.when(pos2 == 0)
    def _():
        c_s[...] = jnp.zeros(c_s.shape, F32)
        n_s[...] = jnp.zeros(n_s.shape, F32)
        m_s[...] = jnp.zeros(m_s.shape, F32)

    row = lax.broadcasted_iota(jnp.int32, (A_CHUNK, A_CHUNK), 0)
    col = lax.broadcasted_iota(jnp.int32, (A_CHUNK, A_CHUNK), 1)
    causal = col <= row
    ghn = ghn_ref[...]
    skp = skip_ref[...]
    mk = mkv_ref[0, :, 0:M_WIDTH]
    mv = mkv_ref[0, :, M_WIDTH:2 * M_WIDTH]

    def stage2(pslot):
        ym = _mem_attention(qm_s[pslot], mk, mv) * _silu(zm_s[pslot])
        ymix_s[:, A_INNER:A_INNER + M_WIDTH] = ym.astype(BF16)
        yield
        for c in range(nsub):
            rs = slice(c * A_CHUNK, (c + 1) * A_CHUNK)
            hs = []
            yield from _mlstm_chunk(rs, gates_s[pslot, rs, :], qkv_s.at[pslot], kt_s.at[pslot, :, c],
                                    c_s, n_s, m_s, causal, hs)
            parts = []
            for h in range(A_HEADS):
                v = _sigmoid(opre_s[pslot, rs, h * A_HDIM:(h + 1) * A_HDIM]) * hs[h]
                mu = jnp.mean(v, axis=-1, keepdims=True)
                var = jnp.mean(jnp.square(v - mu), axis=-1, keepdims=True)
                parts.append((v - mu) * lax.rsqrt(var + EPS))
            hn = jnp.concatenate(parts, axis=-1) * ghn
            y = hn + skp * xc_s[pslot, rs, :]
            ymix_s[rs, 0:A_INNER] = (y * _silu(zg_s[pslot, rs, :])).astype(BF16)
            yield

    def stage1(slot):
        hb = h_s[...]
        u_s[8:8 + tt, :] = _dot(hb, win_ref[:, 0:A_INNER])
        yield
        cw = convw_ref[...]
        cb = convb_ref[...]
        for c in range(nsub):
            r0 = c * A_CHUNK
            blk = u_s[r0:r0 + A_CHUNK + 8, :]
            xc = cb + pltpu.roll(blk, 3, 0)[8:, :] * cw[0:1, :]
            xc = xc + pltpu.roll(blk, 2, 0)[8:, :] * cw[1:2, :]
            xc = xc + pltpu.roll(blk, 1, 0)[8:, :] * cw[2:3, :]
            xc = xc + blk[8:, :] * cw[3:4, :]
            xc_s[slot, r0:r0 + A_CHUNK, :] = _silu(xc)
        opre_s[slot] = _dot(hb, win_ref[:, A_INNER:2 * A_INNER])
        yield
        zg_s[slot] = _dot(hb, win_ref[:, 2 * A_INNER:3 * A_INNER])
        yield
        qm_s[slot] = _dot(hb, win_ref[:, 3 * A_INNER:3 * A_INNER + M_WIDTH]).astype(BF16)
        zm_s[slot] = _dot(hb, win_ref[:, 3 * A_INNER + M_WIDTH:3 * A_INNER + 2 * M_WIDTH])
        yield
        for h in range(A_HEADS):
            sl = slice(h * A_HDIM, (h + 1) * A_HDIM)
            xh = xc_s[slot, :, sl].astype(BF16)
            uh = u_s[8:8 + tt, sl].astype(BF16)
            base = h * 3 * A_HDIM
            qkv_s[slot, :, base:base + A_HDIM] = _dot(xh, wq_ref[h]).astype(BF16)
            qkv_s[slot, :, base + A_HDIM:base + 2 * A_HDIM] = (
                _dot(xh, wk_ref[h]) * (A_HDIM ** -0.5)).astype(BF16)
            qkv_s[slot, :, base + 2 * A_HDIM:base + 3 * A_HDIM] = _dot(uh, wv_ref[h]).astype(BF16)
            kt = (_dot_nt(wkt_ref[h], xh) * (A_HDIM ** -0.5)).astype(BF16)
            for c in range(nsub):
                kt_s[slot, h, c] = kt[:, c * A_CHUNK:(c + 1) * A_CHUNK]
            yield
        gates_s[slot] = _dot(qkv_s[slot], wif_ref[...]) + bif_ref[...]
        yield

    def step(slot):
        if n_dec_seq:
            first = slot == 1
            heads = (0, 1) if first else (2, 3)
            _dec_mlstm_body(jnp.minimum(t // 2, n_dec_seq - 1), heads, first, *dec_in, *dec_out)
        pending = [stage2(1 - slot), stage1(slot)]
        while pending:
            for gen in list(pending):
                try:
                    next(gen)
                except StopIteration:
                    pending.remove(gen)
        h_next = (_rms_scale(x_ref[0]) * gpre_ref[...]).astype(BF16)
        out = _dot(ymix_s[...], wout_ref[...])
        h_s[...] = h_next
        x1_ref[0] = xp_ref[0] + _rms_scale(out) * gpost_ref[...]

    for s in range(2):
        pl.when(parity == s)(functools.partial(step, s))

    @pl.when(jnp.logical_and(pos1 == nt - 1, t > 0))
    def _():
        conv_out[0, 0] = u_s[tt + 5:tt + 8, :]

    u_s[0:8, :] = u_s[tt:tt + 8, :]

    @pl.when(jnp.logical_and(pos2 == nt - 1, t > 1))
    def _():
        for h in range(A_HEADS):
            c_out[0, 0, h] = c_s[h].T
        n_out[0, 0] = n_s[0:A_HEADS, :]
        m_out[0] = m_s[...]


def _layer0_prompt_pipelined(x, g_pre, w_in, conv_w, conv_b, wq, wk, wkt, wv, wif, bif, ghn, skip, mkv_bf,
                             w_out, g_post, dec_job=None):
    b, s, _ = x.shape
    tt = min(L0_TILE, s)
    nt = s // tt
    ntiles = b * nt
    a_in = w_in.shape[1]
    nsub = tt // A_CHUNK

    def cur(t):
        t1 = jnp.minimum(t, ntiles - 1)
        return (t1 // nt, t1 % nt, 0)

    def prev(t):
        t2 = jnp.maximum(t - 2, 0)
        return (t2 // nt, t2 % nt, 0)

    def prev_b(t):
        return jnp.maximum(t - 2, 0) // nt

    in_specs = [
        pl.BlockSpec((1, tt, D_MODEL), cur),
        pl.BlockSpec((1, tt, D_MODEL), prev),
        _const_spec((1, D_MODEL)),
        _const_spec((D_MODEL, a_in)),
        _const_spec((CONV_W, A_INNER)),
        _const_spec((1, A_INNER)),
        _const_spec((A_HEADS, A_HDIM, A_HDIM)),
        _const_spec((A_HEADS, A_HDIM, A_HDIM)),
        _const_spec((A_HEADS, A_HDIM, A_HDIM)),
        _const_spec((A_HEADS, A_HDIM, A_HDIM)),
        _const_spec((3 * A_INNER, LANES)),
        _const_spec((1, LANES)),
        _const_spec((1, A_INNER)),
        _const_spec((1, A_INNER)),
        pl.BlockSpec((1, N_MEM, 2 * M_WIDTH), lambda t: (prev_b(t), 0, 0)),
        _const_spec((A_INNER + M_WIDTH, D_MODEL)),
        _const_spec((1, D_MODEL)),
    ]
    out_specs = [
        pl.BlockSpec((1, tt, D_MODEL), prev),
        pl.BlockSpec((1, 1, CONV_W - 1, A_INNER), lambda t: (0, prev_b(t), 0, 0)),
        pl.BlockSpec((1, 1, A_HEADS, A_HDIM, A_HDIM), lambda t: (0, prev_b(t), 0, 0, 0)),
        pl.BlockSpec((1, 1, A_HEADS, A_HDIM), lambda t: (0, prev_b(t), 0, 0)),
        pl.BlockSpec((1, 8, LANES), lambda t: (prev_b(t), 0, 0)),
    ]
    out_shape = [
        jax.ShapeDtypeStruct((b, s, D_MODEL), F32),
        jax.ShapeDtypeStruct((1, b, CONV_W - 1, A_INNER), F32),
        jax.ShapeDtypeStruct((1, b, A_HEADS, A_HDIM, A_HDIM), F32),
        jax.ShapeDtypeStruct((1, b, A_HEADS, A_HDIM), F32),
        jax.ShapeDtypeStruct((b, 8, LANES), F32),
    ]
    scratch = [
        pltpu.VMEM((tt, D_MODEL), BF16),
        pltpu.VMEM((tt + 8, A_INNER), F32),
        pltpu.VMEM((tt, A_INNER + M_WIDTH), BF16),
        pltpu.VMEM((2, tt, A_INNER), F32),
        pltpu.VMEM((2, tt, A_INNER), F32),
        pltpu.VMEM((2, tt, A_INNER), F32),
        pltpu.VMEM((2, tt, M_WIDTH), BF16),
        pltpu.VMEM((2, tt, M_WIDTH), F32),
        pltpu.VMEM((2, tt, 3 * A_INNER), BF16),
        pltpu.VMEM((2, A_HEADS, nsub, A_HDIM, A_CHUNK), BF16),
        pltpu.VMEM((2, tt, LANES), F32),
        pltpu.VMEM((A_HEADS, A_HDIM, A_HDIM), F32),
        pltpu.VMEM((8, A_HDIM), F32),
        pltpu.VMEM((8, LANES), F32),
    ]
    dec_args, dec_specs, dec_out_specs, dec_out_shapes, n_dec_seq = [], [], [], [], 0
    if dec_job is not None:
        n_dec_seq = dec_job[0].shape[0]
        dec_args, dec_specs, dec_out_specs, dec_out_shapes = _dec_mlstm_specs(
            *dec_job, seq_index=lambda t: jnp.minimum(t // 2, n_dec_seq - 1))
    return pl.pallas_call(
        functools.partial(_l0p_kernel, nt, n_dec_seq),
        grid=(ntiles + 2,),
        in_specs=in_specs + dec_specs,
        out_specs=out_specs + dec_out_specs,
        out_shape=out_shape + dec_out_shapes,
        scratch_shapes=scratch,
        compiler_params=pltpu.CompilerParams(
            dimension_semantics=("arbitrary",), vmem_limit_bytes=VMEM_LIMIT),
        name="layer0_prompt",
    )(x, x, g_pre, w_in, conv_w, conv_b, wq, wk, wkt, wv, wif, bif, ghn, skip, mkv_bf, w_out, g_post,
      *dec_args)


def _rope_cols(x, cos, sin_signed):
    outs = []
    for cblk in range(x.shape[1] // B_HDIM):
        xb = x[:, cblk * B_HDIM:(cblk + 1) * B_HDIM]
        outs.append(xb * cos + pltpu.roll(xb, B_HDIM // 2, 1) * sin_signed)
    return jnp.concatenate(outs, axis=-1)


def _l1a_kernel(n_dec_in, *refs):
    x_ref, gkv_ref, gpre_ref, wkv_ref, win_ref, cos_ref, sin_ref = refs[0:7]
    dec_in = refs[7:7 + n_dec_in]
    (q0_ref, q1_ref, q2_ref, k0_ref, k1_ref, k2_ref, v0_ref, v1_ref, v2_ref,
     zg_ref, qm_ref, zm_ref, w0_ref, w1_ref, w2_ref) = refs[7 + n_dec_in:22 + n_dec_in]
    dec_out = refs[22 + n_dec_in:]
    tt = x_ref.shape[1]
    xn = _rms_scale(x_ref[0])
    hk = (xn * gkv_ref[...]).astype(BF16)
    hq = (xn * gpre_ref[...]).astype(BF16)
    cos = cos_ref[...]
    sin = sin_ref[...]
    q_refs = (q0_ref, q1_ref, q2_ref)
    k_refs = (k0_ref, k1_ref, k2_ref)
    v_refs = (v0_ref, v1_ref, v2_ref)
    w_refs = (w0_ref, w1_ref, w2_ref)
    for g in (2, 1, 0):
        d = B_GROUPS[g][1]
        kf = _rope_cols(_dot(hk, wkv_ref[:, g * 2 * B_WIDTH:g * 2 * B_WIDTH + B_WIDTH]), cos, sin)
        vf = _dot(hk, wkv_ref[:, g * 2 * B_WIDTH + B_WIDTH:(g + 1) * 2 * B_WIDTH])
        qf = _rope_cols(_dot(hq, win_ref[:, g * B_WIDTH:(g + 1) * B_WIDTH]), cos, sin)
        wr = w_refs[g]
        wrows = wr.shape[1]
        wr[0] = _rows_to_kv_heads(kf[tt - wrows:, :], vf[tt - wrows:, :])
        for val, ref in ((qf.astype(BF16), q_refs[g]), (kf.astype(BF16), k_refs[g]), (vf.astype(BF16), v_refs[g])):
            if d == 1:
                ref[0, 0] = val
            else:
                ref[0] = jnp.swapaxes(val.reshape(tt // d, d, val.shape[1]), 0, 1)
        if g == 2 and n_dec_in:
            _dec_attn_kernel(*dec_in, *dec_out)
    qoff = N_GROUPS * B_WIDTH
    zg_ref[0] = _dot(hq, win_ref[:, qoff:qoff + B_WIDTH]).astype(BF16)
    qm_ref[0] = _dot(hq, win_ref[:, qoff + B_WIDTH:qoff + B_WIDTH + M_WIDTH]).astype(BF16)
    zm_ref[0] = _dot(hq, win_ref[:, qoff + B_WIDTH + M_WIDTH:qoff + B_WIDTH + 2 * M_WIDTH]).astype(BF16)


def _layer1_proj_prompt(x1, g_kv, g_pre, wkv, win, cos_t, sin_t, dec_job=None):
    b, s, _ = x1.shape
    tt = min(TOK_TILE, s)
    nt = s // tt
    tile = lambda bb, i: (bb, i, 0)
    in_specs = [
        pl.BlockSpec((1, tt, D_MODEL), tile),
        _const_spec((1, D_MODEL)),
        _const_spec((1, D_MODEL)),
        _const_spec(wkv.shape),
        _const_spec(win.shape),
        pl.BlockSpec((tt, B_HDIM), lambda bb, i: (i, 0)),
        pl.BlockSpec((tt, B_HDIM), lambda bb, i: (i, 0)),
    ]
    qkv_specs, qkv_shapes = [], []
    for _ in range(3):
        for (_, d) in B_GROUPS:
            qkv_specs.append(pl.BlockSpec((1, d, tt // d, B_WIDTH), lambda bb, i: (bb, 0, i, 0)))
            qkv_shapes.append(jax.ShapeDtypeStruct((b, d, s // d, B_WIDTH), BF16))
    gate_specs = [pl.BlockSpec((1, tt, B_WIDTH), tile)] * 3
    gate_shapes = [jax.ShapeDtypeStruct((b, s, B_WIDTH), BF16)] * 3
    win_specs, win_shapes = [], []
    for (w, _) in B_GROUPS:
        wr = min(w, s)
        rows = min(wr, tt)
        nblk = wr // rows
        win_specs.append(pl.BlockSpec(
            (1, rows, 2 * B_HEADS, B_HDIM),
            functools.partial(lambda bb, i, nb: (bb, jnp.maximum(i - (nt - nb), 0), 0, 0), nb=nblk)))
        win_shapes.append(jax.ShapeDtypeStruct((b, wr, 2 * B_HEADS, B_HDIM), F32))
    dec_args, dec_specs, dec_out_specs, dec_out_shapes = [], [], [], []
    if dec_job is not None:
        dec_args, dec_specs, dec_out_specs, dec_out_shapes = _dec_attn_specs(
            *dec_job, seq_index=lambda bb, i: bb * nt + i)
    return pl.pallas_call(
        functools.partial(_l1a_kernel, len(dec_args)),
        grid=(b, nt),
        in_specs=in_specs + dec_specs,
        out_specs=qkv_specs + gate_specs + win_specs + dec_out_specs,
        out_shape=qkv_shapes + gate_shapes + win_shapes + dec_out_shapes,
        compiler_params=pltpu.CompilerParams(
            dimension_semantics=("arbitrary", "arbitrary"), vmem_limit_bytes=VMEM_LIMIT),
        name="layer1_proj_prompt",
    )(x1, g_kv, g_pre, wkv, win, cos_t, sin_t, *dec_args)


def _cols_to_lanes(cols):
    t = cols[0].shape[0]
    lane = lax.broadcasted_iota(jnp.int32, (t, LANES), 1)
    acc = jnp.zeros((t, LANES), F32)
    for h, cvec in enumerate(cols):
        acc = jnp.where(lane == h, cvec, acc)
    return acc


def _band_attn_kernel(q_ref, kc_ref, kp_ref, vc_ref, vp_ref, o_ref, lse_ref):
    nres, tq = q_ref.shape[1:3]
    nsb = tq // ATT_BLK
    j = pl.program_id(2)
    row = lax.broadcasted_iota(jnp.int32, (ATT_BLK, 2 * ATT_BLK), 0)
    col = lax.broadcasted_iota(jnp.int32, (ATT_BLK, 2 * ATT_BLK), 1)
    band = jnp.logical_and(col >= row, col <= row + ATT_BLK)
    first_pen = jnp.where(col < ATT_BLK, jnp.where(j > 0, 0.0, -jnp.inf), 0.0)
    scale = B_HDIM ** -0.5
    blocks = [(r, sb) for r in range(nres) for sb in range(nsb)]
    for g0 in range(0, len(blocks), ATT_GROUP):
        grp = blocks[g0:g0 + ATT_GROUP]
        qs, ks, vs, pens = [], [], [], []
        for r, sb in grp:
            rs = slice(sb * ATT_BLK, (sb + 1) * ATT_BLK)
            ps = slice((sb - 1) * ATT_BLK, sb * ATT_BLK)
            for h in range(B_HEADS):
                hs = slice(h * B_HDIM, (h + 1) * B_HDIM)
                qs.append(q_ref[0, r, rs, hs])
                kp = kp_ref[0, r, :, hs] if sb == 0 else kc_ref[0, r, ps, hs]
                vp = vp_ref[0, r, :, hs] if sb == 0 else vc_ref[0, r, ps, hs]
                ks.append(jnp.concatenate([kp, kc_ref[0, r, rs, hs]], axis=0))
                vs.append(jnp.concatenate([vp, vc_ref[0, r, rs, hs]], axis=0))
                pens.append(sb == 0)
        q3 = jnp.stack(qs)
        k3 = jnp.stack(ks)
        v3 = jnp.stack(vs)
        s = jnp.einsum('uqd,ukd->uqk', q3, k3, preferred_element_type=F32)
        s = jnp.stack([s[u] + first_pen if pens[u] else s[u] for u in range(len(pens))])
        s = jnp.where(band[None], s, -jnp.inf)
        mx = jnp.max(s, axis=-1, keepdims=True)
        p = jnp.exp2((s - mx) * (scale * LOG2E))
        l = jnp.sum(p, axis=-1, keepdims=True)
        o = jnp.einsum('uqk,ukd->uqd', p.astype(BF16), v3, preferred_element_type=F32) / l
        lse = mx * scale + jnp.log(l)
        for i, (r, sb) in enumerate(grp):
            rs = slice(sb * ATT_BLK, (sb + 1) * ATT_BLK)
            for h in range(B_HEADS):
                o_ref[0, r, rs, h * B_HDIM:(h + 1) * B_HDIM] = o[i * B_HEADS + h].astype(BF16)
            lse_ref[0, r, rs, :] = _cols_to_lanes([lse[i * B_HEADS + h] for h in range(B_HEADS)])


def _band_attention(q, k, v):
    b, d, ls, _ = q.shape
    tq = min(ATT_TILE, ls)
    nj = ls // tq
    ratio = tq // ATT_BLK
    nres = min(d, ATT_TILE // tq)
    cur = lambda bb, r, j: (bb, r, j, 0)
    prev = lambda bb, r, j: (bb, r, jnp.maximum(j * ratio - 1, 0), 0)
    return pl.pallas_call(
        _band_attn_kernel,
        grid=(b, d // nres, nj),
        in_specs=[pl.BlockSpec((1, nres, tq, B_WIDTH), cur),
                  pl.BlockSpec((1, nres, tq, B_WIDTH), cur),
                  pl.BlockSpec((1, nres, ATT_BLK, B_WIDTH), prev),
                  pl.BlockSpec((1, nres, tq, B_WIDTH), cur),
                  pl.BlockSpec((1, nres, ATT_BLK, B_WIDTH), prev)],
        out_specs=[pl.BlockSpec((1, nres, tq, B_WIDTH), cur),
                   pl.BlockSpec((1, nres, tq, LANES), cur)],
        out_shape=[jax.ShapeDtypeStruct((b, d, ls, B_WIDTH), BF16),
                   jax.ShapeDtypeStruct((b, d, ls, LANES), F32)],
        compiler_params=pltpu.CompilerParams(
            dimension_semantics=("arbitrary", "arbitrary", "arbitrary"), vmem_limit_bytes=VMEM_LIMIT),
        name="band_attention_d%d" % d,
    )(q, k, k, v, v)


def _unpermute(ref):
    d, rows, width = ref.shape[1:]
    if d == 1:
        return ref[0, 0]
    return jnp.swapaxes(ref[0], 0, 1).reshape(d * rows, width)


def _l1c_kernel(x_ref, o0_ref, o1_ref, o2_ref, l0_ref, l1_ref, l2_ref, zg_ref, qm_ref, zm_ref,
                mkv_ref, wout_ref, gpost_ref, y_ref):
    tt = x_ref.shape[1]
    o_refs = (o0_ref, o1_ref, o2_ref)
    l_refs = (l0_ref, l1_ref, l2_ref)
    outs, lses = [], []
    for g, (_, d) in enumerate(B_GROUPS):
        outs.append(_unpermute(o_refs[g]))
        lses.append(_unpermute(l_refs[g])[:, 0:B_HEADS])
    mx = jnp.maximum(jnp.maximum(lses[0], lses[1]), lses[2])
    es = [jnp.exp(l - mx) for l in lses]
    tot = es[0] + es[1] + es[2]
    ws = [(e / tot).astype(BF16) for e in es]
    parts = []
    for h in range(B_HEADS):
        hs = slice(h * B_HDIM, (h + 1) * B_HDIM)
        acc = ws[0][:, h:h + 1] * outs[0][:, hs]
        acc = acc + ws[1][:, h:h + 1] * outs[1][:, hs]
        acc = acc + ws[2][:, h:h + 1] * outs[2][:, hs]
        parts.append(acc)
    ydil = jnp.concatenate(parts, axis=-1)
    ymix = (ydil.astype(F32) * _silu(zg_ref[0].astype(F32))).astype(BF16)
    mk = mkv_ref[0, :, 0:M_WIDTH]
    mv = mkv_ref[0, :, M_WIDTH:2 * M_WIDTH]
    ym = (_mem_attention(qm_ref[0], mk, mv) * _silu(zm_ref[0].astype(F32))).astype(BF16)
    out = _dot(ymix, wout_ref[0:B_WIDTH, :]) + _dot(ym, wout_ref[B_WIDTH:B_WIDTH + M_WIDTH, :])
    y_ref[0] = x_ref[0] + _rms_scale(out) * gpost_ref[...]


def _layer1_out_prompt(x1, os_, ls_, zg, qm, zm, mkv_bf, w_out, g_post):
    b, s, _ = x1.shape
    tt = min(L1OUT_TILE, s)
    nt = s // tt
    tile = lambda bb, i: (bb, i, 0)
    perm = lambda bb, i: (bb, 0, i, 0)
    in_specs = [pl.BlockSpec((1, tt, D_MODEL), tile)]
    for width in (B_WIDTH, LANES):
        for (_, d) in B_GROUPS:
            in_specs.append(pl.BlockSpec((1, d, tt // d, width), perm))
    in_specs += [pl.BlockSpec((1, tt, B_WIDTH), tile)] * 3
    in_specs += [pl.BlockSpec((1, N_MEM, 2 * M_WIDTH), lambda bb, i: (bb, 0, 0)),
                 _const_spec(w_out.shape), _const_spec((1, D_MODEL))]
    return pl.pallas_call(
        _l1c_kernel,
        grid=(b, nt),
        in_specs=in_specs,
        out_specs=pl.BlockSpec((1, tt, D_MODEL), tile),
        out_shape=jax.ShapeDtypeStruct((b, s, D_MODEL), F32),
        compiler_params=pltpu.CompilerParams(
            dimension_semantics=("arbitrary", "arbitrary"), vmem_limit_bytes=VMEM_LIMIT),
        name="layer1_out_prompt",
    )(x1, *os_, *ls_, zg, qm, zm, mkv_bf, w_out, g_post)


def _rope_tables(pos):
    half = B_HDIM // 2
    inv = ROPE_THETA ** (-jnp.arange(half, dtype=F32) / half)
    ang = pos[:, None] * inv[None, :]
    cos = jnp.cos(ang)
    sin = jnp.sin(ang)
    return jnp.concatenate([cos, cos], axis=-1), jnp.concatenate([-sin, sin], axis=-1)


def _prompt_group(x_prompt, mem_prompt, p, sample=None):
    b, s, _ = x_prompt.shape
    memkv_f, memkv_b = _memkv(mem_prompt.reshape(b * N_MEM, D_MODEL), p['w_mkv'])
    depth = memkv_f.shape[0]
    memkv_b = memkv_b.reshape(depth, b, N_MEM, 2 * M_WIDTH)
    job0 = sample.mlstm_job() if sample is not None else None
    if job0 is not None and b * (s // min(L0_TILE, s)) + 2 < 2 * job0[0].shape[0]:
        job0 = None
    outs0 = _layer0_prompt_pipelined(
        x_prompt, p['g_pre'][0:1], p['w_in_a'][0], p['conv_w_a'][0], p['conv_b_a'], p['w_q_a'][0],
        p['w_k_a'][0], jnp.swapaxes(p['w_k_a'][0], 1, 2), p['w_v_a'][0], p['w_if_a'], p['b_if_a'],
        p['g_hn_a'], p['skip_a'], memkv_b[0], p['w_out_a'][0], p['g_post'][0:1], job0)
    x1, conv_p, c_p, n_p, m_pad = outs0[0:5]
    if sample is not None:
        sample.after_mlstm(outs0[5:10] if job0 is not None else _dec_mlstm(*sample.mlstm_job()))
    cos_t, sin_t = _rope_tables(jnp.arange(s, dtype=F32))
    job1 = sample.attn_job if sample is not None else None
    if job1 is not None and b * (s // min(TOK_TILE, s)) != job1[0].shape[0]:
        job1 = None
    outs = _layer1_proj_prompt(x1, p['g_kv'], p['g_pre'][1:2], p['w_kv_b'], p['w_in_b'][0], cos_t, sin_t,
                               job1)
    if sample is not None:
        sample.after_attn(tuple(outs[15:17]) if job1 is not None else _dec_attn(*sample.attn_job))
    qs, ks, vs = outs[0:3], outs[3:6], outs[6:9]
    zg, qm, zm = outs[9:12]
    wins = outs[12:15]
    os_, ls_ = [], []
    for g in range(N_GROUPS):
        o, l = _band_attention(qs[g], ks[g], vs[g])
        os_.append(o)
        ls_.append(l)
    y = _layer1_out_prompt(x1, os_, ls_, zg, qm, zm, memkv_b[1], p['w_out_b'][0], p['g_post'][1:2])
    m_p = m_pad[:, 0:A_HEADS, 0][None]
    wins = [w.reshape(b, w.shape[1], 2, B_HEADS, B_HDIM) for w in wins]
    memkv_p = memkv_f.reshape(depth, b, N_MEM, 2, M_HEADS, M_HDIM)
    return y, conv_p, c_p, n_p, m_p, wins, memkv_p


def _prep_params(g_pre, g_post, w_in_a, conv_w_a, conv_b_a, w_q_a, w_k_a, w_v_a, w_if_a, b_if_a,
                 g_hn_a, skip_a, w_out_a, g_kv, w_kv_b, w_in_b, w_out_b, w_mkv):
    wif = jnp.pad(w_if_a[0], ((0, 0), (0, LANES - 2 * A_HEADS))).astype(BF16)
    bif = jnp.pad(b_if_a[0], (0, LANES - 2 * A_HEADS))[None, :]
    return {
        'g_pre': g_pre, 'g_post': g_post,
        'w_in_a': w_in_a.astype(BF16), 'conv_w_a': conv_w_a, 'conv_b_a': conv_b_a,
        'w_q_a': w_q_a.astype(BF16), 'w_k_a': w_k_a.astype(BF16), 'w_v_a': w_v_a.astype(BF16),
        'w_if_a': wif, 'b_if_a': bif, 'g_hn_a': g_hn_a, 'skip_a': skip_a,
        'w_out_a': w_out_a.astype(BF16), 'g_kv': g_kv[None, :], 'w_kv_b': w_kv_b.astype(BF16),
        'w_in_b': w_in_b.astype(BF16), 'w_out_b': w_out_b.astype(BF16), 'w_mkv': w_mkv.astype(BF16),
    }


def _dec_l0_proj_kernel(x_ref, gpre_ref, win_ref, cst_ref, convw_ref, convb_ref, wq_ref, wk_ref, wv_ref,
                        wif_ref, bif_ref,
                        q_ref, k_ref, v_ref, gates_ref, xc_ref, opre_ref, zg_ref, qm_ref, zm_ref, cnew_ref):
    h = (_rms_scale(x_ref[:, 0, :]) * gpre_ref[...]).astype(BF16)
    u = _dot(h, win_ref[:, 0:A_INNER])
    opre_ref[...] = _dot(h, win_ref[:, A_INNER:2 * A_INNER])
    zg_ref[...] = _dot(h, win_ref[:, 2 * A_INNER:3 * A_INNER])
    qm_ref[...] = _rows_to_heads(_dot(h, win_ref[:, 3 * A_INNER:3 * A_INNER + M_WIDTH]))
    zm_ref[...] = _dot(h, win_ref[:, 3 * A_INNER + M_WIDTH:3 * A_INNER + 2 * M_WIDTH])
    cw = convw_ref[...]
    xc = convb_ref[...] + cst_ref[0, :, 0, :] * cw[0:1, :]
    xc = xc + cst_ref[0, :, 1, :] * cw[1:2, :]
    xc = xc + cst_ref[0, :, 2, :] * cw[2:3, :]
    xc = xc + u * cw[3:4, :]
    xc = _silu(xc)
    xc_ref[...] = xc
    cnew_ref[0, :, 0, :] = cst_ref[0, :, 1, :]
    cnew_ref[0, :, 1, :] = cst_ref[0, :, 2, :]
    cnew_ref[0, :, 2, :] = u
    qs, ks, vs, cat = [], [], [], []
    for hd in range(A_HEADS):
        sl = slice(hd * A_HDIM, (hd + 1) * A_HDIM)
        xh = xc[:, sl].astype(BF16)
        qh = _dot(xh, wq_ref[hd])
        kh = _dot(xh, wk_ref[hd]) * (A_HDIM ** -0.5)
        vh = _dot(u[:, sl].astype(BF16), wv_ref[hd])
        qs.append(qh)
        ks.append(kh)
        vs.append(vh)
        cat += [qh.astype(BF16), kh.astype(BF16), vh.astype(BF16)]
    q_ref[...] = jnp.concatenate(qs, axis=-1)
    k_ref[...] = jnp.concatenate(ks, axis=-1)
    v_ref[...] = jnp.concatenate(vs, axis=-1)
    gates_ref[...] = _dot(jnp.concatenate(cat, axis=-1), wif_ref[...]) + bif_ref[...]


def _whole(shape):
    nd = len(shape)
    return pl.BlockSpec(shape, lambda *_: (0,) * nd)


def _dec_l0_proj(x, g_pre, w_in, cst, conv_w, conv_b, wq, wk, wv, wif, bif):
    nb = x.shape[0]
    args = (x, g_pre, w_in, cst, conv_w, conv_b, wq, wk, wv, wif, bif)
    f = lambda *s: jax.ShapeDtypeStruct(s, F32)
    out_shape = [f(nb, A_INNER), f(nb, A_INNER), f(nb, A_INNER), f(nb, LANES), f(nb, A_INNER), f(nb, A_INNER),
                 f(nb, A_INNER), f(nb, M_HEADS, M_HDIM), f(nb, M_WIDTH), f(1, nb, CONV_W - 1, A_INNER)]
    return pl.pallas_call(
        _dec_l0_proj_kernel,
        grid=(1,),
        in_specs=[_whole(a.shape) for a in args],
        out_specs=[_whole(o.shape) for o in out_shape],
        out_shape=out_shape,
        compiler_params=pltpu.CompilerParams(dimension_semantics=("arbitrary",), vmem_limit_bytes=VMEM_LIMIT),
        name="dec_l0_proj",
    )(*args)


def _row_to_col(row, eye):
    return jnp.sum(jnp.where(eye, row, 0.0), axis=-1, keepdims=True)


def _col_to_row(colv, eye):
    return jnp.sum(jnp.where(eye, colv, 0.0), axis=0, keepdims=True)


def _dec_mem_attention(q, kv_ref_view):
    kk = kv_ref_view[:, 0]
    vv = kv_ref_view[:, 1]
    s = jnp.sum(kk * (q * (M_HDIM ** -0.5))[None], axis=-1, keepdims=True)
    mx = jnp.max(s, axis=0, keepdims=True)
    p = jnp.exp(s - mx)
    return jnp.sum(p * vv, axis=0) / jnp.sum(p, axis=0)


def _dec_mlstm_body(b, heads, with_mem, q_ref, k_ref, v_ref, gates_ref, m_ref, c_ref, n_ref, qm_ref, kv_ref,
                    hs_ref, c_out, n_out, m_out, ym_ref):
    rb = pl.ds(b, 1)
    g = gates_ref[rb, :]
    mrow = m_ref[0, rb, :]
    r = lax.broadcasted_iota(jnp.int32, (A_HDIM, A_HDIM), 0)
    c = lax.broadcasted_iota(jnp.int32, (A_HDIM, A_HDIM), 1)
    eye = r == c
    sl = {h: slice(h * A_HDIM, (h + 1) * A_HDIM) for h in heads}
    qh = {h: q_ref[rb, sl[h]] for h in heads}
    kh = {h: k_ref[rb, sl[h]] for h in heads}
    vh = {h: v_ref[rb, sl[h]] for h in heads}
    c_old = {h: c_ref[0, 0, h] for h in heads}
    n_old = {h: n_ref[0, 0, h:h + 1, :] for h in heads}
    li = {h: g[:, h:h + 1] for h in heads}
    lf = {h: _log_sigmoid(g[:, 4 + h:5 + h]) for h in heads}
    m_old = {h: mrow[:, h:h + 1] for h in heads}
    cq = {h: jnp.sum(c_old[h] * qh[h], axis=-1, keepdims=True) for h in heads}
    v_col = {h: _row_to_col(vh[h], eye) for h in heads}
    nq = {h: jnp.sum(n_old[h] * qh[h], axis=-1, keepdims=True) for h in heads}
    qk = {h: jnp.sum(qh[h] * kh[h], axis=-1, keepdims=True) for h in heads}
    inter = {h: lf[h] + m_old[h] for h in heads}
    m_new = {h: jnp.maximum(inter[h], li[h]) for h in heads}
    ws = {h: jnp.exp(li[h] - m_new[h]) for h in heads}
    dec = {h: jnp.exp(inter[h] - m_new[h]) for h in heads}
    sc = {h: qk[h] * ws[h] for h in heads}
    den = {h: sc[h] + dec[h] * nq[h] for h in heads}
    h_col = {h: (sc[h] * v_col[h] + dec[h] * cq[h]) / jnp.maximum(jnp.abs(den[h]), jnp.exp(-m_new[h]))
             for h in heads}
    for h in heads:
        c_out[0, 0, h] = dec[h] * c_old[h] + (ws[h] * v_col[h]) * kh[h]
        n_out[0, 0, h:h + 1, :] = dec[h] * n_old[h] + ws[h] * kh[h]
        m_out[0, h:h + 1, :] = jnp.broadcast_to(m_new[h], (1, LANES))
    for h in heads:
        hs_ref[0, :, sl[h]] = _col_to_row(h_col[h], eye)
    if with_mem:
        ym_ref[0] = _dec_mem_attention(qm_ref[0], kv_ref.at[0, 0])


def _dec_mlstm_kernel(*refs):
    _dec_mlstm_body(pl.program_id(0), range(A_HEADS), True, *refs)


def _dec_mlstm_specs(q, k, v, gates, m_in, state_c, state_n, qm3, cache_mem_kv, seq_index):
    nb = q.shape[0]

    def at(*tail, lead=()):
        return lambda *idx: lead + (seq_index(*idx),) + tail

    in_specs = [_whole(q.shape), _whole(k.shape), _whole(v.shape), _whole(gates.shape), _whole(m_in.shape),
                pl.BlockSpec((1, 1, A_HEADS, A_HDIM, A_HDIM), at(0, 0, 0, lead=(0,))),
                pl.BlockSpec((1, 1, A_HEADS, A_HDIM), at(0, 0, lead=(0,))),
                pl.BlockSpec((1, M_HEADS, M_HDIM), at(0, 0)),
                pl.BlockSpec((1, 1, N_MEM, 2, M_HEADS, M_HDIM), at(0, 0, 0, 0, lead=(0,)))]
    out_specs = [pl.BlockSpec((1, 1, A_INNER), at(0, 0)),
                 pl.BlockSpec((1, 1, A_HEADS, A_HDIM, A_HDIM), at(0, 0, 0, lead=(0,))),
                 pl.BlockSpec((1, 1, A_HEADS, A_HDIM), at(0, 0, lead=(0,))),
                 pl.BlockSpec((1, A_HEADS, LANES), at(0, 0)),
                 pl.BlockSpec((1, M_HEADS, M_HDIM), at(0, 0))]
    out_shapes = [jax.ShapeDtypeStruct((nb, 1, A_INNER), F32),
                  jax.ShapeDtypeStruct(state_c.shape, F32),
                  jax.ShapeDtypeStruct(state_n.shape, F32),
                  jax.ShapeDtypeStruct((nb, A_HEADS, LANES), F32),
                  jax.ShapeDtypeStruct((nb, M_HEADS, M_HDIM), F32)]
    return [q, k, v, gates, m_in, state_c, state_n, qm3, cache_mem_kv], in_specs, out_specs, out_shapes


def _dec_mlstm(*job):
    args, in_specs, out_specs, out_shapes = _dec_mlstm_specs(*job, seq_index=lambda b: b)
    return pl.pallas_call(
        _dec_mlstm_kernel,
        grid=(args[0].shape[0],),
        in_specs=in_specs,
        out_specs=out_specs,
        out_shape=out_shapes,
        compiler_params=pltpu.CompilerParams(dimension_semantics=("arbitrary",), vmem_limit_bytes=VMEM_LIMIT),
        name="dec_mlstm",
    )(*args)


def _dec_mid_kernel(hs_ref, opre_ref, xc_ref, zg_ref, ym_ref, zm_ref, x_ref, ghn_ref, skip_ref, wout_ref,
                    gpost_ref, gkv_ref, gpre_ref, wkv_ref, win_ref, cos_ref, sin_ref,
                    x1_ref, q_ref, k_ref, v_ref, zg1_ref, qm1_ref, zm1_ref, win0_ref, win1_ref, win2_ref):
    hh = _sigmoid(opre_ref[...]) * hs_ref[:, 0, :]
    parts = []
    for h in range(A_HEADS):
        v = hh[:, h * A_HDIM:(h + 1) * A_HDIM]
        mu = jnp.mean(v, axis=-1, keepdims=True)
        var = jnp.mean(jnp.square(v - mu), axis=-1, keepdims=True)
        parts.append((v - mu) * lax.rsqrt(var + EPS))
    y = jnp.concatenate(parts, axis=-1) * ghn_ref[...] + skip_ref[...] * xc_ref[...]
    ymix = (y * _silu(zg_ref[...])).astype(BF16)
    ym = (_heads_to_rows(ym_ref) * _silu(zm_ref[...])).astype(BF16)
    out = _dot(ymix, wout_ref[0:A_INNER, :]) + _dot(ym, wout_ref[A_INNER:A_INNER + M_WIDTH, :])
    x1 = x_ref[:, 0, :] + _rms_scale(out) * gpost_ref[...]
    x1_ref[...] = x1
    xn = _rms_scale(x1)
    hk = (xn * gkv_ref[...]).astype(BF16)
    hq = (xn * gpre_ref[...]).astype(BF16)
    cos = cos_ref[...]
    sin = sin_ref[...]
    ks, vs = [], []
    for g in range(N_GROUPS):
        ks.append(_rope_cols(_dot(hk, wkv_ref[:, g * 2 * B_WIDTH:g * 2 * B_WIDTH + B_WIDTH]), cos, sin))
        vs.append(_dot(hk, wkv_ref[:, g * 2 * B_WIDTH + B_WIDTH:(g + 1) * 2 * B_WIDTH]))
    k_ref[...] = _rows_to_heads(jnp.concatenate(ks, axis=-1))
    v_ref[...] = _rows_to_heads(jnp.concatenate(vs, axis=-1))
    for g, wref in enumerate((win0_ref, win1_ref, win2_ref)):
        wref[...] = _rows_to_kv_heads(ks[g], vs[g])
    qoff = N_GROUPS * B_WIDTH
    q_ref[...] = _rows_to_heads(_rope_cols(_dot(hq, win_ref[:, 0:qoff]), cos, sin))
    zg1_ref[...] = _dot(hq, win_ref[:, qoff:qoff + B_WIDTH])
    qm1_ref[...] = _rows_to_heads(_dot(hq, win_ref[:, qoff + B_WIDTH:qoff + B_WIDTH + M_WIDTH]))
    zm1_ref[...] = _dot(hq, win_ref[:, qoff + B_WIDTH + M_WIDTH:qoff + B_WIDTH + 2 * M_WIDTH])


def _dec_mid(hs, opre, xc, zg, ym, zm, x, ghn, skip, w_out, g_post, g_kv, g_pre, wkv, win, cos, sin):
    nb = x.shape[0]
    args = (hs, opre, xc, zg, ym, zm, x, ghn, skip, w_out, g_post, g_kv, g_pre, wkv, win, cos, sin)
    f = lambda *s: jax.ShapeDtypeStruct(s, F32)
    gh = N_GROUPS * B_HEADS
    out_shape = [f(nb, D_MODEL), f(nb, gh, B_HDIM), f(nb, gh, B_HDIM), f(nb, gh, B_HDIM),
                 f(nb, B_WIDTH), f(nb, M_HEADS, M_HDIM), f(nb, M_WIDTH)] + [f(nb, 2 * B_HEADS, B_HDIM)] * N_GROUPS
    return pl.pallas_call(
        _dec_mid_kernel,
        grid=(1,),
        in_specs=[_whole(a.shape) for a in args],
        out_specs=[_whole(o.shape) for o in out_shape],
        out_shape=out_shape,
        compiler_params=pltpu.CompilerParams(dimension_semantics=("arbitrary",), vmem_limit_bytes=VMEM_LIMIT),
        name="dec_mid",
    )(*args)


def _dec_attn_kernel(q_ref, kn_ref, vn_ref, w0_ref, w1_ref, w2_ref, qm_ref, kv_ref, ydil_ref, ym_ref):
    w_refs = (w0_ref, w1_ref, w2_ref)
    scale = B_HDIM ** -0.5
    groups = range(N_GROUPS)
    hsl = [slice(g * B_HEADS, (g + 1) * B_HEADS) for g in groups]
    q = [q_ref[0, hsl[g], :] * scale for g in groups]
    s_c = [jnp.sum(w_refs[g][0, :, 0] * q[g][None], axis=-1, keepdims=True) for g in groups]
    s_n = [jnp.sum(kn_ref[0, hsl[g], :] * q[g], axis=-1, keepdims=True) for g in groups]
    mxs = [jnp.maximum(jnp.max(s_c[g], axis=0), s_n[g]) for g in groups]
    p_c = [jnp.exp(s_c[g] - mxs[g][None]) for g in groups]
    p_n = [jnp.exp(s_n[g] - mxs[g]) for g in groups]
    ls = [jnp.sum(p_c[g], axis=0) + p_n[g] for g in groups]
    outs = [(jnp.sum(p_c[g] * w_refs[g][0, :, 1], axis=0) + p_n[g] * vn_ref[0, hsl[g], :]) / ls[g]
            for g in groups]
    lses = [mxs[g] + jnp.log(ls[g]) for g in groups]
    mx = jnp.maximum(jnp.maximum(lses[0], lses[1]), lses[2])
    es = [jnp.exp(l - mx) for l in lses]
    tot = es[0] + es[1] + es[2]
    ydil_ref[0] = (es[0] / tot) * outs[0] + (es[1] / tot) * outs[1] + (es[2] / tot) * outs[2]
    ym_ref[0] = _dec_mem_attention(qm_ref[0], kv_ref.at[0, 0])


def _dec_attn_specs(q4, kn4, vn4, cw0, cw1, cw2, qm3, cache_mem_kv, layer, seq_index):
    nb = q4.shape[0]
    rows = B_GROUPS[0][0]

    def at(*tail):
        return lambda *idx: (seq_index(*idx),) + tail

    win_specs = [pl.BlockSpec((1, rows, 2, B_HEADS, B_HDIM), at(0, 0, 0, 0)),
                 pl.BlockSpec((1, rows, None, 2, B_HEADS, B_HDIM), at(0, 0, 0, 0, 0)),
                 pl.BlockSpec((1, rows, None, 2, B_HEADS, B_HDIM), at(0, 0, 0, 0, 0))]
    in_specs = [pl.BlockSpec((1, N_GROUPS * B_HEADS, B_HDIM), at(0, 0))] * 3 + win_specs + [
        pl.BlockSpec((1, M_HEADS, M_HDIM), at(0, 0)),
        pl.BlockSpec((1, 1, N_MEM, 2, M_HEADS, M_HDIM), lambda *idx: (layer, seq_index(*idx), 0, 0, 0, 0))]
    out_specs = [pl.BlockSpec((1, B_HEADS, B_HDIM), at(0, 0)), pl.BlockSpec((1, M_HEADS, M_HDIM), at(0, 0))]
    out_shapes = [jax.ShapeDtypeStruct((nb, B_HEADS, B_HDIM), F32),
                  jax.ShapeDtypeStruct((nb, M_HEADS, M_HDIM), F32)]
    return [q4, kn4, vn4, cw0, cw1, cw2, qm3, cache_mem_kv], in_specs, out_specs, out_shapes


def _dec_attn(q4, kn4, vn4, cw0, cw1, cw2, qm3, cache_mem_kv, layer):
    args, in_specs, out_specs, out_shapes = _dec_attn_specs(
        q4, kn4, vn4, cw0, cw1, cw2, qm3, cache_mem_kv, layer, seq_index=lambda b: b)
    return pl.pallas_call(
        _dec_attn_kernel,
        grid=(q4.shape[0],),
        in_specs=in_specs,
        out_specs=out_specs,
        out_shape=out_shapes,
        compiler_params=pltpu.CompilerParams(dimension_semantics=("arbitrary",), vmem_limit_bytes=VMEM_LIMIT),
        name="dec_attn",
    )(*args)


def _dec_out_kernel(ydil_ref, zg_ref, ym_ref, zm_ref, x_ref, wout_ref, gpost_ref, y_ref):
    ymix = (_heads_to_rows(ydil_ref) * _silu(zg_ref[...])).astype(BF16)
    ym = (_heads_to_rows(ym_ref) * _silu(zm_ref[...])).astype(BF16)
    out = _dot(ymix, wout_ref[0:B_WIDTH, :]) + _dot(ym, wout_ref[B_WIDTH:B_WIDTH + M_WIDTH, :])
    y_ref[:, 0, :] = x_ref[...] + _rms_scale(out) * gpost_ref[...]


def _dec_out(ydil, zg, ym, zm, x1, w_out, g_post):
    args = (ydil, zg, ym, zm, x1, w_out, g_post)
    out_shape = (x1.shape[0], 1, x1.shape[1])
    return pl.pallas_call(
        _dec_out_kernel,
        grid=(1,),
        in_specs=[_whole(a.shape) for a in args],
        out_specs=_whole(out_shape),
        out_shape=jax.ShapeDtypeStruct(out_shape, F32),
        compiler_params=pltpu.CompilerParams(dimension_semantics=("arbitrary",), vmem_limit_bytes=VMEM_LIMIT),
        name="dec_out",
    )(*args)


class _SampleGroup:
    def __init__(self, x_sample, state_conv, state_c, state_n, state_m, cache_wins, cache_mem_kv, p):
        self.p = p
        self.nb = nb = x_sample.shape[0]
        self.cache_wins = cache_wins
        self.cache_mem_kv = cache_mem_kv
        self.x = x_sample
        q, k, v, gates, self.xc, self.opre, self.zg, qm3, self.zm, self.conv_s = _dec_l0_proj(
            x_sample, p['g_pre'][0:1], p['w_in_a'][0], state_conv, p['conv_w_a'][0], p['conv_b_a'],
            p['w_q_a'][0], p['w_k_a'][0], p['w_v_a'][0], p['w_if_a'], p['b_if_a'])
        self._mlstm_job = (q, k, v, gates, state_m, state_c, state_n, qm3, cache_mem_kv)
        self.attn_job = None

    def mlstm_job(self):
        return self._mlstm_job

    def after_mlstm(self, res):
        p, nb = self.p, self.nb
        hs, self.c_s, self.n_s, m_rows, ym0 = res
        self.m_s = m_rows[:, :, 0][None]
        cos, sin = _rope_tables(PAST_LEN + jnp.arange(1, dtype=F32))
        self.x1, qd, kn, vn, self.zg1, qm1, self.zm1, w0, w1, w2 = _dec_mid(
            hs, self.opre, self.xc, self.zg, ym0, self.zm, self.x,
            p['g_hn_a'], p['skip_a'], p['w_out_a'][0], p['g_post'][0:1], p['g_kv'], p['g_pre'][1:2],
            p['w_kv_b'], p['w_in_b'][0], cos, sin)
        cws = [self.cache_wins[0]]
        for g in (1, 2):
            w, d = B_GROUPS[g]
            cws.append(self.cache_wins[g].reshape(nb, w // d, d, 2, B_HEADS, B_HDIM))
        self.attn_job = (qd, kn, vn, cws[0], cws[1], cws[2], qm1, self.cache_mem_kv, 1)
        self.wins_s = [w.reshape(nb, 1, 2, B_HEADS, B_HDIM) for w in (w0, w1, w2)]

    def after_attn(self, res):
        p = self.p
        ydil, ym1 = res
        self.y = _dec_out(ydil, self.zg1, ym1, self.zm1, self.x1, p['w_out_b'][0], p['g_post'][1:2])

    def outputs(self):
        return self.y, self.conv_s, self.c_s, self.n_s, self.m_s, self.wins_s


def kernel(x_prompt, x_sample, mem_prompt, state_conv, state_C, state_n, state_m, cache_win0, cache_win1,
           cache_win2, cache_mem_kv, g_pre, g_post, w_in_a, conv_w_a, conv_b_a, w_q_a, w_k_a, w_v_a, w_if_a,
           b_if_a, g_hn_a, skip_a, w_out_a, g_kv, w_kv_b, w_in_b, w_out_b, w_mkv):
    p = _prep_params(g_pre, g_post, w_in_a, conv_w_a, conv_b_a, w_q_a, w_k_a, w_v_a, w_if_a, b_if_a,
                     g_hn_a, skip_a, w_out_a, g_kv, w_kv_b, w_in_b, w_out_b, w_mkv)
    sample = _SampleGroup(x_sample, state_conv, state_C, state_n, state_m,
                          (cache_win0, cache_win1, cache_win2), cache_mem_kv, p)
    y_p, conv_p, c_p, n_p, m_p, wins_p, memkv_p = _prompt_group(x_prompt, mem_prompt, p, sample)
    y_s, conv_s, c_s, n_s, m_s, wins_s = sample.outputs()
    return (y_p, y_s, conv_p, c_p, n_p, m_p, wins_p[0], wins_p[1], wins_p[2], memkv_p,
            conv_s, c_s, n_s, m_s, wins_s[0], wins_s[1], wins_s[2])
```

```python
import functools

import jax
import jax.numpy as jnp
from jax import lax
from jax.experimental import pallas as pl
from jax.experimental.pallas import tpu as pltpu

F32 = jnp.float32
BF16 = jnp.bfloat16

D_MODEL = 1024
A_HEADS = 4
A_HDIM = 256
A_INNER = 1024
CONV_W = 4
A_CHUNK = 128
B_GROUPS = ((128, 1), (512, 4), (2048, 16))
N_GROUPS = 3
B_HEADS = 4
B_HDIM = 128
B_WIDTH = 512
N_MEM = 256
M_HEADS = 4
M_HDIM = 128
M_WIDTH = 512
ROPE_THETA = 10000.0
EPS = 1e-6
PAST_LEN = 8192

LANES = 128
TOK_TILE = 512
L0_TILE = 256
L1OUT_TILE = 1024
ATT_BLK = 128
ATT_TILE = 2048
ATT_GROUP = 2
VMEM_LIMIT = 56 * 1024 * 1024

NT_DIMS = (((1,), (1,)), ((), ()))
LOG2E = 1.4426950408889634


def _dot(a, b):
    return jnp.dot(a, b, preferred_element_type=F32)


def _dot_nt(a, b):
    return lax.dot_general(a, b, NT_DIMS, preferred_element_type=F32)


def _sigmoid(x):
    return 1.0 / (1.0 + jnp.exp(-x))


def _silu(x):
    return x * _sigmoid(x)


def _log_sigmoid(x):
    return jnp.minimum(x, 0.0) - jnp.log(1.0 + jnp.exp(-jnp.abs(x)))


def _rms_scale(x):
    return x * lax.rsqrt(jnp.mean(x * x, axis=-1, keepdims=True) + EPS)


def _const_spec(shape):
    nd = len(shape)
    return pl.BlockSpec(shape, lambda *_: (0,) * nd, pipeline_mode=pl.Buffered(1))


def _mem_attention(qm, mk, mv):
    heads = range(M_HEADS)
    sl = [slice(h * M_HDIM, (h + 1) * M_HDIM) for h in heads]
    s = [_dot_nt(qm[:, sl[h]], mk[:, sl[h]]) for h in heads]
    mx = [jnp.max(s[h], axis=-1, keepdims=True) for h in heads]
    p = [jnp.exp2((s[h] - mx[h]) * (M_HDIM ** -0.5 * LOG2E)) for h in heads]
    l = [jnp.sum(p[h], axis=-1, keepdims=True) for h in heads]
    outs = [_dot(p[h].astype(BF16), mv[:, sl[h]]) / l[h] for h in heads]
    return jnp.concatenate(outs, axis=-1)


def _rows_to_heads(x):
    return jnp.swapaxes(jnp.stack([x[:, h * LANES:(h + 1) * LANES] for h in range(x.shape[1] // LANES)]), 0, 1)


def _heads_to_rows(ref):
    return jnp.concatenate([ref[:, h, :] for h in range(ref.shape[1])], axis=-1)


def _rows_to_kv_heads(k, v):
    pieces = [a[:, h * LANES:(h + 1) * LANES] for a in (k, v) for h in range(a.shape[1] // LANES)]
    return jnp.swapaxes(jnp.stack(pieces), 0, 1)


def _memkv_kernel(m_ref, w_ref, o_ref, ob_ref):
    r = _dot(m_ref[...].astype(BF16), w_ref[0])
    o_ref[0] = _rows_to_kv_heads(r[:, 0:M_WIDTH], r[:, M_WIDTH:2 * M_WIDTH])
    ob_ref[0] = r.astype(BF16)


def _memkv(mem2d, w_bf):
    nm = mem2d.shape[0]
    nl = w_bf.shape[0]
    tm = min(512, nm)
    return pl.pallas_call(
        _memkv_kernel,
        grid=(nl, nm // tm),
        in_specs=[pl.BlockSpec((tm, D_MODEL), lambda l, i: (i, 0)),
                  pl.BlockSpec((1, D_MODEL, 2 * M_WIDTH), lambda l, i: (l, 0, 0))],
        out_specs=[pl.BlockSpec((1, tm, 2 * M_HEADS, M_HDIM), lambda l, i: (l, i, 0, 0)),
                   pl.BlockSpec((1, tm, 2 * M_WIDTH), lambda l, i: (l, i, 0))],
        out_shape=[jax.ShapeDtypeStruct((nl, nm, 2 * M_HEADS, M_HDIM), F32),
                   jax.ShapeDtypeStruct((nl, nm, 2 * M_WIDTH), BF16)],
        compiler_params=pltpu.CompilerParams(dimension_semantics=("arbitrary", "arbitrary")),
        name="memkv",
    )(mem2d, w_bf)


def _mlstm_chunk(rs, g, qkv_v, kt_v, c_s, n_s, m_s, causal, hs):
    ls = _log_sigmoid(g)
    tok = lax.broadcasted_iota(jnp.int32, (A_CHUNK, LANES), 0)
    bc = ls
    shift = 1
    while shift < A_CHUNK:
        bc = bc + jnp.where(tok >= shift, pltpu.roll(bc, shift, 0), 0.0)
        shift *= 2
    lane = lax.broadcasted_iota(jnp.int32, (A_CHUNK, LANES), 1)
    xt = jnp.where(lane < A_HEADS, g, bc).T
    yield
    heads = range(A_HEADS)
    b_col = [bc[:, 4 + h:5 + h] for h in heads]
    b_row = [xt[4 + h:5 + h, :] for h in heads]
    li_row = [xt[h:h + 1, :] for h in heads]
    li_col = [g[:, h:h + 1] for h in heads]
    m_old = [m_s[h:h + 1, 0:1] for h in heads]
    b_last = [bc[A_CHUNK - 1:A_CHUNK, 4 + h:5 + h] for h in heads]
    qh = [qkv_v[rs, h * 3 * A_HDIM:h * 3 * A_HDIM + A_HDIM] for h in heads]
    kh = [qkv_v[rs, h * 3 * A_HDIM + A_HDIM:h * 3 * A_HDIM + 2 * A_HDIM] for h in heads]
    vh = [qkv_v[rs, h * 3 * A_HDIM + 2 * A_HDIM:(h + 1) * 3 * A_HDIM] for h in heads]
    kt = [kt_v[h] for h in heads]
    c_old = [c_s[h] for h in heads]
    n_old = [n_s[h:h + 1, :] for h in heads]
    qk = [_dot_nt(qh[h], kh[h]) for h in heads]
    qc = [_dot(qh[h], c_old[h].astype(BF16)) for h in heads]
    dm = [jnp.where(causal, b_col[h] - b_row[h] + li_row[h], -jnp.inf) for h in heads]
    inter = [b_col[h] + m_old[h] for h in heads]
    m_row = [jnp.maximum(inter[h], jnp.max(dm[h], axis=-1, keepdims=True)) for h in heads]
    g_max = [jnp.max(b_last[h] - b_row[h] + li_row[h], axis=-1, keepdims=True) for h in heads]
    m_new = [jnp.maximum(b_last[h] + m_old[h], g_max[h]) for h in heads]
    yield
    sc = [qk[h] * jnp.exp(dm[h] - m_row[h]) for h in heads]
    dec = [jnp.exp(inter[h] - m_row[h]) for h in heads]
    ws_col = [jnp.exp(b_last[h] - b_col[h] + li_col[h] - m_new[h]) for h in heads]
    dc = [jnp.exp(b_last[h] + m_old[h] - m_new[h]) for h in heads]
    yield
    sv = [_dot(sc[h].astype(BF16), vh[h]) for h in heads]
    wv = [(ws_col[h] * vh[h].astype(F32)).astype(BF16) for h in heads]
    upd = [_dot(kt[h], wv[h]) for h in heads]
    yield
    for h in heads:
        den = (jnp.sum(sc[h], axis=-1, keepdims=True)
               + dec[h] * jnp.sum(qh[h].astype(F32) * n_old[h], axis=-1, keepdims=True))
        num = sv[h] + dec[h] * qc[h]
        hs.append(num / jnp.maximum(jnp.abs(den), jnp.exp(-m_row[h])))
    yield
    for h in heads:
        c_s[h] = dc[h] * c_old[h] + upd[h]
        n_s[h:h + 1, :] = dc[h] * n_old[h] + jnp.sum(ws_col[h] * kh[h].astype(F32), axis=0, keepdims=True)
        m_s[h:h + 1, :] = jnp.broadcast_to(m_new[h], (1, LANES))
    yield


def _l0p_kernel(nt, n_dec_seq, *refs):
    n_dec_in, n_dec_out = (9, 5) if n_dec_seq else (0, 0)
    (x_ref, xp_ref, gpre_ref, win_ref, convw_ref, convb_ref, wq_ref, wk_ref, wv_ref,
     wif_ref, bif_ref, ghn_ref, skip_ref, mkv_ref, wout_ref, gpost_ref) = refs[0:16]
    dec_in = refs[16:16 + n_dec_in]
    x1_ref, conv_out, c_out, n_out, m_out = refs[16 + n_dec_in:21 + n_dec_in]
    dec_out = refs[21 + n_dec_in:21 + n_dec_in + n_dec_out]
    (h_s, u_s, ymix_s, xc_s, opre_s, zg_s, qm_s, zm_s, qkv_s, kt_s, gates_s,
     c_s, n_s, m_s) = refs[21 + n_dec_in + n_dec_out:]
    tt = x_ref.shape[1]
    nsub = tt // A_CHUNK
    t = pl.program_id(0)
    parity = lax.rem(t + 1, 2)
    pos1 = lax.rem(t + nt - 1, nt)
    pos2 = lax.rem(t + 2 * nt - 2, nt)

    @pl.when(t == 0)
    def _():
        h_s[...] = jnp.zeros(h_s.shape, BF16)
        u_s[...] = jnp.zeros(u_s.shape, F32)
        xc_s[0] = jnp.zeros(xc_s.shape[1:], F32)
        opre_s[0] = jnp.zeros(opre_s.shape[1:], F32)
        zg_s[0] = jnp.zeros(zg_s.shape[1:], F32)
        qm_s[0] = jnp.zeros(qm_s.shape[1:], BF16)
        zm_s[0] = jnp.zeros(zm_s.shape[1:], F32)
        qkv_s[0] = jnp.zeros(qkv_s.shape[1:], BF16)
        kt_s[0] = jnp.zeros(kt_s.shape[1:], BF16)
        gates_s[0] = jnp.zeros(gates_s.shape[1:], F32)

    @pl.when(pos1 == 0)
    def _():
        u_s[0:8, :] = jnp.zeros((8, A_INNER), F32)

    @pl.when(pos2 == 0)
    def _():
        c_s[...] = jnp.zeros(c_s.shape, F32)
        n_s[...] = jnp.zeros(n_s.shape, F32)
        m_s[...] = jnp.zeros(m_s.shape, F32)

    row = lax.broadcasted_iota(jnp.int32, (A_CHUNK, A_CHUNK), 0)
    col = lax.broadcasted_iota(jnp.int32, (A_CHUNK, A_CHUNK), 1)
    causal = col <= row
    ghn = ghn_ref[...]
    skp = skip_ref[...]
    mk = mkv_ref[0, :, 0:M_WIDTH]
    mv = mkv_ref[0, :, M_WIDTH:2 * M_WIDTH]

    def stage2(pslot):
        ym = _mem_attention(qm_s[pslot], mk, mv) * _silu(zm_s[pslot])
        ymix_s[:, A_INNER:A_INNER + M_WIDTH] = ym.astype(BF16)
        yield
        for c in range(nsub):
            rs = slice(c * A_CHUNK, (c + 1) * A_CHUNK)
            hs = []
            yield from _mlstm_chunk(rs, gates_s[pslot, rs, :], qkv_s.at[pslot], kt_s.at[pslot, :, c],
                                    c_s, n_s, m_s, causal, hs)
            parts = []
            for h in range(A_HEADS):
                v = _sigmoid(opre_s[pslot, rs, h * A_HDIM:(h + 1) * A_HDIM]) * hs[h]
                mu = jnp.mean(v, axis=-1, keepdims=True)
                var = jnp.mean(jnp.square(v - mu), axis=-1, keepdims=True)
                parts.append((v - mu) * lax.rsqrt(var + EPS))
            hn = jnp.concatenate(parts, axis=-1) * ghn
            y = hn + skp * xc_s[pslot, rs, :]
            ymix_s[rs, 0:A_INNER] = (y * _silu(zg_s[pslot, rs, :])).astype(BF16)
            yield

    def stage1(slot):
        hb = h_s[...]
        u_s[8:8 + tt, :] = _dot(hb, win_ref[:, 0:A_INNER])
        yield
        cw = convw_ref[...]
        cb = convb_ref[...]
        for c in range(nsub):
            r0 = c * A_CHUNK
            blk = u_s[r0:r0 + A_CHUNK + 8, :]
            xc = cb + pltpu.roll(blk, 3, 0)[8:, :] * cw[0:1, :]
            xc = xc + pltpu.roll(blk, 2, 0)[8:, :] * cw[1:2, :]
            xc = xc + pltpu.roll(blk, 1, 0)[8:, :] * cw[2:3, :]
            xc = xc + blk[8:, :] * cw[3:4, :]
            xc_s[slot, r0:r0 + A_CHUNK, :] = _silu(xc)
        opre_s[slot] = _dot(hb, win_ref[:, A_INNER:2 * A_INNER])
        yield
        zg_s[slot] = _dot(hb, win_ref[:, 2 * A_INNER:3 * A_INNER])
        yield
        qm_s[slot] = _dot(hb, win_ref[:, 3 * A_INNER:3 * A_INNER + M_WIDTH]).astype(BF16)
        zm_s[slot] = _dot(hb, win_ref[:, 3 * A_INNER + M_WIDTH:3 * A_INNER + 2 * M_WIDTH])
        yield
        for h in range(A_HEADS):
            sl = slice(h * A_HDIM, (h + 1) * A_HDIM)
            xh = xc_s[slot, :, sl].astype(BF16)
            uh = u_s[8:8 + tt, sl].astype(BF16)
            base = h * 3 * A_HDIM
            qkv_s[slot, :, base:base + A_HDIM] = _dot(xh, wq_ref[h]).astype(BF16)
            kf = _dot(xh, wk_ref[h]) * (A_HDIM ** -0.5)
            qkv_s[slot, :, base + A_HDIM:base + 2 * A_HDIM] = kf.astype(BF16)
            qkv_s[slot, :, base + 2 * A_HDIM:base + 3 * A_HDIM] = _dot(uh, wv_ref[h]).astype(BF16)
            kt = kf.T.astype(BF16)
            for c in range(nsub):
                kt_s[slot, h, c] = kt[:, c * A_CHUNK:(c + 1) * A_CHUNK]
            yield
        gates_s[slot] = _dot(qkv_s[slot], wif_ref[...]) + bif_ref[...]
        yield

    def step(slot):
        if n_dec_seq:
            first = slot == 1
            heads = (0, 1) if first else (2, 3)
            _dec_mlstm_body(jnp.minimum(t // 2, n_dec_seq - 1), heads, first, *dec_in, *dec_out)
        pending = [stage2(1 - slot), stage1(slot)]
        while pending:
            for gen in list(pending):
                try:
                    next(gen)
                except StopIteration:
                    pending.remove(gen)
        h_next = (_rms_scale(x_ref[0]) * gpre_ref[...]).astype(BF16)
        out = _dot(ymix_s[...], wout_ref[...])
        h_s[...] = h_next
        x1_ref[0] = xp_ref[0] + _rms_scale(out) * gpost_ref[...]

    for s in range(2):
        pl.when(parity == s)(functools.partial(step, s))

    @pl.when(jnp.logical_and(pos1 == nt - 1, t > 0))
    def _():
        conv_out[0, 0] = u_s[tt + 5:tt + 8, :]

    u_s[0:8, :] = u_s[tt:tt + 8, :]

    @pl.when(jnp.logical_and(pos2 == nt - 1, t > 1))
    def _():
        for h in range(A_HEADS):
            c_out[0, 0, h] = c_s[h].T
        n_out[0, 0] = n_s[0:A_HEADS, :]
        m_out[0] = m_s[...]


def _layer0_prompt_pipelined(x, g_pre, w_in, conv_w, conv_b, wq, wk, wv, wif, bif, ghn, skip, mkv_bf,
                             w_out, g_post, dec_job=None):
    b, s, _ = x.shape
    tt = min(L0_TILE, s)
    nt = s // tt
    ntiles = b * nt
    a_in = w_in.shape[1]
    nsub = tt // A_CHUNK

    def cur(t):
        t1 = jnp.minimum(t, ntiles - 1)
        return (t1 // nt, t1 % nt, 0)

    def prev(t):
        t2 = jnp.maximum(t - 2, 0)
        return (t2 // nt, t2 % nt, 0)

    def prev_b(t):
        return jnp.maximum(t - 2, 0) // nt

    in_specs = [
        pl.BlockSpec((1, tt, D_MODEL), cur),
        pl.BlockSpec((1, tt, D_MODEL), prev),
        _const_spec((1, D_MODEL)),
        _const_spec((D_MODEL, a_in)),
        _const_spec((CONV_W, A_INNER)),
        _const_spec((1, A_INNER)),
        _const_spec((A_HEADS, A_HDIM, A_HDIM)),
        _const_spec((A_HEADS, A_HDIM, A_HDIM)),
        _const_spec((A_HEADS, A_HDIM, A_HDIM)),
        _const_spec((3 * A_INNER, LANES)),
        _const_spec((1, LANES)),
        _const_spec((1, A_INNER)),
        _const_spec((1, A_INNER)),
        pl.BlockSpec((1, N_MEM, 2 * M_WIDTH), lambda t: (prev_b(t), 0, 0)),
        _const_spec((A_INNER + M_WIDTH, D_MODEL)),
        _const_spec((1, D_MODEL)),
    ]
    out_specs = [
        pl.BlockSpec((1, tt, D_MODEL), prev),
        pl.BlockSpec((1, 1, CONV_W - 1, A_INNER), lambda t: (0, prev_b(t), 0, 0)),
        pl.BlockSpec((1, 1, A_HEADS, A_HDIM, A_HDIM), lambda t: (0, prev_b(t), 0, 0, 0)),
        pl.BlockSpec((1, 1, A_HEADS, A_HDIM), lambda t: (0, prev_b(t), 0, 0)),
        pl.BlockSpec((1, 8, LANES), lambda t: (prev_b(t), 0, 0)),
    ]
    out_shape = [
        jax.ShapeDtypeStruct((b, s, D_MODEL), F32),
        jax.ShapeDtypeStruct((1, b, CONV_W - 1, A_INNER), F32),
        jax.ShapeDtypeStruct((1, b, A_HEADS, A_HDIM, A_HDIM), F32),
        jax.ShapeDtypeStruct((1, b, A_HEADS, A_HDIM), F32),
        jax.ShapeDtypeStruct((b, 8, LANES), F32),
    ]
    scratch = [
        pltpu.VMEM((tt, D_MODEL), BF16),
        pltpu.VMEM((tt + 8, A_INNER), F32),
        pltpu.VMEM((tt, A_INNER + M_WIDTH), BF16),
        pltpu.VMEM((2, tt, A_INNER), F32),
        pltpu.VMEM((2, tt, A_INNER), F32),
        pltpu.VMEM((2, tt, A_INNER), F32),
        pltpu.VMEM((2, tt, M_WIDTH), BF16),
        pltpu.VMEM((2, tt, M_WIDTH), F32),
        pltpu.VMEM((2, tt, 3 * A_INNER), BF16),
        pltpu.VMEM((2, A_HEADS, nsub, A_HDIM, A_CHUNK), BF16),
        pltpu.VMEM((2, tt, LANES), F32),
        pltpu.VMEM((A_HEADS, A_HDIM, A_HDIM), F32),
        pltpu.VMEM((8, A_HDIM), F32),
        pltpu.VMEM((8, LANES), F32),
    ]
    dec_args, dec_specs, dec_out_specs, dec_out_shapes, n_dec_seq = [], [], [], [], 0
    if dec_job is not None:
        n_dec_seq = dec_job[0].shape[0]
        dec_args, dec_specs, dec_out_specs, dec_out_shapes = _dec_mlstm_specs(
            *dec_job, seq_index=lambda t: jnp.minimum(t // 2, n_dec_seq - 1))
    return pl.pallas_call(
        functools.partial(_l0p_kernel, nt, n_dec_seq),
        grid=(ntiles + 2,),
        in_specs=in_specs + dec_specs,
        out_specs=out_specs + dec_out_specs,
        out_shape=out_shape + dec_out_shapes,
        scratch_shapes=scratch,
        compiler_params=pltpu.CompilerParams(
            dimension_semantics=("arbitrary",), vmem_limit_bytes=VMEM_LIMIT),
        name="layer0_prompt",
    )(x, x, g_pre, w_in, conv_w, conv_b, wq, wk, wv, wif, bif, ghn, skip, mkv_bf, w_out, g_post,
      *dec_args)


def _rope_cols(x, cos, sin_signed):
    outs = []
    for cblk in range(x.shape[1] // B_HDIM):
        xb = x[:, cblk * B_HDIM:(cblk + 1) * B_HDIM]
        outs.append(xb * cos + pltpu.roll(xb, B_HDIM // 2, 1) * sin_signed)
    return jnp.concatenate(outs, axis=-1)


def _l1a_kernel(n_dec_in, *refs):
    x_ref, gkv_ref, gpre_ref, wkv_ref, win_ref, cos_ref, sin_ref = refs[0:7]
    dec_in = refs[7:7 + n_dec_in]
    (q0_ref, q1_ref, q2_ref, k0_ref, k1_ref, k2_ref, v0_ref, v1_ref, v2_ref,
     zg_ref, qm_ref, zm_ref, w0_ref, w1_ref, w2_ref) = refs[7 + n_dec_in:22 + n_dec_in]
    dec_out = refs[22 + n_dec_in:]
    tt = x_ref.shape[1]
    xn = _rms_scale(x_ref[0])
    hk = (xn * gkv_ref[...]).astype(BF16)
    hq = (xn * gpre_ref[...]).astype(BF16)
    cos = cos_ref[...]
    sin = sin_ref[...]
    q_refs = (q0_ref, q1_ref, q2_ref)
    k_refs = (k0_ref, k1_ref, k2_ref)
    v_refs = (v0_ref, v1_ref, v2_ref)
    w_refs = (w0_ref, w1_ref, w2_ref)
    for g in (2, 1, 0):
        d = B_GROUPS[g][1]
        kf = _rope_cols(_dot(hk, wkv_ref[:, g * 2 * B_WIDTH:g * 2 * B_WIDTH + B_WIDTH]), cos, sin)
        vf = _dot(hk, wkv_ref[:, g * 2 * B_WIDTH + B_WIDTH:(g + 1) * 2 * B_WIDTH])
        qf = _rope_cols(_dot(hq, win_ref[:, g * B_WIDTH:(g + 1) * B_WIDTH]), cos, sin)
        wr = w_refs[g]
        wrows = wr.shape[1]
        wr[0] = _rows_to_kv_heads(kf[tt - wrows:, :], vf[tt - wrows:, :])
        for val, ref in ((qf.astype(BF16), q_refs[g]), (kf.astype(BF16), k_refs[g]), (vf.astype(BF16), v_refs[g])):
            if d == 1:
                ref[0, 0] = val
            else:
                ref[0] = jnp.swapaxes(val.reshape(tt // d, d, val.shape[1]), 0, 1)
        if g == 2 and n_dec_in:
            _dec_attn_kernel(*dec_in, *dec_out)
    qoff = N_GROUPS * B_WIDTH
    zg_ref[0] = _dot(hq, win_ref[:, qoff:qoff + B_WIDTH]).astype(BF16)
    qm_ref[0] = _dot(hq, win_ref[:, qoff + B_WIDTH:qoff + B_WIDTH + M_WIDTH]).astype(BF16)
    zm_ref[0] = _dot(hq, win_ref[:, qoff + B_WIDTH + M_WIDTH:qoff + B_WIDTH + 2 * M_WIDTH]).astype(BF16)


def _layer1_proj_prompt(x1, g_kv, g_pre, wkv, win, cos_t, sin_t, dec_job=None):
    b, s, _ = x1.shape
    tt = min(TOK_TILE, s)
    nt = s // tt
    tile = lambda bb, i: (bb, i, 0)
    in_specs = [
        pl.BlockSpec((1, tt, D_MODEL), tile),
        _const_spec((1, D_MODEL)),
        _const_spec((1, D_MODEL)),
        _const_spec(wkv.shape),
        _const_spec(win.shape),
        pl.BlockSpec((tt, B_HDIM), lambda bb, i: (i, 0)),
        pl.BlockSpec((tt, B_HDIM), lambda bb, i: (i, 0)),
    ]
    qkv_specs, qkv_shapes = [], []
    for _ in range(3):
        for (_, d) in B_GROUPS:
            qkv_specs.append(pl.BlockSpec((1, d, tt // d, B_WIDTH), lambda bb, i: (bb, 0, i, 0)))
            qkv_shapes.append(jax.ShapeDtypeStruct((b, d, s // d, B_WIDTH), BF16))
    gate_specs = [pl.BlockSpec((1, tt, B_WIDTH), tile)] * 3
    gate_shapes = [jax.ShapeDtypeStruct((b, s, B_WIDTH), BF16)] * 3
    win_specs, win_shapes = [], []
    for (w, _) in B_GROUPS:
        wr = min(w, s)
        rows = min(wr, tt)
        nblk = wr // rows
        win_specs.append(pl.BlockSpec(
            (1, rows, 2 * B_HEADS, B_HDIM),
            functools.partial(lambda bb, i, nb: (bb, jnp.maximum(i - (nt - nb), 0), 0, 0), nb=nblk)))
        win_shapes.append(jax.ShapeDtypeStruct((b, wr, 2 * B_HEADS, B_HDIM), F32))
    dec_args, dec_specs, dec_out_specs, dec_out_shapes = [], [], [], []
    if dec_job is not None:
        dec_args, dec_specs, dec_out_specs, dec_out_shapes = _dec_attn_specs(
            *dec_job, seq_index=lambda bb, i: bb * nt + i)
    return pl.pallas_call(
        functools.partial(_l1a_kernel, len(dec_args)),
        grid=(b, nt),
        in_specs=in_specs + dec_specs,
        out_specs=qkv_specs + gate_specs + win_specs + dec_out_specs,
        out_shape=qkv_shapes + gate_shapes + win_shapes + dec_out_shapes,
        compiler_params=pltpu.CompilerParams(
            dimension_semantics=("arbitrary", "arbitrary"), vmem_limit_bytes=VMEM_LIMIT),
        name="layer1_proj_prompt",
    )(x1, g_kv, g_pre, wkv, win, cos_t, sin_t, *dec_args)


def _cols_to_lanes(cols):
    t = cols[0].shape[0]
    lane = lax.broadcasted_iota(jnp.int32, (t, LANES), 1)
    acc = jnp.zeros((t, LANES), F32)
    for h, cvec in enumerate(cols):
        acc = jnp.where(lane == h, cvec, acc)
    return acc


def _band_attn_kernel(q_ref, kc_ref, kp_ref, vc_ref, vp_ref, o_ref, lse_ref):
    nres, tq = q_ref.shape[1:3]
    nsb = tq // ATT_BLK
    j = pl.program_id(2)
    row = lax.broadcasted_iota(jnp.int32, (ATT_BLK, 2 * ATT_BLK), 0)
    col = lax.broadcasted_iota(jnp.int32, (ATT_BLK, 2 * ATT_BLK), 1)
    band = jnp.logical_and(col >= row, col <= row + ATT_BLK)
    first_pen = jnp.where(col < ATT_BLK, jnp.where(j > 0, 0.0, -jnp.inf), 0.0)
    scale = B_HDIM ** -0.5
    blocks = [(r, sb) for r in range(nres) for sb in range(nsb)]
    for g0 in range(0, len(blocks), ATT_GROUP):
        grp = blocks[g0:g0 + ATT_GROUP]
        qs, ks, vs, pens = [], [], [], []
        for r, sb in grp:
            rs = slice(sb * ATT_BLK, (sb + 1) * ATT_BLK)
            ps = slice((sb - 1) * ATT_BLK, sb * ATT_BLK)
            for h in range(B_HEADS):
                hs = slice(h * B_HDIM, (h + 1) * B_HDIM)
                qs.append(q_ref[0, r, rs, hs])
                kp = kp_ref[0, r, :, hs] if sb == 0 else kc_ref[0, r, ps, hs]
                vp = vp_ref[0, r, :, hs] if sb == 0 else vc_ref[0, r, ps, hs]
                ks.append(jnp.concatenate([kp, kc_ref[0, r, rs, hs]], axis=0))
                vs.append(jnp.concatenate([vp, vc_ref[0, r, rs, hs]], axis=0))
                pens.append(sb == 0)
        q3 = jnp.stack(qs)
        k3 = jnp.stack(ks)
        v3 = jnp.stack(vs)
        s = jnp.einsum('uqd,ukd->uqk', q3, k3, preferred_element_type=F32)
        s = jnp.stack([s[u] + first_pen if pens[u] else s[u] for u in range(len(pens))])
        s = jnp.where(band[None], s, -jnp.inf)
        mx = jnp.max(s, axis=-1, keepdims=True)
        p = jnp.exp2((s - mx) * (scale * LOG2E))
        l = jnp.sum(p, axis=-1, keepdims=True)
        o = jnp.einsum('uqk,ukd->uqd', p.astype(BF16), v3, preferred_element_type=F32) / l
        lse = mx * scale + jnp.log(l)
        for i, (r, sb) in enumerate(grp):
            rs = slice(sb * ATT_BLK, (sb + 1) * ATT_BLK)
            for h in range(B_HEADS):
                o_ref[0, r, rs, h * B_HDIM:(h + 1) * B_HDIM] = o[i * B_HEADS + h].astype(BF16)
            lse_ref[0, r, rs, :] = _cols_to_lanes([lse[i * B_HEADS + h] for h in range(B_HEADS)])


def _band_attention(q, k, v):
    b, d, ls, _ = q.shape
    tq = min(ATT_TILE, ls)
    nj = ls // tq
    ratio = tq // ATT_BLK
    nres = min(d, ATT_TILE // tq)
    cur = lambda bb, r, j: (bb, r, j, 0)
    prev = lambda bb, r, j: (bb, r, jnp.maximum(j * ratio - 1, 0), 0)
    return pl.pallas_call(
        _band_attn_kernel,
        grid=(b, d // nres, nj),
        in_specs=[pl.BlockSpec((1, nres, tq, B_WIDTH), cur),
                  pl.BlockSpec((1, nres, tq, B_WIDTH), cur),
                  pl.BlockSpec((1, nres, ATT_BLK, B_WIDTH), prev),
                  pl.BlockSpec((1, nres, tq, B_WIDTH), cur),
                  pl.BlockSpec((1, nres, ATT_BLK, B_WIDTH), prev)],
        out_specs=[pl.BlockSpec((1, nres, tq, B_WIDTH), cur),
                   pl.BlockSpec((1, nres, tq, LANES), cur)],
        out_shape=[jax.ShapeDtypeStruct((b, d, ls, B_WIDTH), BF16),
                   jax.ShapeDtypeStruct((b, d, ls, LANES), F32)],
        compiler_params=pltpu.CompilerParams(
            dimension_semantics=("arbitrary", "arbitrary", "arbitrary"), vmem_limit_bytes=VMEM_LIMIT),
        name="band_attention_d%d" % d,
    )(q, k, k, v, v)


def _unpermute(ref):
    d, rows, width = ref.shape[1:]
    if d == 1:
        return ref[0, 0]
    return jnp.swapaxes(ref[0], 0, 1).reshape(d * rows, width)


def _l1c_kernel(x_ref, o0_ref, o1_ref, o2_ref, l0_ref, l1_ref, l2_ref, zg_ref, qm_ref, zm_ref,
                mkv_ref, wout_ref, gpost_ref, y_ref):
    tt = x_ref.shape[1]
    o_refs = (o0_ref, o1_ref, o2_ref)
    l_refs = (l0_ref, l1_ref, l2_ref)
    outs, lses = [], []
    for g, (_, d) in enumerate(B_GROUPS):
        outs.append(_unpermute(o_refs[g]))
        lses.append(_unpermute(l_refs[g])[:, 0:B_HEADS])
    mx = jnp.maximum(jnp.maximum(lses[0], lses[1]), lses[2])
    es = [jnp.exp(l - mx) for l in lses]
    tot = es[0] + es[1] + es[2]
    ws = [(e / tot).astype(BF16) for e in es]
    parts = []
    for h in range(B_HEADS):
        hs = slice(h * B_HDIM, (h + 1) * B_HDIM)
        acc = ws[0][:, h:h + 1] * outs[0][:, hs]
        acc = acc + ws[1][:, h:h + 1] * outs[1][:, hs]
        acc = acc + ws[2][:, h:h + 1] * outs[2][:, hs]
        parts.append(acc)
    ydil = jnp.concatenate(parts, axis=-1)
    ymix = (ydil.astype(F32) * _silu(zg_ref[0].astype(F32))).astype(BF16)
    mk = mkv_ref[0, :, 0:M_WIDTH]
    mv = mkv_ref[0, :, M_WIDTH:2 * M_WIDTH]
    ym = (_mem_attention(qm_ref[0], mk, mv) * _silu(zm_ref[0].astype(F32))).astype(BF16)
    out = _dot(ymix, wout_ref[0:B_WIDTH, :]) + _dot(ym, wout_ref[B_WIDTH:B_WIDTH + M_WIDTH, :])
    y_ref[0] = x_ref[0] + _rms_scale(out) * gpost_ref[...]


def _layer1_out_prompt(x1, os_, ls_, zg, qm, zm, mkv_bf, w_out, g_post):
    b, s, _ = x1.shape
    tt = min(L1OUT_TILE, s)
    nt = s // tt
    tile = lambda bb, i: (bb, i, 0)
    perm = lambda bb, i: (bb, 0, i, 0)
    in_specs = [pl.BlockSpec((1, tt, D_MODEL), tile)]
    for width in (B_WIDTH, LANES):
        for (_, d) in B_GROUPS:
            in_specs.append(pl.BlockSpec((1, d, tt // d, width), perm))
    in_specs += [pl.BlockSpec((1, tt, B_WIDTH), tile)] * 3
    in_specs += [pl.BlockSpec((1, N_MEM, 2 * M_WIDTH), lambda bb, i: (bb, 0, 0)),
                 _const_spec(w_out.shape), _const_spec((1, D_MODEL))]
    return pl.pallas_call(
        _l1c_kernel,
        grid=(b, nt),
        in_specs=in_specs,
        out_specs=pl.BlockSpec((1, tt, D_MODEL), tile),
        out_shape=jax.ShapeDtypeStruct((b, s, D_MODEL), F32),
        compiler_params=pltpu.CompilerParams(
            dimension_semantics=("arbitrary", "arbitrary"), vmem_limit_bytes=VMEM_LIMIT),
        name="layer1_out_prompt",
    )(x1, *os_, *ls_, zg, qm, zm, mkv_bf, w_out, g_post)


def _rope_tables(pos):
    half = B_HDIM // 2
    inv = ROPE_THETA ** (-jnp.arange(half, dtype=F32) / half)
    ang = pos[:, None] * inv[None, :]
    cos = jnp.cos(ang)
    sin = jnp.sin(ang)
    return jnp.concatenate([cos, cos], axis=-1), jnp.concatenate([-sin, sin], axis=-1)


def _prompt_group(x_prompt, mem_prompt, p, sample=None):
    b, s, _ = x_prompt.shape
    memkv_f, memkv_b = _memkv(mem_prompt.reshape(b * N_MEM, D_MODEL), p['w_mkv'])
    depth = memkv_f.shape[0]
    memkv_b = memkv_b.reshape(depth, b, N_MEM, 2 * M_WIDTH)
    job0 = sample.mlstm_job() if sample is not None else None
    if job0 is not None and b * (s // min(L0_TILE, s)) + 2 < 2 * job0[0].shape[0]:
        job0 = None
    outs0 = _layer0_prompt_pipelined(
        x_prompt, p['g_pre'][0:1], p['w_in_a'][0], p['conv_w_a'][0], p['conv_b_a'], p['w_q_a'][0],
        p['w_k_a'][0], p['w_v_a'][0], p['w_if_a'], p['b_if_a'],
        p['g_hn_a'], p['skip_a'], memkv_b[0], p['w_out_a'][0], p['g_post'][0:1], job0)
    x1, conv_p, c_p, n_p, m_pad = outs0[0:5]
    if sample is not None:
        sample.after_mlstm(outs0[5:10] if job0 is not None else _dec_mlstm(*sample.mlstm_job()))
    cos_t, sin_t = _rope_tables(jnp.arange(s, dtype=F32))
    job1 = sample.attn_job if sample is not None else None
    if job1 is not None and b * (s // min(TOK_TILE, s)) != job1[0].shape[0]:
        job1 = None
    outs = _layer1_proj_prompt(x1, p['g_kv'], p['g_pre'][1:2], p['w_kv_b'], p['w_in_b'][0], cos_t, sin_t,
                               job1)
    if sample is not None:
        sample.after_attn(tuple(outs[15:17]) if job1 is not None else _dec_attn(*sample.attn_job))
    qs, ks, vs = outs[0:3], outs[3:6], outs[6:9]
    zg, qm, zm = outs[9:12]
    wins = outs[12:15]
    os_, ls_ = [], []
    for g in range(N_GROUPS):
        o, l = _band_attention(qs[g], ks[g], vs[g])
        os_.append(o)
        ls_.append(l)
    y = _layer1_out_prompt(x1, os_, ls_, zg, qm, zm, memkv_b[1], p['w_out_b'][0], p['g_post'][1:2])
    m_p = m_pad[:, 0:A_HEADS, 0][None]
    wins = [w.reshape(b, w.shape[1], 2, B_HEADS, B_HDIM) for w in wins]
    memkv_p = memkv_f.reshape(depth, b, N_MEM, 2, M_HEADS, M_HDIM)
    return y, conv_p, c_p, n_p, m_p, wins, memkv_p


def _prep_params(g_pre, g_post, w_in_a, conv_w_a, conv_b_a, w_q_a, w_k_a, w_v_a, w_if_a, b_if_a,
                 g_hn_a, skip_a, w_out_a, g_kv, w_kv_b, w_in_b, w_out_b, w_mkv):
    wif = jnp.pad(w_if_a[0], ((0, 0), (0, LANES - 2 * A_HEADS))).astype(BF16)
    bif = jnp.pad(b_if_a[0], (0, LANES - 2 * A_HEADS))[None, :]
    return {
        'g_pre': g_pre, 'g_post': g_post,
        'w_in_a': w_in_a.astype(BF16), 'conv_w_a': conv_w_a, 'conv_b_a': conv_b_a,
        'w_q_a': w_q_a.astype(BF16), 'w_k_a': w_k_a.astype(BF16), 'w_v_a': w_v_a.astype(BF16),
        'w_if_a': wif, 'b_if_a': bif, 'g_hn_a': g_hn_a, 'skip_a': skip_a,
        'w_out_a': w_out_a.astype(BF16), 'g_kv': g_kv[None, :], 'w_kv_b': w_kv_b.astype(BF16),
        'w_in_b': w_in_b.astype(BF16), 'w_out_b': w_out_b.astype(BF16), 'w_mkv': w_mkv.astype(BF16),
    }


def _dec_l0_proj_kernel(x_ref, gpre_ref, win_ref, cst_ref, convw_ref, convb_ref, wq_ref, wk_ref, wv_ref,
                        wif_ref, bif_ref,
                        q_ref, k_ref, v_ref, gates_ref, xc_ref, opre_ref, zg_ref, qm_ref, zm_ref, cnew_ref):
    h = (_rms_scale(x_ref[:, 0, :]) * gpre_ref[...]).astype(BF16)
    u = _dot(h, win_ref[:, 0:A_INNER])
    opre_ref[...] = _dot(h, win_ref[:, A_INNER:2 * A_INNER])
    zg_ref[...] = _dot(h, win_ref[:, 2 * A_INNER:3 * A_INNER])
    qm_ref[...] = _rows_to_heads(_dot(h, win_ref[:, 3 * A_INNER:3 * A_INNER + M_WIDTH]))
    zm_ref[...] = _dot(h, win_ref[:, 3 * A_INNER + M_WIDTH:3 * A_INNER + 2 * M_WIDTH])
    cw = convw_ref[...]
    xc = convb_ref[...] + cst_ref[0, :, 0, :] * cw[0:1, :]
    xc = xc + cst_ref[0, :, 1, :] * cw[1:2, :]
    xc = xc + cst_ref[0, :, 2, :] * cw[2:3, :]
    xc = xc + u * cw[3:4, :]
    xc = _silu(xc)
    xc_ref[...] = xc
    cnew_ref[0, :, 0, :] = cst_ref[0, :, 1, :]
    cnew_ref[0, :, 1, :] = cst_ref[0, :, 2, :]
    cnew_ref[0, :, 2, :] = u
    qs, ks, vs, cat = [], [], [], []
    for hd in range(A_HEADS):
        sl = slice(hd * A_HDIM, (hd + 1) * A_HDIM)
        xh = xc[:, sl].astype(BF16)
        qh = _dot(xh, wq_ref[hd])
        kh = _dot(xh, wk_ref[hd]) * (A_HDIM ** -0.5)
        vh = _dot(u[:, sl].astype(BF16), wv_ref[hd])
        qs.append(qh)
        ks.append(kh)
        vs.append(vh)
        cat += [qh.astype(BF16), kh.astype(BF16), vh.astype(BF16)]
    q_ref[...] = jnp.concatenate(qs, axis=-1)
    k_ref[...] = jnp.concatenate(ks, axis=-1)
    v_ref[...] = jnp.concatenate(vs, axis=-1)
    gates_ref[...] = _dot(jnp.concatenate(cat, axis=-1), wif_ref[...]) + bif_ref[...]


def _whole(shape):
    nd = len(shape)
    return pl.BlockSpec(shape, lambda *_: (0,) * nd)


def _dec_l0_proj(x, g_pre, w_in, cst, conv_w, conv_b, wq, wk, wv, wif, bif):
    nb = x.shape[0]
    args = (x, g_pre, w_in, cst, conv_w, conv_b, wq, wk, wv, wif, bif)
    f = lambda *s: jax.ShapeDtypeStruct(s, F32)
    out_shape = [f(nb, A_INNER), f(nb, A_INNER), f(nb, A_INNER), f(nb, LANES), f(nb, A_INNER), f(nb, A_INNER),
                 f(nb, A_INNER), f(nb, M_HEADS, M_HDIM), f(nb, M_WIDTH), f(1, nb, CONV_W - 1, A_INNER)]
    return pl.pallas_call(
        _dec_l0_proj_kernel,
        grid=(1,),
        in_specs=[_whole(a.shape) for a in args],
        out_specs=[_whole(o.shape) for o in out_shape],
        out_shape=out_shape,
        compiler_params=pltpu.CompilerParams(dimension_semantics=("arbitrary",), vmem_limit_bytes=VMEM_LIMIT),
        name="dec_l0_proj",
    )(*args)


def _row_to_col(row, eye):
    return jnp.sum(jnp.where(eye, row, 0.0), axis=-1, keepdims=True)


def _col_to_row(colv, eye):
    return jnp.sum(jnp.where(eye, colv, 0.0), axis=0, keepdims=True)


def _dec_mem_attention(q, kv_ref_view):
    kk = kv_ref_view[:, 0]
    vv = kv_ref_view[:, 1]
    s = jnp.sum(kk * (q * (M_HDIM ** -0.5))[None], axis=-1, keepdims=True)
    mx = jnp.max(s, axis=0, keepdims=True)
    p = jnp.exp(s - mx)
    return jnp.sum(p * vv, axis=0) / jnp.sum(p, axis=0)


def _dec_mlstm_body(b, heads, with_mem, q_ref, k_ref, v_ref, gates_ref, m_ref, c_ref, n_ref, qm_ref, kv_ref,
                    hs_ref, c_out, n_out, m_out, ym_ref):
    rb = pl.ds(b, 1)
    g = gates_ref[rb, :]
    mrow = m_ref[0, rb, :]
    r = lax.broadcasted_iota(jnp.int32, (A_HDIM, A_HDIM), 0)
    c = lax.broadcasted_iota(jnp.int32, (A_HDIM, A_HDIM), 1)
    eye = r == c
    sl = {h: slice(h * A_HDIM, (h + 1) * A_HDIM) for h in heads}
    qh = {h: q_ref[rb, sl[h]] for h in heads}
    kh = {h: k_ref[rb, sl[h]] for h in heads}
    vh = {h: v_ref[rb, sl[h]] for h in heads}
    c_old = {h: c_ref[0, 0, h] for h in heads}
    n_old = {h: n_ref[0, 0, h:h + 1, :] for h in heads}
    li = {h: g[:, h:h + 1] for h in heads}
    lf = {h: _log_sigmoid(g[:, 4 + h:5 + h]) for h in heads}
    m_old = {h: mrow[:, h:h + 1] for h in heads}
    cq = {h: jnp.sum(c_old[h] * qh[h], axis=-1, keepdims=True) for h in heads}
    v_col = {h: _row_to_col(vh[h], eye) for h in heads}
    nq = {h: jnp.sum(n_old[h] * qh[h], axis=-1, keepdims=True) for h in heads}
    qk = {h: jnp.sum(qh[h] * kh[h], axis=-1, keepdims=True) for h in heads}
    inter = {h: lf[h] + m_old[h] for h in heads}
    m_new = {h: jnp.maximum(inter[h], li[h]) for h in heads}
    ws = {h: jnp.exp(li[h] - m_new[h]) for h in heads}
    dec = {h: jnp.exp(inter[h] - m_new[h]) for h in heads}
    sc = {h: qk[h] * ws[h] for h in heads}
    den = {h: sc[h] + dec[h] * nq[h] for h in heads}
    h_col = {h: (sc[h] * v_col[h] + dec[h] * cq[h]) / jnp.maximum(jnp.abs(den[h]), jnp.exp(-m_new[h]))
             for h in heads}
    for h in heads:
        c_out[0, 0, h] = dec[h] * c_old[h] + (ws[h] * v_col[h]) * kh[h]
        n_out[0, 0, h:h + 1, :] = dec[h] * n_old[h] + ws[h] * kh[h]
        m_out[0, h:h + 1, :] = jnp.broadcast_to(m_new[h], (1, LANES))
    for h in heads:
        hs_ref[0, :, sl[h]] = _col_to_row(h_col[h], eye)
    if with_mem:
        ym_ref[0] = _dec_mem_attention(qm_ref[0], kv_ref.at[0, 0])


def _dec_mlstm_kernel(*refs):
    _dec_mlstm_body(pl.program_id(0), range(A_HEADS), True, *refs)


def _dec_mlstm_specs(q, k, v, gates, m_in, state_c, state_n, qm3, cache_mem_kv, seq_index):
    nb = q.shape[0]

    def at(*tail, lead=()):
        return lambda *idx: lead + (seq_index(*idx),) + tail

    in_specs = [_whole(q.shape), _whole(k.shape), _whole(v.shape), _whole(gates.shape), _whole(m_in.shape),
                pl.BlockSpec((1, 1, A_HEADS, A_HDIM, A_HDIM), at(0, 0, 0, lead=(0,))),
                pl.BlockSpec((1, 1, A_HEADS, A_HDIM), at(0, 0, lead=(0,))),
                pl.BlockSpec((1, M_HEADS, M_HDIM), at(0, 0)),
                pl.BlockSpec((1, 1, N_MEM, 2, M_HEADS, M_HDIM), at(0, 0, 0, 0, lead=(0,)))]
    out_specs = [pl.BlockSpec((1, 1, A_INNER), at(0, 0)),
                 pl.BlockSpec((1, 1, A_HEADS, A_HDIM, A_HDIM), at(0, 0, 0, lead=(0,))),
                 pl.BlockSpec((1, 1, A_HEADS, A_HDIM), at(0, 0, lead=(0,))),
                 pl.BlockSpec((1, A_HEADS, LANES), at(0, 0)),
                 pl.BlockSpec((1, M_HEADS, M_HDIM), at(0, 0))]
    out_shapes = [jax.ShapeDtypeStruct((nb, 1, A_INNER), F32),
                  jax.ShapeDtypeStruct(state_c.shape, F32),
                  jax.ShapeDtypeStruct(state_n.shape, F32),
                  jax.ShapeDtypeStruct((nb, A_HEADS, LANES), F32),
                  jax.ShapeDtypeStruct((nb, M_HEADS, M_HDIM), F32)]
    return [q, k, v, gates, m_in, state_c, state_n, qm3, cache_mem_kv], in_specs, out_specs, out_shapes


def _dec_mlstm(*job):
    args, in_specs, out_specs, out_shapes = _dec_mlstm_specs(*job, seq_index=lambda b: b)
    return pl.pallas_call(
        _dec_mlstm_kernel,
        grid=(args[0].shape[0],),
        in_specs=in_specs,
        out_specs=out_specs,
        out_shape=out_shapes,
        compiler_params=pltpu.CompilerParams(dimension_semantics=("arbitrary",), vmem_limit_bytes=VMEM_LIMIT),
        name="dec_mlstm",
    )(*args)


def _dec_mid_kernel(hs_ref, opre_ref, xc_ref, zg_ref, ym_ref, zm_ref, x_ref, ghn_ref, skip_ref, wout_ref,
                    gpost_ref, gkv_ref, gpre_ref, wkv_ref, win_ref, cos_ref, sin_ref,
                    x1_ref, q_ref, k_ref, v_ref, zg1_ref, qm1_ref, zm1_ref, win0_ref, win1_ref, win2_ref):
    hh = _sigmoid(opre_ref[...]) * hs_ref[:, 0, :]
    parts = []
    for h in range(A_HEADS):
        v = hh[:, h * A_HDIM:(h + 1) * A_HDIM]
        mu = jnp.mean(v, axis=-1, keepdims=True)
        var = jnp.mean(jnp.square(v - mu), axis=-1, keepdims=True)
        parts.append((v - mu) * lax.rsqrt(var + EPS))
    y = jnp.concatenate(parts, axis=-1) * ghn_ref[...] + skip_ref[...] * xc_ref[...]
    ymix = (y * _silu(zg_ref[...])).astype(BF16)
    ym = (_heads_to_rows(ym_ref) * _silu(zm_ref[...])).astype(BF16)
    out = _dot(ymix, wout_ref[0:A_INNER, :]) + _dot(ym, wout_ref[A_INNER:A_INNER + M_WIDTH, :])
    x1 = x_ref[:, 0, :] + _rms_scale(out) * gpost_ref[...]
    x1_ref[...] = x1
    xn = _rms_scale(x1)
    hk = (xn * gkv_ref[...]).astype(BF16)
    hq = (xn * gpre_ref[...]).astype(BF16)
    cos = cos_ref[...]
    sin = sin_ref[...]
    ks, vs = [], []
    for g in range(N_GROUPS):
        ks.append(_rope_cols(_dot(hk, wkv_ref[:, g * 2 * B_WIDTH:g * 2 * B_WIDTH + B_WIDTH]), cos, sin))
        vs.append(_dot(hk, wkv_ref[:, g * 2 * B_WIDTH + B_WIDTH:(g + 1) * 2 * B_WIDTH]))
    k_ref[...] = _rows_to_heads(jnp.concatenate(ks, axis=-1))
    v_ref[...] = _rows_to_heads(jnp.concatenate(vs, axis=-1))
    for g, wref in enumerate((win0_ref, win1_ref, win2_ref)):
        wref[...] = _rows_to_kv_heads(ks[g], vs[g])
    qoff = N_GROUPS * B_WIDTH
    q_ref[...] = _rows_to_heads(_rope_cols(_dot(hq, win_ref[:, 0:qoff]), cos, sin))
    zg1_ref[...] = _dot(hq, win_ref[:, qoff:qoff + B_WIDTH])
    qm1_ref[...] = _rows_to_heads(_dot(hq, win_ref[:, qoff + B_WIDTH:qoff + B_WIDTH + M_WIDTH]))
    zm1_ref[...] = _dot(hq, win_ref[:, qoff + B_WIDTH + M_WIDTH:qoff + B_WIDTH + 2 * M_WIDTH])


def _dec_mid(hs, opre, xc, zg, ym, zm, x, ghn, skip, w_out, g_post, g_kv, g_pre, wkv, win, cos, sin):
    nb = x.shape[0]
    args = (hs, opre, xc, zg, ym, zm, x, ghn, skip, w_out, g_post, g_kv, g_pre, wkv, win, cos, sin)
    f = lambda *s: jax.ShapeDtypeStruct(s, F32)
    gh = N_GROUPS * B_HEADS
    out_shape = [f(nb, D_MODEL), f(nb, gh, B_HDIM), f(nb, gh, B_HDIM), f(nb, gh, B_HDIM),
                 f(nb, B_WIDTH), f(nb, M_HEADS, M_HDIM), f(nb, M_WIDTH)] + [f(nb, 2 * B_HEADS, B_HDIM)] * N_GROUPS
    return pl.pallas_call(
        _dec_mid_kernel,
        grid=(1,),
        in_specs=[_whole(a.shape) for a in args],
        out_specs=[_whole(o.shape) for o in out_shape],
        out_shape=out_shape,
        compiler_params=pltpu.CompilerParams(dimension_semantics=("arbitrary",), vmem_limit_bytes=VMEM_LIMIT),
        name="dec_mid",
    )(*args)


def _dec_attn_kernel(q_ref, kn_ref, vn_ref, w0_ref, w1_ref, w2_ref, qm_ref, kv_ref, ydil_ref, ym_ref):
    w_refs = (w0_ref, w1_ref, w2_ref)
    scale = B_HDIM ** -0.5
    groups = range(N_GROUPS)
    hsl = [slice(g * B_HEADS, (g + 1) * B_HEADS) for g in groups]
    q = [q_ref[0, hsl[g], :] * scale for g in groups]
    s_c = [jnp.sum(w_refs[g][0, :, 0] * q[g][None], axis=-1, keepdims=True) for g in groups]
    s_n = [jnp.sum(kn_ref[0, hsl[g], :] * q[g], axis=-1, keepdims=True) for g in groups]
    mxs = [jnp.maximum(jnp.max(s_c[g], axis=0), s_n[g]) for g in groups]
    p_c = [jnp.exp(s_c[g] - mxs[g][None]) for g in groups]
    p_n = [jnp.exp(s_n[g] - mxs[g]) for g in groups]
    ls = [jnp.sum(p_c[g], axis=0) + p_n[g] for g in groups]
    outs = [(jnp.sum(p_c[g] * w_refs[g][0, :, 1], axis=0) + p_n[g] * vn_ref[0, hsl[g], :]) / ls[g]
            for g in groups]
    lses = [mxs[g] + jnp.log(ls[g]) for g in groups]
    mx = jnp.maximum(jnp.maximum(lses[0], lses[1]), lses[2])
    es = [jnp.exp(l - mx) for l in lses]
    tot = es[0] + es[1] + es[2]
    ydil_ref[0] = (es[0] / tot) * outs[0] + (es[1] / tot) * outs[1] + (es[2] / tot) * outs[2]
    ym_ref[0] = _dec_mem_attention(qm_ref[0], kv_ref.at[0, 0])


def _dec_attn_specs(q4, kn4, vn4, cw0, cw1, cw2, qm3, cache_mem_kv, layer, seq_index):
    nb = q4.shape[0]
    rows = B_GROUPS[0][0]

    def at(*tail):
        return lambda *idx: (seq_index(*idx),) + tail

    win_specs = [pl.BlockSpec((1, rows, 2, B_HEADS, B_HDIM), at(0, 0, 0, 0)),
                 pl.BlockSpec((1, rows, None, 2, B_HEADS, B_HDIM), at(0, 0, 0, 0, 0)),
                 pl.BlockSpec((1, rows, None, 2, B_HEADS, B_HDIM), at(0, 0, 0, 0, 0))]
    in_specs = [pl.BlockSpec((1, N_GROUPS * B_HEADS, B_HDIM), at(0, 0))] * 3 + win_specs + [
        pl.BlockSpec((1, M_HEADS, M_HDIM), at(0, 0)),
        pl.BlockSpec((1, 1, N_MEM, 2, M_HEADS, M_HDIM), lambda *idx: (layer, seq_index(*idx), 0, 0, 0, 0))]
    out_specs = [pl.BlockSpec((1, B_HEADS, B_HDIM), at(0, 0)), pl.BlockSpec((1, M_HEADS, M_HDIM), at(0, 0))]
    out_shapes = [jax.ShapeDtypeStruct((nb, B_HEADS, B_HDIM), F32),
                  jax.ShapeDtypeStruct((nb, M_HEADS, M_HDIM), F32)]
    return [q4, kn4, vn4, cw0, cw1, cw2, qm3, cache_mem_kv], in_specs, out_specs, out_shapes


def _dec_attn(q4, kn4, vn4, cw0, cw1, cw2, qm3, cache_mem_kv, layer):
    args, in_specs, out_specs, out_shapes = _dec_attn_specs(
        q4, kn4, vn4, cw0, cw1, cw2, qm3, cache_mem_kv, layer, seq_index=lambda b: b)
    return pl.pallas_call(
        _dec_attn_kernel,
        grid=(q4.shape[0],),
        in_specs=in_specs,
        out_specs=out_specs,
        out_shape=out_shapes,
        compiler_params=pltpu.CompilerParams(dimension_semantics=("arbitrary",), vmem_limit_bytes=VMEM_LIMIT),
        name="dec_attn",
    )(*args)


def _dec_out_kernel(ydil_ref, zg_ref, ym_ref, zm_ref, x_ref, wout_ref, gpost_ref, y_ref):
    ymix = (_heads_to_rows(ydil_ref) * _silu(zg_ref[...])).astype(BF16)
    ym = (_heads_to_rows(ym_ref) * _silu(zm_ref[...])).astype(BF16)
    out = _dot(ymix, wout_ref[0:B_WIDTH, :]) + _dot(ym, wout_ref[B_WIDTH:B_WIDTH + M_WIDTH, :])
    y_ref[:, 0, :] = x_ref[...] + _rms_scale(out) * gpost_ref[...]


def _dec_out(ydil, zg, ym, zm, x1, w_out, g_post):
    args = (ydil, zg, ym, zm, x1, w_out, g_post)
    out_shape = (x1.shape[0], 1, x1.shape[1])
    return pl.pallas_call(
        _dec_out_kernel,
        grid=(1,),
        in_specs=[_whole(a.shape) for a in args],
        out_specs=_whole(out_shape),
        out_shape=jax.ShapeDtypeStruct(out_shape, F32),
        compiler_params=pltpu.CompilerParams(dimension_semantics=("arbitrary",), vmem_limit_bytes=VMEM_LIMIT),
        name="dec_out",
    )(*args)


class _SampleGroup:
    def __init__(self, x_sample, state_conv, state_c, state_n, state_m, cache_wins, cache_mem_kv, p):
        self.p = p
        self.nb = nb = x_sample.shape[0]
        self.cache_wins = cache_wins
        self.cache_mem_kv = cache_mem_kv
        self.x = x_sample
        q, k, v, gates, self.xc, self.opre, self.zg, qm3, self.zm, self.conv_s = _dec_l0_proj(
            x_sample, p['g_pre'][0:1], p['w_in_a'][0], state_conv, p['conv_w_a'][0], p['conv_b_a'],
            p['w_q_a'][0], p['w_k_a'][0], p['w_v_a'][0], p['w_if_a'], p['b_if_a'])
        self._mlstm_job = (q, k, v, gates, state_m, state_c, state_n, qm3, cache_mem_kv)
        self.attn_job = None

    def mlstm_job(self):
        return self._mlstm_job

    def after_mlstm(self, res):
        p, nb = self.p, self.nb
        hs, self.c_s, self.n_s, m_rows, ym0 = res
        self.m_s = m_rows[:, :, 0][None]
        cos, sin = _rope_tables(PAST_LEN + jnp.arange(1, dtype=F32))
        self.x1, qd, kn, vn, self.zg1, qm1, self.zm1, w0, w1, w2 = _dec_mid(
            hs, self.opre, self.xc, self.zg, ym0, self.zm, self.x,
            p['g_hn_a'], p['skip_a'], p['w_out_a'][0], p['g_post'][0:1], p['g_kv'], p['g_pre'][1:2],
            p['w_kv_b'], p['w_in_b'][0], cos, sin)
        cws = [self.cache_wins[0]]
        for g in (1, 2):
            w, d = B_GROUPS[g]
            cws.append(self.cache_wins[g].reshape(nb, w // d, d, 2, B_HEADS, B_HDIM))
        self.attn_job = (qd, kn, vn, cws[0], cws[1], cws[2], qm1, self.cache_mem_kv, 1)
        self.wins_s = [w.reshape(nb, 1, 2, B_HEADS, B_HDIM) for w in (w0, w1, w2)]

    def after_attn(self, res):
        p = self.p
        ydil, ym1 = res
        self.y = _dec_out(ydil, self.zg1, ym1, self.zm1, self.x1, p['w_out_b'][0], p['g_post'][1:2])

    def outputs(self):
        return self.y, self.conv_s, self.c_s, self.n_s, self.m_s, self.wins_s


def kernel(x_prompt, x_sample, mem_prompt, state_conv, state_C, state_n, state_m, cache_win0, cache_win1,
           cache_win2, cache_mem_kv, g_pre, g_post, w_in_a, conv_w_a, conv_b_a, w_q_a, w_k_a, w_v_a, w_if_a,
           b_if_a, g_hn_a, skip_a, w_out_a, g_kv, w_kv_b, w_in_b, w_out_b, w_mkv):
    p = _prep_params(g_pre, g_post, w_in_a, conv_w_a, conv_b_a, w_q_a, w_k_a, w_v_a, w_if_a, b_if_a,
                     g_hn_a, skip_a, w_out_a, g_kv, w_kv_b, w_in_b, w_out_b, w_mkv)
    sample = _SampleGroup(x_sample, state_conv, state_C, state_n, state_m,
                          (cache_win0, cache_win1, cache_win2), cache_mem_kv, p)
    y_p, conv_p, c_p, n_p, m_p, wins_p, memkv_p = _prompt_group(x_prompt, mem_prompt, p, sample)
    y_s, conv_s, c_s, n_s, m_s, wins_s = sample.outputs()
    return (y_p, y_s, conv_p, c_p, n_p, m_p, wins_p[0], wins_p[1], wins_p[2], memkv_p,
            conv_s, c_s, n_s, m_s, wins_s[0], wins_s[1], wins_s[2])
```

```python
import functools

import jax
import jax.numpy as jnp
from jax import lax
from jax.experimental import pallas as pl
from jax.experimental.pallas import tpu as pltpu

F32 = jnp.float32
BF16 = jnp.bfloat16

D_MODEL = 1024
A_HEADS = 4
A_HDIM = 256
A_INNER = 1024
CONV_W = 4
A_CHUNK = 128
B_GROUPS = ((128, 1), (512, 4), (2048, 16))
N_GROUPS = 3
B_HEADS = 4
B_HDIM = 128
B_WIDTH = 512
N_MEM = 256
M_HEADS = 4
M_HDIM = 128
M_WIDTH = 512
ROPE_THETA = 10000.0
EPS = 1e-6
PAST_LEN = 8192

LANES = 128
TOK_TILE = 512
L0_TILE = 256
L1OUT_TILE = 1024
ATT_BLK = 128
ATT_TILE = 2048
ATT_GROUP = 2
VMEM_LIMIT = 56 * 1024 * 1024

NT_DIMS = (((1,), (1,)), ((), ()))
LOG2E = 1.4426950408889634


def _dot(a, b):
    return jnp.dot(a, b, preferred_element_type=F32)


def _dot_nt(a, b):
    return lax.dot_general(a, b, NT_DIMS, preferred_element_type=F32)


def _sigmoid(x):
    return 1.0 / (1.0 + jnp.exp(-x))


def _silu(x):
    return x * _sigmoid(x)


def _log_sigmoid(x):
    return jnp.minimum(x, 0.0) - jnp.log(1.0 + jnp.exp(-jnp.abs(x)))


def _rms_scale(x):
    return x * lax.rsqrt(jnp.mean(x * x, axis=-1, keepdims=True) + EPS)


def _const_spec(shape):
    nd = len(shape)
    return pl.BlockSpec(shape, lambda *_: (0,) * nd, pipeline_mode=pl.Buffered(1))


def _mem_attention(qm, mk, mv):
    heads = range(M_HEADS)
    sl = [slice(h * M_HDIM, (h + 1) * M_HDIM) for h in heads]
    s = [_dot_nt(qm[:, sl[h]], mk[:, sl[h]]) for h in heads]
    mx = [jnp.max(s[h], axis=-1, keepdims=True) for h in heads]
    p = [jnp.exp2((s[h] - mx[h]) * (M_HDIM ** -0.5 * LOG2E)) for h in heads]
    l = [jnp.sum(p[h], axis=-1, keepdims=True) for h in heads]
    outs = [_dot(p[h].astype(BF16), mv[:, sl[h]]) / l[h] for h in heads]
    return jnp.concatenate(outs, axis=-1)


def _rows_to_heads(x):
    return jnp.swapaxes(jnp.stack([x[:, h * LANES:(h + 1) * LANES] for h in range(x.shape[1] // LANES)]), 0, 1)


def _heads_to_rows(ref):
    return jnp.concatenate([ref[:, h, :] for h in range(ref.shape[1])], axis=-1)


def _rows_to_kv_heads(k, v):
    pieces = [a[:, h * LANES:(h + 1) * LANES] for a in (k, v) for h in range(a.shape[1] // LANES)]
    return jnp.swapaxes(jnp.stack(pieces), 0, 1)


def _memkv_kernel(m_ref, w_ref, o_ref, ob_ref):
    r = _dot(m_ref[...].astype(BF16), w_ref[0])
    o_ref[0] = _rows_to_kv_heads(r[:, 0:M_WIDTH], r[:, M_WIDTH:2 * M_WIDTH])
    ob_ref[0] = r.astype(BF16)


def _memkv(mem2d, w_bf):
    nm = mem2d.shape[0]
    nl = w_bf.shape[0]
    tm = min(512, nm)
    return pl.pallas_call(
        _memkv_kernel,
        grid=(nl, nm // tm),
        in_specs=[pl.BlockSpec((tm, D_MODEL), lambda l, i: (i, 0)),
                  pl.BlockSpec((1, D_MODEL, 2 * M_WIDTH), lambda l, i: (l, 0, 0))],
        out_specs=[pl.BlockSpec((1, tm, 2 * M_HEADS, M_HDIM), lambda l, i: (l, i, 0, 0)),
                   pl.BlockSpec((1, tm, 2 * M_WIDTH), lambda l, i: (l, i, 0))],
        out_shape=[jax.ShapeDtypeStruct((nl, nm, 2 * M_HEADS, M_HDIM), F32),
                   jax.ShapeDtypeStruct((nl, nm, 2 * M_WIDTH), BF16)],
        compiler_params=pltpu.CompilerParams(dimension_semantics=("arbitrary", "arbitrary")),
        name="memkv",
    )(mem2d, w_bf)


def _mlstm_chunk(rs, g, qkv_v, kt_v, c_s, n_s, m_s, causal, hs):
    ls = _log_sigmoid(g)
    tok = lax.broadcasted_iota(jnp.int32, (A_CHUNK, LANES), 0)
    bc = ls
    shift = 1
    while shift < A_CHUNK:
        bc = bc + jnp.where(tok >= shift, pltpu.roll(bc, shift, 0), 0.0)
        shift *= 2
    lane = lax.broadcasted_iota(jnp.int32, (A_CHUNK, LANES), 1)
    xt = jnp.where(lane < A_HEADS, g, bc).T
    yield
    heads = range(A_HEADS)
    b_col = [bc[:, 4 + h:5 + h] for h in heads]
    b_row = [xt[4 + h:5 + h, :] for h in heads]
    li_row = [xt[h:h + 1, :] for h in heads]
    li_col = [g[:, h:h + 1] for h in heads]
    m_old = [m_s[h:h + 1, 0:1] for h in heads]
    b_last = [bc[A_CHUNK - 1:A_CHUNK, 4 + h:5 + h] for h in heads]
    qh = [qkv_v[rs, h * 3 * A_HDIM:h * 3 * A_HDIM + A_HDIM] for h in heads]
    kh = [qkv_v[rs, h * 3 * A_HDIM + A_HDIM:h * 3 * A_HDIM + 2 * A_HDIM] for h in heads]
    vh = [qkv_v[rs, h * 3 * A_HDIM + 2 * A_HDIM:(h + 1) * 3 * A_HDIM] for h in heads]
    kt = [kt_v[h] for h in heads]
    c_old = [c_s[h] for h in heads]
    n_old = [n_s[h:h + 1, :] for h in heads]
    qk = [_dot_nt(qh[h], kh[h]) for h in heads]
    qc = [_dot(qh[h], c_old[h].astype(BF16)) for h in heads]
    dm = [jnp.where(causal, b_col[h] - b_row[h] + li_row[h], -jnp.inf) for h in heads]
    inter = [b_col[h] + m_old[h] for h in heads]
    m_row = [jnp.maximum(inter[h], jnp.max(dm[h], axis=-1, keepdims=True)) for h in heads]
    g_max = [jnp.max(b_last[h] - b_row[h] + li_row[h], axis=-1, keepdims=True) for h in heads]
    m_new = [jnp.maximum(b_last[h] + m_old[h], g_max[h]) for h in heads]
    yield
    sc = [qk[h] * jnp.exp(dm[h] - m_row[h]) for h in heads]
    dec = [jnp.exp(inter[h] - m_row[h]) for h in heads]
    ws_col = [jnp.exp(b_last[h] - b_col[h] + li_col[h] - m_new[h]) for h in heads]
    dc = [jnp.exp(b_last[h] + m_old[h] - m_new[h]) for h in heads]
    yield
    sv = [_dot(sc[h].astype(BF16), vh[h]) for h in heads]
    wv =[(ws_col[h] * vh[h].astype(F32)).astype(BF16) for h in heads]
    upd = [_dot(kt[h], wv[h]) for h in heads]
    yield
    for h in heads:
        den = (jnp.sum(sc[h], axis=-1, keepdims=True)
               + dec[h] * jnp.sum(qh[h].astype(F32) * n_old[h], axis=-1, keepdims=True))
        num = sv[h] + dec[h] * qc[h]
        hs.append(num / jnp.maximum(jnp.abs(den), jnp.exp(-m_row[h])))
    yield
    for h in heads:
        c_s[h] = dc[h] * c_old[h] + upd[h]
        n_s[h:h + 1, :] = dc[h] * n_old[h] + jnp.sum(ws_col[h] * kh[h].astype(F32), axis=0, keepdims=True)
        m_s[h:h + 1, :] = jnp.broadcast_to(m_new[h], (1, LANES))
    yield


def _l0p_kernel(nt, n_dec_seq, *refs):
    k_in, k_out = 16, 5
    k_dec_in, k_dec_out = (9, 5) if n_dec_seq else (0, 0)
    (x_ref, xp_ref, gpre_ref, win_ref, convw_ref, convb_ref, wq_ref, wk_ref, wv_ref,
     wif_ref, bif_ref, ghn_ref, skip_ref, mkv_ref, wout_ref, gpost_ref) = refs[0:k_in]
    dec_in = refs[k_in:k_in + k_dec_in]
    o0 = k_in + k_dec_in
    x1_ref, conv_out, c_out, n_out, m_out = refs[o0:o0 + k_out]
    dec_out = refs[o0 + k_out:o0 + k_out + k_dec_out]
    (h_s, u_s, ymix_s, xc_s, opre_s, zg_s, qm_s, zm_s, qkv_s, kt_s, gates_s,
     c_s, n_s, m_s) = refs[o0 + k_out + k_dec_out:]
    tt = x_ref.shape[1]
    nsub = tt // A_CHUNK
    t = pl.program_id(0)
    parity = lax.rem(t + 1, 2)
    pos1 = lax.rem(t + nt - 1, nt)
    pos2 = lax.rem(t + 2 * nt - 2, nt)

    @pl.when(t == 0)
    def _():
        h_s[...] = jnp.zeros(h_s.shape, BF16)
        u_s[...] = jnp.zeros(u_s.shape, F32)
        xc_s[0] = jnp.zeros(xc_s.shape[1:], F32)
        opre_s[0] = jnp.zeros(opre_s.shape[1:], F32)
        zg_s[0] = jnp.zeros(zg_s.shape[1:], F32)
        qm_s[0] = jnp.zeros(qm_s.shape[1:], BF16)
        zm_s[0] = jnp.zeros(zm_s.shape[1:], F32)
        qkv_s[0] = jnp.zeros(qkv_s.shape[1:], BF16)
        kt_s[0] = jnp.zeros(kt_s.shape[1:], BF16)
        gates_s[0] = jnp.zeros(gates_s.shape[1:], F32)

    @pl.when(pos1 == 0)
    def _():
        u_s[0:8, :] = jnp.zeros((8, A_INNER), F32)

    @pl.when(pos2 == 0)
    def _():
        c_s[...] = jnp.zeros(c_s.shape, F32)
        n_s[...] = jnp.zeros(n_s.shape, F32)
        m_s[...] = jnp.zeros(m_s.shape, F32)

    row = lax.broadcasted_iota(jnp.int32, (A_CHUNK, A_CHUNK), 0)
    col = lax.broadcasted_iota(jnp.int32, (A_CHUNK, A_CHUNK), 1)
    causal = col <= row
    ghn = ghn_ref[...]
    skp = skip_ref[...]
    mk = mkv_ref[0, :, 0:M_WIDTH]
    mv = mkv_ref[0, :, M_WIDTH:2 * M_WIDTH]

    def stage2(pslot):
        ym = _mem_attention(qm_s[pslot], mk, mv) * _silu(zm_s[pslot])
        ymix_s[:, A_INNER:A_INNER + M_WIDTH] = ym.astype(BF16)
        yield
        for c in range(nsub):
            rs = slice(c * A_CHUNK, (c + 1) * A_CHUNK)
            hs = []
            yield from _mlstm_chunk(rs, gates_s[pslot, rs, :], qkv_s.at[pslot], kt_s.at[pslot, :, c],
                                    c_s, n_s, m_s, causal, hs)
            parts = []
            for h in range(A_HEADS):
                v = _sigmoid(opre_s[pslot, rs, h * A_HDIM:(h + 1) * A_HDIM]) * hs[h]
                mu = jnp.mean(v, axis=-1, keepdims=True)
                var = jnp.mean(jnp.square(v - mu), axis=-1, keepdims=True)
                parts.append((v - mu) * lax.rsqrt(var + EPS))
            hn = jnp.concatenate(parts, axis=-1) * ghn
            y = hn + skp * xc_s[pslot, rs, :]
            ymix_s[rs, 0:A_INNER] = (y * _silu(zg_s[pslot, rs, :])).astype(BF16)
            yield

    def stage1(slot):
        hb = h_s[...]
        u_s[8:8 + tt, :] = _dot(hb, win_ref[:, 0:A_INNER])
        yield
        cw = convw_ref[...]
        cb = convb_ref[...]
        for c in range(nsub):
            r0 = c * A_CHUNK
            blk = u_s[r0:r0 + A_CHUNK + 8, :]
            xc = cb + pltpu.roll(blk, 3, 0)[8:, :] * cw[0:1, :]
            xc = xc + pltpu.roll(blk, 2, 0)[8:, :] * cw[1:2, :]
            xc = xc + pltpu.roll(blk, 1, 0)[8:, :] * cw[2:3, :]
            xc = xc + blk[8:, :] * cw[3:4, :]
            xc_s[slot, r0:r0 + A_CHUNK, :] = _silu(xc)
        opre_s[slot] = _dot(hb, win_ref[:, A_INNER:2 * A_INNER])
        yield
        zg_s[slot] = _dot(hb, win_ref[:, 2 * A_INNER:3 * A_INNER])
        yield
        qm_s[slot] = _dot(hb, win_ref[:, 3 * A_INNER:3 * A_INNER + M_WIDTH]).astype(BF16)
        zm_s[slot] = _dot(hb, win_ref[:, 3 * A_INNER + M_WIDTH:3 * A_INNER + 2 * M_WIDTH])
        yield
        for h in range(A_HEADS):
            sl = slice(h * A_HDIM, (h + 1) * A_HDIM)
            xh = xc_s[slot, :, sl].astype(BF16)
            uh = u_s[8:8 + tt, sl].astype(BF16)
            base = h * 3 * A_HDIM
            qkv_s[slot, :, base:base + A_HDIM] = _dot(xh, wq_ref[h]).astype(BF16)
            kf = _dot(xh, wk_ref[h]) * (A_HDIM ** -0.5)
            qkv_s[slot, :, base + A_HDIM:base + 2 * A_HDIM] = kf.astype(BF16)
            qkv_s[slot, :, base + 2 * A_HDIM:base + 3 * A_HDIM] = _dot(uh, wv_ref[h]).astype(BF16)
            kt = kf.T.astype(BF16)
            for c in range(nsub):
                kt_s[slot, h, c] = kt[:, c * A_CHUNK:(c + 1) * A_CHUNK]
            yield
        gates_s[slot] = _dot(qkv_s[slot], wif_ref[...]) + bif_ref[...]
        yield

    def step(slot):
        if n_dec_seq:
            first = slot == 1
            heads = (0, 1) if first else (2, 3)
            _dec_mlstm_body(jnp.minimum(t // 2, n_dec_seq - 1), heads, first, *dec_in, *dec_out)
        pending = [stage2(1 - slot), stage1(slot)]
        while pending:
            for gen in list(pending):
                try:
                    next(gen)
                except StopIteration:
                    pending.remove(gen)
        h_next = (_rms_scale(x_ref[0]) * gpre_ref[...]).astype(BF16)
        out = _dot(ymix_s[...], wout_ref[...])
        h_s[...] = h_next
        x1_ref[0] = xp_ref[0] + _rms_scale(out) * gpost_ref[...]

    for s in range(2):
        pl.when(parity == s)(functools.partial(step, s))

    @pl.when(jnp.logical_and(pos1 == nt - 1, t > 0))
    def _():
        conv_out[0, 0] = u_s[tt + 5:tt + 8, :]

    u_s[0:8, :] = u_s[tt:tt + 8, :]

    @pl.when(jnp.logical_and(pos2 == nt - 1, t > 1))
    def _():
        for h in range(A_HEADS):
            c_out[0, 0, h] = c_s[h].T
        n_out[0, 0] = n_s[0:A_HEADS, :]
        m_out[0] = m_s[...]


def _layer0_prompt_pipelined(x, g_pre, w_in, conv_w, conv_b, wq, wk, wv, wif, bif, ghn, skip, mkv_bf,
                             w_out, g_post, dec_job=None):
    b, s, _ = x.shape
    tt = min(L0_TILE, s)
    nt = s // tt
    ntiles = b * nt
    a_in = w_in.shape[1]
    nsub = tt // A_CHUNK

    def cur(t):
        t1 = jnp.minimum(t, ntiles - 1)
        return (t1 // nt, t1 % nt, 0)

    def prev(t):
        t2 = jnp.maximum(t - 2, 0)
        return (t2 // nt, t2 % nt, 0)

    def prev_b(t):
        return jnp.maximum(t - 2, 0) // nt

    in_specs = [
        pl.BlockSpec((1, tt, D_MODEL), cur),
        pl.BlockSpec((1, tt, D_MODEL), prev),
        _const_spec((1, D_MODEL)),
        _const_spec((D_MODEL, a_in)),
        _const_spec((CONV_W, A_INNER)),
        _const_spec((1, A_INNER)),
        _const_spec((A_HEADS, A_HDIM, A_HDIM)),
        _const_spec((A_HEADS, A_HDIM, A_HDIM)),
        _const_spec((A_HEADS, A_HDIM, A_HDIM)),
        _const_spec((3 * A_INNER, LANES)),
        _const_spec((1, LANES)),
        _const_spec((1, A_INNER)),
        _const_spec((1, A_INNER)),
        pl.BlockSpec((1, N_MEM, 2 * M_WIDTH), lambda t: (prev_b(t), 0, 0)),
        _const_spec((A_INNER + M_WIDTH, D_MODEL)),
        _const_spec((1, D_MODEL)),
    ]
    out_specs = [
        pl.BlockSpec((1, tt, D_MODEL), prev),
        pl.BlockSpec((1, 1, CONV_W - 1, A_INNER), lambda t: (0, prev_b(t), 0, 0)),
        pl.BlockSpec((1, 1, A_HEADS, A_HDIM, A_HDIM), lambda t: (0, prev_b(t), 0, 0, 0)),
        pl.BlockSpec((1, 1, A_HEADS, A_HDIM), lambda t: (0, prev_b(t), 0, 0)),
        pl.BlockSpec((1, 8, LANES), lambda t: (prev_b(t), 0, 0)),
    ]
    out_shape = [
        jax.ShapeDtypeStruct((b, s, D_MODEL), F32),
        jax.ShapeDtypeStruct((1, b, CONV_W - 1, A_INNER), F32),
        jax.ShapeDtypeStruct((1, b, A_HEADS, A_HDIM, A_HDIM), F32),
        jax.ShapeDtypeStruct((1, b, A_HEADS, A_HDIM), F32),
        jax.ShapeDtypeStruct((b, 8, LANES), F32),
    ]
    scratch = [
        pltpu.VMEM((tt, D_MODEL), BF16),
        pltpu.VMEM((tt + 8, A_INNER), F32),
        pltpu.VMEM((tt, A_INNER + M_WIDTH), BF16),
        pltpu.VMEM((2, tt, A_INNER), F32),
        pltpu.VMEM((2, tt, A_INNER), F32),
        pltpu.VMEM((2, tt, A_INNER), F32),
        pltpu.VMEM((2, tt, M_WIDTH), BF16),
        pltpu.VMEM((2, tt, M_WIDTH), F32),
        pltpu.VMEM((2, tt, 3 * A_INNER), BF16),
        pltpu.VMEM((2, A_HEADS, nsub, A_HDIM, A_CHUNK), BF16),
        pltpu.VMEM((2, tt, LANES), F32),
        pltpu.VMEM((A_HEADS, A_HDIM, A_HDIM), F32),
        pltpu.VMEM((8, A_HDIM), F32),
        pltpu.VMEM((8, LANES), F32),
    ]
    dec_args, dec_specs, dec_out_specs, dec_out_shapes, n_dec_seq = [], [], [], [], 0
    if dec_job is not None:
        n_dec_seq = dec_job[0].shape[0]
        dec_args, dec_specs, dec_out_specs, dec_out_shapes = _dec_mlstm_specs(
            *dec_job, seq_index=lambda t: jnp.minimum(t // 2, n_dec_seq - 1))
    return pl.pallas_call(
        functools.partial(_l0p_kernel, nt, n_dec_seq),
        grid=(ntiles + 2,),
        in_specs=in_specs + dec_specs,
        out_specs=out_specs + dec_out_specs,
        out_shape=out_shape + dec_out_shapes,
        scratch_shapes=scratch,
        compiler_params=pltpu.CompilerParams(
            dimension_semantics=("arbitrary",), vmem_limit_bytes=VMEM_LIMIT),
        name="layer0_prompt",
    )(x, x, g_pre, w_in, conv_w, conv_b, wq, wk, wv, wif, bif, ghn, skip, mkv_bf, w_out, g_post,
      *dec_args)


def _rope_cols(x, cos, sin_signed):
    outs = []
    for cblk in range(x.shape[1] // B_HDIM):
        xb = x[:, cblk * B_HDIM:(cblk + 1) * B_HDIM]
        outs.append(xb * cos + pltpu.roll(xb, B_HDIM // 2, 1) * sin_signed)
    return jnp.concatenate(outs, axis=-1)


def _l1a_kernel(n_dec_in, *refs):
    n_in, n_out = 7, 15
    x_ref, gkv_ref, gpre_ref, wkv_ref, win_ref, cos_ref, sin_ref = refs[0:n_in]
    dec_in = refs[n_in:n_in + n_dec_in]
    (q0_ref, q1_ref, q2_ref, k0_ref, k1_ref, k2_ref, v0_ref, v1_ref, v2_ref,
     zg_ref, qm_ref, zm_ref, w0_ref, w1_ref, w2_ref) = refs[n_in + n_dec_in:n_in + n_dec_in + n_out]
    dec_out = refs[n_in + n_dec_in + n_out:]
    tt = x_ref.shape[1]
    xn = _rms_scale(x_ref[0])
    hk = (xn * gkv_ref[...]).astype(BF16)
    hq = (xn * gpre_ref[...]).astype(BF16)
    cos = cos_ref[...]
    sin = sin_ref[...]
    q_refs = (q0_ref, q1_ref, q2_ref)
    k_refs = (k0_ref, k1_ref, k2_ref)
    v_refs = (v0_ref, v1_ref, v2_ref)
    w_refs = (w0_ref, w1_ref, w2_ref)
    for g in (2, 1, 0):
        d = B_GROUPS[g][1]
        kf = _rope_cols(_dot(hk, wkv_ref[:, g * 2 * B_WIDTH:g * 2 * B_WIDTH + B_WIDTH]), cos, sin)
        vf = _dot(hk, wkv_ref[:, g * 2 * B_WIDTH + B_WIDTH:(g + 1) * 2 * B_WIDTH])
        qf = _rope_cols(_dot(hq, win_ref[:, g * B_WIDTH:(g + 1) * B_WIDTH]), cos, sin)
        wr = w_refs[g]
        wrows = wr.shape[1]
        wr[0] = _rows_to_kv_heads(kf[tt - wrows:, :], vf[tt - wrows:, :])
        for val, ref in ((qf.astype(BF16), q_refs[g]), (kf.astype(BF16), k_refs[g]), (vf.astype(BF16), v_refs[g])):
            if d == 1:
                ref[0, 0] = val
            else:
                ref[0] = jnp.swapaxes(val.reshape(tt // d, d, val.shape[1]), 0, 1)
        if g == 2 and n_dec_in:
            _dec_attn_kernel(*dec_in, *dec_out)
    qoff = N_GROUPS * B_WIDTH
    zg_ref[0] = _dot(hq, win_ref[:, qoff:qoff + B_WIDTH]).astype(BF16)
    qm_ref[0] = _dot(hq, win_ref[:, qoff + B_WIDTH:qoff + B_WIDTH + M_WIDTH]).astype(BF16)
    zm_ref[0] = _dot(hq, win_ref[:, qoff + B_WIDTH + M_WIDTH:qoff + B_WIDTH + 2 * M_WIDTH]).astype(BF16)


def _layer1_proj_prompt(x1, g_kv, g_pre, wkv, win, cos_t, sin_t, dec_job=None):
    b, s, _ = x1.shape
    tt = min(TOK_TILE, s)
    nt = s // tt
    tile = lambda bb, i: (bb, i, 0)
    in_specs = [
        pl.BlockSpec((1, tt, D_MODEL), tile),
        _const_spec((1, D_MODEL)),
        _const_spec((1, D_MODEL)),
        _const_spec(wkv.shape),
        _const_spec(win.shape),
        pl.BlockSpec((tt, B_HDIM), lambda bb, i: (i, 0)),
        pl.BlockSpec((tt, B_HDIM), lambda bb, i: (i, 0)),
    ]
    qkv_specs, qkv_shapes = [], []
    for _ in range(3):
        for (_, d) in B_GROUPS:
            qkv_specs.append(pl.BlockSpec((1, d, tt // d, B_WIDTH), lambda bb, i: (bb, 0, i, 0)))
            qkv_shapes.append(jax.ShapeDtypeStruct((b, d, s // d, B_WIDTH), BF16))
    gate_specs = [pl.BlockSpec((1, tt, B_WIDTH), tile)] * 3
    gate_shapes = [jax.ShapeDtypeStruct((b, s, B_WIDTH), BF16)] * 3
    win_specs, win_shapes = [], []
    for (w, _) in B_GROUPS:
        wr = min(w, s)
        rows = min(wr, tt)
        nblk = wr // rows
        win_specs.append(pl.BlockSpec(
            (1, rows, 2 * B_HEADS, B_HDIM),
            functools.partial(lambda bb, i, nb: (bb, jnp.maximum(i - (nt - nb), 0), 0, 0), nb=nblk)))
        win_shapes.append(jax.ShapeDtypeStruct((b, wr, 2 * B_HEADS, B_HDIM), F32))
    dec_args, dec_specs, dec_out_specs, dec_out_shapes = [], [], [], []
    if dec_job is not None:
        dec_args, dec_specs, dec_out_specs, dec_out_shapes = _dec_attn_specs(
            *dec_job, seq_index=lambda bb, i: bb * nt + i)
    return pl.pallas_call(
        functools.partial(_l1a_kernel, len(dec_args)),
        grid=(b, nt),
        in_specs=in_specs + dec_specs,
        out_specs=qkv_specs + gate_specs + win_specs + dec_out_specs,
        out_shape=qkv_shapes + gate_shapes + win_shapes + dec_out_shapes,
        compiler_params=pltpu.CompilerParams(
            dimension_semantics=("arbitrary", "arbitrary"), vmem_limit_bytes=VMEM_LIMIT),
        name="layer1_proj_prompt",
    )(x1, g_kv, g_pre, wkv, win, cos_t, sin_t, *dec_args)


def _cols_to_lanes(cols):
    t = cols[0].shape[0]
    lane = lax.broadcasted_iota(jnp.int32, (t, LANES), 1)
    acc = jnp.zeros((t, LANES), F32)
    for h, cvec in enumerate(cols):
        acc = jnp.where(lane == h, cvec, acc)
    return acc


def _band_attn_kernel(q_ref, kc_ref, kp_ref, vc_ref, vp_ref, o_ref, lse_ref):
    nres, tq = q_ref.shape[1:3]
    nsb = tq // ATT_BLK
    j = pl.program_id(2)
    row = lax.broadcasted_iota(jnp.int32, (ATT_BLK, 2 * ATT_BLK), 0)
    col = lax.broadcasted_iota(jnp.int32, (ATT_BLK, 2 * ATT_BLK), 1)
    band = jnp.logical_and(col >= row, col <= row + ATT_BLK)
    first_pen = jnp.where(col < ATT_BLK, jnp.where(j > 0, 0.0, -jnp.inf), 0.0)
    scale = B_HDIM ** -0.5
    blocks = [(r, sb) for r in range(nres) for sb in range(nsb)]
    for g0 in range(0, len(blocks), ATT_GROUP):
        grp = blocks[g0:g0 + ATT_GROUP]
        qs, ks, vs, pens = [], [], [], []
        for r, sb in grp:
            rs = slice(sb * ATT_BLK, (sb + 1) * ATT_BLK)
            ps = slice((sb - 1) * ATT_BLK, sb * ATT_BLK)
            for h in range(B_HEADS):
                hs = slice(h * B_HDIM, (h + 1) * B_HDIM)
                qs.append(q_ref[0, r, rs, hs])
                kp = kp_ref[0, r, :, hs] if sb == 0 else kc_ref[0, r, ps, hs]
                vp = vp_ref[0, r, :, hs] if sb == 0 else vc_ref[0, r, ps, hs]
                ks.append(jnp.concatenate([kp, kc_ref[0, r, rs, hs]], axis=0))
                vs.append(jnp.concatenate([vp, vc_ref[0, r, rs, hs]], axis=0))
                pens.append(sb == 0)
        q3 = jnp.stack(qs)
        k3 = jnp.stack(ks)
        v3 = jnp.stack(vs)
        s = jnp.einsum('uqd,ukd->uqk', q3, k3, preferred_element_type=F32)
        s = jnp.stack([s[u] + first_pen if pens[u] else s[u] for u in range(len(pens))])
        s = jnp.where(band[None], s, -jnp.inf)
        mx = jnp.max(s, axis=-1, keepdims=True)
        p = jnp.exp2((s - mx) * (scale * LOG2E))
        l = jnp.sum(p, axis=-1, keepdims=True)
        o = jnp.einsum('uqk,ukd->uqd', p.astype(BF16), v3, preferred_element_type=F32) / l
        lse = mx * scale + jnp.log(l)
        for i, (r, sb) in enumerate(grp):
            rs = slice(sb * ATT_BLK, (sb + 1) * ATT_BLK)
            for h in range(B_HEADS):
                o_ref[0, r, rs, h * B_HDIM:(h + 1) * B_HDIM] = o[i * B_HEADS + h].astype(BF16)
            lse_ref[0, r, rs, :] = _cols_to_lanes([lse[i * B_HEADS + h] for h in range(B_HEADS)])


def _band_attention(q, k, v):
    b, d, ls, _ = q.shape
    tq = min(ATT_TILE, ls)
    nj = ls // tq
    ratio = tq // ATT_BLK
    nres = min(d, ATT_TILE // tq)
    cur = lambda bb, r, j: (bb, r, j, 0)
    prev = lambda bb, r, j: (bb, r, jnp.maximum(j * ratio - 1, 0), 0)
    return pl.pallas_call(
        _band_attn_kernel,
        grid=(b, d // nres, nj),
        in_specs=[pl.BlockSpec((1, nres, tq, B_WIDTH), cur),
                  pl.BlockSpec((1, nres, tq, B_WIDTH), cur),
                  pl.BlockSpec((1, nres, ATT_BLK, B_WIDTH), prev),
                  pl.BlockSpec((1, nres, tq, B_WIDTH), cur),
                  pl.BlockSpec((1, nres, ATT_BLK, B_WIDTH), prev)],
        out_specs=[pl.BlockSpec((1, nres, tq, B_WIDTH), cur),
                   pl.BlockSpec((1, nres, tq, LANES), cur)],
        out_shape=[jax.ShapeDtypeStruct((b, d, ls, B_WIDTH), BF16),
                   jax.ShapeDtypeStruct((b, d, ls, LANES), F32)],
        compiler_params=pltpu.CompilerParams(
            dimension_semantics=("arbitrary", "arbitrary", "arbitrary"), vmem_limit_bytes=VMEM_LIMIT),
        name="band_attention_d%d" % d,
    )(q, k, k, v, v)


def _unpermute(ref):
    d, rows, width = ref.shape[1:]
    if d == 1:
        return ref[0, 0]
    return jnp.swapaxes(ref[0], 0, 1).reshape(d * rows, width)


def _l1c_kernel(x_ref, o0_ref, o1_ref, o2_ref, l0_ref, l1_ref, l2_ref, zg_ref, qm_ref, zm_ref,
                mkv_ref, wout_ref, gpost_ref, y_ref):
    tt = x_ref.shape[1]
    o_refs = (o0_ref, o1_ref, o2_ref)
    l_refs = (l0_ref, l1_ref, l2_ref)
    outs, lses = [], []
    for g, (_, d) in enumerate(B_GROUPS):
        outs.append(_unpermute(o_refs[g]))
        lses.append(_unpermute(l_refs[g])[:, 0:B_HEADS])
    mx = jnp.maximum(jnp.maximum(lses[0], lses[1]), lses[2])
    es = [jnp.exp(l - mx) for l in lses]
    tot = es[0] + es[1] + es[2]
    ws = [(e / tot).astype(BF16) for e in es]
    parts = []
    for h in range(B_HEADS):
        hs = slice(h * B_HDIM, (h + 1) * B_HDIM)
        acc = ws[0][:, h:h + 1] * outs[0][:, hs]
        acc = acc + ws[1][:, h:h + 1] * outs[1][:, hs]
        acc = acc + ws[2][:, h:h + 1] * outs[2][:, hs]
        parts.append(acc)
    ydil = jnp.concatenate(parts, axis=-1)
    ymix = (ydil.astype(F32) * _silu(zg_ref[0].astype(F32))).astype(BF16)
    mk = mkv_ref[0, :, 0:M_WIDTH]
    mv = mkv_ref[0, :, M_WIDTH:2 * M_WIDTH]
    ym = (_mem_attention(qm_ref[0], mk, mv) * _silu(zm_ref[0].astype(F32))).astype(BF16)
    out = _dot(ymix, wout_ref[0:B_WIDTH, :]) + _dot(ym, wout_ref[B_WIDTH:B_WIDTH + M_WIDTH, :])
    y_ref[0] = x_ref[0] + _rms_scale(out) * gpost_ref[...]


def _layer1_out_prompt(x1, os_, ls_, zg, qm, zm, mkv_bf, w_out, g_post):
    b, s, _ = x1.shape
    tt = min(L1OUT_TILE, s)
    nt = s // tt
    tile = lambda bb, i: (bb, i, 0)
    perm = lambda bb, i: (bb, 0, i, 0)
    in_specs = [pl.BlockSpec((1, tt, D_MODEL), tile)]
    for width in (B_WIDTH, LANES):
        for (_, d) in B_GROUPS:
            in_specs.append(pl.BlockSpec((1, d, tt // d, width), perm))
    in_specs += [pl.BlockSpec((1, tt, B_WIDTH), tile)] * 3
    in_specs += [pl.BlockSpec((1, N_MEM, 2 * M_WIDTH), lambda bb, i: (bb, 0, 0)),
                 _const_spec(w_out.shape), _const_spec((1, D_MODEL))]
    return pl.pallas_call(
        _l1c_kernel,
        grid=(b, nt),
        in_specs=in_specs,
        out_specs=pl.BlockSpec((1, tt, D_MODEL), tile),
        out_shape=jax.ShapeDtypeStruct((b, s, D_MODEL), F32),
        compiler_params=pltpu.CompilerParams(
            dimension_semantics=("arbitrary", "arbitrary"), vmem_limit_bytes=VMEM_LIMIT),
        name="layer1_out_prompt",
    )(x1, *os_, *ls_, zg, qm, zm, mkv_bf, w_out, g_post)


def _rope_tables(pos):
    half = B_HDIM // 2
    inv = ROPE_THETA ** (-jnp.arange(half, dtype=F32) / half)
    ang = pos[:, None] * inv[None, :]
    cos = jnp.cos(ang)
    sin = jnp.sin(ang)
    return jnp.concatenate([cos, cos], axis=-1), jnp.concatenate([-sin, sin], axis=-1)


def _prompt_group(x_prompt, mem_prompt, p, sample=None):
    b, s, _ = x_prompt.shape
    memkv_f, memkv_b = _memkv(mem_prompt.reshape(b * N_MEM, D_MODEL), p['w_mkv'])
    depth = memkv_f.shape[0]
    memkv_b = memkv_b.reshape(depth, b, N_MEM, 2 * M_WIDTH)
    job0 = sample.mlstm_job() if sample is not None else None
    if job0 is not None and b * (s // min(L0_TILE, s)) + 2 < 2 * job0[0].shape[0]:
        job0 = None
    outs0 = _layer0_prompt_pipelined(
        x_prompt, p['g_pre'][0:1], p['w_in_a'][0], p['conv_w_a'][0], p['conv_b_a'], p['w_q_a'][0],
        p['w_k_a'][0], p['w_v_a'][0], p['w_if_a'], p['b_if_a'],
        p['g_hn_a'], p['skip_a'], memkv_b[0], p['w_out_a'][0], p['g_post'][0:1], job0)
    x1, conv_p, c_p, n_p, m_pad = outs0[0:5]
    if sample is not None:
        sample.after_mlstm(outs0[5:10] if job0 is not None else _dec_mlstm(*sample.mlstm_job()))
    cos_t, sin_t = _rope_tables(jnp.arange(s, dtype=F32))
    job1 = sample.attn_job if sample is not None else None
    if job1 is not None and b * (s // min(TOK_TILE, s)) != job1[0].shape[0]:
        job1 = None
    outs = _layer1_proj_prompt(x1, p['g_kv'], p['g_pre'][1:2], p['w_kv_b'], p['w_in_b'][0], cos_t, sin_t,
                               job1)
    if sample is not None:
        sample.after_attn(tuple(outs[15:17]) if job1 is not None else _dec_attn(*sample.attn_job))
    qs, ks, vs = outs[0:3], outs[3:6], outs[6:9]
    zg, qm, zm = outs[9:12]
    wins = outs[12:15]
    os_, ls_ = [], []
    for g in range(N_GROUPS):
        o, l = _band_attention(qs[g], ks[g], vs[g])
        os_.append(o)
        ls_.append(l)
    y = _layer1_out_prompt(x1, os_, ls_, zg, qm, zm, memkv_b[1], p['w_out_b'][0], p['g_post'][1:2])
    m_p = m_pad[:, 0:A_HEADS, 0][None]
    wins = [w.reshape(b, w.shape[1], 2, B_HEADS, B_HDIM) for w in wins]
    memkv_p = memkv_f.reshape(depth, b, N_MEM, 2, M_HEADS, M_HDIM)
    return y, conv_p, c_p, n_p, m_p, wins, memkv_p


def _prep_params(g_pre, g_post, w_in_a, conv_w_a, conv_b_a, w_q_a, w_k_a, w_v_a, w_if_a, b_if_a,
                 g_hn_a, skip_a, w_out_a, g_kv, w_kv_b, w_in_b, w_out_b, w_mkv):
    wif = jnp.pad(w_if_a[0], ((0, 0), (0, LANES - 2 * A_HEADS))).astype(BF16)
    bif = jnp.pad(b_if_a[0], (0, LANES - 2 * A_HEADS))[None, :]
    return {
        'g_pre': g_pre, 'g_post': g_post,
        'w_in_a': w_in_a.astype(BF16), 'conv_w_a': conv_w_a, 'conv_b_a': conv_b_a,
        'w_q_a': w_q_a.astype(BF16), 'w_k_a': w_k_a.astype(BF16), 'w_v_a': w_v_a.astype(BF16),
        'w_if_a': wif, 'b_if_a': bif, 'g_hn_a': g_hn_a, 'skip_a': skip_a,
        'w_out_a': w_out_a.astype(BF16), 'g_kv': g_kv[None, :], 'w_kv_b': w_kv_b.astype(BF16),
        'w_in_b': w_in_b.astype(BF16), 'w_out_b': w_out_b.astype(BF16), 'w_mkv': w_mkv.astype(BF16),
    }


def _dec_l0_proj_kernel(x_ref, gpre_ref, win_ref, cst_ref, convw_ref, convb_ref, wq_ref, wk_ref, wv_ref,
                        wif_ref, bif_ref,
                        q_ref, k_ref, v_ref, gates_ref, xc_ref, opre_ref, zg_ref, qm_ref, zm_ref, cnew_ref):
    h = (_rms_scale(x_ref[:, 0, :]) * gpre_ref[...]).astype(BF16)
    u = _dot(h, win_ref[:, 0:A_INNER])
    opre_ref[...] = _dot(h, win_ref[:, A_INNER:2 * A_INNER])
    zg_ref[...] = _dot(h, win_ref[:, 2 * A_INNER:3 * A_INNER])
    qm_ref[...] = _rows_to_heads(_dot(h, win_ref[:, 3 * A_INNER:3 * A_INNER + M_WIDTH]))
    zm_ref[...] = _dot(h, win_ref[:, 3 * A_INNER + M_WIDTH:3 * A_INNER + 2 * M_WIDTH])
    cw = convw_ref[...]
    xc = convb_ref[...] + cst_ref[0, :, 0, :] * cw[0:1, :]
    xc = xc + cst_ref[0, :, 1, :] * cw[1:2, :]
    xc = xc + cst_ref[0, :, 2, :] * cw[2:3, :]
    xc = xc + u * cw[3:4, :]
    xc = _silu(xc)
    xc_ref[...] = xc
    cnew_ref[0, :, 0, :] = cst_ref[0, :, 1, :]
    cnew_ref[0, :, 1, :] = cst_ref[0, :, 2, :]
    cnew_ref[0, :, 2, :] = u
    qs, ks, vs, cat = [], [], [], []
    for hd in range(A_HEADS):
        sl = slice(hd * A_HDIM, (hd + 1) * A_HDIM)
        xh = xc[:, sl].astype(BF16)
        qh = _dot(xh, wq_ref[hd])
        kh = _dot(xh, wk_ref[hd]) * (A_HDIM ** -0.5)
        vh = _dot(u[:, sl].astype(BF16), wv_ref[hd])
        qs.append(qh)
        ks.append(kh)
        vs.append(vh)
        cat += [qh.astype(BF16), kh.astype(BF16), vh.astype(BF16)]
    q_ref[...] = jnp.concatenate(qs, axis=-1)
    k_ref[...] = jnp.concatenate(ks, axis=-1)
    v_ref[...] = jnp.concatenate(vs, axis=-1)
    gates_ref[...] = _dot(jnp.concatenate(cat, axis=-1), wif_ref[...]) + bif_ref[...]


def _whole(shape):
    nd = len(shape)
    return pl.BlockSpec(shape, lambda *_: (0,) * nd)


def _dec_l0_proj(x, g_pre, w_in, cst, conv_w, conv_b, wq, wk, wv, wif, bif):
    nb = x.shape[0]
    args = (x, g_pre, w_in, cst, conv_w, conv_b, wq, wk, wv, wif, bif)
    f = lambda *s: jax.ShapeDtypeStruct(s, F32)
    out_shape = [f(nb, A_INNER), f(nb, A_INNER), f(nb, A_INNER), f(nb, LANES), f(nb, A_INNER), f(nb, A_INNER),
                 f(nb, A_INNER), f(nb, M_HEADS, M_HDIM), f(nb, M_WIDTH), f(1, nb, CONV_W - 1, A_INNER)]
    return pl.pallas_call(
        _dec_l0_proj_kernel,
        grid=(1,),
        in_specs=[_whole(a.shape) for a in args],
        out_specs=[_whole(o.shape) for o in out_shape],
        out_shape=out_shape,
        compiler_params=pltpu.CompilerParams(dimension_semantics=("arbitrary",), vmem_limit_bytes=VMEM_LIMIT),
        name="dec_l0_proj",
    )(*args)


def _row_to_col(row, eye):
    return jnp.sum(jnp.where(eye, row, 0.0), axis=-1, keepdims=True)


def _col_to_row(colv, eye):
    return jnp.sum(jnp.where(eye, colv, 0.0), axis=0, keepdims=True)


def _dec_mem_attention(q, kv_ref_view):
    kk = kv_ref_view[:, 0]
    vv = kv_ref_view[:, 1]
    s = jnp.sum(kk * (q * (M_HDIM ** -0.5))[None], axis=-1, keepdims=True)
    mx = jnp.max(s, axis=0, keepdims=True)
    p = jnp.exp(s - mx)
    return jnp.sum(p * vv, axis=0) / jnp.sum(p, axis=0)


def _dec_mlstm_body(b, heads, with_mem, q_ref, k_ref, v_ref, gates_ref, m_ref, c_ref, n_ref, qm_ref, kv_ref,
                    hs_ref, c_out, n_out, m_out, ym_ref):
    rb = pl.ds(b, 1)
    g = gates_ref[rb, :]
    mrow = m_ref[0, rb, :]
    r = lax.broadcasted_iota(jnp.int32, (A_HDIM, A_HDIM), 0)
    c = lax.broadcasted_iota(jnp.int32, (A_HDIM, A_HDIM), 1)
    eye = r == c
    sl = {h: slice(h * A_HDIM, (h + 1) * A_HDIM) for h in heads}
    qh = {h: q_ref[rb, sl[h]] for h in heads}
    kh = {h: k_ref[rb, sl[h]] for h in heads}
    vh = {h: v_ref[rb, sl[h]] for h in heads}
    c_old = {h: c_ref[0, 0, h] for h in heads}
    n_old = {h: n_ref[0, 0, h:h + 1, :] for h in heads}
    li = {h: g[:, h:h + 1] for h in heads}
    lf = {h: _log_sigmoid(g[:, 4 + h:5 + h]) for h in heads}
    m_old = {h: mrow[:, h:h + 1] for h in heads}
    cq = {h: jnp.sum(c_old[h] * qh[h], axis=-1, keepdims=True) for h in heads}
    v_col = {h: _row_to_col(vh[h], eye) for h in heads}
    nq = {h: jnp.sum(n_old[h] * qh[h], axis=-1, keepdims=True) for h in heads}
    qk = {h: jnp.sum(qh[h] * kh[h], axis=-1, keepdims=True) for h in heads}
    inter = {h: lf[h] + m_old[h] for h in heads}
    m_new = {h: jnp.maximum(inter[h], li[h]) for h in heads}
    ws = {h: jnp.exp(li[h] - m_new[h]) for h in heads}
    dec = {h: jnp.exp(inter[h] - m_new[h]) for h in heads}
    sc = {h: qk[h] * ws[h] for h in heads}
    den = {h: sc[h] + dec[h] * nq[h] for h in heads}
    h_col = {h: (sc[h] * v_col[h] + dec[h] * cq[h]) / jnp.maximum(jnp.abs(den[h]), jnp.exp(-m_new[h]))
             for h in heads}
    for h in heads:
        c_out[0, 0, h] = dec[h] * c_old[h] + (ws[h] * v_col[h]) * kh[h]
        n_out[0, 0, h:h + 1, :] = dec[h] * n_old[h] + ws[h] * kh[h]
        m_out[0, h:h + 1, :] = jnp.broadcast_to(m_new[h], (1, LANES))
    for h in heads:
        hs_ref[0, :, sl[h]] = _col_to_row(h_col[h], eye)
    if with_mem:
        ym_ref[0] = _dec_mem_attention(qm_ref[0], kv_ref.at[0, 0])


def _dec_mlstm_kernel(*refs):
    _dec_mlstm_body(pl.program_id(0), range(A_HEADS), True, *refs)


def _dec_mlstm_specs(q, k, v, gates, m_in, state_c, state_n, qm3, cache_mem_kv, seq_index):
    nb = q.shape[0]

    def at(*tail, lead=()):
        return lambda *idx: lead + (seq_index(*idx),) + tail

    in_specs = [_whole(q.shape), _whole(k.shape), _whole(v.shape), _whole(gates.shape), _whole(m_in.shape),
                pl.BlockSpec((1, 1, A_HEADS, A_HDIM, A_HDIM), at(0, 0, 0, lead=(0,))),
                pl.BlockSpec((1, 1, A_HEADS, A_HDIM), at(0, 0, lead=(0,))),
                pl.BlockSpec((1, M_HEADS, M_HDIM), at(0, 0)),
                pl.BlockSpec((1, 1, N_MEM, 2, M_HEADS, M_HDIM), at(0, 0, 0, 0, lead=(0,)))]
    out_specs = [pl.BlockSpec((1, 1, A_INNER), at(0, 0)),
                 pl.BlockSpec((1, 1, A_HEADS, A_HDIM, A_HDIM), at(0, 0, 0, lead=(0,))),
                 pl.BlockSpec((1, 1, A_HEADS, A_HDIM), at(0, 0, lead=(0,))),
                 pl.BlockSpec((1, A_HEADS, LANES), at(0, 0)),
                 pl.BlockSpec((1, M_HEADS, M_HDIM), at(0, 0))]
    out_shapes = [jax.ShapeDtypeStruct((nb, 1, A_INNER), F32),
                  jax.ShapeDtypeStruct(state_c.shape, F32),
                  jax.ShapeDtypeStruct(state_n.shape, F32),
                  jax.ShapeDtypeStruct((nb, A_HEADS, LANES), F32),
                  jax.ShapeDtypeStruct((nb, M_HEADS, M_HDIM), F32)]
    return [q, k, v, gates, m_in, state_c, state_n, qm3, cache_mem_kv], in_specs, out_specs, out_shapes


def _dec_mlstm(*job):
    args, in_specs, out_specs, out_shapes = _dec_mlstm_specs(*job, seq_index=lambda b: b)
    return pl.pallas_call(
        _dec_mlstm_kernel,
        grid=(args[0].shape[0],),
        in_specs=in_specs,
        out_specs=out_specs,
        out_shape=out_shapes,
        compiler_params=pltpu.CompilerParams(dimension_semantics=("arbitrary",), vmem_limit_bytes=VMEM_LIMIT),
        name="dec_mlstm",
    )(*args)


def _dec_mid_kernel(hs_ref, opre_ref, xc_ref, zg_ref, ym_ref, zm_ref, x_ref, ghn_ref, skip_ref, wout_ref,
                    gpost_ref, gkv_ref, gpre_ref, wkv_ref, win_ref, cos_ref, sin_ref,
                    x1_ref, q_ref, k_ref, v_ref, zg1_ref, qm1_ref, zm1_ref, win0_ref, win1_ref, win2_ref):
    hh = _sigmoid(opre_ref[...]) * hs_ref[:, 0, :]
    parts = []
    for h in range(A_HEADS):
        v = hh[:, h * A_HDIM:(h + 1) * A_HDIM]
        mu = jnp.mean(v, axis=-1, keepdims=True)
        var = jnp.mean(jnp.square(v - mu), axis=-1, keepdims=True)
        parts.append((v - mu) * lax.rsqrt(var + EPS))
    y = jnp.concatenate(parts, axis=-1) * ghn_ref[...] + skip_ref[...] * xc_ref[...]
    ymix = (y * _silu(zg_ref[...])).astype(BF16)
    ym = (_heads_to_rows(ym_ref) * _silu(zm_ref[...])).astype(BF16)
    out = _dot(ymix, wout_ref[0:A_INNER, :]) + _dot(ym, wout_ref[A_INNER:A_INNER + M_WIDTH, :])
    x1 = x_ref[:, 0, :] + _rms_scale(out) * gpost_ref[...]
    x1_ref[...] = x1
    xn = _rms_scale(x1)
    hk = (xn * gkv_ref[...]).astype(BF16)
    hq = (xn * gpre_ref[...]).astype(BF16)
    cos = cos_ref[...]
    sin = sin_ref[...]
    ks, vs = [], []
    for g in range(N_GROUPS):
        ks.append(_rope_cols(_dot(hk, wkv_ref[:, g * 2 * B_WIDTH:g * 2 * B_WIDTH + B_WIDTH]), cos, sin))
        vs.append(_dot(hk, wkv_ref[:, g * 2 * B_WIDTH + B_WIDTH:(g + 1) * 2 * B_WIDTH]))
    k_ref[...] = _rows_to_heads(jnp.concatenate(ks, axis=-1))
    v_ref[...] = _rows_to_heads(jnp.concatenate(vs, axis=-1))
    for g, wref in enumerate((win0_ref, win1_ref, win2_ref)):
        wref[...] = _rows_to_kv_heads(ks[g], vs[g])
    qoff = N_GROUPS * B_WIDTH
    q_ref[...] = _rows_to_heads(_rope_cols(_dot(hq, win_ref[:, 0:qoff]), cos, sin))
    zg1_ref[...] = _dot(hq, win_ref[:, qoff:qoff + B_WIDTH])
    qm1_ref[...] = _rows_to_heads(_dot(hq, win_ref[:, qoff + B_WIDTH:qoff + B_WIDTH + M_WIDTH]))
    zm1_ref[...] = _dot(hq, win_ref[:, qoff + B_WIDTH + M_WIDTH:qoff + B_WIDTH + 2 * M_WIDTH])


def _dec_mid(hs, opre, xc, zg, ym, zm, x, ghn, skip, w_out, g_post, g_kv, g_pre, wkv, win, cos, sin):
    nb = x.shape[0]
    args = (hs, opre, xc, zg, ym, zm, x, ghn, skip, w_out, g_post, g_kv, g_pre, wkv, win, cos, sin)
    f = lambda *s: jax.ShapeDtypeStruct(s, F32)
    gh = N_GROUPS * B_HEADS
    out_shape = [f(nb, D_MODEL), f(nb, gh, B_HDIM), f(nb, gh, B_HDIM), f(nb, gh, B_HDIM),
                 f(nb, B_WIDTH), f(nb, M_HEADS, M_HDIM), f(nb, M_WIDTH)] + [f(nb, 2 * B_HEADS, B_HDIM)] * N_GROUPS
    return pl.pallas_call(
        _dec_mid_kernel,
        grid=(1,),
        in_specs=[_whole(a.shape) for a in args],
        out_specs=[_whole(o.shape) for o in out_shape],
        out_shape=out_shape,
        compiler_params=pltpu.CompilerParams(dimension_semantics=("arbitrary",), vmem_limit_bytes=VMEM_LIMIT),
        name="dec_mid",
    )(*args)


def _dec_attn_kernel(q_ref, kn_ref, vn_ref, w0_ref, w1_ref, w2_ref, qm_ref, kv_ref, ydil_ref, ym_ref):
    w_refs = (w0_ref, w1_ref, w2_ref)
    scale = B_HDIM ** -0.5
    groups = range(N_GROUPS)
    hsl = [slice(g * B_HEADS, (g + 1) * B_HEADS) for g in groups]
    q = [q_ref[0, hsl[g], :] * scale for g in groups]
    s_c = [jnp.sum(w_refs[g][0, :, 0] * q[g][None], axis=-1, keepdims=True) for g in groups]
    s_n = [jnp.sum(kn_ref[0, hsl[g], :] * q[g], axis=-1, keepdims=True) for g in groups]
    mxs = [jnp.maximum(jnp.max(s_c[g], axis=0), s_n[g]) for g in groups]
    p_c = [jnp.exp(s_c[g] - mxs[g][None]) for g in groups]
    p_n = [jnp.exp(s_n[g] - mxs[g]) for g in groups]
    ls = [jnp.sum(p_c[g], axis=0) + p_n[g] for g in groups]
    outs = [(jnp.sum(p_c[g] * w_refs[g][0, :, 1], axis=0) + p_n[g] * vn_ref[0, hsl[g], :]) / ls[g]
            for g in groups]
    lses = [mxs[g] + jnp.log(ls[g]) for g in groups]
    mx = jnp.maximum(jnp.maximum(lses[0], lses[1]), lses[2])
    es = [jnp.exp(l - mx) for l in lses]
    tot = es[0] + es[1] + es[2]
    ydil_ref[0] = (es[0] / tot) * outs[0] + (es[1] / tot) * outs[1] + (es[2] / tot) * outs[2]
    ym_ref[0] = _dec_mem_attention(qm_ref[0], kv_ref.at[0, 0])


def _dec_attn_specs(q4, kn4, vn4, cw0, cw1, cw2, qm3, cache_mem_kv, layer, seq_index):
    nb = q4.shape[0]
    rows = B_GROUPS[0][0]

    def at(*tail):
        return lambda *idx: (seq_index(*idx),) + tail

    win_specs = [pl.BlockSpec((1, rows, 2, B_HEADS, B_HDIM), at(0, 0, 0, 0)),
                 pl.BlockSpec((1, rows, None, 2, B_HEADS, B_HDIM), at(0, 0, 0, 0, 0)),
                 pl.BlockSpec((1, rows, None, 2, B_HEADS, B_HDIM), at(0, 0, 0, 0, 0))]
    in_specs = [pl.BlockSpec((1, N_GROUPS * B_HEADS, B_HDIM), at(0, 0))] * 3 + win_specs + [
        pl.BlockSpec((1, M_HEADS, M_HDIM), at(0, 0)),
        pl.BlockSpec((1, 1, N_MEM, 2, M_HEADS, M_HDIM), lambda *idx: (layer, seq_index(*idx), 0, 0, 0, 0))]
    out_specs = [pl.BlockSpec((1, B_HEADS, B_HDIM), at(0, 0)), pl.BlockSpec((1, M_HEADS, M_HDIM), at(0, 0))]
    out_shapes = [jax.ShapeDtypeStruct((nb, B_HEADS, B_HDIM), F32),
                  jax.ShapeDtypeStruct((nb, M_HEADS, M_HDIM), F32)]
    return [q4, kn4, vn4, cw0, cw1, cw2, qm3, cache_mem_kv], in_specs, out_specs, out_shapes


def _dec_attn(q4, kn4, vn4, cw0, cw1, cw2, qm3, cache_mem_kv, layer):
    args, in_specs, out_specs, out_shapes = _dec_attn_specs(
        q4, kn4, vn4, cw0, cw1, cw2, qm3, cache_mem_kv, layer, seq_index=lambda b: b)
    return pl.pallas_call(
        _dec_attn_kernel,
        grid=(q4.shape[0],),
        in_specs=in_specs,
        out_specs=out_specs,
        out_shape=out_shapes,
        compiler_params=pltpu.CompilerParams(dimension_semantics=("arbitrary",), vmem_limit_bytes=VMEM_LIMIT),
        name="dec_attn",
    )(*args)


def _dec_out_kernel(ydil_ref, zg_ref, ym_ref, zm_ref, x_ref, wout_ref, gpost_ref, y_ref):
    ymix = (_heads_to_rows(ydil_ref) * _silu(zg_ref[...])).astype(BF16)
    ym = (_heads_to_rows(ym_ref) * _silu(zm_ref[...])).astype(BF16)
    out = _dot(ymix, wout_ref[0:B_WIDTH, :]) + _dot(ym, wout_ref[B_WIDTH:B_WIDTH + M_WIDTH, :])
    y_ref[:, 0, :] = x_ref[...] + _rms_scale(out) * gpost_ref[...]


def _dec_out(ydil, zg, ym, zm, x1, w_out, g_post):
    args = (ydil, zg, ym, zm, x1, w_out, g_post)
    out_shape = (x1.shape[0], 1, x1.shape[1])
    return pl.pallas_call(
        _dec_out_kernel,
        grid=(1,),
        in_specs=[_whole(a.shape) for a in args],
        out_specs=_whole(out_shape),
        out_shape=jax.ShapeDtypeStruct(out_shape, F32),
        compiler_params=pltpu.CompilerParams(dimension_semantics=("arbitrary",), vmem_limit_bytes=VMEM_LIMIT),
        name="dec_out",
    )(*args)


class _SampleGroup:
    def __init__(self, x_sample, state_conv, state_c, state_n, state_m, cache_wins, cache_mem_kv, p):
        self.p = p
        self.nb = nb = x_sample.shape[0]
        self.cache_wins = cache_wins
        self.cache_mem_kv = cache_mem_kv
        self.x = x_sample
        q, k, v, gates, self.xc, self.opre, self.zg, qm3, self.zm, self.conv_s = _dec_l0_proj(
            x_sample, p['g_pre'][0:1], p['w_in_a'][0], state_conv, p['conv_w_a'][0], p['conv_b_a'],
            p['w_q_a'][0], p['w_k_a'][0], p['w_v_a'][0], p['w_if_a'], p['b_if_a'])
        self._mlstm_job = (q, k, v, gates, state_m, state_c, state_n, qm3, cache_mem_kv)
        self.attn_job = None

    def mlstm_job(self):
        return self._mlstm_job

    def after_mlstm(self, res):
        p, nb = self.p, self.nb
        hs, self.c_s, self.n_s, m_rows, ym0 = res
        self.m_s = m_rows[:, :, 0][None]
        cos, sin = _rope_tables(PAST_LEN + jnp.arange(1, dtype=F32))
        self.x1, qd, kn, vn, self.zg1, qm1, self.zm1, w0, w1, w2 = _dec_mid(
            hs, self.opre, self.xc, self.zg, ym0, self.zm, self.x,
            p['g_hn_a'], p['skip_a'], p['w_out_a'][0], p['g_post'][0:1], p['g_kv'], p['g_pre'][1:2],
            p['w_kv_b'], p['w_in_b'][0], cos, sin)
        cws = [self.cache_wins[0]]
        for g in (1, 2):
            w, d = B_GROUPS[g]
            cws.append(self.cache_wins[g].reshape(nb, w // d, d, 2, B_HEADS, B_HDIM))
        self.attn_job = (qd, kn, vn, cws[0], cws[1], cws[2], qm1, self.cache_mem_kv, 1)
        self.wins_s = [w.reshape(nb, 1, 2, B_HEADS, B_HDIM) for w in (w0, w1, w2)]

    def after_attn(self, res):
        p = self.p
        ydil, ym1 = res
        self.y = _dec_out(ydil, self.zg1, ym1, self.zm1, self.x1, p['w_out_b'][0], p['g_post'][1:2])

    def outputs(self):
        return self.y, self.conv_s, self.c_s, self.n_s, self.m_s, self.wins_s


def kernel(x_prompt, x_sample, mem_prompt, state_conv, state_C, state_n, state_m, cache_win0, cache_win1,
           cache_win2, cache_mem_kv, g_pre, g_post, w_in_a, conv_w_a, conv_b_a, w_q_a, w_k_a, w_v_a, w_if_a,
           b_if_a, g_hn_a, skip_a, w_out_a, g_kv, w_kv_b, w_in_b, w_out_b, w_mkv):
    p = _prep_params(g_pre, g_post, w_in_a, conv_w_a, conv_b_a, w_q_a, w_k_a, w_v_a, w_if_a, b_if_a,
                     g_hn_a, skip_a, w_out_a, g_kv, w_kv_b, w_in_b, w_out_b, w_mkv)
    sample = _SampleGroup(x_sample, state_conv, state_C, state_n, state_m,
                          (cache_win0, cache_win1, cache_win2), cache_mem_kv, p)
    y_p, conv_p, c_p, n_p, m_p, wins_p, memkv_p = _prompt_group(x_prompt, mem_prompt, p, sample)
    y_s, conv_s, c_s, n_s, m_s, wins_s = sample.outputs()
    return (y_p, y_s, conv_p, c_p, n_p, m_p, wins_p[0], wins_p[1], wins_p[2], memkv_p,
            conv_s, c_s, n_s, m_s, wins_s[0], wins_s[1], wins_s[2])
```

```python
import functools

import jax
import jax.numpy as jnp
from jax import lax
from jax.experimental import pallas as pl
from jax.experimental.pallas import tpu as pltpu

F32 = jnp.float32
BF16 = jnp.bfloat16

D_MODEL = 1024
A_HEADS = 4
A_HDIM = 256
A_INNER = 1024
CONV_W = 4
A_CHUNK = 128
B_GROUPS = ((128, 1), (512, 4), (2048, 16))
N_GROUPS = 3
B_HEADS = 4
B_HDIM = 128
B_WIDTH = 512
N_MEM = 256
M_HEADS = 4
M_HDIM = 128
M_WIDTH = 512
ROPE_THETA = 10000.0
EPS = 1e-6
PAST_LEN = 8192

LANES = 128
TOK_TILE = 512
L0_TILE = 256
L1OUT_TILE = 1024
ATT_BLK = 128
ATT_TILE = 2048
ATT_GROUP = 2
CAST_STEPS = 8
VMEM_LIMIT = 56 * 1024 * 1024

NT_DIMS = (((1,), (1,)), ((), ()))
LOG2E = 1.4426950408889634


def _dot(a, b):
    return jnp.dot(a, b, preferred_element_type=F32)


def _dot_nt(a, b):
    return lax.dot_general(a, b, NT_DIMS, preferred_element_type=F32)


def _sigmoid(x):
    return 1.0 / (1.0 + jnp.exp(-x))


def _silu(x):
    return x * _sigmoid(x)


def _log_sigmoid(x):
    return jnp.minimum(x, 0.0) - jnp.log(1.0 + jnp.exp(-jnp.abs(x)))


def _rms_scale(x):
    return x * lax.rsqrt(jnp.mean(x * x, axis=-1, keepdims=True) + EPS)


def _const_spec(shape):
    nd = len(shape)
    return pl.BlockSpec(shape, lambda *_: (0,) * nd, pipeline_mode=pl.Buffered(1))


def _mem_attention(qm, mk, mv):
    heads = range(M_HEADS)
    sl = [slice(h * M_HDIM, (h + 1) * M_HDIM) for h in heads]
    s = [_dot_nt(qm[:, sl[h]], mk[:, sl[h]]) for h in heads]
    mx = [jnp.max(s[h], axis=-1, keepdims=True) for h in heads]
    p = [jnp.exp2((s[h] - mx[h]) * (M_HDIM ** -0.5 * LOG2E)) for h in heads]
    l = [jnp.sum(p[h], axis=-1, keepdims=True) for h in heads]
    outs = [_dot(p[h].astype(BF16), mv[:, sl[h]]) / l[h] for h in heads]
    return jnp.concatenate(outs, axis=-1)


def _rows_to_heads(x):
    return jnp.swapaxes(jnp.stack([x[:, h * LANES:(h + 1) * LANES] for h in range(x.shape[1] // LANES)]), 0, 1)


def _heads_to_rows(ref):
    return jnp.concatenate([ref[:, h, :] for h in range(ref.shape[1])], axis=-1)


def _rows_to_kv_heads(k, v):
    pieces = [a[:, h * LANES:(h + 1) * LANES] for a in (k, v) for h in range(a.shape[1] // LANES)]
    return jnp.swapaxes(jnp.stack(pieces), 0, 1)


def _memkv_kernel(m_ref, w_ref, o_ref, ob_ref):
    r = _dot(m_ref[...].astype(BF16), w_ref[0])
    o_ref[0] = _rows_to_kv_heads(r[:, 0:M_WIDTH], r[:, M_WIDTH:2 * M_WIDTH])
    ob_ref[0] = r.astype(BF16)


def _memkv(mem2d, w_bf):
    nm = mem2d.shape[0]
    nl = w_bf.shape[0]
    tm = min(512, nm)
    return pl.pallas_call(
        _memkv_kernel,
        grid=(nl, nm // tm),
        in_specs=[pl.BlockSpec((tm, D_MODEL), lambda l, i: (i, 0)),
                  pl.BlockSpec((1, D_MODEL, 2 * M_WIDTH), lambda l, i: (l, 0, 0))],
        out_specs=[pl.BlockSpec((1, tm, 2 * M_HEADS, M_HDIM), lambda l, i: (l, i, 0, 0)),
                   pl.BlockSpec((1, tm, 2 * M_WIDTH), lambda l, i: (l, i, 0))],
        out_shape=[jax.ShapeDtypeStruct((nl, nm, 2 * M_HEADS, M_HDIM), F32),
                   jax.ShapeDtypeStruct((nl, nm, 2 * M_WIDTH), BF16)],
        compiler_params=pltpu.CompilerParams(dimension_semantics=("arbitrary", "arbitrary")),
        name="memkv",
    )(mem2d, w_bf)


def _mlstm_chunk(rs, g, qkv_v, kt_v, c_s, n_s, m_s, causal, hs):
    ls = _log_sigmoid(g)
    tok = lax.broadcasted_iota(jnp.int32, (A_CHUNK, LANES), 0)
    bc = ls
    shift = 1
    while shift < A_CHUNK:
        bc = bc + jnp.where(tok >= shift, pltpu.roll(bc, shift, 0), 0.0)
        shift *= 2
    lane = lax.broadcasted_iota(jnp.int32, (A_CHUNK, LANES), 1)
    xt = jnp.where(lane < A_HEADS, g, bc).T
    yield
    heads = range(A_HEADS)
    b_col = [bc[:, 4 + h:5 + h] for h in heads]
    b_row = [xt[4 + h:5 + h, :] for h in heads]
    li_row = [xt[h:h + 1, :] for h in heads]
    li_col = [g[:, h:h + 1] for h in heads]
    m_old = [m_s[h:h + 1, 0:1] for h in heads]
    b_last = [bc[A_CHUNK - 1:A_CHUNK, 4 + h:5 + h] for h in heads]
    qh = [qkv_v[rs, h * 3 * A_HDIM:h * 3 * A_HDIM + A_HDIM] for h in heads]
    kh = [qkv_v[rs, h * 3 * A_HDIM + A_HDIM:h * 3 * A_HDIM + 2 * A_HDIM] for h in heads]
    vh = [qkv_v[rs, h * 3 * A_HDIM + 2 * A_HDIM:(h + 1) * 3 * A_HDIM] for h in heads]
    kt = [kt_v[h] for h in heads]
    c_old = [c_s[h] for h in heads]
    n_old = [n_s[h:h + 1, :] for h in heads]
    qk = [_dot_nt(qh[h], kh[h]) for h in heads]
    qc = [_dot(qh[h], c_old[h].astype(BF16)) for h in heads]
    dm = [jnp.where(causal, b_col[h] - b_row[h] + li_row[h], -jnp.inf) for h in heads]
    inter = [b_col[h] + m_old[h] for h in heads]
    m_row = [jnp.maximum(inter[h], jnp.max(dm[h], axis=-1, keepdims=True)) for h in heads]
    g_max = [jnp.max(b_last[h] - b_row[h] + li_row[h], axis=-1, keepdims=True) for h in heads]
    m_new = [jnp.maximum(b_last[h] + m_old[h], g_max[h]) for h in heads]
    yield
    sc = [qk[h] * jnp.exp(dm[h] - m_row[h]) for h in heads]
    dec = [jnp.exp(inter[h] - m_row[h]) for h in heads]
    ws_col = [jnp.exp(b_last[h] - b_col[h] + li_col[h] - m_new[h]) for h in heads]
    dc = [jnp.exp(b_last[h] + m_old[h] - m_new[h]) for h in heads]
    yield
    sv = [_dot(sc[h].astype(BF16), vh[h]) for h in heads]
    wv =[(ws_col[h] * vh[h].astype(F32)).astype(BF16) for h in heads]
    upd = [_dot(kt[h], wv[h]) for h in heads]
    yield
    for h in heads:
        den = (jnp.sum(sc[h], axis=-1, keepdims=True)
               + dec[h] * jnp.sum(qh[h].astype(F32) * n_old[h], axis=-1, keepdims=True))
        num = sv[h] + dec[h] * qc[h]
        hs.append(num / jnp.maximum(jnp.abs(den), jnp.exp(-m_row[h])))
    yield
    for h in heads:
        c_s[h] = dc[h] * c_old[h] + upd[h]
        n_s[h:h + 1, :] = dc[h] * n_old[h] + jnp.sum(ws_col[h] * kh[h].astype(F32), axis=0, keepdims=True)
        m_s[h:h + 1, :] = jnp.broadcast_to(m_new[h], (1, LANES))
    yield


def _l0p_kernel(nt, n_dec_seq, *refs):
    k_in, k_out = 16, 5
    k_dec_in, k_dec_out = (9, 5) if n_dec_seq else (0, 0)
    (x_ref, xp_ref, gpre_ref, win_ref, convw_ref, convb_ref, wq_ref, wk_ref, wv_ref,
     wif_ref, bif_ref, ghn_ref, skip_ref, mkv_ref, wout_ref, gpost_ref) = refs[0:k_in]
    dec_in = refs[k_in:k_in + k_dec_in]
    o0 = k_in + k_dec_in
    x1_ref, conv_out, c_out, n_out, m_out = refs[o0:o0 + k_out]
    dec_out = refs[o0 + k_out:o0 + k_out + k_dec_out]
    (h_s, u_s, ymix_s, xc_s, opre_s, zg_s, qm_s, zm_s, qkv_s, kt_s, gates_s,
     c_s, n_s, m_s) = refs[o0 + k_out + k_dec_out:]
    tt = x_ref.shape[1]
    nsub = tt // A_CHUNK
    t = pl.program_id(0)
    parity = lax.rem(t + 1, 2)
    pos1 = lax.rem(t + nt - 1, nt)
    pos2 = lax.rem(t + 2 * nt - 2, nt)

    @pl.when(t == 0)
    def _():
        h_s[...] = jnp.zeros(h_s.shape, BF16)
        u_s[...] = jnp.zeros(u_s.shape, F32)
        xc_s[0] = jnp.zeros(xc_s.shape[1:], F32)
        opre_s[0] = jnp.zeros(opre_s.shape[1:], F32)
        zg_s[0] = jnp.zeros(zg_s.shape[1:], F32)
        qm_s[0] = jnp.zeros(qm_s.shape[1:], BF16)
        zm_s[0] = jnp.zeros(zm_s.shape[1:], F32)
        qkv_s[0] = jnp.zeros(qkv_s.shape[1:], BF16)
        kt_s[0] = jnp.zeros(kt_s.shape[1:], BF16)
        gates_s[0] = jnp.zeros(gates_s.shape[1:], F32)

    @pl.when(pos1 == 0)
    def _():
        u_s[0:8, :] = jnp.zeros((8, A_INNER), F32)

    @pl.when(pos2 == 0)
    def _():
        c_s[...] = jnp.zeros(c_s.shape, F32)
        n_s[...] = jnp.zeros(n_s.shape, F32)
        m_s[...] = jnp.zeros(m_s.shape, F32)

    row = lax.broadcasted_iota(jnp.int32, (A_CHUNK, A_CHUNK), 0)
    col = lax.broadcasted_iota(jnp.int32, (A_CHUNK, A_CHUNK), 1)
    causal = col <= row
    ghn = ghn_ref[...]
    skp = skip_ref[...]
    mk = mkv_ref[0, :, 0:M_WIDTH]
    mv = mkv_ref[0, :, M_WIDTH:2 * M_WIDTH]

    def stage2(pslot):
        ym = _mem_attention(qm_s[pslot], mk, mv) * _silu(zm_s[pslot])
        ymix_s[:, A_INNER:A_INNER + M_WIDTH] = ym.astype(BF16)
        yield
        for c in range(nsub):
            rs = slice(c * A_CHUNK, (c + 1) * A_CHUNK)
            hs = []
            yield from _mlstm_chunk(rs, gates_s[pslot, rs, :], qkv_s.at[pslot], kt_s.at[pslot, :, c],
                                    c_s, n_s, m_s, causal, hs)
            parts = []
            for h in range(A_HEADS):
                v = _sigmoid(opre_s[pslot, rs, h * A_HDIM:(h + 1) * A_HDIM]) * hs[h]
                mu = jnp.mean(v, axis=-1, keepdims=True)
                var = jnp.mean(jnp.square(v - mu), axis=-1, keepdims=True)
                parts.append((v - mu) * lax.rsqrt(var + EPS))
            hn = jnp.concatenate(parts, axis=-1) * ghn
            y = hn + skp * xc_s[pslot, rs, :]
            ymix_s[rs, 0:A_INNER] = (y * _silu(zg_s[pslot, rs, :])).astype(BF16)
            yield

    def stage1(slot):
        hb = h_s[...]
        u_s[8:8 + tt, :] = _dot(hb, win_ref[:, 0:A_INNER])
        yield
        cw = convw_ref[...]
        cb = convb_ref[...]
        for c in range(nsub):
            r0 = c * A_CHUNK
            blk = u_s[r0:r0 + A_CHUNK + 8, :]
            xc = cb + pltpu.roll(blk, 3, 0)[8:, :] * cw[0:1, :]
            xc = xc + pltpu.roll(blk, 2, 0)[8:, :] * cw[1:2, :]
            xc = xc + pltpu.roll(blk, 1, 0)[8:, :] * cw[2:3, :]
            xc = xc + blk[8:, :] * cw[3:4, :]
            xc_s[slot, r0:r0 + A_CHUNK, :] = _silu(xc)
        opre_s[slot] = _dot(hb, win_ref[:, A_INNER:2 * A_INNER])
        yield
        zg_s[slot] = _dot(hb, win_ref[:, 2 * A_INNER:3 * A_INNER])
        yield
        qm_s[slot] = _dot(hb, win_ref[:, 3 * A_INNER:3 * A_INNER + M_WIDTH]).astype(BF16)
        zm_s[slot] = _dot(hb, win_ref[:, 3 * A_INNER + M_WIDTH:3 * A_INNER + 2 * M_WIDTH])
        yield
        for h in range(A_HEADS):
            sl = slice(h * A_HDIM, (h + 1) * A_HDIM)
            xh = xc_s[slot, :, sl].astype(BF16)
            uh = u_s[8:8 + tt, sl].astype(BF16)
            base = h * 3 * A_HDIM
            qkv_s[slot, :, base:base + A_HDIM] = _dot(xh, wq_ref[h]).astype(BF16)
            kf = _dot(xh, wk_ref[h]) * (A_HDIM ** -0.5)
            qkv_s[slot, :, base + A_HDIM:base + 2 * A_HDIM] = kf.astype(BF16)
            qkv_s[slot, :, base + 2 * A_HDIM:base + 3 * A_HDIM] = _dot(uh, wv_ref[h]).astype(BF16)
            kt = kf.T.astype(BF16)
            for c in range(nsub):
                kt_s[slot, h, c] = kt[:, c * A_CHUNK:(c + 1) * A_CHUNK]
            yield
        gates_s[slot] = _dot(qkv_s[slot], wif_ref[...]) + bif_ref[...]
        yield

    def step(slot):
        if n_dec_seq:
            first = slot == 1
            heads = (0, 1) if first else (2, 3)
            _dec_mlstm_body(jnp.minimum(t // 2, n_dec_seq - 1), heads, first, *dec_in, *dec_out)
        pending = [stage2(1 - slot), stage1(slot)]
        while pending:
            for gen in list(pending):
                try:
                    next(gen)
                except StopIteration:
                    pending.remove(gen)
        h_next = (_rms_scale(x_ref[0]) * gpre_ref[...]).astype(BF16)
        out = _dot(ymix_s[...], wout_ref[...])
        h_s[...] = h_next
        x1_ref[0] = xp_ref[0] + _rms_scale(out) * gpost_ref[...]

    for s in range(2):
        pl.when(parity == s)(functools.partial(step, s))

    @pl.when(jnp.logical_and(pos1 == nt - 1, t > 0))
    def _():
        conv_out[0, 0] = u_s[tt + 5:tt + 8, :]

    u_s[0:8, :] = u_s[tt:tt + 8, :]

    @pl.when(jnp.logical_and(pos2 == nt - 1, t > 1))
    def _():
        for h in range(A_HEADS):
            c_out[0, 0, h] = c_s[h].T
        n_out[0, 0] = n_s[0:A_HEADS, :]
        m_out[0] = m_s[...]


def _layer0_prompt_pipelined(x, g_pre, w_in, conv_w, conv_b, wq, wk, wv, wif, bif, ghn, skip, mkv_bf,
                             w_out, g_post, dec_job=None):
    b, s, _ = x.shape
    tt = min(L0_TILE, s)
    nt = s // tt
    ntiles = b * nt
    a_in = w_in.shape[1]
    nsub = tt // A_CHUNK

    def cur(t):
        t1 = jnp.minimum(t, ntiles - 1)
        return (t1 // nt, t1 % nt, 0)

    def prev(t):
        t2 = jnp.maximum(t - 2, 0)
        return (t2 // nt, t2 % nt, 0)

    def prev_b(t):
        return jnp.maximum(t - 2, 0) // nt

    in_specs = [
        pl.BlockSpec((1, tt, D_MODEL), cur),
        pl.BlockSpec((1, tt, D_MODEL), prev),
        _const_spec((1, D_MODEL)),
        _const_spec((D_MODEL, a_in)),
        _const_spec((CONV_W, A_INNER)),
        _const_spec((1, A_INNER)),
        _const_spec((A_HEADS, A_HDIM, A_HDIM)),
        _const_spec((A_HEADS, A_HDIM, A_HDIM)),
        _const_spec((A_HEADS, A_HDIM, A_HDIM)),
        _const_spec((3 * A_INNER, LANES)),
        _const_spec((1, LANES)),
        _const_spec((1, A_INNER)),
        _const_spec((1, A_INNER)),
        pl.BlockSpec((1, N_MEM, 2 * M_WIDTH), lambda t: (prev_b(t), 0, 0)),
        _const_spec((A_INNER + M_WIDTH, D_MODEL)),
        _const_spec((1, D_MODEL)),
    ]
    out_specs = [
        pl.BlockSpec((1, tt, D_MODEL), prev),
        pl.BlockSpec((1, 1, CONV_W - 1, A_INNER), lambda t: (0, prev_b(t), 0, 0)),
        pl.BlockSpec((1, 1, A_HEADS, A_HDIM, A_HDIM), lambda t: (0, prev_b(t), 0, 0, 0)),
        pl.BlockSpec((1, 1, A_HEADS, A_HDIM), lambda t: (0, prev_b(t), 0, 0)),
        pl.BlockSpec((1, 8, LANES), lambda t: (prev_b(t), 0, 0)),
    ]
    out_shape = [
        jax.ShapeDtypeStruct((b, s, D_MODEL), F32),
        jax.ShapeDtypeStruct((1, b, CONV_W - 1, A_INNER), F32),
        jax.ShapeDtypeStruct((1, b, A_HEADS, A_HDIM, A_HDIM), F32),
        jax.ShapeDtypeStruct((1, b, A_HEADS, A_HDIM), F32),
        jax.ShapeDtypeStruct((b, 8, LANES), F32),
    ]
    scratch = [
        pltpu.VMEM((tt, D_MODEL), BF16),
        pltpu.VMEM((tt + 8, A_INNER), F32),
        pltpu.VMEM((tt, A_INNER + M_WIDTH), BF16),
        pltpu.VMEM((2, tt, A_INNER), F32),
        pltpu.VMEM((2, tt, A_INNER), F32),
        pltpu.VMEM((2, tt, A_INNER), F32),
        pltpu.VMEM((2, tt, M_WIDTH), BF16),
        pltpu.VMEM((2, tt, M_WIDTH), F32),
        pltpu.VMEM((2, tt, 3 * A_INNER), BF16),
        pltpu.VMEM((2, A_HEADS, nsub, A_HDIM, A_CHUNK), BF16),
        pltpu.VMEM((2, tt, LANES), F32),
        pltpu.VMEM((A_HEADS, A_HDIM, A_HDIM), F32),
        pltpu.VMEM((8, A_HDIM), F32),
        pltpu.VMEM((8, LANES), F32),
    ]
    dec_args, dec_specs, dec_out_specs, dec_out_shapes, n_dec_seq = [], [], [], [], 0
    if dec_job is not None:
        n_dec_seq = dec_job[0].shape[0]
        dec_args, dec_specs, dec_out_specs, dec_out_shapes = _dec_mlstm_specs(
            *dec_job, seq_index=lambda t: jnp.minimum(t // 2, n_dec_seq - 1))
    return pl.pallas_call(
        functools.partial(_l0p_kernel, nt, n_dec_seq),
        grid=(ntiles + 2,),
        in_specs=in_specs + dec_specs,
        out_specs=out_specs + dec_out_specs,
        out_shape=out_shape + dec_out_shapes,
        scratch_shapes=scratch,
        compiler_params=pltpu.CompilerParams(
            dimension_semantics=("arbitrary",), vmem_limit_bytes=VMEM_LIMIT),
        name="layer0_prompt",
    )(x, x, g_pre, w_in, conv_w, conv_b, wq, wk, wv, wif, bif, ghn, skip, mkv_bf, w_out, g_post,
      *dec_args)


def _rope_cols(x, cos, sin_signed):
    outs = []
    for cblk in range(x.shape[1] // B_HDIM):
        xb = x[:, cblk * B_HDIM:(cblk + 1) * B_HDIM]
        outs.append(xb * cos + pltpu.roll(xb, B_HDIM // 2, 1) * sin_signed)
    return jnp.concatenate(outs, axis=-1)


def _l1a_kernel(n_dec_in, *refs):
    n_in, n_out = 7, 15
    x_ref, gkv_ref, gpre_ref, wkv_ref, win_ref, cos_ref, sin_ref = refs[0:n_in]
    dec_in = refs[n_in:n_in + n_dec_in]
    (q0_ref, q1_ref, q2_ref, k0_ref, k1_ref, k2_ref, v0_ref, v1_ref, v2_ref,
     zg_ref, qm_ref, zm_ref, w0_ref, w1_ref, w2_ref) = refs[n_in + n_dec_in:n_in + n_dec_in + n_out]
    dec_out = refs[n_in + n_dec_in + n_out:]
    tt = x_ref.shape[1]
    xn = _rms_scale(x_ref[0])
    hk = (xn * gkv_ref[...]).astype(BF16)
    hq = (xn * gpre_ref[...]).astype(BF16)
    cos = cos_ref[...]
    sin = sin_ref[...]
    q_refs = (q0_ref, q1_ref, q2_ref)
    k_refs = (k0_ref, k1_ref, k2_ref)
    v_refs = (v0_ref, v1_ref, v2_ref)
    w_refs = (w0_ref, w1_ref, w2_ref)
    for g in (2, 1, 0):
        d = B_GROUPS[g][1]
        kf = _rope_cols(_dot(hk, wkv_ref[:, g * 2 * B_WIDTH:g * 2 * B_WIDTH + B_WIDTH]), cos, sin)
        vf = _dot(hk, wkv_ref[:, g * 2 * B_WIDTH + B_WIDTH:(g + 1) * 2 * B_WIDTH])
        qf = _rope_cols(_dot(hq, win_ref[:, g * B_WIDTH:(g + 1) * B_WIDTH]), cos, sin)
        wr = w_refs[g]
        wrows = wr.shape[1]
        wr[0] = _rows_to_kv_heads(kf[tt - wrows:, :], vf[tt - wrows:, :])
        for val, ref in ((qf.astype(BF16), q_refs[g]), (kf.astype(BF16), k_refs[g]), (vf.astype(BF16), v_refs[g])):
            if d == 1:
                ref[0, 0] = val
            else:
                ref[0] = jnp.swapaxes(val.reshape(tt // d, d, val.shape[1]), 0, 1)
        if g == 2 and n_dec_in:
            _dec_attn_kernel(*dec_in, *dec_out)
    qoff = N_GROUPS * B_WIDTH
    zg_ref[0] = _dot(hq, win_ref[:, qoff:qoff + B_WIDTH]).astype(BF16)
    qm_ref[0] = _dot(hq, win_ref[:, qoff + B_WIDTH:qoff + B_WIDTH + M_WIDTH]).astype(BF16)
    zm_ref[0] = _dot(hq, win_ref[:, qoff + B_WIDTH + M_WIDTH:qoff + B_WIDTH + 2 * M_WIDTH]).astype(BF16)


def _layer1_proj_prompt(x1, g_kv, g_pre, wkv, win, cos_t, sin_t, dec_job=None):
    b, s, _ = x1.shape
    tt = min(TOK_TILE, s)
    nt = s // tt
    tile = lambda bb, i: (bb, i, 0)
    in_specs = [
        pl.BlockSpec((1, tt, D_MODEL), tile),
        _const_spec((1, D_MODEL)),
        _const_spec((1, D_MODEL)),
        _const_spec(wkv.shape),
        _const_spec(win.shape),
        pl.BlockSpec((tt, B_HDIM), lambda bb, i: (i, 0)),
        pl.BlockSpec((tt, B_HDIM), lambda bb, i: (i, 0)),
    ]
    qkv_specs, qkv_shapes = [], []
    for _ in range(3):
        for (_, d) in B_GROUPS:
            qkv_specs.append(pl.BlockSpec((1, d, tt // d, B_WIDTH), lambda bb, i: (bb, 0, i, 0)))
            qkv_shapes.append(jax.ShapeDtypeStruct((b, d, s // d, B_WIDTH), BF16))
    gate_specs = [pl.BlockSpec((1, tt, B_WIDTH), tile)] * 3
    gate_shapes = [jax.ShapeDtypeStruct((b, s, B_WIDTH), BF16)] * 3
    win_specs, win_shapes = [], []
    for (w, _) in B_GROUPS:
        wr = min(w, s)
        rows = min(wr, tt)
        nblk = wr // rows
        win_specs.append(pl.BlockSpec(
            (1, rows, 2 * B_HEADS, B_HDIM),
            functools.partial(lambda bb, i, nb: (bb, jnp.maximum(i - (nt - nb), 0), 0, 0), nb=nblk)))
        win_shapes.append(jax.ShapeDtypeStruct((b, wr, 2 * B_HEADS, B_HDIM), F32))
    dec_args, dec_specs, dec_out_specs, dec_out_shapes = [], [], [], []
    if dec_job is not None:
        dec_args, dec_specs, dec_out_specs, dec_out_shapes = _dec_attn_specs(
            *dec_job, seq_index=lambda bb, i: bb * nt + i)
    return pl.pallas_call(
        functools.partial(_l1a_kernel, len(dec_args)),
        grid=(b, nt),
        in_specs=in_specs + dec_specs,
        out_specs=qkv_specs + gate_specs + win_specs + dec_out_specs,
        out_shape=qkv_shapes + gate_shapes + win_shapes + dec_out_shapes,
        compiler_params=pltpu.CompilerParams(
            dimension_semantics=("arbitrary", "arbitrary"), vmem_limit_bytes=VMEM_LIMIT),
        name="layer1_proj_prompt",
    )(x1, g_kv, g_pre, wkv, win, cos_t, sin_t, *dec_args)


def _cols_to_lanes(cols):
    t = cols[0].shape[0]
    lane = lax.broadcasted_iota(jnp.int32, (t, LANES), 1)
    acc = jnp.zeros((t, LANES), F32)
    for h, cvec in enumerate(cols):
        acc = jnp.where(lane == h, cvec, acc)
    return acc


def _band_attn_kernel(q_ref, kc_ref, kp_ref, vc_ref, vp_ref, o_ref, lse_ref):
    nres, tq = q_ref.shape[1:3]
    nsb = tq // ATT_BLK
    j = pl.program_id(2)
    row = lax.broadcasted_iota(jnp.int32, (ATT_BLK, 2 * ATT_BLK), 0)
    col = lax.broadcasted_iota(jnp.int32, (ATT_BLK, 2 * ATT_BLK), 1)
    band = jnp.logical_and(col >= row, col <= row + ATT_BLK)
    first_pen = jnp.where(col < ATT_BLK, jnp.where(j > 0, 0.0, -jnp.inf), 0.0)
    scale = B_HDIM ** -0.5
    blocks = [(r, sb) for r in range(nres) for sb in range(nsb)]
    for g0 in range(0, len(blocks), ATT_GROUP):
        grp = blocks[g0:g0 + ATT_GROUP]
        qs, ks, vs, pens = [], [], [], []
        for r, sb in grp:
            rs = slice(sb * ATT_BLK, (sb + 1) * ATT_BLK)
            ps = slice((sb - 1) * ATT_BLK, sb * ATT_BLK)
            for h in range(B_HEADS):
                hs = slice(h * B_HDIM, (h + 1) * B_HDIM)
                qs.append(q_ref[0, r, rs, hs])
                kp = kp_ref[0, r, :, hs] if sb == 0 else kc_ref[0, r, ps, hs]
                vp = vp_ref[0, r, :, hs] if sb == 0 else vc_ref[0, r, ps, hs]
                ks.append(jnp.concatenate([kp, kc_ref[0, r, rs, hs]], axis=0))
                vs.append(jnp.concatenate([vp, vc_ref[0, r, rs, hs]], axis=0))
                pens.append(sb == 0)
        q3 = jnp.stack(qs)
        k3 = jnp.stack(ks)
        v3 = jnp.stack(vs)
        s = jnp.einsum('uqd,ukd->uqk', q3, k3, preferred_element_type=F32)
        s = jnp.stack([s[u] + first_pen if pens[u] else s[u] for u in range(len(pens))])
        s = jnp.where(band[None], s, -jnp.inf)
        mx = jnp.max(s, axis=-1, keepdims=True)
        p = jnp.exp2((s - mx) * (scale * LOG2E))
        l = jnp.sum(p, axis=-1, keepdims=True)
        o = jnp.einsum('uqk,ukd->uqd', p.astype(BF16), v3, preferred_element_type=F32) / l
        lse = mx * scale + jnp.log(l)
        for i, (r, sb) in enumerate(grp):
            rs = slice(sb * ATT_BLK, (sb + 1) * ATT_BLK)
            for h in range(B_HEADS):
                o_ref[0, r, rs, h * B_HDIM:(h + 1) * B_HDIM] = o[i * B_HEADS + h].astype(BF16)
            lse_ref[0, r, rs, :] = _cols_to_lanes([lse[i * B_HEADS + h] for h in range(B_HEADS)])


def _band_attention(q, k, v):
    b, d, ls, _ = q.shape
    tq = min(ATT_TILE, ls)
    nj = ls // tq
    ratio = tq // ATT_BLK
    nres = min(d, ATT_TILE // tq)
    cur = lambda bb, r, j: (bb, r, j, 0)
    prev = lambda bb, r, j: (bb, r, jnp.maximum(j * ratio - 1, 0), 0)
    return pl.pallas_call(
        _band_attn_kernel,
        grid=(b, d // nres, nj),
        in_specs=[pl.BlockSpec((1, nres, tq, B_WIDTH), cur),
                  pl.BlockSpec((1, nres, tq, B_WIDTH), cur),
                  pl.BlockSpec((1, nres, ATT_BLK, B_WIDTH), prev),
                  pl.BlockSpec((1, nres, tq, B_WIDTH), cur),
                  pl.BlockSpec((1, nres, ATT_BLK, B_WIDTH), prev)],
        out_specs=[pl.BlockSpec((1, nres, tq, B_WIDTH), cur),
                   pl.BlockSpec((1, nres, tq, LANES), cur)],
        out_shape=[jax.ShapeDtypeStruct((b, d, ls, B_WIDTH), BF16),
                   jax.ShapeDtypeStruct((b, d, ls, LANES), F32)],
        compiler_params=pltpu.CompilerParams(
            dimension_semantics=("arbitrary", "arbitrary", "arbitrary"), vmem_limit_bytes=VMEM_LIMIT),
        name="band_attention_d%d" % d,
    )(q, k, k, v, v)


def _unpermute(ref):
    d, rows, width = ref.shape[1:]
    if d == 1:
        return ref[0, 0]
    return jnp.swapaxes(ref[0], 0, 1).reshape(d * rows, width)


def _l1c_kernel(x_ref, o0_ref, o1_ref, o2_ref, l0_ref, l1_ref, l2_ref, zg_ref, qm_ref, zm_ref,
                mkv_ref, wout_ref, gpost_ref, y_ref):
    tt = x_ref.shape[1]
    o_refs = (o0_ref, o1_ref, o2_ref)
    l_refs = (l0_ref, l1_ref, l2_ref)
    outs, lses = [], []
    for g, (_, d) in enumerate(B_GROUPS):
        outs.append(_unpermute(o_refs[g]))
        lses.append(_unpermute(l_refs[g])[:, 0:B_HEADS])
    mx = jnp.maximum(jnp.maximum(lses[0], lses[1]), lses[2])
    es = [jnp.exp(l - mx) for l in lses]
    tot = es[0] + es[1] + es[2]
    ws = [(e / tot).astype(BF16) for e in es]
    parts = []
    for h in range(B_HEADS):
        hs = slice(h * B_HDIM, (h + 1) * B_HDIM)
        acc = ws[0][:, h:h + 1] * outs[0][:, hs]
        acc = acc + ws[1][:, h:h + 1] * outs[1][:, hs]
        acc = acc + ws[2][:, h:h + 1] * outs[2][:, hs]
        parts.append(acc)
    ydil = jnp.concatenate(parts, axis=-1)
    ymix = (ydil.astype(F32) * _silu(zg_ref[0].astype(F32))).astype(BF16)
    mk = mkv_ref[0, :, 0:M_WIDTH]
    mv = mkv_ref[0, :, M_WIDTH:2 * M_WIDTH]
    ym = (_mem_attention(qm_ref[0], mk, mv) * _silu(zm_ref[0].astype(F32))).astype(BF16)
    out = _dot(ymix, wout_ref[0:B_WIDTH, :]) + _dot(ym, wout_ref[B_WIDTH:B_WIDTH + M_WIDTH, :])
    y_ref[0] = x_ref[0] + _rms_scale(out) * gpost_ref[...]


def _layer1_out_prompt(x1, os_, ls_, zg, qm, zm, mkv_bf, w_out, g_post):
    b, s, _ = x1.shape
    tt = min(L1OUT_TILE, s)
    nt = s // tt
    tile = lambda bb, i: (bb, i, 0)
    perm = lambda bb, i: (bb, 0, i, 0)
    in_specs = [pl.BlockSpec((1, tt, D_MODEL), tile)]
    for width in (B_WIDTH, LANES):
        for (_, d) in B_GROUPS:
            in_specs.append(pl.BlockSpec((1, d, tt // d, width), perm))
    in_specs += [pl.BlockSpec((1, tt, B_WIDTH), tile)] * 3
    in_specs += [pl.BlockSpec((1, N_MEM, 2 * M_WIDTH), lambda bb, i: (bb, 0, 0)),
                 _const_spec(w_out.shape), _const_spec((1, D_MODEL))]
    return pl.pallas_call(
        _l1c_kernel,
        grid=(b, nt),
        in_specs=in_specs,
        out_specs=pl.BlockSpec((1, tt, D_MODEL), tile),
        out_shape=jax.ShapeDtypeStruct((b, s, D_MODEL), F32),
        compiler_params=pltpu.CompilerParams(
            dimension_semantics=("arbitrary", "arbitrary"), vmem_limit_bytes=VMEM_LIMIT),
        name="layer1_out_prompt",
    )(x1, *os_, *ls_, zg, qm, zm, mkv_bf, w_out, g_post)


def _rope_tables(pos):
    half = B_HDIM // 2
    inv = ROPE_THETA ** (-jnp.arange(half, dtype=F32) / half)
    ang = pos[:, None] * inv[None, :]
    cos = jnp.cos(ang)
    sin = jnp.sin(ang)
    return jnp.concatenate([cos, cos], axis=-1), jnp.concatenate([-sin, sin], axis=-1)


def _prompt_group(x_prompt, mem_prompt, p, sample=None):
    b, s, _ = x_prompt.shape
    memkv_f, memkv_b = _memkv(mem_prompt.reshape(b * N_MEM, D_MODEL), p['w_mkv'])
    depth = memkv_f.shape[0]
    memkv_b = memkv_b.reshape(depth, b, N_MEM, 2 * M_WIDTH)
    job0 = sample.mlstm_job() if sample is not None else None
    if job0 is not None and b * (s // min(L0_TILE, s)) + 2 < 2 * job0[0].shape[0]:
        job0 = None
    outs0 = _layer0_prompt_pipelined(
        x_prompt, p['g_pre'][0:1], p['w_in_a'][0], p['conv_w_a'][0], p['conv_b_a'], p['w_q_a'][0],
        p['w_k_a'][0], p['w_v_a'][0], p['w_if_a'], p['b_if_a'],
        p['g_hn_a'], p['skip_a'], memkv_b[0], p['w_out_a'][0], p['g_post'][0:1], job0)
    x1, conv_p, c_p, n_p, m_pad = outs0[0:5]
    if sample is not None:
        sample.after_mlstm(outs0[5:10] if job0 is not None else _dec_mlstm(*sample.mlstm_job()))
    cos_t, sin_t = _rope_tables(jnp.arange(s, dtype=F32))
    job1 = sample.attn_job if sample is not None else None
    if job1 is not None and b * (s // min(TOK_TILE, s)) != job1[0].shape[0]:
        job1 = None
    outs = _layer1_proj_prompt(x1, p['g_kv'], p['g_pre'][1:2], p['w_kv_b'], p['w_in_b'][0], cos_t, sin_t,
                               job1)
    if sample is not None:
        sample.after_attn(tuple(outs[15:17]) if job1 is not None else _dec_attn(*sample.attn_job))
    qs, ks, vs = outs[0:3], outs[3:6], outs[6:9]
    zg, qm, zm = outs[9:12]
    wins = outs[12:15]
    os_, ls_ = [], []
    for g in range(N_GROUPS):
        o, l = _band_attention(qs[g], ks[g], vs[g])
        os_.append(o)
        ls_.append(l)
    y = _layer1_out_prompt(x1, os_, ls_, zg, qm, zm, memkv_b[1], p['w_out_b'][0], p['g_post'][1:2])
    m_p = m_pad[:, 0:A_HEADS, 0][None]
    wins = [w.reshape(b, w.shape[1], 2, B_HEADS, B_HDIM) for w in wins]
    memkv_p = memkv_f.reshape(depth, b, N_MEM, 2, M_HEADS, M_HDIM)
    return y, conv_p, c_p, n_p, m_p, wins, memkv_p


def _cast_kernel(*refs):
    n = len(refs) // 2
    for src, dst in zip(refs[:n], refs[n:]):
        dst[...] = src[...].astype(BF16)


def _cast_bf16(arrays):
    flat = [a.reshape(-1, a.shape[-1]) for a in arrays]
    specs = [pl.BlockSpec((f.shape[0] // CAST_STEPS, f.shape[1]), lambda i: (i, 0)) for f in flat]
    outs = pl.pallas_call(
        _cast_kernel,
        grid=(CAST_STEPS,),
        in_specs=specs,
        out_specs=specs,
        out_shape=[jax.ShapeDtypeStruct(f.shape, BF16) for f in flat],
        compiler_params=pltpu.CompilerParams(dimension_semantics=("arbitrary",), vmem_limit_bytes=VMEM_LIMIT),
        name="cast_weights",
    )(*flat)
    return [o.reshape(a.shape) for o, a in zip(outs, arrays)]


def _prep_params(g_pre, g_post, w_in_a, conv_w_a, conv_b_a, w_q_a, w_k_a, w_v_a, w_if_a, b_if_a,
                 g_hn_a, skip_a, w_out_a, g_kv, w_kv_b, w_in_b, w_out_b, w_mkv):
    wif = jnp.pad(w_if_a[0], ((0, 0), (0, LANES - 2 * A_HEADS))).astype(BF16)
    bif = jnp.pad(b_if_a[0], (0, LANES - 2 * A_HEADS))[None, :]
    w_in_a, w_q_a, w_k_a, w_v_a, w_out_a, w_kv_b, w_in_b, w_out_b, w_mkv = _cast_bf16(
        [w_in_a, w_q_a, w_k_a, w_v_a, w_out_a, w_kv_b, w_in_b, w_out_b, w_mkv])
    return {
        'g_pre': g_pre, 'g_post': g_post,
        'w_in_a': w_in_a, 'conv_w_a': conv_w_a, 'conv_b_a': conv_b_a,
        'w_q_a': w_q_a, 'w_k_a': w_k_a, 'w_v_a': w_v_a,
        'w_if_a': wif, 'b_if_a': bif, 'g_hn_a': g_hn_a, 'skip_a': skip_a,
        'w_out_a': w_out_a, 'g_kv': g_kv[None, :], 'w_kv_b': w_kv_b,
        'w_in_b': w_in_b, 'w_out_b': w_out_b, 'w_mkv': w_mkv,
    }


def _dec_l0_proj_kernel(x_ref, gpre_ref, win_ref, cst_ref, convw_ref, convb_ref, wq_ref, wk_ref, wv_ref,
                        wif_ref, bif_ref,
                        q_ref, k_ref, v_ref, gates_ref, xc_ref, opre_ref, zg_ref, qm_ref, zm_ref, cnew_ref):
    h = (_rms_scale(x_ref[:, 0, :]) * gpre_ref[...]).astype(BF16)
    u = _dot(h, win_ref[:, 0:A_INNER])
    opre_ref[...] = _dot(h, win_ref[:, A_INNER:2 * A_INNER])
    zg_ref[...] = _dot(h, win_ref[:, 2 * A_INNER:3 * A_INNER])
    qm_ref[...] = _rows_to_heads(_dot(h, win_ref[:, 3 * A_INNER:3 * A_INNER + M_WIDTH]))
    zm_ref[...] = _dot(h, win_ref[:, 3 * A_INNER + M_WIDTH:3 * A_INNER + 2 * M_WIDTH])
    cw = convw_ref[...]
    xc = convb_ref[...] + cst_ref[0, :, 0, :] * cw[0:1, :]
    xc = xc + cst_ref[0, :, 1, :] * cw[1:2, :]
    xc = xc + cst_ref[0, :, 2, :] * cw[2:3, :]
    xc = xc + u * cw[3:4, :]
    xc = _silu(xc)
    xc_ref[...] = xc
    cnew_ref[0, :, 0, :] = cst_ref[0, :, 1, :]
    cnew_ref[0, :, 1, :] = cst_ref[0, :, 2, :]
    cnew_ref[0, :, 2, :] = u
    qs, ks, vs, cat = [], [], [], []
    for hd in range(A_HEADS):
        sl = slice(hd * A_HDIM, (hd + 1) * A_HDIM)
        xh = xc[:, sl].astype(BF16)
        qh = _dot(xh, wq_ref[hd])
        kh = _dot(xh, wk_ref[hd]) * (A_HDIM ** -0.5)
        vh = _dot(u[:, sl].astype(BF16), wv_ref[hd])
        qs.append(qh)
        ks.append(kh)
        vs.append(vh)
        cat += [qh.astype(BF16), kh.astype(BF16), vh.astype(BF16)]
    q_ref[...] = jnp.concatenate(qs, axis=-1)
    k_ref[...] = jnp.concatenate(ks, axis=-1)
    v_ref[...] = jnp.concatenate(vs, axis=-1)
    gates_ref[...] = _dot(jnp.concatenate(cat, axis=-1), wif_ref[...]) + bif_ref[...]


def _whole(shape):
    nd = len(shape)
    return pl.BlockSpec(shape, lambda *_: (0,) * nd)


def _dec_l0_proj(x, g_pre, w_in, cst, conv_w, conv_b, wq, wk, wv, wif, bif):
    nb = x.shape[0]
    args = (x, g_pre, w_in, cst, conv_w, conv_b, wq, wk, wv, wif, bif)
    f = lambda *s: jax.ShapeDtypeStruct(s, F32)
    out_shape = [f(nb, A_INNER), f(nb, A_INNER), f(nb, A_INNER), f(nb, LANES), f(nb, A_INNER), f(nb, A_INNER),
                 f(nb, A_INNER), f(nb, M_HEADS, M_HDIM), f(nb, M_WIDTH), f(1, nb, CONV_W - 1, A_INNER)]
    return pl.pallas_call(
        _dec_l0_proj_kernel,
        grid=(1,),
        in_specs=[_whole(a.shape) for a in args],
        out_specs=[_whole(o.shape) for o in out_shape],
        out_shape=out_shape,
        compiler_params=pltpu.CompilerParams(dimension_semantics=("arbitrary",), vmem_limit_bytes=VMEM_LIMIT),
        name="dec_l0_proj",
    )(*args)


def _row_to_col(row, eye):
    return jnp.sum(jnp.where(eye, row, 0.0), axis=-1, keepdims=True)


def _col_to_row(colv, eye):
    return jnp.sum(jnp.where(eye, colv, 0.0), axis=0, keepdims=True)


def _dec_mem_attention(q, kv_ref_view):
    kk = kv_ref_view[:, 0]
    vv = kv_ref_view[:, 1]
    s = jnp.sum(kk * (q * (M_HDIM ** -0.5))[None], axis=-1, keepdims=True)
    mx = jnp.max(s, axis=0, keepdims=True)
    p = jnp.exp(s - mx)
    return jnp.sum(p * vv, axis=0) / jnp.sum(p, axis=0)


def _dec_mlstm_body(b, heads, with_mem, q_ref, k_ref, v_ref, gates_ref, m_ref, c_ref, n_ref, qm_ref, kv_ref,
                    hs_ref, c_out, n_out, m_out, ym_ref):
    rb = pl.ds(b, 1)
    g = gates_ref[rb, :]
    mrow = m_ref[0, rb, :]
    r = lax.broadcasted_iota(jnp.int32, (A_HDIM, A_HDIM), 0)
    c = lax.broadcasted_iota(jnp.int32, (A_HDIM, A_HDIM), 1)
    eye = r == c
    sl = {h: slice(h * A_HDIM, (h + 1) * A_HDIM) for h in heads}
    qh = {h: q_ref[rb, sl[h]] for h in heads}
    kh = {h: k_ref[rb, sl[h]] for h in heads}
    vh = {h: v_ref[rb, sl[h]] for h in heads}
    c_old = {h: c_ref[0, 0, h] for h in heads}
    n_old = {h: n_ref[0, 0, h:h + 1, :] for h in heads}
    li = {h: g[:, h:h + 1] for h in heads}
    lf = {h: _log_sigmoid(g[:, 4 + h:5 + h]) for h in heads}
    m_old = {h: mrow[:, h:h + 1] for h in heads}
    cq = {h: jnp.sum(c_old[h] * qh[h], axis=-1, keepdims=True) for h in heads}
    v_col = {h: _row_to_col(vh[h], eye) for h in heads}
    nq = {h: jnp.sum(n_old[h] * qh[h], axis=-1, keepdims=True) for h in heads}
    qk = {h: jnp.sum(qh[h] * kh[h], axis=-1, keepdims=True) for h in heads}
    inter = {h: lf[h] + m_old[h] for h in heads}
    m_new = {h: jnp.maximum(inter[h], li[h]) for h in heads}
    ws = {h: jnp.exp(li[h] - m_new[h]) for h in heads}
    dec = {h: jnp.exp(inter[h] - m_new[h]) for h in heads}
    sc = {h: qk[h] * ws[h] for h in heads}
    den = {h: sc[h] + dec[h] * nq[h] for h in heads}
    h_col = {h: (sc[h] * v_col[h] + dec[h] * cq[h]) / jnp.maximum(jnp.abs(den[h]), jnp.exp(-m_new[h]))
             for h in heads}
    for h in heads:
        c_out[0, 0, h] = dec[h] * c_old[h] + (ws[h] * v_col[h]) * kh[h]
        n_out[0, 0, h:h + 1, :] = dec[h] * n_old[h] + ws[h] * kh[h]
        m_out[0, h:h + 1, :] = jnp.broadcast_to(m_new[h], (1, LANES))
    for h in heads:
        hs_ref[0, :, sl[h]] = _col_to_row(h_col[h], eye)
    if with_mem:
        ym_ref[0] = _dec_mem_attention(qm_ref[0], kv_ref.at[0, 0])


def _dec_mlstm_kernel(*refs):
    _dec_mlstm_body(pl.program_id(0), range(A_HEADS), True, *refs)


def _dec_mlstm_specs(q, k, v, gates, m_in, state_c, state_n, qm3, cache_mem_kv, seq_index):
    nb = q.shape[0]

    def at(*tail, lead=()):
        return lambda *idx: lead + (seq_index(*idx),) + tail

    in_specs = [_whole(q.shape), _whole(k.shape), _whole(v.shape), _whole(gates.shape), _whole(m_in.shape),
                pl.BlockSpec((1, 1, A_HEADS, A_HDIM, A_HDIM), at(0, 0, 0, lead=(0,))),
                pl.BlockSpec((1, 1, A_HEADS, A_HDIM), at(0, 0, lead=(0,))),
                pl.BlockSpec((1, M_HEADS, M_HDIM), at(0, 0)),
                pl.BlockSpec((1, 1, N_MEM, 2, M_HEADS, M_HDIM), at(0, 0, 0, 0, lead=(0,)))]
    out_specs = [pl.BlockSpec((1, 1, A_INNER), at(0, 0)),
                 pl.BlockSpec((1, 1, A_HEADS, A_HDIM, A_HDIM), at(0, 0, 0, lead=(0,))),
                 pl.BlockSpec((1, 1, A_HEADS, A_HDIM), at(0, 0, lead=(0,))),
                 pl.BlockSpec((1, A_HEADS, LANES), at(0, 0)),
                 pl.BlockSpec((1, M_HEADS, M_HDIM), at(0, 0))]
    out_shapes = [jax.ShapeDtypeStruct((nb, 1, A_INNER), F32),
                  jax.ShapeDtypeStruct(state_c.shape, F32),
                  jax.ShapeDtypeStruct(state_n.shape, F32),
                  jax.ShapeDtypeStruct((nb, A_HEADS, LANES), F32),
                  jax.ShapeDtypeStruct((nb, M_HEADS, M_HDIM), F32)]
    return [q, k, v, gates, m_in, state_c, state_n, qm3, cache_mem_kv], in_specs, out_specs, out_shapes


def _dec_mlstm(*job):
    args, in_specs, out_specs, out_shapes = _dec_mlstm_specs(*job, seq_index=lambda b: b)
    return pl.pallas_call(
        _dec_mlstm_kernel,
        grid=(args[0].shape[0],),
        in_specs=in_specs,
        out_specs=out_specs,
        out_shape=out_shapes,
        compiler_params=pltpu.CompilerParams(dimension_semantics=("arbitrary",), vmem_limit_bytes=VMEM_LIMIT),
        name="dec_mlstm",
    )(*args)


def _dec_mid_kernel(hs_ref, opre_ref, xc_ref, zg_ref, ym_ref, zm_ref, x_ref, ghn_ref, skip_ref, wout_ref,
                    gpost_ref, gkv_ref, gpre_ref, wkv_ref, win_ref, cos_ref, sin_ref,
                    x1_ref, q_ref, k_ref, v_ref, zg1_ref, qm1_ref, zm1_ref, win0_ref, win1_ref, win2_ref):
    hh = _sigmoid(opre_ref[...]) * hs_ref[:, 0, :]
    parts = []
    for h in range(A_HEADS):
        v = hh[:, h * A_HDIM:(h + 1) * A_HDIM]
        mu = jnp.mean(v, axis=-1, keepdims=True)
        var = jnp.mean(jnp.square(v - mu), axis=-1, keepdims=True)
        parts.append((v - mu) * lax.rsqrt(var + EPS))
    y = jnp.concatenate(parts, axis=-1) * ghn_ref[...] + skip_ref[...] * xc_ref[...]
    ymix = (y * _silu(zg_ref[...])).astype(BF16)
    ym = (_heads_to_rows(ym_ref) * _silu(zm_ref[...])).astype(BF16)
    out = _dot(ymix, wout_ref[0:A_INNER, :]) + _dot(ym, wout_ref[A_INNER:A_INNER + M_WIDTH, :])
    x1 = x_ref[:, 0, :] + _rms_scale(out) * gpost_ref[...]
    x1_ref[...] = x1
    xn = _rms_scale(x1)
    hk = (xn * gkv_ref[...]).astype(BF16)
    hq = (xn * gpre_ref[...]).astype(BF16)
    cos = cos_ref[...]
    sin = sin_ref[...]
    ks, vs = [], []
    for g in range(N_GROUPS):
        ks.append(_rope_cols(_dot(hk, wkv_ref[:, g * 2 * B_WIDTH:g * 2 * B_WIDTH + B_WIDTH]), cos, sin))
        vs.append(_dot(hk, wkv_ref[:, g * 2 * B_WIDTH + B_WIDTH:(g + 1) * 2 * B_WIDTH]))
    k_ref[...] = _rows_to_heads(jnp.concatenate(ks, axis=-1))
    v_ref[...] = _rows_to_heads(jnp.concatenate(vs, axis=-1))
    for g, wref in enumerate((win0_ref, win1_ref, win2_ref)):
        wref[...] = _rows_to_kv_heads(ks[g], vs[g])
    qoff = N_GROUPS * B_WIDTH
    q_ref[...] = _rows_to_heads(_rope_cols(_dot(hq, win_ref[:, 0:qoff]), cos, sin))
    zg1_ref[...] = _dot(hq, win_ref[:, qoff:qoff + B_WIDTH])
    qm1_ref[...] = _rows_to_heads(_dot(hq, win_ref[:, qoff + B_WIDTH:qoff + B_WIDTH + M_WIDTH]))
    zm1_ref[...] = _dot(hq, win_ref[:, qoff + B_WIDTH + M_WIDTH:qoff + B_WIDTH + 2 * M_WIDTH])


def _dec_mid(hs, opre, xc, zg, ym, zm, x, ghn, skip, w_out, g_post, g_kv, g_pre, wkv, win, cos, sin):
    nb = x.shape[0]
    args = (hs, opre, xc, zg, ym, zm, x, ghn, skip, w_out, g_post, g_kv, g_pre, wkv, win, cos, sin)
    f = lambda *s: jax.ShapeDtypeStruct(s, F32)
    gh = N_GROUPS * B_HEADS
    out_shape = [f(nb, D_MODEL), f(nb, gh, B_HDIM), f(nb, gh, B_HDIM), f(nb, gh, B_HDIM),
                 f(nb, B_WIDTH), f(nb, M_HEADS, M_HDIM), f(nb, M_WIDTH)] + [f(nb, 2 * B_HEADS, B_HDIM)] * N_GROUPS
    return pl.pallas_call(
        _dec_mid_kernel,
        grid=(1,),
        in_specs=[_whole(a.shape) for a in args],
        out_specs=[_whole(o.shape) for o in out_shape],
        out_shape=out_shape,
        compiler_params=pltpu.CompilerParams(dimension_semantics=("arbitrary",), vmem_limit_bytes=VMEM_LIMIT),
        name="dec_mid",
    )(*args)


def _dec_attn_kernel(q_ref, kn_ref, vn_ref, w0_ref, w1_ref, w2_ref, qm_ref, kv_ref, ydil_ref, ym_ref):
    w_refs = (w0_ref, w1_ref, w2_ref)
    scale = B_HDIM ** -0.5
    groups = range(N_GROUPS)
    hsl = [slice(g * B_HEADS, (g + 1) * B_HEADS) for g in groups]
    q = [q_ref[0, hsl[g], :] * scale for g in groups]
    s_c = [jnp.sum(w_refs[g][0, :, 0] * q[g][None], axis=-1, keepdims=True) for g in groups]
    s_n = [jnp.sum(kn_ref[0, hsl[g], :] * q[g], axis=-1, keepdims=True) for g in groups]
    mxs = [jnp.maximum(jnp.max(s_c[g], axis=0), s_n[g]) for g in groups]
    p_c = [jnp.exp(s_c[g] - mxs[g][None]) for g in groups]
    p_n = [jnp.exp(s_n[g] - mxs[g]) for g in groups]
    ls = [jnp.sum(p_c[g], axis=0) + p_n[g] for g in groups]
    outs = [(jnp.sum(p_c[g] * w_refs[g][0, :, 1], axis=0) + p_n[g] * vn_ref[0, hsl[g], :]) / ls[g]
            for g in groups]
    lses = [mxs[g] + jnp.log(ls[g]) for g in groups]
    mx = jnp.maximum(jnp.maximum(lses[0], lses[1]), lses[2])
    es = [jnp.exp(l - mx) for l in lses]
    tot = es[0] + es[1] + es[2]
    ydil_ref[0] = (es[0] / tot) * outs[0] + (es[1] / tot) * outs[1] + (es[2] / tot) * outs[2]
    ym_ref[0] = _dec_mem_attention(qm_ref[0], kv_ref.at[0, 0])


def _dec_attn_specs(q4, kn4, vn4, cw0, cw1, cw2, qm3, cache_mem_kv, layer, seq_index):
    nb = q4.shape[0]
    rows = B_GROUPS[0][0]

    def at(*tail):
        return lambda *idx: (seq_index(*idx),) + tail

    win_specs = [pl.BlockSpec((1, rows, 2, B_HEADS, B_HDIM), at(0, 0, 0, 0)),
                 pl.BlockSpec((1, rows, None, 2, B_HEADS, B_HDIM), at(0, 0, 0, 0, 0)),
                 pl.BlockSpec((1, rows, None, 2, B_HEADS, B_HDIM), at(0, 0, 0, 0, 0))]
    in_specs = [pl.BlockSpec((1, N_GROUPS * B_HEADS, B_HDIM), at(0, 0))] * 3 + win_specs + [
        pl.BlockSpec((1, M_HEADS, M_HDIM), at(0, 0)),
        pl.BlockSpec((1, 1, N_MEM, 2, M_HEADS, M_HDIM), lambda *idx: (layer, seq_index(*idx), 0, 0, 0, 0))]
    out_specs = [pl.BlockSpec((1, B_HEADS, B_HDIM), at(0, 0)), pl.BlockSpec((1, M_HEADS, M_HDIM), at(0, 0))]
    out_shapes = [jax.ShapeDtypeStruct((nb, B_HEADS, B_HDIM), F32),
                  jax.ShapeDtypeStruct((nb, M_HEADS, M_HDIM), F32)]
    return [q4, kn4, vn4, cw0, cw1, cw2, qm3, cache_mem_kv], in_specs, out_specs, out_shapes


def _dec_attn(q4, kn4, vn4, cw0, cw1, cw2, qm3, cache_mem_kv, layer):
    args, in_specs, out_specs, out_shapes = _dec_attn_specs(
        q4, kn4, vn4, cw0, cw1, cw2, qm3, cache_mem_kv, layer, seq_index=lambda b: b)
    return pl.pallas_call(
        _dec_attn_kernel,
        grid=(q4.shape[0],),
        in_specs=in_specs,
        out_specs=out_specs,
        out_shape=out_shapes,
        compiler_params=pltpu.CompilerParams(dimension_semantics=("arbitrary",), vmem_limit_bytes=VMEM_LIMIT),
        name="dec_attn",
    )(*args)


def _dec_out_kernel(ydil_ref, zg_ref, ym_ref, zm_ref, x_ref, wout_ref, gpost_ref, y_ref):
    ymix = (_heads_to_rows(ydil_ref) * _silu(zg_ref[...])).astype(BF16)
    ym = (_heads_to_rows(ym_ref) * _silu(zm_ref[...])).astype(BF16)
    out = _dot(ymix, wout_ref[0:B_WIDTH, :]) + _dot(ym, wout_ref[B_WIDTH:B_WIDTH + M_WIDTH, :])
    y_ref[:, 0, :] = x_ref[...] + _rms_scale(out) * gpost_ref[...]


def _dec_out(ydil, zg, ym, zm, x1, w_out, g_post):
    args = (ydil, zg, ym, zm, x1, w_out, g_post)
    out_shape = (x1.shape[0], 1, x1.shape[1])
    return pl.pallas_call(
        _dec_out_kernel,
        grid=(1,),
        in_specs=[_whole(a.shape) for a in args],
        out_specs=_whole(out_shape),
        out_shape=jax.ShapeDtypeStruct(out_shape, F32),
        compiler_params=pltpu.CompilerParams(dimension_semantics=("arbitrary",), vmem_limit_bytes=VMEM_LIMIT),
        name="dec_out",
    )(*args)


class _SampleGroup:
    def __init__(self, x_sample, state_conv, state_c, state_n, state_m, cache_wins, cache_mem_kv, p):
        self.p = p
        self.nb = nb = x_sample.shape[0]
        self.cache_wins = cache_wins
        self.cache_mem_kv = cache_mem_kv
        self.x = x_sample
        q, k, v, gates, self.xc, self.opre, self.zg, qm3, self.zm, self.conv_s = _dec_l0_proj(
            x_sample, p['g_pre'][0:1], p['w_in_a'][0], state_conv, p['conv_w_a'][0], p['conv_b_a'],
            p['w_q_a'][0], p['w_k_a'][0], p['w_v_a'][0], p['w_if_a'], p['b_if_a'])
        self._mlstm_job = (q, k, v, gates, state_m, state_c, state_n, qm3, cache_mem_kv)
        self.attn_job = None

    def mlstm_job(self):
        return self._mlstm_job

    def after_mlstm(self, res):
        p, nb = self.p, self.nb
        hs, self.c_s, self.n_s, m_rows, ym0 = res
        self.m_s = m_rows[:, :, 0][None]
        cos, sin = _rope_tables(PAST_LEN + jnp.arange(1, dtype=F32))
        self.x1, qd, kn, vn, self.zg1, qm1, self.zm1, w0, w1, w2 = _dec_mid(
            hs, self.opre, self.xc, self.zg, ym0, self.zm, self.x,
            p['g_hn_a'], p['skip_a'], p['w_out_a'][0], p['g_post'][0:1], p['g_kv'], p['g_pre'][1:2],
            p['w_kv_b'], p['w_in_b'][0], cos, sin)
        cws = [self.cache_wins[0]]
        for g in (1, 2):
            w, d = B_GROUPS[g]
            cws.append(self.cache_wins[g].reshape(nb, w // d, d, 2, B_HEADS, B_HDIM))
        self.attn_job = (qd, kn, vn, cws[0], cws[1], cws[2], qm1, self.cache_mem_kv, 1)
        self.wins_s = [w.reshape(nb, 1, 2, B_HEADS, B_HDIM) for w in (w0, w1, w2)]

    def after_attn(self, res):
        p = self.p
        ydil, ym1 = res
        self.y = _dec_out(ydil, self.zg1, ym1, self.zm1, self.x1, p['w_out_b'][0], p['g_post'][1:2])

    def outputs(self):
        return self.y, self.conv_s, self.c_s, self.n_s, self.m_s, self.wins_s


def kernel(x_prompt, x_sample, mem_prompt, state_conv, state_C, state_n, state_m, cache_win0, cache_win1,
           cache_win2, cache_mem_kv, g_pre, g_post, w_in_a, conv_w_a, conv_b_a, w_q_a, w_k_a, w_v_a, w_if_a,
           b_if_a, g_hn_a, skip_a, w_out_a, g_kv, w_kv_b, w_in_b, w_out_b, w_mkv):
    p = _prep_params(g_pre, g_post, w_in_a, conv_w_a, conv_b_a, w_q_a, w_k_a, w_v_a, w_if_a, b_if_a,
                     g_hn_a, skip_a, w_out_a, g_kv, w_kv_b, w_in_b, w_out_b, w_mkv)
    sample = _SampleGroup(x_sample, state_conv, state_C, state_n, state_m,
                          (cache_win0, cache_win1, cache_win2), cache_mem_kv, p)
    y_p, conv_p, c_p, n_p, m_p, wins_p, memkv_p = _prompt_group(x_prompt, mem_prompt, p, sample)
    y_s, conv_s, c_s, n_s, m_s, wins_s = sample.outputs()
    return (y_p, y_s, conv_p, c_p, n_p, m_p, wins_p[0], wins_p[1], wins_p[2], memkv_p,
            conv_s, c_s, n_s, m_s, wins_s[0], wins_s[1], wins_s[2])
```

```python
import functools

import jax
import jax.numpy as jnp
from jax import lax
from jax.experimental import pallas as pl
from jax.experimental.pallas import tpu as pltpu

F32 = jnp.float32
BF16 = jnp.bfloat16

D_MODEL = 1024
A_HEADS = 4
A_HDIM = 256
A_INNER = 1024
CONV_W = 4
A_CHUNK = 128
B_GROUPS = ((128, 1), (512, 4), (2048, 16))
N_GROUPS = 3
B_HEADS = 4
B_HDIM = 128
B_WIDTH = 512
N_MEM = 256
M_HEADS = 4
M_HDIM = 128
M_WIDTH = 512
ROPE_THETA = 10000.0
EPS = 1e-6
PAST_LEN = 8192

LANES = 128
TOK_TILE = 512
L0_TILE = 256
L1OUT_TILE = 1024
ATT_BLK = 128
ATT_TILE = 2048
ATT_GROUP = 2
CAST_STEPS = 8
VMEM_LIMIT = 56 * 1024 * 1024

NT_DIMS = (((1,), (1,)), ((), ()))
LOG2E = 1.4426950408889634


def _dot(a, b):
    return jnp.dot(a, b, preferred_element_type=F32)


def _dot_nt(a, b):
    return lax.dot_general(a, b, NT_DIMS, preferred_element_type=F32)


def _sigmoid(x):
    return 1.0 / (1.0 + jnp.exp(-x))


def _silu(x):
    return x * _sigmoid(x)


def _log_sigmoid(x):
    return jnp.minimum(x, 0.0) - jnp.log(1.0 + jnp.exp(-jnp.abs(x)))


def _rms_scale(x):
    return x * lax.rsqrt(jnp.mean(x * x, axis=-1, keepdims=True) + EPS)


def _const_spec(shape):
    nd = len(shape)
    return pl.BlockSpec(shape, lambda *_: (0,) * nd, pipeline_mode=pl.Buffered(1))


def _mem_attention(qm, mk, mv):
    heads = range(M_HEADS)
    sl = [slice(h * M_HDIM, (h + 1) * M_HDIM) for h in heads]
    s = [_dot_nt(qm[:, sl[h]], mk[:, sl[h]]) for h in heads]
    mx = [jnp.max(s[h], axis=-1, keepdims=True) for h in heads]
    p = [jnp.exp2((s[h] - mx[h]) * (M_HDIM ** -0.5 * LOG2E)) for h in heads]
    l = [jnp.sum(p[h], axis=-1, keepdims=True) for h in heads]
    outs = [_dot(p[h].astype(BF16), mv[:, sl[h]]) / l[h] for h in heads]
    return jnp.concatenate(outs, axis=-1)


def _rows_to_heads(x):
    return jnp.swapaxes(jnp.stack([x[:, h * LANES:(h + 1) * LANES] for h in range(x.shape[1] // LANES)]), 0, 1)


def _heads_to_rows(ref):
    return jnp.concatenate([ref[:, h, :] for h in range(ref.shape[1])], axis=-1)


def _rows_to_kv_heads(k, v):
    pieces = [a[:, h * LANES:(h + 1) * LANES] for a in (k, v) for h in range(a.shape[1] // LANES)]
    return jnp.swapaxes(jnp.stack(pieces), 0, 1)


def _memkv_kernel(m_ref, w_ref, o_ref, ob_ref):
    r = _dot(m_ref[...].astype(BF16), w_ref[0])
    o_ref[0] = _rows_to_kv_heads(r[:, 0:M_WIDTH], r[:, M_WIDTH:2 * M_WIDTH])
    ob_ref[0] = r.astype(BF16)


def _memkv(mem2d, w_bf):
    nm = mem2d.shape[0]
    nl = w_bf.shape[0]
    tm = min(512, nm)
    return pl.pallas_call(
        _memkv_kernel,
        grid=(nl, nm // tm),
        in_specs=[pl.BlockSpec((tm, D_MODEL), lambda l, i: (i, 0)),
                  pl.BlockSpec((1, D_MODEL, 2 * M_WIDTH), lambda l, i: (l, 0, 0))],
        out_specs=[pl.BlockSpec((1, tm, 2 * M_HEADS, M_HDIM), lambda l, i: (l, i, 0, 0)),
                   pl.BlockSpec((1, tm, 2 * M_WIDTH), lambda l, i: (l, i, 0))],
        out_shape=[jax.ShapeDtypeStruct((nl, nm, 2 * M_HEADS, M_HDIM), F32),
                   jax.ShapeDtypeStruct((nl, nm, 2 * M_WIDTH), BF16)],
        compiler_params=pltpu.CompilerParams(dimension_semantics=("arbitrary", "arbitrary")),
        name="memkv",
    )(mem2d, w_bf)


def _mlstm_chunk(rs, g, qkv_v, kt_v, c_s, n_s, m_s, causal, hs):
    ls = _log_sigmoid(g)
    tok = lax.broadcasted_iota(jnp.int32, (A_CHUNK, LANES), 0)
    bc = ls
    shift = 1
    while shift < A_CHUNK:
        bc = bc + jnp.where(tok >= shift, pltpu.roll(bc, shift, 0), 0.0)
        shift *= 2
    lane = lax.broadcasted_iota(jnp.int32, (A_CHUNK, LANES), 1)
    xt = jnp.where(lane < A_HEADS, g, bc).T
    yield
    heads = range(A_HEADS)
    b_col = [bc[:, 4 + h:5 + h] for h in heads]
    b_row = [xt[4 + h:5 + h, :] for h in heads]
    li_row = [xt[h:h + 1, :] for h in heads]
    li_col = [g[:, h:h + 1] for h in heads]
    m_old = [m_s[h:h + 1, 0:1] for h in heads]
    b_last = [bc[A_CHUNK - 1:A_CHUNK, 4 + h:5 + h] for h in heads]
    qh = [qkv_v[rs, h * 3 * A_HDIM:h * 3 * A_HDIM + A_HDIM] for h in heads]
    kh = [qkv_v[rs, h * 3 * A_HDIM + A_HDIM:h * 3 * A_HDIM + 2 * A_HDIM] for h in heads]
    vh = [qkv_v[rs, h * 3 * A_HDIM + 2 * A_HDIM:(h + 1) * 3 * A_HDIM] for h in heads]
    kt = [kt_v[h] for h in heads]
    c_old = [c_s[h] for h in heads]
    n_old = [n_s[h:h + 1, :] for h in heads]
    qk = [_dot_nt(qh[h], kh[h]) for h in heads]
    qc = [_dot(qh[h], c_old[h].astype(BF16)) for h in heads]
    dm = [jnp.where(causal, b_col[h] - b_row[h] + li_row[h], -jnp.inf) for h in heads]
    inter = [b_col[h] + m_old[h] for h in heads]
    m_row = [jnp.maximum(inter[h], jnp.max(dm[h], axis=-1, keepdims=True)) for h in heads]
    g_max = [jnp.max(b_last[h] - b_row[h] + li_row[h], axis=-1, keepdims=True) for h in heads]
    m_new = [jnp.maximum(b_last[h] + m_old[h], g_max[h]) for h in heads]
    yield
    sc = [qk[h] * jnp.exp(dm[h] - m_row[h]) for h in heads]
    dec = [jnp.exp(inter[h] - m_row[h]) for h in heads]
    ws_col = [jnp.exp(b_last[h] - b_col[h] + li_col[h] - m_new[h]) for h in heads]
    dc = [jnp.exp(b_last[h] + m_old[h] - m_new[h]) for h in heads]
    yield
    sv = [_dot(sc[h].astype(BF16), vh[h]) for h in heads]
    wv =[(ws_col[h] * vh[h].astype(F32)).astype(BF16) for h in heads]
    upd = [_dot(kt[h], wv[h]) for h in heads]
    yield
    for h in heads:
        den = (jnp.sum(sc[h], axis=-1, keepdims=True)
               + dec[h] * jnp.sum(qh[h].astype(F32) * n_old[h], axis=-1, keepdims=True))
        num = sv[h] + dec[h] * qc[h]
        hs.append(num / jnp.maximum(jnp.abs(den), jnp.exp(-m_row[h])))
    yield
    for h in heads:
        c_s[h] = dc[h] * c_old[h] + upd[h]
        n_s[h:h + 1, :] = dc[h] * n_old[h] + jnp.sum(ws_col[h] * kh[h].astype(F32), axis=0, keepdims=True)
        m_s[h:h + 1, :] = jnp.broadcast_to(m_new[h], (1, LANES))
    yield


def _l0p_kernel(nt, n_dec_seq, *refs):
    k_in, k_out = 16, 5
    k_dec_in, k_dec_out = (9, 5) if n_dec_seq else (0, 0)
    (x_ref, xp_ref, gpre_ref, win_ref, convw_ref, convb_ref, wq_ref, wk_ref, wv_ref,
     wif_ref, bif_ref, ghn_ref, skip_ref, mkv_ref, wout_ref, gpost_ref) = refs[0:k_in]
    dec_in = refs[k_in:k_in + k_dec_in]
    o0 = k_in + k_dec_in
    x1_ref, conv_out, c_out, n_out, m_out = refs[o0:o0 + k_out]
    dec_out = refs[o0 + k_out:o0 + k_out + k_dec_out]
    (h_s, u_s, ymix_s, xc_s, opre_s, zg_s, qm_s, zm_s, qkv_s, kt_s, gates_s,
     c_s, n_s, m_s) = refs[o0 + k_out + k_dec_out:]
    tt = x_ref.shape[1]
    nsub = tt // A_CHUNK
    t = pl.program_id(0)
    parity = lax.rem(t + 1, 2)
    pos1 = lax.rem(t + nt - 1, nt)
    pos2 = lax.rem(t + 2 * nt - 2, nt)

    @pl.when(t == 0)
    def _():
        h_s[...] = jnp.zeros(h_s.shape, BF16)
        u_s[...] = jnp.zeros(u_s.shape, F32)
        xc_s[0] = jnp.zeros(xc_s.shape[1:], F32)
        opre_s[0] = jnp.zeros(opre_s.shape[1:], F32)
        zg_s[0] = jnp.zeros(zg_s.shape[1:], F32)
        qm_s[0] = jnp.zeros(qm_s.shape[1:], BF16)
        zm_s[0] = jnp.zeros(zm_s.shape[1:], F32)
        qkv_s[0] = jnp.zeros(qkv_s.shape[1:], BF16)
        kt_s[0] = jnp.zeros(kt_s.shape[1:], BF16)
        gates_s[0] = jnp.zeros(gates_s.shape[1:], F32)

    @pl.when(pos1 == 0)
    def _():
        u_s[0:8, :] = jnp.zeros((8, A_INNER), F32)

    @pl.when(pos2 == 0)
    def _():
        c_s[...] = jnp.zeros(c_s.shape, F32)
        n_s[...] = jnp.zeros(n_s.shape, F32)
        m_s[...] = jnp.zeros(m_s.shape, F32)

    row = lax.broadcasted_iota(jnp.int32, (A_CHUNK, A_CHUNK), 0)
    col = lax.broadcasted_iota(jnp.int32, (A_CHUNK, A_CHUNK), 1)
    causal = col <= row
    ghn = ghn_ref[...]
    skp = skip_ref[...]
    mk = mkv_ref[0, :, 0:M_WIDTH]
    mv = mkv_ref[0, :, M_WIDTH:2 * M_WIDTH]

    def stage2(pslot):
        ym = _mem_attention(qm_s[pslot], mk, mv) * _silu(zm_s[pslot])
        ymix_s[:, A_INNER:A_INNER + M_WIDTH] = ym.astype(BF16)
        yield
        for c in range(nsub):
            rs = slice(c * A_CHUNK, (c + 1) * A_CHUNK)
            hs = []
            yield from _mlstm_chunk(rs, gates_s[pslot, rs, :], qkv_s.at[pslot], kt_s.at[pslot, :, c],
                                    c_s, n_s, m_s, causal, hs)
            parts = []
            for h in range(A_HEADS):
                v = _sigmoid(opre_s[pslot, rs, h * A_HDIM:(h + 1) * A_HDIM]) * hs[h]
                mu = jnp.mean(v, axis=-1, keepdims=True)
                var = jnp.mean(jnp.square(v - mu), axis=-1, keepdims=True)
                parts.append((v - mu) * lax.rsqrt(var + EPS))
            hn = jnp.concatenate(parts, axis=-1) * ghn
            y = hn + skp * xc_s[pslot, rs, :]
            ymix_s[rs, 0:A_INNER] = (y * _silu(zg_s[pslot, rs, :])).astype(BF16)
            yield

    def stage1(slot):
        hb = h_s[...]
        u_s[8:8 + tt, :] = _dot(hb, win_ref[:, 0:A_INNER])
        yield
        cw = convw_ref[...]
        cb = convb_ref[...]
        for c in range(nsub):
            r0 = c * A_CHUNK
            blk = u_s[r0:r0 + A_CHUNK + 8, :]
            xc = cb + pltpu.roll(blk, 3, 0)[8:, :] * cw[0:1, :]
            xc = xc + pltpu.roll(blk, 2, 0)[8:, :] * cw[1:2, :]
            xc = xc + pltpu.roll(blk, 1, 0)[8:, :] * cw[2:3, :]
            xc = xc + blk[8:, :] * cw[3:4, :]
            xc_s[slot, r0:r0 + A_CHUNK, :] = _silu(xc)
        opre_s[slot] = _dot(hb, win_ref[:, A_INNER:2 * A_INNER])
        yield
        zg_s[slot] = _dot(hb, win_ref[:, 2 * A_INNER:3 * A_INNER])
        yield
        qm_s[slot] = _dot(hb, win_ref[:, 3 * A_INNER:3 * A_INNER + M_WIDTH]).astype(BF16)
        zm_s[slot] = _dot(hb, win_ref[:, 3 * A_INNER + M_WIDTH:3 * A_INNER + 2 * M_WIDTH])
        yield
        for h in range(A_HEADS):
            sl = slice(h * A_HDIM, (h + 1) * A_HDIM)
            xh = xc_s[slot, :, sl].astype(BF16)
            uh = u_s[8:8 + tt, sl].astype(BF16)
            base = h * 3 * A_HDIM
            qkv_s[slot, :, base:base + A_HDIM] = _dot(xh, wq_ref[h]).astype(BF16)
            kf = _dot(xh, wk_ref[h]) * (A_HDIM ** -0.5)
            qkv_s[slot, :, base + A_HDIM:base + 2 * A_HDIM] = kf.astype(BF16)
            qkv_s[slot, :, base + 2 * A_HDIM:base + 3 * A_HDIM] = _dot(uh, wv_ref[h]).astype(BF16)
            kt = kf.T.astype(BF16)
            for c in range(nsub):
                kt_s[slot, h, c] = kt[:, c * A_CHUNK:(c + 1) * A_CHUNK]
            yield
        gates_s[slot] = _dot(qkv_s[slot], wif_ref[...]) + bif_ref[...]
        yield

    def step(slot):
        if n_dec_seq:
            first = slot == 1
            heads = (0, 1) if first else (2, 3)
            _dec_mlstm_body(jnp.minimum(t // 2, n_dec_seq - 1), heads, first, *dec_in, *dec_out)
        pending = [stage2(1 - slot), stage1(slot)]
        while pending:
            for gen in list(pending):
                try:
                    next(gen)
                except StopIteration:
                    pending.remove(gen)
        h_next = (_rms_scale(x_ref[0]) * gpre_ref[...]).astype(BF16)
        out = _dot(ymix_s[...], wout_ref[...])
        h_s[...] = h_next
        x1_ref[0] = xp_ref[0] + _rms_scale(out) * gpost_ref[...]

    for s in range(2):
        pl.when(parity == s)(functools.partial(step, s))

    @pl.when(jnp.logical_and(pos1 == nt - 1, t > 0))
    def _():
        conv_out[0, 0] = u_s[tt + 5:tt + 8, :]

    u_s[0:8, :] = u_s[tt:tt + 8, :]

    @pl.when(jnp.logical_and(pos2 == nt - 1, t > 1))
    def _():
        for h in range(A_HEADS):
            c_out[0, 0, h] = c_s[h].T
        n_out[0, 0] = n_s[0:A_HEADS, :]
        m_out[0] = m_s[...]


def _layer0_prompt_pipelined(x, g_pre, w_in, conv_w, conv_b, wq, wk, wv, wif, bif, ghn, skip, mkv_bf,
                             w_out, g_post, dec_job=None):
    b, s, _ = x.shape
    tt = min(L0_TILE, s)
    nt = s // tt
    ntiles = b * nt
    a_in = w_in.shape[1]
    nsub = tt // A_CHUNK

    def cur(t):
        t1 = jnp.minimum(t, ntiles - 1)
        return (t1 // nt, t1 % nt, 0)

    def prev(t):
        t2 = jnp.maximum(t - 2, 0)
        return (t2 // nt, t2 % nt, 0)

    def prev_b(t):
        return jnp.maximum(t - 2, 0) // nt

    in_specs = [
        pl.BlockSpec((1, tt, D_MODEL), cur),
        pl.BlockSpec((1, tt, D_MODEL), prev),
        _const_spec((1, D_MODEL)),
        _const_spec((D_MODEL, a_in)),
        _const_spec((CONV_W, A_INNER)),
        _const_spec((1, A_INNER)),
        _const_spec((A_HEADS, A_HDIM, A_HDIM)),
        _const_spec((A_HEADS, A_HDIM, A_HDIM)),
        _const_spec((A_HEADS, A_HDIM, A_HDIM)),
        _const_spec((3 * A_INNER, LANES)),
        _const_spec((1, LANES)),
        _const_spec((1, A_INNER)),
        _const_spec((1, A_INNER)),
        pl.BlockSpec((1, N_MEM, 2 * M_WIDTH), lambda t: (prev_b(t), 0, 0)),
        _const_spec((A_INNER + M_WIDTH, D_MODEL)),
        _const_spec((1, D_MODEL)),
    ]
    out_specs = [
        pl.BlockSpec((1, tt, D_MODEL), prev),
        pl.BlockSpec((1, 1, CONV_W - 1, A_INNER), lambda t: (0, prev_b(t), 0, 0)),
        pl.BlockSpec((1, 1, A_HEADS, A_HDIM, A_HDIM), lambda t: (0, prev_b(t), 0, 0, 0)),
        pl.BlockSpec((1, 1, A_HEADS, A_HDIM), lambda t: (0, prev_b(t), 0, 0)),
        pl.BlockSpec((1, 8, LANES), lambda t: (prev_b(t), 0, 0)),
    ]
    out_shape = [
        jax.ShapeDtypeStruct((b, s, D_MODEL), F32),
        jax.ShapeDtypeStruct((1, b, CONV_W - 1, A_INNER), F32),
        jax.ShapeDtypeStruct((1, b, A_HEADS, A_HDIM, A_HDIM), F32),
        jax.ShapeDtypeStruct((1, b, A_HEADS, A_HDIM), F32),
        jax.ShapeDtypeStruct((b, 8, LANES), F32),
    ]
    scratch = [
        pltpu.VMEM((tt, D_MODEL), BF16),
        pltpu.VMEM((tt + 8, A_INNER), F32),
        pltpu.VMEM((tt, A_INNER + M_WIDTH), BF16),
        pltpu.VMEM((2, tt, A_INNER), F32),
        pltpu.VMEM((2, tt, A_INNER), F32),
        pltpu.VMEM((2, tt, A_INNER), F32),
        pltpu.VMEM((2, tt, M_WIDTH), BF16),
        pltpu.VMEM((2, tt, M_WIDTH), F32),
        pltpu.VMEM((2, tt, 3 * A_INNER), BF16),
        pltpu.VMEM((2, A_HEADS, nsub, A_HDIM, A_CHUNK), BF16),
        pltpu.VMEM((2, tt, LANES), F32),
        pltpu.VMEM((A_HEADS, A_HDIM, A_HDIM), F32),
        pltpu.VMEM((8, A_HDIM), F32),
        pltpu.VMEM((8, LANES), F32),
    ]
    dec_args, dec_specs, dec_out_specs, dec_out_shapes, n_dec_seq = [], [], [], [], 0
    if dec_job is not None:
        n_dec_seq = dec_job[0].shape[0]
        dec_args, dec_specs, dec_out_specs, dec_out_shapes = _dec_mlstm_specs(
            *dec_job, seq_index=lambda t: jnp.minimum(t // 2, n_dec_seq - 1))
    return pl.pallas_call(
        functools.partial(_l0p_kernel, nt, n_dec_seq),
        grid=(ntiles + 2,),
        in_specs=in_specs + dec_specs,
        out_specs=out_specs + dec_out_specs,
        out_shape=out_shape + dec_out_shapes,
        scratch_shapes=scratch,
        compiler_params=pltpu.CompilerParams(
            dimension_semantics=("arbitrary",), vmem_limit_bytes=VMEM_LIMIT),
        name="layer0_prompt",
    )(x, x, g_pre, w_in, conv_w, conv_b, wq, wk, wv, wif, bif, ghn, skip, mkv_bf, w_out, g_post,
      *dec_args)


def _rope_cols(x, cos, sin_signed):
    outs = []
    for cblk in range(x.shape[1] // B_HDIM):
        xb = x[:, cblk * B_HDIM:(cblk + 1) * B_HDIM]
        outs.append(xb * cos + pltpu.roll(xb, B_HDIM // 2, 1) * sin_signed)
    return jnp.concatenate(outs, axis=-1)


def _l1a_kernel(n_dec_in, *refs):
    n_in, n_out = 7, 15
    x_ref, gkv_ref, gpre_ref, wkv_ref, win_ref, cos_ref, sin_ref = refs[0:n_in]
    dec_in = refs[n_in:n_in + n_dec_in]
    (q0_ref, q1_ref, q2_ref, k0_ref, k1_ref, k2_ref, v0_ref, v1_ref, v2_ref,
     zg_ref, qm_ref, zm_ref, w0_ref, w1_ref, w2_ref) = refs[n_in + n_dec_in:n_in + n_dec_in + n_out]
    dec_out = refs[n_in + n_dec_in + n_out:]
    tt = x_ref.shape[1]
    xn = _rms_scale(x_ref[0])
    hk = (xn * gkv_ref[...]).astype(BF16)
    hq = (xn * gpre_ref[...]).astype(BF16)
    cos = cos_ref[...]
    sin = sin_ref[...]
    q_refs = (q0_ref, q1_ref, q2_ref)
    k_refs = (k0_ref, k1_ref, k2_ref)
    v_refs = (v0_ref, v1_ref, v2_ref)
    w_refs = (w0_ref, w1_ref, w2_ref)
    for g in (2, 1, 0):
        d = B_GROUPS[g][1]
        kf = _rope_cols(_dot(hk, wkv_ref[:, g * 2 * B_WIDTH:g * 2 * B_WIDTH + B_WIDTH]), cos, sin)
        vf = _dot(hk, wkv_ref[:, g * 2 * B_WIDTH + B_WIDTH:(g + 1) * 2 * B_WIDTH])
        qf = _rope_cols(_dot(hq, win_ref[:, g * B_WIDTH:(g + 1) * B_WIDTH]), cos, sin)
        wr = w_refs[g]
        wrows = wr.shape[1]
        wr[0] = _rows_to_kv_heads(kf[tt - wrows:, :], vf[tt - wrows:, :])
        for val, ref in ((qf.astype(BF16), q_refs[g]), (kf.astype(BF16), k_refs[g]), (vf.astype(BF16), v_refs[g])):
            if d == 1:
                ref[0, 0] = val
            else:
                ref[0] = jnp.swapaxes(val.reshape(tt // d, d, val.shape[1]), 0, 1)
        if g == 2 and n_dec_in:
            _dec_attn_kernel(*dec_in, *dec_out)
    qoff = N_GROUPS * B_WIDTH
    zg_ref[0] = _dot(hq, win_ref[:, qoff:qoff + B_WIDTH]).astype(BF16)
    qm_ref[0] = _dot(hq, win_ref[:, qoff + B_WIDTH:qoff + B_WIDTH + M_WIDTH]).astype(BF16)
    zm_ref[0] = _dot(hq, win_ref[:, qoff + B_WIDTH + M_WIDTH:qoff + B_WIDTH + 2 * M_WIDTH]).astype(BF16)


def _layer1_proj_prompt(x1, g_kv, g_pre, wkv, win, cos_t, sin_t, dec_job=None):
    b, s, _ = x1.shape
    tt = min(TOK_TILE, s)
    nt = s // tt
    tile = lambda bb, i: (bb, i, 0)
    in_specs = [
        pl.BlockSpec((1, tt, D_MODEL), tile),
        _const_spec((1, D_MODEL)),
        _const_spec((1, D_MODEL)),
        _const_spec(wkv.shape),
        _const_spec(win.shape),
        pl.BlockSpec((tt, B_HDIM), lambda bb, i: (i, 0)),
        pl.BlockSpec((tt, B_HDIM), lambda bb, i: (i, 0)),
    ]
    qkv_specs, qkv_shapes = [], []
    for _ in range(3):
        for (_, d) in B_GROUPS:
            qkv_specs.append(pl.BlockSpec((1, d, tt // d, B_WIDTH), lambda bb, i: (bb, 0, i, 0)))
            qkv_shapes.append(jax.ShapeDtypeStruct((b, d, s // d, B_WIDTH), BF16))
    gate_specs = [pl.BlockSpec((1, tt, B_WIDTH), tile)] * 3
    gate_shapes = [jax.ShapeDtypeStruct((b, s, B_WIDTH), BF16)] * 3
    win_specs, win_shapes = [], []
    for (w, _) in B_GROUPS:
        wr = min(w, s)
        rows = min(wr, tt)
        nblk = wr // rows
        win_specs.append(pl.BlockSpec(
            (1, rows, 2 * B_HEADS, B_HDIM),
            functools.partial(lambda bb, i, nb: (bb, jnp.maximum(i - (nt - nb), 0), 0, 0), nb=nblk)))
        win_shapes.append(jax.ShapeDtypeStruct((b, wr, 2 * B_HEADS, B_HDIM), F32))
    dec_args, dec_specs, dec_out_specs, dec_out_shapes = [], [], [], []
    if dec_job is not None:
        dec_args, dec_specs, dec_out_specs, dec_out_shapes = _dec_attn_specs(
            *dec_job, seq_index=lambda bb, i: bb * nt + i)
    return pl.pallas_call(
        functools.partial(_l1a_kernel, len(dec_args)),
        grid=(b, nt),
        in_specs=in_specs + dec_specs,
        out_specs=qkv_specs + gate_specs + win_specs + dec_out_specs,
        out_shape=qkv_shapes + gate_shapes + win_shapes + dec_out_shapes,
        compiler_params=pltpu.CompilerParams(
            dimension_semantics=("arbitrary", "arbitrary"), vmem_limit_bytes=VMEM_LIMIT),
        name="layer1_proj_prompt",
    )(x1, g_kv, g_pre, wkv, win, cos_t, sin_t, *dec_args)


def _cols_to_lanes(cols, fill):
    t = cols[0].shape[0]
    lane = lax.broadcasted_iota(jnp.int32, (t, LANES), 1)
    acc = jnp.full((t, LANES), fill, F32)
    for h, cvec in enumerate(cols):
        acc = jnp.where(lane == h, cvec, acc)
    return acc


def _band_attn_kernel(q_ref, kc_ref, kp_ref, vc_ref, vp_ref, o_ref, lse_ref):
    nres, tq = q_ref.shape[1:3]
    nsb = tq // ATT_BLK
    j = pl.program_id(2)
    row = lax.broadcasted_iota(jnp.int32, (ATT_BLK, 2 * ATT_BLK), 0)
    col = lax.broadcasted_iota(jnp.int32, (ATT_BLK, 2 * ATT_BLK), 1)
    band = jnp.logical_and(col >= row, col <= row + ATT_BLK)
    first_pen = jnp.where(col < ATT_BLK, jnp.where(j > 0, 0.0, -jnp.inf), 0.0)
    scale = B_HDIM ** -0.5
    blocks = [(r, sb) for r in range(nres) for sb in range(nsb)]
    for g0 in range(0, len(blocks), ATT_GROUP):
        grp = blocks[g0:g0 + ATT_GROUP]
        qs, ks, vs, pens = [], [], [], []
        for r, sb in grp:
            rs = slice(sb * ATT_BLK, (sb + 1) * ATT_BLK)
            ps = slice((sb - 1) * ATT_BLK, sb * ATT_BLK)
            for h in range(B_HEADS):
                hs = slice(h * B_HDIM, (h + 1) * B_HDIM)
                qs.append(q_ref[0, r, rs, hs])
                kp = kp_ref[0, r, :, hs] if sb == 0 else kc_ref[0, r, ps, hs]
                vp = vp_ref[0, r, :, hs] if sb == 0 else vc_ref[0, r, ps, hs]
                ks.append(jnp.concatenate([kp, kc_ref[0, r, rs, hs]], axis=0))
                vs.append(jnp.concatenate([vp, vc_ref[0, r, rs, hs]], axis=0))
                pens.append(sb == 0)
        q3 = jnp.stack(qs)
        k3 = jnp.stack(ks)
        v3 = jnp.stack(vs)
        s = jnp.einsum('uqd,ukd->uqk', q3, k3, preferred_element_type=F32)
        s = jnp.stack([s[u] + first_pen if pens[u] else s[u] for u in range(len(pens))])
        s = jnp.where(band[None], s, -jnp.inf)
        mx = jnp.max(s, axis=-1, keepdims=True)
        p = jnp.exp2((s - mx) * (scale * LOG2E))
        l = jnp.sum(p, axis=-1, keepdims=True)
        o = jnp.einsum('uqk,ukd->uqd', p.astype(BF16), v3, preferred_element_type=F32) / l
        for i, (r, sb) in enumerate(grp):
            rs = slice(sb * ATT_BLK, (sb + 1) * ATT_BLK)
            for h in range(B_HEADS):
                o_ref[0, r, rs, h * B_HDIM:(h + 1) * B_HDIM] = o[i * B_HEADS + h].astype(BF16)
            mx_l = _cols_to_lanes([mx[i * B_HEADS + h] for h in range(B_HEADS)], 0.0)
            l_l = _cols_to_lanes([l[i * B_HEADS + h] for h in range(B_HEADS)], 1.0)
            lse_ref[0, r, rs, :] = mx_l * scale + jnp.log(l_l)


def _band_attention(q, k, v):
    b, d, ls, _ = q.shape
    tq = min(ATT_TILE, ls)
    nj = ls // tq
    ratio = tq // ATT_BLK
    nres = min(d, ATT_TILE // tq)
    cur = lambda bb, r, j: (bb, r, j, 0)
    prev = lambda bb, r, j: (bb, r, jnp.maximum(j * ratio - 1, 0), 0)
    return pl.pallas_call(
        _band_attn_kernel,
        grid=(b, d // nres, nj),
        in_specs=[pl.BlockSpec((1, nres, tq, B_WIDTH), cur),
                  pl.BlockSpec((1, nres, tq, B_WIDTH), cur),
                  pl.BlockSpec((1, nres, ATT_BLK, B_WIDTH), prev),
                  pl.BlockSpec((1, nres, tq, B_WIDTH), cur),
                  pl.BlockSpec((1, nres, ATT_BLK, B_WIDTH), prev)],
        out_specs=[pl.BlockSpec((1, nres, tq, B_WIDTH), cur),
                   pl.BlockSpec((1, nres, tq, LANES), cur)],
        out_shape=[jax.ShapeDtypeStruct((b, d, ls, B_WIDTH), BF16),
                   jax.ShapeDtypeStruct((b, d, ls, LANES), F32)],
        compiler_params=pltpu.CompilerParams(
            dimension_semantics=("arbitrary", "arbitrary", "arbitrary"), vmem_limit_bytes=VMEM_LIMIT),
        name="band_attention_d%d" % d,
    )(q, k, k, v, v)


def _unpermute(ref):
    d, rows, width = ref.shape[1:]
    if d == 1:
        return ref[0, 0]
    return jnp.swapaxes(ref[0], 0, 1).reshape(d * rows, width)


def _l1c_kernel(x_ref, o0_ref, o1_ref, o2_ref, l0_ref, l1_ref, l2_ref, zg_ref, qm_ref, zm_ref,
                mkv_ref, wout_ref, gpost_ref, y_ref):
    tt = x_ref.shape[1]
    o_refs = (o0_ref, o1_ref, o2_ref)
    l_refs = (l0_ref, l1_ref, l2_ref)
    outs, lses = [], []
    for g, (_, d) in enumerate(B_GROUPS):
        outs.append(_unpermute(o_refs[g]))
        lses.append(_unpermute(l_refs[g])[:, 0:B_HEADS])
    mx = jnp.maximum(jnp.maximum(lses[0], lses[1]), lses[2])
    es = [jnp.exp(l - mx) for l in lses]
    tot = es[0] + es[1] + es[2]
    ws = [(e / tot).astype(BF16) for e in es]
    parts = []
    for h in range(B_HEADS):
        hs = slice(h * B_HDIM, (h + 1) * B_HDIM)
        acc = ws[0][:, h:h + 1] * outs[0][:, hs]
        acc = acc + ws[1][:, h:h + 1] * outs[1][:, hs]
        acc = acc + ws[2][:, h:h + 1] * outs[2][:, hs]
        parts.append(acc)
    ydil = jnp.concatenate(parts, axis=-1)
    ymix = (ydil.astype(F32) * _silu(zg_ref[0].astype(F32))).astype(BF16)
    mk = mkv_ref[0, :, 0:M_WIDTH]
    mv = mkv_ref[0, :, M_WIDTH:2 * M_WIDTH]
    ym = (_mem_attention(qm_ref[0], mk, mv) * _silu(zm_ref[0].astype(F32))).astype(BF16)
    out = _dot(ymix, wout_ref[0:B_WIDTH, :]) + _dot(ym, wout_ref[B_WIDTH:B_WIDTH + M_WIDTH, :])
    y_ref[0] = x_ref[0] + _rms_scale(out) * gpost_ref[...]


def _layer1_out_prompt(x1, os_, ls_, zg, qm, zm, mkv_bf, w_out, g_post):
    b, s, _ = x1.shape
    tt = min(L1OUT_TILE, s)
    nt = s // tt
    tile = lambda bb, i: (bb, i, 0)
    perm = lambda bb, i: (bb, 0, i, 0)
    in_specs = [pl.BlockSpec((1, tt, D_MODEL), tile)]
    for width in (B_WIDTH, LANES):
        for (_, d) in B_GROUPS:
            in_specs.append(pl.BlockSpec((1, d, tt // d, width), perm))
    in_specs += [pl.BlockSpec((1, tt, B_WIDTH), tile)] * 3
    in_specs += [pl.BlockSpec((1, N_MEM, 2 * M_WIDTH), lambda bb, i: (bb, 0, 0)),
                 _const_spec(w_out.shape), _const_spec((1, D_MODEL))]
    return pl.pallas_call(
        _l1c_kernel,
        grid=(b, nt),
        in_specs=in_specs,
        out_specs=pl.BlockSpec((1, tt, D_MODEL), tile),
        out_shape=jax.ShapeDtypeStruct((b, s, D_MODEL), F32),
        compiler_params=pltpu.CompilerParams(
            dimension_semantics=("arbitrary", "arbitrary"), vmem_limit_bytes=VMEM_LIMIT),
        name="layer1_out_prompt",
    )(x1, *os_, *ls_, zg, qm, zm, mkv_bf, w_out, g_post)


def _rope_tables(pos):
    half = B_HDIM // 2
    inv = ROPE_THETA ** (-jnp.arange(half, dtype=F32) / half)
    ang = pos[:, None] * inv[None, :]
    cos = jnp.cos(ang)
    sin = jnp.sin(ang)
    return jnp.concatenate([cos, cos], axis=-1), jnp.concatenate([-sin, sin], axis=-1)


def _prompt_group(x_prompt, mem_prompt, p, sample=None):
    b, s, _ = x_prompt.shape
    memkv_f, memkv_b = _memkv(mem_prompt.reshape(b * N_MEM, D_MODEL), p['w_mkv'])
    depth = memkv_f.shape[0]
    memkv_b = memkv_b.reshape(depth, b, N_MEM, 2 * M_WIDTH)
    job0 = sample.mlstm_job() if sample is not None else None
    if job0 is not None and b * (s // min(L0_TILE, s)) + 2 < 2 * job0[0].shape[0]:
        job0 = None
    outs0 = _layer0_prompt_pipelined(
        x_prompt, p['g_pre'][0:1], p['w_in_a'][0], p['conv_w_a'][0], p['conv_b_a'], p['w_q_a'][0],
        p['w_k_a'][0], p['w_v_a'][0], p['w_if_a'], p['b_if_a'],
        p['g_hn_a'], p['skip_a'], memkv_b[0], p['w_out_a'][0], p['g_post'][0:1], job0)
    x1, conv_p, c_p, n_p, m_pad = outs0[0:5]
    if sample is not None:
        sample.after_mlstm(outs0[5:10] if job0 is not None else _dec_mlstm(*sample.mlstm_job()))
    cos_t, sin_t = _rope_tables(jnp.arange(s, dtype=F32))
    job1 = sample.attn_job if sample is not None else None
    if job1 is not None and b * (s // min(TOK_TILE, s)) != job1[0].shape[0]:
        job1 = None
    outs = _layer1_proj_prompt(x1, p['g_kv'], p['g_pre'][1:2], p['w_kv_b'], p['w_in_b'][0], cos_t, sin_t,
                               job1)
    if sample is not None:
        sample.after_attn(tuple(outs[15:17]) if job1 is not None else _dec_attn(*sample.attn_job))
    qs, ks, vs = outs[0:3], outs[3:6], outs[6:9]
    zg, qm, zm = outs[9:12]
    wins = outs[12:15]
    os_, ls_ = [], []
    for g in range(N_GROUPS):
        o, l = _band_attention(qs[g], ks[g], vs[g])
        os_.append(o)
        ls_.append(l)
    y = _layer1_out_prompt(x1, os_, ls_, zg, qm, zm, memkv_b[1], p['w_out_b'][0], p['g_post'][1:2])
    m_p = m_pad[:, 0:A_HEADS, 0][None]
    wins = [w.reshape(b, w.shape[1], 2, B_HEADS, B_HDIM) for w in wins]
    memkv_p = memkv_f.reshape(depth, b, N_MEM, 2, M_HEADS, M_HDIM)
    return y, conv_p, c_p, n_p, m_p, wins, memkv_p


def _cast_kernel(*refs):
    n = len(refs) // 2
    for src, dst in zip(refs[:n], refs[n:]):
        dst[...] = src[...].astype(BF16)


def _cast_bf16(arrays):
    flat = [a.reshape(-1, a.shape[-1]) for a in arrays]
    specs = [pl.BlockSpec((f.shape[0] // CAST_STEPS, f.shape[1]), lambda i: (i, 0)) for f in flat]
    outs = pl.pallas_call(
        _cast_kernel,
        grid=(CAST_STEPS,),
        in_specs=specs,
        out_specs=specs,
        out_shape=[jax.ShapeDtypeStruct(f.shape, BF16) for f in flat],
        compiler_params=pltpu.CompilerParams(dimension_semantics=("arbitrary",), vmem_limit_bytes=VMEM_LIMIT),
        name="cast_weights",
    )(*flat)
    return [o.reshape(a.shape) for o, a in zip(outs, arrays)]


def _prep_params(g_pre, g_post, w_in_a, conv_w_a, conv_b_a, w_q_a, w_k_a, w_v_a, w_if_a, b_if_a,
                 g_hn_a, skip_a, w_out_a, g_kv, w_kv_b, w_in_b, w_out_b, w_mkv):
    wif = jnp.pad(w_if_a[0], ((0, 0), (0, LANES - 2 * A_HEADS))).astype(BF16)
    bif = jnp.pad(b_if_a[0], (0, LANES - 2 * A_HEADS))[None, :]
    w_in_a, w_q_a, w_k_a, w_v_a, w_out_a, w_kv_b, w_in_b, w_out_b, w_mkv = _cast_bf16(
        [w_in_a, w_q_a, w_k_a, w_v_a, w_out_a, w_kv_b, w_in_b, w_out_b, w_mkv])
    return {
        'g_pre': g_pre, 'g_post': g_post,
        'w_in_a': w_in_a, 'conv_w_a': conv_w_a, 'conv_b_a': conv_b_a,
        'w_q_a': w_q_a, 'w_k_a': w_k_a, 'w_v_a': w_v_a,
        'w_if_a': wif, 'b_if_a': bif, 'g_hn_a': g_hn_a, 'skip_a': skip_a,
        'w_out_a': w_out_a, 'g_kv': g_kv[None, :], 'w_kv_b': w_kv_b,
        'w_in_b': w_in_b, 'w_out_b': w_out_b, 'w_mkv': w_mkv,
    }


def _dec_l0_proj_kernel(x_ref, gpre_ref, win_ref, cst_ref, convw_ref, convb_ref, wq_ref, wk_ref, wv_ref,
                        wif_ref, bif_ref,
                        q_ref, k_ref, v_ref, gates_ref, xc_ref, opre_ref, zg_ref, qm_ref, zm_ref, cnew_ref):
    h = (_rms_scale(x_ref[:, 0, :]) * gpre_ref[...]).astype(BF16)
    u = _dot(h, win_ref[:, 0:A_INNER])
    opre_ref[...] = _dot(h, win_ref[:, A_INNER:2 * A_INNER])
    zg_ref[...] = _dot(h, win_ref[:, 2 * A_INNER:3 * A_INNER])
    qm_ref[...] = _rows_to_heads(_dot(h, win_ref[:, 3 * A_INNER:3 * A_INNER + M_WIDTH]))
    zm_ref[...] = _dot(h, win_ref[:, 3 * A_INNER + M_WIDTH:3 * A_INNER + 2 * M_WIDTH])
    cw = convw_ref[...]
    xc = convb_ref[...] + cst_ref[0, :, 0, :] * cw[0:1, :]
    xc = xc + cst_ref[0, :, 1, :] * cw[1:2, :]
    xc = xc + cst_ref[0, :, 2, :] * cw[2:3, :]
    xc = xc + u * cw[3:4, :]
    xc = _silu(xc)
    xc_ref[...] = xc
    cnew_ref[0, :, 0, :] = cst_ref[0, :, 1, :]
    cnew_ref[0, :, 1, :] = cst_ref[0, :, 2, :]
    cnew_ref[0, :, 2, :] = u
    qs, ks, vs, cat = [], [], [], []
    for hd in range(A_HEADS):
        sl = slice(hd * A_HDIM, (hd + 1) * A_HDIM)
        xh = xc[:, sl].astype(BF16)
        qh = _dot(xh, wq_ref[hd])
        kh = _dot(xh, wk_ref[hd]) * (A_HDIM ** -0.5)
        vh = _dot(u[:, sl].astype(BF16), wv_ref[hd])
        qs.append(qh)
        ks.append(kh)
        vs.append(vh)
        cat += [qh.astype(BF16), kh.astype(BF16), vh.astype(BF16)]
    q_ref[...] = jnp.concatenate(qs, axis=-1)
    k_ref[...] = jnp.concatenate(ks, axis=-1)
    v_ref[...] = jnp.concatenate(vs, axis=-1)
    gates_ref[...] = _dot(jnp.concatenate(cat, axis=-1), wif_ref[...]) + bif_ref[...]


def _whole(shape):
    nd = len(shape)
    return pl.BlockSpec(shape, lambda *_: (0,) * nd)


def _dec_l0_proj(x, g_pre, w_in, cst, conv_w, conv_b, wq, wk, wv, wif, bif):
    nb = x.shape[0]
    args = (x, g_pre, w_in, cst, conv_w, conv_b, wq, wk, wv, wif, bif)
    f = lambda *s: jax.ShapeDtypeStruct(s, F32)
    out_shape = [f(nb, A_INNER), f(nb, A_INNER), f(nb, A_INNER), f(nb, LANES), f(nb, A_INNER), f(nb, A_INNER),
                 f(nb, A_INNER), f(nb, M_HEADS, M_HDIM), f(nb, M_WIDTH), f(1, nb, CONV_W - 1, A_INNER)]
    return pl.pallas_call(
        _dec_l0_proj_kernel,
        grid=(1,),
        in_specs=[_whole(a.shape) for a in args],
        out_specs=[_whole(o.shape) for o in out_shape],
        out_shape=out_shape,
        compiler_params=pltpu.CompilerParams(dimension_semantics=("arbitrary",), vmem_limit_bytes=VMEM_LIMIT),
        name="dec_l0_proj",
    )(*args)


def _row_to_col(row, eye):
    return jnp.sum(jnp.where(eye, row, 0.0), axis=-1, keepdims=True)


def _col_to_row(colv, eye):
    return jnp.sum(jnp.where(eye, colv, 0.0), axis=0, keepdims=True)


def _dec_mem_attention(q, kv_ref_view):
    kk = kv_ref_view[:, 0]
    vv = kv_ref_view[:, 1]
    s = jnp.sum(kk * (q * (M_HDIM ** -0.5))[None], axis=-1, keepdims=True)
    mx = jnp.max(s, axis=0, keepdims=True)
    p = jnp.exp(s - mx)
    return jnp.sum(p * vv, axis=0) / jnp.sum(p, axis=0)


def _dec_mlstm_body(b, heads, with_mem, q_ref, k_ref, v_ref, gates_ref, m_ref, c_ref, n_ref, qm_ref, kv_ref,
                    hs_ref, c_out, n_out, m_out, ym_ref):
    rb = pl.ds(b, 1)
    g = gates_ref[rb, :]
    mrow = m_ref[0, rb, :]
    r = lax.broadcasted_iota(jnp.int32, (A_HDIM, A_HDIM), 0)
    c = lax.broadcasted_iota(jnp.int32, (A_HDIM, A_HDIM), 1)
    eye = r == c
    sl = {h: slice(h * A_HDIM, (h + 1) * A_HDIM) for h in heads}
    qh = {h: q_ref[rb, sl[h]] for h in heads}
    kh = {h: k_ref[rb, sl[h]] for h in heads}
    vh = {h: v_ref[rb, sl[h]] for h in heads}
    c_old = {h: c_ref[0, 0, h] for h in heads}
    n_old = {h: n_ref[0, 0, h:h + 1, :] for h in heads}
    li = {h: g[:, h:h + 1] for h in heads}
    lf = {h: _log_sigmoid(g[:, 4 + h:5 + h]) for h in heads}
    m_old = {h: mrow[:, h:h + 1] for h in heads}
    cq = {h: jnp.sum(c_old[h] * qh[h], axis=-1, keepdims=True) for h in heads}
    v_col = {h: _row_to_col(vh[h], eye) for h in heads}
    nq = {h: jnp.sum(n_old[h] * qh[h], axis=-1, keepdims=True) for h in heads}
    qk = {h: jnp.sum(qh[h] * kh[h], axis=-1, keepdims=True) for h in heads}
    inter = {h: lf[h] + m_old[h] for h in heads}
    m_new = {h: jnp.maximum(inter[h], li[h]) for h in heads}
    ws = {h: jnp.exp(li[h] - m_new[h]) for h in heads}
    dec = {h: jnp.exp(inter[h] - m_new[h]) for h in heads}
    sc = {h: qk[h] * ws[h] for h in heads}
    den = {h: sc[h] + dec[h] * nq[h] for h in heads}
    h_col = {h: (sc[h] * v_col[h] + dec[h] * cq[h]) / jnp.maximum(jnp.abs(den[h]), jnp.exp(-m_new[h]))
             for h in heads}
    for h in heads:
        c_out[0, 0, h] = dec[h] * c_old[h] + (ws[h] * v_col[h]) * kh[h]
        n_out[0, 0, h:h + 1, :] = dec[h] * n_old[h] + ws[h] * kh[h]
        m_out[0, h:h + 1, :] = jnp.broadcast_to(m_new[h], (1, LANES))
    for h in heads:
        hs_ref[0, :, sl[h]] = _col_to_row(h_col[h], eye)
    if with_mem:
        ym_ref[0] = _dec_mem_attention(qm_ref[0], kv_ref.at[0, 0])


def _dec_mlstm_kernel(*refs):
    _dec_mlstm_body(pl.program_id(0), range(A_HEADS), True, *refs)


def _dec_mlstm_specs(q, k, v, gates, m_in, state_c, state_n, qm3, cache_mem_kv, seq_index):
    nb = q.shape[0]

    def at(*tail, lead=()):
        return lambda *idx: lead + (seq_index(*idx),) + tail

    in_specs = [_whole(q.shape), _whole(k.shape), _whole(v.shape), _whole(gates.shape), _whole(m_in.shape),
                pl.BlockSpec((1, 1, A_HEADS, A_HDIM, A_HDIM), at(0, 0, 0, lead=(0,))),
                pl.BlockSpec((1, 1, A_HEADS, A_HDIM), at(0, 0, lead=(0,))),
                pl.BlockSpec((1, M_HEADS, M_HDIM), at(0, 0)),
                pl.BlockSpec((1, 1, N_MEM, 2, M_HEADS, M_HDIM), at(0, 0, 0, 0, lead=(0,)))]
    out_specs = [pl.BlockSpec((1, 1, A_INNER), at(0, 0)),
                 pl.BlockSpec((1, 1, A_HEADS, A_HDIM, A_HDIM), at(0, 0, 0, lead=(0,))),
                 pl.BlockSpec((1, 1, A_HEADS, A_HDIM), at(0, 0, lead=(0,))),
                 pl.BlockSpec((1, A_HEADS, LANES), at(0, 0)),
                 pl.BlockSpec((1, M_HEADS, M_HDIM), at(0, 0))]
    out_shapes = [jax.ShapeDtypeStruct((nb, 1, A_INNER), F32),
                  jax.ShapeDtypeStruct(state_c.shape, F32),
                  jax.ShapeDtypeStruct(state_n.shape, F32),
                  jax.ShapeDtypeStruct((nb, A_HEADS, LANES), F32),
                  jax.ShapeDtypeStruct((nb, M_HEADS, M_HDIM), F32)]
    return [q, k, v, gates, m_in, state_c, state_n, qm3, cache_mem_kv], in_specs, out_specs, out_shapes


def _dec_mlstm(*job):
    args, in_specs, out_specs, out_shapes = _dec_mlstm_specs(*job, seq_index=lambda b: b)
    return pl.pallas_call(
        _dec_mlstm_kernel,
        grid=(args[0].shape[0],),
        in_specs=in_specs,
        out_specs=out_specs,
        out_shape=out_shapes,
        compiler_params=pltpu.CompilerParams(dimension_semantics=("arbitrary",), vmem_limit_bytes=VMEM_LIMIT),
        name="dec_mlstm",
    )(*args)


def _dec_mid_kernel(hs_ref, opre_ref, xc_ref, zg_ref, ym_ref, zm_ref, x_ref, ghn_ref, skip_ref, wout_ref,
                    gpost_ref, gkv_ref, gpre_ref, wkv_ref, win_ref, cos_ref, sin_ref,
                    x1_ref, q_ref, k_ref, v_ref, zg1_ref, qm1_ref, zm1_ref, win0_ref, win1_ref, win2_ref):
    hh = _sigmoid(opre_ref[...]) * hs_ref[:, 0, :]
    parts = []
    for h in range(A_HEADS):
        v = hh[:, h * A_HDIM:(h + 1) * A_HDIM]
        mu = jnp.mean(v, axis=-1, keepdims=True)
        var = jnp.mean(jnp.square(v - mu), axis=-1, keepdims=True)
        parts.append((v - mu) * lax.rsqrt(var + EPS))
    y = jnp.concatenate(parts, axis=-1) * ghn_ref[...] + skip_ref[...] * xc_ref[...]
    ymix = (y * _silu(zg_ref[...])).astype(BF16)
    ym = (_heads_to_rows(ym_ref) * _silu(zm_ref[...])).astype(BF16)
    out = _dot(ymix, wout_ref[0:A_INNER, :]) + _dot(ym, wout_ref[A_INNER:A_INNER + M_WIDTH, :])
    x1 = x_ref[:, 0, :] + _rms_scale(out) * gpost_ref[...]
    x1_ref[...] = x1
    xn = _rms_scale(x1)
    hk = (xn * gkv_ref[...]).astype(BF16)
    hq = (xn * gpre_ref[...]).astype(BF16)
    cos = cos_ref[...]
    sin = sin_ref[...]
    ks, vs = [], []
    for g in range(N_GROUPS):
        ks.append(_rope_cols(_dot(hk, wkv_ref[:, g * 2 * B_WIDTH:g * 2 * B_WIDTH + B_WIDTH]), cos, sin))
        vs.append(_dot(hk, wkv_ref[:, g * 2 * B_WIDTH + B_WIDTH:(g + 1) * 2 * B_WIDTH]))
    k_ref[...] = _rows_to_heads(jnp.concatenate(ks, axis=-1))
    v_ref[...] = _rows_to_heads(jnp.concatenate(vs, axis=-1))
    for g, wref in enumerate((win0_ref, win1_ref, win2_ref)):
        wref[...] = _rows_to_kv_heads(ks[g], vs[g])
    qoff = N_GROUPS * B_WIDTH
    q_ref[...] = _rows_to_heads(_rope_cols(_dot(hq, win_ref[:, 0:qoff]), cos, sin))
    zg1_ref[...] = _dot(hq, win_ref[:, qoff:qoff + B_WIDTH])
    qm1_ref[...] = _rows_to_heads(_dot(hq, win_ref[:, qoff + B_WIDTH:qoff + B_WIDTH + M_WIDTH]))
    zm1_ref[...] = _dot(hq, win_ref[:, qoff + B_WIDTH + M_WIDTH:qoff + B_WIDTH + 2 * M_WIDTH])


def _dec_mid(hs, opre, xc, zg, ym, zm, x, ghn, skip, w_out, g_post, g_kv, g_pre, wkv, win, cos, sin):
    nb = x.shape[0]
    args = (hs, opre, xc, zg, ym, zm, x, ghn, skip, w_out, g_post, g_kv, g_pre, wkv, win, cos, sin)
    f = lambda *s: jax.ShapeDtypeStruct(s, F32)
    gh = N_GROUPS * B_HEADS
    out_shape = [f(nb, D_MODEL), f(nb, gh, B_HDIM), f(nb, gh, B_HDIM), f(nb, gh, B_HDIM),
                 f(nb, B_WIDTH), f(nb, M_HEADS, M_HDIM), f(nb, M_WIDTH)] + [f(nb, 2 * B_HEADS, B_HDIM)] * N_GROUPS
    return pl.pallas_call(
        _dec_mid_kernel,
        grid=(1,),
        in_specs=[_whole(a.shape) for a in args],
        out_specs=[_whole(o.shape) for o in out_shape],
        out_shape=out_shape,
        compiler_params=pltpu.CompilerParams(dimension_semantics=("arbitrary",), vmem_limit_bytes=VMEM_LIMIT),
        name="dec_mid",
    )(*args)


def _dec_attn_kernel(q_ref, kn_ref, vn_ref, w0_ref, w1_ref, w2_ref, qm_ref, kv_ref, ydil_ref, ym_ref):
    w_refs = (w0_ref, w1_ref, w2_ref)
    scale = B_HDIM ** -0.5
    groups = range(N_GROUPS)
    hsl = [slice(g * B_HEADS, (g + 1) * B_HEADS) for g in groups]
    q = [q_ref[0, hsl[g], :] * scale for g in groups]
    s_c = [jnp.sum(w_refs[g][0, :, 0] * q[g][None], axis=-1, keepdims=True) for g in groups]
    s_n = [jnp.sum(kn_ref[0, hsl[g], :] * q[g], axis=-1, keepdims=True) for g in groups]
    mxs = [jnp.maximum(jnp.max(s_c[g], axis=0), s_n[g]) for g in groups]
    p_c = [jnp.exp(s_c[g] - mxs[g][None]) for g in groups]
    p_n = [jnp.exp(s_n[g] - mxs[g]) for g in groups]
    ls = [jnp.sum(p_c[g], axis=0) + p_n[g] for g in groups]
    outs = [(jnp.sum(p_c[g] * w_refs[g][0, :, 1], axis=0) + p_n[g] * vn_ref[0, hsl[g], :]) / ls[g]
            for g in groups]
    lses = [mxs[g] + jnp.log(ls[g]) for g in groups]
    mx = jnp.maximum(jnp.maximum(lses[0], lses[1]), lses[2])
    es = [jnp.exp(l - mx) for l in lses]
    tot = es[0] + es[1] + es[2]
    ydil_ref[0] = (es[0] / tot) * outs[0] + (es[1] / tot) * outs[1] + (es[2] / tot) * outs[2]
    ym_ref[0] = _dec_mem_attention(qm_ref[0], kv_ref.at[0, 0])


def _dec_attn_specs(q4, kn4, vn4, cw0, cw1, cw2, qm3, cache_mem_kv, layer, seq_index):
    nb = q4.shape[0]
    rows = B_GROUPS[0][0]

    def at(*tail):
        return lambda *idx: (seq_index(*idx),) + tail

    win_specs = [pl.BlockSpec((1, rows, 2, B_HEADS, B_HDIM), at(0, 0, 0, 0)),
                 pl.BlockSpec((1, rows, None, 2, B_HEADS, B_HDIM), at(0, 0, 0, 0, 0)),
                 pl.BlockSpec((1, rows, None, 2, B_HEADS, B_HDIM), at(0, 0, 0, 0, 0))]
    in_specs = [pl.BlockSpec((1, N_GROUPS * B_HEADS, B_HDIM), at(0, 0))] * 3 + win_specs + [
        pl.BlockSpec((1, M_HEADS, M_HDIM), at(0, 0)),
        pl.BlockSpec((1, 1, N_MEM, 2, M_HEADS, M_HDIM), lambda *idx: (layer, seq_index(*idx), 0, 0, 0, 0))]
    out_specs = [pl.BlockSpec((1, B_HEADS, B_HDIM), at(0, 0)), pl.BlockSpec((1, M_HEADS, M_HDIM), at(0, 0))]
    out_shapes = [jax.ShapeDtypeStruct((nb, B_HEADS, B_HDIM), F32),
                  jax.ShapeDtypeStruct((nb, M_HEADS, M_HDIM), F32)]
    return [q4, kn4, vn4, cw0, cw1, cw2, qm3, cache_mem_kv], in_specs, out_specs, out_shapes


def _dec_attn(q4, kn4, vn4, cw0, cw1, cw2, qm3, cache_mem_kv, layer):
    args, in_specs, out_specs, out_shapes = _dec_attn_specs(
        q4, kn4, vn4, cw0, cw1, cw2, qm3, cache_mem_kv, layer, seq_index=lambda b: b)
    return pl.pallas_call(
        _dec_attn_kernel,
        grid=(q4.shape[0],),
        in_specs=in_specs,
        out_specs=out_specs,
        out_shape=out_shapes,
        compiler_params=pltpu.CompilerParams(dimension_semantics=("arbitrary",), vmem_limit_bytes=VMEM_LIMIT),
        name="dec_attn",
    )(*args)


def _dec_out_kernel(ydil_ref, zg_ref, ym_ref, zm_ref, x_ref, wout_ref, gpost_ref, y_ref):
    ymix = (_heads_to_rows(ydil_ref) * _silu(zg_ref[...])).astype(BF16)
    ym = (_heads_to_rows(ym_ref) * _silu(zm_ref[...])).astype(BF16)
    out = _dot(ymix, wout_ref[0:B_WIDTH, :]) + _dot(ym, wout_ref[B_WIDTH:B_WIDTH + M_WIDTH, :])
    y_ref[:, 0, :] = x_ref[...] + _rms_scale(out) * gpost_ref[...]


def _dec_out(ydil, zg, ym, zm, x1, w_out, g_post):
    args = (ydil, zg, ym, zm, x1, w_out, g_post)
    out_shape = (x1.shape[0], 1, x1.shape[1])
    return pl.pallas_call(
        _dec_out_kernel,
        grid=(1,),
        in_specs=[_whole(a.shape) for a in args],
        out_specs=_whole(out_shape),
        out_shape=jax.ShapeDtypeStruct(out_shape, F32),
        compiler_params=pltpu.CompilerParams(dimension_semantics=("arbitrary",), vmem_limit_bytes=VMEM_LIMIT),
        name="dec_out",
    )(*args)


class _SampleGroup:
    def __init__(self, x_sample, state_conv, state_c, state_n, state_m, cache_wins, cache_mem_kv, p):
        self.p = p
        self.nb = nb = x_sample.shape[0]
        self.cache_wins = cache_wins
        self.cache_mem_kv = cache_mem_kv
        self.x = x_sample
        q, k, v, gates, self.xc, self.opre, self.zg, qm3, self.zm, self.conv_s = _dec_l0_proj(
            x_sample, p['g_pre'][0:1], p['w_in_a'][0], state_conv, p['conv_w_a'][0], p['conv_b_a'],
            p['w_q_a'][0], p['w_k_a'][0], p['w_v_a'][0], p['w_if_a'], p['b_if_a'])
        self._mlstm_job = (q, k, v, gates, state_m, state_c, state_n, qm3, cache_mem_kv)
        self.attn_job = None

    def mlstm_job(self):
        return self._mlstm_job

    def after_mlstm(self, res):
        p, nb = self.p, self.nb
        hs, self.c_s, self.n_s, m_rows, ym0 = res
        self.m_s = m_rows[:, :, 0][None]
        cos, sin = _rope_tables(PAST_LEN + jnp.arange(1, dtype=F32))
        self.x1, qd, kn, vn, self.zg1, qm1, self.zm1, w0, w1, w2 = _dec_mid(
            hs, self.opre, self.xc, self.zg, ym0, self.zm, self.x,
            p['g_hn_a'], p['skip_a'], p['w_out_a'][0], p['g_post'][0:1], p['g_kv'], p['g_pre'][1:2],
            p['w_kv_b'], p['w_in_b'][0], cos, sin)
        cws = [self.cache_wins[0]]
        for g in (1, 2):
            w, d = B_GROUPS[g]
            cws.append(self.cache_wins[g].reshape(nb, w // d, d, 2, B_HEADS, B_HDIM))
        self.attn_job = (qd, kn, vn, cws[0], cws[1], cws[2], qm1, self.cache_mem_kv, 1)
        self.wins_s = [w.reshape(nb, 1, 2, B_HEADS, B_HDIM) for w in (w0, w1, w2)]

    def after_attn(self, res):
        p = self.p
        ydil, ym1 = res
        self.y = _dec_out(ydil, self.zg1, ym1, self.zm1, self.x1, p['w_out_b'][0], p['g_post'][1:2])

    def outputs(self):
        return self.y, self.conv_s, self.c_s, self.n_s, self.m_s, self.wins_s


def kernel(x_prompt, x_sample, mem_prompt, state_conv, state_C, state_n, state_m, cache_win0, cache_win1,
           cache_win2, cache_mem_kv, g_pre, g_post, w_in_a, conv_w_a, conv_b_a, w_q_a, w_k_a, w_v_a, w_if_a,
           b_if_a, g_hn_a, skip_a, w_out_a, g_kv, w_kv_b, w_in_b, w_out_b, w_mkv):
    p = _prep_params(g_pre, g_post, w_in_a, conv_w_a, conv_b_a, w_q_a, w_k_a, w_v_a, w_if_a, b_if_a,
                     g_hn_a, skip_a, w_out_a, g_kv, w_kv_b, w_in_b, w_out_b, w_mkv)
    sample = _SampleGroup(x_sample, state_conv, state_C, state_n, state_m,
                          (cache_win0, cache_win1, cache_win2), cache_mem_kv, p)
    y_p, conv_p, c_p, n_p, m_p, wins_p, memkv_p = _prompt_group(x_prompt, mem_prompt, p, sample)
    y_s, conv_s, c_s, n_s, m_s, wins_s = sample.outputs()
    return (y_p, y_s, conv_p, c_p, n_p, m_p, wins_p[0], wins_p[1], wins_p[2], memkv_p,
            conv_s, c_s, n_s, m_s, wins_s[0], wins_s[1], wins_s[2])
```
